```python
import jax, jax.numpy as jnp
from jax import lax
import numpy as np

D_MODEL = 1024
BATCH = 8
SEQ = 2048
DEPTH = 4

CHUNK = 128
EPS = 1e-6
GM_HEADS = 4
GM_HEAD_DIM = 64
GM_WIDTH = GM_HEADS * GM_HEAD_DIM
HG_HEADS = 4
HG_HEAD_DIM = 64
HG_WIDTH = HG_HEADS * HG_HEAD_DIM
MLA_HEADS = 8
QK_NOPE = 64
QK_ROPE = 32
QK_DIM = QK_NOPE + QK_ROPE
V_DIM = 64
Q_LORA = 256
KV_LORA = 128
MLA_WIDTH = MLA_HEADS * V_DIM
ROPE_THETA = 10000.0
Q_BLOCK = 128
D_MIX = GM_WIDTH + HG_WIDTH + MLA_WIDTH
D_FF = 4 * D_MODEL
IN_SPLITS = (GM_WIDTH, GM_WIDTH, HG_WIDTH, HG_WIDTH, HG_WIDTH, HG_WIDTH, Q_LORA, KV_LORA, QK_ROPE)
D_IN = 2 * GM_WIDTH + 4 * HG_WIDTH + Q_LORA + KV_LORA + QK_ROPE

kernel_name = "hybrid_gmlp_hgrn2_mla_trunk"


def rms_norm(x, gain):
    xf = x.astype(jnp.float32)
    y = xf * lax.rsqrt(jnp.mean(xf * xf, axis=-1, keepdims=True) + EPS)
    return (y * gain.astype(jnp.float32)).astype(x.dtype)


def head_rms_norm(x, n_heads, gain):
    shp = x.shape
    xh = x.reshape(shp[:-1] + (n_heads, shp[-1] // n_heads))
    return rms_norm(xh, gain.reshape(n_heads, -1)).reshape(shp)


def rope(x, positions):
    half = QK_ROPE // 2
    inv_freq = ROPE_THETA ** (-jnp.arange(half, dtype=jnp.float32) / half)
    ang = positions.astype(jnp.float32)[:, :, None, None] * inv_freq
    cos, sin = jnp.cos(ang), jnp.sin(ang)
    xf = x.astype(jnp.float32)
    x1, x2 = xf[..., :half], xf[..., half:]
    return jnp.concatenate([x1 * cos - x2 * sin, x2 * cos + x1 * sin], axis=-1).astype(x.dtype)


def chunked_spatial_gating(u_raw, v_raw, v_gain, w_s, b_s, out_gain):
    B, S, _ = u_raw.shape
    nc = S // CHUNK
    u = jax.nn.gelu(u_raw)
    v = head_rms_norm(jax.nn.gelu(v_raw), GM_HEADS, v_gain)
    v = v.reshape(B, nc, CHUNK, GM_HEADS, GM_HEAD_DIM)
    causal = jnp.tril(jnp.ones((CHUNK, CHUNK), dtype=bool))
    w = jnp.where(causal, w_s, 0).astype(v.dtype)
    y = jnp.einsum('hts,bnshd->bnthd', w, v) + b_s.T[:, :, None].astype(v.dtype)
    out = u * y.reshape(B, S, GM_WIDTH)
    return head_rms_norm(out, GM_HEADS, out_gain)


def hgrn2(q_raw, f_raw, i_raw, g_raw, lower_bound, out_gain):
    B, S, _ = q_raw.shape
    nc = S // CHUNK
    f32 = jnp.float32
    q = jax.nn.silu(q_raw.astype(f32))
    lb = lower_bound.astype(f32)
    f = lb + (1.0 - lb) * jax.nn.sigmoid(f_raw.astype(f32))
    k = 1.0 - f
    log_f = jnp.log(f)

    def to_chunks(t):
        return t.reshape(B, nc, CHUNK, HG_HEADS, HG_HEAD_DIM).transpose(1, 0, 3, 2, 4)

    qc, kc, vc, lc = to_chunks(q), to_chunks(k), to_chunks(i_raw.astype(f32)), to_chunks(log_f)
    bc = jnp.cumsum(lc, axis=-2)
    causal = jnp.tril(jnp.ones((CHUNK, CHUNK), dtype=bool))[:, :, None]

    def step(state, inp):
        q_, k_, v_, b_ = inp
        inter = jnp.einsum('bhtk,bhkv->bhtv', q_ * jnp.exp(b_), state)
        diff = b_[:, :, :, None, :] - b_[:, :, None, :, :]
        decay = jnp.exp(jnp.where(causal, diff, -jnp.inf))
        scores = jnp.einsum('bhtk,bhtsk,bhsk->bhts', q_, decay, k_)
        intra = jnp.einsum('bhts,bhsv->bhtv', scores, v_)
        b_last = b_[:, :, -1:, :]
        new_state = (jnp.exp(b_last[:, :, 0, :, None]) * state
                     + jnp.einsum('bhsk,bhsv->bhkv', k_ * jnp.exp(b_last - b_), v_))
        return new_state, inter + intra

    s0 = jnp.zeros((B, HG_HEADS, HG_HEAD_DIM, HG_HEAD_DIM), f32)
    _, o = lax.scan(step, s0, (qc, kc, vc, bc))
    o = o.transpose(1, 0, 3, 2, 4).reshape(B, S, HG_WIDTH)
    o = head_rms_norm(o, HG_HEADS, out_gain) * jax.nn.silu(g_raw.astype(f32))
    return o.astype(q_raw.dtype)


def mla(cq_raw, ckv_raw, kpe_raw, positions, q_a_gain, w_uq, kv_a_gain, w_ukv,
        q_gain, k_gain, out_gain):
    B, S, _ = cq_raw.shape
    q = (rms_norm(cq_raw, q_a_gain) @ w_uq).reshape(B, S, MLA_HEADS, QK_DIM)
    kv = (rms_norm(ckv_raw, kv_a_gain) @ w_ukv).reshape(B, S, MLA_HEADS, QK_NOPE + V_DIM)
    k_nope, v = kv[..., :QK_NOPE], kv[..., QK_NOPE:]
    k_pe = jnp.broadcast_to(kpe_raw[:, :, None, :], (B, S, MLA_HEADS, QK_ROPE))
    k = jnp.concatenate([k_nope, k_pe], axis=-1)
    q = rms_norm(q, q_gain)
    k = rms_norm(k, k_gain)
    q = jnp.concatenate([q[..., :QK_NOPE], rope(q[..., QK_NOPE:], positions)], axis=-1)
    k = jnp.concatenate([k[..., :QK_NOPE], rope(k[..., QK_NOPE:], positions)], axis=-1)
    q = q.transpose(0, 2, 1, 3)
    k = k.transpose(0, 2, 1, 3)
    v = v.transpose(0, 2, 1, 3)
    scale = QK_DIM ** -0.5
    outs = []
    for blk in range(S // Q_BLOCK):
        lo, hi = blk * Q_BLOCK, (blk + 1) * Q_BLOCK
        s = jnp.einsum('bhqd,bhkd->bhqk', q[:, :, lo:hi], k[:, :, :hi]).astype(jnp.float32) * scale
        mask = (lo + jnp.arange(Q_BLOCK))[:, None] >= jnp.arange(hi)[None, :]
        p = jax.nn.softmax(jnp.where(mask, s, -jnp.inf), axis=-1).astype(v.dtype)
        outs.append(jnp.einsum('bhqk,bhkd->bhqd', p, v[:, :, :hi]))
    o = jnp.concatenate(outs, axis=2).transpose(0, 2, 1, 3).reshape(B, S, MLA_WIDTH)
    return head_rms_norm(o, MLA_HEADS, out_gain)


def _fwd_setup_inputs(seed: int = 0) -> dict:
    key = jax.random.key(seed)
    ks = jax.random.split(key, 24)
    f32 = jnp.float32

    def nrm(k, shape, scale):
        return jax.random.normal(k, shape, f32) * scale

    def gain(k, shape):
        return 1.0 + 0.02 * jax.random.normal(k, shape, f32)

    x = jax.random.normal(ks[0], (BATCH, SEQ, D_MODEL), f32)
    offsets = jax.random.randint(ks[1], (BATCH, 1), 0, 1024, dtype=jnp.int32)
    positions = offsets + jnp.arange(SEQ, dtype=jnp.int32)[None, :]
    return {
        "x": x,
        "positions": positions,
        "norm1_gain": gain(ks[2], (DEPTH, D_MODEL)),
        "w_in": nrm(ks[3], (DEPTH, D_MODEL, D_IN), D_MODEL ** -0.5),
        "gm_v_gain": gain(ks[4], (DEPTH, GM_WIDTH)),
        "gm_w_s": nrm(ks[5], (DEPTH, GM_HEADS, CHUNK, CHUNK), CHUNK ** -0.5),
        "gm_b_s": gain(ks[6], (DEPTH, GM_HEADS, CHUNK)),
        "gm_out_gain": gain(ks[7], (DEPTH, GM_WIDTH)),
        "hg_lower_bound": nrm(ks[8], (DEPTH, HG_WIDTH), 0.1),
        "hg_out_gain": gain(ks[9], (DEPTH, HG_WIDTH)),
        "mla_q_a_gain": gain(ks[10], (DEPTH, Q_LORA)),
        "mla_w_uq": nrm(ks[11], (DEPTH, Q_LORA, MLA_HEADS * QK_DIM), Q_LORA ** -0.5),
        "mla_kv_a_gain": gain(ks[12], (DEPTH, KV_LORA)),
        "mla_w_ukv": nrm(ks[13], (DEPTH, KV_LORA, MLA_HEADS * (QK_NOPE + V_DIM)), KV_LORA ** -0.5),
        "mla_q_gain": gain(ks[14], (DEPTH, QK_DIM)),
        "mla_k_gain": gain(ks[15], (DEPTH, QK_DIM)),
        "mla_out_gain": gain(ks[16], (DEPTH, MLA_WIDTH)),
        "w_out": nrm(ks[17], (DEPTH, D_MIX, D_MODEL), (2 * D_MIX) ** -0.5),
        "norm2_gain": gain(ks[18], (DEPTH, D_MODEL)),
        "w_ff1": nrm(ks[19], (DEPTH, D_MODEL, D_FF), D_MODEL ** -0.5),
        "w_ff2": nrm(ks[20], (DEPTH, D_FF, D_MODEL), (2 * D_FF) ** -0.5),
    }


def _fwd_reference(x, positions, norm1_gain, w_in, gm_v_gain, gm_w_s, gm_b_s, gm_out_gain,
              hg_lower_bound, hg_out_gain, mla_q_a_gain, mla_w_uq, mla_kv_a_gain, mla_w_ukv,
              mla_q_gain, mla_k_gain, mla_out_gain, w_out, norm2_gain, w_ff1, w_ff2):
    lb_soft = jax.nn.softmax(hg_lower_bound.astype(jnp.float32), axis=0)
    lower_bounds = jnp.cumsum(lb_soft, axis=0) - lb_soft[0]
    split_points = [int(s) for s in np.cumsum(IN_SPLITS)[:-1]]
    for l in range(DEPTH):
        h = rms_norm(x, norm1_gain[l])
        proj = h @ w_in[l]
        a_u, a_v, b_q, b_f, b_i, b_g, c_q, c_kv, c_pe = jnp.split(proj, split_points, axis=-1)
        y_a = chunked_spatial_gating(a_u, a_v, gm_v_gain[l], gm_w_s[l], gm_b_s[l], gm_out_gain[l])
        y_b = hgrn2(b_q, b_f, b_i, b_g, lower_bounds[l], hg_out_gain[l])
        y_c = mla(c_q, c_kv, c_pe, positions, mla_q_a_gain[l], mla_w_uq[l], mla_kv_a_gain[l],
                  mla_w_ukv[l], mla_q_gain[l], mla_k_gain[l], mla_out_gain[l])
        mix = jnp.concatenate([y_a, y_b, y_c], axis=-1)
        x = x + mix @ w_out[l]
        h2 = rms_norm(x, norm2_gain[l])
        x = x + jnp.square(jax.nn.relu(h2 @ w_ff1[l])) @ w_ff2[l]
    return x


import jax as _jax
import jax.numpy as _jnp

TWIN_FORMAT = 'train_step'
FWD_PARAMS = ['x', 'positions', 'norm1_gain', 'w_in', 'gm_v_gain', 'gm_w_s', 'gm_b_s', 'gm_out_gain', 'hg_lower_bound', 'hg_out_gain', 'mla_q_a_gain', 'mla_w_uq', 'mla_kv_a_gain', 'mla_w_ukv', 'mla_q_gain', 'mla_k_gain', 'mla_out_gain', 'w_out', 'norm2_gain', 'w_ff1', 'w_ff2']
TWIN_WEIGHTS = ['norm1_gain', 'w_in', 'gm_v_gain', 'gm_w_s', 'gm_b_s', 'gm_out_gain', 'hg_lower_bound', 'hg_out_gain', 'mla_q_a_gain', 'mla_w_uq', 'mla_kv_a_gain', 'mla_w_ukv', 'mla_q_gain', 'mla_k_gain', 'mla_out_gain', 'w_out', 'norm2_gain', 'w_ff1', 'w_ff2']
TWIN_DIFF_INPUT = 'x'
TWIN_INPUTS = ['x', 'positions', 'norm1_gain', 'w_in', 'gm_v_gain', 'gm_w_s', 'gm_b_s', 'gm_out_gain', 'hg_lower_bound', 'hg_out_gain', 'mla_q_a_gain', 'mla_w_uq', 'mla_kv_a_gain', 'mla_w_ukv', 'mla_q_gain', 'mla_k_gain', 'mla_out_gain', 'w_out', 'norm2_gain', 'w_ff1', 'w_ff2', 'loss_target', 'm_norm1_gain', 'm_w_in', 'm_gm_v_gain', 'm_gm_w_s', 'm_gm_b_s', 'm_gm_out_gain', 'm_hg_lower_bound', 'm_hg_out_gain', 'm_mla_q_a_gain', 'm_mla_w_uq', 'm_mla_kv_a_gain', 'm_mla_w_ukv', 'm_mla_q_gain', 'm_mla_k_gain', 'm_mla_out_gain', 'm_w_out', 'm_norm2_gain', 'm_w_ff1', 'm_w_ff2', 'v_norm1_gain', 'v_w_in', 'v_gm_v_gain', 'v_gm_w_s', 'v_gm_b_s', 'v_gm_out_gain', 'v_hg_lower_bound', 'v_hg_out_gain', 'v_mla_q_a_gain', 'v_mla_w_uq', 'v_mla_kv_a_gain', 'v_mla_w_ukv', 'v_mla_q_gain', 'v_mla_k_gain', 'v_mla_out_gain', 'v_w_out', 'v_norm2_gain', 'v_w_ff1', 'v_w_ff2']
TWIN_OUTPUTS = ['loss', 'grad_x', 'grad_norm1_gain', 'grad_w_in', 'grad_gm_v_gain', 'grad_gm_w_s', 'grad_gm_b_s', 'grad_gm_out_gain', 'grad_hg_lower_bound', 'grad_hg_out_gain', 'grad_mla_q_a_gain', 'grad_mla_w_uq', 'grad_mla_kv_a_gain', 'grad_mla_w_ukv', 'grad_mla_q_gain', 'grad_mla_k_gain', 'grad_mla_out_gain', 'grad_w_out', 'grad_norm2_gain', 'grad_w_ff1', 'grad_w_ff2', 'delta_norm1_gain', 'delta_w_in', 'delta_gm_v_gain', 'delta_gm_w_s', 'delta_gm_b_s', 'delta_gm_out_gain', 'delta_hg_lower_bound', 'delta_hg_out_gain', 'delta_mla_q_a_gain', 'delta_mla_w_uq', 'delta_mla_kv_a_gain', 'delta_mla_w_ukv', 'delta_mla_q_gain', 'delta_mla_k_gain', 'delta_mla_out_gain', 'delta_w_out', 'delta_norm2_gain', 'delta_w_ff1', 'delta_w_ff2', 'new_m_norm1_gain', 'new_m_w_in', 'new_m_gm_v_gain', 'new_m_gm_w_s', 'new_m_gm_b_s', 'new_m_gm_out_gain', 'new_m_hg_lower_bound', 'new_m_hg_out_gain', 'new_m_mla_q_a_gain', 'new_m_mla_w_uq', 'new_m_mla_kv_a_gain', 'new_m_mla_w_ukv', 'new_m_mla_q_gain', 'new_m_mla_k_gain', 'new_m_mla_out_gain', 'new_m_w_out', 'new_m_norm2_gain', 'new_m_w_ff1', 'new_m_w_ff2', 'new_v_norm1_gain', 'new_v_w_in', 'new_v_gm_v_gain', 'new_v_gm_w_s', 'new_v_gm_b_s', 'new_v_gm_out_gain', 'new_v_hg_lower_bound', 'new_v_hg_out_gain', 'new_v_mla_q_a_gain', 'new_v_mla_w_uq', 'new_v_mla_kv_a_gain', 'new_v_mla_w_ukv', 'new_v_mla_q_gain', 'new_v_mla_k_gain', 'new_v_mla_out_gain', 'new_v_w_out', 'new_v_norm2_gain', 'new_v_w_ff1', 'new_v_w_ff2']
TWIN_LEAF_KINDS = {'loss': 'loss', 'grad_x': 'grad_x', 'grad_norm1_gain': 'grad_w', 'grad_w_in': 'grad_w', 'grad_gm_v_gain': 'grad_w', 'grad_gm_w_s': 'grad_w', 'grad_gm_b_s': 'grad_w', 'grad_gm_out_gain': 'grad_w', 'grad_hg_lower_bound': 'grad_w', 'grad_hg_out_gain': 'grad_w', 'grad_mla_q_a_gain': 'grad_w', 'grad_mla_w_uq': 'grad_w', 'grad_mla_kv_a_gain': 'grad_w', 'grad_mla_w_ukv': 'grad_w', 'grad_mla_q_gain': 'grad_w', 'grad_mla_k_gain': 'grad_w', 'grad_mla_out_gain': 'grad_w', 'grad_w_out': 'grad_w', 'grad_norm2_gain': 'grad_w', 'grad_w_ff1': 'grad_w', 'grad_w_ff2': 'grad_w', 'delta_norm1_gain': 'delta_w', 'delta_w_in': 'delta_w', 'delta_gm_v_gain': 'delta_w', 'delta_gm_w_s': 'delta_w', 'delta_gm_b_s': 'delta_w', 'delta_gm_out_gain': 'delta_w', 'delta_hg_lower_bound': 'delta_w', 'delta_hg_out_gain': 'delta_w', 'delta_mla_q_a_gain': 'delta_w', 'delta_mla_w_uq': 'delta_w', 'delta_mla_kv_a_gain': 'delta_w', 'delta_mla_w_ukv': 'delta_w', 'delta_mla_q_gain': 'delta_w', 'delta_mla_k_gain': 'delta_w', 'delta_mla_out_gain': 'delta_w', 'delta_w_out': 'delta_w', 'delta_norm2_gain': 'delta_w', 'delta_w_ff1': 'delta_w', 'delta_w_ff2': 'delta_w', 'new_m_norm1_gain': 'new_m', 'new_m_w_in': 'new_m', 'new_m_gm_v_gain': 'new_m', 'new_m_gm_w_s': 'new_m', 'new_m_gm_b_s': 'new_m', 'new_m_gm_out_gain': 'new_m', 'new_m_hg_lower_bound': 'new_m', 'new_m_hg_out_gain': 'new_m', 'new_m_mla_q_a_gain': 'new_m', 'new_m_mla_w_uq': 'new_m', 'new_m_mla_kv_a_gain': 'new_m', 'new_m_mla_w_ukv': 'new_m', 'new_m_mla_q_gain': 'new_m', 'new_m_mla_k_gain': 'new_m', 'new_m_mla_out_gain': 'new_m', 'new_m_w_out': 'new_m', 'new_m_norm2_gain': 'new_m', 'new_m_w_ff1': 'new_m', 'new_m_w_ff2': 'new_m', 'new_v_norm1_gain': 'new_v', 'new_v_w_in': 'new_v', 'new_v_gm_v_gain': 'new_v', 'new_v_gm_w_s': 'new_v', 'new_v_gm_b_s': 'new_v', 'new_v_gm_out_gain': 'new_v', 'new_v_hg_lower_bound': 'new_v', 'new_v_hg_out_gain': 'new_v', 'new_v_mla_q_a_gain': 'new_v', 'new_v_mla_w_uq': 'new_v', 'new_v_mla_kv_a_gain': 'new_v', 'new_v_mla_w_ukv': 'new_v', 'new_v_mla_q_gain': 'new_v', 'new_v_mla_k_gain': 'new_v', 'new_v_mla_out_gain': 'new_v', 'new_v_w_out': 'new_v', 'new_v_norm2_gain': 'new_v', 'new_v_w_ff1': 'new_v', 'new_v_w_ff2': 'new_v'}


def _forward(args):
    return _fwd_reference(*[args[k] for k in FWD_PARAMS])


def _output_shape():
    out = _jax.eval_shape(lambda: _forward(_fwd_setup_inputs(0)))
    return out.shape, out.dtype

N_MICROBATCH = 1
ADAM_LR = 0.001
ADAM_B1 = 0.9
ADAM_B2 = 0.999
ADAM_EPS = 1e-08
ADAM_WD = 0.01
ADAM_STEP = 10
PER_EXAMPLE_BATCH_AXIS = {'x': 0, 'positions': 0, 'loss_target': 0}
SHARED_INPUTS = []
_WEIGHT_DTYPES = {'norm1_gain': _jnp.float32, 'w_in': _jnp.float32, 'gm_v_gain': _jnp.float32, 'gm_w_s': _jnp.float32, 'gm_b_s': _jnp.float32, 'gm_out_gain': _jnp.float32, 'hg_lower_bound': _jnp.float32, 'hg_out_gain': _jnp.float32, 'mla_q_a_gain': _jnp.float32, 'mla_w_uq': _jnp.float32, 'mla_kv_a_gain': _jnp.float32, 'mla_w_ukv': _jnp.float32, 'mla_q_gain': _jnp.float32, 'mla_k_gain': _jnp.float32, 'mla_out_gain': _jnp.float32, 'w_out': _jnp.float32, 'norm2_gain': _jnp.float32, 'w_ff1': _jnp.float32, 'w_ff2': _jnp.float32}
MOMENT_SCALE = {'norm1_gain': 9.718659e+00, 'w_in': 6.985527e+00, 'gm_v_gain': 9.815182e-01, 'gm_w_s': 2.153350e-01, 'gm_b_s': 2.259501e-01, 'gm_out_gain': 1.237589e+01, 'hg_lower_bound': 4.217350e-02, 'hg_out_gain': 4.829867e+00, 'mla_q_a_gain': 1.554763e+00, 'mla_w_uq': 8.178722e-01, 'mla_kv_a_gain': 2.592058e+01, 'mla_w_ukv': 9.175811e+00, 'mla_q_gain': 2.683695e+00, 'mla_k_gain': 2.683436e+00, 'mla_out_gain': 1.686228e+01, 'w_out': 1.426860e+01, 'norm2_gain': 2.527771e+01, 'w_ff1': 3.342295e+00, 'w_ff2': 1.587472e+01}


def _to_microbatches(a, axis):
    t = _jnp.moveaxis(a, axis, 0)
    t = t.reshape((N_MICROBATCH, t.shape[0] // N_MICROBATCH) + t.shape[1:])
    return _jnp.moveaxis(t, 1, axis + 1)


def setup_inputs(seed: int = 0) -> dict:
    inp = _fwd_setup_inputs(seed)
    key = _jax.random.fold_in(_jax.random.key(seed), 7919)
    shape, _ = _output_shape()
    out = dict(inp)
    out["loss_target"] = _jax.random.normal(_jax.random.fold_in(key, 0), shape, _jnp.float32)
    for i, name in enumerate(TWIN_WEIGHTS):
        w = inp[name].astype(_jnp.float32)
        if MOMENT_SCALE is None:
            s = _jnp.sqrt(_jnp.mean(_jnp.square(w)) + 1e-30)
        else:
            s = MOMENT_SCALE[name]
        km, kv = _jax.random.split(_jax.random.fold_in(key, i + 1))
        out[name] = w
        out["m_" + name] = s * _jax.random.normal(km, w.shape, _jnp.float32)
        out["v_" + name] = (s * s) * _jax.random.uniform(kv, w.shape, _jnp.float32, 0.5, 1.5)
    if N_MICROBATCH > 1:
        for name, axis in PER_EXAMPLE_BATCH_AXIS.items():
            out[name] = _to_microbatches(out[name], axis)
    return {'x': out['x'], 'positions': out['positions'], 'norm1_gain': out['norm1_gain'], 'w_in': out['w_in'], 'gm_v_gain': out['gm_v_gain'], 'gm_w_s': out['gm_w_s'], 'gm_b_s': out['gm_b_s'], 'gm_out_gain': out['gm_out_gain'], 'hg_lower_bound': out['hg_lower_bound'], 'hg_out_gain': out['hg_out_gain'], 'mla_q_a_gain': out['mla_q_a_gain'], 'mla_w_uq': out['mla_w_uq'], 'mla_kv_a_gain': out['mla_kv_a_gain'], 'mla_w_ukv': out['mla_w_ukv'], 'mla_q_gain': out['mla_q_gain'], 'mla_k_gain': out['mla_k_gain'], 'mla_out_gain': out['mla_out_gain'], 'w_out': out['w_out'], 'norm2_gain': out['norm2_gain'], 'w_ff1': out['w_ff1'], 'w_ff2': out['w_ff2'], 'loss_target': out['loss_target'], 'm_norm1_gain': out['m_norm1_gain'], 'm_w_in': out['m_w_in'], 'm_gm_v_gain': out['m_gm_v_gain'], 'm_gm_w_s': out['m_gm_w_s'], 'm_gm_b_s': out['m_gm_b_s'], 'm_gm_out_gain': out['m_gm_out_gain'], 'm_hg_lower_bound': out['m_hg_lower_bound'], 'm_hg_out_gain': out['m_hg_out_gain'], 'm_mla_q_a_gain': out['m_mla_q_a_gain'], 'm_mla_w_uq': out['m_mla_w_uq'], 'm_mla_kv_a_gain': out['m_mla_kv_a_gain'], 'm_mla_w_ukv': out['m_mla_w_ukv'], 'm_mla_q_gain': out['m_mla_q_gain'], 'm_mla_k_gain': out['m_mla_k_gain'], 'm_mla_out_gain': out['m_mla_out_gain'], 'm_w_out': out['m_w_out'], 'm_norm2_gain': out['m_norm2_gain'], 'm_w_ff1': out['m_w_ff1'], 'm_w_ff2': out['m_w_ff2'], 'v_norm1_gain': out['v_norm1_gain'], 'v_w_in': out['v_w_in'], 'v_gm_v_gain': out['v_gm_v_gain'], 'v_gm_w_s': out['v_gm_w_s'], 'v_gm_b_s': out['v_gm_b_s'], 'v_gm_out_gain': out['v_gm_out_gain'], 'v_hg_lower_bound': out['v_hg_lower_bound'], 'v_hg_out_gain': out['v_hg_out_gain'], 'v_mla_q_a_gain': out['v_mla_q_a_gain'], 'v_mla_w_uq': out['v_mla_w_uq'], 'v_mla_kv_a_gain': out['v_mla_kv_a_gain'], 'v_mla_w_ukv': out['v_mla_w_ukv'], 'v_mla_q_gain': out['v_mla_q_gain'], 'v_mla_k_gain': out['v_mla_k_gain'], 'v_mla_out_gain': out['v_mla_out_gain'], 'v_w_out': out['v_w_out'], 'v_norm2_gain': out['v_norm2_gain'], 'v_w_ff1': out['v_w_ff1'], 'v_w_ff2': out['v_w_ff2']}


def _loss(weights, diff, rest, loss_target):
    with _jax.named_scope("forward"):
        args = {**rest, TWIN_DIFF_INPUT: diff, **{k: w.astype(_WEIGHT_DTYPES[k]) for k, w in weights.items()}}
        y = _forward(args)
    with _jax.named_scope("loss_head"):
        err = _jnp.square(y.astype(_jnp.float32) - loss_target)
        return 0.5 * _jnp.sum(_jnp.mean(err, axis=-1)) if err.ndim else 0.5 * err


def _adamw(w, g, m, v):
    m = ADAM_B1 * m + (1.0 - ADAM_B1) * g
    v = ADAM_B2 * v + (1.0 - ADAM_B2) * _jnp.square(g)
    m_hat = m / (1.0 - ADAM_B1 ** ADAM_STEP)
    v_hat = v / (1.0 - ADAM_B2 ** ADAM_STEP)
    delta = -ADAM_LR * (m_hat / (_jnp.sqrt(v_hat) + ADAM_EPS) + ADAM_WD * w)
    return delta, m, v


def reference(x, positions, norm1_gain, w_in, gm_v_gain, gm_w_s, gm_b_s, gm_out_gain, hg_lower_bound, hg_out_gain, mla_q_a_gain, mla_w_uq, mla_kv_a_gain, mla_w_ukv, mla_q_gain, mla_k_gain, mla_out_gain, w_out, norm2_gain, w_ff1, w_ff2, loss_target, m_norm1_gain, m_w_in, m_gm_v_gain, m_gm_w_s, m_gm_b_s, m_gm_out_gain, m_hg_lower_bound, m_hg_out_gain, m_mla_q_a_gain, m_mla_w_uq, m_mla_kv_a_gain, m_mla_w_ukv, m_mla_q_gain, m_mla_k_gain, m_mla_out_gain, m_w_out, m_norm2_gain, m_w_ff1, m_w_ff2, v_norm1_gain, v_w_in, v_gm_v_gain, v_gm_w_s, v_gm_b_s, v_gm_out_gain, v_hg_lower_bound, v_hg_out_gain, v_mla_q_a_gain, v_mla_w_uq, v_mla_kv_a_gain, v_mla_w_ukv, v_mla_q_gain, v_mla_k_gain, v_mla_out_gain, v_w_out, v_norm2_gain, v_w_ff1, v_w_ff2):
    given = dict(x=x, positions=positions, norm1_gain=norm1_gain, w_in=w_in, gm_v_gain=gm_v_gain, gm_w_s=gm_w_s, gm_b_s=gm_b_s, gm_out_gain=gm_out_gain, hg_lower_bound=hg_lower_bound, hg_out_gain=hg_out_gain, mla_q_a_gain=mla_q_a_gain, mla_w_uq=mla_w_uq, mla_kv_a_gain=mla_kv_a_gain, mla_w_ukv=mla_w_ukv, mla_q_gain=mla_q_gain, mla_k_gain=mla_k_gain, mla_out_gain=mla_out_gain, w_out=w_out, norm2_gain=norm2_gain, w_ff1=w_ff1, w_ff2=w_ff2, loss_target=loss_target, m_norm1_gain=m_norm1_gain, m_w_in=m_w_in, m_gm_v_gain=m_gm_v_gain, m_gm_w_s=m_gm_w_s, m_gm_b_s=m_gm_b_s, m_gm_out_gain=m_gm_out_gain, m_hg_lower_bound=m_hg_lower_bound, m_hg_out_gain=m_hg_out_gain, m_mla_q_a_gain=m_mla_q_a_gain, m_mla_w_uq=m_mla_w_uq, m_mla_kv_a_gain=m_mla_kv_a_gain, m_mla_w_ukv=m_mla_w_ukv, m_mla_q_gain=m_mla_q_gain, m_mla_k_gain=m_mla_k_gain, m_mla_out_gain=m_mla_out_gain, m_w_out=m_w_out, m_norm2_gain=m_norm2_gain, m_w_ff1=m_w_ff1, m_w_ff2=m_w_ff2, v_norm1_gain=v_norm1_gain, v_w_in=v_w_in, v_gm_v_gain=v_gm_v_gain, v_gm_w_s=v_gm_w_s, v_gm_b_s=v_gm_b_s, v_gm_out_gain=v_gm_out_gain, v_hg_lower_bound=v_hg_lower_bound, v_hg_out_gain=v_hg_out_gain, v_mla_q_a_gain=v_mla_q_a_gain, v_mla_w_uq=v_mla_w_uq, v_mla_kv_a_gain=v_mla_kv_a_gain, v_mla_w_ukv=v_mla_w_ukv, v_mla_q_gain=v_mla_q_gain, v_mla_k_gain=v_mla_k_gain, v_mla_out_gain=v_mla_out_gain, v_w_out=v_w_out, v_norm2_gain=v_norm2_gain, v_w_ff1=v_w_ff1, v_w_ff2=v_w_ff2)
    weights = {n: given[n] for n in TWIN_WEIGHTS}
    shared = {n: given[n] for n in SHARED_INPUTS}
    per_example = {n: given[n] for n in ['x', 'positions']}
    grad_fn = _jax.value_and_grad(_loss, argnums=(0, 1))

    def one_microbatch(ex, loss_target):
        ex = dict(ex)
        diff = ex.pop(TWIN_DIFF_INPUT)
        return grad_fn(weights, diff, {**shared, **ex}, loss_target)

    if N_MICROBATCH == 1:
        loss, (grad_w, grad_x) = one_microbatch(per_example, given["loss_target"])
    else:
        def body(carry, xs):
            loss_sum, grad_sum = carry
            l_k, (gw_k, gx_k) = one_microbatch(xs[0], xs[1])
            with _jax.named_scope("update"):
                return (loss_sum + l_k, _jax.tree.map(_jnp.add, grad_sum, gw_k)), gx_k

        init = (_jnp.zeros((), _jnp.float32), _jax.tree.map(_jnp.zeros_like, weights))
        (loss, grad_w), grad_x = _jax.lax.scan(body, init, (per_example, given["loss_target"]))
    with _jax.named_scope("update"):
        delta_w, new_m, new_v = {}, {}, {}
        for n in TWIN_WEIGHTS:
            delta_w[n], new_m[n], new_v[n] = _adamw(weights[n], grad_w[n], given["m_" + n], given["v_" + n])
    return (loss, grad_x, *[grad_w[n] for n in TWIN_WEIGHTS], *[delta_w[n] for n in TWIN_WEIGHTS],
            *[new_m[n] for n in TWIN_WEIGHTS], *[new_v[n] for n in TWIN_WEIGHTS])
```

```python
import functools

import jax
import jax.numpy as jnp
from jax import lax
from jax.experimental import pallas as pl
from jax.experimental.pallas import tpu as pltpu

F32 = jnp.float32
BF16 = jnp.bfloat16
MXU_DTYPE = BF16

D_MODEL = 1024
DEPTH = 4
CHUNK = 128
EPS = 1e-6
HEAD64 = 64
MLA_HEADS = 8
QK_NOPE = 64
QK_ROPE = 32
QK_DIM = 96
V_DIM = 64
Q_LORA = 256
KV_LORA = 128
SLOT = 128
ROPE_THETA = 10000.0
D_FF_SHARD = 1024
N_CHIPS = 4

ADAM_LR = 0.001
ADAM_B1 = 0.9
ADAM_B2 = 0.999
ADAM_EPS = 1e-08
ADAM_WD = 0.01
ADAM_STEP = 10

TM = 256
TQ = 256
VMEM_LIMIT = 56 * 1024 * 1024

NN = (((1,), (0,)), ((), ()))
NT = (((1,), (1,)), ((), ()))
TN = (((0,), (0,)), ((), ()))
BNN = (((2,), (1,)), ((0,), (0,)))
BNT = (((2,), (2,)), ((0,), (0,)))
BTN = (((1,), (1,)), ((0,), (0,)))


def _dot(a, b, dims):
    return lax.dot_general(a.astype(MXU_DTYPE), b.astype(MXU_DTYPE), dims, preferred_element_type=F32)


def _hdot(a, b, dims=NN):
    return lax.dot_general(a, b, dims, precision=lax.Precision.HIGHEST, preferred_element_type=F32)


def _make_ad(dims, da_dims, da_swap, db_dims, db_swap):
    @jax.custom_vjp
    def f(a, b):
        return _dot(a, b, dims)

    def fwd(a, b):
        return _dot(a, b, dims), (a, b)

    def bwd(res, g):
        a, b = res
        da = _dot(b, g, da_dims) if da_swap else _dot(g, b, da_dims)
        db = _dot(g, a, db_dims) if db_swap else _dot(a, g, db_dims)
        return da, db

    f.defvjp(fwd, bwd)
    return f


@functools.partial(jax.custom_vjp, nondiff_argnums=(1,))
def _roll_ad(x, shift):
    return pltpu.roll(x, shift, 1)


def _roll_ad_fwd(x, shift):
    return pltpu.roll(x, shift, 1), None


def _roll_ad_bwd(shift, _, g):
    return (pltpu.roll(g, (g.shape[1] - shift) % g.shape[1], 1),)


_roll_ad.defvjp(_roll_ad_fwd, _roll_ad_bwd)


class _Ops:
    pass


PLAIN = _Ops()
PLAIN.mm = lambda a, b: _dot(a, b, NN)
PLAIN.mm_nt = lambda a, b: _dot(a, b, NT)
PLAIN.mm_tn = lambda a, b: _dot(a, b, TN)
PLAIN.bmm = lambda a, b: _dot(a, b, BNN)
PLAIN.bmm_nt = lambda a, b: _dot(a, b, BNT)
PLAIN.roll = lambda x, s: pltpu.roll(x, s, 1)

AD = _Ops()
AD.mm = _make_ad(NN, NT, False, TN, False)
AD.mm_nt = _make_ad(NT, NN, False, TN, True)
AD.mm_tn = _make_ad(TN, NT, True, NN, False)
AD.bmm = _make_ad(BNN, BNT, False, BTN, False)
AD.bmm_nt = _make_ad(BNT, BNN, False, BTN, True)
AD.roll = _roll_ad


def _sigmoid(x):
    return jax.nn.sigmoid(x)


def _gelu(x):
    return 0.5 * x * (1.0 + jnp.tanh(0.7978845608028654 * (x + 0.044715 * (x * x * x))))


def _rms(x, g):
    return x * lax.rsqrt(jnp.mean(x * x, axis=-1, keepdims=True) + EPS) * g


def _head_masks256():
    lane = lax.broadcasted_iota(jnp.int32, (1, 4 * HEAD64), 1)
    return [(jnp.right_shift(lane, 6) == h).astype(F32) for h in range(4)]


def _headnorm256(x, g):
    ms = jnp.zeros_like(x)
    sq = x * x
    for m in _head_masks256():
        ms = ms + m * (jnp.sum(sq * m, axis=-1, keepdims=True) * (1.0 / HEAD64))
    return x * lax.rsqrt(ms + EPS) * g


def _slot_norm(x, g, n):
    return x * lax.rsqrt(jnp.sum(x * x, axis=-1, keepdims=True) * (1.0 / n) + EPS) * g


def _rope(ops, x, cos_t, sin_a, sin_b):
    return x * cos_t + ops.roll(x, SLOT - QK_ROPE // 2) * sin_a + ops.roll(x, QK_ROPE // 2) * sin_b


def _inproj(ops, x, g1, wa, wb, wc):
    h = _rms(x, g1)
    return ops.mm(h, wa), ops.mm(h, wb), ops.mm(h, wc)


def _gm_chunk(ops, ur, vr, vg, ws4, bs, og):
    c = ur.shape[0]
    masks = _head_masks256()
    mh = jnp.concatenate([m[None] for m in masks], axis=0)
    u = _gelu(ur)
    v = _headnorm256(_gelu(vr), vg)
    t = lax.broadcasted_iota(jnp.int32, (c, c), 0)
    s = lax.broadcasted_iota(jnp.int32, (c, c), 1)
    w = jnp.where((t >= s)[None], ws4, 0.0)
    y = jnp.sum(ops.bmm(w, v[None] * mh), axis=0)
    for h in range(4):
        y = y + bs[h] * masks[h]
    return _headnorm256(u * y, og)


def _hg_chunk(ops, st, qr, fr, ir, gr, lb, og):
    c = qr.shape[0]
    n = qr.shape[1]
    masks = _head_masks256()
    mh = jnp.concatenate([m[None] for m in masks], axis=0)
    q = qr * _sigmoid(qr)
    f = lb + (1.0 - lb) * _sigmoid(fr)
    k = 1.0 - f
    logf = jnp.log(f)
    t = lax.broadcasted_iota(jnp.int32, (c, c), 0)
    s = lax.broadcasted_iota(jnp.int32, (c, c), 1)
    tcol = lax.broadcasted_iota(jnp.int32, (c, 1), 0)
    b = _hdot((t >= s).astype(F32), logf)
    btot = jnp.sum(logf, axis=0, keepdims=True)
    inter = ops.mm_nt(q * jnp.exp(b), st)
    p4 = jnp.zeros((4, c, c), F32)
    lg = 6
    while lg >= 0:
        m = 1 << lg
        bnd = jnp.left_shift(jnp.right_shift(t, lg + 1), lg + 1) + (m - 1)
        r = _hdot((s == bnd).astype(F32), b)
        right = jnp.bitwise_and(jnp.right_shift(tcol, lg), 1) == 1
        qe = jnp.where(right, q * jnp.exp(jnp.where(right, b - r, 0.0)), 0.0)
        ke = jnp.where(right, 0.0, k * jnp.exp(jnp.where(right, 0.0, r - b)))
        lm = ((jnp.right_shift(t, lg + 1) == jnp.right_shift(s, lg + 1))
              & (jnp.bitwise_and(jnp.right_shift(t, lg), 1) == 1)
              & (jnp.bitwise_and(jnp.right_shift(s, lg), 1) == 0))
        s4 = ops.mm_nt((qe[None] * mh).reshape(4 * c, n), ke).reshape(4, c, c)
        p4 = p4 + lm.astype(F32)[None] * s4
        lg -= 1
    intra = jnp.sum(ops.bmm(p4, ir[None] * mh), axis=0)
    qk = q * k
    dsum = jnp.zeros_like(qk)
    for m_ in masks:
        dsum = dsum + m_ * jnp.sum(qk * m_, axis=-1, keepdims=True)
    o = inter + intra + dsum * ir
    kd = k * jnp.exp(btot - b)
    rr = lax.broadcasted_iota(jnp.int32, (n, n), 0)
    cc = lax.broadcasted_iota(jnp.int32, (n, n), 1)
    bd = (jnp.right_shift(rr, 6) == jnp.right_shift(cc, 6)).astype(F32)
    st_new = st * jnp.exp(btot) + bd * ops.mm_tn(ir, kd)
    y = _headnorm256(o, og) * (gr * _sigmoid(gr))
    return st_new, y


def _mla_pre(ops, cq, ckv, kpe, cos_t, sin_a, sin_b, qag, kvag, qg, kg, wq, wk, wv):
    cqn = _rms(cq, qag)
    ckvn = _rms(ckv, kvag)
    kper = ops.roll(kpe, QK_NOPE)
    qs, ks, vs = [], [], []
    for h in range(MLA_HEADS):
        qh = _slot_norm(ops.mm(cqn, wq[h]), qg, QK_DIM)
        qs.append(_rope(ops, qh, cos_t, sin_a, sin_b))
        kh = _slot_norm(ops.mm(ckvn, wk[h]) + kper, kg, QK_DIM)
        ks.append(_rope(ops, kh, cos_t, sin_a, sin_b))
        vs.append(ops.mm(ckvn, wv[h]))
    return qs, ks, vs


def _outproj(ops, x, ya, yb, o, mog, woa, wob, woc):
    acc = x + ops.mm(ya, woa) + ops.mm(yb, wob)
    for h in range(MLA_HEADS):
        acc = acc + ops.mm(_slot_norm(o[h], mog[h], V_DIM), woc[h])
    return acc


def _ffn_part(ops, x1, g2, w1p, w2p):
    a = ops.mm(_rms(x1, g2), w1p)
    r = jnp.maximum(a, 0.0)
    return ops.mm(r * r, w2p)


def _lower_bounds(r0, r1, r2, r3):
    mx = jnp.maximum(jnp.maximum(r0, r1), jnp.maximum(r2, r3))
    e0, e1, e2, e3 = jnp.exp(r0 - mx), jnp.exp(r1 - mx), jnp.exp(r2 - mx), jnp.exp(r3 - mx)
    inv = 1.0 / (e0 + e1 + e2 + e3)
    s1, s2, s3 = e1 * inv, e2 * inv, e3 * inv
    return jnp.zeros_like(r0), s1, s1 + s2, s1 + s2 + s3


def _cp(sem):
    return pltpu.CompilerParams(dimension_semantics=sem, vmem_limit_bytes=VMEM_LIMIT)


def _rows(tm, n):
    return pl.BlockSpec((tm, n), lambda i: (i, 0))


def _full(a):
    nd = len(a.shape)
    return pl.BlockSpec(a.shape, lambda *_: (0,) * nd)


def _sds(shape, dtype=F32):
    return jax.ShapeDtypeStruct(shape, dtype)


def _acc(ref, val, first):
    @pl.when(first)
    def _():
        ref[...] = val

    @pl.when(jnp.logical_not(first))
    def _():
        ref[...] = ref[...] + val


def _f32(ref):
    return ref[...].astype(F32)


def inproj_fwd(x, g1, wa, wb, wc):
    s, d = x.shape

    def body(x_ref, g_ref, wa_ref, wb_ref, wc_ref, pa_ref, pb_ref, pc_ref):
        pa, pb, pc = _inproj(PLAIN, x_ref[...], g_ref[...], wa_ref[...], wb_ref[...], wc_ref[...])
        pa_ref[...] = pa
        pb_ref[...] = pb
        pc_ref[...] = pc

    return pl.pallas_call(
        body, name="inproj_fwd", grid=(s // TM,),
        in_specs=[_rows(TM, d), _full(g1), _full(wa), _full(wb), _full(wc)],
        out_specs=[_rows(TM, wa.shape[1]), _rows(TM, wb.shape[1]), _rows(TM, wc.shape[1])],
        out_shape=[_sds((s, wa.shape[1])), _sds((s, wb.shape[1])), _sds((s, wc.shape[1]))],
        compiler_params=_cp(("parallel",)),
    )(x, g1, wa, wb, wc)


def inproj_bwd(x, g1, wa, wb, wc, dpa, dpb, dpc, dres):
    s, d = x.shape

    def body(x_ref, g_ref, wa_ref, wb_ref, wc_ref, dpa_ref, dpb_ref, dpc_ref, dres_ref,
             dx_ref, dg_ref, dwa_ref, dwb_ref, dwc_ref):
        first = pl.program_id(0) == 0
        _, vjp = jax.vjp(functools.partial(_inproj, AD), x_ref[...], g_ref[...],
                         _f32(wa_ref), _f32(wb_ref), _f32(wc_ref))
        dx, dg, dwa, dwb, dwc = vjp((dpa_ref[...], dpb_ref[...], dpc_ref[...]))
        dx_ref[...] = dx + dres_ref[...]
        _acc(dg_ref, dg, first)
        _acc(dwa_ref, dwa, first)
        _acc(dwb_ref, dwb, first)
        _acc(dwc_ref, dwc, first)

    return pl.pallas_call(
        body, name="inproj_bwd", grid=(s // TM,),
        in_specs=[_rows(TM, d), _full(g1), _full(wa), _full(wb), _full(wc),
                  _rows(TM, wa.shape[1]), _rows(TM, wb.shape[1]), _rows(TM, wc.shape[1]), _rows(TM, d)],
        out_specs=[_rows(TM, d), _full(g1), _full(wa), _full(wb), _full(wc)],
        out_shape=[_sds((s, d)), _sds(g1.shape), _sds(wa.shape), _sds(wb.shape), _sds(wc.shape)],
        compiler_params=_cp(("arbitrary",)),
    )(x, g1, wa, wb, wc, dpa, dpb, dpc, dres)


def gm_fwd(pa, vg, ws4, bs, og):
    s = pa.shape[0]
    w = pa.shape[1] // 2

    def body(pa_ref, vg_ref, ws_ref, bs_ref, og_ref, ya_ref):
        bsl = [bs_ref[h] for h in range(4)]
        ya_ref[...] = _gm_chunk(PLAIN, pa_ref[:, 0:w], pa_ref[:, w:2 * w], vg_ref[...], ws_ref[...], bsl, og_ref[...])

    return pl.pallas_call(
        body, name="gm_fwd", grid=(s // CHUNK,),
        in_specs=[_rows(CHUNK, 2 * w), _full(vg), _full(ws4), _full(bs), _full(og)],
        out_specs=_rows(CHUNK, w), out_shape=_sds((s, w)),
        compiler_params=_cp(("parallel",)),
    )(pa, vg, ws4, bs, og)


def gm_bwd(pa, vg, ws4, bs, og, dya):
    s = pa.shape[0]
    w = pa.shape[1] // 2

    def body(pa_ref, vg_ref, ws_ref, bs_ref, og_ref, dya_ref, dpa_ref, dvg_ref, dws_ref, dbs_ref, dog_ref):
        first = pl.program_id(0) == 0
        bsl = [bs_ref[h] for h in range(4)]
        _, vjp = jax.vjp(functools.partial(_gm_chunk, AD), pa_ref[:, 0:w], pa_ref[:, w:2 * w],
                         vg_ref[...], ws_ref[...], bsl, og_ref[...])
        du, dv, dvg, dws, dbs, dog = vjp(dya_ref[...])
        dpa_ref[:, 0:w] = du
        dpa_ref[:, w:2 * w] = dv
        _acc(dvg_ref, dvg, first)
        _acc(dws_ref, dws, first)
        _acc(dog_ref, dog, first)
        for h in range(4):
            _acc(dbs_ref.at[h], dbs[h], first)

    return pl.pallas_call(
        body, name="gm_bwd", grid=(s // CHUNK,),
        in_specs=[_rows(CHUNK, 2 * w), _full(vg), _full(ws4), _full(bs), _full(og), _rows(CHUNK, w)],
        out_specs=[_rows(CHUNK, 2 * w), _full(vg), _full(ws4), _full(bs), _full(og)],
        out_shape=[_sds((s, 2 * w)), _sds(vg.shape), _sds(ws4.shape), _sds(bs.shape), _sds(og.shape)],
        compiler_params=_cp(("arbitrary",)),
    )(pa, vg, ws4, bs, og, dya)


def hg_fwd(pb, lb, og):
    s = pb.shape[0]
    w = pb.shape[1] // 4
    nc = s // CHUNK

    def body(pb_ref, lb_ref, og_ref, yb_ref, states_ref, st_ref):
        @pl.when(pl.program_id(0) == 0)
        def _():
            st_ref[...] = jnp.zeros_like(st_ref)

        st = st_ref[...]
        states_ref[...] = st
        st_new, y = _hg_chunk(PLAIN, st, pb_ref[:, 0:w], pb_ref[:, w:2 * w], pb_ref[:, 2 * w:3 * w],
                              pb_ref[:, 3 * w:4 * w], lb_ref[...], og_ref[...])
        st_ref[...] = st_new
        yb_ref[...] = y

    return pl.pallas_call(
        body, name="hg_fwd", grid=(nc,),
        in_specs=[_rows(CHUNK, 4 * w), _full(lb), _full(og)],
        out_specs=[_rows(CHUNK, w), pl.BlockSpec((None, w, w), lambda i: (i, 0, 0))],
        out_shape=[_sds((s, w)), _sds((nc, w, w))],
        scratch_shapes=[pltpu.VMEM((w, w), F32)],
        compiler_params=_cp(("arbitrary",)),
    )(pb, lb, og)


def hg_bwd(pb, lb, og, states, dyb):
    s = pb.shape[0]
    w = pb.shape[1] // 4
    nc = s // CHUNK

    def body(pb_ref, lb_ref, og_ref, states_ref, dyb_ref, dpb_ref, dlb_ref, dog_ref, dst_ref):
        first = pl.program_id(0) == 0

        @pl.when(first)
        def _():
            dst_ref[...] = jnp.zeros_like(dst_ref)

        _, vjp = jax.vjp(functools.partial(_hg_chunk, AD), states_ref[...], pb_ref[:, 0:w], pb_ref[:, w:2 * w],
                         pb_ref[:, 2 * w:3 * w], pb_ref[:, 3 * w:4 * w], lb_ref[...], og_ref[...])
        dst, dq, df, di, dg, dlb, dog = vjp((dst_ref[...], dyb_ref[...]))
        dst_ref[...] = dst
        dpb_ref[:, 0:w] = dq
        dpb_ref[:, w:2 * w] = df
        dpb_ref[:, 2 * w:3 * w] = di
        dpb_ref[:, 3 * w:4 * w] = dg
        _acc(dlb_ref, dlb, first)
        _acc(dog_ref, dog, first)

    rev = lambda i: (nc - 1 - i, 0)
    return pl.pallas_call(
        body, name="hg_bwd", grid=(nc,),
        in_specs=[pl.BlockSpec((CHUNK, 4 * w), rev), _full(lb), _full(og),
                  pl.BlockSpec((None, w, w), lambda i: (nc - 1 - i, 0, 0)), pl.BlockSpec((CHUNK, w), rev)],
        out_specs=[pl.BlockSpec((CHUNK, 4 * w), rev), _full(lb), _full(og)],
        out_shape=[_sds((s, 4 * w)), _sds(lb.shape), _sds(og.shape)],
        scratch_shapes=[pltpu.VMEM((w, w), F32)],
        compiler_params=_cp(("arbitrary",)),
    )(pb, lb, og, states, dyb)


def lower_bounds_fwd(hlb):
    def body(h_ref, o_ref):
        outs = _lower_bounds(*[h_ref[pl.ds(i, 1), :] for i in range(DEPTH)])
        for i in range(DEPTH):
            o_ref[pl.ds(i, 1), :] = outs[i]

    return pl.pallas_call(body, name="lower_bounds_fwd", out_shape=_sds(hlb.shape))(hlb)


def lower_bounds_bwd(hlb, dlbs):
    def body(h_ref, d_ref, o_ref):
        _, vjp = jax.vjp(_lower_bounds, *[h_ref[pl.ds(i, 1), :] for i in range(DEPTH)])
        outs = vjp(tuple(d_ref[pl.ds(i, 1), :] for i in range(DEPTH)))
        for i in range(DEPTH):
            o_ref[pl.ds(i, 1), :] = outs[i]

    return pl.pallas_call(body, name="lower_bounds_bwd", out_shape=_sds(hlb.shape))(hlb, dlbs)


def _mla_pre_args(pc_ref, cos_ref, sa_ref, sb_ref, qag_ref, kvag_ref, qg_ref, kg_ref, wq_ref, wk_ref, wv_ref, cast):
    sl = lambda h: slice(h * SLOT, (h + 1) * SLOT)
    ld = (lambda r, h: r[:, sl(h)].astype(F32)) if cast else (lambda r, h: r[:, sl(h)])
    diff = (pc_ref[:, 0:Q_LORA], pc_ref[:, Q_LORA:Q_LORA + KV_LORA], pc_ref[:, Q_LORA + KV_LORA:Q_LORA + 2 * KV_LORA],
            qag_ref[...], kvag_ref[...], qg_ref[...], kg_ref[...],
            [ld(wq_ref, h) for h in range(MLA_HEADS)], [ld(wk_ref, h) for h in range(MLA_HEADS)],
            [ld(wv_ref, h) for h in range(MLA_HEADS)])
    tables = (cos_ref[...], sa_ref[...], sb_ref[...])
    return diff, tables


def _mla_pre_fn(ops, tables, cq, ckv, kpe, qag, kvag, qg, kg, wq, wk, wv):
    return _mla_pre(ops, cq, ckv, kpe, *tables, qag, kvag, qg, kg, wq, wk, wv)


def mla_pre_fwd(pc, cos_t, sin_a, sin_b, qag, kvag, qg, kg, wq, wk, wv):
    s = pc.shape[0]
    hw = MLA_HEADS * SLOT

    def body(pc_ref, cos_ref, sa_ref, sb_ref, qag_ref, kvag_ref, qg_ref, kg_ref, wq_ref, wk_ref, wv_ref,
             q_ref, k_ref, v_ref):
        diff, tables = _mla_pre_args(pc_ref, cos_ref, sa_ref, sb_ref, qag_ref, kvag_ref, qg_ref, kg_ref,
                                     wq_ref, wk_ref, wv_ref, False)
        qs, ks, vs = _mla_pre_fn(PLAIN, tables, *diff)
        for h in range(MLA_HEADS):
            q_ref[:, h * SLOT:(h + 1) * SLOT] = qs[h].astype(q_ref.dtype)
            k_ref[:, h * SLOT:(h + 1) * SLOT] = ks[h].astype(k_ref.dtype)
            v_ref[:, h * SLOT:(h + 1) * SLOT] = vs[h].astype(v_ref.dtype)

    return pl.pallas_call(
        body, name="mla_pre_fwd", grid=(s // TM,),
        in_specs=[_rows(TM, pc.shape[1]), _rows(TM, SLOT), _rows(TM, SLOT), _rows(TM, SLOT),
                  _full(qag), _full(kvag), _full(qg), _full(kg), _full(wq), _full(wk), _full(wv)],
        out_specs=[_rows(TM, hw)] * 3, out_shape=[_sds((s, hw), MXU_DTYPE)] * 3,
        compiler_params=_cp(("parallel",)),
    )(pc, cos_t, sin_a, sin_b, qag, kvag, qg, kg, wq, wk, wv)


def mla_pre_bwd(pc, cos_t, sin_a, sin_b, qag, kvag, qg, kg, wq, wk, wv, dq, dk, dv):
    s = pc.shape[0]
    hw = MLA_HEADS * SLOT

    def body(pc_ref, cos_ref, sa_ref, sb_ref, qag_ref, kvag_ref, qg_ref, kg_ref, wq_ref, wk_ref, wv_ref,
             dq_ref, dk_ref, dv_ref, dpc_ref, dqag_ref, dkvag_ref, dqg_ref, dkg_ref, dwq_ref, dwk_ref, dwv_ref):
        first = pl.program_id(0) == 0
        diff, tables = _mla_pre_args(pc_ref, cos_ref, sa_ref, sb_ref, qag_ref, kvag_ref, qg_ref, kg_ref,
                                     wq_ref, wk_ref, wv_ref, True)
        _, vjp = jax.vjp(functools.partial(_mla_pre_fn, AD, tables), *diff)
        sl = lambda h: slice(h * SLOT, (h + 1) * SLOT)
        cot = ([dq_ref[:, sl(h)] for h in range(MLA_HEADS)], [dk_ref[:, sl(h)] for h in range(MLA_HEADS)],
               [dv_ref[:, sl(h)] for h in range(MLA_HEADS)])
        dcq, dckv, dkpe, dqag, dkvag, dqg, dkg, dwq, dwk, dwv = vjp(cot)
        dpc_ref[:, 0:Q_LORA] = dcq
        dpc_ref[:, Q_LORA:Q_LORA + KV_LORA] = dckv
        dpc_ref[:, Q_LORA + KV_LORA:Q_LORA + 2 * KV_LORA] = dkpe
        _acc(dqag_ref, dqag, first)
        _acc(dkvag_ref, dkvag, first)
        _acc(dqg_ref, dqg, first)
        _acc(dkg_ref, dkg, first)
        for h in range(MLA_HEADS):
            _acc(dwq_ref.at[:, sl(h)], dwq[h], first)
            _acc(dwk_ref.at[:, sl(h)], dwk[h], first)
            _acc(dwv_ref.at[:, sl(h)], dwv[h], first)

    return pl.pallas_call(
        body, name="mla_pre_bwd", grid=(s // TM,),
        in_specs=[_rows(TM, pc.shape[1]), _rows(TM, SLOT), _rows(TM, SLOT), _rows(TM, SLOT),
                  _full(qag), _full(kvag), _full(qg), _full(kg), _full(wq), _full(wk), _full(wv),
                  _rows(TM, hw), _rows(TM, hw), _rows(TM, hw)],
        out_specs=[_rows(TM, pc.shape[1]), _full(qag), _full(kvag), _full(qg), _full(kg),
                   _full(wq), _full(wk), _full(wv)],
        out_shape=[_sds(pc.shape), _sds(qag.shape), _sds(kvag.shape), _sds(qg.shape), _sds(kg.shape),
                   _sds(wq.shape), _sds(wk.shape), _sds(wv.shape)],
        compiler_params=_cp(("arbitrary",)),
    )(pc, cos_t, sin_a, sin_b, qag, kvag, qg, kg, wq, wk, wv, dq, dk, dv)


ATT_SCALE = QK_DIM ** -0.5
NEG_BIG = -1e30


def attn_fwd(q, k, v):
    s = q.shape[0]
    nq = s // TQ

    def body(q_ref, k_ref, v_ref, o_ref, lse_ref):
        qi = pl.program_id(1)
        qq = q_ref[...]
        row = lax.broadcasted_iota(jnp.int32, (TQ, TQ), 0)
        col = lax.broadcasted_iota(jnp.int32, (TQ, TQ), 1)

        def step(ki, carry, masked):
            m, l, acc = carry
            rows = pl.ds(pl.multiple_of(ki * TQ, TQ), TQ)
            sc = _dot(qq, k_ref[rows, :], NT) * ATT_SCALE
            if masked:
                sc = jnp.where(row >= col, sc, NEG_BIG)
            m_new = jnp.maximum(m, jnp.max(sc, axis=-1, keepdims=True))
            alpha = jnp.exp(m - m_new)
            p = jnp.exp(sc - m_new)
            l = alpha * l + jnp.sum(p, axis=-1, keepdims=True)
            acc = alpha * acc + _dot(p, v_ref[rows, :], NN)
            return m_new, l, acc

        init = (jnp.full((TQ, 1), NEG_BIG, F32), jnp.zeros((TQ, 1), F32), jnp.zeros((TQ, SLOT), F32))
        carry = lax.fori_loop(0, qi, lambda ki, c: step(ki, c, False), init)
        m, l, acc = step(qi, carry, True)
        o_ref[...] = acc / l
        lse_ref[...] = m + jnp.log(l)

    head_col = pl.BlockSpec((s, SLOT), lambda h, i: (0, h))
    return pl.pallas_call(
        body, name="attn_fwd", grid=(MLA_HEADS, nq),
        in_specs=[pl.BlockSpec((TQ, SLOT), lambda h, i: (i, h)), head_col, head_col],
        out_specs=[pl.BlockSpec((TQ, SLOT), lambda h, i: (i, h)), pl.BlockSpec((None, TQ, 1), lambda h, i: (h, i, 0))],
        out_shape=[_sds((s, MLA_HEADS * SLOT)), _sds((MLA_HEADS, s, 1))],
        compiler_params=_cp(("parallel", "parallel")),
    )(q, k, v)


def attn_bwd(q, k, v, o, do, lse):
    s = q.shape[0]
    nq = s // TQ

    def body(q_ref, k_ref, v_ref, o_ref, do_ref, lse_ref, dq_ref, dk_ref, dv_ref, delta_ref):
        ki = pl.program_id(1)

        @pl.when(ki == 0)
        def _():
            dq_ref[...] = jnp.zeros_like(dq_ref)

            def dl(i, c):
                rows = pl.ds(pl.multiple_of(i * TQ, TQ), TQ)
                delta_ref[rows, :] = jnp.sum(do_ref[rows, :] * o_ref[rows, :], axis=-1, keepdims=True)
                return c

            lax.fori_loop(0, nq, dl, 0)

        kk = k_ref[...]
        vv = v_ref[...]
        row = lax.broadcasted_iota(jnp.int32, (TQ, TQ), 0)
        col = lax.broadcasted_iota(jnp.int32, (TQ, TQ), 1)

        def step(qi, carry, masked):
            dk, dv = carry
            rows = pl.ds(pl.multiple_of(qi * TQ, TQ), TQ)
            qq = q_ref[rows, :]
            dd = do_ref[rows, :]
            sc = _dot(qq, kk, NT) * ATT_SCALE
            if masked:
                sc = jnp.where(row >= col, sc, NEG_BIG)
            p = jnp.exp(sc - lse_ref[rows, :])
            dv = dv + _dot(p, dd, TN)
            dp = _dot(dd, vv, NT)
            ds = p * (dp - delta_ref[rows, :]) * ATT_SCALE
            dk = dk + _dot(ds, qq, TN)
            dq_ref[rows, :] = dq_ref[rows, :] + _dot(ds, kk, NN)
            return dk, dv

        carry = step(ki, (jnp.zeros((TQ, SLOT), F32), jnp.zeros((TQ, SLOT), F32)), True)
        dk, dv = lax.fori_loop(ki + 1, nq, lambda qi, c: step(qi, c, False), carry)
        dk_ref[...] = dk
        dv_ref[...] = dv

    head_col = pl.BlockSpec((s, SLOT), lambda h, i: (0, h))
    tile = pl.BlockSpec((TQ, SLOT), lambda h, i: (i, h))
    return pl.pallas_call(
        body, name="attn_bwd", grid=(MLA_HEADS, nq),
        in_specs=[head_col, tile, tile, head_col, head_col, pl.BlockSpec((None, s, 1), lambda h, i: (h, 0, 0))],
        out_specs=[head_col, tile, tile],
        out_shape=[_sds((s, MLA_HEADS * SLOT))] * 3,
        scratch_shapes=[pltpu.VMEM((s, 1), F32)],
        compiler_params=_cp(("arbitrary", "arbitrary")),
    )(q, k, v, o, do, lse)


def _outproj_args(ya_ref, yb_ref, o_ref, mog_ref, woa_ref, wob_ref, woc_ref, cast):
    sl = lambda h: slice(h * SLOT, (h + 1) * SLOT)
    ldw = (lambda r: r[...].astype(F32)) if cast else (lambda r: r[...])
    ldc = (lambda h: woc_ref[sl(h), :].astype(F32)) if cast else (lambda h: woc_ref[sl(h), :])
    return (ya_ref[...], yb_ref[...], [o_ref[:, sl(h)] for h in range(MLA_HEADS)],
            [mog_ref[:, sl(h)] for h in range(MLA_HEADS)], ldw(woa_ref), ldw(wob_ref),
            [ldc(h) for h in range(MLA_HEADS)])


def outproj_fwd(x, ya, yb, o, mog, woa, wob, woc):
    s, d = x.shape

    def body(x_ref, ya_ref, yb_ref, o_ref, mog_ref, woa_ref, wob_ref, woc_ref, x1_ref):
        x1_ref[...] = _outproj(PLAIN, x_ref[...], *_outproj_args(ya_ref, yb_ref, o_ref, mog_ref, woa_ref, wob_ref,
                                                                  woc_ref, False))

    return pl.pallas_call(
        body, name="outproj_fwd", grid=(s // TM,),
        in_specs=[_rows(TM, d), _rows(TM, ya.shape[1]), _rows(TM, yb.shape[1]), _rows(TM, o.shape[1]),
                  _full(mog), _full(woa), _full(wob), _full(woc)],
        out_specs=_rows(TM, d), out_shape=_sds((s, d)),
        compiler_params=_cp(("parallel",)),
    )(x, ya, yb, o, mog, woa, wob, woc)


def outproj_bwd(ya, yb, o, mog, woa, wob, woc, dx2, dx1p):
    s, d = dx2.shape
    npart = dx1p.shape[0]

    def body(ya_ref, yb_ref, o_ref, mog_ref, woa_ref, wob_ref, woc_ref, dx2_ref, dx1p_ref,
             dx1_ref, dya_ref, dyb_ref, do_ref, dmog_ref, dwoa_ref, dwob_ref, dwoc_ref):
        first = pl.program_id(0) == 0
        sl = lambda h: slice(h * SLOT, (h + 1) * SLOT)
        dx1 = dx2_ref[...]
        for p in range(npart):
            dx1 = dx1 + dx1p_ref[p]
        dx1_ref[...] = dx1
        args = _outproj_args(ya_ref, yb_ref, o_ref, mog_ref, woa_ref, wob_ref, woc_ref, True)
        _, vjp = jax.vjp(lambda *a: _outproj(AD, jnp.zeros_like(dx1), *a), *args)
        dya, dyb, do, dmog, dwoa, dwob, dwoc = vjp(dx1)
        dya_ref[...] = dya
        dyb_ref[...] = dyb
        _acc(dwoa_ref, dwoa, first)
        _acc(dwob_ref, dwob, first)
        for h in range(MLA_HEADS):
            do_ref[:, sl(h)] = do[h]
            _acc(dmog_ref.at[:, sl(h)], dmog[h], first)
            _acc(dwoc_ref.at[sl(h), :], dwoc[h], first)

    return pl.pallas_call(
        body, name="outproj_bwd", grid=(s // TM,),
        in_specs=[_rows(TM, ya.shape[1]), _rows(TM, yb.shape[1]), _rows(TM, o.shape[1]),
                  _full(mog), _full(woa), _full(wob), _full(woc), _rows(TM, d),
                  pl.BlockSpec((npart, TM, d), lambda i: (0, i, 0))],
        out_specs=[_rows(TM, d), _rows(TM, ya.shape[1]), _rows(TM, yb.shape[1]), _rows(TM, o.shape[1]),
                   _full(mog), _full(woa), _full(wob), _full(woc)],
        out_shape=[_sds((s, d)), _sds(ya.shape), _sds(yb.shape), _sds(o.shape),
                   _sds(mog.shape), _sds(woa.shape), _sds(wob.shape), _sds(woc.shape)],
        compiler_params=_cp(("arbitrary",)),
    )(ya, yb, o, mog, woa, wob, woc, dx2, dx1p)


def ffn_fwd(x1, g2, w1, w2):
    s, d = x1.shape
    npart, _, fs = w1.shape

    def body(x1_ref, g_ref, w1_ref, w2_ref, x2_ref):
        p = pl.program_id(1)
        x1v = x1_ref[...]
        part = _ffn_part(PLAIN, x1v, g_ref[...], w1_ref[...], w2_ref[...])

        @pl.when(p == 0)
        def _():
            x2_ref[...] = x1v + part

        @pl.when(p != 0)
        def _():
            x2_ref[...] = x2_ref[...] + part

    return pl.pallas_call(
        body, name="ffn_fwd", grid=(s // TM, npart),
        in_specs=[pl.BlockSpec((TM, d), lambda i, p: (i, 0)), pl.BlockSpec(g2.shape, lambda i, p: (0, 0)),
                  pl.BlockSpec((None, d, fs), lambda i, p: (p, 0, 0)), pl.BlockSpec((None, fs, d), lambda i, p: (p, 0, 0))],
        out_specs=pl.BlockSpec((TM, d), lambda i, p: (i, 0)), out_shape=_sds((s, d)),
        compiler_params=_cp(("parallel", "arbitrary")),
    )(x1, g2, w1, w2)


def ffn_bwd(x1, g2, w1, w2, dx2):
    s, d = x1.shape
    npart, _, fs = w1.shape

    def body(x1_ref, g_ref, w1_ref, w2_ref, dx2_ref, dx1p_ref, dg_ref, dw1_ref, dw2_ref):
        p = pl.program_id(0)
        i = pl.program_id(1)
        _, vjp = jax.vjp(functools.partial(_ffn_part, AD), x1_ref[...], g_ref[...], _f32(w1_ref), _f32(w2_ref))
        dx1, dg, dw1, dw2 = vjp(dx2_ref[...])
        dx1p_ref[...] = dx1
        _acc(dg_ref, dg, (p == 0) & (i == 0))
        _acc(dw1_ref, dw1, i == 0)
        _acc(dw2_ref, dw2, i == 0)

    return pl.pallas_call(
        body, name="ffn_bwd", grid=(npart, s // TM),
        in_specs=[pl.BlockSpec((TM, d), lambda p, i: (i, 0)), pl.BlockSpec(g2.shape, lambda p, i: (0, 0)),
                  pl.BlockSpec((None, d, fs), lambda p, i: (p, 0, 0)), pl.BlockSpec((None, fs, d), lambda p, i: (p, 0, 0)),
                  pl.BlockSpec((TM, d), lambda p, i: (i, 0))],
        out_specs=[pl.BlockSpec((None, TM, d), lambda p, i: (p, i, 0)), pl.BlockSpec(g2.shape, lambda p, i: (0, 0)),
                   pl.BlockSpec((None, d, fs), lambda p, i: (p, 0, 0)), pl.BlockSpec((None, fs, d), lambda p, i: (p, 0, 0))],
        out_shape=[_sds((npart, s, d)), _sds(g2.shape), _sds(w1.shape), _sds(w2.shape)],
        compiler_params=_cp(("arbitrary", "arbitrary")),
    )(x1, g2, w1, w2, dx2)


def loss_head(y, target):
    s, d = y.shape

    def body(y_ref, t_ref, dy_ref, loss_ref):
        err = y_ref[...] - t_ref[...]
        dy_ref[...] = err * (1.0 / d)
        part = jnp.sum(jnp.sum(err * err, axis=-1, keepdims=True), axis=0, keepdims=True) * (0.5 / d)
        _acc(loss_ref, jnp.broadcast_to(part, loss_ref.shape), pl.program_id(0) == 0)

    return pl.pallas_call(
        body, name="loss_head", grid=(s // TM,),
        in_specs=[_rows(TM, d), _rows(TM, d)],
        out_specs=[_rows(TM, d), pl.BlockSpec((1, SLOT), lambda i: (0, 0))],
        out_shape=[_sds((s, d)), _sds((1, SLOT))],
        compiler_params=_cp(("arbitrary",)),
    )(y, target)


def _row_block(r):
    for b in (512, 256, 128, 64, 32, 16, 8):
        if r % b == 0:
            return b
    return r


def add_to(parts, out_dtype, name):
    r, c = parts[0].shape
    br = _row_block(r)
    n = len(parts)

    def body(*refs):
        acc = refs[0][...].astype(F32)
        for ref in refs[1:n]:
            acc = acc + ref[...].astype(F32)
        refs[n][...] = acc.astype(out_dtype)

    return pl.pallas_call(
        body, name=name, grid=(r // br,), in_specs=[_rows(br, c)] * n, out_specs=_rows(br, c),
        out_shape=_sds((r, c), out_dtype), compiler_params=_cp(("parallel",)),
    )(*parts)


def adamw(w, g, m, v, name):
    r, c = w.shape
    br = _row_block(r)
    c1 = 1.0 / (1.0 - ADAM_B1 ** ADAM_STEP)
    c2 = 1.0 / (1.0 - ADAM_B2 ** ADAM_STEP)

    def body(w_ref, g_ref, m_ref, v_ref, d_ref, nm_ref, nv_ref):
        gg = g_ref[...]
        nm = ADAM_B1 * m_ref[...] + (1.0 - ADAM_B1) * gg
        nv = ADAM_B2 * v_ref[...] + (1.0 - ADAM_B2) * (gg * gg)
        d_ref[...] = -ADAM_LR * ((nm * c1) / (jnp.sqrt(nv * c2) + ADAM_EPS) + ADAM_WD * w_ref[...])
        nm_ref[...] = nm
        nv_ref[...] = nv

    return pl.pallas_call(
        body, name=name, grid=(r // br,), in_specs=[_rows(br, c)] * 4, out_specs=[_rows(br, c)] * 3,
        out_shape=[_sds((r, c))] * 3, compiler_params=_cp(("parallel",)),
    )(w, g, m, v)


MESH = pl.DeviceIdType.MESH
ANY = pl.BlockSpec(memory_space=pl.ANY)


def _place():
    x, y, c = lax.axis_index("x"), lax.axis_index("y"), lax.axis_index("c")
    chips = [(1 - x, y), (x, 1 - y), (1 - x, 1 - y)]
    return x, y, c, chips


def gather_chips(arrs, name):
    n = len(arrs)

    def body(*refs):
        ins, outs = refs[:n], refs[n:2 * n]
        send_sems, recv_sems, local_sems = refs[2 * n:]
        x, y, c, chips = _place()
        me = 2 * x + y
        half = pl.ds(2 * c, 2)
        other = pl.ds(2 - 2 * c, 2)
        sibling = (x, y, 1 - c)

        def copy(i, k, src, dst, to):
            return pltpu.make_async_remote_copy(src_ref=src, dst_ref=dst, send_sem=send_sems.at[i, k],
                                                recv_sem=recv_sems.at[i, k], device_id=to, device_id_type=MESH)

        local = [pltpu.make_async_copy(ins[i], outs[i].at[me], local_sems.at[i]) for i in range(n)]
        for cp in local:
            cp.start()
        sent = []
        for i in range(n):
            for j, chip in enumerate(chips):
                cp = copy(i, j, ins[i].at[half], outs[i].at[me, half], (*chip, c))
                cp.start()
                sent.append(cp)
        for i in range(n):
            for j, (cx, cy) in enumerate(chips):
                blk = outs[i].at[2 * cx + cy, half]
                copy(i, j, blk, blk, (cx, cy, c)).wait_recv()
                cp = copy(i, 3 + j, blk, blk, sibling)
                cp.start()
                sent.append(cp)
        for i in range(n):
            for j, (cx, cy) in enumerate(chips):
                blk = outs[i].at[2 * cx + cy, other]
                copy(i, 3 + j, blk, blk, sibling).wait_recv()
        for cp in sent:
            cp.wait_send()
        for cp in local:
            cp.wait()

    return pl.pallas_call(
        body, name=name, in_specs=[ANY] * n, out_specs=[ANY] * n,
        out_shape=[_sds((N_CHIPS,) + a.shape, a.dtype) for a in arrs],
        scratch_shapes=[pltpu.SemaphoreType.DMA((n, 6)), pltpu.SemaphoreType.DMA((n, 6)), pltpu.SemaphoreType.DMA((n,))],
    )(*arrs)


def swap_halves(arrs, name):
    n = len(arrs)

    def body(*refs):
        ins, outs = refs[:n], refs[n:2 * n]
        send_sems, recv_sems = refs[2 * n:]
        x, y, c, _ = _place()
        cps = [pltpu.make_async_remote_copy(src_ref=ins[i].at[1 - c], dst_ref=outs[i], send_sem=send_sems.at[i],
                                            recv_sem=recv_sems.at[i], device_id=(x, y, 1 - c), device_id_type=MESH)
               for i in range(n)]
        for cp in cps:
            cp.start()
        for cp in cps:
            cp.wait()

    return pl.pallas_call(
        body, name=name, in_specs=[ANY] * n, out_specs=[ANY] * n,
        out_shape=[_sds(a.shape[1:], a.dtype) for a in arrs],
        scratch_shapes=[pltpu.SemaphoreType.DMA((n,)), pltpu.SemaphoreType.DMA((n,))],
    )(*arrs)


def scatter_chips(arrs, name):
    n = len(arrs)

    def body(*refs):
        ins, outs = refs[:n], refs[n:2 * n]
        send_sems, recv_sems = refs[2 * n:]
        x, y, c, chips = _place()
        me = 2 * x + y
        cps = []
        for i in range(n):
            for j, (cx, cy) in enumerate(chips):
                for t in range(2):
                    cps.append(pltpu.make_async_remote_copy(
                        src_ref=ins[i].at[t, 2 * cx + cy], dst_ref=outs[i].at[j, t], send_sem=send_sems.at[i, j, t],
                        recv_sem=recv_sems.at[i, j, t], device_id=(cx, cy, c), device_id_type=MESH))
        for cp in cps:
            cp.start()
        for cp in cps:
            cp.wait()
        del me

    return pl.pallas_call(
        body, name=name, in_specs=[ANY] * n, out_specs=[ANY] * n,
        out_shape=[_sds((3, 2) + a.shape[2:], a.dtype) for a in arrs],
        scratch_shapes=[pltpu.SemaphoreType.DMA((n, 3, 2)), pltpu.SemaphoreType.DMA((n, 3, 2))],
    )(*arrs)


def share_halves(arrs, name):
    n = len(arrs)

    def body(*refs):
        ins, outs = refs[:n], refs[n:2 * n]
        send_sems, recv_sems, local_sems = refs[2 * n:]
        x, y, c, _ = _place()
        half = pl.ds(2 * c, 2)
        local = [pltpu.make_async_copy(ins[i], outs[i].at[half], local_sems.at[i]) for i in range(n)]
        cps = [pltpu.make_async_remote_copy(src_ref=ins[i], dst_ref=outs[i].at[half], send_sem=send_sems.at[i],
                                            recv_sem=recv_sems.at[i], device_id=(x, y, 1 - c), device_id_type=MESH)
               for i in range(n)]
        for cp in local + cps:
            cp.start()
        for cp in cps:
            cp.wait()
        for cp in local:
            cp.wait()

    return pl.pallas_call(
        body, name=name, in_specs=[ANY] * n, out_specs=[ANY] * n,
        out_shape=[_sds((4,) + a.shape[1:], a.dtype) for a in arrs],
        scratch_shapes=[pltpu.SemaphoreType.DMA((n,)), pltpu.SemaphoreType.DMA((n,)), pltpu.SemaphoreType.DMA((n,))],
    )(*arrs)


def _pad_slots(a, live):
    lead = a.shape[:-1]
    a = a.reshape(lead + (MLA_HEADS, live))
    a = jnp.pad(a, [(0, 0)] * len(lead) + [(0, 0), (0, SLOT - live)])
    return a.reshape(lead + (MLA_HEADS * SLOT,))


def _unpad_slots(a, live):
    lead = a.shape[:-1]
    return a.reshape(lead + (MLA_HEADS, SLOT))[..., :live].reshape(lead + (MLA_HEADS * live,))


def _rope_tables(positions, s):
    half = QK_ROPE // 2
    inv_freq = ROPE_THETA ** (-jnp.arange(half, dtype=F32) / half)
    ang = positions.reshape(s).astype(F32)[:, None] * inv_freq[None, :]
    cos, sin = jnp.cos(ang), jnp.sin(ang)
    one = jnp.ones((s, QK_NOPE), F32)
    z64, z16, z32 = jnp.zeros((s, QK_NOPE), F32), jnp.zeros((s, half), F32), jnp.zeros((s, SLOT - QK_DIM), F32)
    cos_t = jnp.concatenate([one, cos, cos, z32], axis=1)
    sin_a = jnp.concatenate([z64, -sin, z16, z32], axis=1)
    sin_b = jnp.concatenate([z64, z16, sin, z32], axis=1)
    return cos_t, sin_a, sin_b


def _layer_weights(full, small, l):
    w_in = jnp.concatenate([full["w_in"][p, l] for p in range(N_CHIPS)], axis=1)
    wc = jnp.pad(w_in[:, 1536:], ((0, 0), (0, 512 - (w_in.shape[1] - 1536))))
    w_uq = jnp.concatenate([full["mla_w_uq"][p, l] for p in range(N_CHIPS)], axis=1)
    w_ukv = jnp.concatenate([full["mla_w_ukv"][p, l] for p in range(N_CHIPS)], axis=1)
    ukv = w_ukv.reshape(KV_LORA, MLA_HEADS, QK_NOPE + V_DIM)
    w_out = jnp.concatenate([full["w_out"][p, l] for p in range(N_CHIPS)], axis=0)
    woc = w_out[512:].reshape(MLA_HEADS, V_DIM, D_MODEL)
    woc = jnp.pad(woc, ((0, 0), (0, SLOT - V_DIM), (0, 0))).reshape(MLA_HEADS * SLOT, D_MODEL)
    row = lambda a: a.reshape(1, -1)
    return dict(
        g1=row(small["norm1_gain"][l]), wa=w_in[:, :512], wb=w_in[:, 512:1536], wc=wc,
        vg=row(small["gm_v_gain"][l]), ws=small["gm_w_s"][l], bs=small["gm_b_s"][l].reshape(4, CHUNK, 1),
        gog=row(small["gm_out_gain"][l]), hog=row(small["hg_out_gain"][l]),
        qag=row(small["mla_q_a_gain"][l]), kvag=row(small["mla_kv_a_gain"][l]),
        qg=row(jnp.pad(small["mla_q_gain"][l], (0, SLOT - QK_DIM))), kg=row(jnp.pad(small["mla_k_gain"][l], (0, SLOT - QK_DIM))),
        wq=_pad_slots(w_uq, QK_DIM), wk=_pad_slots(ukv[..., :QK_NOPE].reshape(KV_LORA, -1), QK_NOPE),
        wv=_pad_slots(ukv[..., QK_NOPE:].reshape(KV_LORA, -1), V_DIM),
        mog=row(_pad_slots(small["mla_out_gain"][l], V_DIM)),
        woa=w_out[:256], wob=w_out[256:512], woc=woc,
        g2=row(small["norm2_gain"][l]), w1=full["w_ff1"][:, l], w2=full["w_ff2"][:, l],
    )


def _shard_cols(a):
    r, c4 = a.shape
    return a.reshape(r, N_CHIPS, c4 // N_CHIPS).transpose(1, 0, 2)


def local_step(x, positions, target, full, small):
    s = x.shape[0]
    cos_t, sin_a, sin_b = _rope_tables(positions, s)
    lbs = lower_bounds_fwd(small["hg_lower_bound"])
    lw = [_layer_weights(full, small, l) for l in range(DEPTH)]
    saved = []
    for l in range(DEPTH):
        w = lw[l]
        lb = lbs[l].reshape(1, -1)
        pa, pb, pc = inproj_fwd(x, w["g1"], w["wa"], w["wb"], w["wc"])
        ya = gm_fwd(pa, w["vg"], w["ws"], w["bs"], w["gog"])
        yb, states = hg_fwd(pb, lb, w["hog"])
        q, k, v = mla_pre_fwd(pc, cos_t, sin_a, sin_b, w["qag"], w["kvag"], w["qg"], w["kg"], w["wq"], w["wk"], w["wv"])
        o, lse = attn_fwd(q, k, v)
        x1 = outproj_fwd(x, ya, yb, o, w["mog"], w["woa"], w["wob"], w["woc"])
        x2 = ffn_fwd(x1, w["g2"], w["w1"], w["w2"])
        saved.append(dict(x=x, pa=pa, pb=pb, pc=pc, ya=ya, yb=yb, states=states, q=q, k=k, v=v, o=o, lse=lse, x1=x1, lb=lb))
        x = x2
    dx, loss_part = loss_head(x, target)
    big = {n: [None] * DEPTH for n in ("w_in", "mla_w_uq", "mla_w_ukv", "w_out", "w_ff1", "w_ff2")}
    sm = {n: [None] * DEPTH for n in ("norm1_gain", "gm_v_gain", "gm_w_s", "gm_b_s", "gm_out_gain", "hg_out_gain",
                                       "mla_q_a_gain", "mla_kv_a_gain", "mla_q_gain", "mla_k_gain", "mla_out_gain",
                                       "norm2_gain")}
    dlbs = [None] * DEPTH
    for l in reversed(range(DEPTH)):
        w, a = lw[l], saved[l]
        dx1p, dg2, dw1, dw2 = ffn_bwd(a["x1"], w["g2"], w["w1"], w["w2"], dx)
        dx1, dya, dyb, do, dmog, dwoa, dwob, dwoc = outproj_bwd(a["ya"], a["yb"], a["o"], w["mog"], w["woa"], w["wob"],
                                                                 w["woc"], dx, dx1p)
        dq, dk, dv = attn_bwd(a["q"], a["k"], a["v"], a["o"], do, a["lse"])
        dpc, dqag, dkvag, dqg, dkg, dwq, dwk, dwv = mla_pre_bwd(a["pc"], cos_t, sin_a, sin_b, w["qag"], w["kvag"], w["qg"],
                                                                  w["kg"], w["wq"], w["wk"], w["wv"], dq, dk, dv)
        dpb, dlb, dhog = hg_bwd(a["pb"], a["lb"], w["hog"], a["states"], dyb)
        dpa, dvg, dws, dbs, dgog = gm_bwd(a["pa"], w["vg"], w["ws"], w["bs"], w["gog"], dya)
        dx, dg1, dwa, dwb, dwc = inproj_bwd(a["x"], w["g1"], w["wa"], w["wb"], w["wc"], dpa, dpb, dpc, dx1)
        big["w_in"][l] = _shard_cols(jnp.concatenate([dwa, dwb, dwc[:, :1952 - 1536]], axis=1))
        big["mla_w_uq"][l] = _shard_cols(_unpad_slots(dwq, QK_DIM))
        dukv = jnp.concatenate([dwk.reshape(KV_LORA, MLA_HEADS, SLOT)[..., :QK_NOPE],
                                dwv.reshape(KV_LORA, MLA_HEADS, SLOT)[..., :V_DIM]], axis=-1)
        big["mla_w_ukv"][l] = _shard_cols(dukv.reshape(KV_LORA, -1))
        dwo = jnp.concatenate([dwoa, dwob, dwoc.reshape(MLA_HEADS, SLOT, D_MODEL)[:, :V_DIM].reshape(-1, D_MODEL)], axis=0)
        big["w_out"][l] = dwo.reshape(N_CHIPS, -1, D_MODEL)
        big["w_ff1"][l] = dw1
        big["w_ff2"][l] = dw2
        sm["norm1_gain"][l] = dg1[0]
        sm["gm_v_gain"][l] = dvg[0]
        sm["gm_w_s"][l] = dws
        sm["gm_b_s"][l] = dbs[..., 0]
        sm["gm_out_gain"][l] = dgog[0]
        sm["hg_out_gain"][l] = dhog[0]
        sm["mla_q_a_gain"][l] = dqag[0]
        sm["mla_kv_a_gain"][l] = dkvag[0]
        sm["mla_q_gain"][l] = dqg[0, :QK_DIM]
        sm["mla_k_gain"][l] = dkg[0, :QK_DIM]
        sm["mla_out_gain"][l] = _unpad_slots(dmog[0], V_DIM)
        sm["norm2_gain"][l] = dg2[0]
        dlbs[l] = dlb[0]
    big = {n: jnp.stack(v) for n, v in big.items()}
    sm = {n: jnp.stack(v) for n, v in sm.items()}
    sm["hg_lower_bound"] = lower_bounds_bwd(small["hg_lower_bound"], jnp.stack(dlbs))
    return loss_part, dx, big, sm


BIG = ("w_in", "mla_w_uq", "mla_w_ukv", "w_out", "w_ff1", "w_ff2")
SMALL = ("norm1_gain", "gm_v_gain", "gm_w_s", "gm_b_s", "gm_out_gain", "hg_lower_bound", "hg_out_gain",
         "mla_q_a_gain", "mla_kv_a_gain", "mla_q_gain", "mla_k_gain", "mla_out_gain", "norm2_gain")
ORDER = ("norm1_gain", "w_in", "gm_v_gain", "gm_w_s", "gm_b_s", "gm_out_gain", "hg_lower_bound", "hg_out_gain",
         "mla_q_a_gain", "mla_w_uq", "mla_kv_a_gain", "mla_w_ukv", "mla_q_gain", "mla_k_gain", "mla_out_gain",
         "w_out", "norm2_gain", "w_ff1", "w_ff2")
PACK_ROWS = 144


def _pack(arrs):
    flat = jnp.concatenate([a.reshape(-1) for a in arrs])
    total = 16 * PACK_ROWS * SLOT
    return jnp.pad(flat, (0, total - flat.shape[0]))


def _unpack(flat, shapes):
    out, off = [], 0
    for sh in shapes:
        size = 1
        for d in sh:
            size *= d
        out.append(flat[off:off + size].reshape(sh))
        off += size
    return out


def reduce_big(grads):
    flat2 = lambda a: a.reshape(-1, a.shape[-1])
    c = lax.axis_index("c")
    me = 2 * lax.axis_index("x") + lax.axis_index("y")
    got = swap_halves(grads, "swap_halves")
    mine = [lax.dynamic_index_in_dim(g, c, 0, keepdims=False) for g in grads]
    wire = [add_to([flat2(a), flat2(b)], BF16, "sum_cores_wire").reshape(a.shape) for a, b in zip(mine, got)]
    own = []
    for a, b in zip(mine, got):
        a_me = lax.dynamic_index_in_dim(a, me, 1, keepdims=False)
        b_me = lax.dynamic_index_in_dim(b, me, 1, keepdims=False)
        own.append(add_to([flat2(a_me), flat2(b_me)], F32, "sum_cores_own").reshape(a_me.shape))
    recv = scatter_chips(wire, "scatter_chips")
    halves = [add_to([flat2(o), flat2(r[0]), flat2(r[1]), flat2(r[2])], F32, "sum_chips").reshape(o.shape)
              for o, r in zip(own, recv)]
    return share_halves(halves, "share_halves")


def kernel(x, positions, norm1_gain, w_in, gm_v_gain, gm_w_s, gm_b_s, gm_out_gain, hg_lower_bound, hg_out_gain, mla_q_a_gain, mla_w_uq, mla_kv_a_gain, mla_w_ukv, mla_q_gain, mla_k_gain, mla_out_gain, w_out, norm2_gain, w_ff1, w_ff2, loss_target, m_norm1_gain, m_w_in, m_gm_v_gain, m_gm_w_s, m_gm_b_s, m_gm_out_gain, m_hg_lower_bound, m_hg_out_gain, m_mla_q_a_gain, m_mla_w_uq, m_mla_kv_a_gain, m_mla_w_ukv, m_mla_q_gain, m_mla_k_gain, m_mla_out_gain, m_w_out, m_norm2_gain, m_w_ff1, m_w_ff2, v_norm1_gain, v_w_in, v_gm_v_gain, v_gm_w_s, v_gm_b_s, v_gm_out_gain, v_hg_lower_bound, v_hg_out_gain, v_mla_q_a_gain, v_mla_w_uq, v_mla_kv_a_gain, v_mla_w_ukv, v_mla_q_gain, v_mla_k_gain, v_mla_out_gain, v_w_out, v_norm2_gain, v_w_ff1, v_w_ff2):
    given = dict(locals())
    weights = {n: given[n] for n in ORDER}
    moms = {n: given["m_" + n] for n in ORDER}
    vars_ = {n: given["v_" + n] for n in ORDER}
    s, d = x.shape[1], x.shape[2]

    gathered = gather_chips([weights[n].astype(MXU_DTYPE) for n in BIG], "gather_weights")
    full = dict(zip(BIG, gathered))
    small = {n: weights[n] for n in SMALL}

    loss_part, dx, big_g, small_g = local_step(x.reshape(s, d), positions, loss_target.reshape(s, d), full, small)
    loss = lax.psum(loss_part[0, 0], ("x", "y", "c"))

    pack_g = _pack([small_g[n] for n in SMALL]).reshape(2, 2, N_CHIPS, PACK_ROWS, SLOT)
    wire_in = [big_g[n].reshape((2, 2) + big_g[n].shape[1:]) for n in BIG] + [pack_g]
    reduced = reduce_big(wire_in)
    grads = dict(zip(BIG, reduced[:-1]))
    pack_full = gather_chips([reduced[-1]], "gather_small")[0].transpose(1, 0, 2, 3).reshape(-1)
    grads.update(zip(SMALL, _unpack(pack_full, [weights[n].shape for n in SMALL])))

    delta, new_m, new_v = {}, {}, {}
    flat2 = lambda a: a.reshape(-1, a.shape[-1])
    for n in BIG:
        outs = adamw(flat2(weights[n]), flat2(grads[n]), flat2(moms[n]), flat2(vars_[n]), "adamw_" + n)
        delta[n], new_m[n], new_v[n] = [o.reshape(weights[n].shape) for o in outs]
    shapes = [weights[n].shape for n in SMALL]
    pk = lambda t: _pack([t[n] for n in SMALL]).reshape(-1, SLOT)
    outs = adamw(pk(weights), pack_full.reshape(-1, SLOT), pk(moms), pk(vars_), "adamw_small")
    for tgt, o in zip((delta, new_m, new_v), outs):
        tgt.update(zip(SMALL, _unpack(o.reshape(-1), shapes)))

    return (loss, dx.reshape(x.shape), *[grads[n] for n in ORDER], *[delta[n] for n in ORDER],
            *[new_m[n] for n in ORDER], *[new_v[n] for n in ORDER])
```

```python
import functools

import jax
import jax.numpy as jnp
from jax import lax
from jax.experimental import pallas as pl
from jax.experimental.pallas import tpu as pltpu

F32 = jnp.float32
BF16 = jnp.bfloat16
MXU_DTYPE = BF16

D_MODEL = 1024
DEPTH = 4
CHUNK = 128
EPS = 1e-6
HEAD64 = 64
MLA_HEADS = 8
QK_NOPE = 64
QK_ROPE = 32
QK_DIM = 96
V_DIM = 64
Q_LORA = 256
KV_LORA = 128
SLOT = 128
ROPE_THETA = 10000.0
D_FF_SHARD = 1024
N_CHIPS = 4

ADAM_LR = 0.001
ADAM_B1 = 0.9
ADAM_B2 = 0.999
ADAM_EPS = 1e-08
ADAM_WD = 0.01
ADAM_STEP = 10

TM = 256
TQ = 128
VMEM_LIMIT = 56 * 1024 * 1024

NN = (((1,), (0,)), ((), ()))
NT = (((1,), (1,)), ((), ()))
TN = (((0,), (0,)), ((), ()))
BNN = (((2,), (1,)), ((0,), (0,)))
BNT = (((2,), (2,)), ((0,), (0,)))
BTN = (((1,), (1,)), ((0,), (0,)))


def _dot(a, b, dims):
    return lax.dot_general(a.astype(MXU_DTYPE), b.astype(MXU_DTYPE), dims, preferred_element_type=F32)


def _hdot(a, b, dims=NN):
    return lax.dot_general(a, b, dims, precision=lax.Precision.HIGHEST, preferred_element_type=F32)


def _make_ad(dims, da_dims, da_swap, db_dims, db_swap):
    @jax.custom_vjp
    def f(a, b):
        return _dot(a, b, dims)

    def fwd(a, b):
        return _dot(a, b, dims), (a, b)

    def bwd(res, g):
        a, b = res
        da = _dot(b, g, da_dims) if da_swap else _dot(g, b, da_dims)
        db = _dot(g, a, db_dims) if db_swap else _dot(a, g, db_dims)
        return da, db

    f.defvjp(fwd, bwd)
    return f


@functools.partial(jax.custom_vjp, nondiff_argnums=(1,))
def _roll_ad(x, shift):
    return pltpu.roll(x, shift, 1)


def _roll_ad_fwd(x, shift):
    return pltpu.roll(x, shift, 1), None


def _roll_ad_bwd(shift, _, g):
    return (pltpu.roll(g, (g.shape[1] - shift) % g.shape[1], 1),)


_roll_ad.defvjp(_roll_ad_fwd, _roll_ad_bwd)


class _Ops:
    pass


PLAIN = _Ops()
PLAIN.mm = lambda a, b: _dot(a, b, NN)
PLAIN.mm_nt = lambda a, b: _dot(a, b, NT)
PLAIN.mm_tn = lambda a, b: _dot(a, b, TN)
PLAIN.bmm = lambda a, b: _dot(a, b, BNN)
PLAIN.bmm_nt = lambda a, b: _dot(a, b, BNT)
PLAIN.roll = lambda x, s: pltpu.roll(x, s, 1)

AD = _Ops()
AD.mm = _make_ad(NN, NT, False, TN, False)
AD.mm_nt = _make_ad(NT, NN, False, TN, True)
AD.mm_tn = _make_ad(TN, NT, True, NN, False)
AD.bmm = _make_ad(BNN, BNT, False, BTN, False)
AD.bmm_nt = _make_ad(BNT, BNN, False, BTN, True)
AD.roll = _roll_ad


def _sigmoid(x):
    return jax.nn.sigmoid(x)


def _gelu(x):
    return 0.5 * x * (1.0 + jnp.tanh(0.7978845608028654 * (x + 0.044715 * (x * x * x))))


def _rms(x, g):
    return x * lax.rsqrt(jnp.mean(x * x, axis=-1, keepdims=True) + EPS) * g


def _head_masks256():
    lane = lax.broadcasted_iota(jnp.int32, (1, 4 * HEAD64), 1)
    return [(jnp.right_shift(lane, 6) == h).astype(F32) for h in range(4)]


def _headnorm256(x, g):
    ms = jnp.zeros_like(x)
    sq = x * x
    for m in _head_masks256():
        ms = ms + m * (jnp.sum(sq * m, axis=-1, keepdims=True) * (1.0 / HEAD64))
    return x * lax.rsqrt(ms + EPS) * g


def _slot_norm(x, g, n):
    return x * lax.rsqrt(jnp.sum(x * x, axis=-1, keepdims=True) * (1.0 / n) + EPS) * g


def _rope(ops, x, cos_t, sin_a, sin_b):
    return x * cos_t + ops.roll(x, SLOT - QK_ROPE // 2) * sin_a + ops.roll(x, QK_ROPE // 2) * sin_b


def _inproj(ops, x, g1, wa, wb, wc):
    h = _rms(x, g1)
    return ops.mm(h, wa), ops.mm(h, wb), ops.mm(h, wc)


def _gm_chunk(ops, ur, vr, vg, ws4, bs, og):
    c = ur.shape[0]
    masks = _head_masks256()
    mh = jnp.concatenate([m[None] for m in masks], axis=0)
    u = _gelu(ur)
    v = _headnorm256(_gelu(vr), vg)
    t = lax.broadcasted_iota(jnp.int32, (c, c), 0)
    s = lax.broadcasted_iota(jnp.int32, (c, c), 1)
    w = jnp.where((t >= s)[None], ws4, 0.0)
    y = jnp.sum(ops.bmm(w, v[None] * mh), axis=0)
    for h in range(4):
        y = y + bs[h] * masks[h]
    return _headnorm256(u * y, og)


def _hg_chunk(ops, st, qr, fr, ir, gr, lb, og):
    c = qr.shape[0]
    n = qr.shape[1]
    masks = _head_masks256()
    mh = jnp.concatenate([m[None] for m in masks], axis=0)
    q = qr * _sigmoid(qr)
    f = lb + (1.0 - lb) * _sigmoid(fr)
    k = 1.0 - f
    logf = jnp.log(f)
    t = lax.broadcasted_iota(jnp.int32, (c, c), 0)
    s = lax.broadcasted_iota(jnp.int32, (c, c), 1)
    tcol = lax.broadcasted_iota(jnp.int32, (c, 1), 0)
    b = _hdot((t >= s).astype(F32), logf)
    btot = jnp.sum(logf, axis=0, keepdims=True)
    inter = ops.mm_nt(q * jnp.exp(b), st)
    p4 = jnp.zeros((4, c, c), F32)
    lg = 6
    while lg >= 0:
        m = 1 << lg
        bnd = jnp.left_shift(jnp.right_shift(t, lg + 1), lg + 1) + (m - 1)
        r = _hdot((s == bnd).astype(F32), b)
        right = jnp.bitwise_and(jnp.right_shift(tcol, lg), 1) == 1
        qe = jnp.where(right, q * jnp.exp(jnp.where(right, b - r, 0.0)), 0.0)
        ke = jnp.where(right, 0.0, k * jnp.exp(jnp.where(right, 0.0, r - b)))
        lm = ((jnp.right_shift(t, lg + 1) == jnp.right_shift(s, lg + 1))
              & (jnp.bitwise_and(jnp.right_shift(t, lg), 1) == 1)
              & (jnp.bitwise_and(jnp.right_shift(s, lg), 1) == 0))
        s4 = ops.mm_nt((qe[None] * mh).reshape(4 * c, n), ke).reshape(4, c, c)
        p4 = p4 + lm.astype(F32)[None] * s4
        lg -= 1
    intra = jnp.sum(ops.bmm(p4, ir[None] * mh), axis=0)
    qk = q * k
    dsum = jnp.zeros_like(qk)
    for m_ in masks:
        dsum = dsum + m_ * jnp.sum(qk * m_, axis=-1, keepdims=True)
    o = inter + intra + dsum * ir
    kd = k * jnp.exp(btot - b)
    rr = lax.broadcasted_iota(jnp.int32, (n, n), 0)
    cc = lax.broadcasted_iota(jnp.int32, (n, n), 1)
    bd = (jnp.right_shift(rr, 6) == jnp.right_shift(cc, 6)).astype(F32)
    st_new = st * jnp.exp(btot) + bd * ops.mm_tn(ir, kd)
    y = _headnorm256(o, og) * (gr * _sigmoid(gr))
    return st_new, y


def _mla_pre(ops, cq, ckv, kpe, cos_t, sin_a, sin_b, qag, kvag, qg, kg, wq, wk, wv):
    cqn = _rms(cq, qag)
    ckvn = _rms(ckv, kvag)
    kper = ops.roll(kpe, QK_NOPE)
    qs, ks, vs = [], [], []
    for h in range(MLA_HEADS):
        qh = _slot_norm(ops.mm(cqn, wq[h]), qg, QK_DIM)
        qs.append(_rope(ops, qh, cos_t, sin_a, sin_b))
        kh = _slot_norm(ops.mm(ckvn, wk[h]) + kper, kg, QK_DIM)
        ks.append(_rope(ops, kh, cos_t, sin_a, sin_b))
        vs.append(ops.mm(ckvn, wv[h]))
    return qs, ks, vs


def _outproj(ops, x, ya, yb, o, mog, woa, wob, woc):
    acc = x + ops.mm(ya, woa) + ops.mm(yb, wob)
    for h in range(MLA_HEADS):
        acc = acc + ops.mm(_slot_norm(o[h], mog[h], V_DIM), woc[h])
    return acc


def _ffn_part(ops, x1, g2, w1p, w2p):
    a = ops.mm(_rms(x1, g2), w1p)
    r = jnp.maximum(a, 0.0)
    return ops.mm(r * r, w2p)


def _lower_bounds(r0, r1, r2, r3):
    mx = jnp.maximum(jnp.maximum(r0, r1), jnp.maximum(r2, r3))
    e0, e1, e2, e3 = jnp.exp(r0 - mx), jnp.exp(r1 - mx), jnp.exp(r2 - mx), jnp.exp(r3 - mx)
    inv = 1.0 / (e0 + e1 + e2 + e3)
    s1, s2, s3 = e1 * inv, e2 * inv, e3 * inv
    return jnp.zeros_like(r0), s1, s1 + s2, s1 + s2 + s3


def _cp(sem):
    return pltpu.CompilerParams(dimension_semantics=sem, vmem_limit_bytes=VMEM_LIMIT)


def _rows(tm, n):
    return pl.BlockSpec((tm, n), lambda i: (i, 0))


def _full(a):
    nd = len(a.shape)
    return pl.BlockSpec(a.shape, lambda *_: (0,) * nd)


def _sds(shape, dtype=F32):
    return jax.ShapeDtypeStruct(shape, dtype)


def _acc(ref, val, first):
    @pl.when(first)
    def _():
        ref[...] = val

    @pl.when(jnp.logical_not(first))
    def _():
        ref[...] = ref[...] + val


def _f32(ref):
    return ref[...].astype(F32)


def inproj_fwd(x, g1, wa, wb, wc):
    s, d = x.shape

    def body(x_ref, g_ref, wa_ref, wb_ref, wc_ref, pa_ref, pb_ref, pc_ref):
        pa, pb, pc = _inproj(PLAIN, x_ref[...], g_ref[...], wa_ref[...], wb_ref[...], wc_ref[...])
        pa_ref[...] = pa
        pb_ref[...] = pb
        pc_ref[...] = pc

    return pl.pallas_call(
        body, name="inproj_fwd", grid=(s // TM,),
        in_specs=[_rows(TM, d), _full(g1), _full(wa), _full(wb), _full(wc)],
        out_specs=[_rows(TM, wa.shape[1]), _rows(TM, wb.shape[1]), _rows(TM, wc.shape[1])],
        out_shape=[_sds((s, wa.shape[1])), _sds((s, wb.shape[1])), _sds((s, wc.shape[1]))],
        compiler_params=_cp(("parallel",)),
    )(x, g1, wa, wb, wc)


def inproj_bwd(x, g1, wa, wb, wc, dpa, dpb, dpc, dres):
    s, d = x.shape

    def body(x_ref, g_ref, wa_ref, wb_ref, wc_ref, dpa_ref, dpb_ref, dpc_ref, dres_ref,
             dx_ref, dg_ref, dwa_ref, dwb_ref, dwc_ref):
        first = pl.program_id(0) == 0
        _, vjp = jax.vjp(functools.partial(_inproj, AD), x_ref[...], g_ref[...],
                         _f32(wa_ref), _f32(wb_ref), _f32(wc_ref))
        dx, dg, dwa, dwb, dwc = vjp((dpa_ref[...], dpb_ref[...], dpc_ref[...]))
        dx_ref[...] = dx + dres_ref[...]
        _acc(dg_ref, dg, first)
        _acc(dwa_ref, dwa, first)
        _acc(dwb_ref, dwb, first)
        _acc(dwc_ref, dwc, first)

    return pl.pallas_call(
        body, name="inproj_bwd", grid=(s // TM,),
        in_specs=[_rows(TM, d), _full(g1), _full(wa), _full(wb), _full(wc),
                  _rows(TM, wa.shape[1]), _rows(TM, wb.shape[1]), _rows(TM, wc.shape[1]), _rows(TM, d)],
        out_specs=[_rows(TM, d), _full(g1), _full(wa), _full(wb), _full(wc)],
        out_shape=[_sds((s, d)), _sds(g1.shape), _sds(wa.shape), _sds(wb.shape), _sds(wc.shape)],
        compiler_params=_cp(("arbitrary",)),
    )(x, g1, wa, wb, wc, dpa, dpb, dpc, dres)


def gm_fwd(pa, vg, ws4, bs, og):
    s = pa.shape[0]
    w = pa.shape[1] // 2

    def body(pa_ref, vg_ref, ws_ref, bs_ref, og_ref, ya_ref):
        bsl = [bs_ref[h] for h in range(4)]
        ya_ref[...] = _gm_chunk(PLAIN, pa_ref[:, 0:w], pa_ref[:, w:2 * w], vg_ref[...], ws_ref[...], bsl, og_ref[...])

    return pl.pallas_call(
        body, name="gm_fwd", grid=(s // CHUNK,),
        in_specs=[_rows(CHUNK, 2 * w), _full(vg), _full(ws4), _full(bs), _full(og)],
        out_specs=_rows(CHUNK, w), out_shape=_sds((s, w)),
        compiler_params=_cp(("parallel",)),
    )(pa, vg, ws4, bs, og)


def gm_bwd(pa, vg, ws4, bs, og, dya):
    s = pa.shape[0]
    w = pa.shape[1] // 2

    def body(pa_ref, vg_ref, ws_ref, bs_ref, og_ref, dya_ref, dpa_ref, dvg_ref, dws_ref, dbs_ref, dog_ref):
        first = pl.program_id(0) == 0
        bsl = [bs_ref[h] for h in range(4)]
        _, vjp = jax.vjp(functools.partial(_gm_chunk, AD), pa_ref[:, 0:w], pa_ref[:, w:2 * w],
                         vg_ref[...], ws_ref[...], bsl, og_ref[...])
        du, dv, dvg, dws, dbs, dog = vjp(dya_ref[...])
        dpa_ref[:, 0:w] = du
        dpa_ref[:, w:2 * w] = dv
        _acc(dvg_ref, dvg, first)
        _acc(dws_ref, dws, first)
        _acc(dog_ref, dog, first)
        for h in range(4):
            _acc(dbs_ref.at[h], dbs[h], first)

    return pl.pallas_call(
        body, name="gm_bwd", grid=(s // CHUNK,),
        in_specs=[_rows(CHUNK, 2 * w), _full(vg), _full(ws4), _full(bs), _full(og), _rows(CHUNK, w)],
        out_specs=[_rows(CHUNK, 2 * w), _full(vg), _full(ws4), _full(bs), _full(og)],
        out_shape=[_sds((s, 2 * w)), _sds(vg.shape), _sds(ws4.shape), _sds(bs.shape), _sds(og.shape)],
        compiler_params=_cp(("arbitrary",)),
    )(pa, vg, ws4, bs, og, dya)


def hg_fwd(pb, lb, og):
    s = pb.shape[0]
    w = pb.shape[1] // 4
    nc = s // CHUNK

    def body(pb_ref, lb_ref, og_ref, yb_ref, states_ref, st_ref):
        @pl.when(pl.program_id(0) == 0)
        def _():
            st_ref[...] = jnp.zeros_like(st_ref)

        st = st_ref[...]
        states_ref[...] = st
        st_new, y = _hg_chunk(PLAIN, st, pb_ref[:, 0:w], pb_ref[:, w:2 * w], pb_ref[:, 2 * w:3 * w],
                              pb_ref[:, 3 * w:4 * w], lb_ref[...], og_ref[...])
        st_ref[...] = st_new
        yb_ref[...] = y

    return pl.pallas_call(
        body, name="hg_fwd", grid=(nc,),
        in_specs=[_rows(CHUNK, 4 * w), _full(lb), _full(og)],
        out_specs=[_rows(CHUNK, w), pl.BlockSpec((None, w, w), lambda i: (i, 0, 0))],
        out_shape=[_sds((s, w)), _sds((nc, w, w))],
        scratch_shapes=[pltpu.VMEM((w, w), F32)],
        compiler_params=_cp(("arbitrary",)),
    )(pb, lb, og)


def hg_bwd(pb, lb, og, states, dyb):
    s = pb.shape[0]
    w = pb.shape[1] // 4
    nc = s // CHUNK

    def body(pb_ref, lb_ref, og_ref, states_ref, dyb_ref, dpb_ref, dlb_ref, dog_ref, dst_ref):
        first = pl.program_id(0) == 0

        @pl.when(first)
        def _():
            dst_ref[...] = jnp.zeros_like(dst_ref)

        _, vjp = jax.vjp(functools.partial(_hg_chunk, AD), states_ref[...], pb_ref[:, 0:w], pb_ref[:, w:2 * w],
                         pb_ref[:, 2 * w:3 * w], pb_ref[:, 3 * w:4 * w], lb_ref[...], og_ref[...])
        dst, dq, df, di, dg, dlb, dog = vjp((dst_ref[...], dyb_ref[...]))
        dst_ref[...] = dst
        dpb_ref[:, 0:w] = dq
        dpb_ref[:, w:2 * w] = df
        dpb_ref[:, 2 * w:3 * w] = di
        dpb_ref[:, 3 * w:4 * w] = dg
        _acc(dlb_ref, dlb, first)
        _acc(dog_ref, dog, first)

    rev = lambda i: (nc - 1 - i, 0)
    return pl.pallas_call(
        body, name="hg_bwd", grid=(nc,),
        in_specs=[pl.BlockSpec((CHUNK, 4 * w), rev), _full(lb), _full(og),
                  pl.BlockSpec((None, w, w), lambda i: (nc - 1 - i, 0, 0)), pl.BlockSpec((CHUNK, w), rev)],
        out_specs=[pl.BlockSpec((CHUNK, 4 * w), rev), _full(lb), _full(og)],
        out_shape=[_sds((s, 4 * w)), _sds(lb.shape), _sds(og.shape)],
        scratch_shapes=[pltpu.VMEM((w, w), F32)],
        compiler_params=_cp(("arbitrary",)),
    )(pb, lb, og, states, dyb)


def lower_bounds_fwd(hlb):
    def body(h_ref, o_ref):
        outs = _lower_bounds(*[h_ref[pl.ds(i, 1), :] for i in range(DEPTH)])
        for i in range(DEPTH):
            o_ref[pl.ds(i, 1), :] = outs[i]

    return pl.pallas_call(body, name="lower_bounds_fwd", out_shape=_sds(hlb.shape))(hlb)


def lower_bounds_bwd(hlb, dlbs):
    def body(h_ref, d_ref, o_ref):
        _, vjp = jax.vjp(_lower_bounds, *[h_ref[pl.ds(i, 1), :] for i in range(DEPTH)])
        outs = vjp(tuple(d_ref[pl.ds(i, 1), :] for i in range(DEPTH)))
        for i in range(DEPTH):
            o_ref[pl.ds(i, 1), :] = outs[i]

    return pl.pallas_call(body, name="lower_bounds_bwd", out_shape=_sds(hlb.shape))(hlb, dlbs)


def _mla_pre_args(pc_ref, cos_ref, sa_ref, sb_ref, qag_ref, kvag_ref, qg_ref, kg_ref, wq_ref, wk_ref, wv_ref, cast):
    sl = lambda h: slice(h * SLOT, (h + 1) * SLOT)
    ld = (lambda r, h: r[:, sl(h)].astype(F32)) if cast else (lambda r, h: r[:, sl(h)])
    diff = (pc_ref[:, 0:Q_LORA], pc_ref[:, Q_LORA:Q_LORA + KV_LORA], pc_ref[:, Q_LORA + KV_LORA:Q_LORA + 2 * KV_LORA],
            qag_ref[...], kvag_ref[...], qg_ref[...], kg_ref[...],
            [ld(wq_ref, h) for h in range(MLA_HEADS)], [ld(wk_ref, h) for h in range(MLA_HEADS)],
            [ld(wv_ref, h) for h in range(MLA_HEADS)])
    tables = (cos_ref[...], sa_ref[...], sb_ref[...])
    return diff, tables


def _mla_pre_fn(ops, tables, cq, ckv, kpe, qag, kvag, qg, kg, wq, wk, wv):
    return _mla_pre(ops, cq, ckv, kpe, *tables, qag, kvag, qg, kg, wq, wk, wv)


def mla_pre_fwd(pc, cos_t, sin_a, sin_b, qag, kvag, qg, kg, wq, wk, wv):
    s = pc.shape[0]
    hw = MLA_HEADS * SLOT

    def body(pc_ref, cos_ref, sa_ref, sb_ref, qag_ref, kvag_ref, qg_ref, kg_ref, wq_ref, wk_ref, wv_ref,
             q_ref, k_ref, v_ref):
        diff, tables = _mla_pre_args(pc_ref, cos_ref, sa_ref, sb_ref, qag_ref, kvag_ref, qg_ref, kg_ref,
                                     wq_ref, wk_ref, wv_ref, False)
        qs, ks, vs = _mla_pre_fn(PLAIN, tables, *diff)
        ones_lane = (lax.broadcasted_iota(jnp.int32, (1, SLOT), 1) == V_DIM).astype(F32)
        for h in range(MLA_HEADS):
            q_ref[:, h * SLOT:(h + 1) * SLOT] = qs[h].astype(q_ref.dtype)
            k_ref[:, h * SLOT:(h + 1) * SLOT] = ks[h].astype(k_ref.dtype)
            v_ref[:, h * SLOT:(h + 1) * SLOT] = (vs[h] + ones_lane).astype(v_ref.dtype)

    return pl.pallas_call(
        body, name="mla_pre_fwd", grid=(s // TM,),
        in_specs=[_rows(TM, pc.shape[1]), _rows(TM, SLOT), _rows(TM, SLOT), _rows(TM, SLOT),
                  _full(qag), _full(kvag), _full(qg), _full(kg), _full(wq), _full(wk), _full(wv)],
        out_specs=[_rows(TM, hw)] * 3, out_shape=[_sds((s, hw), MXU_DTYPE)] * 3,
        compiler_params=_cp(("parallel",)),
    )(pc, cos_t, sin_a, sin_b, qag, kvag, qg, kg, wq, wk, wv)


def mla_pre_bwd(pc, cos_t, sin_a, sin_b, qag, kvag, qg, kg, wq, wk, wv, dq, dk, dv):
    s = pc.shape[0]
    hw = MLA_HEADS * SLOT

    def body(pc_ref, cos_ref, sa_ref, sb_ref, qag_ref, kvag_ref, qg_ref, kg_ref, wq_ref, wk_ref, wv_ref,
             dq_ref, dk_ref, dv_ref, dpc_ref, dqag_ref, dkvag_ref, dqg_ref, dkg_ref, dwq_ref, dwk_ref, dwv_ref):
        first = pl.program_id(0) == 0
        diff, tables = _mla_pre_args(pc_ref, cos_ref, sa_ref, sb_ref, qag_ref, kvag_ref, qg_ref, kg_ref,
                                     wq_ref, wk_ref, wv_ref, True)
        _, vjp = jax.vjp(functools.partial(_mla_pre_fn, AD, tables), *diff)
        sl = lambda h: slice(h * SLOT, (h + 1) * SLOT)
        cot = ([dq_ref[:, sl(h)] for h in range(MLA_HEADS)], [dk_ref[:, sl(h)] for h in range(MLA_HEADS)],
               [dv_ref[:, sl(h)] for h in range(MLA_HEADS)])
        dcq, dckv, dkpe, dqag, dkvag, dqg, dkg, dwq, dwk, dwv = vjp(cot)
        dpc_ref[:, 0:Q_LORA] = dcq
        dpc_ref[:, Q_LORA:Q_LORA + KV_LORA] = dckv
        dpc_ref[:, Q_LORA + KV_LORA:Q_LORA + 2 * KV_LORA] = dkpe
        _acc(dqag_ref, dqag, first)
        _acc(dkvag_ref, dkvag, first)
        _acc(dqg_ref, dqg, first)
        _acc(dkg_ref, dkg, first)
        for h in range(MLA_HEADS):
            _acc(dwq_ref.at[:, sl(h)], dwq[h], first)
            _acc(dwk_ref.at[:, sl(h)], dwk[h], first)
            _acc(dwv_ref.at[:, sl(h)], dwv[h], first)

    return pl.pallas_call(
        body, name="mla_pre_bwd", grid=(s // TM,),
        in_specs=[_rows(TM, pc.shape[1]), _rows(TM, SLOT), _rows(TM, SLOT), _rows(TM, SLOT),
                  _full(qag), _full(kvag), _full(qg), _full(kg), _full(wq), _full(wk), _full(wv),
                  _rows(TM, hw), _rows(TM, hw), _rows(TM, hw)],
        out_specs=[_rows(TM, pc.shape[1]), _full(qag), _full(kvag), _full(qg), _full(kg),
                   _full(wq), _full(wk), _full(wv)],
        out_shape=[_sds(pc.shape), _sds(qag.shape), _sds(kvag.shape), _sds(qg.shape), _sds(kg.shape),
                   _sds(wq.shape), _sds(wk.shape), _sds(wv.shape)],
        compiler_params=_cp(("arbitrary",)),
    )(pc, cos_t, sin_a, sin_b, qag, kvag, qg, kg, wq, wk, wv, dq, dk, dv)


ATT_SCALE = QK_DIM ** -0.5
NEG_BIG = -1e30


def attn_fwd(q, k, v):
    s = q.shape[0]
    nq = s // TQ

    def body(q_ref, k_ref, v_ref, o_ref, lse_ref):
        row = lax.broadcasted_iota(jnp.int32, (TQ, TQ), 0)
        col = lax.broadcasted_iota(jnp.int32, (TQ, TQ), 1)
        lane = lax.broadcasted_iota(jnp.int32, (1, SLOT), 1)

        def q_loop(qi, c):
            rq = pl.ds(pl.multiple_of(qi * TQ, TQ), TQ)
            qq = q_ref[rq, :]

            def step(ki, carry, masked):
                m, acc = carry
                rk = pl.ds(pl.multiple_of(ki * TQ, TQ), TQ)
                sc = _dot(qq, k_ref[rk, :], NT) * ATT_SCALE
                if masked:
                    sc = jnp.where(row >= col, sc, NEG_BIG)
                m_new = jnp.maximum(m, jnp.max(sc, axis=-1, keepdims=True))
                acc = jnp.exp(m - m_new) * acc + _dot(jnp.exp(sc - m_new), v_ref[rk, :], NN)
                return m_new, acc

            init = (jnp.full((TQ, 1), NEG_BIG, F32), jnp.zeros((TQ, SLOT), F32))
            carry = lax.fori_loop(0, qi, lambda ki, cr: step(ki, cr, False), init)
            m, acc = step(qi, carry, True)
            l = jnp.sum(jnp.where(lane == V_DIM, acc, 0.0), axis=-1, keepdims=True)
            o_ref[rq, :] = jnp.where(lane < V_DIM, acc / l, 0.0)
            lse_ref[rq, :] = m + jnp.log(l)
            return c

        lax.fori_loop(0, nq, q_loop, 0)

    head_col = pl.BlockSpec((s, SLOT), lambda h: (0, h))
    return pl.pallas_call(
        body, name="attn_fwd", grid=(MLA_HEADS,),
        in_specs=[head_col, head_col, head_col],
        out_specs=[head_col, pl.BlockSpec((None, s, 1), lambda h: (h, 0, 0))],
        out_shape=[_sds((s, MLA_HEADS * SLOT)), _sds((MLA_HEADS, s, 1))],
        compiler_params=_cp(("parallel",)),
    )(q, k, v)


def attn_bwd(q, k, v, o, do, lse):
    s = q.shape[0]
    nq = s // TQ

    def body(q_ref, k_ref, v_ref, o_ref, do_ref, lse_ref, dq_ref, dk_ref, dv_ref, delta_ref):
        row = lax.broadcasted_iota(jnp.int32, (TQ, TQ), 0)
        col = lax.broadcasted_iota(jnp.int32, (TQ, TQ), 1)

        def prep(i, c):
            rows = pl.ds(pl.multiple_of(i * TQ, TQ), TQ)
            delta_ref[rows, :] = jnp.sum(do_ref[rows, :] * o_ref[rows, :], axis=-1, keepdims=True)
            dq_ref[rows, :] = jnp.zeros((TQ, SLOT), F32)
            return c

        lax.fori_loop(0, nq, prep, 0)

        def k_loop(ki, c):
            rk = pl.ds(pl.multiple_of(ki * TQ, TQ), TQ)
            kk = k_ref[rk, :]
            vv = v_ref[rk, :]

            def step(qi, carry, masked):
                dk, dv = carry
                rq = pl.ds(pl.multiple_of(qi * TQ, TQ), TQ)
                qq = q_ref[rq, :]
                dd = do_ref[rq, :]
                sc = _dot(qq, kk, NT) * ATT_SCALE
                if masked:
                    sc = jnp.where(row >= col, sc, NEG_BIG)
                p = jnp.exp(sc - lse_ref[rq, :])
                dv = dv + _dot(p, dd, TN)
                ds = p * (_dot(dd, vv, NT) - delta_ref[rq, :]) * ATT_SCALE
                dk = dk + _dot(ds, qq, TN)
                dq_ref[rq, :] = dq_ref[rq, :] + _dot(ds, kk, NN)
                return dk, dv

            carry = step(ki, (jnp.zeros((TQ, SLOT), F32), jnp.zeros((TQ, SLOT), F32)), True)
            dk, dv = lax.fori_loop(ki + 1, nq, lambda qi, cr: step(qi, cr, False), carry)
            dk_ref[rk, :] = dk
            dv_ref[rk, :] = dv
            return c

        lax.fori_loop(0, nq, k_loop, 0)

    head_col = pl.BlockSpec((s, SLOT), lambda h: (0, h))
    return pl.pallas_call(
        body, name="attn_bwd", grid=(MLA_HEADS,),
        in_specs=[head_col] * 5 + [pl.BlockSpec((None, s, 1), lambda h: (h, 0, 0))],
        out_specs=[head_col] * 3,
        out_shape=[_sds((s, MLA_HEADS * SLOT))] * 3,
        scratch_shapes=[pltpu.VMEM((s, 1), F32)],
        compiler_params=_cp(("parallel",)),
    )(q, k, v, o, do, lse)


def _outproj_args(ya_ref, yb_ref, o_ref, mog_ref, woa_ref, wob_ref, woc_ref, cast):
    sl = lambda h: slice(h * SLOT, (h + 1) * SLOT)
    ldw = (lambda r: r[...].astype(F32)) if cast else (lambda r: r[...])
    ldc = (lambda h: woc_ref[sl(h), :].astype(F32)) if cast else (lambda h: woc_ref[sl(h), :])
    return (ya_ref[...], yb_ref[...], [o_ref[:, sl(h)] for h in range(MLA_HEADS)],
            [mog_ref[:, sl(h)] for h in range(MLA_HEADS)], ldw(woa_ref), ldw(wob_ref),
            [ldc(h) for h in range(MLA_HEADS)])


def outproj_fwd(x, ya, yb, o, mog, woa, wob, woc):
    s, d = x.shape

    def body(x_ref, ya_ref, yb_ref, o_ref, mog_ref, woa_ref, wob_ref, woc_ref, x1_ref):
        x1_ref[...] = _outproj(PLAIN, x_ref[...], *_outproj_args(ya_ref, yb_ref, o_ref, mog_ref, woa_ref, wob_ref,
                                                                  woc_ref, False))

    return pl.pallas_call(
        body, name="outproj_fwd", grid=(s // TM,),
        in_specs=[_rows(TM, d), _rows(TM, ya.shape[1]), _rows(TM, yb.shape[1]), _rows(TM, o.shape[1]),
                  _full(mog), _full(woa), _full(wob), _full(woc)],
        out_specs=_rows(TM, d), out_shape=_sds((s, d)),
        compiler_params=_cp(("parallel",)),
    )(x, ya, yb, o, mog, woa, wob, woc)


def outproj_bwd(ya, yb, o, mog, woa, wob, woc, dx2, dx1p):
    s, d = dx2.shape
    npart = dx1p.shape[0]

    def body(ya_ref, yb_ref, o_ref, mog_ref, woa_ref, wob_ref, woc_ref, dx2_ref, dx1p_ref,
             dx1_ref, dya_ref, dyb_ref, do_ref, dmog_ref, dwoa_ref, dwob_ref, dwoc_ref):
        first = pl.program_id(0) == 0
        sl = lambda h: slice(h * SLOT, (h + 1) * SLOT)
        dx1 = dx2_ref[...]
        for p in range(npart):
            dx1 = dx1 + dx1p_ref[p]
        dx1_ref[...] = dx1
        args = _outproj_args(ya_ref, yb_ref, o_ref, mog_ref, woa_ref, wob_ref, woc_ref, True)
        _, vjp = jax.vjp(lambda *a: _outproj(AD, jnp.zeros_like(dx1), *a), *args)
        dya, dyb, do, dmog, dwoa, dwob, dwoc = vjp(dx1)
        dya_ref[...] = dya
        dyb_ref[...] = dyb
        _acc(dwoa_ref, dwoa, first)
        _acc(dwob_ref, dwob, first)
        for h in range(MLA_HEADS):
            do_ref[:, sl(h)] = do[h]
            _acc(dmog_ref.at[:, sl(h)], dmog[h], first)
            _acc(dwoc_ref.at[sl(h), :], dwoc[h], first)

    return pl.pallas_call(
        body, name="outproj_bwd", grid=(s // TM,),
        in_specs=[_rows(TM, ya.shape[1]), _rows(TM, yb.shape[1]), _rows(TM, o.shape[1]),
                  _full(mog), _full(woa), _full(wob), _full(woc), _rows(TM, d),
                  pl.BlockSpec((npart, TM, d), lambda i: (0, i, 0))],
        out_specs=[_rows(TM, d), _rows(TM, ya.shape[1]), _rows(TM, yb.shape[1]), _rows(TM, o.shape[1]),
                   _full(mog), _full(woa), _full(wob), _full(woc)],
        out_shape=[_sds((s, d)), _sds(ya.shape), _sds(yb.shape), _sds(o.shape),
                   _sds(mog.shape), _sds(woa.shape), _sds(wob.shape), _sds(woc.shape)],
        compiler_params=_cp(("arbitrary",)),
    )(ya, yb, o, mog, woa, wob, woc, dx2, dx1p)


def ffn_fwd(x1, g2, w1, w2):
    s, d = x1.shape
    npart, _, fs = w1.shape

    def body(x1_ref, g_ref, w1_ref, w2_ref, x2_ref):
        p = pl.program_id(1)
        x1v = x1_ref[...]
        part = _ffn_part(PLAIN, x1v, g_ref[...], w1_ref[...], w2_ref[...])

        @pl.when(p == 0)
        def _():
            x2_ref[...] = x1v + part

        @pl.when(p != 0)
        def _():
            x2_ref[...] = x2_ref[...] + part

    return pl.pallas_call(
        body, name="ffn_fwd", grid=(s // TM, npart),
        in_specs=[pl.BlockSpec((TM, d), lambda i, p: (i, 0)), pl.BlockSpec(g2.shape, lambda i, p: (0, 0)),
                  pl.BlockSpec((None, d, fs), lambda i, p: (p, 0, 0)), pl.BlockSpec((None, fs, d), lambda i, p: (p, 0, 0))],
        out_specs=pl.BlockSpec((TM, d), lambda i, p: (i, 0)), out_shape=_sds((s, d)),
        compiler_params=_cp(("parallel", "arbitrary")),
    )(x1, g2, w1, w2)


def ffn_bwd(x1, g2, w1, w2, dx2):
    s, d = x1.shape
    npart, _, fs = w1.shape

    def body(x1_ref, g_ref, w1_ref, w2_ref, dx2_ref, dx1p_ref, dg_ref, dw1_ref, dw2_ref):
        p = pl.program_id(0)
        i = pl.program_id(1)
        _, vjp = jax.vjp(functools.partial(_ffn_part, AD), x1_ref[...], g_ref[...], _f32(w1_ref), _f32(w2_ref))
        dx1, dg, dw1, dw2 = vjp(dx2_ref[...])
        dx1p_ref[...] = dx1
        _acc(dg_ref, dg, (p == 0) & (i == 0))
        _acc(dw1_ref, dw1, i == 0)
        _acc(dw2_ref, dw2, i == 0)

    return pl.pallas_call(
        body, name="ffn_bwd", grid=(npart, s // TM),
        in_specs=[pl.BlockSpec((TM, d), lambda p, i: (i, 0)), pl.BlockSpec(g2.shape, lambda p, i: (0, 0)),
                  pl.BlockSpec((None, d, fs), lambda p, i: (p, 0, 0)), pl.BlockSpec((None, fs, d), lambda p, i: (p, 0, 0)),
                  pl.BlockSpec((TM, d), lambda p, i: (i, 0))],
        out_specs=[pl.BlockSpec((None, TM, d), lambda p, i: (p, i, 0)), pl.BlockSpec(g2.shape, lambda p, i: (0, 0)),
                   pl.BlockSpec((None, d, fs), lambda p, i: (p, 0, 0)), pl.BlockSpec((None, fs, d), lambda p, i: (p, 0, 0))],
        out_shape=[_sds((npart, s, d)), _sds(g2.shape), _sds(w1.shape), _sds(w2.shape)],
        compiler_params=_cp(("arbitrary", "arbitrary")),
    )(x1, g2, w1, w2, dx2)


def loss_head(y, target):
    s, d = y.shape

    def body(y_ref, t_ref, dy_ref, loss_ref):
        err = y_ref[...] - t_ref[...]
        dy_ref[...] = err * (1.0 / d)
        part = jnp.sum(jnp.sum(err * err, axis=-1, keepdims=True), axis=0, keepdims=True) * (0.5 / d)
        _acc(loss_ref, jnp.broadcast_to(part, loss_ref.shape), pl.program_id(0) == 0)

    return pl.pallas_call(
        body, name="loss_head", grid=(s // TM,),
        in_specs=[_rows(TM, d), _rows(TM, d)],
        out_specs=[_rows(TM, d), pl.BlockSpec((1, SLOT), lambda i: (0, 0))],
        out_shape=[_sds((s, d)), _sds((1, SLOT))],
        compiler_params=_cp(("arbitrary",)),
    )(y, target)


def _row_block(r):
    for b in (512, 256, 128, 64, 32, 16, 8):
        if r % b == 0:
            return b
    return r


def add_to(parts, out_dtype, name):
    r, c = parts[0].shape
    br = _row_block(r)
    n = len(parts)

    def body(*refs):
        acc = refs[0][...].astype(F32)
        for ref in refs[1:n]:
            acc = acc + ref[...].astype(F32)
        refs[n][...] = acc.astype(out_dtype)

    return pl.pallas_call(
        body, name=name, grid=(r // br,), in_specs=[_rows(br, c)] * n, out_specs=_rows(br, c),
        out_shape=_sds((r, c), out_dtype), compiler_params=_cp(("parallel",)),
    )(*parts)


def add_into_half(parts, half, name):
    r, c = parts[0].shape
    br = _row_block(r)
    nb = r // br
    n = len(parts)

    def body(half_ref, *refs):
        del half_ref
        acc = refs[0][...].astype(F32)
        for ref in refs[1:n]:
            acc = acc + ref[...].astype(F32)
        refs[n][...] = acc

    grid_spec = pltpu.PrefetchScalarGridSpec(
        num_scalar_prefetch=1, grid=(nb,),
        in_specs=[pl.BlockSpec((br, c), lambda i, h: (i, 0))] * n,
        out_specs=pl.BlockSpec((br, c), lambda i, h: (h[0] * nb + i, 0)))
    return pl.pallas_call(body, name=name, grid_spec=grid_spec, out_shape=_sds((2 * r, c)),
                          compiler_params=_cp(("parallel",)))(half.reshape(1).astype(jnp.int32), *parts)


def adamw(w, g, m, v, name):
    r, c = w.shape
    br = _row_block(r)
    c1 = 1.0 / (1.0 - ADAM_B1 ** ADAM_STEP)
    c2 = 1.0 / (1.0 - ADAM_B2 ** ADAM_STEP)

    def body(w_ref, g_ref, m_ref, v_ref, d_ref, nm_ref, nv_ref):
        gg = g_ref[...]
        nm = ADAM_B1 * m_ref[...] + (1.0 - ADAM_B1) * gg
        nv = ADAM_B2 * v_ref[...] + (1.0 - ADAM_B2) * (gg * gg)
        d_ref[...] = -ADAM_LR * ((nm * c1) / (jnp.sqrt(nv * c2) + ADAM_EPS) + ADAM_WD * w_ref[...])
        nm_ref[...] = nm
        nv_ref[...] = nv

    return pl.pallas_call(
        body, name=name, grid=(r // br,), in_specs=[_rows(br, c)] * 4, out_specs=[_rows(br, c)] * 3,
        out_shape=[_sds((r, c))] * 3, compiler_params=_cp(("parallel",)),
    )(w, g, m, v)


MESH = pl.DeviceIdType.MESH
ANY = pl.BlockSpec(memory_space=pl.ANY)


def _place():
    x, y, c = lax.axis_index("x"), lax.axis_index("y"), lax.axis_index("c")
    chips = [(1 - x, y), (x, 1 - y), (1 - x, 1 - y)]
    return x, y, c, chips


def gather_chips(arrs, name):
    n = len(arrs)
    me_chip = 2 * lax.axis_index("x") + lax.axis_index("y")
    bufs = [lax.dynamic_update_index_in_dim(lax.empty((N_CHIPS,) + a.shape, a.dtype), a, me_chip, 0) for a in arrs]

    def body(*refs):
        ins, outs = refs[:n], refs[2 * n:3 * n]
        send_sems, recv_sems = refs[3 * n:]
        x, y, c, chips = _place()
        me = 2 * x + y
        half = pl.ds(2 * c, 2)
        other = pl.ds(2 - 2 * c, 2)
        sibling = (x, y, 1 - c)

        def copy(i, k, src, dst, to):
            return pltpu.make_async_remote_copy(src_ref=src, dst_ref=dst, send_sem=send_sems.at[i, k],
                                                recv_sem=recv_sems.at[i, k], device_id=to, device_id_type=MESH)

        sent = []
        for i in range(n):
            for j, chip in enumerate(chips):
                cp = copy(i, j, ins[i].at[half], outs[i].at[me, half], (*chip, c))
                cp.start()
                sent.append(cp)
        for i in range(n):
            for j, (cx, cy) in enumerate(chips):
                blk = outs[i].at[2 * cx + cy, half]
                copy(i, j, blk, blk, (cx, cy, c)).wait_recv()
                cp = copy(i, 3 + j, blk, blk, sibling)
                cp.start()
                sent.append(cp)
        for i in range(n):
            for j, (cx, cy) in enumerate(chips):
                blk = outs[i].at[2 * cx + cy, other]
                copy(i, 3 + j, blk, blk, sibling).wait_recv()
        for cp in sent:
            cp.wait_send()

    return pl.pallas_call(
        body, name=name, in_specs=[ANY] * (2 * n), out_specs=[ANY] * n,
        out_shape=[_sds((N_CHIPS,) + a.shape, a.dtype) for a in arrs],
        input_output_aliases={n + i: i for i in range(n)},
        scratch_shapes=[pltpu.SemaphoreType.DMA((n, 6)), pltpu.SemaphoreType.DMA((n, 6))],
    )(*arrs, *bufs)


def swap_halves(arrs, name):
    n = len(arrs)

    def body(*refs):
        ins, outs = refs[:n], refs[n:2 * n]
        send_sems, recv_sems = refs[2 * n:]
        x, y, c, _ = _place()
        cps = [pltpu.make_async_remote_copy(src_ref=ins[i].at[1 - c], dst_ref=outs[i], send_sem=send_sems.at[i],
                                            recv_sem=recv_sems.at[i], device_id=(x, y, 1 - c), device_id_type=MESH)
               for i in range(n)]
        for cp in cps:
            cp.start()
        for cp in cps:
            cp.wait()

    return pl.pallas_call(
        body, name=name, in_specs=[ANY] * n, out_specs=[ANY] * n,
        out_shape=[_sds(a.shape[1:], a.dtype) for a in arrs],
        scratch_shapes=[pltpu.SemaphoreType.DMA((n,)), pltpu.SemaphoreType.DMA((n,))],
    )(*arrs)


def scatter_chips(arrs, name):
    n = len(arrs)

    def body(*refs):
        ins, outs = refs[:n], refs[n:2 * n]
        send_sems, recv_sems = refs[2 * n:]
        x, y, c, chips = _place()
        me = 2 * x + y
        cps = []
        for i in range(n):
            for j, (cx, cy) in enumerate(chips):
                for t in range(2):
                    cps.append(pltpu.make_async_remote_copy(
                        src_ref=ins[i].at[t, 2 * cx + cy], dst_ref=outs[i].at[j, t], send_sem=send_sems.at[i, j, t],
                        recv_sem=recv_sems.at[i, j, t], device_id=(cx, cy, c), device_id_type=MESH))
        for cp in cps:
            cp.start()
        for cp in cps:
            cp.wait()
        del me

    return pl.pallas_call(
        body, name=name, in_specs=[ANY] * n, out_specs=[ANY] * n,
        out_shape=[_sds((3, 2) + a.shape[2:], a.dtype) for a in arrs],
        scratch_shapes=[pltpu.SemaphoreType.DMA((n, 3, 2)), pltpu.SemaphoreType.DMA((n, 3, 2))],
    )(*arrs)


def share_halves(arrs, name):
    n = len(arrs)

    def body(*refs):
        ins, outs = refs[:n], refs[n:2 * n]
        send_sems, recv_sems = refs[2 * n:]
        x, y, c, _ = _place()
        half = pl.ds(2 * c, 2)
        cps = [pltpu.make_async_remote_copy(src_ref=ins[i].at[half], dst_ref=outs[i].at[half], send_sem=send_sems.at[i],
                                            recv_sem=recv_sems.at[i], device_id=(x, y, 1 - c), device_id_type=MESH)
               for i in range(n)]
        for cp in cps:
            cp.start()
        for cp in cps:
            cp.wait()

    return pl.pallas_call(
        body, name=name, in_specs=[ANY] * n, out_specs=[ANY] * n,
        out_shape=[_sds(a.shape, a.dtype) for a in arrs],
        input_output_aliases={i: i for i in range(n)},
        scratch_shapes=[pltpu.SemaphoreType.DMA((n,)), pltpu.SemaphoreType.DMA((n,))],
    )(*arrs)


def _pad_slots(a, live):
    lead = a.shape[:-1]
    a = a.reshape(lead + (MLA_HEADS, live))
    a = jnp.pad(a, [(0, 0)] * len(lead) + [(0, 0), (0, SLOT - live)])
    return a.reshape(lead + (MLA_HEADS * SLOT,))


def _unpad_slots(a, live):
    lead = a.shape[:-1]
    return a.reshape(lead + (MLA_HEADS, SLOT))[..., :live].reshape(lead + (MLA_HEADS * live,))


def _rope_tables(positions, s):
    half = QK_ROPE // 2
    inv_freq = ROPE_THETA ** (-jnp.arange(half, dtype=F32) / half)
    ang = positions.reshape(s).astype(F32)[:, None] * inv_freq[None, :]
    cos, sin = jnp.cos(ang), jnp.sin(ang)
    one = jnp.ones((s, QK_NOPE), F32)
    z64, z16, z32 = jnp.zeros((s, QK_NOPE), F32), jnp.zeros((s, half), F32), jnp.zeros((s, SLOT - QK_DIM), F32)
    cos_t = jnp.concatenate([one, cos, cos, z32], axis=1)
    sin_a = jnp.concatenate([z64, -sin, z16, z32], axis=1)
    sin_b = jnp.concatenate([z64, z16, sin, z32], axis=1)
    return cos_t, sin_a, sin_b


def _layer_weights(full, small, l):
    w_in = jnp.concatenate([full["w_in"][p, l] for p in range(N_CHIPS)], axis=1)
    wc = jnp.pad(w_in[:, 1536:], ((0, 0), (0, 512 - (w_in.shape[1] - 1536))))
    w_uq = jnp.concatenate([full["mla_w_uq"][p, l] for p in range(N_CHIPS)], axis=1)
    w_ukv = jnp.concatenate([full["mla_w_ukv"][p, l] for p in range(N_CHIPS)], axis=1)
    ukv = w_ukv.reshape(KV_LORA, MLA_HEADS, QK_NOPE + V_DIM)
    w_out = jnp.concatenate([full["w_out"][p, l] for p in range(N_CHIPS)], axis=0)
    woc = w_out[512:].reshape(MLA_HEADS, V_DIM, D_MODEL)
    woc = jnp.pad(woc, ((0, 0), (0, SLOT - V_DIM), (0, 0))).reshape(MLA_HEADS * SLOT, D_MODEL)
    row = lambda a: a.reshape(1, -1)
    return dict(
        g1=row(small["norm1_gain"][l]), wa=w_in[:, :512], wb=w_in[:, 512:1536], wc=wc,
        vg=row(small["gm_v_gain"][l]), ws=small["gm_w_s"][l], bs=small["gm_b_s"][l].reshape(4, CHUNK, 1),
        gog=row(small["gm_out_gain"][l]), hog=row(small["hg_out_gain"][l]),
        qag=row(small["mla_q_a_gain"][l]), kvag=row(small["mla_kv_a_gain"][l]),
        qg=row(jnp.pad(small["mla_q_gain"][l], (0, SLOT - QK_DIM))), kg=row(jnp.pad(small["mla_k_gain"][l], (0, SLOT - QK_DIM))),
        wq=_pad_slots(w_uq, QK_DIM), wk=_pad_slots(ukv[..., :QK_NOPE].reshape(KV_LORA, -1), QK_NOPE),
        wv=_pad_slots(ukv[..., QK_NOPE:].reshape(KV_LORA, -1), V_DIM),
        mog=row(_pad_slots(small["mla_out_gain"][l], V_DIM)),
        woa=w_out[:256], wob=w_out[256:512], woc=woc,
        g2=row(small["norm2_gain"][l]), w1=full["w_ff1"][:, l], w2=full["w_ff2"][:, l],
    )


def _shard_cols(a):
    r, c4 = a.shape
    return a.reshape(r, N_CHIPS, c4 // N_CHIPS).transpose(1, 0, 2)


def local_step(x, positions, target, full, small):
    s = x.shape[0]
    cos_t, sin_a, sin_b = _rope_tables(positions, s)
    lbs = lower_bounds_fwd(small["hg_lower_bound"])
    lw = [_layer_weights(full, small, l) for l in range(DEPTH)]
    saved = []
    for l in range(DEPTH):
        w = lw[l]
        lb = lbs[l].reshape(1, -1)
        pa, pb, pc = inproj_fwd(x, w["g1"], w["wa"], w["wb"], w["wc"])
        ya = gm_fwd(pa, w["vg"], w["ws"], w["bs"], w["gog"])
        yb, states = hg_fwd(pb, lb, w["hog"])
        q, k, v = mla_pre_fwd(pc, cos_t, sin_a, sin_b, w["qag"], w["kvag"], w["qg"], w["kg"], w["wq"], w["wk"], w["wv"])
        o, lse = attn_fwd(q, k, v)
        x1 = outproj_fwd(x, ya, yb, o, w["mog"], w["woa"], w["wob"], w["woc"])
        x2 = ffn_fwd(x1, w["g2"], w["w1"], w["w2"])
        saved.append(dict(x=x, pa=pa, pb=pb, pc=pc, ya=ya, yb=yb, states=states, q=q, k=k, v=v, o=o, lse=lse, x1=x1, lb=lb))
        x = x2
    dx, loss_part = loss_head(x, target)
    big = {n: [None] * DEPTH for n in ("w_in", "mla_w_uq", "mla_w_ukv", "w_out", "w_ff1", "w_ff2")}
    sm = {n: [None] * DEPTH for n in ("norm1_gain", "gm_v_gain", "gm_w_s", "gm_b_s", "gm_out_gain", "hg_out_gain",
                                       "mla_q_a_gain", "mla_kv_a_gain", "mla_q_gain", "mla_k_gain", "mla_out_gain",
                                       "norm2_gain")}
    dlbs = [None] * DEPTH
    for l in reversed(range(DEPTH)):
        w, a = lw[l], saved[l]
        dx1p, dg2, dw1, dw2 = ffn_bwd(a["x1"], w["g2"], w["w1"], w["w2"], dx)
        dx1, dya, dyb, do, dmog, dwoa, dwob, dwoc = outproj_bwd(a["ya"], a["yb"], a["o"], w["mog"], w["woa"], w["wob"],
                                                                 w["woc"], dx, dx1p)
        dq, dk, dv = attn_bwd(a["q"], a["k"], a["v"], a["o"], do, a["lse"])
        dpc, dqag, dkvag, dqg, dkg, dwq, dwk, dwv = mla_pre_bwd(a["pc"], cos_t, sin_a, sin_b, w["qag"], w["kvag"], w["qg"],
                                                                  w["kg"], w["wq"], w["wk"], w["wv"], dq, dk, dv)
        dpb, dlb, dhog = hg_bwd(a["pb"], a["lb"], w["hog"], a["states"], dyb)
        dpa, dvg, dws, dbs, dgog = gm_bwd(a["pa"], w["vg"], w["ws"], w["bs"], w["gog"], dya)
        dx, dg1, dwa, dwb, dwc = inproj_bwd(a["x"], w["g1"], w["wa"], w["wb"], w["wc"], dpa, dpb, dpc, dx1)
        big["w_in"][l] = _shard_cols(jnp.concatenate([dwa, dwb, dwc[:, :1952 - 1536]], axis=1))
        big["mla_w_uq"][l] = _shard_cols(_unpad_slots(dwq, QK_DIM))
        dukv = jnp.concatenate([dwk.reshape(KV_LORA, MLA_HEADS, SLOT)[..., :QK_NOPE],
                                dwv.reshape(KV_LORA, MLA_HEADS, SLOT)[..., :V_DIM]], axis=-1)
        big["mla_w_ukv"][l] = _shard_cols(dukv.reshape(KV_LORA, -1))
        dwo = jnp.concatenate([dwoa, dwob, dwoc.reshape(MLA_HEADS, SLOT, D_MODEL)[:, :V_DIM].reshape(-1, D_MODEL)], axis=0)
        big["w_out"][l] = dwo.reshape(N_CHIPS, -1, D_MODEL)
        big["w_ff1"][l] = dw1
        big["w_ff2"][l] = dw2
        sm["norm1_gain"][l] = dg1[0]
        sm["gm_v_gain"][l] = dvg[0]
        sm["gm_w_s"][l] = dws
        sm["gm_b_s"][l] = dbs[..., 0]
        sm["gm_out_gain"][l] = dgog[0]
        sm["hg_out_gain"][l] = dhog[0]
        sm["mla_q_a_gain"][l] = dqag[0]
        sm["mla_kv_a_gain"][l] = dkvag[0]
        sm["mla_q_gain"][l] = dqg[0, :QK_DIM]
        sm["mla_k_gain"][l] = dkg[0, :QK_DIM]
        sm["mla_out_gain"][l] = _unpad_slots(dmog[0], V_DIM)
        sm["norm2_gain"][l] = dg2[0]
        dlbs[l] = dlb[0]
    big = {n: jnp.stack(v) for n, v in big.items()}
    sm = {n: jnp.stack(v) for n, v in sm.items()}
    sm["hg_lower_bound"] = lower_bounds_bwd(small["hg_lower_bound"], jnp.stack(dlbs))
    return loss_part, dx, big, sm


BIG = ("w_in", "mla_w_uq", "mla_w_ukv", "w_out", "w_ff1", "w_ff2")
SMALL = ("norm1_gain", "gm_v_gain", "gm_w_s", "gm_b_s", "gm_out_gain", "hg_lower_bound", "hg_out_gain",
         "mla_q_a_gain", "mla_kv_a_gain", "mla_q_gain", "mla_k_gain", "mla_out_gain", "norm2_gain")
ORDER = ("norm1_gain", "w_in", "gm_v_gain", "gm_w_s", "gm_b_s", "gm_out_gain", "hg_lower_bound", "hg_out_gain",
         "mla_q_a_gain", "mla_w_uq", "mla_kv_a_gain", "mla_w_ukv", "mla_q_gain", "mla_k_gain", "mla_out_gain",
         "w_out", "norm2_gain", "w_ff1", "w_ff2")
PACK_ROWS = 144


def _pack(arrs):
    flat = jnp.concatenate([a.reshape(-1) for a in arrs])
    total = 16 * PACK_ROWS * SLOT
    return jnp.pad(flat, (0, total - flat.shape[0]))


def _unpack(flat, shapes):
    out, off = [], 0
    for sh in shapes:
        size = 1
        for d in sh:
            size *= d
        out.append(flat[off:off + size].reshape(sh))
        off += size
    return out


def reduce_big(grads):
    flat2 = lambda a: a.reshape(-1, a.shape[-1])
    c = lax.axis_index("c")
    me = 2 * lax.axis_index("x") + lax.axis_index("y")
    got = swap_halves(grads, "swap_halves")
    mine = [lax.dynamic_index_in_dim(g, c, 0, keepdims=False) for g in grads]
    wire = [add_to([flat2(a), flat2(b)], BF16, "sum_cores_wire").reshape(a.shape) for a, b in zip(mine, got)]
    own = []
    for a, b in zip(mine, got):
        a_me = lax.dynamic_index_in_dim(a, me, 1, keepdims=False)
        b_me = lax.dynamic_index_in_dim(b, me, 1, keepdims=False)
        own.append(add_to([flat2(a_me), flat2(b_me)], F32, "sum_cores_own").reshape(a_me.shape))
    recv = scatter_chips(wire, "scatter_chips")
    fulls = [add_into_half([flat2(o), flat2(r[0]), flat2(r[1]), flat2(r[2])], c, "sum_chips").reshape((4,) + o.shape[1:])
             for o, r in zip(own, recv)]
    return share_halves(fulls, "share_halves")


def kernel(x, positions, norm1_gain, w_in, gm_v_gain, gm_w_s, gm_b_s, gm_out_gain, hg_lower_bound, hg_out_gain, mla_q_a_gain, mla_w_uq, mla_kv_a_gain, mla_w_ukv, mla_q_gain, mla_k_gain, mla_out_gain, w_out, norm2_gain, w_ff1, w_ff2, loss_target, m_norm1_gain, m_w_in, m_gm_v_gain, m_gm_w_s, m_gm_b_s, m_gm_out_gain, m_hg_lower_bound, m_hg_out_gain, m_mla_q_a_gain, m_mla_w_uq, m_mla_kv_a_gain, m_mla_w_ukv, m_mla_q_gain, m_mla_k_gain, m_mla_out_gain, m_w_out, m_norm2_gain, m_w_ff1, m_w_ff2, v_norm1_gain, v_w_in, v_gm_v_gain, v_gm_w_s, v_gm_b_s, v_gm_out_gain, v_hg_lower_bound, v_hg_out_gain, v_mla_q_a_gain, v_mla_w_uq, v_mla_kv_a_gain, v_mla_w_ukv, v_mla_q_gain, v_mla_k_gain, v_mla_out_gain, v_w_out, v_norm2_gain, v_w_ff1, v_w_ff2):
    given = dict(locals())
    weights = {n: given[n] for n in ORDER}
    moms = {n: given["m_" + n] for n in ORDER}
    vars_ = {n: given["v_" + n] for n in ORDER}
    s, d = x.shape[1], x.shape[2]

    gathered = gather_chips([weights[n].astype(MXU_DTYPE) for n in BIG], "gather_weights")
    full = dict(zip(BIG, gathered))
    small = {n: weights[n] for n in SMALL}

    loss_part, dx, big_g, small_g = local_step(x.reshape(s, d), positions, loss_target.reshape(s, d), full, small)
    loss = lax.psum(loss_part[0, 0], ("x", "y", "c"))

    pack_g = _pack([small_g[n] for n in SMALL]).reshape(2, 2, N_CHIPS, PACK_ROWS, SLOT)
    wire_in = [big_g[n].reshape((2, 2) + big_g[n].shape[1:]) for n in BIG] + [pack_g]
    reduced = reduce_big(wire_in)
    grads = dict(zip(BIG, reduced[:-1]))
    pack_full = gather_chips([reduced[-1]], "gather_small")[0].transpose(1, 0, 2, 3).reshape(-1)
    grads.update(zip(SMALL, _unpack(pack_full, [weights[n].shape for n in SMALL])))

    delta, new_m, new_v = {}, {}, {}
    flat2 = lambda a: a.reshape(-1, a.shape[-1])
    for n in BIG:
        outs = adamw(flat2(weights[n]), flat2(grads[n]), flat2(moms[n]), flat2(vars_[n]), "adamw_" + n)
        delta[n], new_m[n], new_v[n] = [o.reshape(weights[n].shape) for o in outs]
    shapes = [weights[n].shape for n in SMALL]
    pk = lambda t: _pack([t[n] for n in SMALL]).reshape(-1, SLOT)
    outs = adamw(pk(weights), pack_full.reshape(-1, SLOT), pk(moms), pk(vars_), "adamw_small")
    for tgt, o in zip((delta, new_m, new_v), outs):
        tgt.update(zip(SMALL, _unpack(o.reshape(-1), shapes)))

    return (loss, dx.reshape(x.shape), *[grads[n] for n in ORDER], *[delta[n] for n in ORDER],
            *[new_m[n] for n in ORDER], *[new_v[n] for n in ORDER])
```

```python
import functools

import jax
import jax.numpy as jnp
from jax import lax
from jax.experimental import pallas as pl
from jax.experimental.pallas import tpu as pltpu

F32 = jnp.float32
BF16 = jnp.bfloat16
MXU_DTYPE = BF16

D_MODEL = 1024
DEPTH = 4
CHUNK = 128
EPS = 1e-6
HEAD64 = 64
MLA_HEADS = 8
QK_NOPE = 64
QK_ROPE = 32
QK_DIM = 96
V_DIM = 64
Q_LORA = 256
KV_LORA = 128
SLOT = 128
ROPE_THETA = 10000.0
D_FF_SHARD = 1024
N_CHIPS = 4

ADAM_LR = 0.001
ADAM_B1 = 0.9
ADAM_B2 = 0.999
ADAM_EPS = 1e-08
ADAM_WD = 0.01
ADAM_STEP = 10

TM = 256
TQ = 256
ATT_HEADS_PER_STEP = 2
VMEM_LIMIT = 56 * 1024 * 1024

NN = (((1,), (0,)), ((), ()))
NT = (((1,), (1,)), ((), ()))
TN = (((0,), (0,)), ((), ()))
BNN = (((2,), (1,)), ((0,), (0,)))
BNT = (((2,), (2,)), ((0,), (0,)))
BTN = (((1,), (1,)), ((0,), (0,)))


def _dot(a, b, dims):
    return lax.dot_general(a.astype(MXU_DTYPE), b.astype(MXU_DTYPE), dims, preferred_element_type=F32)


def _hdot(a, b, dims=NN):
    return lax.dot_general(a, b, dims, precision=lax.Precision.HIGHEST, preferred_element_type=F32)


def _make_ad(dims, da_dims, da_swap, db_dims, db_swap):
    @jax.custom_vjp
    def f(a, b):
        return _dot(a, b, dims)

    def fwd(a, b):
        return _dot(a, b, dims), (a, b)

    def bwd(res, g):
        a, b = res
        da = _dot(b, g, da_dims) if da_swap else _dot(g, b, da_dims)
        db = _dot(g, a, db_dims) if db_swap else _dot(a, g, db_dims)
        return da, db

    f.defvjp(fwd, bwd)
    return f


@functools.partial(jax.custom_vjp, nondiff_argnums=(1,))
def _roll_ad(x, shift):
    return pltpu.roll(x, shift, 1)


def _roll_ad_fwd(x, shift):
    return pltpu.roll(x, shift, 1), None


def _roll_ad_bwd(shift, _, g):
    return (pltpu.roll(g, (g.shape[1] - shift) % g.shape[1], 1),)


_roll_ad.defvjp(_roll_ad_fwd, _roll_ad_bwd)


class _Ops:
    pass


PLAIN = _Ops()
PLAIN.mm = lambda a, b: _dot(a, b, NN)
PLAIN.mm_nt = lambda a, b: _dot(a, b, NT)
PLAIN.mm_tn = lambda a, b: _dot(a, b, TN)
PLAIN.bmm = lambda a, b: _dot(a, b, BNN)
PLAIN.bmm_nt = lambda a, b: _dot(a, b, BNT)
PLAIN.roll = lambda x, s: pltpu.roll(x, s, 1)

AD = _Ops()
AD.mm = _make_ad(NN, NT, False, TN, False)
AD.mm_nt = _make_ad(NT, NN, False, TN, True)
AD.mm_tn = _make_ad(TN, NT, True, NN, False)
AD.bmm = _make_ad(BNN, BNT, False, BTN, False)
AD.bmm_nt = _make_ad(BNT, BNN, False, BTN, True)
AD.roll = _roll_ad


def _sigmoid(x):
    return jax.nn.sigmoid(x)


def _gelu(x):
    return 0.5 * x * (1.0 + jnp.tanh(0.7978845608028654 * (x + 0.044715 * (x * x * x))))


def _rms(x, g):
    return x * lax.rsqrt(jnp.mean(x * x, axis=-1, keepdims=True) + EPS) * g


def _head_masks256():
    lane = lax.broadcasted_iota(jnp.int32, (1, 4 * HEAD64), 1)
    return [(jnp.right_shift(lane, 6) == h).astype(F32) for h in range(4)]


def _headnorm256(x, g):
    ms = jnp.zeros_like(x)
    sq = x * x
    for m in _head_masks256():
        ms = ms + m * (jnp.sum(sq * m, axis=-1, keepdims=True) * (1.0 / HEAD64))
    return x * lax.rsqrt(ms + EPS) * g


def _slot_norm(x, g, n):
    return x * lax.rsqrt(jnp.sum(x * x, axis=-1, keepdims=True) * (1.0 / n) + EPS) * g


def _rope(ops, x, cos_t, sin_a, sin_b):
    return x * cos_t + ops.roll(x, SLOT - QK_ROPE // 2) * sin_a + ops.roll(x, QK_ROPE // 2) * sin_b


def _inproj(ops, x, g1, wa, wb, wc):
    h = _rms(x, g1)
    return ops.mm(h, wa), ops.mm(h, wb), ops.mm(h, wc)


def _gm_chunk(ops, ur, vr, vg, ws4, bs, og):
    c = ur.shape[0]
    masks = _head_masks256()
    mh = jnp.concatenate([m[None] for m in masks], axis=0)
    u = _gelu(ur)
    v = _headnorm256(_gelu(vr), vg)
    t = lax.broadcasted_iota(jnp.int32, (c, c), 0)
    s = lax.broadcasted_iota(jnp.int32, (c, c), 1)
    w = jnp.where((t >= s)[None], ws4, 0.0)
    y = jnp.sum(ops.bmm(w, v[None] * mh), axis=0)
    for h in range(4):
        y = y + bs[h] * masks[h]
    return _headnorm256(u * y, og)


def _hg_chunk(ops, st, qr, fr, ir, gr, lb, og):
    c = qr.shape[0]
    n = qr.shape[1]
    masks = _head_masks256()
    mh = jnp.concatenate([m[None] for m in masks], axis=0)
    q = qr * _sigmoid(qr)
    f = lb + (1.0 - lb) * _sigmoid(fr)
    k = 1.0 - f
    logf = jnp.log(f)
    t = lax.broadcasted_iota(jnp.int32, (c, c), 0)
    s = lax.broadcasted_iota(jnp.int32, (c, c), 1)
    tcol = lax.broadcasted_iota(jnp.int32, (c, 1), 0)
    b = _hdot((t >= s).astype(F32), logf)
    btot = jnp.sum(logf, axis=0, keepdims=True)
    inter = ops.mm_nt(q * jnp.exp(b), st)
    p4 = jnp.zeros((4, c, c), F32)
    lg = 6
    while lg >= 0:
        m = 1 << lg
        bnd = jnp.left_shift(jnp.right_shift(t, lg + 1), lg + 1) + (m - 1)
        r = _hdot((s == bnd).astype(F32), b)
        right = jnp.bitwise_and(jnp.right_shift(tcol, lg), 1) == 1
        qe = jnp.where(right, q * jnp.exp(jnp.where(right, b - r, 0.0)), 0.0)
        ke = jnp.where(right, 0.0, k * jnp.exp(jnp.where(right, 0.0, r - b)))
        lm = ((jnp.right_shift(t, lg + 1) == jnp.right_shift(s, lg + 1))
              & (jnp.bitwise_and(jnp.right_shift(t, lg), 1) == 1)
              & (jnp.bitwise_and(jnp.right_shift(s, lg), 1) == 0))
        s4 = ops.mm_nt((qe[None] * mh).reshape(4 * c, n), ke).reshape(4, c, c)
        p4 = p4 + lm.astype(F32)[None] * s4
        lg -= 1
    intra = jnp.sum(ops.bmm(p4, ir[None] * mh), axis=0)
    qk = q * k
    dsum = jnp.zeros_like(qk)
    for m_ in masks:
        dsum = dsum + m_ * jnp.sum(qk * m_, axis=-1, keepdims=True)
    o = inter + intra + dsum * ir
    kd = k * jnp.exp(btot - b)
    rr = lax.broadcasted_iota(jnp.int32, (n, n), 0)
    cc = lax.broadcasted_iota(jnp.int32, (n, n), 1)
    bd = (jnp.right_shift(rr, 6) == jnp.right_shift(cc, 6)).astype(F32)
    st_new = st * jnp.exp(btot) + bd * ops.mm_tn(ir, kd)
    y = _headnorm256(o, og) * (gr * _sigmoid(gr))
    return st_new, y


def _mla_pre(ops, cq, ckv, kpe, cos_t, sin_a, sin_b, qag, kvag, qg, kg, wq, wk, wv):
    cqn = _rms(cq, qag)
    ckvn = _rms(ckv, kvag)
    kper = ops.roll(kpe, QK_NOPE)
    qs, ks, vs = [], [], []
    for h in range(MLA_HEADS):
        qh = _slot_norm(ops.mm(cqn, wq[h]), qg, QK_DIM)
        qs.append(_rope(ops, qh, cos_t, sin_a, sin_b))
        kh = _slot_norm(ops.mm(ckvn, wk[h]) + kper, kg, QK_DIM)
        ks.append(_rope(ops, kh, cos_t, sin_a, sin_b))
        vs.append(ops.mm(ckvn, wv[h]))
    return qs, ks, vs


def _outproj(ops, x, ya, yb, o, mog, woa, wob, woc):
    acc = x + ops.mm(ya, woa) + ops.mm(yb, wob)
    for h in range(MLA_HEADS):
        acc = acc + ops.mm(_slot_norm(o[h], mog[h], V_DIM), woc[h])
    return acc


def _ffn_part(ops, x1, g2, w1p, w2p):
    a = ops.mm(_rms(x1, g2), w1p)
    r = jnp.maximum(a, 0.0)
    return ops.mm(r * r, w2p)


def _lower_bounds(r0, r1, r2, r3):
    mx = jnp.maximum(jnp.maximum(r0, r1), jnp.maximum(r2, r3))
    e0, e1, e2, e3 = jnp.exp(r0 - mx), jnp.exp(r1 - mx), jnp.exp(r2 - mx), jnp.exp(r3 - mx)
    inv = 1.0 / (e0 + e1 + e2 + e3)
    s1, s2, s3 = e1 * inv, e2 * inv, e3 * inv
    return jnp.zeros_like(r0), s1, s1 + s2, s1 + s2 + s3


def _cp(sem):
    return pltpu.CompilerParams(dimension_semantics=sem, vmem_limit_bytes=VMEM_LIMIT)


def _rows(tm, n):
    return pl.BlockSpec((tm, n), lambda i: (i, 0))


def _full(a):
    nd = len(a.shape)
    return pl.BlockSpec(a.shape, lambda *_: (0,) * nd)


def _sds(shape, dtype=F32):
    return jax.ShapeDtypeStruct(shape, dtype)


def _acc(ref, val, first):
    @pl.when(first)
    def _():
        ref[...] = val

    @pl.when(jnp.logical_not(first))
    def _():
        ref[...] = ref[...] + val


def _f32(ref):
    return ref[...].astype(F32)


MESH = pl.DeviceIdType.MESH
ANY = pl.BlockSpec(memory_space=pl.ANY)


class Rider:
    def __init__(self, arrays, out_shapes, aliases, sems, start, finish):
        self.arrays, self.out_shapes, self.aliases, self.sems = list(arrays), list(out_shapes), dict(aliases), list(sems)
        self.start, self.finish = start, finish


def run_rider(rider, name):
    n_in, n_out = len(rider.arrays), len(rider.out_shapes)

    def body(*refs):
        ins, outs, sems = refs[:n_in], refs[n_in:n_in + n_out], refs[n_in + n_out:]
        rider.start(ins, outs, sems)
        rider.finish(ins, outs, sems)

    return pl.pallas_call(
        body, name=name, in_specs=[ANY] * n_in, out_specs=[ANY] * n_out, out_shape=rider.out_shapes,
        input_output_aliases=rider.aliases, scratch_shapes=rider.sems,
    )(*rider.arrays)


def _ride(compute, rider, *, name, grid, in_specs, out_specs, out_shape, operands, scratch_shapes=(), sem=None):
    single = not isinstance(out_shape, (list, tuple))
    if single:
        out_specs, out_shape = [out_specs], [out_shape]
    if rider is None:
        res = pl.pallas_call(compute, name=name, grid=grid, in_specs=in_specs, out_specs=out_specs, out_shape=out_shape,
                             scratch_shapes=list(scratch_shapes), compiler_params=_cp(sem))(*operands)
        return (res[0] if single else res), None
    n_in, n_out, n_s = len(in_specs), len(out_specs), len(scratch_shapes)
    r_in, r_out = len(rider.arrays), len(rider.out_shapes)

    def body(*refs):
        ins, rins = refs[:n_in], refs[n_in:n_in + r_in]
        outs = refs[n_in + r_in:n_in + r_in + n_out]
        routs = refs[n_in + r_in + n_out:n_in + r_in + n_out + r_out]
        scr = refs[n_in + r_in + n_out + r_out:n_in + r_in + n_out + r_out + n_s]
        rsems = refs[n_in + r_in + n_out + r_out + n_s:]
        first = functools.reduce(jnp.logical_and, [pl.program_id(a) == 0 for a in range(len(grid))])
        last = functools.reduce(jnp.logical_and, [pl.program_id(a) == grid[a] - 1 for a in range(len(grid))])

        @pl.when(first)
        def _():
            rider.start(rins, routs, rsems)

        compute(*ins, *outs, *scr)

        @pl.when(last)
        def _():
            rider.finish(rins, routs, rsems)

    res = pl.pallas_call(
        body, name=name, grid=grid, in_specs=list(in_specs) + [ANY] * r_in, out_specs=list(out_specs) + [ANY] * r_out,
        out_shape=list(out_shape) + rider.out_shapes,
        input_output_aliases={n_in + k: n_out + v for k, v in rider.aliases.items()},
        scratch_shapes=list(scratch_shapes) + rider.sems, compiler_params=_cp(("arbitrary",) * len(grid)),
    )(*operands, *rider.arrays)
    main = res[:n_out]
    return (main[0] if single else main), res[n_out:]


def inproj_fwd(x, g1, wa, wb, wc):
    s, d = x.shape

    def body(x_ref, g_ref, wa_ref, wb_ref, wc_ref, pa_ref, pb_ref, pc_ref):
        pa, pb, pc = _inproj(PLAIN, x_ref[...], g_ref[...], wa_ref[...], wb_ref[...], wc_ref[...])
        pa_ref[...] = pa
        pb_ref[...] = pb
        pc_ref[...] = pc

    return pl.pallas_call(
        body, name="inproj_fwd", grid=(s // TM,),
        in_specs=[_rows(TM, d), _full(g1), _full(wa), _full(wb), _full(wc)],
        out_specs=[_rows(TM, wa.shape[1]), _rows(TM, wb.shape[1]), _rows(TM, wc.shape[1])],
        out_shape=[_sds((s, wa.shape[1])), _sds((s, wb.shape[1])), _sds((s, wc.shape[1]))],
        compiler_params=_cp(("parallel",)),
    )(x, g1, wa, wb, wc)


def inproj_bwd(x, g1, wa, wb, wc, dpa, dpb, dpc, dres):
    s, d = x.shape

    def body(x_ref, g_ref, wa_ref, wb_ref, wc_ref, dpa_ref, dpb_ref, dpc_ref, dres_ref,
             dx_ref, dg_ref, dwa_ref, dwb_ref, dwc_ref):
        first = pl.program_id(0) == 0
        _, vjp = jax.vjp(functools.partial(_inproj, AD), x_ref[...], g_ref[...],
                         _f32(wa_ref), _f32(wb_ref), _f32(wc_ref))
        dx, dg, dwa, dwb, dwc = vjp((dpa_ref[...], dpb_ref[...], dpc_ref[...]))
        dx_ref[...] = dx + dres_ref[...]
        _acc(dg_ref, dg, first)
        _acc(dwa_ref, dwa, first)
        _acc(dwb_ref, dwb, first)
        _acc(dwc_ref, dwc, first)

    return pl.pallas_call(
        body, name="inproj_bwd", grid=(s // TM,),
        in_specs=[_rows(TM, d), _full(g1), _full(wa), _full(wb), _full(wc),
                  _rows(TM, wa.shape[1]), _rows(TM, wb.shape[1]), _rows(TM, wc.shape[1]), _rows(TM, d)],
        out_specs=[_rows(TM, d), _full(g1), _full(wa), _full(wb), _full(wc)],
        out_shape=[_sds((s, d)), _sds(g1.shape), _sds(wa.shape), _sds(wb.shape), _sds(wc.shape)],
        compiler_params=_cp(("arbitrary",)),
    )(x, g1, wa, wb, wc, dpa, dpb, dpc, dres)


def gm_fwd(pa, vg, ws4, bs, og):
    s = pa.shape[0]
    w = pa.shape[1] // 2

    def body(pa_ref, vg_ref, ws_ref, bs_ref, og_ref, ya_ref):
        bsl = [bs_ref[h] for h in range(4)]
        ya_ref[...] = _gm_chunk(PLAIN, pa_ref[:, 0:w], pa_ref[:, w:2 * w], vg_ref[...], ws_ref[...], bsl, og_ref[...])

    return pl.pallas_call(
        body, name="gm_fwd", grid=(s // CHUNK,),
        in_specs=[_rows(CHUNK, 2 * w), _full(vg), _full(ws4), _full(bs), _full(og)],
        out_specs=_rows(CHUNK, w), out_shape=_sds((s, w)),
        compiler_params=_cp(("parallel",)),
    )(pa, vg, ws4, bs, og)


def gm_bwd(pa, vg, ws4, bs, og, dya):
    s = pa.shape[0]
    w = pa.shape[1] // 2

    def body(pa_ref, vg_ref, ws_ref, bs_ref, og_ref, dya_ref, dpa_ref, dvg_ref, dws_ref, dbs_ref, dog_ref):
        first = pl.program_id(0) == 0
        bsl = [bs_ref[h] for h in range(4)]
        _, vjp = jax.vjp(functools.partial(_gm_chunk, AD), pa_ref[:, 0:w], pa_ref[:, w:2 * w],
                         vg_ref[...], ws_ref[...], bsl, og_ref[...])
        du, dv, dvg, dws, dbs, dog = vjp(dya_ref[...])
        dpa_ref[:, 0:w] = du
        dpa_ref[:, w:2 * w] = dv
        _acc(dvg_ref, dvg, first)
        _acc(dws_ref, dws, first)
        _acc(dog_ref, dog, first)
        for h in range(4):
            _acc(dbs_ref.at[h], dbs[h], first)

    return pl.pallas_call(
        body, name="gm_bwd", grid=(s // CHUNK,),
        in_specs=[_rows(CHUNK, 2 * w), _full(vg), _full(ws4), _full(bs), _full(og), _rows(CHUNK, w)],
        out_specs=[_rows(CHUNK, 2 * w), _full(vg), _full(ws4), _full(bs), _full(og)],
        out_shape=[_sds((s, 2 * w)), _sds(vg.shape), _sds(ws4.shape), _sds(bs.shape), _sds(og.shape)],
        compiler_params=_cp(("arbitrary",)),
    )(pa, vg, ws4, bs, og, dya)


def hg_fwd(pb, lb, og):
    s = pb.shape[0]
    w = pb.shape[1] // 4
    nc = s // CHUNK

    def body(pb_ref, lb_ref, og_ref, yb_ref, states_ref, st_ref):
        @pl.when(pl.program_id(0) == 0)
        def _():
            st_ref[...] = jnp.zeros_like(st_ref)

        st = st_ref[...]
        states_ref[...] = st
        st_new, y = _hg_chunk(PLAIN, st, pb_ref[:, 0:w], pb_ref[:, w:2 * w], pb_ref[:, 2 * w:3 * w],
                              pb_ref[:, 3 * w:4 * w], lb_ref[...], og_ref[...])
        st_ref[...] = st_new
        yb_ref[...] = y

    return pl.pallas_call(
        body, name="hg_fwd", grid=(nc,),
        in_specs=[_rows(CHUNK, 4 * w), _full(lb), _full(og)],
        out_specs=[_rows(CHUNK, w), pl.BlockSpec((None, w, w), lambda i: (i, 0, 0))],
        out_shape=[_sds((s, w)), _sds((nc, w, w))],
        scratch_shapes=[pltpu.VMEM((w, w), F32)],
        compiler_params=_cp(("arbitrary",)),
    )(pb, lb, og)


def hg_bwd(pb, lb, og, states, dyb, rider=None):
    s = pb.shape[0]
    w = pb.shape[1] // 4
    nc = s // CHUNK

    def body(pb_ref, lb_ref, og_ref, states_ref, dyb_ref, dpb_ref, dlb_ref, dog_ref, dst_ref):
        first = pl.program_id(0) == 0

        @pl.when(first)
        def _():
            dst_ref[...] = jnp.zeros_like(dst_ref)

        _, vjp = jax.vjp(functools.partial(_hg_chunk, AD), states_ref[...], pb_ref[:, 0:w], pb_ref[:, w:2 * w],
                         pb_ref[:, 2 * w:3 * w], pb_ref[:, 3 * w:4 * w], lb_ref[...], og_ref[...])
        dst, dq, df, di, dg, dlb, dog = vjp((dst_ref[...], dyb_ref[...]))
        dst_ref[...] = dst
        dpb_ref[:, 0:w] = dq
        dpb_ref[:, w:2 * w] = df
        dpb_ref[:, 2 * w:3 * w] = di
        dpb_ref[:, 3 * w:4 * w] = dg
        _acc(dlb_ref, dlb, first)
        _acc(dog_ref, dog, first)

    rev = lambda i: (nc - 1 - i, 0)
    return _ride(
        body, rider, name="hg_bwd", grid=(nc,),
        in_specs=[pl.BlockSpec((CHUNK, 4 * w), rev), _full(lb), _full(og),
                  pl.BlockSpec((None, w, w), lambda i: (nc - 1 - i, 0, 0)), pl.BlockSpec((CHUNK, w), rev)],
        out_specs=[pl.BlockSpec((CHUNK, 4 * w), rev), _full(lb), _full(og)],
        out_shape=[_sds((s, 4 * w)), _sds(lb.shape), _sds(og.shape)],
        scratch_shapes=[pltpu.VMEM((w, w), F32)],
        operands=(pb, lb, og, states, dyb), sem=("arbitrary",))


def lower_bounds_fwd(hlb):
    def body(h_ref, o_ref):
        outs = _lower_bounds(*[h_ref[pl.ds(i, 1), :] for i in range(DEPTH)])
        for i in range(DEPTH):
            o_ref[pl.ds(i, 1), :] = outs[i]

    return pl.pallas_call(body, name="lower_bounds_fwd", out_shape=_sds(hlb.shape))(hlb)


def lower_bounds_bwd(hlb, dlbs):
    def body(h_ref, d_ref, o_ref):
        _, vjp = jax.vjp(_lower_bounds, *[h_ref[pl.ds(i, 1), :] for i in range(DEPTH)])
        outs = vjp(tuple(d_ref[pl.ds(i, 1), :] for i in range(DEPTH)))
        for i in range(DEPTH):
            o_ref[pl.ds(i, 1), :] = outs[i]

    return pl.pallas_call(body, name="lower_bounds_bwd", out_shape=_sds(hlb.shape))(hlb, dlbs)


def _mla_pre_args(pc_ref, cos_ref, sa_ref, sb_ref, qag_ref, kvag_ref, qg_ref, kg_ref, wq_ref, wk_ref, wv_ref, cast):
    sl = lambda h: slice(h * SLOT, (h + 1) * SLOT)
    ld = (lambda r, h: r[:, sl(h)].astype(F32)) if cast else (lambda r, h: r[:, sl(h)])
    diff = (pc_ref[:, 0:Q_LORA], pc_ref[:, Q_LORA:Q_LORA + KV_LORA], pc_ref[:, Q_LORA + KV_LORA:Q_LORA + 2 * KV_LORA],
            qag_ref[...], kvag_ref[...], qg_ref[...], kg_ref[...],
            [ld(wq_ref, h) for h in range(MLA_HEADS)], [ld(wk_ref, h) for h in range(MLA_HEADS)],
            [ld(wv_ref, h) for h in range(MLA_HEADS)])
    tables = (cos_ref[...], sa_ref[...], sb_ref[...])
    return diff, tables


def _mla_pre_fn(ops, tables, cq, ckv, kpe, qag, kvag, qg, kg, wq, wk, wv):
    return _mla_pre(ops, cq, ckv, kpe, *tables, qag, kvag, qg, kg, wq, wk, wv)


def mla_pre_fwd(pc, cos_t, sin_a, sin_b, qag, kvag, qg, kg, wq, wk, wv):
    s = pc.shape[0]
    hw = MLA_HEADS * SLOT

    def body(pc_ref, cos_ref, sa_ref, sb_ref, qag_ref, kvag_ref, qg_ref, kg_ref, wq_ref, wk_ref, wv_ref,
             q_ref, k_ref, v_ref):
        diff, tables = _mla_pre_args(pc_ref, cos_ref, sa_ref, sb_ref, qag_ref, kvag_ref, qg_ref, kg_ref,
                                     wq_ref, wk_ref, wv_ref, False)
        qs, ks, vs = _mla_pre_fn(PLAIN, tables, *diff)
        ones_lane = (lax.broadcasted_iota(jnp.int32, (1, SLOT), 1) == V_DIM).astype(F32)
        for h in range(MLA_HEADS):
            q_ref[:, h * SLOT:(h + 1) * SLOT] = qs[h].astype(q_ref.dtype)
            k_ref[:, h * SLOT:(h + 1) * SLOT] = ks[h].astype(k_ref.dtype)
            v_ref[:, h * SLOT:(h + 1) * SLOT] = (vs[h] + ones_lane).astype(v_ref.dtype)

    return pl.pallas_call(
        body, name="mla_pre_fwd", grid=(s // TM,),
        in_specs=[_rows(TM, pc.shape[1]), _rows(TM, SLOT), _rows(TM, SLOT), _rows(TM, SLOT),
                  _full(qag), _full(kvag), _full(qg), _full(kg), _full(wq), _full(wk), _full(wv)],
        out_specs=[_rows(TM, hw)] * 3, out_shape=[_sds((s, hw), MXU_DTYPE)] * 3,
        compiler_params=_cp(("parallel",)),
    )(pc, cos_t, sin_a, sin_b, qag, kvag, qg, kg, wq, wk, wv)


def mla_pre_bwd(pc, cos_t, sin_a, sin_b, qag, kvag, qg, kg, wq, wk, wv, dq, dk, dv):
    s = pc.shape[0]
    hw = MLA_HEADS * SLOT

    def body(pc_ref, cos_ref, sa_ref, sb_ref, qag_ref, kvag_ref, qg_ref, kg_ref, wq_ref, wk_ref, wv_ref,
             dq_ref, dk_ref, dv_ref, dpc_ref, dqag_ref, dkvag_ref, dqg_ref, dkg_ref, dwq_ref, dwk_ref, dwv_ref):
        first = pl.program_id(0) == 0
        diff, tables = _mla_pre_args(pc_ref, cos_ref, sa_ref, sb_ref, qag_ref, kvag_ref, qg_ref, kg_ref,
                                     wq_ref, wk_ref, wv_ref, True)
        _, vjp = jax.vjp(functools.partial(_mla_pre_fn, AD, tables), *diff)
        sl = lambda h: slice(h * SLOT, (h + 1) * SLOT)
        cot = ([dq_ref[:, sl(h)] for h in range(MLA_HEADS)], [dk_ref[:, sl(h)] for h in range(MLA_HEADS)],
               [dv_ref[:, sl(h)] for h in range(MLA_HEADS)])
        dcq, dckv, dkpe, dqag, dkvag, dqg, dkg, dwq, dwk, dwv = vjp(cot)
        dpc_ref[:, 0:Q_LORA] = dcq
        dpc_ref[:, Q_LORA:Q_LORA + KV_LORA] = dckv
        dpc_ref[:, Q_LORA + KV_LORA:Q_LORA + 2 * KV_LORA] = dkpe
        _acc(dqag_ref, dqag, first)
        _acc(dkvag_ref, dkvag, first)
        _acc(dqg_ref, dqg, first)
        _acc(dkg_ref, dkg, first)
        for h in range(MLA_HEADS):
            _acc(dwq_ref.at[:, sl(h)], dwq[h], first)
            _acc(dwk_ref.at[:, sl(h)], dwk[h], first)
            _acc(dwv_ref.at[:, sl(h)], dwv[h], first)

    return pl.pallas_call(
        body, name="mla_pre_bwd", grid=(s // TM,),
        in_specs=[_rows(TM, pc.shape[1]), _rows(TM, SLOT), _rows(TM, SLOT), _rows(TM, SLOT),
                  _full(qag), _full(kvag), _full(qg), _full(kg), _full(wq), _full(wk), _full(wv),
                  _rows(TM, hw), _rows(TM, hw), _rows(TM, hw)],
        out_specs=[_rows(TM, pc.shape[1]), _full(qag), _full(kvag), _full(qg), _full(kg),
                   _full(wq), _full(wk), _full(wv)],
        out_shape=[_sds(pc.shape), _sds(qag.shape), _sds(kvag.shape), _sds(qg.shape), _sds(kg.shape),
                   _sds(wq.shape), _sds(wk.shape), _sds(wv.shape)],
        compiler_params=_cp(("arbitrary",)),
    )(pc, cos_t, sin_a, sin_b, qag, kvag, qg, kg, wq, wk, wv, dq, dk, dv)


ATT_SCALE = QK_DIM ** -0.5
NEG_BIG = -1e30


def attn_fwd(q, k, v, rider=None):
    s = q.shape[0]
    nq = s // TQ
    hp = ATT_HEADS_PER_STEP
    sl = lambda j: slice(j * SLOT, (j + 1) * SLOT)

    def body(q_ref, k_ref, v_ref, o_ref, lse_ref):
        qi = pl.program_id(1)
        row = lax.broadcasted_iota(jnp.int32, (TQ, TQ), 0)
        col = lax.broadcasted_iota(jnp.int32, (TQ, TQ), 1)
        lane = lax.broadcasted_iota(jnp.int32, (1, SLOT), 1)
        qs = [q_ref[:, sl(j)] for j in range(hp)]

        def step(ki, carry, masked):
            rk = pl.ds(pl.multiple_of(ki * TQ, TQ), TQ)
            out = []
            for j in range(hp):
                m, acc = carry[j]
                sc = _dot(qs[j], k_ref[rk, sl(j)], NT) * ATT_SCALE
                if masked:
                    sc = jnp.where(row >= col, sc, NEG_BIG)
                m_new = jnp.maximum(m, jnp.max(sc, axis=-1, keepdims=True))
                acc = jnp.exp(m - m_new) * acc + _dot(jnp.exp(sc - m_new), v_ref[rk, sl(j)], NN)
                out.append((m_new, acc))
            return tuple(out)

        init = tuple((jnp.full((TQ, 1), NEG_BIG, F32), jnp.zeros((TQ, SLOT), F32)) for _ in range(hp))
        carry = lax.fori_loop(0, qi, lambda ki, cr: step(ki, cr, False), init)
        carry = step(qi, carry, True)
        for j in range(hp):
            m, acc = carry[j]
            l = jnp.sum(jnp.where(lane == V_DIM, acc, 0.0), axis=-1, keepdims=True)
            o_ref[:, sl(j)] = jnp.where(lane < V_DIM, acc / l, 0.0)
            lse_ref[j] = m + jnp.log(l)

    head_col = pl.BlockSpec((s, hp * SLOT), lambda g, i: (0, g))
    tile = pl.BlockSpec((TQ, hp * SLOT), lambda g, i: (i, g))
    return _ride(
        body, rider, name="attn_fwd", grid=(MLA_HEADS // hp, nq),
        in_specs=[tile, head_col, head_col],
        out_specs=[tile, pl.BlockSpec((hp, TQ, 1), lambda g, i: (g, i, 0))],
        out_shape=[_sds((s, MLA_HEADS * SLOT)), _sds((MLA_HEADS, s, 1))],
        operands=(q, k, v), sem=("parallel", "parallel"))


def attn_bwd(q, k, v, o, do, lse, rider=None):
    s = q.shape[0]
    nq = s // TQ
    hp = ATT_HEADS_PER_STEP
    sl = lambda j: slice(j * SLOT, (j + 1) * SLOT)

    def body(q_ref, k_ref, v_ref, o_ref, do_ref, lse_ref, dq_ref, dk_ref, dv_ref, delta_ref):
        ki = pl.program_id(1)
        row = lax.broadcasted_iota(jnp.int32, (TQ, TQ), 0)
        col = lax.broadcasted_iota(jnp.int32, (TQ, TQ), 1)

        @pl.when(ki == 0)
        def _():
            dq_ref[...] = jnp.zeros_like(dq_ref)

            def prep(i, c):
                rows = pl.ds(pl.multiple_of(i * TQ, TQ), TQ)
                for j in range(hp):
                    delta_ref[j, rows, :] = jnp.sum(do_ref[rows, sl(j)] * o_ref[rows, sl(j)], axis=-1, keepdims=True)
                return c

            lax.fori_loop(0, nq, prep, 0)

        kks = [k_ref[:, sl(j)] for j in range(hp)]
        vvs = [v_ref[:, sl(j)] for j in range(hp)]

        def step(qi, carry, masked):
            rq = pl.ds(pl.multiple_of(qi * TQ, TQ), TQ)
            out = []
            for j in range(hp):
                dk, dv = carry[j]
                qq = q_ref[rq, sl(j)]
                dd = do_ref[rq, sl(j)]
                sc = _dot(qq, kks[j], NT) * ATT_SCALE
                if masked:
                    sc = jnp.where(row >= col, sc, NEG_BIG)
                p = jnp.exp(sc - lse_ref[j, rq, :])
                dv = dv + _dot(p, dd, TN)
                ds = p * (_dot(dd, vvs[j], NT) - delta_ref[j, rq, :]) * ATT_SCALE
                dk = dk + _dot(ds, qq, TN)
                dq_ref[rq, sl(j)] = dq_ref[rq, sl(j)] + _dot(ds, kks[j], NN)
                out.append((dk, dv))
            return tuple(out)

        zero = jnp.zeros((TQ, SLOT), F32)
        carry = step(ki, tuple((zero, zero) for _ in range(hp)), True)
        carry = lax.fori_loop(ki + 1, nq, lambda qi, cr: step(qi, cr, False), carry)
        for j in range(hp):
            dk_ref[:, sl(j)] = carry[j][0]
            dv_ref[:, sl(j)] = carry[j][1]

    head_col = pl.BlockSpec((s, hp * SLOT), lambda g, i: (0, g))
    tile = pl.BlockSpec((TQ, hp * SLOT), lambda g, i: (i, g))
    return _ride(
        body, rider, name="attn_bwd", grid=(MLA_HEADS // hp, nq),
        in_specs=[head_col, tile, tile, head_col, head_col, pl.BlockSpec((hp, s, 1), lambda g, i: (g, 0, 0))],
        out_specs=[head_col, tile, tile],
        out_shape=[_sds((s, MLA_HEADS * SLOT))] * 3,
        scratch_shapes=[pltpu.VMEM((hp, s, 1), F32)],
        operands=(q, k, v, o, do, lse), sem=("arbitrary", "arbitrary"))


def _outproj_args(ya_ref, yb_ref, o_ref, mog_ref, woa_ref, wob_ref, woc_ref, cast):
    sl = lambda h: slice(h * SLOT, (h + 1) * SLOT)
    ldw = (lambda r: r[...].astype(F32)) if cast else (lambda r: r[...])
    ldc = (lambda h: woc_ref[sl(h), :].astype(F32)) if cast else (lambda h: woc_ref[sl(h), :])
    return (ya_ref[...], yb_ref[...], [o_ref[:, sl(h)] for h in range(MLA_HEADS)],
            [mog_ref[:, sl(h)] for h in range(MLA_HEADS)], ldw(woa_ref), ldw(wob_ref),
            [ldc(h) for h in range(MLA_HEADS)])


def outproj_fwd(x, ya, yb, o, mog, woa, wob, woc):
    s, d = x.shape

    def body(x_ref, ya_ref, yb_ref, o_ref, mog_ref, woa_ref, wob_ref, woc_ref, x1_ref):
        x1_ref[...] = _outproj(PLAIN, x_ref[...], *_outproj_args(ya_ref, yb_ref, o_ref, mog_ref, woa_ref, wob_ref,
                                                                  woc_ref, False))

    return pl.pallas_call(
        body, name="outproj_fwd", grid=(s // TM,),
        in_specs=[_rows(TM, d), _rows(TM, ya.shape[1]), _rows(TM, yb.shape[1]), _rows(TM, o.shape[1]),
                  _full(mog), _full(woa), _full(wob), _full(woc)],
        out_specs=_rows(TM, d), out_shape=_sds((s, d)),
        compiler_params=_cp(("parallel",)),
    )(x, ya, yb, o, mog, woa, wob, woc)


def outproj_bwd(ya, yb, o, mog, woa, wob, woc, dx2, dx1p):
    s, d = dx2.shape
    npart = dx1p.shape[0]

    def body(ya_ref, yb_ref, o_ref, mog_ref, woa_ref, wob_ref, woc_ref, dx2_ref, dx1p_ref,
             dx1_ref, dya_ref, dyb_ref, do_ref, dmog_ref, dwoa_ref, dwob_ref, dwoc_ref):
        first = pl.program_id(0) == 0
        sl = lambda h: slice(h * SLOT, (h + 1) * SLOT)
        dx1 = dx2_ref[...]
        for p in range(npart):
            dx1 = dx1 + dx1p_ref[p]
        dx1_ref[...] = dx1
        args = _outproj_args(ya_ref, yb_ref, o_ref, mog_ref, woa_ref, wob_ref, woc_ref, True)
        _, vjp = jax.vjp(lambda *a: _outproj(AD, jnp.zeros_like(dx1), *a), *args)
        dya, dyb, do, dmog, dwoa, dwob, dwoc = vjp(dx1)
        dya_ref[...] = dya
        dyb_ref[...] = dyb
        _acc(dwoa_ref, dwoa, first)
        _acc(dwob_ref, dwob, first)
        for h in range(MLA_HEADS):
            do_ref[:, sl(h)] = do[h]
            _acc(dmog_ref.at[:, sl(h)], dmog[h], first)
            _acc(dwoc_ref.at[sl(h), :], dwoc[h], first)

    return pl.pallas_call(
        body, name="outproj_bwd", grid=(s // TM,),
        in_specs=[_rows(TM, ya.shape[1]), _rows(TM, yb.shape[1]), _rows(TM, o.shape[1]),
                  _full(mog), _full(woa), _full(wob), _full(woc), _rows(TM, d),
                  pl.BlockSpec((npart, TM, d), lambda i: (0, i, 0))],
        out_specs=[_rows(TM, d), _rows(TM, ya.shape[1]), _rows(TM, yb.shape[1]), _rows(TM, o.shape[1]),
                   _full(mog), _full(woa), _full(wob), _full(woc)],
        out_shape=[_sds((s, d)), _sds(ya.shape), _sds(yb.shape), _sds(o.shape),
                   _sds(mog.shape), _sds(woa.shape), _sds(wob.shape), _sds(woc.shape)],
        compiler_params=_cp(("arbitrary",)),
    )(ya, yb, o, mog, woa, wob, woc, dx2, dx1p)


def ffn_fwd(x1, g2, w1, w2):
    s, d = x1.shape
    npart, _, fs = w1.shape

    def body(x1_ref, g_ref, w1_ref, w2_ref, x2_ref):
        p = pl.program_id(1)
        x1v = x1_ref[...]
        part = _ffn_part(PLAIN, x1v, g_ref[...], w1_ref[...], w2_ref[...])

        @pl.when(p == 0)
        def _():
            x2_ref[...] = x1v + part

        @pl.when(p != 0)
        def _():
            x2_ref[...] = x2_ref[...] + part

    return pl.pallas_call(
        body, name="ffn_fwd", grid=(s // TM, npart),
        in_specs=[pl.BlockSpec((TM, d), lambda i, p: (i, 0)), pl.BlockSpec(g2.shape, lambda i, p: (0, 0)),
                  pl.BlockSpec((None, d, fs), lambda i, p: (p, 0, 0)), pl.BlockSpec((None, fs, d), lambda i, p: (p, 0, 0))],
        out_specs=pl.BlockSpec((TM, d), lambda i, p: (i, 0)), out_shape=_sds((s, d)),
        compiler_params=_cp(("parallel", "arbitrary")),
    )(x1, g2, w1, w2)


def ffn_bwd(x1, g2, w1, w2, dx2, rider=None):
    s, d = x1.shape
    npart, _, fs = w1.shape

    def body(x1_ref, g_ref, w1_ref, w2_ref, dx2_ref, dx1p_ref, dg_ref, dw1_ref, dw2_ref):
        p = pl.program_id(0)
        i = pl.program_id(1)
        _, vjp = jax.vjp(functools.partial(_ffn_part, AD), x1_ref[...], g_ref[...], _f32(w1_ref), _f32(w2_ref))
        dx1, dg, dw1, dw2 = vjp(dx2_ref[...])
        dx1p_ref[...] = dx1
        _acc(dg_ref, dg, (p == 0) & (i == 0))
        _acc(dw1_ref, dw1, i == 0)
        _acc(dw2_ref, dw2, i == 0)

    return _ride(
        body, rider, name="ffn_bwd", grid=(npart, s // TM),
        in_specs=[pl.BlockSpec((TM, d), lambda p, i: (i, 0)), pl.BlockSpec(g2.shape, lambda p, i: (0, 0)),
                  pl.BlockSpec((None, d, fs), lambda p, i: (p, 0, 0)), pl.BlockSpec((None, fs, d), lambda p, i: (p, 0, 0)),
                  pl.BlockSpec((TM, d), lambda p, i: (i, 0))],
        out_specs=[pl.BlockSpec((None, TM, d), lambda p, i: (p, i, 0)), pl.BlockSpec(g2.shape, lambda p, i: (0, 0)),
                   pl.BlockSpec((None, d, fs), lambda p, i: (p, 0, 0)), pl.BlockSpec((None, fs, d), lambda p, i: (p, 0, 0))],
        out_shape=[_sds((npart, s, d)), _sds(g2.shape), _sds(w1.shape), _sds(w2.shape)],
        operands=(x1, g2, w1, w2, dx2), sem=("arbitrary", "arbitrary"))


def loss_head(y, target):
    s, d = y.shape

    def body(y_ref, t_ref, dy_ref, loss_ref):
        err = y_ref[...] - t_ref[...]
        dy_ref[...] = err * (1.0 / d)
        part = jnp.sum(jnp.sum(err * err, axis=-1, keepdims=True), axis=0, keepdims=True) * (0.5 / d)
        _acc(loss_ref, jnp.broadcast_to(part, loss_ref.shape), pl.program_id(0) == 0)

    return pl.pallas_call(
        body, name="loss_head", grid=(s // TM,),
        in_specs=[_rows(TM, d), _rows(TM, d)],
        out_specs=[_rows(TM, d), pl.BlockSpec((1, SLOT), lambda i: (0, 0))],
        out_shape=[_sds((s, d)), _sds((1, SLOT))],
        compiler_params=_cp(("arbitrary",)),
    )(y, target)


def _row_block(r):
    for b in (512, 256, 128, 64, 32, 16, 8):
        if r % b == 0:
            return b
    return r


def sum_cores(a, got, half, me):
    nch, _, r, c = a.shape
    br = _row_block(r)

    def body(sp_ref, a_ref, g_ref, wire_ref, own_ref):
        tot = a_ref[...] + g_ref[...]
        wire_ref[...] = tot.astype(wire_ref.dtype)

        @pl.when(pl.program_id(1) == sp_ref[1])
        def _():
            own_ref[...] = tot

    grid_spec = pltpu.PrefetchScalarGridSpec(
        num_scalar_prefetch=1, grid=(r // br, nch),
        in_specs=[pl.BlockSpec((None, None, br, c), lambda i, p, sp: (p, sp[0], i, 0)),
                  pl.BlockSpec((None, br, c), lambda i, p, sp: (p, i, 0))],
        out_specs=[pl.BlockSpec((None, br, c), lambda i, p, sp: (p, i, 0)), pl.BlockSpec((br, c), lambda i, p, sp: (i, 0))])
    return pl.pallas_call(body, name="sum_cores", grid_spec=grid_spec, out_shape=[_sds((nch, r, c), BF16), _sds((r, c))],
                          compiler_params=_cp(("parallel", "arbitrary")))(jnp.stack([half, me]).astype(jnp.int32), a, got)


def sum_chips(own, recv, half):
    r, c = own.shape
    br = _row_block(r)

    def body(sp_ref, own_ref, r0_ref, r1_ref, r2_ref, out_ref):
        del sp_ref
        out_ref[...] = ((own_ref[...] + r0_ref[...].astype(F32)) + r1_ref[...].astype(F32)) + r2_ref[...].astype(F32)

    grid_spec = pltpu.PrefetchScalarGridSpec(
        num_scalar_prefetch=1, grid=(r // br,),
        in_specs=[pl.BlockSpec((br, c), lambda i, sp: (i, 0))]
        + [pl.BlockSpec((None, br, c), functools.partial(lambda i, sp, j: (j, i, 0), j=j)) for j in range(3)],
        out_specs=pl.BlockSpec((None, br, c), lambda i, sp: (sp[0], i, 0)))
    return pl.pallas_call(body, name="sum_chips", grid_spec=grid_spec, out_shape=_sds((2, r, c)),
                          compiler_params=_cp(("parallel",)))(half.reshape(1).astype(jnp.int32), own, recv, recv, recv)


def adamw(w, g, m, v, name):
    r, c = w.shape
    br = _row_block(r)
    c1 = 1.0 / (1.0 - ADAM_B1 ** ADAM_STEP)
    c2 = 1.0 / (1.0 - ADAM_B2 ** ADAM_STEP)

    def body(w_ref, g_ref, m_ref, v_ref, d_ref, nm_ref, nv_ref):
        gg = g_ref[...]
        nm = ADAM_B1 * m_ref[...] + (1.0 - ADAM_B1) * gg
        nv = ADAM_B2 * v_ref[...] + (1.0 - ADAM_B2) * (gg * gg)
        d_ref[...] = -ADAM_LR * ((nm * c1) / (jnp.sqrt(nv * c2) + ADAM_EPS) + ADAM_WD * w_ref[...])
        nm_ref[...] = nm
        nv_ref[...] = nv

    return pl.pallas_call(
        body, name=name, grid=(r // br,), in_specs=[_rows(br, c)] * 4, out_specs=[_rows(br, c)] * 3,
        out_shape=[_sds((r, c))] * 3, compiler_params=_cp(("parallel",)),
    )(w, g, m, v)


def _place():
    x, y, c = lax.axis_index("x"), lax.axis_index("y"), lax.axis_index("c")
    chips = [(1 - x, y), (x, 1 - y), (1 - x, 1 - y)]
    return x, y, c, chips


def _remote(src, dst, send_sem, recv_sem, to):
    return pltpu.make_async_remote_copy(src_ref=src, dst_ref=dst, send_sem=send_sem, recv_sem=recv_sem,
                                        device_id=to, device_id_type=MESH)


def gather_rider(arrs):
    n = len(arrs)
    me_chip = 2 * lax.axis_index("x") + lax.axis_index("y")
    bufs = [lax.dynamic_update_index_in_dim(lax.empty((N_CHIPS,) + a.shape, a.dtype), a, me_chip, 0) for a in arrs]

    def plan(ins, outs, sems):
        send_sems, recv_sems = sems
        x, y, c, chips = _place()
        me = 2 * x + y
        half, other, sibling = pl.ds(2 * c, 2), pl.ds(2 - 2 * c, 2), (x, y, 1 - c)
        cp = lambda i, k, src, dst, to: _remote(src, dst, send_sems.at[i, k], recv_sems.at[i, k], to)
        pairs = [(i, j, cx, cy) for i in range(n) for j, (cx, cy) in enumerate(chips)]
        blk = lambda i, cx, cy, part: outs[i].at[2 * cx + cy, part]
        first = lambda: [cp(i, j, ins[i].at[half], outs[i].at[me, half], (cx, cy, c)) for i, j, cx, cy in pairs]
        landed = lambda: [cp(i, j, blk(i, cx, cy, half), blk(i, cx, cy, half), (cx, cy, c)) for i, j, cx, cy in pairs]
        passed = lambda: [cp(i, 3 + j, blk(i, cx, cy, half), blk(i, cx, cy, half), sibling) for i, j, cx, cy in pairs]
        from_sibling = lambda: [cp(i, 3 + j, blk(i, cx, cy, other), blk(i, cx, cy, other), sibling) for i, j, cx, cy in pairs]
        return first, landed, passed, from_sibling

    def start(ins, outs, sems):
        for cp in plan(ins, outs, sems)[0]():
            cp.start()

    def finish(ins, outs, sems):
        first, landed, passed, from_sibling = plan(ins, outs, sems)
        forwards = passed()
        for a, b in zip(landed(), forwards):
            a.wait_recv()
            b.start()
        for cp in from_sibling():
            cp.wait_recv()
        for cp in first() + forwards:
            cp.wait_send()

    return Rider(list(arrs) + bufs, [_sds((N_CHIPS,) + a.shape, a.dtype) for a in arrs], {n + i: i for i in range(n)},
                 [pltpu.SemaphoreType.DMA((n, 6)), pltpu.SemaphoreType.DMA((n, 6))], start, finish)


class Reducer:
    def __init__(self, arrs):
        self.a = list(arrs)
        self.n = len(self.a)
        self.c = lax.axis_index("c")
        self.me = 2 * lax.axis_index("x") + lax.axis_index("y")

    def swap_rider(self):
        n = self.n

        def plan(ins, outs, sems):
            x, y, c, _ = _place()
            return [_remote(ins[i].at[p, 1 - c], outs[i].at[p], sems[0].at[i, p], sems[1].at[i, p], (x, y, 1 - c))
                    for i in range(n) for p in range(N_CHIPS)]

        return Rider(self.a, [_sds((N_CHIPS,) + a.shape[2:]) for a in self.a], {},
                     [pltpu.SemaphoreType.DMA((n, N_CHIPS)), pltpu.SemaphoreType.DMA((n, N_CHIPS))],
                     lambda *r: [cp.start() for cp in plan(*r)], lambda *r: [cp.wait() for cp in plan(*r)])

    def after_swap(self, got):
        pairs = [sum_cores(a, g, self.c, self.me) for a, g in zip(self.a, got)]
        self.wire, self.own = [p[0] for p in pairs], [p[1] for p in pairs]

    def scatter_rider(self):
        n = self.n

        def plan(ins, outs, sems):
            x, y, c, chips = _place()
            return [_remote(ins[i].at[2 * cx + cy], outs[i].at[j], sems[0].at[i, j], sems[1].at[i, j], (cx, cy, c))
                    for i in range(n) for j, (cx, cy) in enumerate(chips)]

        return Rider(self.wire, [_sds((3,) + w.shape[1:], w.dtype) for w in self.wire], {},
                     [pltpu.SemaphoreType.DMA((n, 3)), pltpu.SemaphoreType.DMA((n, 3))],
                     lambda *r: [cp.start() for cp in plan(*r)], lambda *r: [cp.wait() for cp in plan(*r)])

    def after_scatter(self, recv):
        self.full = [sum_chips(o, r, self.c) for o, r in zip(self.own, recv)]

    def share_rider(self):
        n = self.n

        def plan(ins, outs, sems):
            x, y, c, _ = _place()
            return [_remote(ins[i].at[c], outs[i].at[c], sems[0].at[i], sems[1].at[i], (x, y, 1 - c)) for i in range(n)]

        return Rider(self.full, [_sds(f.shape) for f in self.full], {i: i for i in range(n)},
                     [pltpu.SemaphoreType.DMA((n,)), pltpu.SemaphoreType.DMA((n,))],
                     lambda *r: [cp.start() for cp in plan(*r)], lambda *r: [cp.wait() for cp in plan(*r)])

    def run(self):
        self.after_swap(run_rider(self.swap_rider(), "swap_halves"))
        self.after_scatter(run_rider(self.scatter_rider(), "scatter_chips"))
        return run_rider(self.share_rider(), "share_halves")


def _pad_slots(a, live):
    lead = a.shape[:-1]
    a = a.reshape(lead + (MLA_HEADS, live))
    a = jnp.pad(a, [(0, 0)] * len(lead) + [(0, 0), (0, SLOT - live)])
    return a.reshape(lead + (MLA_HEADS * SLOT,))


def _unpad_slots(a, live):
    lead = a.shape[:-1]
    return a.reshape(lead + (MLA_HEADS, SLOT))[..., :live].reshape(lead + (MLA_HEADS * live,))


def _rope_tables(positions, s):
    half = QK_ROPE // 2
    inv_freq = ROPE_THETA ** (-jnp.arange(half, dtype=F32) / half)
    ang = positions.reshape(s).astype(F32)[:, None] * inv_freq[None, :]
    cos, sin = jnp.cos(ang), jnp.sin(ang)
    one = jnp.ones((s, QK_NOPE), F32)
    z64, z16, z32 = jnp.zeros((s, QK_NOPE), F32), jnp.zeros((s, half), F32), jnp.zeros((s, SLOT - QK_DIM), F32)
    cos_t = jnp.concatenate([one, cos, cos, z32], axis=1)
    sin_a = jnp.concatenate([z64, -sin, z16, z32], axis=1)
    sin_b = jnp.concatenate([z64, z16, sin, z32], axis=1)
    return cos_t, sin_a, sin_b


def _layer_weights(full, small, l):
    w_in = jnp.concatenate([full["w_in"][p] for p in range(N_CHIPS)], axis=1)
    wc = jnp.pad(w_in[:, 1536:], ((0, 0), (0, 512 - (w_in.shape[1] - 1536))))
    w_uq = jnp.concatenate([full["mla_w_uq"][p] for p in range(N_CHIPS)], axis=1)
    w_ukv = jnp.concatenate([full["mla_w_ukv"][p] for p in range(N_CHIPS)], axis=1)
    ukv = w_ukv.reshape(KV_LORA, MLA_HEADS, QK_NOPE + V_DIM)
    w_out = jnp.concatenate([full["w_out"][p] for p in range(N_CHIPS)], axis=0)
    woc = w_out[512:].reshape(MLA_HEADS, V_DIM, D_MODEL)
    woc = jnp.pad(woc, ((0, 0), (0, SLOT - V_DIM), (0, 0))).reshape(MLA_HEADS * SLOT, D_MODEL)
    row = lambda a: a.reshape(1, -1)
    return dict(
        g1=row(small["norm1_gain"][l]), wa=w_in[:, :512], wb=w_in[:, 512:1536], wc=wc,
        vg=row(small["gm_v_gain"][l]), ws=small["gm_w_s"][l], bs=small["gm_b_s"][l].reshape(4, CHUNK, 1),
        gog=row(small["gm_out_gain"][l]), hog=row(small["hg_out_gain"][l]),
        qag=row(small["mla_q_a_gain"][l]), kvag=row(small["mla_kv_a_gain"][l]),
        qg=row(jnp.pad(small["mla_q_gain"][l], (0, SLOT - QK_DIM))), kg=row(jnp.pad(small["mla_k_gain"][l], (0, SLOT - QK_DIM))),
        wq=_pad_slots(w_uq, QK_DIM), wk=_pad_slots(ukv[..., :QK_NOPE].reshape(KV_LORA, -1), QK_NOPE),
        wv=_pad_slots(ukv[..., QK_NOPE:].reshape(KV_LORA, -1), V_DIM),
        mog=row(_pad_slots(small["mla_out_gain"][l], V_DIM)),
        woa=w_out[:256], wob=w_out[256:512], woc=woc,
        g2=row(small["norm2_gain"][l]), w1=full["w_ff1"], w2=full["w_ff2"],
    )


def _shard_cols(a):
    r, c4 = a.shape
    return a.reshape(r, N_CHIPS, c4 // N_CHIPS).transpose(1, 0, 2)


def local_step(x, positions, target, small, comm):
    s = x.shape[0]
    cos_t, sin_a, sin_b = _rope_tables(positions, s)
    lbs = lower_bounds_fwd(small["hg_lower_bound"])
    lw, saved = [], []
    for l in range(DEPTH):
        w = _layer_weights(comm.weights(l), small, l)
        lw.append(w)
        lb = lbs[l].reshape(1, -1)
        pa, pb, pc = inproj_fwd(x, w["g1"], w["wa"], w["wb"], w["wc"])
        ya = gm_fwd(pa, w["vg"], w["ws"], w["bs"], w["gog"])
        yb, states = hg_fwd(pb, lb, w["hog"])
        q, k, v = mla_pre_fwd(pc, cos_t, sin_a, sin_b, w["qag"], w["kvag"], w["qg"], w["kg"], w["wq"], w["wk"], w["wv"])
        rider = comm.gather_rider(l + 1) if l + 1 < DEPTH else None
        (o, lse), got = attn_fwd(q, k, v, rider)
        if rider is not None:
            comm.gathered(l + 1, got)
        x1 = outproj_fwd(x, ya, yb, o, w["mog"], w["woa"], w["wob"], w["woc"])
        x2 = ffn_fwd(x1, w["g2"], w["w1"], w["w2"])
        saved.append(dict(x=x, pa=pa, pb=pb, pc=pc, ya=ya, yb=yb, states=states, q=q, k=k, v=v, o=o, lse=lse, x1=x1, lb=lb))
        x = x2
    dx, loss_part = loss_head(x, target)
    groups = [None] * DEPTH
    sm = {n: [None] * DEPTH for n in ("norm1_gain", "gm_v_gain", "gm_w_s", "gm_b_s", "gm_out_gain", "hg_out_gain",
                                       "mla_q_a_gain", "mla_kv_a_gain", "mla_q_gain", "mla_k_gain", "mla_out_gain",
                                       "norm2_gain")}
    dlbs = [None] * DEPTH
    red, red_layer = None, None
    for l in reversed(range(DEPTH)):
        w, a = lw[l], saved[l]
        (dx1p, dg2, dw1, dw2), got = ffn_bwd(a["x1"], w["g2"], w["w1"], w["w2"], dx, red.swap_rider() if red else None)
        if red:
            red.after_swap(got)
        dx1, dya, dyb, do, dmog, dwoa, dwob, dwoc = outproj_bwd(a["ya"], a["yb"], a["o"], w["mog"], w["woa"], w["wob"],
                                                                 w["woc"], dx, dx1p)
        (dq, dk, dv), got = attn_bwd(a["q"], a["k"], a["v"], a["o"], do, a["lse"], red.scatter_rider() if red else None)
        if red:
            red.after_scatter(got)
        dpc, dqag, dkvag, dqg, dkg, dwq, dwk, dwv = mla_pre_bwd(a["pc"], cos_t, sin_a, sin_b, w["qag"], w["kvag"], w["qg"],
                                                                  w["kg"], w["wq"], w["wk"], w["wv"], dq, dk, dv)
        (dpb, dlb, dhog), got = hg_bwd(a["pb"], a["lb"], w["hog"], a["states"], dyb, red.share_rider() if red else None)
        if red:
            groups[red_layer] = list(got)
        dpa, dvg, dws, dbs, dgog = gm_bwd(a["pa"], w["vg"], w["ws"], w["bs"], w["gog"], dya)
        dx, dg1, dwa, dwb, dwc = inproj_bwd(a["x"], w["g1"], w["wa"], w["wb"], w["wc"], dpa, dpb, dpc, dx1)
        dukv = jnp.concatenate([dwk.reshape(KV_LORA, MLA_HEADS, SLOT)[..., :QK_NOPE],
                                dwv.reshape(KV_LORA, MLA_HEADS, SLOT)[..., :V_DIM]], axis=-1)
        dwo = jnp.concatenate([dwoa, dwob, dwoc.reshape(MLA_HEADS, SLOT, D_MODEL)[:, :V_DIM].reshape(-1, D_MODEL)], axis=0)
        arrs = [_shard_cols(jnp.concatenate([dwa, dwb, dwc[:, :1952 - 1536]], axis=1)), _shard_cols(_unpad_slots(dwq, QK_DIM)),
                _shard_cols(dukv.reshape(KV_LORA, -1)), dwo.reshape(N_CHIPS, -1, D_MODEL), dw1, dw2]
        arrs = [g.reshape(N_CHIPS, 2, g.shape[1] // 2, g.shape[2]) for g in arrs]
        red = comm.reducer(arrs) if l > 0 else None
        red_layer = l
        if red is None:
            groups[l] = arrs
        sm["norm1_gain"][l] = dg1[0]
        sm["gm_v_gain"][l] = dvg[0]
        sm["gm_w_s"][l] = dws
        sm["gm_b_s"][l] = dbs[..., 0]
        sm["gm_out_gain"][l] = dgog[0]
        sm["hg_out_gain"][l] = dhog[0]
        sm["mla_q_a_gain"][l] = dqag[0]
        sm["mla_kv_a_gain"][l] = dkvag[0]
        sm["mla_q_gain"][l] = dqg[0, :QK_DIM]
        sm["mla_k_gain"][l] = dkg[0, :QK_DIM]
        sm["mla_out_gain"][l] = _unpad_slots(dmog[0], V_DIM)
        sm["norm2_gain"][l] = dg2[0]
        dlbs[l] = dlb[0]
    sm = {n: jnp.stack(v) for n, v in sm.items()}
    sm["hg_lower_bound"] = lower_bounds_bwd(small["hg_lower_bound"], jnp.stack(dlbs))
    return loss_part, dx, groups, sm


BIG = ("w_in", "mla_w_uq", "mla_w_ukv", "w_out", "w_ff1", "w_ff2")
SMALL = ("norm1_gain", "gm_v_gain", "gm_w_s", "gm_b_s", "gm_out_gain", "hg_lower_bound", "hg_out_gain",
         "mla_q_a_gain", "mla_kv_a_gain", "mla_q_gain", "mla_k_gain", "mla_out_gain", "norm2_gain")
ORDER = ("norm1_gain", "w_in", "gm_v_gain", "gm_w_s", "gm_b_s", "gm_out_gain", "hg_lower_bound", "hg_out_gain",
         "mla_q_a_gain", "mla_w_uq", "mla_kv_a_gain", "mla_w_ukv", "mla_q_gain", "mla_k_gain", "mla_out_gain",
         "w_out", "norm2_gain", "w_ff1", "w_ff2")
PACK_ROWS = 288


def _pack(arrs):
    flat = jnp.concatenate([a.reshape(-1) for a in arrs])
    total = 2 * N_CHIPS * PACK_ROWS * SLOT
    return jnp.pad(flat, (0, total - flat.shape[0]))


def _unpack(flat, shapes):
    out, off = [], 0
    for sh in shapes:
        size = 1
        for d in sh:
            size *= d
        out.append(flat[off:off + size].reshape(sh))
        off += size
    return out


class ChipComm:
    def __init__(self, shards):
        self.shards = shards
        self.full = {}

    def gather_rider(self, l):
        return gather_rider([self.shards[n][l].astype(MXU_DTYPE).reshape(4, self.shards[n].shape[1] // 4, -1) for n in BIG])

    def gathered(self, l, outs):
        self.full[l] = {n: o.reshape((N_CHIPS,) + self.shards[n].shape[1:]) for n, o in zip(BIG, outs)}

    def weights(self, l):
        if l not in self.full:
            self.gathered(l, run_rider(self.gather_rider(l), "gather_weights"))
        return self.full[l]

    def reducer(self, arrs):
        return Reducer(arrs)


def kernel(x, positions, norm1_gain, w_in, gm_v_gain, gm_w_s, gm_b_s, gm_out_gain, hg_lower_bound, hg_out_gain, mla_q_a_gain, mla_w_uq, mla_kv_a_gain, mla_w_ukv, mla_q_gain, mla_k_gain, mla_out_gain, w_out, norm2_gain, w_ff1, w_ff2, loss_target, m_norm1_gain, m_w_in, m_gm_v_gain, m_gm_w_s, m_gm_b_s, m_gm_out_gain, m_hg_lower_bound, m_hg_out_gain, m_mla_q_a_gain, m_mla_w_uq, m_mla_kv_a_gain, m_mla_w_ukv, m_mla_q_gain, m_mla_k_gain, m_mla_out_gain, m_w_out, m_norm2_gain, m_w_ff1, m_w_ff2, v_norm1_gain, v_w_in, v_gm_v_gain, v_gm_w_s, v_gm_b_s, v_gm_out_gain, v_hg_lower_bound, v_hg_out_gain, v_mla_q_a_gain, v_mla_w_uq, v_mla_kv_a_gain, v_mla_w_ukv, v_mla_q_gain, v_mla_k_gain, v_mla_out_gain, v_w_out, v_norm2_gain, v_w_ff1, v_w_ff2):
    given = dict(locals())
    weights = {n: given[n] for n in ORDER}
    moms = {n: given["m_" + n] for n in ORDER}
    vars_ = {n: given["v_" + n] for n in ORDER}
    s, d = x.shape[1], x.shape[2]

    small = {n: weights[n] for n in SMALL}
    comm = ChipComm({n: weights[n] for n in BIG})
    loss_part, dx, groups, small_g = local_step(x.reshape(s, d), positions, loss_target.reshape(s, d), small, comm)
    loss = lax.psum(loss_part[0, 0], ("x", "y", "c"))

    pack_g = _pack([small_g[n] for n in SMALL]).reshape(N_CHIPS, 2, PACK_ROWS, SLOT)
    last = Reducer(groups[0] + [pack_g]).run()
    groups[0] = list(last[:-1])
    pack_full = run_rider(gather_rider([last[-1].reshape(4, PACK_ROWS // 2, SLOT)]), "gather_small")[0].reshape(-1)
    grads = {n: jnp.stack([groups[l][i].reshape(weights[n].shape[1:]) for l in range(DEPTH)]) for i, n in enumerate(BIG)}
    grads.update(zip(SMALL, _unpack(pack_full, [weights[n].shape for n in SMALL])))

    delta, new_m, new_v = {}, {}, {}
    flat2 = lambda a: a.reshape(-1, a.shape[-1])
    for n in BIG:
        outs = adamw(flat2(weights[n]), flat2(grads[n]), flat2(moms[n]), flat2(vars_[n]), "adamw_" + n)
        delta[n], new_m[n], new_v[n] = [o.reshape(weights[n].shape) for o in outs]
    shapes = [weights[n].shape for n in SMALL]
    pk = lambda t: _pack([t[n] for n in SMALL]).reshape(-1, SLOT)
    outs = adamw(pk(weights), pack_full.reshape(-1, SLOT), pk(moms), pk(vars_), "adamw_small")
    for tgt, o in zip((delta, new_m, new_v), outs):
        tgt.update(zip(SMALL, _unpack(o.reshape(-1), shapes)))

    return (loss, dx.reshape(x.shape), *[grads[n] for n in ORDER], *[delta[n] for n in ORDER],
            *[new_m[n] for n in ORDER], *[new_v[n] for n in ORDER])
```

```python
import functools

import jax
import jax.numpy as jnp
from jax import lax
from jax.experimental import pallas as pl
from jax.experimental.pallas import tpu as pltpu

F32 = jnp.float32
BF16 = jnp.bfloat16
MXU_DTYPE = BF16

D_MODEL = 1024
DEPTH = 4
CHUNK = 128
EPS = 1e-6
HEAD64 = 64
MLA_HEADS = 8
QK_NOPE = 64
QK_ROPE = 32
QK_DIM = 96
V_DIM = 64
Q_LORA = 256
KV_LORA = 128
SLOT = 128
ROPE_THETA = 10000.0
D_FF_SHARD = 1024
N_CHIPS = 4

ADAM_LR = 0.001
ADAM_B1 = 0.9
ADAM_B2 = 0.999
ADAM_EPS = 1e-08
ADAM_WD = 0.01
ADAM_STEP = 10

TM = 256
TM_FFN = 512
TQ = 256
ATT_HEADS_PER_STEP = 2
VMEM_LIMIT = 56 * 1024 * 1024

NN = (((1,), (0,)), ((), ()))
NT = (((1,), (1,)), ((), ()))
TN = (((0,), (0,)), ((), ()))
BNN = (((2,), (1,)), ((0,), (0,)))
BNT = (((2,), (2,)), ((0,), (0,)))
BTN = (((1,), (1,)), ((0,), (0,)))


def _dot(a, b, dims):
    return lax.dot_general(a.astype(MXU_DTYPE), b.astype(MXU_DTYPE), dims, preferred_element_type=F32)


def _hdot(a, b, dims=NN):
    return lax.dot_general(a, b, dims, precision=lax.Precision.HIGHEST, preferred_element_type=F32)


def _make_ad(dims, da_dims, da_swap, db_dims, db_swap):
    @jax.custom_vjp
    def f(a, b):
        return _dot(a, b, dims)

    def fwd(a, b):
        return _dot(a, b, dims), (a, b)

    def bwd(res, g):
        a, b = res
        da = _dot(b, g, da_dims) if da_swap else _dot(g, b, da_dims)
        db = _dot(g, a, db_dims) if db_swap else _dot(a, g, db_dims)
        return da, db

    f.defvjp(fwd, bwd)
    return f


@functools.partial(jax.custom_vjp, nondiff_argnums=(1,))
def _roll_ad(x, shift):
    return pltpu.roll(x, shift, 1)


def _roll_ad_fwd(x, shift):
    return pltpu.roll(x, shift, 1), None


def _roll_ad_bwd(shift, _, g):
    return (pltpu.roll(g, (g.shape[1] - shift) % g.shape[1], 1),)


_roll_ad.defvjp(_roll_ad_fwd, _roll_ad_bwd)


class _Ops:
    pass


PLAIN = _Ops()
PLAIN.mm = lambda a, b: _dot(a, b, NN)
PLAIN.mm_nt = lambda a, b: _dot(a, b, NT)
PLAIN.mm_tn = lambda a, b: _dot(a, b, TN)
PLAIN.bmm = lambda a, b: _dot(a, b, BNN)
PLAIN.bmm_nt = lambda a, b: _dot(a, b, BNT)
PLAIN.roll = lambda x, s: pltpu.roll(x, s, 1)

AD = _Ops()
AD.mm = _make_ad(NN, NT, False, TN, False)
AD.mm_nt = _make_ad(NT, NN, False, TN, True)
AD.mm_tn = _make_ad(TN, NT, True, NN, False)
AD.bmm = _make_ad(BNN, BNT, False, BTN, False)
AD.bmm_nt = _make_ad(BNT, BNN, False, BTN, True)
AD.roll = _roll_ad


def _sigmoid(x):
    return jax.nn.sigmoid(x)


def _gelu(x):
    return 0.5 * x * (1.0 + jnp.tanh(0.7978845608028654 * (x + 0.044715 * (x * x * x))))


def _rms(x, g):
    return x * lax.rsqrt(jnp.mean(x * x, axis=-1, keepdims=True) + EPS) * g


def _head_masks256():
    lane = lax.broadcasted_iota(jnp.int32, (1, 4 * HEAD64), 1)
    return [(jnp.right_shift(lane, 6) == h).astype(F32) for h in range(4)]


def _headnorm256(x, g):
    ms = jnp.zeros_like(x)
    sq = x * x
    for m in _head_masks256():
        ms = ms + m * (jnp.sum(sq * m, axis=-1, keepdims=True) * (1.0 / HEAD64))
    return x * lax.rsqrt(ms + EPS) * g


def _slot_norm(x, g, n):
    return x * lax.rsqrt(jnp.sum(x * x, axis=-1, keepdims=True) * (1.0 / n) + EPS) * g


def _rope(ops, x, cos_t, sin_a, sin_b):
    return x * cos_t + ops.roll(x, SLOT - QK_ROPE // 2) * sin_a + ops.roll(x, QK_ROPE // 2) * sin_b


def _inproj(ops, x, g1, wa, wb, wc):
    h = _rms(x, g1)
    return ops.mm(h, wa), ops.mm(h, wb), ops.mm(h, wc)


def _gm_chunk(ops, ur, vr, vg, ws4, bs, og):
    c = ur.shape[0]
    masks = _head_masks256()
    mh = jnp.concatenate([m[None] for m in masks], axis=0)
    u = _gelu(ur)
    v = _headnorm256(_gelu(vr), vg)
    t = lax.broadcasted_iota(jnp.int32, (c, c), 0)
    s = lax.broadcasted_iota(jnp.int32, (c, c), 1)
    w = jnp.where((t >= s)[None], ws4, 0.0)
    y = jnp.sum(ops.bmm(w, v[None] * mh), axis=0)
    for h in range(4):
        y = y + bs[h] * masks[h]
    return _headnorm256(u * y, og)


def _hg_chunk(ops, st, qr, fr, ir, gr, lb, og):
    c = qr.shape[0]
    n = qr.shape[1]
    masks = _head_masks256()
    mh = jnp.concatenate([m[None] for m in masks], axis=0)
    q = qr * _sigmoid(qr)
    f = lb + (1.0 - lb) * _sigmoid(fr)
    k = 1.0 - f
    logf = jnp.log(f)
    t = lax.broadcasted_iota(jnp.int32, (c, c), 0)
    s = lax.broadcasted_iota(jnp.int32, (c, c), 1)
    tcol = lax.broadcasted_iota(jnp.int32, (c, 1), 0)
    b = _hdot((t >= s).astype(F32), logf)
    btot = jnp.sum(logf, axis=0, keepdims=True)
    inter = ops.mm_nt(q * jnp.exp(b), st)
    p4 = jnp.zeros((4, c, c), F32)
    lg = 6
    while lg >= 0:
        m = 1 << lg
        bnd = jnp.left_shift(jnp.right_shift(t, lg + 1), lg + 1) + (m - 1)
        r = _hdot((s == bnd).astype(F32), b)
        right = jnp.bitwise_and(jnp.right_shift(tcol, lg), 1) == 1
        qe = jnp.where(right, q * jnp.exp(jnp.where(right, b - r, 0.0)), 0.0)
        ke = jnp.where(right, 0.0, k * jnp.exp(jnp.where(right, 0.0, r - b)))
        lm = ((jnp.right_shift(t, lg + 1) == jnp.right_shift(s, lg + 1))
              & (jnp.bitwise_and(jnp.right_shift(t, lg), 1) == 1)
              & (jnp.bitwise_and(jnp.right_shift(s, lg), 1) == 0))
        s4 = ops.mm_nt((qe[None] * mh).reshape(4 * c, n), ke).reshape(4, c, c)
        p4 = p4 + lm.astype(F32)[None] * s4
        lg -= 1
    intra = jnp.sum(ops.bmm(p4, ir[None] * mh), axis=0)
    qk = q * k
    dsum = jnp.zeros_like(qk)
    for m_ in masks:
        dsum = dsum + m_ * jnp.sum(qk * m_, axis=-1, keepdims=True)
    o = inter + intra + dsum * ir
    kd = k * jnp.exp(btot - b)
    rr = lax.broadcasted_iota(jnp.int32, (n, n), 0)
    cc = lax.broadcasted_iota(jnp.int32, (n, n), 1)
    bd = (jnp.right_shift(rr, 6) == jnp.right_shift(cc, 6)).astype(F32)
    st_new = st * jnp.exp(btot) + bd * ops.mm_tn(ir, kd)
    y = _headnorm256(o, og) * (gr * _sigmoid(gr))
    return st_new, y


def _mla_pre(ops, cq, ckv, kpe, cos_t, sin_a, sin_b, qag, kvag, qg, kg, wq, wk, wv):
    cqn = _rms(cq, qag)
    ckvn = _rms(ckv, kvag)
    kper = ops.roll(kpe, QK_NOPE)
    qs, ks, vs = [], [], []
    for h in range(MLA_HEADS):
        qh = _slot_norm(ops.mm(cqn, wq[h]), qg, QK_DIM)
        qs.append(_rope(ops, qh, cos_t, sin_a, sin_b))
        kh = _slot_norm(ops.mm(ckvn, wk[h]) + kper, kg, QK_DIM)
        ks.append(_rope(ops, kh, cos_t, sin_a, sin_b))
        vs.append(ops.mm(ckvn, wv[h]))
    return qs, ks, vs


def _outproj(ops, x, ya, yb, o, mog, woa, wob, woc):
    acc = x + ops.mm(ya, woa) + ops.mm(yb, wob)
    for h in range(MLA_HEADS):
        acc = acc + ops.mm(_slot_norm(o[h], mog[h], V_DIM), woc[h])
    return acc


def _ffn_part(ops, x1, g2, w1p, w2p):
    a = ops.mm(_rms(x1, g2), w1p)
    r = jnp.maximum(a, 0.0)
    return ops.mm(r * r, w2p)


def _lower_bounds(r0, r1, r2, r3):
    mx = jnp.maximum(jnp.maximum(r0, r1), jnp.maximum(r2, r3))
    e0, e1, e2, e3 = jnp.exp(r0 - mx), jnp.exp(r1 - mx), jnp.exp(r2 - mx), jnp.exp(r3 - mx)
    inv = 1.0 / (e0 + e1 + e2 + e3)
    s1, s2, s3 = e1 * inv, e2 * inv, e3 * inv
    return jnp.zeros_like(r0), s1, s1 + s2, s1 + s2 + s3


def _cp(sem):
    return pltpu.CompilerParams(dimension_semantics=sem, vmem_limit_bytes=VMEM_LIMIT)


def _rows(tm, n):
    return pl.BlockSpec((tm, n), lambda i: (i, 0))


def _full(a):
    nd = len(a.shape)
    return pl.BlockSpec(a.shape, lambda *_: (0,) * nd)


def _sds(shape, dtype=F32):
    return jax.ShapeDtypeStruct(shape, dtype)


def _acc(ref, val, first):
    @pl.when(first)
    def _():
        ref[...] = val

    @pl.when(jnp.logical_not(first))
    def _():
        ref[...] = ref[...] + val


def _f32(ref):
    return ref[...].astype(F32)


MESH = pl.DeviceIdType.MESH
ANY = pl.BlockSpec(memory_space=pl.ANY)


class Rider:
    def __init__(self, arrays, out_shapes, aliases, sems, start, finish):
        self.arrays, self.out_shapes, self.aliases, self.sems = list(arrays), list(out_shapes), dict(aliases), list(sems)
        self.start, self.finish = start, finish


def run_rider(rider, name):
    n_in, n_out = len(rider.arrays), len(rider.out_shapes)

    def body(*refs):
        ins, outs, sems = refs[:n_in], refs[n_in:n_in + n_out], refs[n_in + n_out:]
        rider.start(ins, outs, sems)
        rider.finish(ins, outs, sems)

    return pl.pallas_call(
        body, name=name, in_specs=[ANY] * n_in, out_specs=[ANY] * n_out, out_shape=rider.out_shapes,
        input_output_aliases=rider.aliases, scratch_shapes=rider.sems,
    )(*rider.arrays)


def _merge_riders(riders):
    bounds, a0, o0, s0 = [], 0, 0, 0
    for r in riders:
        bounds.append((a0, o0, s0))
        a0, o0, s0 = a0 + len(r.arrays), o0 + len(r.out_shapes), s0 + len(r.sems)

    def part(k, ins, outs, sems):
        a, o, s = bounds[k]
        r = riders[k]
        return ins[a:a + len(r.arrays)], outs[o:o + len(r.out_shapes)], sems[s:s + len(r.sems)]

    return Rider(
        [x for r in riders for x in r.arrays], [x for r in riders for x in r.out_shapes],
        {bounds[k][0] + i: bounds[k][1] + o for k, r in enumerate(riders) for i, o in r.aliases.items()},
        [x for r in riders for x in r.sems],
        lambda *refs: [r.start(*part(k, *refs)) for k, r in enumerate(riders)],
        lambda *refs: [r.finish(*part(k, *refs)) for k, r in enumerate(riders)])


def _ride(compute, riders, *, name, grid, in_specs, out_specs, out_shape, operands, scratch_shapes=(), sem=None):
    single = not isinstance(out_shape, (list, tuple))
    if single:
        out_specs, out_shape = [out_specs], [out_shape]
    live = [r for r in riders if r is not None]
    if not live:
        res = pl.pallas_call(compute, name=name, grid=grid, in_specs=in_specs, out_specs=out_specs, out_shape=out_shape,
                             scratch_shapes=list(scratch_shapes), compiler_params=_cp(sem))(*operands)
        return (res[0] if single else res), [None] * len(riders)
    rider = live[0] if len(live) == 1 else _merge_riders(live)
    n_in, n_out, n_s = len(in_specs), len(out_specs), len(scratch_shapes)
    r_in, r_out = len(rider.arrays), len(rider.out_shapes)

    def body(*refs):
        ins, rins = refs[:n_in], refs[n_in:n_in + r_in]
        outs = refs[n_in + r_in:n_in + r_in + n_out]
        routs = refs[n_in + r_in + n_out:n_in + r_in + n_out + r_out]
        scr = refs[n_in + r_in + n_out + r_out:n_in + r_in + n_out + r_out + n_s]
        rsems = refs[n_in + r_in + n_out + r_out + n_s:]
        first = functools.reduce(jnp.logical_and, [pl.program_id(a) == 0 for a in range(len(grid))])
        last = functools.reduce(jnp.logical_and, [pl.program_id(a) == grid[a] - 1 for a in range(len(grid))])

        @pl.when(first)
        def _():
            rider.start(rins, routs, rsems)

        compute(*ins, *outs, *scr)

        @pl.when(last)
        def _():
            rider.finish(rins, routs, rsems)

    res = pl.pallas_call(
        body, name=name, grid=grid, in_specs=list(in_specs) + [ANY] * r_in, out_specs=list(out_specs) + [ANY] * r_out,
        out_shape=list(out_shape) + rider.out_shapes,
        input_output_aliases={n_in + k: n_out + v for k, v in rider.aliases.items()},
        scratch_shapes=list(scratch_shapes) + rider.sems, compiler_params=_cp(("arbitrary",) * len(grid)),
    )(*operands, *rider.arrays)
    main, rest, per_rider = res[:n_out], list(res[n_out:]), []
    for r in riders:
        per_rider.append(None if r is None else [rest.pop(0) for _ in r.out_shapes])
    return (main[0] if single else main), per_rider


def inproj_fwd(x, g1, wa, wb, wc):
    s, d = x.shape

    def body(x_ref, g_ref, wa_ref, wb_ref, wc_ref, pa_ref, pb_ref, pc_ref):
        pa, pb, pc = _inproj(PLAIN, x_ref[...], g_ref[...], wa_ref[...], wb_ref[...], wc_ref[...])
        pa_ref[...] = pa
        pb_ref[...] = pb
        pc_ref[...] = pc

    return pl.pallas_call(
        body, name="inproj_fwd", grid=(s // TM,),
        in_specs=[_rows(TM, d), _full(g1), _full(wa), _full(wb), _full(wc)],
        out_specs=[_rows(TM, wa.shape[1]), _rows(TM, wb.shape[1]), _rows(TM, wc.shape[1])],
        out_shape=[_sds((s, wa.shape[1])), _sds((s, wb.shape[1])), _sds((s, wc.shape[1]))],
        compiler_params=_cp(("parallel",)),
    )(x, g1, wa, wb, wc)


def inproj_bwd(x, g1, wa, wb, wc, dpa, dpb, dpc, dres):
    s, d = x.shape

    def body(x_ref, g_ref, wa_ref, wb_ref, wc_ref, dpa_ref, dpb_ref, dpc_ref, dres_ref,
             dx_ref, dg_ref, dwa_ref, dwb_ref, dwc_ref):
        first = pl.program_id(0) == 0
        _, vjp = jax.vjp(functools.partial(_inproj, AD), x_ref[...], g_ref[...],
                         _f32(wa_ref), _f32(wb_ref), _f32(wc_ref))
        dx, dg, dwa, dwb, dwc = vjp((dpa_ref[...], dpb_ref[...], dpc_ref[...]))
        dx_ref[...] = dx + dres_ref[...]
        _acc(dg_ref, dg, first)
        _acc(dwa_ref, dwa, first)
        _acc(dwb_ref, dwb, first)
        _acc(dwc_ref, dwc, first)

    return pl.pallas_call(
        body, name="inproj_bwd", grid=(s // TM,),
        in_specs=[_rows(TM, d), _full(g1), _full(wa), _full(wb), _full(wc),
                  _rows(TM, wa.shape[1]), _rows(TM, wb.shape[1]), _rows(TM, wc.shape[1]), _rows(TM, d)],
        out_specs=[_rows(TM, d), _full(g1), _full(wa), _full(wb), _full(wc)],
        out_shape=[_sds((s, d)), _sds(g1.shape), _sds(wa.shape), _sds(wb.shape), _sds(wc.shape)],
        compiler_params=_cp(("arbitrary",)),
    )(x, g1, wa, wb, wc, dpa, dpb, dpc, dres)


def gm_fwd(pa, vg, ws4, bs, og):
    s = pa.shape[0]
    w = pa.shape[1] // 2

    def body(pa_ref, vg_ref, ws_ref, bs_ref, og_ref, ya_ref):
        bsl = [bs_ref[h] for h in range(4)]
        ya_ref[...] = _gm_chunk(PLAIN, pa_ref[:, 0:w], pa_ref[:, w:2 * w], vg_ref[...], ws_ref[...], bsl, og_ref[...])

    return pl.pallas_call(
        body, name="gm_fwd", grid=(s // CHUNK,),
        in_specs=[_rows(CHUNK, 2 * w), _full(vg), _full(ws4), _full(bs), _full(og)],
        out_specs=_rows(CHUNK, w), out_shape=_sds((s, w)),
        compiler_params=_cp(("parallel",)),
    )(pa, vg, ws4, bs, og)


def gm_bwd(pa, vg, ws4, bs, og, dya):
    s = pa.shape[0]
    w = pa.shape[1] // 2

    def body(pa_ref, vg_ref, ws_ref, bs_ref, og_ref, dya_ref, dpa_ref, dvg_ref, dws_ref, dbs_ref, dog_ref):
        first = pl.program_id(0) == 0
        bsl = [bs_ref[h] for h in range(4)]
        _, vjp = jax.vjp(functools.partial(_gm_chunk, AD), pa_ref[:, 0:w], pa_ref[:, w:2 * w],
                         vg_ref[...], ws_ref[...], bsl, og_ref[...])
        du, dv, dvg, dws, dbs, dog = vjp(dya_ref[...])
        dpa_ref[:, 0:w] = du
        dpa_ref[:, w:2 * w] = dv
        _acc(dvg_ref, dvg, first)
        _acc(dws_ref, dws, first)
        _acc(dog_ref, dog, first)
        for h in range(4):
            _acc(dbs_ref.at[h], dbs[h], first)

    return pl.pallas_call(
        body, name="gm_bwd", grid=(s // CHUNK,),
        in_specs=[_rows(CHUNK, 2 * w), _full(vg), _full(ws4), _full(bs), _full(og), _rows(CHUNK, w)],
        out_specs=[_rows(CHUNK, 2 * w), _full(vg), _full(ws4), _full(bs), _full(og)],
        out_shape=[_sds((s, 2 * w)), _sds(vg.shape), _sds(ws4.shape), _sds(bs.shape), _sds(og.shape)],
        compiler_params=_cp(("arbitrary",)),
    )(pa, vg, ws4, bs, og, dya)


def hg_fwd(pb, lb, og):
    s = pb.shape[0]
    w = pb.shape[1] // 4
    nc = s // CHUNK

    def body(pb_ref, lb_ref, og_ref, yb_ref, states_ref, st_ref):
        @pl.when(pl.program_id(0) == 0)
        def _():
            st_ref[...] = jnp.zeros_like(st_ref)

        st = st_ref[...]
        states_ref[...] = st
        st_new, y = _hg_chunk(PLAIN, st, pb_ref[:, 0:w], pb_ref[:, w:2 * w], pb_ref[:, 2 * w:3 * w],
                              pb_ref[:, 3 * w:4 * w], lb_ref[...], og_ref[...])
        st_ref[...] = st_new
        yb_ref[...] = y

    return pl.pallas_call(
        body, name="hg_fwd", grid=(nc,),
        in_specs=[_rows(CHUNK, 4 * w), _full(lb), _full(og)],
        out_specs=[_rows(CHUNK, w), pl.BlockSpec((None, w, w), lambda i: (i, 0, 0))],
        out_shape=[_sds((s, w)), _sds((nc, w, w))],
        scratch_shapes=[pltpu.VMEM((w, w), F32)],
        compiler_params=_cp(("arbitrary",)),
    )(pb, lb, og)


def hg_bwd(pb, lb, og, states, dyb, riders=()):
    s = pb.shape[0]
    w = pb.shape[1] // 4
    nc = s // CHUNK

    def body(pb_ref, lb_ref, og_ref, states_ref, dyb_ref, dpb_ref, dlb_ref, dog_ref, dst_ref):
        first = pl.program_id(0) == 0

        @pl.when(first)
        def _():
            dst_ref[...] = jnp.zeros_like(dst_ref)

        _, vjp = jax.vjp(functools.partial(_hg_chunk, AD), states_ref[...], pb_ref[:, 0:w], pb_ref[:, w:2 * w],
                         pb_ref[:, 2 * w:3 * w], pb_ref[:, 3 * w:4 * w], lb_ref[...], og_ref[...])
        dst, dq, df, di, dg, dlb, dog = vjp((dst_ref[...], dyb_ref[...]))
        dst_ref[...] = dst
        dpb_ref[:, 0:w] = dq
        dpb_ref[:, w:2 * w] = df
        dpb_ref[:, 2 * w:3 * w] = di
        dpb_ref[:, 3 * w:4 * w] = dg
        _acc(dlb_ref, dlb, first)
        _acc(dog_ref, dog, first)

    rev = lambda i: (nc - 1 - i, 0)
    return _ride(
        body, riders, name="hg_bwd", grid=(nc,),
        in_specs=[pl.BlockSpec((CHUNK, 4 * w), rev), _full(lb), _full(og),
                  pl.BlockSpec((None, w, w), lambda i: (nc - 1 - i, 0, 0)), pl.BlockSpec((CHUNK, w), rev)],
        out_specs=[pl.BlockSpec((CHUNK, 4 * w), rev), _full(lb), _full(og)],
        out_shape=[_sds((s, 4 * w)), _sds(lb.shape), _sds(og.shape)],
        scratch_shapes=[pltpu.VMEM((w, w), F32)],
        operands=(pb, lb, og, states, dyb), sem=("arbitrary",))


def lower_bounds_fwd(hlb):
    def body(h_ref, o_ref):
        outs = _lower_bounds(*[h_ref[pl.ds(i, 1), :] for i in range(DEPTH)])
        for i in range(DEPTH):
            o_ref[pl.ds(i, 1), :] = outs[i]

    return pl.pallas_call(body, name="lower_bounds_fwd", out_shape=_sds(hlb.shape))(hlb)


def lower_bounds_bwd(hlb, dlbs):
    def body(h_ref, d_ref, o_ref):
        _, vjp = jax.vjp(_lower_bounds, *[h_ref[pl.ds(i, 1), :] for i in range(DEPTH)])
        outs = vjp(tuple(d_ref[pl.ds(i, 1), :] for i in range(DEPTH)))
        for i in range(DEPTH):
            o_ref[pl.ds(i, 1), :] = outs[i]

    return pl.pallas_call(body, name="lower_bounds_bwd", out_shape=_sds(hlb.shape))(hlb, dlbs)


def _mla_pre_args(pc_ref, cos_ref, sa_ref, sb_ref, qag_ref, kvag_ref, qg_ref, kg_ref, wq_ref, wk_ref, wv_ref, cast):
    sl = lambda h: slice(h * SLOT, (h + 1) * SLOT)
    ld = (lambda r, h: r[:, sl(h)].astype(F32)) if cast else (lambda r, h: r[:, sl(h)])
    diff = (pc_ref[:, 0:Q_LORA], pc_ref[:, Q_LORA:Q_LORA + KV_LORA], pc_ref[:, Q_LORA + KV_LORA:Q_LORA + 2 * KV_LORA],
            qag_ref[...], kvag_ref[...], qg_ref[...], kg_ref[...],
            [ld(wq_ref, h) for h in range(MLA_HEADS)], [ld(wk_ref, h) for h in range(MLA_HEADS)],
            [ld(wv_ref, h) for h in range(MLA_HEADS)])
    tables = (cos_ref[...], sa_ref[...], sb_ref[...])
    return diff, tables


def _mla_pre_fn(ops, tables, cq, ckv, kpe, qag, kvag, qg, kg, wq, wk, wv):
    return _mla_pre(ops, cq, ckv, kpe, *tables, qag, kvag, qg, kg, wq, wk, wv)


def mla_pre_fwd(pc, cos_t, sin_a, sin_b, qag, kvag, qg, kg, wq, wk, wv):
    s = pc.shape[0]
    hw = MLA_HEADS * SLOT

    def body(pc_ref, cos_ref, sa_ref, sb_ref, qag_ref, kvag_ref, qg_ref, kg_ref, wq_ref, wk_ref, wv_ref,
             q_ref, k_ref, v_ref):
        diff, tables = _mla_pre_args(pc_ref, cos_ref, sa_ref, sb_ref, qag_ref, kvag_ref, qg_ref, kg_ref,
                                     wq_ref, wk_ref, wv_ref, False)
        qs, ks, vs = _mla_pre_fn(PLAIN, tables, *diff)
        ones_lane = (lax.broadcasted_iota(jnp.int32, (1, SLOT), 1) == V_DIM).astype(F32)
        for h in range(MLA_HEADS):
            q_ref[:, h * SLOT:(h + 1) * SLOT] = qs[h].astype(q_ref.dtype)
            k_ref[:, h * SLOT:(h + 1) * SLOT] = ks[h].astype(k_ref.dtype)
            v_ref[:, h * SLOT:(h + 1) * SLOT] = (vs[h] + ones_lane).astype(v_ref.dtype)

    return pl.pallas_call(
        body, name="mla_pre_fwd", grid=(s // TM,),
        in_specs=[_rows(TM, pc.shape[1]), _rows(TM, SLOT), _rows(TM, SLOT), _rows(TM, SLOT),
                  _full(qag), _full(kvag), _full(qg), _full(kg), _full(wq), _full(wk), _full(wv)],
        out_specs=[_rows(TM, hw)] * 3, out_shape=[_sds((s, hw), MXU_DTYPE)] * 3,
        compiler_params=_cp(("parallel",)),
    )(pc, cos_t, sin_a, sin_b, qag, kvag, qg, kg, wq, wk, wv)


def mla_pre_bwd(pc, cos_t, sin_a, sin_b, qag, kvag, qg, kg, wq, wk, wv, dq, dk, dv):
    s = pc.shape[0]
    hw = MLA_HEADS * SLOT

    def body(pc_ref, cos_ref, sa_ref, sb_ref, qag_ref, kvag_ref, qg_ref, kg_ref, wq_ref, wk_ref, wv_ref,
             dq_ref, dk_ref, dv_ref, dpc_ref, dqag_ref, dkvag_ref, dqg_ref, dkg_ref, dwq_ref, dwk_ref, dwv_ref):
        first = pl.program_id(0) == 0
        diff, tables = _mla_pre_args(pc_ref, cos_ref, sa_ref, sb_ref, qag_ref, kvag_ref, qg_ref, kg_ref,
                                     wq_ref, wk_ref, wv_ref, True)
        _, vjp = jax.vjp(functools.partial(_mla_pre_fn, AD, tables), *diff)
        sl = lambda h: slice(h * SLOT, (h + 1) * SLOT)
        cot = ([dq_ref[:, sl(h)] for h in range(MLA_HEADS)], [dk_ref[:, sl(h)] for h in range(MLA_HEADS)],
               [dv_ref[:, sl(h)] for h in range(MLA_HEADS)])
        dcq, dckv, dkpe, dqag, dkvag, dqg, dkg, dwq, dwk, dwv = vjp(cot)
        dpc_ref[:, 0:Q_LORA] = dcq
        dpc_ref[:, Q_LORA:Q_LORA + KV_LORA] = dckv
        dpc_ref[:, Q_LORA + KV_LORA:Q_LORA + 2 * KV_LORA] = dkpe
        _acc(dqag_ref, dqag, first)
        _acc(dkvag_ref, dkvag, first)
        _acc(dqg_ref, dqg, first)
        _acc(dkg_ref, dkg, first)
        for h in range(MLA_HEADS):
            _acc(dwq_ref.at[:, sl(h)], dwq[h], first)
            _acc(dwk_ref.at[:, sl(h)], dwk[h], first)
            _acc(dwv_ref.at[:, sl(h)], dwv[h], first)

    return pl.pallas_call(
        body, name="mla_pre_bwd", grid=(s // TM,),
        in_specs=[_rows(TM, pc.shape[1]), _rows(TM, SLOT), _rows(TM, SLOT), _rows(TM, SLOT),
                  _full(qag), _full(kvag), _full(qg), _full(kg), _full(wq), _full(wk), _full(wv),
                  _rows(TM, hw), _rows(TM, hw), _rows(TM, hw)],
        out_specs=[_rows(TM, pc.shape[1]), _full(qag), _full(kvag), _full(qg), _full(kg),
                   _full(wq), _full(wk), _full(wv)],
        out_shape=[_sds(pc.shape), _sds(qag.shape), _sds(kvag.shape), _sds(qg.shape), _sds(kg.shape),
                   _sds(wq.shape), _sds(wk.shape), _sds(wv.shape)],
        compiler_params=_cp(("arbitrary",)),
    )(pc, cos_t, sin_a, sin_b, qag, kvag, qg, kg, wq, wk, wv, dq, dk, dv)


ATT_SCALE = QK_DIM ** -0.5
NEG_BIG = -1e30


def attn_fwd(q, k, v, riders=()):
    s = q.shape[0]
    nq = s // TQ
    hp = ATT_HEADS_PER_STEP
    sl = lambda j: slice(j * SLOT, (j + 1) * SLOT)

    def body(q_ref, k_ref, v_ref, o_ref, lse_ref):
        qi = pl.program_id(1)
        row = lax.broadcasted_iota(jnp.int32, (TQ, TQ), 0)
        col = lax.broadcasted_iota(jnp.int32, (TQ, TQ), 1)
        lane = lax.broadcasted_iota(jnp.int32, (1, SLOT), 1)
        qs = [q_ref[:, sl(j)] for j in range(hp)]

        def step(ki, carry, masked):
            rk = pl.ds(pl.multiple_of(ki * TQ, TQ), TQ)
            out = []
            for j in range(hp):
                m, acc = carry[j]
                sc = _dot(qs[j], k_ref[rk, sl(j)], NT) * ATT_SCALE
                if masked:
                    sc = jnp.where(row >= col, sc, NEG_BIG)
                m_new = jnp.maximum(m, jnp.max(sc, axis=-1, keepdims=True))
                acc = jnp.exp(m - m_new) * acc + _dot(jnp.exp(sc - m_new), v_ref[rk, sl(j)], NN)
                out.append((m_new, acc))
            return tuple(out)

        init = tuple((jnp.full((TQ, 1), NEG_BIG, F32), jnp.zeros((TQ, SLOT), F32)) for _ in range(hp))
        carry = lax.fori_loop(0, qi, lambda ki, cr: step(ki, cr, False), init)
        carry = step(qi, carry, True)
        for j in range(hp):
            m, acc = carry[j]
            l = jnp.sum(jnp.where(lane == V_DIM, acc, 0.0), axis=-1, keepdims=True)
            o_ref[:, sl(j)] = jnp.where(lane < V_DIM, acc / l, 0.0)
            lse_ref[j] = m + jnp.log(l)

    head_col = pl.BlockSpec((s, hp * SLOT), lambda g, i: (0, g))
    tile = pl.BlockSpec((TQ, hp * SLOT), lambda g, i: (i, g))
    return _ride(
        body, riders, name="attn_fwd", grid=(MLA_HEADS // hp, nq),
        in_specs=[tile, head_col, head_col],
        out_specs=[tile, pl.BlockSpec((hp, TQ, 1), lambda g, i: (g, i, 0))],
        out_shape=[_sds((s, MLA_HEADS * SLOT)), _sds((MLA_HEADS, s, 1))],
        operands=(q, k, v), sem=("parallel", "parallel"))


def attn_bwd(q, k, v, o, do, lse, riders=()):
    s = q.shape[0]
    nq = s // TQ
    hp = ATT_HEADS_PER_STEP
    sl = lambda j: slice(j * SLOT, (j + 1) * SLOT)

    def body(q_ref, k_ref, v_ref, o_ref, do_ref, lse_ref, dq_ref, dk_ref, dv_ref, delta_ref):
        ki = pl.program_id(1)
        row = lax.broadcasted_iota(jnp.int32, (TQ, TQ), 0)
        col = lax.broadcasted_iota(jnp.int32, (TQ, TQ), 1)

        @pl.when(ki == 0)
        def _():
            dq_ref[...] = jnp.zeros_like(dq_ref)

            def prep(i, c):
                rows = pl.ds(pl.multiple_of(i * TQ, TQ), TQ)
                for j in range(hp):
                    delta_ref[j, rows, :] = jnp.sum(do_ref[rows, sl(j)] * o_ref[rows, sl(j)], axis=-1, keepdims=True)
                return c

            lax.fori_loop(0, nq, prep, 0)

        kks = [k_ref[:, sl(j)] for j in range(hp)]
        vvs = [v_ref[:, sl(j)] for j in range(hp)]

        def step(qi, carry, masked):
            rq = pl.ds(pl.multiple_of(qi * TQ, TQ), TQ)
            out = []
            for j in range(hp):
                dk, dv = carry[j]
                qq = q_ref[rq, sl(j)]
                dd = do_ref[rq, sl(j)]
                sc = _dot(qq, kks[j], NT) * ATT_SCALE
                if masked:
                    sc = jnp.where(row >= col, sc, NEG_BIG)
                p = jnp.exp(sc - lse_ref[j, rq, :])
                dv = dv + _dot(p, dd, TN)
                ds = p * (_dot(dd, vvs[j], NT) - delta_ref[j, rq, :]) * ATT_SCALE
                dk = dk + _dot(ds, qq, TN)
                dq_ref[rq, sl(j)] = dq_ref[rq, sl(j)] + _dot(ds, kks[j], NN)
                out.append((dk, dv))
            return tuple(out)

        zero = jnp.zeros((TQ, SLOT), F32)
        carry = step(ki, tuple((zero, zero) for _ in range(hp)), True)
        carry = lax.fori_loop(ki + 1, nq, lambda qi, cr: step(qi, cr, False), carry)
        for j in range(hp):
            dk_ref[:, sl(j)] = carry[j][0]
            dv_ref[:, sl(j)] = carry[j][1]

    head_col = pl.BlockSpec((s, hp * SLOT), lambda g, i: (0, g))
    tile = pl.BlockSpec((TQ, hp * SLOT), lambda g, i: (i, g))
    return _ride(
        body, riders, name="attn_bwd", grid=(MLA_HEADS // hp, nq),
        in_specs=[head_col, tile, tile, head_col, head_col, pl.BlockSpec((hp, s, 1), lambda g, i: (g, 0, 0))],
        out_specs=[head_col, tile, tile],
        out_shape=[_sds((s, MLA_HEADS * SLOT))] * 3,
        scratch_shapes=[pltpu.VMEM((hp, s, 1), F32)],
        operands=(q, k, v, o, do, lse), sem=("arbitrary", "arbitrary"))


def _outproj_args(ya_ref, yb_ref, o_ref, mog_ref, woa_ref, wob_ref, woc_ref, cast):
    sl = lambda h: slice(h * SLOT, (h + 1) * SLOT)
    ldw = (lambda r: r[...].astype(F32)) if cast else (lambda r: r[...])
    ldc = (lambda h: woc_ref[sl(h), :].astype(F32)) if cast else (lambda h: woc_ref[sl(h), :])
    return (ya_ref[...], yb_ref[...], [o_ref[:, sl(h)] for h in range(MLA_HEADS)],
            [mog_ref[:, sl(h)] for h in range(MLA_HEADS)], ldw(woa_ref), ldw(wob_ref),
            [ldc(h) for h in range(MLA_HEADS)])


def outproj_fwd(x, ya, yb, o, mog, woa, wob, woc):
    s, d = x.shape

    def body(x_ref, ya_ref, yb_ref, o_ref, mog_ref, woa_ref, wob_ref, woc_ref, x1_ref):
        x1_ref[...] = _outproj(PLAIN, x_ref[...], *_outproj_args(ya_ref, yb_ref, o_ref, mog_ref, woa_ref, wob_ref,
                                                                  woc_ref, False))

    return pl.pallas_call(
        body, name="outproj_fwd", grid=(s // TM,),
        in_specs=[_rows(TM, d), _rows(TM, ya.shape[1]), _rows(TM, yb.shape[1]), _rows(TM, o.shape[1]),
                  _full(mog), _full(woa), _full(wob), _full(woc)],
        out_specs=_rows(TM, d), out_shape=_sds((s, d)),
        compiler_params=_cp(("parallel",)),
    )(x, ya, yb, o, mog, woa, wob, woc)


def outproj_bwd(ya, yb, o, mog, woa, wob, woc, dx2, dx1p, riders=()):
    s, d = dx2.shape
    npart = dx1p.shape[0]

    def body(ya_ref, yb_ref, o_ref, mog_ref, woa_ref, wob_ref, woc_ref, dx2_ref, dx1p_ref,
             dx1_ref, dya_ref, dyb_ref, do_ref, dmog_ref, dwoa_ref, dwob_ref, dwoc_ref):
        first = pl.program_id(0) == 0
        sl = lambda h: slice(h * SLOT, (h + 1) * SLOT)
        dx1 = dx2_ref[...]
        for p in range(npart):
            dx1 = dx1 + dx1p_ref[p]
        dx1_ref[...] = dx1
        args = _outproj_args(ya_ref, yb_ref, o_ref, mog_ref, woa_ref, wob_ref, woc_ref, True)
        _, vjp = jax.vjp(lambda *a: _outproj(AD, jnp.zeros_like(dx1), *a), *args)
        dya, dyb, do, dmog, dwoa, dwob, dwoc = vjp(dx1)
        dya_ref[...] = dya
        dyb_ref[...] = dyb
        _acc(dwoa_ref, dwoa, first)
        _acc(dwob_ref, dwob, first)
        for h in range(MLA_HEADS):
            do_ref[:, sl(h)] = do[h]
            _acc(dmog_ref.at[:, sl(h)], dmog[h], first)
            _acc(dwoc_ref.at[sl(h), :], dwoc[h], first)

    return _ride(
        body, riders, name="outproj_bwd", grid=(s // TM,),
        in_specs=[_rows(TM, ya.shape[1]), _rows(TM, yb.shape[1]), _rows(TM, o.shape[1]),
                  _full(mog), _full(woa), _full(wob), _full(woc), _rows(TM, d),
                  pl.BlockSpec((npart, TM, d), lambda i: (0, i, 0))],
        out_specs=[_rows(TM, d), _rows(TM, ya.shape[1]), _rows(TM, yb.shape[1]), _rows(TM, o.shape[1]),
                   _full(mog), _full(woa), _full(wob), _full(woc)],
        out_shape=[_sds((s, d)), _sds(ya.shape), _sds(yb.shape), _sds(o.shape),
                   _sds(mog.shape), _sds(woa.shape), _sds(wob.shape), _sds(woc.shape)],
        operands=(ya, yb, o, mog, woa, wob, woc, dx2, dx1p), sem=("arbitrary",))


def ffn_fwd(x1, g2, w1, w2, riders=()):
    s, d = x1.shape
    npart, _, fs = w1.shape

    def body(x1_ref, g_ref, w1_ref, w2_ref, x2_ref):
        p = pl.program_id(1)
        x1v = x1_ref[...]
        part = _ffn_part(PLAIN, x1v, g_ref[...], w1_ref[...], w2_ref[...])

        @pl.when(p == 0)
        def _():
            x2_ref[...] = x1v + part

        @pl.when(p != 0)
        def _():
            x2_ref[...] = x2_ref[...] + part

    tm = TM_FFN
    return _ride(
        body, riders, name="ffn_fwd", grid=(s // tm, npart),
        in_specs=[pl.BlockSpec((tm, d), lambda i, p: (i, 0)), pl.BlockSpec(g2.shape, lambda i, p: (0, 0)),
                  pl.BlockSpec((None, d, fs), lambda i, p: (p, 0, 0)), pl.BlockSpec((None, fs, d), lambda i, p: (p, 0, 0))],
        out_specs=pl.BlockSpec((tm, d), lambda i, p: (i, 0)), out_shape=_sds((s, d)),
        operands=(x1, g2, w1, w2), sem=("parallel", "arbitrary"))


def ffn_bwd(x1, g2, w1, w2, dx2, riders=()):
    s, d = x1.shape
    npart, _, fs = w1.shape

    def body(x1_ref, g_ref, w1_ref, w2_ref, dx2_ref, dx1p_ref, dg_ref, dw1_ref, dw2_ref):
        p = pl.program_id(0)
        i = pl.program_id(1)
        _, vjp = jax.vjp(functools.partial(_ffn_part, AD), x1_ref[...], g_ref[...], _f32(w1_ref), _f32(w2_ref))
        dx1, dg, dw1, dw2 = vjp(dx2_ref[...])
        dx1p_ref[...] = dx1
        _acc(dg_ref, dg, (p == 0) & (i == 0))
        _acc(dw1_ref, dw1, i == 0)
        _acc(dw2_ref, dw2, i == 0)

    tm = TM_FFN
    return _ride(
        body, riders, name="ffn_bwd", grid=(npart, s // tm),
        in_specs=[pl.BlockSpec((tm, d), lambda p, i: (i, 0)), pl.BlockSpec(g2.shape, lambda p, i: (0, 0)),
                  pl.BlockSpec((None, d, fs), lambda p, i: (p, 0, 0)), pl.BlockSpec((None, fs, d), lambda p, i: (p, 0, 0)),
                  pl.BlockSpec((tm, d), lambda p, i: (i, 0))],
        out_specs=[pl.BlockSpec((None, tm, d), lambda p, i: (p, i, 0)), pl.BlockSpec(g2.shape, lambda p, i: (0, 0)),
                   pl.BlockSpec((None, d, fs), lambda p, i: (p, 0, 0)), pl.BlockSpec((None, fs, d), lambda p, i: (p, 0, 0))],
        out_shape=[_sds((npart, s, d)), _sds(g2.shape), _sds(w1.shape), _sds(w2.shape)],
        operands=(x1, g2, w1, w2, dx2), sem=("arbitrary", "arbitrary"))


def loss_head(y, target):
    s, d = y.shape

    def body(y_ref, t_ref, dy_ref, loss_ref):
        err = y_ref[...] - t_ref[...]
        dy_ref[...] = err * (1.0 / d)
        part = jnp.sum(jnp.sum(err * err, axis=-1, keepdims=True), axis=0, keepdims=True) * (0.5 / d)
        _acc(loss_ref, jnp.broadcast_to(part, loss_ref.shape), pl.program_id(0) == 0)

    return pl.pallas_call(
        body, name="loss_head", grid=(s // TM,),
        in_specs=[_rows(TM, d), _rows(TM, d)],
        out_specs=[_rows(TM, d), pl.BlockSpec((1, SLOT), lambda i: (0, 0))],
        out_shape=[_sds((s, d)), _sds((1, SLOT))],
        compiler_params=_cp(("arbitrary",)),
    )(y, target)


def _row_block(r):
    for b in (512, 256, 128, 64, 32, 16, 8):
        if r % b == 0:
            return b
    return r


def sum_cores(a, got, half, me):
    nch, _, r, c = a.shape
    br = _row_block(r)

    def body(sp_ref, a_ref, g_ref, wire_ref, own_ref):
        tot = a_ref[...] + g_ref[...]
        wire_ref[...] = tot.astype(wire_ref.dtype)

        @pl.when(pl.program_id(1) == sp_ref[1])
        def _():
            own_ref[...] = tot

    grid_spec = pltpu.PrefetchScalarGridSpec(
        num_scalar_prefetch=1, grid=(r // br, nch),
        in_specs=[pl.BlockSpec((None, None, br, c), lambda i, p, sp: (p, sp[0], i, 0)),
                  pl.BlockSpec((None, br, c), lambda i, p, sp: (p, i, 0))],
        out_specs=[pl.BlockSpec((None, br, c), lambda i, p, sp: (p, i, 0)), pl.BlockSpec((br, c), lambda i, p, sp: (i, 0))])
    return pl.pallas_call(body, name="sum_cores", grid_spec=grid_spec, out_shape=[_sds((nch, r, c), BF16), _sds((r, c))],
                          compiler_params=_cp(("parallel", "arbitrary")))(jnp.stack([half, me]).astype(jnp.int32), a, got)


def sum_chips(own, recv, half):
    r, c = own.shape
    br = _row_block(r)

    def body(sp_ref, own_ref, r0_ref, r1_ref, r2_ref, out_ref):
        del sp_ref
        out_ref[...] = ((own_ref[...] + r0_ref[...].astype(F32)) + r1_ref[...].astype(F32)) + r2_ref[...].astype(F32)

    grid_spec = pltpu.PrefetchScalarGridSpec(
        num_scalar_prefetch=1, grid=(r // br,),
        in_specs=[pl.BlockSpec((br, c), lambda i, sp: (i, 0))]
        + [pl.BlockSpec((None, br, c), functools.partial(lambda i, sp, j: (j, i, 0), j=j)) for j in range(3)],
        out_specs=pl.BlockSpec((None, br, c), lambda i, sp: (sp[0], i, 0)))
    return pl.pallas_call(body, name="sum_chips", grid_spec=grid_spec, out_shape=_sds((2, r, c)),
                          compiler_params=_cp(("parallel",)))(half.reshape(1).astype(jnp.int32), own, recv, recv, recv)


def adamw(w, g, m, v, name):
    r, c = w.shape
    br = _row_block(r)
    c1 = 1.0 / (1.0 - ADAM_B1 ** ADAM_STEP)
    c2 = 1.0 / (1.0 - ADAM_B2 ** ADAM_STEP)

    def body(w_ref, g_ref, m_ref, v_ref, d_ref, nm_ref, nv_ref):
        gg = g_ref[...]
        nm = ADAM_B1 * m_ref[...] + (1.0 - ADAM_B1) * gg
        nv = ADAM_B2 * v_ref[...] + (1.0 - ADAM_B2) * (gg * gg)
        d_ref[...] = -ADAM_LR * ((nm * c1) / (jnp.sqrt(nv * c2) + ADAM_EPS) + ADAM_WD * w_ref[...])
        nm_ref[...] = nm
        nv_ref[...] = nv

    return pl.pallas_call(
        body, name=name, grid=(r // br,), in_specs=[_rows(br, c)] * 4, out_specs=[_rows(br, c)] * 3,
        out_shape=[_sds((r, c))] * 3, compiler_params=_cp(("parallel",)),
    )(w, g, m, v)


def _place():
    x, y, c = lax.axis_index("x"), lax.axis_index("y"), lax.axis_index("c")
    chips = [(1 - x, y), (x, 1 - y), (1 - x, 1 - y)]
    return x, y, c, chips


def _remote(src, dst, send_sem, recv_sem, to):
    return pltpu.make_async_remote_copy(src_ref=src, dst_ref=dst, send_sem=send_sem, recv_sem=recv_sem,
                                        device_id=to, device_id_type=MESH)


def gather_rider(arrs):
    n = len(arrs)
    me_chip = 2 * lax.axis_index("x") + lax.axis_index("y")
    bufs = [lax.dynamic_update_index_in_dim(lax.empty((N_CHIPS,) + a.shape, a.dtype), a, me_chip, 0) for a in arrs]

    def plan(ins, outs, sems):
        send_sems, recv_sems = sems
        x, y, c, chips = _place()
        me = 2 * x + y
        half, other, sibling = pl.ds(2 * c, 2), pl.ds(2 - 2 * c, 2), (x, y, 1 - c)
        cp = lambda i, k, src, dst, to: _remote(src, dst, send_sems.at[i, k], recv_sems.at[i, k], to)
        pairs = [(i, j, cx, cy) for i in range(n) for j, (cx, cy) in enumerate(chips)]
        blk = lambda i, cx, cy, part: outs[i].at[2 * cx + cy, part]
        first = lambda: [cp(i, j, ins[i].at[half], outs[i].at[me, half], (cx, cy, c)) for i, j, cx, cy in pairs]
        landed = lambda: [cp(i, j, blk(i, cx, cy, half), blk(i, cx, cy, half), (cx, cy, c)) for i, j, cx, cy in pairs]
        passed = lambda: [cp(i, 3 + j, blk(i, cx, cy, half), blk(i, cx, cy, half), sibling) for i, j, cx, cy in pairs]
        from_sibling = lambda: [cp(i, 3 + j, blk(i, cx, cy, other), blk(i, cx, cy, other), sibling) for i, j, cx, cy in pairs]
        return first, landed, passed, from_sibling

    def start(ins, outs, sems):
        for cp in plan(ins, outs, sems)[0]():
            cp.start()

    def finish(ins, outs, sems):
        first, landed, passed, from_sibling = plan(ins, outs, sems)
        forwards = passed()
        for a, b in zip(landed(), forwards):
            a.wait_recv()
            b.start()
        for cp in from_sibling():
            cp.wait_recv()
        for cp in first() + forwards:
            cp.wait_send()

    return Rider(list(arrs) + bufs, [_sds((N_CHIPS,) + a.shape, a.dtype) for a in arrs], {n + i: i for i in range(n)},
                 [pltpu.SemaphoreType.DMA((n, 6)), pltpu.SemaphoreType.DMA((n, 6))], start, finish)


class Reducer:
    def __init__(self, arrs):
        self.a = list(arrs)
        self.n = len(self.a)
        self.c = lax.axis_index("c")
        self.me = 2 * lax.axis_index("x") + lax.axis_index("y")

    def swap_rider(self):
        n = self.n

        def plan(ins, outs, sems):
            x, y, c, _ = _place()
            return [_remote(ins[i].at[p, 1 - c], outs[i].at[p], sems[0].at[i, p], sems[1].at[i, p], (x, y, 1 - c))
                    for i in range(n) for p in range(N_CHIPS)]

        return Rider(self.a, [_sds((N_CHIPS,) + a.shape[2:]) for a in self.a], {},
                     [pltpu.SemaphoreType.DMA((n, N_CHIPS)), pltpu.SemaphoreType.DMA((n, N_CHIPS))],
                     lambda *r: [cp.start() for cp in plan(*r)], lambda *r: [cp.wait() for cp in plan(*r)])

    def after_swap(self, got):
        pairs = [sum_cores(a, g, self.c, self.me) for a, g in zip(self.a, got)]
        self.wire, self.own = [p[0] for p in pairs], [p[1] for p in pairs]

    def scatter_rider(self):
        n = self.n

        def plan(ins, outs, sems):
            x, y, c, chips = _place()
            return [_remote(ins[i].at[2 * cx + cy], outs[i].at[j], sems[0].at[i, j], sems[1].at[i, j], (cx, cy, c))
                    for i in range(n) for j, (cx, cy) in enumerate(chips)]

        return Rider(self.wire, [_sds((3,) + w.shape[1:], w.dtype) for w in self.wire], {},
                     [pltpu.SemaphoreType.DMA((n, 3)), pltpu.SemaphoreType.DMA((n, 3))],
                     lambda *r: [cp.start() for cp in plan(*r)], lambda *r: [cp.wait() for cp in plan(*r)])

    def after_scatter(self, recv):
        self.full = [sum_chips(o, r, self.c) for o, r in zip(self.own, recv)]

    def share_rider(self):
        n = self.n

        def plan(ins, outs, sems):
            x, y, c, _ = _place()
            return [_remote(ins[i].at[c], outs[i].at[c], sems[0].at[i], sems[1].at[i], (x, y, 1 - c)) for i in range(n)]

        return Rider(self.full, [_sds(f.shape) for f in self.full], {i: i for i in range(n)},
                     [pltpu.SemaphoreType.DMA((n,)), pltpu.SemaphoreType.DMA((n,))],
                     lambda *r: [cp.start() for cp in plan(*r)], lambda *r: [cp.wait() for cp in plan(*r)])

    def run(self):
        self.after_swap(run_rider(self.swap_rider(), "swap_halves"))
        self.after_scatter(run_rider(self.scatter_rider(), "scatter_chips"))
        return run_rider(self.share_rider(), "share_halves")


def _pad_slots(a, live):
    lead = a.shape[:-1]
    a = a.reshape(lead + (MLA_HEADS, live))
    a = jnp.pad(a, [(0, 0)] * len(lead) + [(0, 0), (0, SLOT - live)])
    return a.reshape(lead + (MLA_HEADS * SLOT,))


def _unpad_slots(a, live):
    lead = a.shape[:-1]
    return a.reshape(lead + (MLA_HEADS, SLOT))[..., :live].reshape(lead + (MLA_HEADS * live,))


def _rope_tables(positions, s):
    half = QK_ROPE // 2
    inv_freq = ROPE_THETA ** (-jnp.arange(half, dtype=F32) / half)
    ang = positions.reshape(s).astype(F32)[:, None] * inv_freq[None, :]
    cos, sin = jnp.cos(ang), jnp.sin(ang)
    one = jnp.ones((s, QK_NOPE), F32)
    z64, z16, z32 = jnp.zeros((s, QK_NOPE), F32), jnp.zeros((s, half), F32), jnp.zeros((s, SLOT - QK_DIM), F32)
    cos_t = jnp.concatenate([one, cos, cos, z32], axis=1)
    sin_a = jnp.concatenate([z64, -sin, z16, z32], axis=1)
    sin_b = jnp.concatenate([z64, z16, sin, z32], axis=1)
    return cos_t, sin_a, sin_b


def _layer_weights(full, small, l):
    w_in = jnp.concatenate([full["w_in"][p] for p in range(N_CHIPS)], axis=1)
    wc = jnp.pad(w_in[:, 1536:], ((0, 0), (0, 512 - (w_in.shape[1] - 1536))))
    w_uq = jnp.concatenate([full["mla_w_uq"][p] for p in range(N_CHIPS)], axis=1)
    w_ukv = jnp.concatenate([full["mla_w_ukv"][p] for p in range(N_CHIPS)], axis=1)
    ukv = w_ukv.reshape(KV_LORA, MLA_HEADS, QK_NOPE + V_DIM)
    w_out = jnp.concatenate([full["w_out"][p] for p in range(N_CHIPS)], axis=0)
    woc = w_out[512:].reshape(MLA_HEADS, V_DIM, D_MODEL)
    woc = jnp.pad(woc, ((0, 0), (0, SLOT - V_DIM), (0, 0))).reshape(MLA_HEADS * SLOT, D_MODEL)
    row = lambda a: a.reshape(1, -1)
    return dict(
        g1=row(small["norm1_gain"][l]), wa=w_in[:, :512], wb=w_in[:, 512:1536], wc=wc,
        vg=row(small["gm_v_gain"][l]), ws=small["gm_w_s"][l], bs=small["gm_b_s"][l].reshape(4, CHUNK, 1),
        gog=row(small["gm_out_gain"][l]), hog=row(small["hg_out_gain"][l]),
        qag=row(small["mla_q_a_gain"][l]), kvag=row(small["mla_kv_a_gain"][l]),
        qg=row(jnp.pad(small["mla_q_gain"][l], (0, SLOT - QK_DIM))), kg=row(jnp.pad(small["mla_k_gain"][l], (0, SLOT - QK_DIM))),
        wq=_pad_slots(w_uq, QK_DIM), wk=_pad_slots(ukv[..., :QK_NOPE].reshape(KV_LORA, -1), QK_NOPE),
        wv=_pad_slots(ukv[..., QK_NOPE:].reshape(KV_LORA, -1), V_DIM),
        mog=row(_pad_slots(small["mla_out_gain"][l], V_DIM)),
        woa=w_out[:256], wob=w_out[256:512], woc=woc,
        g2=row(small["norm2_gain"][l]),
    )


def _shard_cols(a):
    r, c4 = a.shape
    return a.reshape(r, N_CHIPS, c4 // N_CHIPS).transpose(1, 0, 2)


def local_step(x, positions, target, small, comm):
    s = x.shape[0]
    cos_t, sin_a, sin_b = _rope_tables(positions, s)
    lbs = lower_bounds_fwd(small["hg_lower_bound"])
    lw, saved = [], []
    for l in range(DEPTH):
        w = _layer_weights(comm.part(l, "mix"), small, l)
        lw.append(w)
        lb = lbs[l].reshape(1, -1)
        pa, pb, pc = inproj_fwd(x, w["g1"], w["wa"], w["wb"], w["wc"])
        ya = gm_fwd(pa, w["vg"], w["ws"], w["bs"], w["gog"])
        yb, states = hg_fwd(pb, lb, w["hog"])
        q, k, v = mla_pre_fwd(pc, cos_t, sin_a, sin_b, w["qag"], w["kvag"], w["qg"], w["kg"], w["wq"], w["wk"], w["wv"])
        rider = comm.gather_rider(l, "ffn")
        (o, lse), got = attn_fwd(q, k, v, [rider])
        comm.gathered(l, "ffn", got[0])
        x1 = outproj_fwd(x, ya, yb, o, w["mog"], w["woa"], w["wob"], w["woc"])
        ffn_w = comm.part(l, "ffn")
        w["w1"], w["w2"] = ffn_w["w_ff1"], ffn_w["w_ff2"]
        rider = comm.gather_rider(l + 1, "mix") if l + 1 < DEPTH else None
        x2, got = ffn_fwd(x1, w["g2"], w["w1"], w["w2"], [rider])
        comm.gathered(l + 1, "mix", got[0])
        saved.append(dict(x=x, pa=pa, pb=pb, pc=pc, ya=ya, yb=yb, states=states, q=q, k=k, v=v, o=o, lse=lse, x1=x1, lb=lb))
        x = x2
    dx, loss_part = loss_head(x, target)
    groups = [dict() for _ in range(DEPTH)]
    sm = {n: [None] * DEPTH for n in ("norm1_gain", "gm_v_gain", "gm_w_s", "gm_b_s", "gm_out_gain", "hg_out_gain",
                                       "mla_q_a_gain", "mla_kv_a_gain", "mla_q_gain", "mla_k_gain", "mla_out_gain",
                                       "norm2_gain")}
    dlbs = [None] * DEPTH
    halves = lambda g: g.reshape(N_CHIPS, 2, g.shape[1] // 2, g.shape[2])
    take = lambda red, f: None if red is None else f(red)
    red_mix = None
    for l in reversed(range(DEPTH)):
        w, a = lw[l], saved[l]
        (dx1p, dg2, dw1, dw2), got = ffn_bwd(a["x1"], w["g2"], w["w1"], w["w2"], dx, [take(red_mix, Reducer.swap_rider)])
        if red_mix:
            red_mix.after_swap(got[0])
        ffn_arrs = [halves(dw1), halves(dw2)]
        red_ffn = comm.reducer(ffn_arrs)
        (dx1, dya, dyb, do, dmog, dwoa, dwob, dwoc), got = outproj_bwd(
            a["ya"], a["yb"], a["o"], w["mog"], w["woa"], w["wob"], w["woc"], dx, dx1p, [take(red_ffn, Reducer.swap_rider)])
        if red_ffn:
            red_ffn.after_swap(got[0])
        (dq, dk, dv), got = attn_bwd(a["q"], a["k"], a["v"], a["o"], do, a["lse"],
                                     [take(red_mix, Reducer.scatter_rider), take(red_ffn, Reducer.scatter_rider)])
        for red, g in zip((red_mix, red_ffn), got):
            if red:
                red.after_scatter(g)
        dpc, dqag, dkvag, dqg, dkg, dwq, dwk, dwv = mla_pre_bwd(a["pc"], cos_t, sin_a, sin_b, w["qag"], w["kvag"], w["qg"],
                                                                  w["kg"], w["wq"], w["wk"], w["wv"], dq, dk, dv)
        (dpb, dlb, dhog), got = hg_bwd(a["pb"], a["lb"], w["hog"], a["states"], dyb,
                                       [take(red_mix, Reducer.share_rider), take(red_ffn, Reducer.share_rider)])
        if red_mix:
            groups[l + 1].update(zip(MIX, got[0]))
        groups[l].update(zip(FFN, got[1] if red_ffn else ffn_arrs))
        dpa, dvg, dws, dbs, dgog = gm_bwd(a["pa"], w["vg"], w["ws"], w["bs"], w["gog"], dya)
        dx, dg1, dwa, dwb, dwc = inproj_bwd(a["x"], w["g1"], w["wa"], w["wb"], w["wc"], dpa, dpb, dpc, dx1)
        dukv = jnp.concatenate([dwk.reshape(KV_LORA, MLA_HEADS, SLOT)[..., :QK_NOPE],
                                dwv.reshape(KV_LORA, MLA_HEADS, SLOT)[..., :V_DIM]], axis=-1)
        dwo = jnp.concatenate([dwoa, dwob, dwoc.reshape(MLA_HEADS, SLOT, D_MODEL)[:, :V_DIM].reshape(-1, D_MODEL)], axis=0)
        mix_arrs = [halves(_shard_cols(jnp.concatenate([dwa, dwb, dwc[:, :1952 - 1536]], axis=1))),
                    halves(_shard_cols(_unpad_slots(dwq, QK_DIM))), halves(_shard_cols(dukv.reshape(KV_LORA, -1))),
                    halves(dwo.reshape(N_CHIPS, -1, D_MODEL))]
        red_mix = comm.reducer(mix_arrs) if l > 0 else None
        if red_mix is None:
            groups[l].update(zip(MIX, mix_arrs))
        sm["norm1_gain"][l] = dg1[0]
        sm["gm_v_gain"][l] = dvg[0]
        sm["gm_w_s"][l] = dws
        sm["gm_b_s"][l] = dbs[..., 0]
        sm["gm_out_gain"][l] = dgog[0]
        sm["hg_out_gain"][l] = dhog[0]
        sm["mla_q_a_gain"][l] = dqag[0]
        sm["mla_kv_a_gain"][l] = dkvag[0]
        sm["mla_q_gain"][l] = dqg[0, :QK_DIM]
        sm["mla_k_gain"][l] = dkg[0, :QK_DIM]
        sm["mla_out_gain"][l] = _unpad_slots(dmog[0], V_DIM)
        sm["norm2_gain"][l] = dg2[0]
        dlbs[l] = dlb[0]
    sm = {n: jnp.stack(v) for n, v in sm.items()}
    sm["hg_lower_bound"] = lower_bounds_bwd(small["hg_lower_bound"], jnp.stack(dlbs))
    return loss_part, dx, groups, sm


MIX = ("w_in", "mla_w_uq", "mla_w_ukv", "w_out")
FFN = ("w_ff1", "w_ff2")
BIG = MIX + FFN
PARTS = {"mix": MIX, "ffn": FFN}
SMALL = ("norm1_gain", "gm_v_gain", "gm_w_s", "gm_b_s", "gm_out_gain", "hg_lower_bound", "hg_out_gain",
         "mla_q_a_gain", "mla_kv_a_gain", "mla_q_gain", "mla_k_gain", "mla_out_gain", "norm2_gain")
ORDER = ("norm1_gain", "w_in", "gm_v_gain", "gm_w_s", "gm_b_s", "gm_out_gain", "hg_lower_bound", "hg_out_gain",
         "mla_q_a_gain", "mla_w_uq", "mla_kv_a_gain", "mla_w_ukv", "mla_q_gain", "mla_k_gain", "mla_out_gain",
         "w_out", "norm2_gain", "w_ff1", "w_ff2")
PACK_ROWS = 288


def _pack(arrs):
    flat = jnp.concatenate([a.reshape(-1) for a in arrs])
    total = 2 * N_CHIPS * PACK_ROWS * SLOT
    return jnp.pad(flat, (0, total - flat.shape[0]))


def _unpack(flat, shapes):
    out, off = [], 0
    for sh in shapes:
        size = 1
        for d in sh:
            size *= d
        out.append(flat[off:off + size].reshape(sh))
        off += size
    return out


class ChipComm:
    def __init__(self, shards):
        self.shards = shards
        self.full = {}

    def gather_rider(self, l, part):
        return gather_rider([self.shards[n][l].astype(MXU_DTYPE).reshape(4, self.shards[n].shape[1] // 4, -1)
                             for n in PARTS[part]])

    def gathered(self, l, part, outs):
        if outs is not None:
            self.full[l, part] = {n: o.reshape((N_CHIPS,) + self.shards[n].shape[1:]) for n, o in zip(PARTS[part], outs)}

    def part(self, l, part):
        if (l, part) not in self.full:
            self.gathered(l, part, run_rider(self.gather_rider(l, part), "gather_weights"))
        return self.full[l, part]

    def reducer(self, arrs):
        return Reducer(arrs)


def kernel(x, positions, norm1_gain, w_in, gm_v_gain, gm_w_s, gm_b_s, gm_out_gain, hg_lower_bound, hg_out_gain, mla_q_a_gain, mla_w_uq, mla_kv_a_gain, mla_w_ukv, mla_q_gain, mla_k_gain, mla_out_gain, w_out, norm2_gain, w_ff1, w_ff2, loss_target, m_norm1_gain, m_w_in, m_gm_v_gain, m_gm_w_s, m_gm_b_s, m_gm_out_gain, m_hg_lower_bound, m_hg_out_gain, m_mla_q_a_gain, m_mla_w_uq, m_mla_kv_a_gain, m_mla_w_ukv, m_mla_q_gain, m_mla_k_gain, m_mla_out_gain, m_w_out, m_norm2_gain, m_w_ff1, m_w_ff2, v_norm1_gain, v_w_in, v_gm_v_gain, v_gm_w_s, v_gm_b_s, v_gm_out_gain, v_hg_lower_bound, v_hg_out_gain, v_mla_q_a_gain, v_mla_w_uq, v_mla_kv_a_gain, v_mla_w_ukv, v_mla_q_gain, v_mla_k_gain, v_mla_out_gain, v_w_out, v_norm2_gain, v_w_ff1, v_w_ff2):
    given = dict(locals())
    weights = {n: given[n] for n in ORDER}
    moms = {n: given["m_" + n] for n in ORDER}
    vars_ = {n: given["v_" + n] for n in ORDER}
    s, d = x.shape[1], x.shape[2]

    small = {n: weights[n] for n in SMALL}
    comm = ChipComm({n: weights[n] for n in BIG})
    loss_part, dx, groups, small_g = local_step(x.reshape(s, d), positions, loss_target.reshape(s, d), small, comm)
    loss = lax.psum(loss_part[0, 0], ("x", "y", "c"))

    pack_g = _pack([small_g[n] for n in SMALL]).reshape(N_CHIPS, 2, PACK_ROWS, SLOT)
    last = Reducer([groups[0][n] for n in MIX] + [pack_g]).run()
    groups[0].update(zip(MIX, last[:-1]))
    pack_full = run_rider(gather_rider([last[-1].reshape(4, PACK_ROWS // 2, SLOT)]), "gather_small")[0].reshape(-1)
    grads = {n: jnp.stack([groups[l][n].reshape(weights[n].shape[1:]) for l in range(DEPTH)]) for n in BIG}
    grads.update(zip(SMALL, _unpack(pack_full, [weights[n].shape for n in SMALL])))

    delta, new_m, new_v = {}, {}, {}
    flat2 = lambda a: a.reshape(-1, a.shape[-1])
    for n in BIG:
        outs = adamw(flat2(weights[n]), flat2(grads[n]), flat2(moms[n]), flat2(vars_[n]), "adamw_" + n)
        delta[n], new_m[n], new_v[n] = [o.reshape(weights[n].shape) for o in outs]
    shapes = [weights[n].shape for n in SMALL]
    pk = lambda t: _pack([t[n] for n in SMALL]).reshape(-1, SLOT)
    outs = adamw(pk(weights), pack_full.reshape(-1, SLOT), pk(moms), pk(vars_), "adamw_small")
    for tgt, o in zip((delta, new_m, new_v), outs):
        tgt.update(zip(SMALL, _unpack(o.reshape(-1), shapes)))

    return (loss, dx.reshape(x.shape), *[grads[n] for n in ORDER], *[delta[n] for n in ORDER],
            *[new_m[n] for n in ORDER], *[new_v[n] for n in ORDER])
```

```python
import functools

import jax
import jax.numpy as jnp
from jax import lax
from jax.experimental import pallas as pl
from jax.experimental.pallas import tpu as pltpu

F32 = jnp.float32
BF16 = jnp.bfloat16
MXU_DTYPE = BF16

D_MODEL = 1024
DEPTH = 4
CHUNK = 128
EPS = 1e-6
HEAD64 = 64
MLA_HEADS = 8
QK_NOPE = 64
QK_ROPE = 32
QK_DIM = 96
V_DIM = 64
Q_LORA = 256
KV_LORA = 128
SLOT = 128
ROPE_THETA = 10000.0
D_FF_SHARD = 1024
N_CHIPS = 4

ADAM_LR = 0.001
ADAM_B1 = 0.9
ADAM_B2 = 0.999
ADAM_EPS = 1e-08
ADAM_WD = 0.01
ADAM_STEP = 10

TM = 256
TM_FFN = 512
TQ = 256
ATT_HEADS_PER_STEP = 2
ATT_WIDE = 512
VMEM_LIMIT = 56 * 1024 * 1024

NN = (((1,), (0,)), ((), ()))
NT = (((1,), (1,)), ((), ()))
TN = (((0,), (0,)), ((), ()))
BNN = (((2,), (1,)), ((0,), (0,)))
BNT = (((2,), (2,)), ((0,), (0,)))
BTN = (((1,), (1,)), ((0,), (0,)))


def _dot(a, b, dims):
    return lax.dot_general(a.astype(MXU_DTYPE), b.astype(MXU_DTYPE), dims, preferred_element_type=F32)


def _hdot(a, b, dims=NN):
    return lax.dot_general(a, b, dims, precision=lax.Precision.HIGHEST, preferred_element_type=F32)


def _make_ad(dims, da_dims, da_swap, db_dims, db_swap):
    @jax.custom_vjp
    def f(a, b):
        return _dot(a, b, dims)

    def fwd(a, b):
        return _dot(a, b, dims), (a, b)

    def bwd(res, g):
        a, b = res
        da = _dot(b, g, da_dims) if da_swap else _dot(g, b, da_dims)
        db = _dot(g, a, db_dims) if db_swap else _dot(a, g, db_dims)
        return da, db

    f.defvjp(fwd, bwd)
    return f


@functools.partial(jax.custom_vjp, nondiff_argnums=(1,))
def _roll_ad(x, shift):
    return pltpu.roll(x, shift, 1)


def _roll_ad_fwd(x, shift):
    return pltpu.roll(x, shift, 1), None


def _roll_ad_bwd(shift, _, g):
    return (pltpu.roll(g, (g.shape[1] - shift) % g.shape[1], 1),)


_roll_ad.defvjp(_roll_ad_fwd, _roll_ad_bwd)


class _Ops:
    pass


PLAIN = _Ops()
PLAIN.mm = lambda a, b: _dot(a, b, NN)
PLAIN.bmm = lambda a, b: _dot(a, b, BNN)
PLAIN.bmm_nt = lambda a, b: _dot(a, b, BNT)
PLAIN.bmm_tn = lambda a, b: _dot(a, b, BTN)
PLAIN.roll = lambda x, s: pltpu.roll(x, s, 1)

AD = _Ops()
AD.mm = _make_ad(NN, NT, False, TN, False)
AD.bmm = _make_ad(BNN, BNT, False, BTN, False)
AD.bmm_nt = _make_ad(BNT, BNN, False, BTN, True)
AD.bmm_tn = _make_ad(BTN, BNT, True, BNN, False)
AD.roll = _roll_ad


def _sigmoid(x):
    return jax.nn.sigmoid(x)


def _gelu(x):
    return 0.5 * x * (1.0 + jnp.tanh(0.7978845608028654 * (x + 0.044715 * (x * x * x))))


def _rms(x, g):
    return x * lax.rsqrt(jnp.mean(x * x, axis=-1, keepdims=True) + EPS) * g


def _head_masks256():
    lane = lax.broadcasted_iota(jnp.int32, (1, 4 * HEAD64), 1)
    return [(jnp.right_shift(lane, 6) == h).astype(F32) for h in range(4)]


def _headnorm256(x, g):
    ms = jnp.zeros_like(x)
    sq = x * x
    for m in _head_masks256():
        ms = ms + m * (jnp.sum(sq * m, axis=-1, keepdims=True) * (1.0 / HEAD64))
    return x * lax.rsqrt(ms + EPS) * g


def _slot_norm(x, g, n):
    return x * lax.rsqrt(jnp.sum(x * x, axis=-1, keepdims=True) * (1.0 / n) + EPS) * g


def _rope(ops, x, cos_t, sin_a, sin_b):
    return x * cos_t + ops.roll(x, SLOT - QK_ROPE // 2) * sin_a + ops.roll(x, QK_ROPE // 2) * sin_b


def _inproj(ops, x, g1, wa, wb, wc):
    h = _rms(x, g1)
    return ops.mm(h, wa), ops.mm(h, wb), ops.mm(h, wc)


def _gm_chunk(ops, ur, vr, vg, ws4, bs, og):
    c = ur.shape[0]
    masks = _head_masks256()
    mh = jnp.concatenate([m[None] for m in masks], axis=0)
    u = _gelu(ur)
    v = _headnorm256(_gelu(vr), vg)
    t = lax.broadcasted_iota(jnp.int32, (c, c), 0)
    s = lax.broadcasted_iota(jnp.int32, (c, c), 1)
    w = jnp.where((t >= s)[None], ws4, 0.0)
    y = jnp.sum(ops.bmm(w, v[None] * mh), axis=0)
    for h in range(4):
        y = y + bs[h] * masks[h]
    return _headnorm256(u * y, og)


def _hg_chunk(ops, st, qr, fr, ir, gr, lb, og):
    c, n = qr.shape
    nh = n // HEAD64
    heads = lambda x: x.reshape(nh, HEAD64, x.shape[-1])
    tr = lambda x: heads(x.T)
    lb4, og4 = heads(lb), heads(og)
    qx = tr(qr)
    q = qx * _sigmoid(qx)
    f = lb4 + (1.0 - lb4) * _sigmoid(tr(fr))
    k = 1.0 - f
    logf = jnp.log(f)
    v = tr(ir)
    gx = tr(gr)
    s = lax.broadcasted_iota(jnp.int32, (c, c), 0)
    t = lax.broadcasted_iota(jnp.int32, (c, c), 1)
    tl = lax.broadcasted_iota(jnp.int32, (1, c), 1).reshape(1, 1, c)
    b2 = _hdot(logf.reshape(n, c), (s <= t).astype(F32))
    b = heads(b2)
    btot = jnp.sum(logf, axis=2, keepdims=True)
    inter = ops.bmm_tn(st, q * jnp.exp(b))
    p4 = jnp.zeros((nh, c, c), F32)
    tt, ss = s, t
    lg = 6
    while lg >= 0:
        m = 1 << lg
        bnd = jnp.left_shift(jnp.right_shift(t, lg + 1), lg + 1) + (m - 1)
        r = heads(_hdot(b2, (s == bnd).astype(F32)))
        right = jnp.bitwise_and(jnp.right_shift(tl, lg), 1) == 1
        qe = jnp.where(right, q * jnp.exp(jnp.where(right, b - r, 0.0)), 0.0)
        ke = jnp.where(right, 0.0, k * jnp.exp(jnp.where(right, 0.0, r - b)))
        lm = ((jnp.right_shift(tt, lg + 1) == jnp.right_shift(ss, lg + 1))
              & (jnp.bitwise_and(jnp.right_shift(tt, lg), 1) == 1)
              & (jnp.bitwise_and(jnp.right_shift(ss, lg), 1) == 0))
        p4 = jnp.where(lm[None], ops.bmm_tn(qe, ke), p4)
        lg -= 1
    intra = ops.bmm_nt(v, p4)
    o = inter + intra + jnp.sum(q * k, axis=1, keepdims=True) * v
    st_new = st * jnp.exp(btot) + ops.bmm_nt(k * jnp.exp(btot - b), v)
    y = o * lax.rsqrt(jnp.mean(o * o, axis=1, keepdims=True) + EPS) * og4 * (gx * _sigmoid(gx))
    return st_new, y.reshape(n, c).T


def _mla_pre(ops, cq, ckv, kpe, cos_t, sin_a, sin_b, qag, kvag, qg, kg, wq, wk, wv):
    cqn = _rms(cq, qag)
    ckvn = _rms(ckv, kvag)
    kper = ops.roll(kpe, QK_NOPE)
    qs, ks, vs = [], [], []
    for h in range(MLA_HEADS):
        qh = _slot_norm(ops.mm(cqn, wq[h]), qg, QK_DIM)
        qs.append(_rope(ops, qh, cos_t, sin_a, sin_b))
        kh = _slot_norm(ops.mm(ckvn, wk[h]) + kper, kg, QK_DIM)
        ks.append(_rope(ops, kh, cos_t, sin_a, sin_b))
        vs.append(ops.mm(ckvn, wv[h]))
    return qs, ks, vs


def _outproj(ops, x, ya, yb, o, mog, woa, wob, woc):
    acc = x + ops.mm(ya, woa) + ops.mm(yb, wob)
    for h in range(MLA_HEADS):
        acc = acc + ops.mm(_slot_norm(o[h], mog[h], V_DIM), woc[h])
    return acc


def _ffn_part(ops, x1, g2, w1p, w2p):
    a = ops.mm(_rms(x1, g2), w1p)
    r = jnp.maximum(a, 0.0)
    return ops.mm(r * r, w2p)


def _lower_bounds(r0, r1, r2, r3):
    mx = jnp.maximum(jnp.maximum(r0, r1), jnp.maximum(r2, r3))
    e0, e1, e2, e3 = jnp.exp(r0 - mx), jnp.exp(r1 - mx), jnp.exp(r2 - mx), jnp.exp(r3 - mx)
    inv = 1.0 / (e0 + e1 + e2 + e3)
    s1, s2, s3 = e1 * inv, e2 * inv, e3 * inv
    return jnp.zeros_like(r0), s1, s1 + s2, s1 + s2 + s3


def _cp(sem):
    return pltpu.CompilerParams(dimension_semantics=sem, vmem_limit_bytes=VMEM_LIMIT)


def _rows(tm, n):
    return pl.BlockSpec((tm, n), lambda i: (i, 0))


def _full(a):
    nd = len(a.shape)
    return pl.BlockSpec(a.shape, lambda *_: (0,) * nd)


def _sds(shape, dtype=F32):
    return jax.ShapeDtypeStruct(shape, dtype)


def _acc(ref, val, first):
    @pl.when(first)
    def _():
        ref[...] = val

    @pl.when(jnp.logical_not(first))
    def _():
        ref[...] = ref[...] + val


def _f32(ref):
    return ref[...].astype(F32)


MESH = pl.DeviceIdType.MESH
ANY = pl.BlockSpec(memory_space=pl.ANY)


class Rider:
    def __init__(self, arrays, out_shapes, aliases, sems, start, finish):
        self.arrays, self.out_shapes, self.aliases, self.sems = list(arrays), list(out_shapes), dict(aliases), list(sems)
        self.start, self.finish = start, finish


def run_rider(rider, name):
    n_in, n_out = len(rider.arrays), len(rider.out_shapes)

    def body(*refs):
        ins, outs, sems = refs[:n_in], refs[n_in:n_in + n_out], refs[n_in + n_out:]
        rider.start(ins, outs, sems)
        rider.finish(ins, outs, sems)

    return pl.pallas_call(
        body, name=name, in_specs=[ANY] * n_in, out_specs=[ANY] * n_out, out_shape=rider.out_shapes,
        input_output_aliases=rider.aliases, scratch_shapes=rider.sems,
    )(*rider.arrays)


def _merge_riders(riders):
    bounds, a0, o0, s0 = [], 0, 0, 0
    for r in riders:
        bounds.append((a0, o0, s0))
        a0, o0, s0 = a0 + len(r.arrays), o0 + len(r.out_shapes), s0 + len(r.sems)

    def part(k, ins, outs, sems):
        a, o, s = bounds[k]
        r = riders[k]
        return ins[a:a + len(r.arrays)], outs[o:o + len(r.out_shapes)], sems[s:s + len(r.sems)]

    return Rider(
        [x for r in riders for x in r.arrays], [x for r in riders for x in r.out_shapes],
        {bounds[k][0] + i: bounds[k][1] + o for k, r in enumerate(riders) for i, o in r.aliases.items()},
        [x for r in riders for x in r.sems],
        lambda *refs: [r.start(*part(k, *refs)) for k, r in enumerate(riders)],
        lambda *refs: [r.finish(*part(k, *refs)) for k, r in enumerate(riders)])


def _ride(compute, riders, *, name, grid, in_specs, out_specs, out_shape, operands, scratch_shapes=(), sem=None):
    single = not isinstance(out_shape, (list, tuple))
    if single:
        out_specs, out_shape = [out_specs], [out_shape]
    live = [r for r in riders if r is not None]
    if not live:
        res = pl.pallas_call(compute, name=name, grid=grid, in_specs=in_specs, out_specs=out_specs, out_shape=out_shape,
                             scratch_shapes=list(scratch_shapes), compiler_params=_cp(sem))(*operands)
        return (res[0] if single else res), [None] * len(riders)
    rider = live[0] if len(live) == 1 else _merge_riders(live)
    n_in, n_out, n_s = len(in_specs), len(out_specs), len(scratch_shapes)
    r_in, r_out = len(rider.arrays), len(rider.out_shapes)

    def body(*refs):
        ins, rins = refs[:n_in], refs[n_in:n_in + r_in]
        outs = refs[n_in + r_in:n_in + r_in + n_out]
        routs = refs[n_in + r_in + n_out:n_in + r_in + n_out + r_out]
        scr = refs[n_in + r_in + n_out + r_out:n_in + r_in + n_out + r_out + n_s]
        rsems = refs[n_in + r_in + n_out + r_out + n_s:]
        first = functools.reduce(jnp.logical_and, [pl.program_id(a) == 0 for a in range(len(grid))])
        last = functools.reduce(jnp.logical_and, [pl.program_id(a) == grid[a] - 1 for a in range(len(grid))])

        @pl.when(first)
        def _():
            rider.start(rins, routs, rsems)

        compute(*ins, *outs, *scr)

        @pl.when(last)
        def _():
            rider.finish(rins, routs, rsems)

    res = pl.pallas_call(
        body, name=name, grid=grid, in_specs=list(in_specs) + [ANY] * r_in, out_specs=list(out_specs) + [ANY] * r_out,
        out_shape=list(out_shape) + rider.out_shapes,
        input_output_aliases={n_in + k: n_out + v for k, v in rider.aliases.items()},
        scratch_shapes=list(scratch_shapes) + rider.sems, compiler_params=_cp(("arbitrary",) * len(grid)),
    )(*operands, *rider.arrays)
    main, rest, per_rider = res[:n_out], list(res[n_out:]), []
    for r in riders:
        per_rider.append(None if r is None else [rest.pop(0) for _ in r.out_shapes])
    return (main[0] if single else main), per_rider


def inproj_fwd(x, g1, wa, wb, wc):
    s, d = x.shape

    def body(x_ref, g_ref, wa_ref, wb_ref, wc_ref, pa_ref, pb_ref, pc_ref):
        pa, pb, pc = _inproj(PLAIN, x_ref[...], g_ref[...], wa_ref[...], wb_ref[...], wc_ref[...])
        pa_ref[...] = pa
        pb_ref[...] = pb
        pc_ref[...] = pc

    return pl.pallas_call(
        body, name="inproj_fwd", grid=(s // TM,),
        in_specs=[_rows(TM, d), _full(g1), _full(wa), _full(wb), _full(wc)],
        out_specs=[_rows(TM, wa.shape[1]), _rows(TM, wb.shape[1]), _rows(TM, wc.shape[1])],
        out_shape=[_sds((s, wa.shape[1])), _sds((s, wb.shape[1])), _sds((s, wc.shape[1]))],
        compiler_params=_cp(("parallel",)),
    )(x, g1, wa, wb, wc)


def inproj_bwd(x, g1, wa, wb, wc, dpa, dpb, dpc, dres):
    s, d = x.shape

    def body(x_ref, g_ref, wa_ref, wb_ref, wc_ref, dpa_ref, dpb_ref, dpc_ref, dres_ref,
             dx_ref, dg_ref, dwa_ref, dwb_ref, dwc_ref):
        first = pl.program_id(0) == 0
        _, vjp = jax.vjp(functools.partial(_inproj, AD), x_ref[...], g_ref[...],
                         _f32(wa_ref), _f32(wb_ref), _f32(wc_ref))
        dx, dg, dwa, dwb, dwc = vjp((dpa_ref[...], dpb_ref[...], dpc_ref[...]))
        dx_ref[...] = dx + dres_ref[...]
        _acc(dg_ref, dg, first)
        _acc(dwa_ref, dwa, first)
        _acc(dwb_ref, dwb, first)
        _acc(dwc_ref, dwc, first)

    return pl.pallas_call(
        body, name="inproj_bwd", grid=(s // TM,),
        in_specs=[_rows(TM, d), _full(g1), _full(wa), _full(wb), _full(wc),
                  _rows(TM, wa.shape[1]), _rows(TM, wb.shape[1]), _rows(TM, wc.shape[1]), _rows(TM, d)],
        out_specs=[_rows(TM, d), _full(g1), _full(wa), _full(wb), _full(wc)],
        out_shape=[_sds((s, d)), _sds(g1.shape), _sds(wa.shape), _sds(wb.shape), _sds(wc.shape)],
        compiler_params=_cp(("arbitrary",)),
    )(x, g1, wa, wb, wc, dpa, dpb, dpc, dres)


def gm_fwd(pa, vg, ws4, bs, og):
    s = pa.shape[0]
    w = pa.shape[1] // 2

    def body(pa_ref, vg_ref, ws_ref, bs_ref, og_ref, ya_ref):
        bsl = [bs_ref[h] for h in range(4)]
        ya_ref[...] = _gm_chunk(PLAIN, pa_ref[:, 0:w], pa_ref[:, w:2 * w], vg_ref[...], ws_ref[...], bsl, og_ref[...])

    return pl.pallas_call(
        body, name="gm_fwd", grid=(s // CHUNK,),
        in_specs=[_rows(CHUNK, 2 * w), _full(vg), _full(ws4), _full(bs), _full(og)],
        out_specs=_rows(CHUNK, w), out_shape=_sds((s, w)),
        compiler_params=_cp(("parallel",)),
    )(pa, vg, ws4, bs, og)


def gm_bwd(pa, vg, ws4, bs, og, dya):
    s = pa.shape[0]
    w = pa.shape[1] // 2

    def body(pa_ref, vg_ref, ws_ref, bs_ref, og_ref, dya_ref, dpa_ref, dvg_ref, dws_ref, dbs_ref, dog_ref):
        first = pl.program_id(0) == 0
        bsl = [bs_ref[h] for h in range(4)]
        _, vjp = jax.vjp(functools.partial(_gm_chunk, AD), pa_ref[:, 0:w], pa_ref[:, w:2 * w],
                         vg_ref[...], ws_ref[...], bsl, og_ref[...])
        du, dv, dvg, dws, dbs, dog = vjp(dya_ref[...])
        dpa_ref[:, 0:w] = du
        dpa_ref[:, w:2 * w] = dv
        _acc(dvg_ref, dvg, first)
        _acc(dws_ref, dws, first)
        _acc(dog_ref, dog, first)
        for h in range(4):
            _acc(dbs_ref.at[h], dbs[h], first)

    return pl.pallas_call(
        body, name="gm_bwd", grid=(s // CHUNK,),
        in_specs=[_rows(CHUNK, 2 * w), _full(vg), _full(ws4), _full(bs), _full(og), _rows(CHUNK, w)],
        out_specs=[_rows(CHUNK, 2 * w), _full(vg), _full(ws4), _full(bs), _full(og)],
        out_shape=[_sds((s, 2 * w)), _sds(vg.shape), _sds(ws4.shape), _sds(bs.shape), _sds(og.shape)],
        compiler_params=_cp(("arbitrary",)),
    )(pa, vg, ws4, bs, og, dya)


def hg_fwd(pb, lb, og):
    s = pb.shape[0]
    w = pb.shape[1] // 4
    nc = s // CHUNK
    st_shape = (w // HEAD64, HEAD64, HEAD64)

    def body(pb_ref, lb_ref, og_ref, yb_ref, states_ref, st_ref):
        @pl.when(pl.program_id(0) == 0)
        def _():
            st_ref[...] = jnp.zeros_like(st_ref)

        st = st_ref[...]
        states_ref[...] = st
        st_new, y = _hg_chunk(PLAIN, st, pb_ref[:, 0:w], pb_ref[:, w:2 * w], pb_ref[:, 2 * w:3 * w],
                              pb_ref[:, 3 * w:4 * w], lb_ref[...], og_ref[...])
        st_ref[...] = st_new
        yb_ref[...] = y

    return pl.pallas_call(
        body, name="hg_fwd", grid=(nc,),
        in_specs=[_rows(CHUNK, 4 * w), _full(lb), _full(og)],
        out_specs=[_rows(CHUNK, w), pl.BlockSpec((None,) + st_shape, lambda i: (i, 0, 0, 0))],
        out_shape=[_sds((s, w)), _sds((nc,) + st_shape)],
        scratch_shapes=[pltpu.VMEM(st_shape, F32)],
        compiler_params=_cp(("arbitrary",)),
    )(pb, lb, og)


def hg_bwd(pb, lb, og, states, dyb, riders=()):
    s = pb.shape[0]
    w = pb.shape[1] // 4
    nc = s // CHUNK
    st_shape = (w // HEAD64, HEAD64, HEAD64)

    def body(pb_ref, lb_ref, og_ref, states_ref, dyb_ref, dpb_ref, dlb_ref, dog_ref, dst_ref):
        first = pl.program_id(0) == 0

        @pl.when(first)
        def _():
            dst_ref[...] = jnp.zeros_like(dst_ref)

        _, vjp = jax.vjp(functools.partial(_hg_chunk, AD), states_ref[...], pb_ref[:, 0:w], pb_ref[:, w:2 * w],
                         pb_ref[:, 2 * w:3 * w], pb_ref[:, 3 * w:4 * w], lb_ref[...], og_ref[...])
        dst, dq, df, di, dg, dlb, dog = vjp((dst_ref[...], dyb_ref[...]))
        dst_ref[...] = dst
        dpb_ref[:, 0:w] = dq
        dpb_ref[:, w:2 * w] = df
        dpb_ref[:, 2 * w:3 * w] = di
        dpb_ref[:, 3 * w:4 * w] = dg
        _acc(dlb_ref, dlb, first)
        _acc(dog_ref, dog, first)

    rev = lambda i: (nc - 1 - i, 0)
    return _ride(
        body, riders, name="hg_bwd", grid=(nc,),
        in_specs=[pl.BlockSpec((CHUNK, 4 * w), rev), _full(lb), _full(og),
                  pl.BlockSpec((None,) + st_shape, lambda i: (nc - 1 - i, 0, 0, 0)), pl.BlockSpec((CHUNK, w), rev)],
        out_specs=[pl.BlockSpec((CHUNK, 4 * w), rev), _full(lb), _full(og)],
        out_shape=[_sds((s, 4 * w)), _sds(lb.shape), _sds(og.shape)],
        scratch_shapes=[pltpu.VMEM(st_shape, F32)],
        operands=(pb, lb, og, states, dyb), sem=("arbitrary",))


def lower_bounds_fwd(hlb):
    def body(h_ref, o_ref):
        outs = _lower_bounds(*[h_ref[pl.ds(i, 1), :] for i in range(DEPTH)])
        for i in range(DEPTH):
            o_ref[pl.ds(i, 1), :] = outs[i]

    return pl.pallas_call(body, name="lower_bounds_fwd", out_shape=_sds(hlb.shape))(hlb)


def lower_bounds_bwd(hlb, dlbs):
    def body(h_ref, d_ref, o_ref):
        _, vjp = jax.vjp(_lower_bounds, *[h_ref[pl.ds(i, 1), :] for i in range(DEPTH)])
        outs = vjp(tuple(d_ref[pl.ds(i, 1), :] for i in range(DEPTH)))
        for i in range(DEPTH):
            o_ref[pl.ds(i, 1), :] = outs[i]

    return pl.pallas_call(body, name="lower_bounds_bwd", out_shape=_sds(hlb.shape))(hlb, dlbs)


def _mla_pre_args(pc_ref, cos_ref, sa_ref, sb_ref, qag_ref, kvag_ref, qg_ref, kg_ref, wq_ref, wk_ref, wv_ref, cast):
    sl = lambda h: slice(h * SLOT, (h + 1) * SLOT)
    ld = (lambda r, h: r[:, sl(h)].astype(F32)) if cast else (lambda r, h: r[:, sl(h)])
    diff = (pc_ref[:, 0:Q_LORA], pc_ref[:, Q_LORA:Q_LORA + KV_LORA], pc_ref[:, Q_LORA + KV_LORA:Q_LORA + 2 * KV_LORA],
            qag_ref[...], kvag_ref[...], qg_ref[...], kg_ref[...],
            [ld(wq_ref, h) for h in range(MLA_HEADS)], [ld(wk_ref, h) for h in range(MLA_HEADS)],
            [ld(wv_ref, h) for h in range(MLA_HEADS)])
    tables = (cos_ref[...], sa_ref[...], sb_ref[...])
    return diff, tables


def _mla_pre_fn(ops, tables, cq, ckv, kpe, qag, kvag, qg, kg, wq, wk, wv):
    return _mla_pre(ops, cq, ckv, kpe, *tables, qag, kvag, qg, kg, wq, wk, wv)


def mla_pre_fwd(pc, cos_t, sin_a, sin_b, qag, kvag, qg, kg, wq, wk, wv):
    s = pc.shape[0]
    hw = MLA_HEADS * SLOT

    def body(pc_ref, cos_ref, sa_ref, sb_ref, qag_ref, kvag_ref, qg_ref, kg_ref, wq_ref, wk_ref, wv_ref,
             q_ref, k_ref, v_ref):
        diff, tables = _mla_pre_args(pc_ref, cos_ref, sa_ref, sb_ref, qag_ref, kvag_ref, qg_ref, kg_ref,
                                     wq_ref, wk_ref, wv_ref, False)
        qs, ks, vs = _mla_pre_fn(PLAIN, tables, *diff)
        ones_lane = (lax.broadcasted_iota(jnp.int32, (1, SLOT), 1) == V_DIM).astype(F32)
        for h in range(MLA_HEADS):
            q_ref[:, h * SLOT:(h + 1) * SLOT] = qs[h].astype(q_ref.dtype)
            k_ref[:, h * SLOT:(h + 1) * SLOT] = ks[h].astype(k_ref.dtype)
            v_ref[:, h * SLOT:(h + 1) * SLOT] = (vs[h] + ones_lane).astype(v_ref.dtype)

    return pl.pallas_call(
        body, name="mla_pre_fwd", grid=(s // TM,),
        in_specs=[_rows(TM, pc.shape[1]), _rows(TM, SLOT), _rows(TM, SLOT), _rows(TM, SLOT),
                  _full(qag), _full(kvag), _full(qg), _full(kg), _full(wq), _full(wk), _full(wv)],
        out_specs=[_rows(TM, hw)] * 3, out_shape=[_sds((s, hw), MXU_DTYPE)] * 3,
        compiler_params=_cp(("parallel",)),
    )(pc, cos_t, sin_a, sin_b, qag, kvag, qg, kg, wq, wk, wv)


def mla_pre_bwd(pc, cos_t, sin_a, sin_b, qag, kvag, qg, kg, wq, wk, wv, dq, dk, dv):
    s = pc.shape[0]
    hw = MLA_HEADS * SLOT

    def body(pc_ref, cos_ref, sa_ref, sb_ref, qag_ref, kvag_ref, qg_ref, kg_ref, wq_ref, wk_ref, wv_ref,
             dq_ref, dk_ref, dv_ref, dpc_ref, dqag_ref, dkvag_ref, dqg_ref, dkg_ref, dwq_ref, dwk_ref, dwv_ref):
        first = pl.program_id(0) == 0
        diff, tables = _mla_pre_args(pc_ref, cos_ref, sa_ref, sb_ref, qag_ref, kvag_ref, qg_ref, kg_ref,
                                     wq_ref, wk_ref, wv_ref, True)
        _, vjp = jax.vjp(functools.partial(_mla_pre_fn, AD, tables), *diff)
        sl = lambda h: slice(h * SLOT, (h + 1) * SLOT)
        cot = ([dq_ref[:, sl(h)] for h in range(MLA_HEADS)], [dk_ref[:, sl(h)] for h in range(MLA_HEADS)],
               [dv_ref[:, sl(h)] for h in range(MLA_HEADS)])
        dcq, dckv, dkpe, dqag, dkvag, dqg, dkg, dwq, dwk, dwv = vjp(cot)
        dpc_ref[:, 0:Q_LORA] = dcq
        dpc_ref[:, Q_LORA:Q_LORA + KV_LORA] = dckv
        dpc_ref[:, Q_LORA + KV_LORA:Q_LORA + 2 * KV_LORA] = dkpe
        _acc(dqag_ref, dqag, first)
        _acc(dkvag_ref, dkvag, first)
        _acc(dqg_ref, dqg, first)
        _acc(dkg_ref, dkg, first)
        for h in range(MLA_HEADS):
            _acc(dwq_ref.at[:, sl(h)], dwq[h], first)
            _acc(dwk_ref.at[:, sl(h)], dwk[h], first)
            _acc(dwv_ref.at[:, sl(h)], dwv[h], first)

    return pl.pallas_call(
        body, name="mla_pre_bwd", grid=(s // TM,),
        in_specs=[_rows(TM, pc.shape[1]), _rows(TM, SLOT), _rows(TM, SLOT), _rows(TM, SLOT),
                  _full(qag), _full(kvag), _full(qg), _full(kg), _full(wq), _full(wk), _full(wv),
                  _rows(TM, hw), _rows(TM, hw), _rows(TM, hw)],
        out_specs=[_rows(TM, pc.shape[1]), _full(qag), _full(kvag), _full(qg), _full(kg),
                   _full(wq), _full(wk), _full(wv)],
        out_shape=[_sds(pc.shape), _sds(qag.shape), _sds(kvag.shape), _sds(qg.shape), _sds(kg.shape),
                   _sds(wq.shape), _sds(wk.shape), _sds(wv.shape)],
        compiler_params=_cp(("arbitrary",)),
    )(pc, cos_t, sin_a, sin_b, qag, kvag, qg, kg, wq, wk, wv, dq, dk, dv)


ATT_SCALE = QK_DIM ** -0.5
NEG_BIG = -1e30


def attn_fwd(q, k, v, riders=()):
    s = q.shape[0]
    nq = s // TQ
    hp = ATT_HEADS_PER_STEP
    sl = lambda j: slice(j * SLOT, (j + 1) * SLOT)

    wide = ATT_WIDE // TQ

    def body(q_ref, k_ref, v_ref, o_ref, lse_ref):
        qi = pl.program_id(1)
        row = lax.broadcasted_iota(jnp.int32, (TQ, TQ), 0)
        col = lax.broadcasted_iota(jnp.int32, (TQ, TQ), 1)
        lane = lax.broadcasted_iota(jnp.int32, (1, SLOT), 1)
        qs = [q_ref[:, sl(j)] for j in range(hp)]

        def step(ki, carry, n_tiles, masked):
            rk = pl.ds(pl.multiple_of(ki * TQ, TQ), n_tiles * TQ)
            out = []
            for j in range(hp):
                m, acc = carry[j]
                sc = _dot(qs[j], k_ref[rk, sl(j)], NT) * ATT_SCALE
                if masked:
                    sc = jnp.where(row >= col, sc, NEG_BIG)
                m_new = jnp.maximum(m, jnp.max(sc, axis=-1, keepdims=True))
                acc = jnp.exp(m - m_new) * acc + _dot(jnp.exp(sc - m_new), v_ref[rk, sl(j)], NN)
                out.append((m_new, acc))
            return tuple(out)

        n_wide = qi // wide
        init = tuple((jnp.full((TQ, 1), NEG_BIG, F32), jnp.zeros((TQ, SLOT), F32)) for _ in range(hp))
        carry = lax.fori_loop(0, n_wide, lambda kw, cr: step(kw * wide, cr, wide, False), init)
        carry = lax.fori_loop(n_wide * wide, qi, lambda ki, cr: step(ki, cr, 1, False), carry)
        carry = step(qi, carry, 1, True)
        for j in range(hp):
            m, acc = carry[j]
            l = jnp.sum(jnp.where(lane == V_DIM, acc, 0.0), axis=-1, keepdims=True)
            o_ref[:, sl(j)] = jnp.where(lane < V_DIM, acc / l, 0.0)
            lse_ref[j] = m + jnp.log(l)

    head_col = pl.BlockSpec((s, hp * SLOT), lambda g, i: (0, g))
    tile = pl.BlockSpec((TQ, hp * SLOT), lambda g, i: (i, g))
    return _ride(
        body, riders, name="attn_fwd", grid=(MLA_HEADS // hp, nq),
        in_specs=[tile, head_col, head_col],
        out_specs=[tile, pl.BlockSpec((hp, TQ, 1), lambda g, i: (g, i, 0))],
        out_shape=[_sds((s, MLA_HEADS * SLOT)), _sds((MLA_HEADS, s, 1))],
        operands=(q, k, v), sem=("parallel", "parallel"))


def attn_bwd(q, k, v, o, do, lse, riders=()):
    s = q.shape[0]
    nq = s // TQ
    hp = ATT_HEADS_PER_STEP
    sl = lambda j: slice(j * SLOT, (j + 1) * SLOT)
    wide = ATT_WIDE // TQ

    def body(q_ref, k_ref, v_ref, o_ref, do_ref, lse_ref, dq_ref, dk_ref, dv_ref, delta_ref):
        ki = pl.program_id(1)
        row = lax.broadcasted_iota(jnp.int32, (TQ, TQ), 0)
        col = lax.broadcasted_iota(jnp.int32, (TQ, TQ), 1)

        @pl.when(ki == 0)
        def _():
            dq_ref[...] = jnp.zeros_like(dq_ref)

            def prep(i, c):
                rows = pl.ds(pl.multiple_of(i * TQ, TQ), TQ)
                for j in range(hp):
                    delta_ref[j, rows, :] = jnp.sum(do_ref[rows, sl(j)] * o_ref[rows, sl(j)], axis=-1, keepdims=True)
                return c

            lax.fori_loop(0, nq, prep, 0)

        kks = [k_ref[:, sl(j)] for j in range(hp)]
        vvs = [v_ref[:, sl(j)] for j in range(hp)]

        def step(qi, carry, n_tiles, masked):
            rq = pl.ds(pl.multiple_of(qi * TQ, TQ), n_tiles * TQ)
            out = []
            for j in range(hp):
                dk, dv = carry[j]
                qq = q_ref[rq, sl(j)]
                dd = do_ref[rq, sl(j)]
                sc = _dot(qq, kks[j], NT) * ATT_SCALE
                if masked:
                    sc = jnp.where(row >= col, sc, NEG_BIG)
                p = jnp.exp(sc - lse_ref[j, rq, :])
                dv = dv + _dot(p, dd, TN)
                ds = p * (_dot(dd, vvs[j], NT) - delta_ref[j, rq, :]) * ATT_SCALE
                dk = dk + _dot(ds, qq, TN)
                dq_ref[rq, sl(j)] = dq_ref[rq, sl(j)] + _dot(ds, kks[j], NN)
                out.append((dk, dv))
            return tuple(out)

        zero = jnp.zeros((TQ, SLOT), F32)
        first_wide = (ki + wide) // wide
        carry = step(ki, tuple((zero, zero) for _ in range(hp)), 1, True)
        carry = lax.fori_loop(ki + 1, first_wide * wide, lambda qi, cr: step(qi, cr, 1, False), carry)
        carry = lax.fori_loop(first_wide, nq // wide, lambda qw, cr: step(qw * wide, cr, wide, False), carry)
        for j in range(hp):
            dk_ref[:, sl(j)] = carry[j][0]
            dv_ref[:, sl(j)] = carry[j][1]

    head_col = pl.BlockSpec((s, hp * SLOT), lambda g, i: (0, g))
    tile = pl.BlockSpec((TQ, hp * SLOT), lambda g, i: (i, g))
    return _ride(
        body, riders, name="attn_bwd", grid=(MLA_HEADS // hp, nq),
        in_specs=[head_col, tile, tile, head_col, head_col, pl.BlockSpec((hp, s, 1), lambda g, i: (g, 0, 0))],
        out_specs=[head_col, tile, tile],
        out_shape=[_sds((s, MLA_HEADS * SLOT))] * 3,
        scratch_shapes=[pltpu.VMEM((hp, s, 1), F32)],
        operands=(q, k, v, o, do, lse), sem=("arbitrary", "arbitrary"))


def _outproj_args(ya_ref, yb_ref, o_ref, mog_ref, woa_ref, wob_ref, woc_ref, cast):
    sl = lambda h: slice(h * SLOT, (h + 1) * SLOT)
    ldw = (lambda r: r[...].astype(F32)) if cast else (lambda r: r[...])
    ldc = (lambda h: woc_ref[sl(h), :].astype(F32)) if cast else (lambda h: woc_ref[sl(h), :])
    return (ya_ref[...], yb_ref[...], [o_ref[:, sl(h)] for h in range(MLA_HEADS)],
            [mog_ref[:, sl(h)] for h in range(MLA_HEADS)], ldw(woa_ref), ldw(wob_ref),
            [ldc(h) for h in range(MLA_HEADS)])


def outproj_fwd(x, ya, yb, o, mog, woa, wob, woc):
    s, d = x.shape

    def body(x_ref, ya_ref, yb_ref, o_ref, mog_ref, woa_ref, wob_ref, woc_ref, x1_ref):
        x1_ref[...] = _outproj(PLAIN, x_ref[...], *_outproj_args(ya_ref, yb_ref, o_ref, mog_ref, woa_ref, wob_ref,
                                                                  woc_ref, False))

    return pl.pallas_call(
        body, name="outproj_fwd", grid=(s // TM,),
        in_specs=[_rows(TM, d), _rows(TM, ya.shape[1]), _rows(TM, yb.shape[1]), _rows(TM, o.shape[1]),
                  _full(mog), _full(woa), _full(wob), _full(woc)],
        out_specs=_rows(TM, d), out_shape=_sds((s, d)),
        compiler_params=_cp(("parallel",)),
    )(x, ya, yb, o, mog, woa, wob, woc)


def outproj_bwd(ya, yb, o, mog, woa, wob, woc, dx2, dx1p, riders=()):
    s, d = dx2.shape
    npart = dx1p.shape[0]

    def body(ya_ref, yb_ref, o_ref, mog_ref, woa_ref, wob_ref, woc_ref, dx2_ref, dx1p_ref,
             dx1_ref, dya_ref, dyb_ref, do_ref, dmog_ref, dwoa_ref, dwob_ref, dwoc_ref):
        first = pl.program_id(0) == 0
        sl = lambda h: slice(h * SLOT, (h + 1) * SLOT)
        dx1 = dx2_ref[...]
        for p in range(npart):
            dx1 = dx1 + dx1p_ref[p]
        dx1_ref[...] = dx1
        args = _outproj_args(ya_ref, yb_ref, o_ref, mog_ref, woa_ref, wob_ref, woc_ref, True)
        _, vjp = jax.vjp(lambda *a: _outproj(AD, jnp.zeros_like(dx1), *a), *args)
        dya, dyb, do, dmog, dwoa, dwob, dwoc = vjp(dx1)
        dya_ref[...] = dya
        dyb_ref[...] = dyb
        _acc(dwoa_ref, dwoa, first)
        _acc(dwob_ref, dwob, first)
        for h in range(MLA_HEADS):
            do_ref[:, sl(h)] = do[h]
            _acc(dmog_ref.at[:, sl(h)], dmog[h], first)
            _acc(dwoc_ref.at[sl(h), :], dwoc[h], first)

    return _ride(
        body, riders, name="outproj_bwd", grid=(s // TM,),
        in_specs=[_rows(TM, ya.shape[1]), _rows(TM, yb.shape[1]), _rows(TM, o.shape[1]),
                  _full(mog), _full(woa), _full(wob), _full(woc), _rows(TM, d),
                  pl.BlockSpec((npart, TM, d), lambda i: (0, i, 0))],
        out_specs=[_rows(TM, d), _rows(TM, ya.shape[1]), _rows(TM, yb.shape[1]), _rows(TM, o.shape[1]),
                   _full(mog), _full(woa), _full(wob), _full(woc)],
        out_shape=[_sds((s, d)), _sds(ya.shape), _sds(yb.shape), _sds(o.shape),
                   _sds(mog.shape), _sds(woa.shape), _sds(wob.shape), _sds(woc.shape)],
        operands=(ya, yb, o, mog, woa, wob, woc, dx2, dx1p), sem=("arbitrary",))


def ffn_fwd(x1, g2, w1, w2, riders=()):
    s, d = x1.shape
    npart, _, fs = w1.shape

    def body(x1_ref, g_ref, w1_ref, w2_ref, x2_ref):
        p = pl.program_id(1)
        x1v = x1_ref[...]
        part = _ffn_part(PLAIN, x1v, g_ref[...], w1_ref[...], w2_ref[...])

        @pl.when(p == 0)
        def _():
            x2_ref[...] = x1v + part

        @pl.when(p != 0)
        def _():
            x2_ref[...] = x2_ref[...] + part

    tm = TM_FFN
    return _ride(
        body, riders, name="ffn_fwd", grid=(s // tm, npart),
        in_specs=[pl.BlockSpec((tm, d), lambda i, p: (i, 0)), pl.BlockSpec(g2.shape, lambda i, p: (0, 0)),
                  pl.BlockSpec((None, d, fs), lambda i, p: (p, 0, 0)), pl.BlockSpec((None, fs, d), lambda i, p: (p, 0, 0))],
        out_specs=pl.BlockSpec((tm, d), lambda i, p: (i, 0)), out_shape=_sds((s, d)),
        operands=(x1, g2, w1, w2), sem=("parallel", "arbitrary"))


def ffn_bwd(x1, g2, w1, w2, dx2, riders=()):
    s, d = x1.shape
    npart, _, fs = w1.shape

    def body(x1_ref, g_ref, w1_ref, w2_ref, dx2_ref, dx1p_ref, dg_ref, dw1_ref, dw2_ref):
        p = pl.program_id(0)
        i = pl.program_id(1)
        _, vjp = jax.vjp(functools.partial(_ffn_part, AD), x1_ref[...], g_ref[...], _f32(w1_ref), _f32(w2_ref))
        dx1, dg, dw1, dw2 = vjp(dx2_ref[...])
        dx1p_ref[...] = dx1
        _acc(dg_ref, dg, (p == 0) & (i == 0))
        _acc(dw1_ref, dw1, i == 0)
        _acc(dw2_ref, dw2, i == 0)

    tm = TM_FFN
    return _ride(
        body, riders, name="ffn_bwd", grid=(npart, s // tm),
        in_specs=[pl.BlockSpec((tm, d), lambda p, i: (i, 0)), pl.BlockSpec(g2.shape, lambda p, i: (0, 0)),
                  pl.BlockSpec((None, d, fs), lambda p, i: (p, 0, 0)), pl.BlockSpec((None, fs, d), lambda p, i: (p, 0, 0)),
                  pl.BlockSpec((tm, d), lambda p, i: (i, 0))],
        out_specs=[pl.BlockSpec((None, tm, d), lambda p, i: (p, i, 0)), pl.BlockSpec(g2.shape, lambda p, i: (0, 0)),
                   pl.BlockSpec((None, d, fs), lambda p, i: (p, 0, 0)), pl.BlockSpec((None, fs, d), lambda p, i: (p, 0, 0))],
        out_shape=[_sds((npart, s, d)), _sds(g2.shape), _sds(w1.shape), _sds(w2.shape)],
        operands=(x1, g2, w1, w2, dx2), sem=("arbitrary", "arbitrary"))


def loss_head(y, target):
    s, d = y.shape

    def body(y_ref, t_ref, dy_ref, loss_ref):
        err = y_ref[...] - t_ref[...]
        dy_ref[...] = err * (1.0 / d)
        part = jnp.sum(jnp.sum(err * err, axis=-1, keepdims=True), axis=0, keepdims=True) * (0.5 / d)
        _acc(loss_ref, jnp.broadcast_to(part, loss_ref.shape), pl.program_id(0) == 0)

    return pl.pallas_call(
        body, name="loss_head", grid=(s // TM,),
        in_specs=[_rows(TM, d), _rows(TM, d)],
        out_specs=[_rows(TM, d), pl.BlockSpec((1, SLOT), lambda i: (0, 0))],
        out_shape=[_sds((s, d)), _sds((1, SLOT))],
        compiler_params=_cp(("arbitrary",)),
    )(y, target)


def _row_block(r):
    for b in (512, 256, 128, 64, 32, 16, 8):
        if r % b == 0:
            return b
    return r


def sum_cores(a, got, half, me):
    nch, _, r, c = a.shape
    br = _row_block(r)

    def body(sp_ref, a_ref, g_ref, wire_ref, own_ref):
        tot = a_ref[...] + g_ref[...]
        wire_ref[...] = tot.astype(wire_ref.dtype)

        @pl.when(pl.program_id(1) == sp_ref[1])
        def _():
            own_ref[...] = tot

    grid_spec = pltpu.PrefetchScalarGridSpec(
        num_scalar_prefetch=1, grid=(r // br, nch),
        in_specs=[pl.BlockSpec((None, None, br, c), lambda i, p, sp: (p, sp[0], i, 0)),
                  pl.BlockSpec((None, br, c), lambda i, p, sp: (p, i, 0))],
        out_specs=[pl.BlockSpec((None, br, c), lambda i, p, sp: (p, i, 0)), pl.BlockSpec((br, c), lambda i, p, sp: (i, 0))])
    return pl.pallas_call(body, name="sum_cores", grid_spec=grid_spec, out_shape=[_sds((nch, r, c), BF16), _sds((r, c))],
                          compiler_params=_cp(("parallel", "arbitrary")))(jnp.stack([half, me]).astype(jnp.int32), a, got)


def sum_chips(own, recv, half):
    r, c = own.shape
    br = _row_block(r)

    def body(sp_ref, own_ref, r0_ref, r1_ref, r2_ref, out_ref):
        del sp_ref
        out_ref[...] = ((own_ref[...] + r0_ref[...].astype(F32)) + r1_ref[...].astype(F32)) + r2_ref[...].astype(F32)

    grid_spec = pltpu.PrefetchScalarGridSpec(
        num_scalar_prefetch=1, grid=(r // br,),
        in_specs=[pl.BlockSpec((br, c), lambda i, sp: (i, 0))]
        + [pl.BlockSpec((None, br, c), functools.partial(lambda i, sp, j: (j, i, 0), j=j)) for j in range(3)],
        out_specs=pl.BlockSpec((None, br, c), lambda i, sp: (sp[0], i, 0)))
    return pl.pallas_call(body, name="sum_chips", grid_spec=grid_spec, out_shape=_sds((2, r, c)),
                          compiler_params=_cp(("parallel",)))(half.reshape(1).astype(jnp.int32), own, recv, recv, recv)


def adamw(w, g, m, v, name):
    r, c = w.shape
    br = _row_block(r)
    c1 = 1.0 / (1.0 - ADAM_B1 ** ADAM_STEP)
    c2 = 1.0 / (1.0 - ADAM_B2 ** ADAM_STEP)

    def body(w_ref, g_ref, m_ref, v_ref, d_ref, nm_ref, nv_ref):
        gg = g_ref[...]
        nm = ADAM_B1 * m_ref[...] + (1.0 - ADAM_B1) * gg
        nv = ADAM_B2 * v_ref[...] + (1.0 - ADAM_B2) * (gg * gg)
        d_ref[...] = -ADAM_LR * ((nm * c1) / (jnp.sqrt(nv * c2) + ADAM_EPS) + ADAM_WD * w_ref[...])
        nm_ref[...] = nm
        nv_ref[...] = nv

    return pl.pallas_call(
        body, name=name, grid=(r // br,), in_specs=[_rows(br, c)] * 4, out_specs=[_rows(br, c)] * 3,
        out_shape=[_sds((r, c))] * 3, compiler_params=_cp(("parallel",)),
    )(w, g, m, v)


def _place():
    x, y, c = lax.axis_index("x"), lax.axis_index("y"), lax.axis_index("c")
    chips = [(1 - x, y), (x, 1 - y), (1 - x, 1 - y)]
    return x, y, c, chips


def _remote(src, dst, send_sem, recv_sem, to):
    return pltpu.make_async_remote_copy(src_ref=src, dst_ref=dst, send_sem=send_sem, recv_sem=recv_sem,
                                        device_id=to, device_id_type=MESH)


def gather_rider(arrs):
    n = len(arrs)
    me_chip = 2 * lax.axis_index("x") + lax.axis_index("y")
    bufs = [lax.dynamic_update_index_in_dim(lax.empty((N_CHIPS,) + a.shape, a.dtype), a, me_chip, 0) for a in arrs]

    def plan(ins, outs, sems):
        send_sems, recv_sems = sems
        x, y, c, chips = _place()
        me = 2 * x + y
        half, other, sibling = pl.ds(2 * c, 2), pl.ds(2 - 2 * c, 2), (x, y, 1 - c)
        cp = lambda i, k, src, dst, to: _remote(src, dst, send_sems.at[i, k], recv_sems.at[i, k], to)
        pairs = [(i, j, cx, cy) for i in range(n) for j, (cx, cy) in enumerate(chips)]
        blk = lambda i, cx, cy, part: outs[i].at[2 * cx + cy, part]
        first = lambda: [cp(i, j, ins[i].at[half], outs[i].at[me, half], (cx, cy, c)) for i, j, cx, cy in pairs]
        landed = lambda: [cp(i, j, blk(i, cx, cy, half), blk(i, cx, cy, half), (cx, cy, c)) for i, j, cx, cy in pairs]
        passed = lambda: [cp(i, 3 + j, blk(i, cx, cy, half), blk(i, cx, cy, half), sibling) for i, j, cx, cy in pairs]
        from_sibling = lambda: [cp(i, 3 + j, blk(i, cx, cy, other), blk(i, cx, cy, other), sibling) for i, j, cx, cy in pairs]
        return first, landed, passed, from_sibling

    def start(ins, outs, sems):
        for cp in plan(ins, outs, sems)[0]():
            cp.start()

    def finish(ins, outs, sems):
        first, landed, passed, from_sibling = plan(ins, outs, sems)
        forwards = passed()
        for a, b in zip(landed(), forwards):
            a.wait_recv()
            b.start()
        for cp in from_sibling():
            cp.wait_recv()
        for cp in first() + forwards:
            cp.wait_send()

    return Rider(list(arrs) + bufs, [_sds((N_CHIPS,) + a.shape, a.dtype) for a in arrs], {n + i: i for i in range(n)},
                 [pltpu.SemaphoreType.DMA((n, 6)), pltpu.SemaphoreType.DMA((n, 6))], start, finish)


class Reducer:
    def __init__(self, arrs):
        self.a = list(arrs)
        self.n = len(self.a)
        self.c = lax.axis_index("c")
        self.me = 2 * lax.axis_index("x") + lax.axis_index("y")

    def swap_rider(self):
        n = self.n

        def plan(ins, outs, sems):
            x, y, c, _ = _place()
            return [_remote(ins[i].at[p, 1 - c], outs[i].at[p], sems[0].at[i, p], sems[1].at[i, p], (x, y, 1 - c))
                    for i in range(n) for p in range(N_CHIPS)]

        return Rider(self.a, [_sds((N_CHIPS,) + a.shape[2:]) for a in self.a], {},
                     [pltpu.SemaphoreType.DMA((n, N_CHIPS)), pltpu.SemaphoreType.DMA((n, N_CHIPS))],
                     lambda *r: [cp.start() for cp in plan(*r)], lambda *r: [cp.wait() for cp in plan(*r)])

    def after_swap(self, got):
        pairs = [sum_cores(a, g, self.c, self.me) for a, g in zip(self.a, got)]
        self.wire, self.own = [p[0] for p in pairs], [p[1] for p in pairs]

    def scatter_rider(self):
        n = self.n

        def plan(ins, outs, sems):
            x, y, c, chips = _place()
            return [_remote(ins[i].at[2 * cx + cy], outs[i].at[j], sems[0].at[i, j], sems[1].at[i, j], (cx, cy, c))
                    for i in range(n) for j, (cx, cy) in enumerate(chips)]

        return Rider(self.wire, [_sds((3,) + w.shape[1:], w.dtype) for w in self.wire], {},
                     [pltpu.SemaphoreType.DMA((n, 3)), pltpu.SemaphoreType.DMA((n, 3))],
                     lambda *r: [cp.start() for cp in plan(*r)], lambda *r: [cp.wait() for cp in plan(*r)])

    def after_scatter(self, recv):
        self.full = [sum_chips(o, r, self.c) for o, r in zip(self.own, recv)]

    def share_rider(self):
        n = self.n

        def plan(ins, outs, sems):
            x, y, c, _ = _place()
            return [_remote(ins[i].at[c], outs[i].at[c], sems[0].at[i], sems[1].at[i], (x, y, 1 - c)) for i in range(n)]

        return Rider(self.full, [_sds(f.shape) for f in self.full], {i: i for i in range(n)},
                     [pltpu.SemaphoreType.DMA((n,)), pltpu.SemaphoreType.DMA((n,))],
                     lambda *r: [cp.start() for cp in plan(*r)], lambda *r: [cp.wait() for cp in plan(*r)])

    def run(self):
        self.after_swap(run_rider(self.swap_rider(), "swap_halves"))
        self.after_scatter(run_rider(self.scatter_rider(), "scatter_chips"))
        return run_rider(self.share_rider(), "share_halves")


def _pad_slots(a, live):
    lead = a.shape[:-1]
    a = a.reshape(lead + (MLA_HEADS, live))
    a = jnp.pad(a, [(0, 0)] * len(lead) + [(0, 0), (0, SLOT - live)])
    return a.reshape(lead + (MLA_HEADS * SLOT,))


def _unpad_slots(a, live):
    lead = a.shape[:-1]
    return a.reshape(lead + (MLA_HEADS, SLOT))[..., :live].reshape(lead + (MLA_HEADS * live,))


def _rope_tables(positions, s):
    half = QK_ROPE // 2
    inv_freq = ROPE_THETA ** (-jnp.arange(half, dtype=F32) / half)
    ang = positions.reshape(s).astype(F32)[:, None] * inv_freq[None, :]
    cos, sin = jnp.cos(ang), jnp.sin(ang)
    one = jnp.ones((s, QK_NOPE), F32)
    z64, z16, z32 = jnp.zeros((s, QK_NOPE), F32), jnp.zeros((s, half), F32), jnp.zeros((s, SLOT - QK_DIM), F32)
    cos_t = jnp.concatenate([one, cos, cos, z32], axis=1)
    sin_a = jnp.concatenate([z64, -sin, z16, z32], axis=1)
    sin_b = jnp.concatenate([z64, z16, sin, z32], axis=1)
    return cos_t, sin_a, sin_b


def _layer_weights(full, small, l):
    w_in = jnp.concatenate([full["w_in"][p] for p in range(N_CHIPS)], axis=1)
    wc = jnp.pad(w_in[:, 1536:], ((0, 0), (0, 512 - (w_in.shape[1] - 1536))))
    w_uq = jnp.concatenate([full["mla_w_uq"][p] for p in range(N_CHIPS)], axis=1)
    w_ukv = jnp.concatenate([full["mla_w_ukv"][p] for p in range(N_CHIPS)], axis=1)
    ukv = w_ukv.reshape(KV_LORA, MLA_HEADS, QK_NOPE + V_DIM)
    w_out = jnp.concatenate([full["w_out"][p] for p in range(N_CHIPS)], axis=0)
    woc = w_out[512:].reshape(MLA_HEADS, V_DIM, D_MODEL)
    woc = jnp.pad(woc, ((0, 0), (0, SLOT - V_DIM), (0, 0))).reshape(MLA_HEADS * SLOT, D_MODEL)
    row = lambda a: a.reshape(1, -1)
    return dict(
        g1=row(small["norm1_gain"][l]), wa=w_in[:, :512], wb=w_in[:, 512:1536], wc=wc,
        vg=row(small["gm_v_gain"][l]), ws=small["gm_w_s"][l], bs=small["gm_b_s"][l].reshape(4, CHUNK, 1),
        gog=row(small["gm_out_gain"][l]), hog=small["hg_out_gain"][l].reshape(-1, 1),
        qag=row(small["mla_q_a_gain"][l]), kvag=row(small["mla_kv_a_gain"][l]),
        qg=row(jnp.pad(small["mla_q_gain"][l], (0, SLOT - QK_DIM))), kg=row(jnp.pad(small["mla_k_gain"][l], (0, SLOT - QK_DIM))),
        wq=_pad_slots(w_uq, QK_DIM), wk=_pad_slots(ukv[..., :QK_NOPE].reshape(KV_LORA, -1), QK_NOPE),
        wv=_pad_slots(ukv[..., QK_NOPE:].reshape(KV_LORA, -1), V_DIM),
        mog=row(_pad_slots(small["mla_out_gain"][l], V_DIM)),
        woa=w_out[:256], wob=w_out[256:512], woc=woc,
        g2=row(small["norm2_gain"][l]),
    )


def _shard_cols(a):
    r, c4 = a.shape
    return a.reshape(r, N_CHIPS, c4 // N_CHIPS).transpose(1, 0, 2)


def local_step(x, positions, target, small, comm):
    s = x.shape[0]
    cos_t, sin_a, sin_b = _rope_tables(positions, s)
    lbs = lower_bounds_fwd(small["hg_lower_bound"])
    lw, saved = [], []
    for l in range(DEPTH):
        w = _layer_weights(comm.part(l, "mix"), small, l)
        lw.append(w)
        lb = lbs[l].reshape(-1, 1)
        pa, pb, pc = inproj_fwd(x, w["g1"], w["wa"], w["wb"], w["wc"])
        ya = gm_fwd(pa, w["vg"], w["ws"], w["bs"], w["gog"])
        yb, states = hg_fwd(pb, lb, w["hog"])
        q, k, v = mla_pre_fwd(pc, cos_t, sin_a, sin_b, w["qag"], w["kvag"], w["qg"], w["kg"], w["wq"], w["wk"], w["wv"])
        rider = comm.gather_rider(l, "ffn")
        (o, lse), got = attn_fwd(q, k, v, [rider])
        comm.gathered(l, "ffn", got[0])
        x1 = outproj_fwd(x, ya, yb, o, w["mog"], w["woa"], w["wob"], w["woc"])
        ffn_w = comm.part(l, "ffn")
        w["w1"], w["w2"] = ffn_w["w_ff1"], ffn_w["w_ff2"]
        rider = comm.gather_rider(l + 1, "mix") if l + 1 < DEPTH else None
        x2, got = ffn_fwd(x1, w["g2"], w["w1"], w["w2"], [rider])
        comm.gathered(l + 1, "mix", got[0])
        saved.append(dict(x=x, pa=pa, pb=pb, pc=pc, ya=ya, yb=yb, states=states, q=q, k=k, v=v, o=o, lse=lse, x1=x1, lb=lb))
        x = x2
    dx, loss_part = loss_head(x, target)
    groups = [dict() for _ in range(DEPTH)]
    sm = {n: [None] * DEPTH for n in ("norm1_gain", "gm_v_gain", "gm_w_s", "gm_b_s", "gm_out_gain", "hg_out_gain",
                                       "mla_q_a_gain", "mla_kv_a_gain", "mla_q_gain", "mla_k_gain", "mla_out_gain",
                                       "norm2_gain")}
    dlbs = [None] * DEPTH
    halves = lambda g: g.reshape(N_CHIPS, 2, g.shape[1] // 2, g.shape[2])
    take = lambda red, f: None if red is None else f(red)
    red_mix = None
    for l in reversed(range(DEPTH)):
        w, a = lw[l], saved[l]
        (dx1p, dg2, dw1, dw2), got = ffn_bwd(a["x1"], w["g2"], w["w1"], w["w2"], dx, [take(red_mix, Reducer.swap_rider)])
        if red_mix:
            red_mix.after_swap(got[0])
        ffn_arrs = [halves(dw1), halves(dw2)]
        red_ffn = comm.reducer(ffn_arrs)
        (dx1, dya, dyb, do, dmog, dwoa, dwob, dwoc), got = outproj_bwd(
            a["ya"], a["yb"], a["o"], w["mog"], w["woa"], w["wob"], w["woc"], dx, dx1p, [take(red_ffn, Reducer.swap_rider)])
        if red_ffn:
            red_ffn.after_swap(got[0])
        (dq, dk, dv), got = attn_bwd(a["q"], a["k"], a["v"], a["o"], do, a["lse"],
                                     [take(red_mix, Reducer.scatter_rider), take(red_ffn, Reducer.scatter_rider)])
        for red, g in zip((red_mix, red_ffn), got):
            if red:
                red.after_scatter(g)
        dpc, dqag, dkvag, dqg, dkg, dwq, dwk, dwv = mla_pre_bwd(a["pc"], cos_t, sin_a, sin_b, w["qag"], w["kvag"], w["qg"],
                                                                  w["kg"], w["wq"], w["wk"], w["wv"], dq, dk, dv)
        (dpb, dlb, dhog), got = hg_bwd(a["pb"], a["lb"], w["hog"], a["states"], dyb,
                                       [take(red_mix, Reducer.share_rider), take(red_ffn, Reducer.share_rider)])
        if red_mix:
            groups[l + 1].update(zip(MIX, got[0]))
        groups[l].update(zip(FFN, got[1] if red_ffn else ffn_arrs))
        dpa, dvg, dws, dbs, dgog = gm_bwd(a["pa"], w["vg"], w["ws"], w["bs"], w["gog"], dya)
        dx, dg1, dwa, dwb, dwc = inproj_bwd(a["x"], w["g1"], w["wa"], w["wb"], w["wc"], dpa, dpb, dpc, dx1)
        dukv = jnp.concatenate([dwk.reshape(KV_LORA, MLA_HEADS, SLOT)[..., :QK_NOPE],
                                dwv.reshape(KV_LORA, MLA_HEADS, SLOT)[..., :V_DIM]], axis=-1)
        dwo = jnp.concatenate([dwoa, dwob, dwoc.reshape(MLA_HEADS, SLOT, D_MODEL)[:, :V_DIM].reshape(-1, D_MODEL)], axis=0)
        mix_arrs = [halves(_shard_cols(jnp.concatenate([dwa, dwb, dwc[:, :1952 - 1536]], axis=1))),
                    halves(_shard_cols(_unpad_slots(dwq, QK_DIM))), halves(_shard_cols(dukv.reshape(KV_LORA, -1))),
                    halves(dwo.reshape(N_CHIPS, -1, D_MODEL))]
        red_mix = comm.reducer(mix_arrs) if l > 0 else None
        if red_mix is None:
            groups[l].update(zip(MIX, mix_arrs))
        sm["norm1_gain"][l] = dg1[0]
        sm["gm_v_gain"][l] = dvg[0]
        sm["gm_w_s"][l] = dws
        sm["gm_b_s"][l] = dbs[..., 0]
        sm["gm_out_gain"][l] = dgog[0]
        sm["hg_out_gain"][l] = dhog[:, 0]
        sm["mla_q_a_gain"][l] = dqag[0]
        sm["mla_kv_a_gain"][l] = dkvag[0]
        sm["mla_q_gain"][l] = dqg[0, :QK_DIM]
        sm["mla_k_gain"][l] = dkg[0, :QK_DIM]
        sm["mla_out_gain"][l] = _unpad_slots(dmog[0], V_DIM)
        sm["norm2_gain"][l] = dg2[0]
        dlbs[l] = dlb[:, 0]
    sm = {n: jnp.stack(v) for n, v in sm.items()}
    sm["hg_lower_bound"] = lower_bounds_bwd(small["hg_lower_bound"], jnp.stack(dlbs))
    return loss_part, dx, groups, sm


MIX = ("w_in", "mla_w_uq", "mla_w_ukv", "w_out")
FFN = ("w_ff1", "w_ff2")
BIG = MIX + FFN
PARTS = {"mix": MIX, "ffn": FFN}
SMALL = ("norm1_gain", "gm_v_gain", "gm_w_s", "gm_b_s", "gm_out_gain", "hg_lower_bound", "hg_out_gain",
         "mla_q_a_gain", "mla_kv_a_gain", "mla_q_gain", "mla_k_gain", "mla_out_gain", "norm2_gain")
ORDER = ("norm1_gain", "w_in", "gm_v_gain", "gm_w_s", "gm_b_s", "gm_out_gain", "hg_lower_bound", "hg_out_gain",
         "mla_q_a_gain", "mla_w_uq", "mla_kv_a_gain", "mla_w_ukv", "mla_q_gain", "mla_k_gain", "mla_out_gain",
         "w_out", "norm2_gain", "w_ff1", "w_ff2")
PACK_ROWS = 288


def _pack(arrs):
    flat = jnp.concatenate([a.reshape(-1) for a in arrs])
    total = 2 * N_CHIPS * PACK_ROWS * SLOT
    return jnp.pad(flat, (0, total - flat.shape[0]))


def _unpack(flat, shapes):
    out, off = [], 0
    for sh in shapes:
        size = 1
        for d in sh:
            size *= d
        out.append(flat[off:off + size].reshape(sh))
        off += size
    return out


class ChipComm:
    def __init__(self, shards):
        self.shards = shards
        self.full = {}

    def gather_rider(self, l, part):
        return gather_rider([self.shards[n][l].astype(MXU_DTYPE).reshape(4, self.shards[n].shape[1] // 4, -1)
                             for n in PARTS[part]])

    def gathered(self, l, part, outs):
        if outs is not None:
            self.full[l, part] = {n: o.reshape((N_CHIPS,) + self.shards[n].shape[1:]) for n, o in zip(PARTS[part], outs)}

    def part(self, l, part):
        if (l, part) not in self.full:
            self.gathered(l, part, run_rider(self.gather_rider(l, part), "gather_weights"))
        return self.full[l, part]

    def reducer(self, arrs):
        return Reducer(arrs)


def kernel(x, positions, norm1_gain, w_in, gm_v_gain, gm_w_s, gm_b_s, gm_out_gain, hg_lower_bound, hg_out_gain, mla_q_a_gain, mla_w_uq, mla_kv_a_gain, mla_w_ukv, mla_q_gain, mla_k_gain, mla_out_gain, w_out, norm2_gain, w_ff1, w_ff2, loss_target, m_norm1_gain, m_w_in, m_gm_v_gain, m_gm_w_s, m_gm_b_s, m_gm_out_gain, m_hg_lower_bound, m_hg_out_gain, m_mla_q_a_gain, m_mla_w_uq, m_mla_kv_a_gain, m_mla_w_ukv, m_mla_q_gain, m_mla_k_gain, m_mla_out_gain, m_w_out, m_norm2_gain, m_w_ff1, m_w_ff2, v_norm1_gain, v_w_in, v_gm_v_gain, v_gm_w_s, v_gm_b_s, v_gm_out_gain, v_hg_lower_bound, v_hg_out_gain, v_mla_q_a_gain, v_mla_w_uq, v_mla_kv_a_gain, v_mla_w_ukv, v_mla_q_gain, v_mla_k_gain, v_mla_out_gain, v_w_out, v_norm2_gain, v_w_ff1, v_w_ff2):
    given = dict(locals())
    weights = {n: given[n] for n in ORDER}
    moms = {n: given["m_" + n] for n in ORDER}
    vars_ = {n: given["v_" + n] for n in ORDER}
    s, d = x.shape[1], x.shape[2]

    small = {n: weights[n] for n in SMALL}
    comm = ChipComm({n: weights[n] for n in BIG})
    loss_part, dx, groups, small_g = local_step(x.reshape(s, d), positions, loss_target.reshape(s, d), small, comm)
    loss = lax.psum(loss_part[0, 0], ("x", "y", "c"))

    pack_g = _pack([small_g[n] for n in SMALL]).reshape(N_CHIPS, 2, PACK_ROWS, SLOT)
    last = Reducer([groups[0][n] for n in MIX] + [pack_g]).run()
    groups[0].update(zip(MIX, last[:-1]))
    pack_full = run_rider(gather_rider([last[-1].reshape(4, PACK_ROWS // 2, SLOT)]), "gather_small")[0].reshape(-1)
    grads = {n: jnp.stack([groups[l][n].reshape(weights[n].shape[1:]) for l in range(DEPTH)]) for n in BIG}
    grads.update(zip(SMALL, _unpack(pack_full, [weights[n].shape for n in SMALL])))

    delta, new_m, new_v = {}, {}, {}
    flat2 = lambda a: a.reshape(-1, a.shape[-1])
    for n in BIG:
        outs = adamw(flat2(weights[n]), flat2(grads[n]), flat2(moms[n]), flat2(vars_[n]), "adamw_" + n)
        delta[n], new_m[n], new_v[n] = [o.reshape(weights[n].shape) for o in outs]
    shapes = [weights[n].shape for n in SMALL]
    pk = lambda t: _pack([t[n] for n in SMALL]).reshape(-1, SLOT)
    outs = adamw(pk(weights), pack_full.reshape(-1, SLOT), pk(moms), pk(vars_), "adamw_small")
    for tgt, o in zip((delta, new_m, new_v), outs):
        tgt.update(zip(SMALL, _unpack(o.reshape(-1), shapes)))

    return (loss, dx.reshape(x.shape), *[grads[n] for n in ORDER], *[delta[n] for n in ORDER],
            *[new_m[n] for n in ORDER], *[new_v[n] for n in ORDER])
```

```python
import functools

import jax
import jax.numpy as jnp
from jax import lax
from jax.experimental import pallas as pl
from jax.experimental.pallas import tpu as pltpu

F32 = jnp.float32
BF16 = jnp.bfloat16
MXU_DTYPE = BF16

D_MODEL = 1024
DEPTH = 4
CHUNK = 128
EPS = 1e-6
HEAD64 = 64
MLA_HEADS = 8
QK_NOPE = 64
QK_ROPE = 32
QK_DIM = 96
V_DIM = 64
Q_LORA = 256
KV_LORA = 128
SLOT = 128
ROPE_THETA = 10000.0
D_FF_SHARD = 1024
N_CHIPS = 4

ADAM_LR = 0.001
ADAM_B1 = 0.9
ADAM_B2 = 0.999
ADAM_EPS = 1e-08
ADAM_WD = 0.01
ADAM_STEP = 10

TM = 512
TM_FFN = 512
TQ = 256
ATT_HEADS_PER_STEP = 2
ATT_WIDE = 512
VMEM_LIMIT = 56 * 1024 * 1024

NN = (((1,), (0,)), ((), ()))
NT = (((1,), (1,)), ((), ()))
TN = (((0,), (0,)), ((), ()))
BNN = (((2,), (1,)), ((0,), (0,)))
BNT = (((2,), (2,)), ((0,), (0,)))
BTN = (((1,), (1,)), ((0,), (0,)))


def _dot(a, b, dims):
    return lax.dot_general(a.astype(MXU_DTYPE), b.astype(MXU_DTYPE), dims, preferred_element_type=F32)


def _hdot(a, b, dims=NN):
    return lax.dot_general(a, b, dims, precision=lax.Precision.HIGHEST, preferred_element_type=F32)


def _make_ad(dims, da_dims, da_swap, db_dims, db_swap):
    @jax.custom_vjp
    def f(a, b):
        return _dot(a, b, dims)

    def fwd(a, b):
        return _dot(a, b, dims), (a, b)

    def bwd(res, g):
        a, b = res
        da = _dot(b, g, da_dims) if da_swap else _dot(g, b, da_dims)
        db = _dot(g, a, db_dims) if db_swap else _dot(a, g, db_dims)
        return da, db

    f.defvjp(fwd, bwd)
    return f


@functools.partial(jax.custom_vjp, nondiff_argnums=(1,))
def _roll_ad(x, shift):
    return pltpu.roll(x, shift, 1)


def _roll_ad_fwd(x, shift):
    return pltpu.roll(x, shift, 1), None


def _roll_ad_bwd(shift, _, g):
    return (pltpu.roll(g, (g.shape[1] - shift) % g.shape[1], 1),)


_roll_ad.defvjp(_roll_ad_fwd, _roll_ad_bwd)


class _Ops:
    pass


PLAIN = _Ops()
PLAIN.mm = lambda a, b: _dot(a, b, NN)
PLAIN.bmm = lambda a, b: _dot(a, b, BNN)
PLAIN.bmm_nt = lambda a, b: _dot(a, b, BNT)
PLAIN.bmm_tn = lambda a, b: _dot(a, b, BTN)
PLAIN.roll = lambda x, s: pltpu.roll(x, s, 1)

AD = _Ops()
AD.mm = _make_ad(NN, NT, False, TN, False)
AD.bmm = _make_ad(BNN, BNT, False, BTN, False)
AD.bmm_nt = _make_ad(BNT, BNN, False, BTN, True)
AD.bmm_tn = _make_ad(BTN, BNT, True, BNN, False)
AD.roll = _roll_ad


def _sigmoid(x):
    return jax.nn.sigmoid(x)


def _gelu(x):
    return 0.5 * x * (1.0 + jnp.tanh(0.7978845608028654 * (x + 0.044715 * (x * x * x))))


def _rms(x, g):
    return x * lax.rsqrt(jnp.mean(x * x, axis=-1, keepdims=True) + EPS) * g


def _head_masks256():
    lane = lax.broadcasted_iota(jnp.int32, (1, 4 * HEAD64), 1)
    return [(jnp.right_shift(lane, 6) == h).astype(F32) for h in range(4)]


def _headnorm256(x, g):
    ms = jnp.zeros_like(x)
    sq = x * x
    for m in _head_masks256():
        ms = ms + m * (jnp.sum(sq * m, axis=-1, keepdims=True) * (1.0 / HEAD64))
    return x * lax.rsqrt(ms + EPS) * g


def _slot_norm(x, g, n):
    return x * lax.rsqrt(jnp.sum(x * x, axis=-1, keepdims=True) * (1.0 / n) + EPS) * g


def _rope(ops, x, cos_t, sin_a, sin_b):
    return x * cos_t + ops.roll(x, SLOT - QK_ROPE // 2) * sin_a + ops.roll(x, QK_ROPE // 2) * sin_b


def _inproj(ops, x, g1, wa, wb, wc):
    h = _rms(x, g1)
    return ops.mm(h, wa), ops.mm(h, wb), ops.mm(h, wc)


def _gm_chunk(ops, ur, vr, vg, ws4, bs, og):
    c = ur.shape[0]
    masks = _head_masks256()
    mh = jnp.concatenate([m[None] for m in masks], axis=0)
    u = _gelu(ur)
    v = _headnorm256(_gelu(vr), vg)
    t = lax.broadcasted_iota(jnp.int32, (c, c), 0)
    s = lax.broadcasted_iota(jnp.int32, (c, c), 1)
    w = jnp.where((t >= s)[None], ws4, 0.0)
    y = jnp.sum(ops.bmm(w, v[None] * mh), axis=0)
    for h in range(4):
        y = y + bs[h] * masks[h]
    return _headnorm256(u * y, og)


def _hg_chunk(ops, st, qr, fr, ir, gr, lb, og):
    c, n = qr.shape
    nh = n // HEAD64
    heads = lambda x: x.reshape(nh, HEAD64, x.shape[-1])
    tr = lambda x: heads(x.T)
    lb4, og4 = heads(lb), heads(og)
    qx = tr(qr)
    q = qx * _sigmoid(qx)
    f = lb4 + (1.0 - lb4) * _sigmoid(tr(fr))
    k = 1.0 - f
    logf = jnp.log(f)
    v = tr(ir)
    gx = tr(gr)
    s = lax.broadcasted_iota(jnp.int32, (c, c), 0)
    t = lax.broadcasted_iota(jnp.int32, (c, c), 1)
    tl = lax.broadcasted_iota(jnp.int32, (1, c), 1).reshape(1, 1, c)
    b2 = _hdot(logf.reshape(n, c), (s <= t).astype(F32))
    b = heads(b2)
    btot = jnp.sum(logf, axis=2, keepdims=True)
    inter = ops.bmm_tn(st, q * jnp.exp(b))
    p4 = jnp.zeros((nh, c, c), F32)
    tt, ss = s, t
    lg = 6
    while lg >= 0:
        m = 1 << lg
        bnd = jnp.left_shift(jnp.right_shift(t, lg + 1), lg + 1) + (m - 1)
        r = heads(_hdot(b2, (s == bnd).astype(F32)))
        right = jnp.bitwise_and(jnp.right_shift(tl, lg), 1) == 1
        qe = jnp.where(right, q * jnp.exp(jnp.where(right, b - r, 0.0)), 0.0)
        ke = jnp.where(right, 0.0, k * jnp.exp(jnp.where(right, 0.0, r - b)))
        lm = ((jnp.right_shift(tt, lg + 1) == jnp.right_shift(ss, lg + 1))
              & (jnp.bitwise_and(jnp.right_shift(tt, lg), 1) == 1)
              & (jnp.bitwise_and(jnp.right_shift(ss, lg), 1) == 0))
        p4 = jnp.where(lm[None], ops.bmm_tn(qe, ke), p4)
        lg -= 1
    intra = ops.bmm_nt(v, p4)
    o = inter + intra + jnp.sum(q * k, axis=1, keepdims=True) * v
    st_new = st * jnp.exp(btot) + ops.bmm_nt(k * jnp.exp(btot - b), v)
    y = o * lax.rsqrt(jnp.mean(o * o, axis=1, keepdims=True) + EPS) * og4 * (gx * _sigmoid(gx))
    return st_new, y.reshape(n, c).T


def _mla_pre(ops, cq, ckv, kpe, cos_t, sin_a, sin_b, qag, kvag, qg, kg, wq, wk, wv):
    cqn = _rms(cq, qag)
    ckvn = _rms(ckv, kvag)
    kper = ops.roll(kpe, QK_NOPE)
    qs, ks, vs = [], [], []
    for h in range(MLA_HEADS):
        qh = _slot_norm(ops.mm(cqn, wq[h]), qg, QK_DIM)
        qs.append(_rope(ops, qh, cos_t, sin_a, sin_b))
        kh = _slot_norm(ops.mm(ckvn, wk[h]) + kper, kg, QK_DIM)
        ks.append(_rope(ops, kh, cos_t, sin_a, sin_b))
        vs.append(ops.mm(ckvn, wv[h]))
    return qs, ks, vs


def _outproj(ops, x, ya, yb, o, mog, woa, wob, woc):
    acc = x + ops.mm(ya, woa) + ops.mm(yb, wob)
    for h in range(MLA_HEADS):
        acc = acc + ops.mm(_slot_norm(o[h], mog[h], V_DIM), woc[h])
    return acc


def _ffn_part(ops, x1, g2, w1p, w2p):
    a = ops.mm(_rms(x1, g2), w1p)
    r = jnp.maximum(a, 0.0)
    return ops.mm(r * r, w2p)


def _lower_bounds(r0, r1, r2, r3):
    mx = jnp.maximum(jnp.maximum(r0, r1), jnp.maximum(r2, r3))
    e0, e1, e2, e3 = jnp.exp(r0 - mx), jnp.exp(r1 - mx), jnp.exp(r2 - mx), jnp.exp(r3 - mx)
    inv = 1.0 / (e0 + e1 + e2 + e3)
    s1, s2, s3 = e1 * inv, e2 * inv, e3 * inv
    return jnp.zeros_like(r0), s1, s1 + s2, s1 + s2 + s3


def _cp(sem):
    return pltpu.CompilerParams(dimension_semantics=sem, vmem_limit_bytes=VMEM_LIMIT)


def _rows(tm, n):
    return pl.BlockSpec((tm, n), lambda i: (i, 0))


def _full(a):
    nd = len(a.shape)
    return pl.BlockSpec(a.shape, lambda *_: (0,) * nd, pipeline_mode=pl.Buffered(1))


def _sds(shape, dtype=F32):
    return jax.ShapeDtypeStruct(shape, dtype)


def _acc(ref, val, first):
    @pl.when(first)
    def _():
        ref[...] = val

    @pl.when(jnp.logical_not(first))
    def _():
        ref[...] = ref[...] + val


def _f32(ref):
    return ref[...].astype(F32)


MESH = pl.DeviceIdType.MESH
ANY = pl.BlockSpec(memory_space=pl.ANY)


class Rider:
    def __init__(self, arrays, out_shapes, aliases, sems, start, finish):
        self.arrays, self.out_shapes, self.aliases, self.sems = list(arrays), list(out_shapes), dict(aliases), list(sems)
        self.start, self.finish = start, finish


def run_rider(rider, name):
    n_in, n_out = len(rider.arrays), len(rider.out_shapes)

    def body(*refs):
        ins, outs, sems = refs[:n_in], refs[n_in:n_in + n_out], refs[n_in + n_out:]
        rider.start(ins, outs, sems)
        rider.finish(ins, outs, sems)

    return pl.pallas_call(
        body, name=name, in_specs=[ANY] * n_in, out_specs=[ANY] * n_out, out_shape=rider.out_shapes,
        input_output_aliases=rider.aliases, scratch_shapes=rider.sems,
    )(*rider.arrays)


def _merge_riders(riders):
    bounds, a0, o0, s0 = [], 0, 0, 0
    for r in riders:
        bounds.append((a0, o0, s0))
        a0, o0, s0 = a0 + len(r.arrays), o0 + len(r.out_shapes), s0 + len(r.sems)

    def part(k, ins, outs, sems):
        a, o, s = bounds[k]
        r = riders[k]
        return ins[a:a + len(r.arrays)], outs[o:o + len(r.out_shapes)], sems[s:s + len(r.sems)]

    return Rider(
        [x for r in riders for x in r.arrays], [x for r in riders for x in r.out_shapes],
        {bounds[k][0] + i: bounds[k][1] + o for k, r in enumerate(riders) for i, o in r.aliases.items()},
        [x for r in riders for x in r.sems],
        lambda *refs: [r.start(*part(k, *refs)) for k, r in enumerate(riders)],
        lambda *refs: [r.finish(*part(k, *refs)) for k, r in enumerate(riders)])


def _ride(compute, riders, *, name, grid, in_specs, out_specs, out_shape, operands, scratch_shapes=(), sem=None):
    single = not isinstance(out_shape, (list, tuple))
    if single:
        out_specs, out_shape = [out_specs], [out_shape]
    live = [r for r in riders if r is not None]
    if not live:
        res = pl.pallas_call(compute, name=name, grid=grid, in_specs=in_specs, out_specs=out_specs, out_shape=out_shape,
                             scratch_shapes=list(scratch_shapes), compiler_params=_cp(sem))(*operands)
        return (res[0] if single else res), [None] * len(riders)
    rider = live[0] if len(live) == 1 else _merge_riders(live)
    n_in, n_out, n_s = len(in_specs), len(out_specs), len(scratch_shapes)
    r_in, r_out = len(rider.arrays), len(rider.out_shapes)

    def body(*refs):
        ins, rins = refs[:n_in], refs[n_in:n_in + r_in]
        outs = refs[n_in + r_in:n_in + r_in + n_out]
        routs = refs[n_in + r_in + n_out:n_in + r_in + n_out + r_out]
        scr = refs[n_in + r_in + n_out + r_out:n_in + r_in + n_out + r_out + n_s]
        rsems = refs[n_in + r_in + n_out + r_out + n_s:]
        first = functools.reduce(jnp.logical_and, [pl.program_id(a) == 0 for a in range(len(grid))])
        last = functools.reduce(jnp.logical_and, [pl.program_id(a) == grid[a] - 1 for a in range(len(grid))])

        @pl.when(first)
        def _():
            rider.start(rins, routs, rsems)

        compute(*ins, *outs, *scr)

        @pl.when(last)
        def _():
            rider.finish(rins, routs, rsems)

    res = pl.pallas_call(
        body, name=name, grid=grid, in_specs=list(in_specs) + [ANY] * r_in, out_specs=list(out_specs) + [ANY] * r_out,
        out_shape=list(out_shape) + rider.out_shapes,
        input_output_aliases={n_in + k: n_out + v for k, v in rider.aliases.items()},
        scratch_shapes=list(scratch_shapes) + rider.sems, compiler_params=_cp(("arbitrary",) * len(grid)),
    )(*operands, *rider.arrays)
    main, rest, per_rider = res[:n_out], list(res[n_out:]), []
    for r in riders:
        per_rider.append(None if r is None else [rest.pop(0) for _ in r.out_shapes])
    return (main[0] if single else main), per_rider


def inproj_fwd(x, g1, wa, wb, wc):
    s, d = x.shape

    def body(x_ref, g_ref, wa_ref, wb_ref, wc_ref, pa_ref, pb_ref, pc_ref):
        pa, pb, pc = _inproj(PLAIN, x_ref[...], g_ref[...], wa_ref[...], wb_ref[...], wc_ref[...])
        pa_ref[...] = pa
        pb_ref[...] = pb
        pc_ref[...] = pc

    return pl.pallas_call(
        body, name="inproj_fwd", grid=(s // TM,),
        in_specs=[_rows(TM, d), _full(g1), _full(wa), _full(wb), _full(wc)],
        out_specs=[_rows(TM, wa.shape[1]), _rows(TM, wb.shape[1]), _rows(TM, wc.shape[1])],
        out_shape=[_sds((s, wa.shape[1])), _sds((s, wb.shape[1])), _sds((s, wc.shape[1]))],
        compiler_params=_cp(("parallel",)),
    )(x, g1, wa, wb, wc)


def inproj_bwd(x, g1, wa, wb, wc, dpa, dpb, dpc, dres):
    s, d = x.shape

    def body(x_ref, g_ref, wa_ref, wb_ref, wc_ref, dpa_ref, dpb_ref, dpc_ref, dres_ref,
             dx_ref, dg_ref, dwa_ref, dwb_ref, dwc_ref):
        first = pl.program_id(0) == 0
        _, vjp = jax.vjp(functools.partial(_inproj, AD), x_ref[...], g_ref[...],
                         _f32(wa_ref), _f32(wb_ref), _f32(wc_ref))
        dx, dg, dwa, dwb, dwc = vjp((dpa_ref[...], dpb_ref[...], dpc_ref[...]))
        dx_ref[...] = dx + dres_ref[...]
        _acc(dg_ref, dg, first)
        _acc(dwa_ref, dwa, first)
        _acc(dwb_ref, dwb, first)
        _acc(dwc_ref, dwc, first)

    return pl.pallas_call(
        body, name="inproj_bwd", grid=(s // TM,),
        in_specs=[_rows(TM, d), _full(g1), _full(wa), _full(wb), _full(wc),
                  _rows(TM, wa.shape[1]), _rows(TM, wb.shape[1]), _rows(TM, wc.shape[1]), _rows(TM, d)],
        out_specs=[_rows(TM, d), _full(g1), _full(wa), _full(wb), _full(wc)],
        out_shape=[_sds((s, d)), _sds(g1.shape), _sds(wa.shape), _sds(wb.shape), _sds(wc.shape)],
        compiler_params=_cp(("arbitrary",)),
    )(x, g1, wa, wb, wc, dpa, dpb, dpc, dres)


def gm_fwd(pa, vg, ws4, bs, og):
    s = pa.shape[0]
    w = pa.shape[1] // 2

    def body(pa_ref, vg_ref, ws_ref, bs_ref, og_ref, ya_ref):
        bsl = [bs_ref[h] for h in range(4)]
        ya_ref[...] = _gm_chunk(PLAIN, pa_ref[:, 0:w], pa_ref[:, w:2 * w], vg_ref[...], ws_ref[...], bsl, og_ref[...])

    return pl.pallas_call(
        body, name="gm_fwd", grid=(s // CHUNK,),
        in_specs=[_rows(CHUNK, 2 * w), _full(vg), _full(ws4), _full(bs), _full(og)],
        out_specs=_rows(CHUNK, w), out_shape=_sds((s, w)),
        compiler_params=_cp(("parallel",)),
    )(pa, vg, ws4, bs, og)


def gm_bwd(pa, vg, ws4, bs, og, dya):
    s = pa.shape[0]
    w = pa.shape[1] // 2

    def body(pa_ref, vg_ref, ws_ref, bs_ref, og_ref, dya_ref, dpa_ref, dvg_ref, dws_ref, dbs_ref, dog_ref):
        first = pl.program_id(0) == 0
        bsl = [bs_ref[h] for h in range(4)]
        _, vjp = jax.vjp(functools.partial(_gm_chunk, AD), pa_ref[:, 0:w], pa_ref[:, w:2 * w],
                         vg_ref[...], ws_ref[...], bsl, og_ref[...])
        du, dv, dvg, dws, dbs, dog = vjp(dya_ref[...])
        dpa_ref[:, 0:w] = du
        dpa_ref[:, w:2 * w] = dv
        _acc(dvg_ref, dvg, first)
        _acc(dws_ref, dws, first)
        _acc(dog_ref, dog, first)
        for h in range(4):
            _acc(dbs_ref.at[h], dbs[h], first)

    return pl.pallas_call(
        body, name="gm_bwd", grid=(s // CHUNK,),
        in_specs=[_rows(CHUNK, 2 * w), _full(vg), _full(ws4), _full(bs), _full(og), _rows(CHUNK, w)],
        out_specs=[_rows(CHUNK, 2 * w), _full(vg), _full(ws4), _full(bs), _full(og)],
        out_shape=[_sds((s, 2 * w)), _sds(vg.shape), _sds(ws4.shape), _sds(bs.shape), _sds(og.shape)],
        compiler_params=_cp(("arbitrary",)),
    )(pa, vg, ws4, bs, og, dya)


def hg_fwd(pb, lb, og):
    s = pb.shape[0]
    w = pb.shape[1] // 4
    nc = s // CHUNK
    st_shape = (w // HEAD64, HEAD64, HEAD64)

    def body(pb_ref, lb_ref, og_ref, yb_ref, states_ref, st_ref):
        @pl.when(pl.program_id(0) == 0)
        def _():
            st_ref[...] = jnp.zeros_like(st_ref)

        st = st_ref[...]
        states_ref[...] = st
        st_new, y = _hg_chunk(PLAIN, st, pb_ref[:, 0:w], pb_ref[:, w:2 * w], pb_ref[:, 2 * w:3 * w],
                              pb_ref[:, 3 * w:4 * w], lb_ref[...], og_ref[...])
        st_ref[...] = st_new
        yb_ref[...] = y

    return pl.pallas_call(
        body, name="hg_fwd", grid=(nc,),
        in_specs=[_rows(CHUNK, 4 * w), _full(lb), _full(og)],
        out_specs=[_rows(CHUNK, w), pl.BlockSpec((None,) + st_shape, lambda i: (i, 0, 0, 0))],
        out_shape=[_sds((s, w)), _sds((nc,) + st_shape)],
        scratch_shapes=[pltpu.VMEM(st_shape, F32)],
        compiler_params=_cp(("arbitrary",)),
    )(pb, lb, og)


def hg_bwd(pb, lb, og, states, dyb, riders=()):
    s = pb.shape[0]
    w = pb.shape[1] // 4
    nc = s // CHUNK
    st_shape = (w // HEAD64, HEAD64, HEAD64)

    def body(pb_ref, lb_ref, og_ref, states_ref, dyb_ref, dpb_ref, dlb_ref, dog_ref, dst_ref):
        first = pl.program_id(0) == 0

        @pl.when(first)
        def _():
            dst_ref[...] = jnp.zeros_like(dst_ref)

        _, vjp = jax.vjp(functools.partial(_hg_chunk, AD), states_ref[...], pb_ref[:, 0:w], pb_ref[:, w:2 * w],
                         pb_ref[:, 2 * w:3 * w], pb_ref[:, 3 * w:4 * w], lb_ref[...], og_ref[...])
        dst, dq, df, di, dg, dlb, dog = vjp((dst_ref[...], dyb_ref[...]))
        dst_ref[...] = dst
        dpb_ref[:, 0:w] = dq
        dpb_ref[:, w:2 * w] = df
        dpb_ref[:, 2 * w:3 * w] = di
        dpb_ref[:, 3 * w:4 * w] = dg
        _acc(dlb_ref, dlb, first)
        _acc(dog_ref, dog, first)

    rev = lambda i: (nc - 1 - i, 0)
    return _ride(
        body, riders, name="hg_bwd", grid=(nc,),
        in_specs=[pl.BlockSpec((CHUNK, 4 * w), rev), _full(lb), _full(og),
                  pl.BlockSpec((None,) + st_shape, lambda i: (nc - 1 - i, 0, 0, 0)), pl.BlockSpec((CHUNK, w), rev)],
        out_specs=[pl.BlockSpec((CHUNK, 4 * w), rev), _full(lb), _full(og)],
        out_shape=[_sds((s, 4 * w)), _sds(lb.shape), _sds(og.shape)],
        scratch_shapes=[pltpu.VMEM(st_shape, F32)],
        operands=(pb, lb, og, states, dyb), sem=("arbitrary",))


def lower_bounds_fwd(hlb):
    def body(h_ref, o_ref):
        outs = _lower_bounds(*[h_ref[pl.ds(i, 1), :] for i in range(DEPTH)])
        for i in range(DEPTH):
            o_ref[pl.ds(i, 1), :] = outs[i]

    return pl.pallas_call(body, name="lower_bounds_fwd", out_shape=_sds(hlb.shape))(hlb)


def lower_bounds_bwd(hlb, dlbs):
    def body(h_ref, d_ref, o_ref):
        _, vjp = jax.vjp(_lower_bounds, *[h_ref[pl.ds(i, 1), :] for i in range(DEPTH)])
        outs = vjp(tuple(d_ref[pl.ds(i, 1), :] for i in range(DEPTH)))
        for i in range(DEPTH):
            o_ref[pl.ds(i, 1), :] = outs[i]

    return pl.pallas_call(body, name="lower_bounds_bwd", out_shape=_sds(hlb.shape))(hlb, dlbs)


def _mla_pre_args(pc_ref, cos_ref, sa_ref, sb_ref, qag_ref, kvag_ref, qg_ref, kg_ref, wq_ref, wk_ref, wv_ref, cast):
    sl = lambda h: slice(h * SLOT, (h + 1) * SLOT)
    ld = (lambda r, h: r[:, sl(h)].astype(F32)) if cast else (lambda r, h: r[:, sl(h)])
    diff = (pc_ref[:, 0:Q_LORA], pc_ref[:, Q_LORA:Q_LORA + KV_LORA], pc_ref[:, Q_LORA + KV_LORA:Q_LORA + 2 * KV_LORA],
            qag_ref[...], kvag_ref[...], qg_ref[...], kg_ref[...],
            [ld(wq_ref, h) for h in range(MLA_HEADS)], [ld(wk_ref, h) for h in range(MLA_HEADS)],
            [ld(wv_ref, h) for h in range(MLA_HEADS)])
    tables = (cos_ref[...], sa_ref[...], sb_ref[...])
    return diff, tables


def _mla_pre_fn(ops, tables, cq, ckv, kpe, qag, kvag, qg, kg, wq, wk, wv):
    return _mla_pre(ops, cq, ckv, kpe, *tables, qag, kvag, qg, kg, wq, wk, wv)


def mla_pre_fwd(pc, cos_t, sin_a, sin_b, qag, kvag, qg, kg, wq, wk, wv):
    s = pc.shape[0]
    hw = MLA_HEADS * SLOT

    def body(pc_ref, cos_ref, sa_ref, sb_ref, qag_ref, kvag_ref, qg_ref, kg_ref, wq_ref, wk_ref, wv_ref,
             q_ref, k_ref, v_ref):
        diff, tables = _mla_pre_args(pc_ref, cos_ref, sa_ref, sb_ref, qag_ref, kvag_ref, qg_ref, kg_ref,
                                     wq_ref, wk_ref, wv_ref, False)
        qs, ks, vs = _mla_pre_fn(PLAIN, tables, *diff)
        ones_lane = (lax.broadcasted_iota(jnp.int32, (1, SLOT), 1) == V_DIM).astype(F32)
        for h in range(MLA_HEADS):
            q_ref[:, h * SLOT:(h + 1) * SLOT] = qs[h].astype(q_ref.dtype)
            k_ref[:, h * SLOT:(h + 1) * SLOT] = ks[h].astype(k_ref.dtype)
            v_ref[:, h * SLOT:(h + 1) * SLOT] = (vs[h] + ones_lane).astype(v_ref.dtype)

    return pl.pallas_call(
        body, name="mla_pre_fwd", grid=(s // TM,),
        in_specs=[_rows(TM, pc.shape[1]), _rows(TM, SLOT), _rows(TM, SLOT), _rows(TM, SLOT),
                  _full(qag), _full(kvag), _full(qg), _full(kg), _full(wq), _full(wk), _full(wv)],
        out_specs=[_rows(TM, hw)] * 3, out_shape=[_sds((s, hw), MXU_DTYPE)] * 3,
        compiler_params=_cp(("parallel",)),
    )(pc, cos_t, sin_a, sin_b, qag, kvag, qg, kg, wq, wk, wv)


def mla_pre_bwd(pc, cos_t, sin_a, sin_b, qag, kvag, qg, kg, wq, wk, wv, dq, dk, dv):
    s = pc.shape[0]
    hw = MLA_HEADS * SLOT

    def body(pc_ref, cos_ref, sa_ref, sb_ref, qag_ref, kvag_ref, qg_ref, kg_ref, wq_ref, wk_ref, wv_ref,
             dq_ref, dk_ref, dv_ref, dpc_ref, dqag_ref, dkvag_ref, dqg_ref, dkg_ref, dwq_ref, dwk_ref, dwv_ref):
        first = pl.program_id(0) == 0
        diff, tables = _mla_pre_args(pc_ref, cos_ref, sa_ref, sb_ref, qag_ref, kvag_ref, qg_ref, kg_ref,
                                     wq_ref, wk_ref, wv_ref, True)
        _, vjp = jax.vjp(functools.partial(_mla_pre_fn, AD, tables), *diff)
        sl = lambda h: slice(h * SLOT, (h + 1) * SLOT)
        cot = ([dq_ref[:, sl(h)] for h in range(MLA_HEADS)], [dk_ref[:, sl(h)] for h in range(MLA_HEADS)],
               [dv_ref[:, sl(h)] for h in range(MLA_HEADS)])
        dcq, dckv, dkpe, dqag, dkvag, dqg, dkg, dwq, dwk, dwv = vjp(cot)
        dpc_ref[:, 0:Q_LORA] = dcq
        dpc_ref[:, Q_LORA:Q_LORA + KV_LORA] = dckv
        dpc_ref[:, Q_LORA + KV_LORA:Q_LORA + 2 * KV_LORA] = dkpe
        _acc(dqag_ref, dqag, first)
        _acc(dkvag_ref, dkvag, first)
        _acc(dqg_ref, dqg, first)
        _acc(dkg_ref, dkg, first)
        for h in range(MLA_HEADS):
            _acc(dwq_ref.at[:, sl(h)], dwq[h], first)
            _acc(dwk_ref.at[:, sl(h)], dwk[h], first)
            _acc(dwv_ref.at[:, sl(h)], dwv[h], first)

    return pl.pallas_call(
        body, name="mla_pre_bwd", grid=(s // TM,),
        in_specs=[_rows(TM, pc.shape[1]), _rows(TM, SLOT), _rows(TM, SLOT), _rows(TM, SLOT),
                  _full(qag), _full(kvag), _full(qg), _full(kg), _full(wq), _full(wk), _full(wv),
                  _rows(TM, hw), _rows(TM, hw), _rows(TM, hw)],
        out_specs=[_rows(TM, pc.shape[1]), _full(qag), _full(kvag), _full(qg), _full(kg),
                   _full(wq), _full(wk), _full(wv)],
        out_shape=[_sds(pc.shape), _sds(qag.shape), _sds(kvag.shape), _sds(qg.shape), _sds(kg.shape),
                   _sds(wq.shape), _sds(wk.shape), _sds(wv.shape)],
        compiler_params=_cp(("arbitrary",)),
    )(pc, cos_t, sin_a, sin_b, qag, kvag, qg, kg, wq, wk, wv, dq, dk, dv)


ATT_SCALE = QK_DIM ** -0.5
NEG_BIG = -1e30


def attn_fwd(q, k, v, riders=()):
    s = q.shape[0]
    nq = s // TQ
    hp = ATT_HEADS_PER_STEP
    sl = lambda j: slice(j * SLOT, (j + 1) * SLOT)

    wide = ATT_WIDE // TQ

    def body(q_ref, k_ref, v_ref, o_ref, lse_ref):
        qi = pl.program_id(1)
        lane = lax.broadcasted_iota(jnp.int32, (1, SLOT), 1)
        qs = [q_ref[:, sl(j)] for j in range(hp)]

        def step(ki, carry, n_tiles, masked):
            rk = pl.ds(pl.multiple_of(ki * TQ, TQ), n_tiles * TQ)
            if masked:
                row = lax.broadcasted_iota(jnp.int32, (TQ, n_tiles * TQ), 0) + (n_tiles - 1) * TQ
                col = lax.broadcasted_iota(jnp.int32, (TQ, n_tiles * TQ), 1)
            out = []
            for j in range(hp):
                m, acc = carry[j]
                sc = _dot(qs[j], k_ref[rk, sl(j)], NT) * ATT_SCALE
                if masked:
                    sc = jnp.where(row >= col, sc, NEG_BIG)
                m_new = jnp.maximum(m, jnp.max(sc, axis=-1, keepdims=True))
                acc = jnp.exp(m - m_new) * acc + _dot(jnp.exp(sc - m_new), v_ref[rk, sl(j)], NN)
                out.append((m_new, acc))
            return tuple(out)

        def tail_single(cr):
            cr = lax.fori_loop(n_wide * wide, qi, lambda ki, c: step(ki, c, 1, False), cr)
            return step(qi, cr, 1, True)

        n_wide = qi // wide
        init = tuple((jnp.full((TQ, 1), NEG_BIG, F32), jnp.zeros((TQ, SLOT), F32)) for _ in range(hp))
        carry = lax.fori_loop(0, n_wide, lambda kw, cr: step(kw * wide, cr, wide, False), init)
        carry = lax.cond(qi % wide == wide - 1, lambda cr: step(qi - (wide - 1), cr, wide, True), tail_single, carry)
        for j in range(hp):
            m, acc = carry[j]
            l = jnp.sum(jnp.where(lane == V_DIM, acc, 0.0), axis=-1, keepdims=True)
            o_ref[:, sl(j)] = jnp.where(lane < V_DIM, acc / l, 0.0)
            lse_ref[j] = m + jnp.log(l)

    head_col = pl.BlockSpec((s, hp * SLOT), lambda g, i: (0, g))
    tile = pl.BlockSpec((TQ, hp * SLOT), lambda g, i: (i, g))
    return _ride(
        body, riders, name="attn_fwd", grid=(MLA_HEADS // hp, nq),
        in_specs=[tile, head_col, head_col],
        out_specs=[tile, pl.BlockSpec((hp, TQ, 1), lambda g, i: (g, i, 0))],
        out_shape=[_sds((s, MLA_HEADS * SLOT)), _sds((MLA_HEADS, s, 1))],
        operands=(q, k, v), sem=("parallel", "parallel"))


def attn_bwd(q, k, v, o, do, lse, riders=()):
    s = q.shape[0]
    nq = s // TQ
    hp = ATT_HEADS_PER_STEP
    sl = lambda j: slice(j * SLOT, (j + 1) * SLOT)
    wide = ATT_WIDE // TQ

    def body(q_ref, k_ref, v_ref, o_ref, do_ref, lse_ref, dq_ref, dk_ref, dv_ref, delta_ref):
        ki = pl.program_id(1)

        @pl.when(ki == 0)
        def _():
            dq_ref[...] = jnp.zeros_like(dq_ref)

            def prep(i, c):
                rows = pl.ds(pl.multiple_of(i * TQ, TQ), TQ)
                for j in range(hp):
                    delta_ref[j, rows, :] = jnp.sum(do_ref[rows, sl(j)] * o_ref[rows, sl(j)], axis=-1, keepdims=True)
                return c

            lax.fori_loop(0, nq, prep, 0)

        kks = [k_ref[:, sl(j)] for j in range(hp)]
        vvs = [v_ref[:, sl(j)] for j in range(hp)]

        def step(qi, carry, n_tiles, masked):
            rq = pl.ds(pl.multiple_of(qi * TQ, TQ), n_tiles * TQ)
            if masked:
                row = lax.broadcasted_iota(jnp.int32, (n_tiles * TQ, TQ), 0)
                col = lax.broadcasted_iota(jnp.int32, (n_tiles * TQ, TQ), 1)
            out = []
            for j in range(hp):
                dk, dv = carry[j]
                qq = q_ref[rq, sl(j)]
                dd = do_ref[rq, sl(j)]
                sc = _dot(qq, kks[j], NT) * ATT_SCALE
                if masked:
                    sc = jnp.where(row >= col, sc, NEG_BIG)
                p = jnp.exp(sc - lse_ref[j, rq, :])
                dv = dv + _dot(p, dd, TN)
                ds = p * (_dot(dd, vvs[j], NT) - delta_ref[j, rq, :]) * ATT_SCALE
                dk = dk + _dot(ds, qq, TN)
                dq_ref[rq, sl(j)] = dq_ref[rq, sl(j)] + _dot(ds, kks[j], NN)
                out.append((dk, dv))
            return tuple(out)

        def head_single(cr):
            cr = step(ki, cr, 1, True)
            return lax.fori_loop(ki + 1, first_wide * wide, lambda qi, c: step(qi, c, 1, False), cr)

        zero = jnp.zeros((TQ, SLOT), F32)
        first_wide = (ki + wide) // wide
        carry = tuple((zero, zero) for _ in range(hp))
        carry = lax.cond(ki % wide == 0, lambda cr: step(ki, cr, wide, True), head_single, carry)
        carry = lax.fori_loop(first_wide, nq // wide, lambda qw, cr: step(qw * wide, cr, wide, False), carry)
        for j in range(hp):
            dk_ref[:, sl(j)] = carry[j][0]
            dv_ref[:, sl(j)] = carry[j][1]

    head_col = pl.BlockSpec((s, hp * SLOT), lambda g, i: (0, g))
    tile = pl.BlockSpec((TQ, hp * SLOT), lambda g, i: (i, g))
    return _ride(
        body, riders, name="attn_bwd", grid=(MLA_HEADS // hp, nq),
        in_specs=[head_col, tile, tile, head_col, head_col, pl.BlockSpec((hp, s, 1), lambda g, i: (g, 0, 0))],
        out_specs=[head_col, tile, tile],
        out_shape=[_sds((s, MLA_HEADS * SLOT))] * 3,
        scratch_shapes=[pltpu.VMEM((hp, s, 1), F32)],
        operands=(q, k, v, o, do, lse), sem=("arbitrary", "arbitrary"))


def _outproj_args(ya_ref, yb_ref, o_ref, mog_ref, woa_ref, wob_ref, woc_ref, cast):
    sl = lambda h: slice(h * SLOT, (h + 1) * SLOT)
    ldw = (lambda r: r[...].astype(F32)) if cast else (lambda r: r[...])
    ldc = (lambda h: woc_ref[sl(h), :].astype(F32)) if cast else (lambda h: woc_ref[sl(h), :])
    return (ya_ref[...], yb_ref[...], [o_ref[:, sl(h)] for h in range(MLA_HEADS)],
            [mog_ref[:, sl(h)] for h in range(MLA_HEADS)], ldw(woa_ref), ldw(wob_ref),
            [ldc(h) for h in range(MLA_HEADS)])


def outproj_fwd(x, ya, yb, o, mog, woa, wob, woc):
    s, d = x.shape

    def body(x_ref, ya_ref, yb_ref, o_ref, mog_ref, woa_ref, wob_ref, woc_ref, x1_ref):
        x1_ref[...] = _outproj(PLAIN, x_ref[...], *_outproj_args(ya_ref, yb_ref, o_ref, mog_ref, woa_ref, wob_ref,
                                                                  woc_ref, False))

    return pl.pallas_call(
        body, name="outproj_fwd", grid=(s // TM,),
        in_specs=[_rows(TM, d), _rows(TM, ya.shape[1]), _rows(TM, yb.shape[1]), _rows(TM, o.shape[1]),
                  _full(mog), _full(woa), _full(wob), _full(woc)],
        out_specs=_rows(TM, d), out_shape=_sds((s, d)),
        compiler_params=_cp(("parallel",)),
    )(x, ya, yb, o, mog, woa, wob, woc)


def outproj_bwd(ya, yb, o, mog, woa, wob, woc, dx2, dx1p, riders=()):
    s, d = dx2.shape
    npart = dx1p.shape[0]

    def body(ya_ref, yb_ref, o_ref, mog_ref, woa_ref, wob_ref, woc_ref, dx2_ref, dx1p_ref,
             dx1_ref, dya_ref, dyb_ref, do_ref, dmog_ref, dwoa_ref, dwob_ref, dwoc_ref):
        first = pl.program_id(0) == 0
        sl = lambda h: slice(h * SLOT, (h + 1) * SLOT)
        dx1 = dx2_ref[...]
        for p in range(npart):
            dx1 = dx1 + dx1p_ref[p]
        dx1_ref[...] = dx1
        args = _outproj_args(ya_ref, yb_ref, o_ref, mog_ref, woa_ref, wob_ref, woc_ref, True)
        _, vjp = jax.vjp(lambda *a: _outproj(AD, jnp.zeros_like(dx1), *a), *args)
        dya, dyb, do, dmog, dwoa, dwob, dwoc = vjp(dx1)
        dya_ref[...] = dya
        dyb_ref[...] = dyb
        _acc(dwoa_ref, dwoa, first)
        _acc(dwob_ref, dwob, first)
        for h in range(MLA_HEADS):
            do_ref[:, sl(h)] = do[h]
            _acc(dmog_ref.at[:, sl(h)], dmog[h], first)
            _acc(dwoc_ref.at[sl(h), :], dwoc[h], first)

    return _ride(
        body, riders, name="outproj_bwd", grid=(s // TM,),
        in_specs=[_rows(TM, ya.shape[1]), _rows(TM, yb.shape[1]), _rows(TM, o.shape[1]),
                  _full(mog), _full(woa), _full(wob), _full(woc), _rows(TM, d),
                  pl.BlockSpec((npart, TM, d), lambda i: (0, i, 0))],
        out_specs=[_rows(TM, d), _rows(TM, ya.shape[1]), _rows(TM, yb.shape[1]), _rows(TM, o.shape[1]),
                   _full(mog), _full(woa), _full(wob), _full(woc)],
        out_shape=[_sds((s, d)), _sds(ya.shape), _sds(yb.shape), _sds(o.shape),
                   _sds(mog.shape), _sds(woa.shape), _sds(wob.shape), _sds(woc.shape)],
        operands=(ya, yb, o, mog, woa, wob, woc, dx2, dx1p), sem=("arbitrary",))


def ffn_fwd(x1, g2, w1, w2, riders=()):
    s, d = x1.shape
    npart, _, fs = w1.shape

    def body(x1_ref, g_ref, w1_ref, w2_ref, x2_ref):
        p = pl.program_id(1)
        x1v = x1_ref[...]
        part = _ffn_part(PLAIN, x1v, g_ref[...], w1_ref[...], w2_ref[...])

        @pl.when(p == 0)
        def _():
            x2_ref[...] = x1v + part

        @pl.when(p != 0)
        def _():
            x2_ref[...] = x2_ref[...] + part

    tm = TM_FFN
    return _ride(
        body, riders, name="ffn_fwd", grid=(s // tm, npart),
        in_specs=[pl.BlockSpec((tm, d), lambda i, p: (i, 0)), pl.BlockSpec(g2.shape, lambda i, p: (0, 0)),
                  pl.BlockSpec((None, d, fs), lambda i, p: (p, 0, 0)), pl.BlockSpec((None, fs, d), lambda i, p: (p, 0, 0))],
        out_specs=pl.BlockSpec((tm, d), lambda i, p: (i, 0)), out_shape=_sds((s, d)),
        operands=(x1, g2, w1, w2), sem=("parallel", "arbitrary"))


def ffn_bwd(x1, g2, w1, w2, dx2, riders=()):
    s, d = x1.shape
    npart, _, fs = w1.shape

    def body(x1_ref, g_ref, w1_ref, w2_ref, dx2_ref, dx1p_ref, dg_ref, dw1_ref, dw2_ref):
        p = pl.program_id(0)
        i = pl.program_id(1)
        _, vjp = jax.vjp(functools.partial(_ffn_part, AD), x1_ref[...], g_ref[...], _f32(w1_ref), _f32(w2_ref))
        dx1, dg, dw1, dw2 = vjp(dx2_ref[...])
        dx1p_ref[...] = dx1
        _acc(dg_ref, dg, (p == 0) & (i == 0))
        _acc(dw1_ref, dw1, i == 0)
        _acc(dw2_ref, dw2, i == 0)

    tm = TM_FFN
    return _ride(
        body, riders, name="ffn_bwd", grid=(npart, s // tm),
        in_specs=[pl.BlockSpec((tm, d), lambda p, i: (i, 0)), pl.BlockSpec(g2.shape, lambda p, i: (0, 0)),
                  pl.BlockSpec((None, d, fs), lambda p, i: (p, 0, 0)), pl.BlockSpec((None, fs, d), lambda p, i: (p, 0, 0)),
                  pl.BlockSpec((tm, d), lambda p, i: (i, 0))],
        out_specs=[pl.BlockSpec((None, tm, d), lambda p, i: (p, i, 0)), pl.BlockSpec(g2.shape, lambda p, i: (0, 0)),
                   pl.BlockSpec((None, d, fs), lambda p, i: (p, 0, 0)), pl.BlockSpec((None, fs, d), lambda p, i: (p, 0, 0))],
        out_shape=[_sds((npart, s, d)), _sds(g2.shape), _sds(w1.shape), _sds(w2.shape)],
        operands=(x1, g2, w1, w2, dx2), sem=("arbitrary", "arbitrary"))


def loss_head(y, target):
    s, d = y.shape

    def body(y_ref, t_ref, dy_ref, loss_ref):
        err = y_ref[...] - t_ref[...]
        dy_ref[...] = err * (1.0 / d)
        part = jnp.sum(jnp.sum(err * err, axis=-1, keepdims=True), axis=0, keepdims=True) * (0.5 / d)
        _acc(loss_ref, jnp.broadcast_to(part, loss_ref.shape), pl.program_id(0) == 0)

    return pl.pallas_call(
        body, name="loss_head", grid=(s // TM,),
        in_specs=[_rows(TM, d), _rows(TM, d)],
        out_specs=[_rows(TM, d), pl.BlockSpec((1, SLOT), lambda i: (0, 0))],
        out_shape=[_sds((s, d)), _sds((1, SLOT))],
        compiler_params=_cp(("arbitrary",)),
    )(y, target)


def _row_block(r):
    for b in (512, 256, 128, 64, 32, 16, 8):
        if r % b == 0:
            return b
    return r


def sum_cores(a, got, half, me):
    nch, _, r, c = a.shape
    br = _row_block(r)

    def body(sp_ref, a_ref, g_ref, wire_ref, own_ref):
        tot = a_ref[...] + g_ref[...]
        wire_ref[...] = tot.astype(wire_ref.dtype)

        @pl.when(pl.program_id(1) == sp_ref[1])
        def _():
            own_ref[...] = tot

    grid_spec = pltpu.PrefetchScalarGridSpec(
        num_scalar_prefetch=1, grid=(r // br, nch),
        in_specs=[pl.BlockSpec((None, None, br, c), lambda i, p, sp: (p, sp[0], i, 0)),
                  pl.BlockSpec((None, br, c), lambda i, p, sp: (p, i, 0))],
        out_specs=[pl.BlockSpec((None, br, c), lambda i, p, sp: (p, i, 0)), pl.BlockSpec((br, c), lambda i, p, sp: (i, 0))])
    return pl.pallas_call(body, name="sum_cores", grid_spec=grid_spec, out_shape=[_sds((nch, r, c), BF16), _sds((r, c))],
                          compiler_params=_cp(("parallel", "arbitrary")))(jnp.stack([half, me]).astype(jnp.int32), a, got)


def sum_chips(own, recv, half):
    r, c = own.shape
    br = _row_block(r)

    def body(sp_ref, own_ref, r0_ref, r1_ref, r2_ref, out_ref):
        del sp_ref
        out_ref[...] = ((own_ref[...] + r0_ref[...].astype(F32)) + r1_ref[...].astype(F32)) + r2_ref[...].astype(F32)

    grid_spec = pltpu.PrefetchScalarGridSpec(
        num_scalar_prefetch=1, grid=(r // br,),
        in_specs=[pl.BlockSpec((br, c), lambda i, sp: (i, 0))]
        + [pl.BlockSpec((None, br, c), functools.partial(lambda i, sp, j: (j, i, 0), j=j)) for j in range(3)],
        out_specs=pl.BlockSpec((None, br, c), lambda i, sp: (sp[0], i, 0)))
    return pl.pallas_call(body, name="sum_chips", grid_spec=grid_spec, out_shape=_sds((2, r, c)),
                          compiler_params=_cp(("parallel",)))(half.reshape(1).astype(jnp.int32), own, recv, recv, recv)


def adamw(w, g, m, v, name):
    r, c = w.shape
    br = _row_block(r)
    c1 = 1.0 / (1.0 - ADAM_B1 ** ADAM_STEP)
    c2 = 1.0 / (1.0 - ADAM_B2 ** ADAM_STEP)

    def body(w_ref, g_ref, m_ref, v_ref, d_ref, nm_ref, nv_ref):
        gg = g_ref[...]
        nm = ADAM_B1 * m_ref[...] + (1.0 - ADAM_B1) * gg
        nv = ADAM_B2 * v_ref[...] + (1.0 - ADAM_B2) * (gg * gg)
        d_ref[...] = -ADAM_LR * ((nm * c1) / (jnp.sqrt(nv * c2) + ADAM_EPS) + ADAM_WD * w_ref[...])
        nm_ref[...] = nm
        nv_ref[...] = nv

    return pl.pallas_call(
        body, name=name, grid=(r // br,), in_specs=[_rows(br, c)] * 4, out_specs=[_rows(br, c)] * 3,
        out_shape=[_sds((r, c))] * 3, compiler_params=_cp(("parallel",)),
    )(w, g, m, v)


def _place():
    x, y, c = lax.axis_index("x"), lax.axis_index("y"), lax.axis_index("c")
    chips = [(1 - x, y), (x, 1 - y), (1 - x, 1 - y)]
    return x, y, c, chips


def _remote(src, dst, send_sem, recv_sem, to):
    return pltpu.make_async_remote_copy(src_ref=src, dst_ref=dst, send_sem=send_sem, recv_sem=recv_sem,
                                        device_id=to, device_id_type=MESH)


def gather_rider(arrs):
    n = len(arrs)
    me_chip = 2 * lax.axis_index("x") + lax.axis_index("y")
    bufs = [lax.dynamic_update_index_in_dim(lax.empty((N_CHIPS,) + a.shape, a.dtype), a, me_chip, 0) for a in arrs]

    def plan(ins, outs, sems):
        send_sems, recv_sems = sems
        x, y, c, chips = _place()
        me = 2 * x + y
        half, other, sibling = pl.ds(2 * c, 2), pl.ds(2 - 2 * c, 2), (x, y, 1 - c)
        cp = lambda i, k, src, dst, to: _remote(src, dst, send_sems.at[i, k], recv_sems.at[i, k], to)
        pairs = [(i, j, cx, cy) for i in range(n) for j, (cx, cy) in enumerate(chips)]
        blk = lambda i, cx, cy, part: outs[i].at[2 * cx + cy, part]
        first = lambda: [cp(i, j, ins[i].at[half], outs[i].at[me, half], (cx, cy, c)) for i, j, cx, cy in pairs]
        landed = lambda: [cp(i, j, blk(i, cx, cy, half), blk(i, cx, cy, half), (cx, cy, c)) for i, j, cx, cy in pairs]
        passed = lambda: [cp(i, 3 + j, blk(i, cx, cy, half), blk(i, cx, cy, half), sibling) for i, j, cx, cy in pairs]
        from_sibling = lambda: [cp(i, 3 + j, blk(i, cx, cy, other), blk(i, cx, cy, other), sibling) for i, j, cx, cy in pairs]
        return first, landed, passed, from_sibling

    def start(ins, outs, sems):
        for cp in plan(ins, outs, sems)[0]():
            cp.start()

    def finish(ins, outs, sems):
        first, landed, passed, from_sibling = plan(ins, outs, sems)
        forwards = passed()
        for a, b in zip(landed(), forwards):
            a.wait_recv()
            b.start()
        for cp in from_sibling():
            cp.wait_recv()
        for cp in first() + forwards:
            cp.wait_send()

    return Rider(list(arrs) + bufs, [_sds((N_CHIPS,) + a.shape, a.dtype) for a in arrs], {n + i: i for i in range(n)},
                 [pltpu.SemaphoreType.DMA((n, 6)), pltpu.SemaphoreType.DMA((n, 6))], start, finish)


class Reducer:
    def __init__(self, arrs):
        self.a = list(arrs)
        self.n = len(self.a)
        self.c = lax.axis_index("c")
        self.me = 2 * lax.axis_index("x") + lax.axis_index("y")

    def swap_rider(self):
        n = self.n

        def plan(ins, outs, sems):
            x, y, c, _ = _place()
            return [_remote(ins[i].at[p, 1 - c], outs[i].at[p], sems[0].at[i, p], sems[1].at[i, p], (x, y, 1 - c))
                    for i in range(n) for p in range(N_CHIPS)]

        return Rider(self.a, [_sds((N_CHIPS,) + a.shape[2:]) for a in self.a], {},
                     [pltpu.SemaphoreType.DMA((n, N_CHIPS)), pltpu.SemaphoreType.DMA((n, N_CHIPS))],
                     lambda *r: [cp.start() for cp in plan(*r)], lambda *r: [cp.wait() for cp in plan(*r)])

    def after_swap(self, got):
        pairs = [sum_cores(a, g, self.c, self.me) for a, g in zip(self.a, got)]
        self.wire, self.own = [p[0] for p in pairs], [p[1] for p in pairs]

    def scatter_rider(self):
        n = self.n

        def plan(ins, outs, sems):
            x, y, c, chips = _place()
            return [_remote(ins[i].at[2 * cx + cy], outs[i].at[j], sems[0].at[i, j], sems[1].at[i, j], (cx, cy, c))
                    for i in range(n) for j, (cx, cy) in enumerate(chips)]

        return Rider(self.wire, [_sds((3,) + w.shape[1:], w.dtype) for w in self.wire], {},
                     [pltpu.SemaphoreType.DMA((n, 3)), pltpu.SemaphoreType.DMA((n, 3))],
                     lambda *r: [cp.start() for cp in plan(*r)], lambda *r: [cp.wait() for cp in plan(*r)])

    def after_scatter(self, recv):
        self.full = [sum_chips(o, r, self.c) for o, r in zip(self.own, recv)]

    def share_rider(self):
        n = self.n

        def plan(ins, outs, sems):
            x, y, c, _ = _place()
            return [_remote(ins[i].at[c], outs[i].at[c], sems[0].at[i], sems[1].at[i], (x, y, 1 - c)) for i in range(n)]

        return Rider(self.full, [_sds(f.shape) for f in self.full], {i: i for i in range(n)},
                     [pltpu.SemaphoreType.DMA((n,)), pltpu.SemaphoreType.DMA((n,))],
                     lambda *r: [cp.start() for cp in plan(*r)], lambda *r: [cp.wait() for cp in plan(*r)])

    def run(self):
        self.after_swap(run_rider(self.swap_rider(), "swap_halves"))
        self.after_scatter(run_rider(self.scatter_rider(), "scatter_chips"))
        return run_rider(self.share_rider(), "share_halves")


def _pad_slots(a, live):
    lead = a.shape[:-1]
    a = a.reshape(lead + (MLA_HEADS, live))
    a = jnp.pad(a, [(0, 0)] * len(lead) + [(0, 0), (0, SLOT - live)])
    return a.reshape(lead + (MLA_HEADS * SLOT,))


def _unpad_slots(a, live):
    lead = a.shape[:-1]
    return a.reshape(lead + (MLA_HEADS, SLOT))[..., :live].reshape(lead + (MLA_HEADS * live,))


def _rope_tables(positions, s):
    half = QK_ROPE // 2
    inv_freq = ROPE_THETA ** (-jnp.arange(half, dtype=F32) / half)
    ang = positions.reshape(s).astype(F32)[:, None] * inv_freq[None, :]
    cos, sin = jnp.cos(ang), jnp.sin(ang)
    one = jnp.ones((s, QK_NOPE), F32)
    z64, z16, z32 = jnp.zeros((s, QK_NOPE), F32), jnp.zeros((s, half), F32), jnp.zeros((s, SLOT - QK_DIM), F32)
    cos_t = jnp.concatenate([one, cos, cos, z32], axis=1)
    sin_a = jnp.concatenate([z64, -sin, z16, z32], axis=1)
    sin_b = jnp.concatenate([z64, z16, sin, z32], axis=1)
    return cos_t, sin_a, sin_b


def _layer_weights(full, small, l):
    w_in = jnp.concatenate([full["w_in"][p] for p in range(N_CHIPS)], axis=1)
    wc = jnp.pad(w_in[:, 1536:], ((0, 0), (0, 512 - (w_in.shape[1] - 1536))))
    w_uq = jnp.concatenate([full["mla_w_uq"][p] for p in range(N_CHIPS)], axis=1)
    w_ukv = jnp.concatenate([full["mla_w_ukv"][p] for p in range(N_CHIPS)], axis=1)
    ukv = w_ukv.reshape(KV_LORA, MLA_HEADS, QK_NOPE + V_DIM)
    w_out = jnp.concatenate([full["w_out"][p] for p in range(N_CHIPS)], axis=0)
    woc = w_out[512:].reshape(MLA_HEADS, V_DIM, D_MODEL)
    woc = jnp.pad(woc, ((0, 0), (0, SLOT - V_DIM), (0, 0))).reshape(MLA_HEADS * SLOT, D_MODEL)
    row = lambda a: a.reshape(1, -1)
    return dict(
        g1=row(small["norm1_gain"][l]), wa=w_in[:, :512], wb=w_in[:, 512:1536], wc=wc,
        vg=row(small["gm_v_gain"][l]), ws=small["gm_w_s"][l], bs=small["gm_b_s"][l].reshape(4, CHUNK, 1),
        gog=row(small["gm_out_gain"][l]), hog=small["hg_out_gain"][l].reshape(-1, 1),
        qag=row(small["mla_q_a_gain"][l]), kvag=row(small["mla_kv_a_gain"][l]),
        qg=row(jnp.pad(small["mla_q_gain"][l], (0, SLOT - QK_DIM))), kg=row(jnp.pad(small["mla_k_gain"][l], (0, SLOT - QK_DIM))),
        wq=_pad_slots(w_uq, QK_DIM), wk=_pad_slots(ukv[..., :QK_NOPE].reshape(KV_LORA, -1), QK_NOPE),
        wv=_pad_slots(ukv[..., QK_NOPE:].reshape(KV_LORA, -1), V_DIM),
        mog=row(_pad_slots(small["mla_out_gain"][l], V_DIM)),
        woa=w_out[:256], wob=w_out[256:512], woc=woc,
        g2=row(small["norm2_gain"][l]),
    )


def _shard_cols(a):
    r, c4 = a.shape
    return a.reshape(r, N_CHIPS, c4 // N_CHIPS).transpose(1, 0, 2)


def local_step(x, positions, target, small, comm):
    s = x.shape[0]
    cos_t, sin_a, sin_b = _rope_tables(positions, s)
    lbs = lower_bounds_fwd(small["hg_lower_bound"])
    lw, saved = [], []
    for l in range(DEPTH):
        w = _layer_weights(comm.part(l, "mix"), small, l)
        lw.append(w)
        lb = lbs[l].reshape(-1, 1)
        pa, pb, pc = inproj_fwd(x, w["g1"], w["wa"], w["wb"], w["wc"])
        ya = gm_fwd(pa, w["vg"], w["ws"], w["bs"], w["gog"])
        yb, states = hg_fwd(pb, lb, w["hog"])
        q, k, v = mla_pre_fwd(pc, cos_t, sin_a, sin_b, w["qag"], w["kvag"], w["qg"], w["kg"], w["wq"], w["wk"], w["wv"])
        rider = comm.gather_rider(l, "ffn")
        (o, lse), got = attn_fwd(q, k, v, [rider])
        comm.gathered(l, "ffn", got[0])
        x1 = outproj_fwd(x, ya, yb, o, w["mog"], w["woa"], w["wob"], w["woc"])
        ffn_w = comm.part(l, "ffn")
        w["w1"], w["w2"] = ffn_w["w_ff1"], ffn_w["w_ff2"]
        rider = comm.gather_rider(l + 1, "mix") if l + 1 < DEPTH else None
        x2, got = ffn_fwd(x1, w["g2"], w["w1"], w["w2"], [rider])
        comm.gathered(l + 1, "mix", got[0])
        saved.append(dict(x=x, pa=pa, pb=pb, pc=pc, ya=ya, yb=yb, states=states, q=q, k=k, v=v, o=o, lse=lse, x1=x1, lb=lb))
        x = x2
    dx, loss_part = loss_head(x, target)
    groups = [dict() for _ in range(DEPTH)]
    sm = {n: [None] * DEPTH for n in ("norm1_gain", "gm_v_gain", "gm_w_s", "gm_b_s", "gm_out_gain", "hg_out_gain",
                                       "mla_q_a_gain", "mla_kv_a_gain", "mla_q_gain", "mla_k_gain", "mla_out_gain",
                                       "norm2_gain")}
    dlbs = [None] * DEPTH
    halves = lambda g: g.reshape(N_CHIPS, 2, g.shape[1] // 2, g.shape[2])
    take = lambda red, f: None if red is None else f(red)
    red_mix = None
    for l in reversed(range(DEPTH)):
        w, a = lw[l], saved[l]
        (dx1p, dg2, dw1, dw2), got = ffn_bwd(a["x1"], w["g2"], w["w1"], w["w2"], dx, [take(red_mix, Reducer.swap_rider)])
        if red_mix:
            red_mix.after_swap(got[0])
        ffn_arrs = [halves(dw1), halves(dw2)]
        red_ffn = comm.reducer(ffn_arrs)
        (dx1, dya, dyb, do, dmog, dwoa, dwob, dwoc), got = outproj_bwd(
            a["ya"], a["yb"], a["o"], w["mog"], w["woa"], w["wob"], w["woc"], dx, dx1p, [take(red_ffn, Reducer.swap_rider)])
        if red_ffn:
            red_ffn.after_swap(got[0])
        (dq, dk, dv), got = attn_bwd(a["q"], a["k"], a["v"], a["o"], do, a["lse"],
                                     [take(red_mix, Reducer.scatter_rider), take(red_ffn, Reducer.scatter_rider)])
        for red, g in zip((red_mix, red_ffn), got):
            if red:
                red.after_scatter(g)
        dpc, dqag, dkvag, dqg, dkg, dwq, dwk, dwv = mla_pre_bwd(a["pc"], cos_t, sin_a, sin_b, w["qag"], w["kvag"], w["qg"],
                                                                  w["kg"], w["wq"], w["wk"], w["wv"], dq, dk, dv)
        (dpb, dlb, dhog), got = hg_bwd(a["pb"], a["lb"], w["hog"], a["states"], dyb,
                                       [take(red_mix, Reducer.share_rider), take(red_ffn, Reducer.share_rider)])
        if red_mix:
            groups[l + 1].update(zip(MIX, got[0]))
        groups[l].update(zip(FFN, got[1] if red_ffn else ffn_arrs))
        dpa, dvg, dws, dbs, dgog = gm_bwd(a["pa"], w["vg"], w["ws"], w["bs"], w["gog"], dya)
        dx, dg1, dwa, dwb, dwc = inproj_bwd(a["x"], w["g1"], w["wa"], w["wb"], w["wc"], dpa, dpb, dpc, dx1)
        dukv = jnp.concatenate([dwk.reshape(KV_LORA, MLA_HEADS, SLOT)[..., :QK_NOPE],
                                dwv.reshape(KV_LORA, MLA_HEADS, SLOT)[..., :V_DIM]], axis=-1)
        dwo = jnp.concatenate([dwoa, dwob, dwoc.reshape(MLA_HEADS, SLOT, D_MODEL)[:, :V_DIM].reshape(-1, D_MODEL)], axis=0)
        mix_arrs = [halves(_shard_cols(jnp.concatenate([dwa, dwb, dwc[:, :1952 - 1536]], axis=1))),
                    halves(_shard_cols(_unpad_slots(dwq, QK_DIM))), halves(_shard_cols(dukv.reshape(KV_LORA, -1))),
                    halves(dwo.reshape(N_CHIPS, -1, D_MODEL))]
        red_mix = comm.reducer(mix_arrs) if l > 0 else None
        if red_mix is None:
            groups[l].update(zip(MIX, mix_arrs))
        sm["norm1_gain"][l] = dg1[0]
        sm["gm_v_gain"][l] = dvg[0]
        sm["gm_w_s"][l] = dws
        sm["gm_b_s"][l] = dbs[..., 0]
        sm["gm_out_gain"][l] = dgog[0]
        sm["hg_out_gain"][l] = dhog[:, 0]
        sm["mla_q_a_gain"][l] = dqag[0]
        sm["mla_kv_a_gain"][l] = dkvag[0]
        sm["mla_q_gain"][l] = dqg[0, :QK_DIM]
        sm["mla_k_gain"][l] = dkg[0, :QK_DIM]
        sm["mla_out_gain"][l] = _unpad_slots(dmog[0], V_DIM)
        sm["norm2_gain"][l] = dg2[0]
        dlbs[l] = dlb[:, 0]
    sm = {n: jnp.stack(v) for n, v in sm.items()}
    sm["hg_lower_bound"] = lower_bounds_bwd(small["hg_lower_bound"], jnp.stack(dlbs))
    return loss_part, dx, groups, sm


MIX = ("w_in", "mla_w_uq", "mla_w_ukv", "w_out")
FFN = ("w_ff1", "w_ff2")
BIG = MIX + FFN
PARTS = {"mix": MIX, "ffn": FFN}
SMALL = ("norm1_gain", "gm_v_gain", "gm_w_s", "gm_b_s", "gm_out_gain", "hg_lower_bound", "hg_out_gain",
         "mla_q_a_gain", "mla_kv_a_gain", "mla_q_gain", "mla_k_gain", "mla_out_gain", "norm2_gain")
ORDER = ("norm1_gain", "w_in", "gm_v_gain", "gm_w_s", "gm_b_s", "gm_out_gain", "hg_lower_bound", "hg_out_gain",
         "mla_q_a_gain", "mla_w_uq", "mla_kv_a_gain", "mla_w_ukv", "mla_q_gain", "mla_k_gain", "mla_out_gain",
         "w_out", "norm2_gain", "w_ff1", "w_ff2")
PACK_ROWS = 288


def _pack(arrs):
    flat = jnp.concatenate([a.reshape(-1) for a in arrs])
    total = 2 * N_CHIPS * PACK_ROWS * SLOT
    return jnp.pad(flat, (0, total - flat.shape[0]))


def _unpack(flat, shapes):
    out, off = [], 0
    for sh in shapes:
        size = 1
        for d in sh:
            size *= d
        out.append(flat[off:off + size].reshape(sh))
        off += size
    return out


class ChipComm:
    def __init__(self, shards):
        self.shards = shards
        self.full = {}

    def gather_rider(self, l, part):
        return gather_rider([self.shards[n][l].astype(MXU_DTYPE).reshape(4, self.shards[n].shape[1] // 4, -1)
                             for n in PARTS[part]])

    def gathered(self, l, part, outs):
        if outs is not None:
            self.full[l, part] = {n: o.reshape((N_CHIPS,) + self.shards[n].shape[1:]) for n, o in zip(PARTS[part], outs)}

    def part(self, l, part):
        if (l, part) not in self.full:
            self.gathered(l, part, run_rider(self.gather_rider(l, part), "gather_weights"))
        return self.full[l, part]

    def reducer(self, arrs):
        return Reducer(arrs)


def kernel(x, positions, norm1_gain, w_in, gm_v_gain, gm_w_s, gm_b_s, gm_out_gain, hg_lower_bound, hg_out_gain, mla_q_a_gain, mla_w_uq, mla_kv_a_gain, mla_w_ukv, mla_q_gain, mla_k_gain, mla_out_gain, w_out, norm2_gain, w_ff1, w_ff2, loss_target, m_norm1_gain, m_w_in, m_gm_v_gain, m_gm_w_s, m_gm_b_s, m_gm_out_gain, m_hg_lower_bound, m_hg_out_gain, m_mla_q_a_gain, m_mla_w_uq, m_mla_kv_a_gain, m_mla_w_ukv, m_mla_q_gain, m_mla_k_gain, m_mla_out_gain, m_w_out, m_norm2_gain, m_w_ff1, m_w_ff2, v_norm1_gain, v_w_in, v_gm_v_gain, v_gm_w_s, v_gm_b_s, v_gm_out_gain, v_hg_lower_bound, v_hg_out_gain, v_mla_q_a_gain, v_mla_w_uq, v_mla_kv_a_gain, v_mla_w_ukv, v_mla_q_gain, v_mla_k_gain, v_mla_out_gain, v_w_out, v_norm2_gain, v_w_ff1, v_w_ff2):
    given = dict(locals())
    weights = {n: given[n] for n in ORDER}
    moms = {n: given["m_" + n] for n in ORDER}
    vars_ = {n: given["v_" + n] for n in ORDER}
    s, d = x.shape[1], x.shape[2]

    small = {n: weights[n] for n in SMALL}
    comm = ChipComm({n: weights[n] for n in BIG})
    loss_part, dx, groups, small_g = local_step(x.reshape(s, d), positions, loss_target.reshape(s, d), small, comm)
    loss = lax.psum(loss_part[0, 0], ("x", "y", "c"))

    pack_g = _pack([small_g[n] for n in SMALL]).reshape(N_CHIPS, 2, PACK_ROWS, SLOT)
    last = Reducer([groups[0][n] for n in MIX] + [pack_g]).run()
    groups[0].update(zip(MIX, last[:-1]))
    pack_full = run_rider(gather_rider([last[-1].reshape(4, PACK_ROWS // 2, SLOT)]), "gather_small")[0].reshape(-1)
    grads = {n: jnp.stack([groups[l][n].reshape(weights[n].shape[1:]) for l in range(DEPTH)]) for n in BIG}
    grads.update(zip(SMALL, _unpack(pack_full, [weights[n].shape for n in SMALL])))

    delta, new_m, new_v = {}, {}, {}
    flat2 = lambda a: a.reshape(-1, a.shape[-1])
    for n in BIG:
        outs = adamw(flat2(weights[n]), flat2(grads[n]), flat2(moms[n]), flat2(vars_[n]), "adamw_" + n)
        delta[n], new_m[n], new_v[n] = [o.reshape(weights[n].shape) for o in outs]
    shapes = [weights[n].shape for n in SMALL]
    pk = lambda t: _pack([t[n] for n in SMALL]).reshape(-1, SLOT)
    outs = adamw(pk(weights), pack_full.reshape(-1, SLOT), pk(moms), pk(vars_), "adamw_small")
    for tgt, o in zip((delta, new_m, new_v), outs):
        tgt.update(zip(SMALL, _unpack(o.reshape(-1), shapes)))

    return (loss, dx.reshape(x.shape), *[grads[n] for n in ORDER], *[delta[n] for n in ORDER],
            *[new_m[n] for n in ORDER], *[new_v[n] for n in ORDER])
```

```python
import functools

import jax
import jax.numpy as jnp
from jax import lax
from jax.experimental import pallas as pl
from jax.experimental.pallas import tpu as pltpu

F32 = jnp.float32
BF16 = jnp.bfloat16
MXU_DTYPE = BF16

D_MODEL = 1024
DEPTH = 4
CHUNK = 128
HG_CHUNK = 128
HG_CHUNKS = 2
GM_CHUNKS = 4
EPS = 1e-6
HEAD64 = 64
MLA_HEADS = 8
QK_NOPE = 64
QK_ROPE = 32
QK_DIM = 96
V_DIM = 64
Q_LORA = 256
KV_LORA = 128
SLOT = 128
ROPE_THETA = 10000.0
D_FF_SHARD = 1024
N_CHIPS = 4

ADAM_LR = 0.001
ADAM_B1 = 0.9
ADAM_B2 = 0.999
ADAM_EPS = 1e-08
ADAM_WD = 0.01
ADAM_STEP = 10

TM = 512
TM_FFN = 512
TQ = 256
ATT_HEADS_PER_STEP = 2
ATT_WIDE = 512
VMEM_LIMIT = 56 * 1024 * 1024

NN = (((1,), (0,)), ((), ()))
NT = (((1,), (1,)), ((), ()))
TN = (((0,), (0,)), ((), ()))
BNN = (((2,), (1,)), ((0,), (0,)))
BNT = (((2,), (2,)), ((0,), (0,)))
BTN = (((1,), (1,)), ((0,), (0,)))


def _dot(a, b, dims):
    return lax.dot_general(a.astype(MXU_DTYPE), b.astype(MXU_DTYPE), dims, preferred_element_type=F32)


def _hdot(a, b, dims=NN):
    return lax.dot_general(a, b, dims, precision=lax.Precision.HIGHEST, preferred_element_type=F32)


def _make_ad(dims, da_dims, da_swap, db_dims, db_swap):
    @jax.custom_vjp
    def f(a, b):
        return _dot(a, b, dims)

    def fwd(a, b):
        return _dot(a, b, dims), (a, b)

    def bwd(res, g):
        a, b = res
        da = _dot(b, g, da_dims) if da_swap else _dot(g, b, da_dims)
        db = _dot(g, a, db_dims) if db_swap else _dot(a, g, db_dims)
        return da, db

    f.defvjp(fwd, bwd)
    return f


@functools.partial(jax.custom_vjp, nondiff_argnums=(1,))
def _roll_ad(x, shift):
    return pltpu.roll(x, shift, 1)


def _roll_ad_fwd(x, shift):
    return pltpu.roll(x, shift, 1), None


def _roll_ad_bwd(shift, _, g):
    return (pltpu.roll(g, (g.shape[1] - shift) % g.shape[1], 1),)


_roll_ad.defvjp(_roll_ad_fwd, _roll_ad_bwd)


class _Ops:
    pass


PLAIN = _Ops()
PLAIN.mm = lambda a, b: _dot(a, b, NN)
PLAIN.bmm = lambda a, b: _dot(a, b, BNN)
PLAIN.bmm_nt = lambda a, b: _dot(a, b, BNT)
PLAIN.bmm_tn = lambda a, b: _dot(a, b, BTN)
PLAIN.roll = lambda x, s: pltpu.roll(x, s, 1)

AD = _Ops()
AD.mm = _make_ad(NN, NT, False, TN, False)
AD.bmm = _make_ad(BNN, BNT, False, BTN, False)
AD.bmm_nt = _make_ad(BNT, BNN, False, BTN, True)
AD.bmm_tn = _make_ad(BTN, BNT, True, BNN, False)
AD.roll = _roll_ad


def _sigmoid(x):
    return jax.nn.sigmoid(x)


def _gelu(x):
    return 0.5 * x * (1.0 + jnp.tanh(0.7978845608028654 * (x + 0.044715 * (x * x * x))))


def _rms(x, g):
    return x * lax.rsqrt(jnp.mean(x * x, axis=-1, keepdims=True) + EPS) * g


def _head_masks256():
    lane = lax.broadcasted_iota(jnp.int32, (1, 4 * HEAD64), 1)
    return [(jnp.right_shift(lane, 6) == h).astype(F32) for h in range(4)]


def _headnorm256(x, g):
    ms = jnp.zeros_like(x)
    sq = x * x
    for m in _head_masks256():
        ms = ms + m * (jnp.sum(sq * m, axis=-1, keepdims=True) * (1.0 / HEAD64))
    return x * lax.rsqrt(ms + EPS) * g


def _slot_norm(x, g, n):
    return x * lax.rsqrt(jnp.sum(x * x, axis=-1, keepdims=True) * (1.0 / n) + EPS) * g


def _rope(ops, x, cos_t, sin_a, sin_b):
    return x * cos_t + ops.roll(x, SLOT - QK_ROPE // 2) * sin_a + ops.roll(x, QK_ROPE // 2) * sin_b


def _inproj(ops, x, g1, wa, wb, wc):
    h = _rms(x, g1)
    return ops.mm(h, wa), ops.mm(h, wb), ops.mm(h, wc)


def _gm_chunk(ops, ur, vr, vg, ws4, bs, og):
    c = ur.shape[0]
    masks = _head_masks256()
    mh = jnp.concatenate([m[None] for m in masks], axis=0)
    u = _gelu(ur)
    v = _headnorm256(_gelu(vr), vg)
    t = lax.broadcasted_iota(jnp.int32, (c, c), 0)
    s = lax.broadcasted_iota(jnp.int32, (c, c), 1)
    w = jnp.where((t >= s)[None], ws4, 0.0)
    y = jnp.sum(ops.bmm(w, v[None] * mh), axis=0)
    for h in range(4):
        y = y + bs[h] * masks[h]
    return _headnorm256(u * y, og)


def _hg_chunk(ops, st, qr, fr, ir, gr, lb, og):
    c, n = qr.shape
    nh = n // HEAD64
    heads = lambda x: x.reshape(nh, HEAD64, x.shape[-1])
    tr = lambda x: heads(x.T)
    lb4, og4 = heads(lb), heads(og)
    qx = tr(qr)
    q = qx * _sigmoid(qx)
    f = lb4 + (1.0 - lb4) * _sigmoid(tr(fr))
    k = 1.0 - f
    logf = jnp.log(f)
    v = tr(ir)
    gx = tr(gr)
    s = lax.broadcasted_iota(jnp.int32, (c, c), 0)
    t = lax.broadcasted_iota(jnp.int32, (c, c), 1)
    tl = lax.broadcasted_iota(jnp.int32, (1, c), 1).reshape(1, 1, c)
    b2 = _hdot(logf.reshape(n, c), (s <= t).astype(F32))
    b = heads(b2)
    btot = jnp.sum(logf, axis=2, keepdims=True)
    inter = ops.bmm_tn(st, q * jnp.exp(b))
    p4 = jnp.zeros((nh, c, c), F32)
    tt, ss = s, t
    lg = c.bit_length() - 2
    while lg >= 0:
        m = 1 << lg
        bnd = jnp.left_shift(jnp.right_shift(t, lg + 1), lg + 1) + (m - 1)
        r = heads(_hdot(b2, (s == bnd).astype(F32)))
        right = jnp.bitwise_and(jnp.right_shift(tl, lg), 1) == 1
        qe = jnp.where(right, q * jnp.exp(jnp.where(right, b - r, 0.0)), 0.0)
        ke = jnp.where(right, 0.0, k * jnp.exp(jnp.where(right, 0.0, r - b)))
        lm = ((jnp.right_shift(tt, lg + 1) == jnp.right_shift(ss, lg + 1))
              & (jnp.bitwise_and(jnp.right_shift(tt, lg), 1) == 1)
              & (jnp.bitwise_and(jnp.right_shift(ss, lg), 1) == 0))
        p4 = jnp.where(lm[None], ops.bmm_tn(qe, ke), p4)
        lg -= 1
    intra = ops.bmm_nt(v, p4)
    o = inter + intra + jnp.sum(q * k, axis=1, keepdims=True) * v
    st_new = st * jnp.exp(btot) + ops.bmm_nt(k * jnp.exp(btot - b), v)
    y = o * lax.rsqrt(jnp.mean(o * o, axis=1, keepdims=True) + EPS) * og4 * (gx * _sigmoid(gx))
    return st_new, y.reshape(n, c).T


def _mla_pre(ops, cq, ckv, kpe, cos_t, sin_a, sin_b, qag, kvag, qg, kg, wq, wk, wv):
    cqn = _rms(cq, qag)
    ckvn = _rms(ckv, kvag)
    kper = ops.roll(kpe, QK_NOPE)
    qs, ks, vs = [], [], []
    for h in range(MLA_HEADS):
        qh = _slot_norm(ops.mm(cqn, wq[h]), qg, QK_DIM)
        qs.append(_rope(ops, qh, cos_t, sin_a, sin_b))
        kh = _slot_norm(ops.mm(ckvn, wk[h]) + kper, kg, QK_DIM)
        ks.append(_rope(ops, kh, cos_t, sin_a, sin_b))
        vs.append(ops.mm(ckvn, wv[h]))
    return qs, ks, vs


def _outproj(ops, x, ya, yb, o, mog, woa, wob, woc):
    acc = x + ops.mm(ya, woa) + ops.mm(yb, wob)
    for h in range(MLA_HEADS):
        acc = acc + ops.mm(_slot_norm(o[h], mog[h], V_DIM), woc[h])
    return acc


def _ffn_part(ops, x1, g2, w1p, w2p):
    a = ops.mm(_rms(x1, g2), w1p)
    r = jnp.maximum(a, 0.0)
    return ops.mm(r * r, w2p)


def _lower_bounds(r0, r1, r2, r3):
    mx = jnp.maximum(jnp.maximum(r0, r1), jnp.maximum(r2, r3))
    e0, e1, e2, e3 = jnp.exp(r0 - mx), jnp.exp(r1 - mx), jnp.exp(r2 - mx), jnp.exp(r3 - mx)
    inv = 1.0 / (e0 + e1 + e2 + e3)
    s1, s2, s3 = e1 * inv, e2 * inv, e3 * inv
    return jnp.zeros_like(r0), s1, s1 + s2, s1 + s2 + s3


def _cp(sem):
    return pltpu.CompilerParams(dimension_semantics=sem, vmem_limit_bytes=VMEM_LIMIT)


def _rows(tm, n):
    return pl.BlockSpec((tm, n), lambda i: (i, 0))


def _full(a):
    nd = len(a.shape)
    return pl.BlockSpec(a.shape, lambda *_: (0,) * nd, pipeline_mode=pl.Buffered(1))


def _sds(shape, dtype=F32):
    return jax.ShapeDtypeStruct(shape, dtype)


def _acc(ref, val, first):
    @pl.when(first)
    def _():
        ref[...] = val

    @pl.when(jnp.logical_not(first))
    def _():
        ref[...] = ref[...] + val


def _f32(ref):
    return ref[...].astype(F32)


MESH = pl.DeviceIdType.MESH
ANY = pl.BlockSpec(memory_space=pl.ANY)


class Rider:
    def __init__(self, arrays, out_shapes, aliases, sems, start, finish):
        self.arrays, self.out_shapes, self.aliases, self.sems = list(arrays), list(out_shapes), dict(aliases), list(sems)
        self.start, self.finish = start, finish


def run_rider(rider, name):
    n_in, n_out = len(rider.arrays), len(rider.out_shapes)

    def body(*refs):
        ins, outs, sems = refs[:n_in], refs[n_in:n_in + n_out], refs[n_in + n_out:]
        rider.start(ins, outs, sems)
        rider.finish(ins, outs, sems)

    return pl.pallas_call(
        body, name=name, in_specs=[ANY] * n_in, out_specs=[ANY] * n_out, out_shape=rider.out_shapes,
        input_output_aliases=rider.aliases, scratch_shapes=rider.sems,
    )(*rider.arrays)


def _merge_riders(riders):
    bounds, a0, o0, s0 = [], 0, 0, 0
    for r in riders:
        bounds.append((a0, o0, s0))
        a0, o0, s0 = a0 + len(r.arrays), o0 + len(r.out_shapes), s0 + len(r.sems)

    def part(k, ins, outs, sems):
        a, o, s = bounds[k]
        r = riders[k]
        return ins[a:a + len(r.arrays)], outs[o:o + len(r.out_shapes)], sems[s:s + len(r.sems)]

    return Rider(
        [x for r in riders for x in r.arrays], [x for r in riders for x in r.out_shapes],
        {bounds[k][0] + i: bounds[k][1] + o for k, r in enumerate(riders) for i, o in r.aliases.items()},
        [x for r in riders for x in r.sems],
        lambda *refs: [r.start(*part(k, *refs)) for k, r in enumerate(riders)],
        lambda *refs: [r.finish(*part(k, *refs)) for k, r in enumerate(riders)])


def _ride(compute, riders, *, name, grid, in_specs, out_specs, out_shape, operands, scratch_shapes=(), sem=None):
    single = not isinstance(out_shape, (list, tuple))
    if single:
        out_specs, out_shape = [out_specs], [out_shape]
    live = [r for r in riders if r is not None]
    if not live:
        res = pl.pallas_call(compute, name=name, grid=grid, in_specs=in_specs, out_specs=out_specs, out_shape=out_shape,
                             scratch_shapes=list(scratch_shapes), compiler_params=_cp(sem))(*operands)
        return (res[0] if single else res), [None] * len(riders)
    rider = live[0] if len(live) == 1 else _merge_riders(live)
    n_in, n_out, n_s = len(in_specs), len(out_specs), len(scratch_shapes)
    r_in, r_out = len(rider.arrays), len(rider.out_shapes)

    def body(*refs):
        ins, rins = refs[:n_in], refs[n_in:n_in + r_in]
        outs = refs[n_in + r_in:n_in + r_in + n_out]
        routs = refs[n_in + r_in + n_out:n_in + r_in + n_out + r_out]
        scr = refs[n_in + r_in + n_out + r_out:n_in + r_in + n_out + r_out + n_s]
        rsems = refs[n_in + r_in + n_out + r_out + n_s:]
        first = functools.reduce(jnp.logical_and, [pl.program_id(a) == 0 for a in range(len(grid))])
        last = functools.reduce(jnp.logical_and, [pl.program_id(a) == grid[a] - 1 for a in range(len(grid))])

        @pl.when(first)
        def _():
            rider.start(rins, routs, rsems)

        compute(*ins, *outs, *scr)

        @pl.when(last)
        def _():
            rider.finish(rins, routs, rsems)

    res = pl.pallas_call(
        body, name=name, grid=grid, in_specs=list(in_specs) + [ANY] * r_in, out_specs=list(out_specs) + [ANY] * r_out,
        out_shape=list(out_shape) + rider.out_shapes,
        input_output_aliases={n_in + k: n_out + v for k, v in rider.aliases.items()},
        scratch_shapes=list(scratch_shapes) + rider.sems, compiler_params=_cp(("arbitrary",) * len(grid)),
    )(*operands, *rider.arrays)
    main, rest, per_rider = res[:n_out], list(res[n_out:]), []
    for r in riders:
        per_rider.append(None if r is None else [rest.pop(0) for _ in r.out_shapes])
    return (main[0] if single else main), per_rider


def inproj_fwd(x, g1, wa, wb, wc):
    s, d = x.shape

    def body(x_ref, g_ref, wa_ref, wb_ref, wc_ref, pa_ref, pb_ref, pc_ref):
        pa, pb, pc = _inproj(PLAIN, x_ref[...], g_ref[...], wa_ref[...], wb_ref[...], wc_ref[...])
        pa_ref[...] = pa
        pb_ref[...] = pb
        pc_ref[...] = pc

    return pl.pallas_call(
        body, name="inproj_fwd", grid=(s // TM,),
        in_specs=[_rows(TM, d), _full(g1), _full(wa), _full(wb), _full(wc)],
        out_specs=[_rows(TM, wa.shape[1]), _rows(TM, wb.shape[1]), _rows(TM, wc.shape[1])],
        out_shape=[_sds((s, wa.shape[1])), _sds((s, wb.shape[1])), _sds((s, wc.shape[1]))],
        compiler_params=_cp(("parallel",)),
    )(x, g1, wa, wb, wc)


def inproj_bwd(x, g1, wa, wb, wc, dpa, dpb, dpc, dres):
    s, d = x.shape

    def body(x_ref, g_ref, wa_ref, wb_ref, wc_ref, dpa_ref, dpb_ref, dpc_ref, dres_ref,
             dx_ref, dg_ref, dwa_ref, dwb_ref, dwc_ref):
        first = pl.program_id(0) == 0
        _, vjp = jax.vjp(functools.partial(_inproj, AD), x_ref[...], g_ref[...],
                         _f32(wa_ref), _f32(wb_ref), _f32(wc_ref))
        dx, dg, dwa, dwb, dwc = vjp((dpa_ref[...], dpb_ref[...], dpc_ref[...]))
        dx_ref[...] = dx + dres_ref[...]
        _acc(dg_ref, dg, first)
        _acc(dwa_ref, dwa, first)
        _acc(dwb_ref, dwb, first)
        _acc(dwc_ref, dwc, first)

    return pl.pallas_call(
        body, name="inproj_bwd", grid=(s // TM,),
        in_specs=[_rows(TM, d), _full(g1), _full(wa), _full(wb), _full(wc),
                  _rows(TM, wa.shape[1]), _rows(TM, wb.shape[1]), _rows(TM, wc.shape[1]), _rows(TM, d)],
        out_specs=[_rows(TM, d), _full(g1), _full(wa), _full(wb), _full(wc)],
        out_shape=[_sds((s, d)), _sds(g1.shape), _sds(wa.shape), _sds(wb.shape), _sds(wc.shape)],
        compiler_params=_cp(("arbitrary",)),
    )(x, g1, wa, wb, wc, dpa, dpb, dpc, dres)


def gm_fwd(pa, vg, ws4, bs, og):
    s = pa.shape[0]
    w = pa.shape[1] // 2

    def body(pa_ref, vg_ref, ws_ref, bs_ref, og_ref, ya_ref):
        bsl = [bs_ref[h] for h in range(4)]
        for j in range(GM_CHUNKS):
            rows = slice(j * CHUNK, (j + 1) * CHUNK)
            ya_ref[rows, :] = _gm_chunk(PLAIN, pa_ref[rows, 0:w], pa_ref[rows, w:2 * w], vg_ref[...], ws_ref[...], bsl,
                                        og_ref[...])

    tm = GM_CHUNKS * CHUNK
    return pl.pallas_call(
        body, name="gm_fwd", grid=(s // tm,),
        in_specs=[_rows(tm, 2 * w), _full(vg), _full(ws4), _full(bs), _full(og)],
        out_specs=_rows(tm, w), out_shape=_sds((s, w)),
        compiler_params=_cp(("parallel",)),
    )(pa, vg, ws4, bs, og)


def gm_bwd(pa, vg, ws4, bs, og, dya):
    s = pa.shape[0]
    w = pa.shape[1] // 2

    def body(pa_ref, vg_ref, ws_ref, bs_ref, og_ref, dya_ref, dpa_ref, dvg_ref, dws_ref, dbs_ref, dog_ref):
        first = pl.program_id(0) == 0
        bsl = [bs_ref[h] for h in range(4)]
        tot = None
        for j in range(GM_CHUNKS):
            rows = slice(j * CHUNK, (j + 1) * CHUNK)
            _, vjp = jax.vjp(functools.partial(_gm_chunk, AD), pa_ref[rows, 0:w], pa_ref[rows, w:2 * w],
                             vg_ref[...], ws_ref[...], bsl, og_ref[...])
            du, dv, *dws = vjp(dya_ref[rows, :])
            dpa_ref[rows, 0:w] = du
            dpa_ref[rows, w:2 * w] = dv
            tot = dws if tot is None else jax.tree.map(jnp.add, tot, dws)
        dvg, dws, dbs, dog = tot
        _acc(dvg_ref, dvg, first)
        _acc(dws_ref, dws, first)
        _acc(dog_ref, dog, first)
        for h in range(4):
            _acc(dbs_ref.at[h], dbs[h], first)

    tm = GM_CHUNKS * CHUNK
    return pl.pallas_call(
        body, name="gm_bwd", grid=(s // tm,),
        in_specs=[_rows(tm, 2 * w), _full(vg), _full(ws4), _full(bs), _full(og), _rows(tm, w)],
        out_specs=[_rows(tm, 2 * w), _full(vg), _full(ws4), _full(bs), _full(og)],
        out_shape=[_sds((s, 2 * w)), _sds(vg.shape), _sds(ws4.shape), _sds(bs.shape), _sds(og.shape)],
        compiler_params=_cp(("arbitrary",)),
    )(pa, vg, ws4, bs, og, dya)


def hg_fwd(pb, lb, og):
    s = pb.shape[0]
    w = pb.shape[1] // 4
    tm = HG_CHUNKS * HG_CHUNK
    st_shape = (w // HEAD64, HEAD64, HEAD64)

    def body(pb_ref, lb_ref, og_ref, yb_ref, states_ref, st_ref):
        @pl.when(pl.program_id(0) == 0)
        def _():
            st_ref[...] = jnp.zeros_like(st_ref)

        st = st_ref[...]
        for j in range(HG_CHUNKS):
            rows = slice(j * HG_CHUNK, (j + 1) * HG_CHUNK)
            states_ref[j] = st
            st, y = _hg_chunk(PLAIN, st, pb_ref[rows, 0:w], pb_ref[rows, w:2 * w], pb_ref[rows, 2 * w:3 * w],
                              pb_ref[rows, 3 * w:4 * w], lb_ref[...], og_ref[...])
            yb_ref[rows, :] = y
        st_ref[...] = st

    return pl.pallas_call(
        body, name="hg_fwd", grid=(s // tm,),
        in_specs=[_rows(tm, 4 * w), _full(lb), _full(og)],
        out_specs=[_rows(tm, w), pl.BlockSpec((HG_CHUNKS,) + st_shape, lambda i: (i, 0, 0, 0))],
        out_shape=[_sds((s, w)), _sds((s // HG_CHUNK,) + st_shape)],
        scratch_shapes=[pltpu.VMEM(st_shape, F32)],
        compiler_params=_cp(("arbitrary",)),
    )(pb, lb, og)


def hg_bwd(pb, lb, og, states, dyb, riders=()):
    s = pb.shape[0]
    w = pb.shape[1] // 4
    tm = HG_CHUNKS * HG_CHUNK
    nc = s // tm
    st_shape = (w // HEAD64, HEAD64, HEAD64)

    def body(pb_ref, lb_ref, og_ref, states_ref, dyb_ref, dpb_ref, dlb_ref, dog_ref, dst_ref):
        first = pl.program_id(0) == 0

        @pl.when(first)
        def _():
            dst_ref[...] = jnp.zeros_like(dst_ref)

        dst, dlb, dog = dst_ref[...], None, None
        for j in reversed(range(HG_CHUNKS)):
            rows = slice(j * HG_CHUNK, (j + 1) * HG_CHUNK)
            _, vjp = jax.vjp(functools.partial(_hg_chunk, AD), states_ref[j], pb_ref[rows, 0:w], pb_ref[rows, w:2 * w],
                             pb_ref[rows, 2 * w:3 * w], pb_ref[rows, 3 * w:4 * w], lb_ref[...], og_ref[...])
            dst, dq, df, di, dg, dlb_j, dog_j = vjp((dst, dyb_ref[rows, :]))
            dpb_ref[rows, 0:w] = dq
            dpb_ref[rows, w:2 * w] = df
            dpb_ref[rows, 2 * w:3 * w] = di
            dpb_ref[rows, 3 * w:4 * w] = dg
            dlb = dlb_j if dlb is None else dlb + dlb_j
            dog = dog_j if dog is None else dog + dog_j
        dst_ref[...] = dst
        _acc(dlb_ref, dlb, first)
        _acc(dog_ref, dog, first)

    rev = lambda i: (nc - 1 - i, 0)
    return _ride(
        body, riders, name="hg_bwd", grid=(nc,),
        in_specs=[pl.BlockSpec((tm, 4 * w), rev), _full(lb), _full(og),
                  pl.BlockSpec((HG_CHUNKS,) + st_shape, lambda i: (nc - 1 - i, 0, 0, 0)), pl.BlockSpec((tm, w), rev)],
        out_specs=[pl.BlockSpec((tm, 4 * w), rev), _full(lb), _full(og)],
        out_shape=[_sds((s, 4 * w)), _sds(lb.shape), _sds(og.shape)],
        scratch_shapes=[pltpu.VMEM(st_shape, F32)],
        operands=(pb, lb, og, states, dyb), sem=("arbitrary",))


def lower_bounds_fwd(hlb):
    def body(h_ref, o_ref):
        outs = _lower_bounds(*[h_ref[pl.ds(i, 1), :] for i in range(DEPTH)])
        for i in range(DEPTH):
            o_ref[pl.ds(i, 1), :] = outs[i]

    return pl.pallas_call(body, name="lower_bounds_fwd", out_shape=_sds(hlb.shape))(hlb)


def lower_bounds_bwd(hlb, dlbs):
    def body(h_ref, d_ref, o_ref):
        _, vjp = jax.vjp(_lower_bounds, *[h_ref[pl.ds(i, 1), :] for i in range(DEPTH)])
        outs = vjp(tuple(d_ref[pl.ds(i, 1), :] for i in range(DEPTH)))
        for i in range(DEPTH):
            o_ref[pl.ds(i, 1), :] = outs[i]

    return pl.pallas_call(body, name="lower_bounds_bwd", out_shape=_sds(hlb.shape))(hlb, dlbs)


def _mla_pre_args(pc_ref, cos_ref, sa_ref, sb_ref, qag_ref, kvag_ref, qg_ref, kg_ref, wq_ref, wk_ref, wv_ref, cast):
    sl = lambda h: slice(h * SLOT, (h + 1) * SLOT)
    ld = (lambda r, h: r[:, sl(h)].astype(F32)) if cast else (lambda r, h: r[:, sl(h)])
    diff = (pc_ref[:, 0:Q_LORA], pc_ref[:, Q_LORA:Q_LORA + KV_LORA], pc_ref[:, Q_LORA + KV_LORA:Q_LORA + 2 * KV_LORA],
            qag_ref[...], kvag_ref[...], qg_ref[...], kg_ref[...],
            [ld(wq_ref, h) for h in range(MLA_HEADS)], [ld(wk_ref, h) for h in range(MLA_HEADS)],
            [ld(wv_ref, h) for h in range(MLA_HEADS)])
    tables = (cos_ref[...], sa_ref[...], sb_ref[...])
    return diff, tables


def _mla_pre_fn(ops, tables, cq, ckv, kpe, qag, kvag, qg, kg, wq, wk, wv):
    return _mla_pre(ops, cq, ckv, kpe, *tables, qag, kvag, qg, kg, wq, wk, wv)


def mla_pre_fwd(pc, cos_t, sin_a, sin_b, qag, kvag, qg, kg, wq, wk, wv):
    s = pc.shape[0]
    hw = MLA_HEADS * SLOT

    def body(pc_ref, cos_ref, sa_ref, sb_ref, qag_ref, kvag_ref, qg_ref, kg_ref, wq_ref, wk_ref, wv_ref,
             q_ref, k_ref, v_ref):
        diff, tables = _mla_pre_args(pc_ref, cos_ref, sa_ref, sb_ref, qag_ref, kvag_ref, qg_ref, kg_ref,
                                     wq_ref, wk_ref, wv_ref, False)
        qs, ks, vs = _mla_pre_fn(PLAIN, tables, *diff)
        ones_lane = (lax.broadcasted_iota(jnp.int32, (1, SLOT), 1) == V_DIM).astype(F32)
        for h in range(MLA_HEADS):
            q_ref[:, h * SLOT:(h + 1) * SLOT] = qs[h].astype(q_ref.dtype)
            k_ref[:, h * SLOT:(h + 1) * SLOT] = ks[h].astype(k_ref.dtype)
            v_ref[:, h * SLOT:(h + 1) * SLOT] = (vs[h] + ones_lane).astype(v_ref.dtype)

    return pl.pallas_call(
        body, name="mla_pre_fwd", grid=(s // TM,),
        in_specs=[_rows(TM, pc.shape[1]), _rows(TM, SLOT), _rows(TM, SLOT), _rows(TM, SLOT),
                  _full(qag), _full(kvag), _full(qg), _full(kg), _full(wq), _full(wk), _full(wv)],
        out_specs=[_rows(TM, hw)] * 3, out_shape=[_sds((s, hw), MXU_DTYPE)] * 3,
        compiler_params=_cp(("parallel",)),
    )(pc, cos_t, sin_a, sin_b, qag, kvag, qg, kg, wq, wk, wv)


def mla_pre_bwd(pc, cos_t, sin_a, sin_b, qag, kvag, qg, kg, wq, wk, wv, dq, dk, dv):
    s = pc.shape[0]
    hw = MLA_HEADS * SLOT

    def body(pc_ref, cos_ref, sa_ref, sb_ref, qag_ref, kvag_ref, qg_ref, kg_ref, wq_ref, wk_ref, wv_ref,
             dq_ref, dk_ref, dv_ref, dpc_ref, dqag_ref, dkvag_ref, dqg_ref, dkg_ref, dwq_ref, dwk_ref, dwv_ref):
        first = pl.program_id(0) == 0
        diff, tables = _mla_pre_args(pc_ref, cos_ref, sa_ref, sb_ref, qag_ref, kvag_ref, qg_ref, kg_ref,
                                     wq_ref, wk_ref, wv_ref, True)
        _, vjp = jax.vjp(functools.partial(_mla_pre_fn, AD, tables), *diff)
        sl = lambda h: slice(h * SLOT, (h + 1) * SLOT)
        cot = ([dq_ref[:, sl(h)] for h in range(MLA_HEADS)], [dk_ref[:, sl(h)] for h in range(MLA_HEADS)],
               [dv_ref[:, sl(h)] for h in range(MLA_HEADS)])
        dcq, dckv, dkpe, dqag, dkvag, dqg, dkg, dwq, dwk, dwv = vjp(cot)
        dpc_ref[:, 0:Q_LORA] = dcq
        dpc_ref[:, Q_LORA:Q_LORA + KV_LORA] = dckv
        dpc_ref[:, Q_LORA + KV_LORA:Q_LORA + 2 * KV_LORA] = dkpe
        _acc(dqag_ref, dqag, first)
        _acc(dkvag_ref, dkvag, first)
        _acc(dqg_ref, dqg, first)
        _acc(dkg_ref, dkg, first)
        for h in range(MLA_HEADS):
            _acc(dwq_ref.at[:, sl(h)], dwq[h], first)
            _acc(dwk_ref.at[:, sl(h)], dwk[h], first)
            _acc(dwv_ref.at[:, sl(h)], dwv[h], first)

    return pl.pallas_call(
        body, name="mla_pre_bwd", grid=(s // TM,),
        in_specs=[_rows(TM, pc.shape[1]), _rows(TM, SLOT), _rows(TM, SLOT), _rows(TM, SLOT),
                  _full(qag), _full(kvag), _full(qg), _full(kg), _full(wq), _full(wk), _full(wv),
                  _rows(TM, hw), _rows(TM, hw), _rows(TM, hw)],
        out_specs=[_rows(TM, pc.shape[1]), _full(qag), _full(kvag), _full(qg), _full(kg),
                   _full(wq), _full(wk), _full(wv)],
        out_shape=[_sds(pc.shape), _sds(qag.shape), _sds(kvag.shape), _sds(qg.shape), _sds(kg.shape),
                   _sds(wq.shape), _sds(wk.shape), _sds(wv.shape)],
        compiler_params=_cp(("arbitrary",)),
    )(pc, cos_t, sin_a, sin_b, qag, kvag, qg, kg, wq, wk, wv, dq, dk, dv)


ATT_SCALE = QK_DIM ** -0.5
NEG_BIG = -1e30


def attn_fwd(q, k, v, riders=()):
    s = q.shape[0]
    nq = s // TQ
    hp = ATT_HEADS_PER_STEP
    sl = lambda j: slice(j * SLOT, (j + 1) * SLOT)

    wide = ATT_WIDE // TQ

    def body(q_ref, k_ref, v_ref, o_ref, lse_ref):
        qi = pl.program_id(1)
        lane = lax.broadcasted_iota(jnp.int32, (1, SLOT), 1)
        qs = [q_ref[:, sl(j)] for j in range(hp)]

        def step(ki, carry, n_tiles, masked):
            rk = pl.ds(pl.multiple_of(ki * TQ, TQ), n_tiles * TQ)
            if masked:
                row = lax.broadcasted_iota(jnp.int32, (TQ, n_tiles * TQ), 0) + (n_tiles - 1) * TQ
                col = lax.broadcasted_iota(jnp.int32, (TQ, n_tiles * TQ), 1)
            out = []
            for j in range(hp):
                m, acc = carry[j]
                sc = _dot(qs[j], k_ref[rk, sl(j)], NT) * ATT_SCALE
                if masked:
                    sc = jnp.where(row >= col, sc, NEG_BIG)
                m_new = jnp.maximum(m, jnp.max(sc, axis=-1, keepdims=True))
                acc = jnp.exp(m - m_new) * acc + _dot(jnp.exp(sc - m_new), v_ref[rk, sl(j)], NN)
                out.append((m_new, acc))
            return tuple(out)

        def tail_single(cr):
            cr = lax.fori_loop(n_wide * wide, qi, lambda ki, c: step(ki, c, 1, False), cr)
            return step(qi, cr, 1, True)

        n_wide = qi // wide
        init = tuple((jnp.full((TQ, 1), NEG_BIG, F32), jnp.zeros((TQ, SLOT), F32)) for _ in range(hp))
        carry = lax.fori_loop(0, n_wide, lambda kw, cr: step(kw * wide, cr, wide, False), init)
        carry = lax.cond(qi % wide == wide - 1, lambda cr: step(qi - (wide - 1), cr, wide, True), tail_single, carry)
        for j in range(hp):
            m, acc = carry[j]
            l = jnp.sum(jnp.where(lane == V_DIM, acc, 0.0), axis=-1, keepdims=True)
            o_ref[:, sl(j)] = jnp.where(lane < V_DIM, acc / l, 0.0)
            lse_ref[j] = m + jnp.log(l)

    head_col = pl.BlockSpec((s, hp * SLOT), lambda g, i: (0, g))
    tile = pl.BlockSpec((TQ, hp * SLOT), lambda g, i: (i, g))
    return _ride(
        body, riders, name="attn_fwd", grid=(MLA_HEADS // hp, nq),
        in_specs=[tile, head_col, head_col],
        out_specs=[tile, pl.BlockSpec((hp, TQ, 1), lambda g, i: (g, i, 0))],
        out_shape=[_sds((s, MLA_HEADS * SLOT)), _sds((MLA_HEADS, s, 1))],
        operands=(q, k, v), sem=("parallel", "parallel"))


def attn_bwd(q, k, v, o, do, lse, riders=()):
    s = q.shape[0]
    nq = s // TQ
    hp = ATT_HEADS_PER_STEP
    sl = lambda j: slice(j * SLOT, (j + 1) * SLOT)
    wide = ATT_WIDE // TQ

    def body(q_ref, k_ref, v_ref, o_ref, do_ref, lse_ref, dq_ref, dk_ref, dv_ref, delta_ref):
        ki = pl.program_id(1)

        @pl.when(ki == 0)
        def _():
            dq_ref[...] = jnp.zeros_like(dq_ref)

            def prep(i, c):
                rows = pl.ds(pl.multiple_of(i * TQ, TQ), TQ)
                for j in range(hp):
                    delta_ref[j, rows, :] = jnp.sum(do_ref[rows, sl(j)] * o_ref[rows, sl(j)], axis=-1, keepdims=True)
                return c

            lax.fori_loop(0, nq, prep, 0)

        kks = [k_ref[:, sl(j)] for j in range(hp)]
        vvs = [v_ref[:, sl(j)] for j in range(hp)]

        def step(qi, carry, n_tiles, masked):
            rq = pl.ds(pl.multiple_of(qi * TQ, TQ), n_tiles * TQ)
            if masked:
                row = lax.broadcasted_iota(jnp.int32, (n_tiles * TQ, TQ), 0)
                col = lax.broadcasted_iota(jnp.int32, (n_tiles * TQ, TQ), 1)
            out = []
            for j in range(hp):
                dk, dv = carry[j]
                qq = q_ref[rq, sl(j)]
                dd = do_ref[rq, sl(j)]
                sc = _dot(qq, kks[j], NT) * ATT_SCALE
                if masked:
                    sc = jnp.where(row >= col, sc, NEG_BIG)
                p = jnp.exp(sc - lse_ref[j, rq, :])
                dv = dv + _dot(p, dd, TN)
                ds = p * (_dot(dd, vvs[j], NT) - delta_ref[j, rq, :]) * ATT_SCALE
                dk = dk + _dot(ds, qq, TN)
                dq_ref[rq, sl(j)] = dq_ref[rq, sl(j)] + _dot(ds, kks[j], NN)
                out.append((dk, dv))
            return tuple(out)

        def head_single(cr):
            cr = step(ki, cr, 1, True)
            return lax.fori_loop(ki + 1, first_wide * wide, lambda qi, c: step(qi, c, 1, False), cr)

        zero = jnp.zeros((TQ, SLOT), F32)
        first_wide = (ki + wide) // wide
        carry = tuple((zero, zero) for _ in range(hp))
        carry = lax.cond(ki % wide == 0, lambda cr: step(ki, cr, wide, True), head_single, carry)
        carry = lax.fori_loop(first_wide, nq // wide, lambda qw, cr: step(qw * wide, cr, wide, False), carry)
        for j in range(hp):
            dk_ref[:, sl(j)] = carry[j][0]
            dv_ref[:, sl(j)] = carry[j][1]

    head_col = pl.BlockSpec((s, hp * SLOT), lambda g, i: (0, g))
    tile = pl.BlockSpec((TQ, hp * SLOT), lambda g, i: (i, g))
    return _ride(
        body, riders, name="attn_bwd", grid=(MLA_HEADS // hp, nq),
        in_specs=[head_col, tile, tile, head_col, head_col, pl.BlockSpec((hp, s, 1), lambda g, i: (g, 0, 0))],
        out_specs=[head_col, tile, tile],
        out_shape=[_sds((s, MLA_HEADS * SLOT))] * 3,
        scratch_shapes=[pltpu.VMEM((hp, s, 1), F32)],
        operands=(q, k, v, o, do, lse), sem=("arbitrary", "arbitrary"))


def _outproj_args(ya_ref, yb_ref, o_ref, mog_ref, woa_ref, wob_ref, woc_ref, cast):
    sl = lambda h: slice(h * SLOT, (h + 1) * SLOT)
    ldw = (lambda r: r[...].astype(F32)) if cast else (lambda r: r[...])
    ldc = (lambda h: woc_ref[sl(h), :].astype(F32)) if cast else (lambda h: woc_ref[sl(h), :])
    return (ya_ref[...], yb_ref[...], [o_ref[:, sl(h)] for h in range(MLA_HEADS)],
            [mog_ref[:, sl(h)] for h in range(MLA_HEADS)], ldw(woa_ref), ldw(wob_ref),
            [ldc(h) for h in range(MLA_HEADS)])


def outproj_fwd(x, ya, yb, o, mog, woa, wob, woc):
    s, d = x.shape

    def body(x_ref, ya_ref, yb_ref, o_ref, mog_ref, woa_ref, wob_ref, woc_ref, x1_ref):
        x1_ref[...] = _outproj(PLAIN, x_ref[...], *_outproj_args(ya_ref, yb_ref, o_ref, mog_ref, woa_ref, wob_ref,
                                                                  woc_ref, False))

    return pl.pallas_call(
        body, name="outproj_fwd", grid=(s // TM,),
        in_specs=[_rows(TM, d), _rows(TM, ya.shape[1]), _rows(TM, yb.shape[1]), _rows(TM, o.shape[1]),
                  _full(mog), _full(woa), _full(wob), _full(woc)],
        out_specs=_rows(TM, d), out_shape=_sds((s, d)),
        compiler_params=_cp(("parallel",)),
    )(x, ya, yb, o, mog, woa, wob, woc)


def outproj_bwd(ya, yb, o, mog, woa, wob, woc, dx2, dx1p, riders=()):
    s, d = dx2.shape
    npart = dx1p.shape[0]

    def body(ya_ref, yb_ref, o_ref, mog_ref, woa_ref, wob_ref, woc_ref, dx2_ref, dx1p_ref,
             dx1_ref, dya_ref, dyb_ref, do_ref, dmog_ref, dwoa_ref, dwob_ref, dwoc_ref):
        first = pl.program_id(0) == 0
        sl = lambda h: slice(h * SLOT, (h + 1) * SLOT)
        dx1 = dx2_ref[...]
        for p in range(npart):
            dx1 = dx1 + dx1p_ref[p]
        dx1_ref[...] = dx1
        args = _outproj_args(ya_ref, yb_ref, o_ref, mog_ref, woa_ref, wob_ref, woc_ref, True)
        _, vjp = jax.vjp(lambda *a: _outproj(AD, jnp.zeros_like(dx1), *a), *args)
        dya, dyb, do, dmog, dwoa, dwob, dwoc = vjp(dx1)
        dya_ref[...] = dya
        dyb_ref[...] = dyb
        _acc(dwoa_ref, dwoa, first)
        _acc(dwob_ref, dwob, first)
        for h in range(MLA_HEADS):
            do_ref[:, sl(h)] = do[h]
            _acc(dmog_ref.at[:, sl(h)], dmog[h], first)
            _acc(dwoc_ref.at[sl(h), :], dwoc[h], first)

    return _ride(
        body, riders, name="outproj_bwd", grid=(s // TM,),
        in_specs=[_rows(TM, ya.shape[1]), _rows(TM, yb.shape[1]), _rows(TM, o.shape[1]),
                  _full(mog), _full(woa), _full(wob), _full(woc), _rows(TM, d),
                  pl.BlockSpec((npart, TM, d), lambda i: (0, i, 0))],
        out_specs=[_rows(TM, d), _rows(TM, ya.shape[1]), _rows(TM, yb.shape[1]), _rows(TM, o.shape[1]),
                   _full(mog), _full(woa), _full(wob), _full(woc)],
        out_shape=[_sds((s, d)), _sds(ya.shape), _sds(yb.shape), _sds(o.shape),
                   _sds(mog.shape), _sds(woa.shape), _sds(wob.shape), _sds(woc.shape)],
        operands=(ya, yb, o, mog, woa, wob, woc, dx2, dx1p), sem=("arbitrary",))


def ffn_fwd(x1, g2, w1, w2, riders=()):
    s, d = x1.shape
    npart, _, fs = w1.shape

    def body(x1_ref, g_ref, w1_ref, w2_ref, x2_ref):
        p = pl.program_id(1)
        x1v = x1_ref[...]
        part = _ffn_part(PLAIN, x1v, g_ref[...], w1_ref[...], w2_ref[...])

        @pl.when(p == 0)
        def _():
            x2_ref[...] = x1v + part

        @pl.when(p != 0)
        def _():
            x2_ref[...] = x2_ref[...] + part

    tm = TM_FFN
    return _ride(
        body, riders, name="ffn_fwd", grid=(s // tm, npart),
        in_specs=[pl.BlockSpec((tm, d), lambda i, p: (i, 0)), pl.BlockSpec(g2.shape, lambda i, p: (0, 0)),
                  pl.BlockSpec((None, d, fs), lambda i, p: (p, 0, 0)), pl.BlockSpec((None, fs, d), lambda i, p: (p, 0, 0))],
        out_specs=pl.BlockSpec((tm, d), lambda i, p: (i, 0)), out_shape=_sds((s, d)),
        operands=(x1, g2, w1, w2), sem=("parallel", "arbitrary"))


def ffn_bwd(x1, g2, w1, w2, dx2, riders=()):
    s, d = x1.shape
    npart, _, fs = w1.shape

    def body(x1_ref, g_ref, w1_ref, w2_ref, dx2_ref, dx1p_ref, dg_ref, dw1_ref, dw2_ref):
        p = pl.program_id(0)
        i = pl.program_id(1)
        _, vjp = jax.vjp(functools.partial(_ffn_part, AD), x1_ref[...], g_ref[...], _f32(w1_ref), _f32(w2_ref))
        dx1, dg, dw1, dw2 = vjp(dx2_ref[...])
        dx1p_ref[...] = dx1
        _acc(dg_ref, dg, (p == 0) & (i == 0))
        _acc(dw1_ref, dw1, i == 0)
        _acc(dw2_ref, dw2, i == 0)

    tm = TM_FFN
    return _ride(
        body, riders, name="ffn_bwd", grid=(npart, s // tm),
        in_specs=[pl.BlockSpec((tm, d), lambda p, i: (i, 0)), pl.BlockSpec(g2.shape, lambda p, i: (0, 0)),
                  pl.BlockSpec((None, d, fs), lambda p, i: (p, 0, 0)), pl.BlockSpec((None, fs, d), lambda p, i: (p, 0, 0)),
                  pl.BlockSpec((tm, d), lambda p, i: (i, 0))],
        out_specs=[pl.BlockSpec((None, tm, d), lambda p, i: (p, i, 0)), pl.BlockSpec(g2.shape, lambda p, i: (0, 0)),
                   pl.BlockSpec((None, d, fs), lambda p, i: (p, 0, 0)), pl.BlockSpec((None, fs, d), lambda p, i: (p, 0, 0))],
        out_shape=[_sds((npart, s, d)), _sds(g2.shape), _sds(w1.shape), _sds(w2.shape)],
        operands=(x1, g2, w1, w2, dx2), sem=("arbitrary", "arbitrary"))


def loss_head(y, target):
    s, d = y.shape

    def body(y_ref, t_ref, dy_ref, loss_ref):
        err = y_ref[...] - t_ref[...]
        dy_ref[...] = err * (1.0 / d)
        part = jnp.sum(jnp.sum(err * err, axis=-1, keepdims=True), axis=0, keepdims=True) * (0.5 / d)
        _acc(loss_ref, jnp.broadcast_to(part, loss_ref.shape), pl.program_id(0) == 0)

    return pl.pallas_call(
        body, name="loss_head", grid=(s // TM,),
        in_specs=[_rows(TM, d), _rows(TM, d)],
        out_specs=[_rows(TM, d), pl.BlockSpec((1, SLOT), lambda i: (0, 0))],
        out_shape=[_sds((s, d)), _sds((1, SLOT))],
        compiler_params=_cp(("arbitrary",)),
    )(y, target)


def _row_block(r):
    for b in (512, 256, 128, 64, 32, 16, 8):
        if r % b == 0:
            return b
    return r


def sum_cores(a, got, half, me):
    nch, _, r, c = a.shape
    br = _row_block(r)

    def body(sp_ref, a_ref, g_ref, wire_ref, own_ref):
        tot = a_ref[...] + g_ref[...]
        wire_ref[...] = tot.astype(wire_ref.dtype)

        @pl.when(pl.program_id(1) == sp_ref[1])
        def _():
            own_ref[...] = tot

    grid_spec = pltpu.PrefetchScalarGridSpec(
        num_scalar_prefetch=1, grid=(r // br, nch),
        in_specs=[pl.BlockSpec((None, None, br, c), lambda i, p, sp: (p, sp[0], i, 0)),
                  pl.BlockSpec((None, br, c), lambda i, p, sp: (p, i, 0))],
        out_specs=[pl.BlockSpec((None, br, c), lambda i, p, sp: (p, i, 0)), pl.BlockSpec((br, c), lambda i, p, sp: (i, 0))])
    return pl.pallas_call(body, name="sum_cores", grid_spec=grid_spec, out_shape=[_sds((nch, r, c), BF16), _sds((r, c))],
                          compiler_params=_cp(("parallel", "arbitrary")))(jnp.stack([half, me]).astype(jnp.int32), a, got)


def sum_chips(own, recv, half):
    r, c = own.shape
    br = _row_block(r)

    def body(sp_ref, own_ref, r0_ref, r1_ref, r2_ref, out_ref):
        del sp_ref
        out_ref[...] = ((own_ref[...] + r0_ref[...].astype(F32)) + r1_ref[...].astype(F32)) + r2_ref[...].astype(F32)

    grid_spec = pltpu.PrefetchScalarGridSpec(
        num_scalar_prefetch=1, grid=(r // br,),
        in_specs=[pl.BlockSpec((br, c), lambda i, sp: (i, 0))]
        + [pl.BlockSpec((None, br, c), functools.partial(lambda i, sp, j: (j, i, 0), j=j)) for j in range(3)],
        out_specs=pl.BlockSpec((None, br, c), lambda i, sp: (sp[0], i, 0)))
    return pl.pallas_call(body, name="sum_chips", grid_spec=grid_spec, out_shape=_sds((2, r, c)),
                          compiler_params=_cp(("parallel",)))(half.reshape(1).astype(jnp.int32), own, recv, recv, recv)


def adamw(w, g, m, v, name):
    r, c = w.shape
    br = _row_block(r)
    c1 = 1.0 / (1.0 - ADAM_B1 ** ADAM_STEP)
    c2 = 1.0 / (1.0 - ADAM_B2 ** ADAM_STEP)

    def body(w_ref, g_ref, m_ref, v_ref, d_ref, nm_ref, nv_ref):
        gg = g_ref[...]
        nm = ADAM_B1 * m_ref[...] + (1.0 - ADAM_B1) * gg
        nv = ADAM_B2 * v_ref[...] + (1.0 - ADAM_B2) * (gg * gg)
        d_ref[...] = -ADAM_LR * ((nm * c1) / (jnp.sqrt(nv * c2) + ADAM_EPS) + ADAM_WD * w_ref[...])
        nm_ref[...] = nm
        nv_ref[...] = nv

    return pl.pallas_call(
        body, name=name, grid=(r // br,), in_specs=[_rows(br, c)] * 4, out_specs=[_rows(br, c)] * 3,
        out_shape=[_sds((r, c))] * 3, compiler_params=_cp(("parallel",)),
    )(w, g, m, v)


def _place():
    x, y, c = lax.axis_index("x"), lax.axis_index("y"), lax.axis_index("c")
    chips = [(1 - x, y), (x, 1 - y), (1 - x, 1 - y)]
    return x, y, c, chips


def _remote(src, dst, send_sem, recv_sem, to):
    return pltpu.make_async_remote_copy(src_ref=src, dst_ref=dst, send_sem=send_sem, recv_sem=recv_sem,
                                        device_id=to, device_id_type=MESH)


def gather_rider(arrs):
    n = len(arrs)
    me_chip = 2 * lax.axis_index("x") + lax.axis_index("y")
    bufs = [lax.dynamic_update_index_in_dim(lax.empty((N_CHIPS,) + a.shape, a.dtype), a, me_chip, 0) for a in arrs]

    def plan(ins, outs, sems):
        send_sems, recv_sems = sems
        x, y, c, chips = _place()
        me = 2 * x + y
        half, other, sibling = pl.ds(2 * c, 2), pl.ds(2 - 2 * c, 2), (x, y, 1 - c)
        cp = lambda i, k, src, dst, to: _remote(src, dst, send_sems.at[i, k], recv_sems.at[i, k], to)
        pairs = [(i, j, cx, cy) for i in range(n) for j, (cx, cy) in enumerate(chips)]
        blk = lambda i, cx, cy, part: outs[i].at[2 * cx + cy, part]
        first = lambda: [cp(i, j, ins[i].at[half], outs[i].at[me, half], (cx, cy, c)) for i, j, cx, cy in pairs]
        landed = lambda: [cp(i, j, blk(i, cx, cy, half), blk(i, cx, cy, half), (cx, cy, c)) for i, j, cx, cy in pairs]
        passed = lambda: [cp(i, 3 + j, blk(i, cx, cy, half), blk(i, cx, cy, half), sibling) for i, j, cx, cy in pairs]
        from_sibling = lambda: [cp(i, 3 + j, blk(i, cx, cy, other), blk(i, cx, cy, other), sibling) for i, j, cx, cy in pairs]
        return first, landed, passed, from_sibling

    def start(ins, outs, sems):
        for cp in plan(ins, outs, sems)[0]():
            cp.start()

    def finish(ins, outs, sems):
        first, landed, passed, from_sibling = plan(ins, outs, sems)
        forwards = passed()
        for a, b in zip(landed(), forwards):
            a.wait_recv()
            b.start()
        for cp in from_sibling():
            cp.wait_recv()
        for cp in first() + forwards:
            cp.wait_send()

    return Rider(list(arrs) + bufs, [_sds((N_CHIPS,) + a.shape, a.dtype) for a in arrs], {n + i: i for i in range(n)},
                 [pltpu.SemaphoreType.DMA((n, 6)), pltpu.SemaphoreType.DMA((n, 6))], start, finish)


class Reducer:
    def __init__(self, arrs):
        self.a = list(arrs)
        self.n = len(self.a)
        self.c = lax.axis_index("c")
        self.me = 2 * lax.axis_index("x") + lax.axis_index("y")

    def swap_rider(self):
        n = self.n

        def plan(ins, outs, sems):
            x, y, c, _ = _place()
            return [_remote(ins[i].at[p, 1 - c], outs[i].at[p], sems[0].at[i, p], sems[1].at[i, p], (x, y, 1 - c))
                    for i in range(n) for p in range(N_CHIPS)]

        return Rider(self.a, [_sds((N_CHIPS,) + a.shape[2:]) for a in self.a], {},
                     [pltpu.SemaphoreType.DMA((n, N_CHIPS)), pltpu.SemaphoreType.DMA((n, N_CHIPS))],
                     lambda *r: [cp.start() for cp in plan(*r)], lambda *r: [cp.wait() for cp in plan(*r)])

    def after_swap(self, got):
        pairs = [sum_cores(a, g, self.c, self.me) for a, g in zip(self.a, got)]
        self.wire, self.own = [p[0] for p in pairs], [p[1] for p in pairs]

    def scatter_rider(self):
        n = self.n

        def plan(ins, outs, sems):
            x, y, c, chips = _place()
            return [_remote(ins[i].at[2 * cx + cy], outs[i].at[j], sems[0].at[i, j], sems[1].at[i, j], (cx, cy, c))
                    for i in range(n) for j, (cx, cy) in enumerate(chips)]

        return Rider(self.wire, [_sds((3,) + w.shape[1:], w.dtype) for w in self.wire], {},
                     [pltpu.SemaphoreType.DMA((n, 3)), pltpu.SemaphoreType.DMA((n, 3))],
                     lambda *r: [cp.start() for cp in plan(*r)], lambda *r: [cp.wait() for cp in plan(*r)])

    def after_scatter(self, recv):
        self.full = [sum_chips(o, r, self.c) for o, r in zip(self.own, recv)]

    def share_rider(self):
        n = self.n

        def plan(ins, outs, sems):
            x, y, c, _ = _place()
            return [_remote(ins[i].at[c], outs[i].at[c], sems[0].at[i], sems[1].at[i], (x, y, 1 - c)) for i in range(n)]

        return Rider(self.full, [_sds(f.shape) for f in self.full], {i: i for i in range(n)},
                     [pltpu.SemaphoreType.DMA((n,)), pltpu.SemaphoreType.DMA((n,))],
                     lambda *r: [cp.start() for cp in plan(*r)], lambda *r: [cp.wait() for cp in plan(*r)])

    def run(self):
        self.after_swap(run_rider(self.swap_rider(), "swap_halves"))
        self.after_scatter(run_rider(self.scatter_rider(), "scatter_chips"))
        return run_rider(self.share_rider(), "share_halves")


def _pad_slots(a, live):
    lead = a.shape[:-1]
    a = a.reshape(lead + (MLA_HEADS, live))
    a = jnp.pad(a, [(0, 0)] * len(lead) + [(0, 0), (0, SLOT - live)])
    return a.reshape(lead + (MLA_HEADS * SLOT,))


def _unpad_slots(a, live):
    lead = a.shape[:-1]
    return a.reshape(lead + (MLA_HEADS, SLOT))[..., :live].reshape(lead + (MLA_HEADS * live,))


def _rope_tables(positions, s):
    half = QK_ROPE // 2
    inv_freq = ROPE_THETA ** (-jnp.arange(half, dtype=F32) / half)
    ang = positions.reshape(s).astype(F32)[:, None] * inv_freq[None, :]
    cos, sin = jnp.cos(ang), jnp.sin(ang)
    one = jnp.ones((s, QK_NOPE), F32)
    z64, z16, z32 = jnp.zeros((s, QK_NOPE), F32), jnp.zeros((s, half), F32), jnp.zeros((s, SLOT - QK_DIM), F32)
    cos_t = jnp.concatenate([one, cos, cos, z32], axis=1)
    sin_a = jnp.concatenate([z64, -sin, z16, z32], axis=1)
    sin_b = jnp.concatenate([z64, z16, sin, z32], axis=1)
    return cos_t, sin_a, sin_b


def _layer_weights(full, small, l):
    w_in = jnp.concatenate([full["w_in"][p] for p in range(N_CHIPS)], axis=1)
    wc = jnp.pad(w_in[:, 1536:], ((0, 0), (0, 512 - (w_in.shape[1] - 1536))))
    w_uq = jnp.concatenate([full["mla_w_uq"][p] for p in range(N_CHIPS)], axis=1)
    w_ukv = jnp.concatenate([full["mla_w_ukv"][p] for p in range(N_CHIPS)], axis=1)
    ukv = w_ukv.reshape(KV_LORA, MLA_HEADS, QK_NOPE + V_DIM)
    w_out = jnp.concatenate([full["w_out"][p] for p in range(N_CHIPS)], axis=0)
    woc = w_out[512:].reshape(MLA_HEADS, V_DIM, D_MODEL)
    woc = jnp.pad(woc, ((0, 0), (0, SLOT - V_DIM), (0, 0))).reshape(MLA_HEADS * SLOT, D_MODEL)
    row = lambda a: a.reshape(1, -1)
    return dict(
        g1=row(small["norm1_gain"][l]), wa=w_in[:, :512], wb=w_in[:, 512:1536], wc=wc,
        vg=row(small["gm_v_gain"][l]), ws=small["gm_w_s"][l], bs=small["gm_b_s"][l].reshape(4, CHUNK, 1),
        gog=row(small["gm_out_gain"][l]), hog=small["hg_out_gain"][l].reshape(-1, 1),
        qag=row(small["mla_q_a_gain"][l]), kvag=row(small["mla_kv_a_gain"][l]),
        qg=row(jnp.pad(small["mla_q_gain"][l], (0, SLOT - QK_DIM))), kg=row(jnp.pad(small["mla_k_gain"][l], (0, SLOT - QK_DIM))),
        wq=_pad_slots(w_uq, QK_DIM), wk=_pad_slots(ukv[..., :QK_NOPE].reshape(KV_LORA, -1), QK_NOPE),
        wv=_pad_slots(ukv[..., QK_NOPE:].reshape(KV_LORA, -1), V_DIM),
        mog=row(_pad_slots(small["mla_out_gain"][l], V_DIM)),
        woa=w_out[:256], wob=w_out[256:512], woc=woc,
        g2=row(small["norm2_gain"][l]),
    )


def _shard_cols(a):
    r, c4 = a.shape
    return a.reshape(r, N_CHIPS, c4 // N_CHIPS).transpose(1, 0, 2)


def local_step(x, positions, target, small, comm):
    s = x.shape[0]
    cos_t, sin_a, sin_b = _rope_tables(positions, s)
    lbs = lower_bounds_fwd(small["hg_lower_bound"])
    lw, saved = [], []
    for l in range(DEPTH):
        w = _layer_weights(comm.part(l, "mix"), small, l)
        lw.append(w)
        lb = lbs[l].reshape(-1, 1)
        pa, pb, pc = inproj_fwd(x, w["g1"], w["wa"], w["wb"], w["wc"])
        ya = gm_fwd(pa, w["vg"], w["ws"], w["bs"], w["gog"])
        yb, states = hg_fwd(pb, lb, w["hog"])
        q, k, v = mla_pre_fwd(pc, cos_t, sin_a, sin_b, w["qag"], w["kvag"], w["qg"], w["kg"], w["wq"], w["wk"], w["wv"])
        rider = comm.gather_rider(l, "ffn")
        (o, lse), got = attn_fwd(q, k, v, [rider])
        comm.gathered(l, "ffn", got[0])
        x1 = outproj_fwd(x, ya, yb, o, w["mog"], w["woa"], w["wob"], w["woc"])
        ffn_w = comm.part(l, "ffn")
        w["w1"], w["w2"] = ffn_w["w_ff1"], ffn_w["w_ff2"]
        rider = comm.gather_rider(l + 1, "mix") if l + 1 < DEPTH else None
        x2, got = ffn_fwd(x1, w["g2"], w["w1"], w["w2"], [rider])
        comm.gathered(l + 1, "mix", got[0])
        saved.append(dict(x=x, pa=pa, pb=pb, pc=pc, ya=ya, yb=yb, states=states, q=q, k=k, v=v, o=o, lse=lse, x1=x1, lb=lb))
        x = x2
    dx, loss_part = loss_head(x, target)
    groups = [dict() for _ in range(DEPTH)]
    sm = {n: [None] * DEPTH for n in ("norm1_gain", "gm_v_gain", "gm_w_s", "gm_b_s", "gm_out_gain", "hg_out_gain",
                                       "mla_q_a_gain", "mla_kv_a_gain", "mla_q_gain", "mla_k_gain", "mla_out_gain",
                                       "norm2_gain")}
    dlbs = [None] * DEPTH
    halves = lambda g: g.reshape(N_CHIPS, 2, g.shape[1] // 2, g.shape[2])
    take = lambda red, f: None if red is None else f(red)
    red_mix = None
    for l in reversed(range(DEPTH)):
        w, a = lw[l], saved[l]
        (dx1p, dg2, dw1, dw2), got = ffn_bwd(a["x1"], w["g2"], w["w1"], w["w2"], dx, [take(red_mix, Reducer.swap_rider)])
        if red_mix:
            red_mix.after_swap(got[0])
        ffn_arrs = [halves(dw1), halves(dw2)]
        red_ffn = comm.reducer(ffn_arrs)
        (dx1, dya, dyb, do, dmog, dwoa, dwob, dwoc), got = outproj_bwd(
            a["ya"], a["yb"], a["o"], w["mog"], w["woa"], w["wob"], w["woc"], dx, dx1p, [take(red_ffn, Reducer.swap_rider)])
        if red_ffn:
            red_ffn.after_swap(got[0])
        (dq, dk, dv), got = attn_bwd(a["q"], a["k"], a["v"], a["o"], do, a["lse"],
                                     [take(red_mix, Reducer.scatter_rider), take(red_ffn, Reducer.scatter_rider)])
        for red, g in zip((red_mix, red_ffn), got):
            if red:
                red.after_scatter(g)
        dpc, dqag, dkvag, dqg, dkg, dwq, dwk, dwv = mla_pre_bwd(a["pc"], cos_t, sin_a, sin_b, w["qag"], w["kvag"], w["qg"],
                                                                  w["kg"], w["wq"], w["wk"], w["wv"], dq, dk, dv)
        (dpb, dlb, dhog), got = hg_bwd(a["pb"], a["lb"], w["hog"], a["states"], dyb,
                                       [take(red_mix, Reducer.share_rider), take(red_ffn, Reducer.share_rider)])
        if red_mix:
            groups[l + 1].update(zip(MIX, got[0]))
        groups[l].update(zip(FFN, got[1] if red_ffn else ffn_arrs))
        dpa, dvg, dws, dbs, dgog = gm_bwd(a["pa"], w["vg"], w["ws"], w["bs"], w["gog"], dya)
        dx, dg1, dwa, dwb, dwc = inproj_bwd(a["x"], w["g1"], w["wa"], w["wb"], w["wc"], dpa, dpb, dpc, dx1)
        dukv = jnp.concatenate([dwk.reshape(KV_LORA, MLA_HEADS, SLOT)[..., :QK_NOPE],
                                dwv.reshape(KV_LORA, MLA_HEADS, SLOT)[..., :V_DIM]], axis=-1)
        dwo = jnp.concatenate([dwoa, dwob, dwoc.reshape(MLA_HEADS, SLOT, D_MODEL)[:, :V_DIM].reshape(-1, D_MODEL)], axis=0)
        mix_arrs = [halves(_shard_cols(jnp.concatenate([dwa, dwb, dwc[:, :1952 - 1536]], axis=1))),
                    halves(_shard_cols(_unpad_slots(dwq, QK_DIM))), halves(_shard_cols(dukv.reshape(KV_LORA, -1))),
                    halves(dwo.reshape(N_CHIPS, -1, D_MODEL))]
        red_mix = comm.reducer(mix_arrs) if l > 0 else None
        if red_mix is None:
            groups[l].update(zip(MIX, mix_arrs))
        sm["norm1_gain"][l] = dg1[0]
        sm["gm_v_gain"][l] = dvg[0]
        sm["gm_w_s"][l] = dws
        sm["gm_b_s"][l] = dbs[..., 0]
        sm["gm_out_gain"][l] = dgog[0]
        sm["hg_out_gain"][l] = dhog[:, 0]
        sm["mla_q_a_gain"][l] = dqag[0]
        sm["mla_kv_a_gain"][l] = dkvag[0]
        sm["mla_q_gain"][l] = dqg[0, :QK_DIM]
        sm["mla_k_gain"][l] = dkg[0, :QK_DIM]
        sm["mla_out_gain"][l] = _unpad_slots(dmog[0], V_DIM)
        sm["norm2_gain"][l] = dg2[0]
        dlbs[l] = dlb[:, 0]
    sm = {n: jnp.stack(v) for n, v in sm.items()}
    sm["hg_lower_bound"] = lower_bounds_bwd(small["hg_lower_bound"], jnp.stack(dlbs))
    return loss_part, dx, groups, sm


MIX = ("w_in", "mla_w_uq", "mla_w_ukv", "w_out")
FFN = ("w_ff1", "w_ff2")
BIG = MIX + FFN
PARTS = {"mix": MIX, "ffn": FFN}
SMALL = ("norm1_gain", "gm_v_gain", "gm_w_s", "gm_b_s", "gm_out_gain", "hg_lower_bound", "hg_out_gain",
         "mla_q_a_gain", "mla_kv_a_gain", "mla_q_gain", "mla_k_gain", "mla_out_gain", "norm2_gain")
ORDER = ("norm1_gain", "w_in", "gm_v_gain", "gm_w_s", "gm_b_s", "gm_out_gain", "hg_lower_bound", "hg_out_gain",
         "mla_q_a_gain", "mla_w_uq", "mla_kv_a_gain", "mla_w_ukv", "mla_q_gain", "mla_k_gain", "mla_out_gain",
         "w_out", "norm2_gain", "w_ff1", "w_ff2")
PACK_ROWS = 288


def _pack(arrs):
    flat = jnp.concatenate([a.reshape(-1) for a in arrs])
    total = 2 * N_CHIPS * PACK_ROWS * SLOT
    return jnp.pad(flat, (0, total - flat.shape[0]))


def _unpack(flat, shapes):
    out, off = [], 0
    for sh in shapes:
        size = 1
        for d in sh:
            size *= d
        out.append(flat[off:off + size].reshape(sh))
        off += size
    return out


class ChipComm:
    def __init__(self, shards):
        self.shards = shards
        self.full = {}

    def gather_rider(self, l, part):
        return gather_rider([self.shards[n][l].astype(MXU_DTYPE).reshape(4, self.shards[n].shape[1] // 4, -1)
                             for n in PARTS[part]])

    def gathered(self, l, part, outs):
        if outs is not None:
            self.full[l, part] = {n: o.reshape((N_CHIPS,) + self.shards[n].shape[1:]) for n, o in zip(PARTS[part], outs)}

    def part(self, l, part):
        if (l, part) not in self.full:
            self.gathered(l, part, run_rider(self.gather_rider(l, part), "gather_weights"))
        return self.full[l, part]

    def reducer(self, arrs):
        return Reducer(arrs)


def kernel(x, positions, norm1_gain, w_in, gm_v_gain, gm_w_s, gm_b_s, gm_out_gain, hg_lower_bound, hg_out_gain, mla_q_a_gain, mla_w_uq, mla_kv_a_gain, mla_w_ukv, mla_q_gain, mla_k_gain, mla_out_gain, w_out, norm2_gain, w_ff1, w_ff2, loss_target, m_norm1_gain, m_w_in, m_gm_v_gain, m_gm_w_s, m_gm_b_s, m_gm_out_gain, m_hg_lower_bound, m_hg_out_gain, m_mla_q_a_gain, m_mla_w_uq, m_mla_kv_a_gain, m_mla_w_ukv, m_mla_q_gain, m_mla_k_gain, m_mla_out_gain, m_w_out, m_norm2_gain, m_w_ff1, m_w_ff2, v_norm1_gain, v_w_in, v_gm_v_gain, v_gm_w_s, v_gm_b_s, v_gm_out_gain, v_hg_lower_bound, v_hg_out_gain, v_mla_q_a_gain, v_mla_w_uq, v_mla_kv_a_gain, v_mla_w_ukv, v_mla_q_gain, v_mla_k_gain, v_mla_out_gain, v_w_out, v_norm2_gain, v_w_ff1, v_w_ff2):
    given = dict(locals())
    weights = {n: given[n] for n in ORDER}
    moms = {n: given["m_" + n] for n in ORDER}
    vars_ = {n: given["v_" + n] for n in ORDER}
    s, d = x.shape[1], x.shape[2]

    small = {n: weights[n] for n in SMALL}
    comm = ChipComm({n: weights[n] for n in BIG})
    loss_part, dx, groups, small_g = local_step(x.reshape(s, d), positions, loss_target.reshape(s, d), small, comm)
    loss = lax.psum(loss_part[0, 0], ("x", "y", "c"))

    pack_g = _pack([small_g[n] for n in SMALL]).reshape(N_CHIPS, 2, PACK_ROWS, SLOT)
    last = Reducer([groups[0][n] for n in MIX] + [pack_g]).run()
    groups[0].update(zip(MIX, last[:-1]))
    pack_full = run_rider(gather_rider([last[-1].reshape(4, PACK_ROWS // 2, SLOT)]), "gather_small")[0].reshape(-1)
    grads = {n: jnp.stack([groups[l][n].reshape(weights[n].shape[1:]) for l in range(DEPTH)]) for n in BIG}
    grads.update(zip(SMALL, _unpack(pack_full, [weights[n].shape for n in SMALL])))

    delta, new_m, new_v = {}, {}, {}
    flat2 = lambda a: a.reshape(-1, a.shape[-1])
    for n in BIG:
        outs = adamw(flat2(weights[n]), flat2(grads[n]), flat2(moms[n]), flat2(vars_[n]), "adamw_" + n)
        delta[n], new_m[n], new_v[n] = [o.reshape(weights[n].shape) for o in outs]
    shapes = [weights[n].shape for n in SMALL]
    pk = lambda t: _pack([t[n] for n in SMALL]).reshape(-1, SLOT)
    outs = adamw(pk(weights), pack_full.reshape(-1, SLOT), pk(moms), pk(vars_), "adamw_small")
    for tgt, o in zip((delta, new_m, new_v), outs):
        tgt.update(zip(SMALL, _unpack(o.reshape(-1), shapes)))

    return (loss, dx.reshape(x.shape), *[grads[n] for n in ORDER], *[delta[n] for n in ORDER],
            *[new_m[n] for n in ORDER], *[new_v[n] for n in ORDER])
```

```python
import functools

import jax
import jax.numpy as jnp
from jax import lax
from jax.experimental import pallas as pl
from jax.experimental.pallas import tpu as pltpu

F32 = jnp.float32
BF16 = jnp.bfloat16
MXU_DTYPE = BF16

D_MODEL = 1024
DEPTH = 4
CHUNK = 128
HG_CHUNK = 128
HG_CHUNKS = 2
GM_CHUNKS = 4
EPS = 1e-6
HEAD64 = 64
MLA_HEADS = 8
QK_NOPE = 64
QK_ROPE = 32
QK_DIM = 96
V_DIM = 64
Q_LORA = 256
KV_LORA = 128
SLOT = 128
ROPE_THETA = 10000.0
D_FF_SHARD = 1024
N_CHIPS = 4

ADAM_LR = 0.001
ADAM_B1 = 0.9
ADAM_B2 = 0.999
ADAM_EPS = 1e-08
ADAM_WD = 0.01
ADAM_STEP = 10

TM = 512
TM_FFN = 512
TQ = 256
ATT_HEADS_PER_STEP = 2
ATT_WIDE = 512
VMEM_LIMIT = 56 * 1024 * 1024

NN = (((1,), (0,)), ((), ()))
NT = (((1,), (1,)), ((), ()))
TN = (((0,), (0,)), ((), ()))
BNN = (((2,), (1,)), ((0,), (0,)))
BNT = (((2,), (2,)), ((0,), (0,)))
BTN = (((1,), (1,)), ((0,), (0,)))


def _dot(a, b, dims):
    return lax.dot_general(a.astype(MXU_DTYPE), b.astype(MXU_DTYPE), dims, preferred_element_type=F32)


def _hdot(a, b, dims=NN):
    return lax.dot_general(a, b, dims, precision=lax.Precision.HIGHEST, preferred_element_type=F32)


def _make_ad(dims, da_dims, da_swap, db_dims, db_swap):
    @jax.custom_vjp
    def f(a, b):
        return _dot(a, b, dims)

    def fwd(a, b):
        return _dot(a, b, dims), (a, b)

    def bwd(res, g):
        a, b = res
        da = _dot(b, g, da_dims) if da_swap else _dot(g, b, da_dims)
        db = _dot(g, a, db_dims) if db_swap else _dot(a, g, db_dims)
        return da, db

    f.defvjp(fwd, bwd)
    return f


@functools.partial(jax.custom_vjp, nondiff_argnums=(1,))
def _roll_ad(x, shift):
    return pltpu.roll(x, shift, 1)


def _roll_ad_fwd(x, shift):
    return pltpu.roll(x, shift, 1), None


def _roll_ad_bwd(shift, _, g):
    return (pltpu.roll(g, (g.shape[1] - shift) % g.shape[1], 1),)


_roll_ad.defvjp(_roll_ad_fwd, _roll_ad_bwd)


class _Ops:
    pass


PLAIN = _Ops()
PLAIN.mm = lambda a, b: _dot(a, b, NN)
PLAIN.bmm = lambda a, b: _dot(a, b, BNN)
PLAIN.bmm_nt = lambda a, b: _dot(a, b, BNT)
PLAIN.bmm_tn = lambda a, b: _dot(a, b, BTN)
PLAIN.roll = lambda x, s: pltpu.roll(x, s, 1)

AD = _Ops()
AD.mm = _make_ad(NN, NT, False, TN, False)
AD.bmm = _make_ad(BNN, BNT, False, BTN, False)
AD.bmm_nt = _make_ad(BNT, BNN, False, BTN, True)
AD.bmm_tn = _make_ad(BTN, BNT, True, BNN, False)
AD.roll = _roll_ad


def _sigmoid(x):
    return jax.nn.sigmoid(x)


def _gelu(x):
    return 0.5 * x * (1.0 + jnp.tanh(0.7978845608028654 * (x + 0.044715 * (x * x * x))))


def _rms(x, g):
    return x * lax.rsqrt(jnp.mean(x * x, axis=-1, keepdims=True) + EPS) * g


def _head_masks256():
    lane = lax.broadcasted_iota(jnp.int32, (1, 4 * HEAD64), 1)
    return [(jnp.right_shift(lane, 6) == h).astype(F32) for h in range(4)]


def _headnorm256(x, g):
    ms = jnp.zeros_like(x)
    sq = x * x
    for m in _head_masks256():
        ms = ms + m * (jnp.sum(sq * m, axis=-1, keepdims=True) * (1.0 / HEAD64))
    return x * lax.rsqrt(ms + EPS) * g


def _slot_norm(x, g, n):
    return x * lax.rsqrt(jnp.sum(x * x, axis=-1, keepdims=True) * (1.0 / n) + EPS) * g


def _rope(ops, x, cos_t, sin_a, sin_b):
    return x * cos_t + ops.roll(x, SLOT - QK_ROPE // 2) * sin_a + ops.roll(x, QK_ROPE // 2) * sin_b


def _inproj(ops, x, g1, wa, wb, wc):
    h = _rms(x, g1)
    return ops.mm(h, wa), ops.mm(h, wb), ops.mm(h, wc)


def _gm_chunk(ops, ur, vr, vg, ws4, bs, og):
    c = ur.shape[0]
    masks = _head_masks256()
    mh = jnp.concatenate([m[None] for m in masks], axis=0)
    u = _gelu(ur)
    v = _headnorm256(_gelu(vr), vg)
    t = lax.broadcasted_iota(jnp.int32, (c, c), 0)
    s = lax.broadcasted_iota(jnp.int32, (c, c), 1)
    w = jnp.where((t >= s)[None], ws4, 0.0)
    y = jnp.sum(ops.bmm(w, v[None] * mh), axis=0)
    for h in range(4):
        y = y + bs[h] * masks[h]
    return _headnorm256(u * y, og)


def _hg_chunk(ops, st, qr, fr, ir, gr, lb, og):
    c, n = qr.shape
    nh = n // HEAD64
    heads = lambda x: x.reshape(nh, HEAD64, x.shape[-1])
    tr = lambda x: heads(x.T)
    lb4, og4 = heads(lb), heads(og)
    qx = tr(qr)
    q = qx * _sigmoid(qx)
    f = lb4 + (1.0 - lb4) * _sigmoid(tr(fr))
    k = 1.0 - f
    logf = jnp.log(f)
    v = tr(ir)
    gx = tr(gr)
    s = lax.broadcasted_iota(jnp.int32, (c, c), 0)
    t = lax.broadcasted_iota(jnp.int32, (c, c), 1)
    tl = lax.broadcasted_iota(jnp.int32, (1, c), 1).reshape(1, 1, c)
    b2 = _hdot(logf.reshape(n, c), (s <= t).astype(F32))
    b = heads(b2)
    btot = jnp.sum(logf, axis=2, keepdims=True)
    inter = ops.bmm_tn(st, q * jnp.exp(b))
    p4 = jnp.zeros((nh, c, c), F32)
    tt, ss = s, t
    lg = c.bit_length() - 2
    while lg >= 0:
        m = 1 << lg
        bnd = jnp.left_shift(jnp.right_shift(t, lg + 1), lg + 1) + (m - 1)
        r = heads(_hdot(b2, (s == bnd).astype(F32)))
        right = jnp.bitwise_and(jnp.right_shift(tl, lg), 1) == 1
        qe = jnp.where(right, q * jnp.exp(jnp.where(right, b - r, 0.0)), 0.0)
        ke = jnp.where(right, 0.0, k * jnp.exp(jnp.where(right, 0.0, r - b)))
        lm = ((jnp.right_shift(tt, lg + 1) == jnp.right_shift(ss, lg + 1))
              & (jnp.bitwise_and(jnp.right_shift(tt, lg), 1) == 1)
              & (jnp.bitwise_and(jnp.right_shift(ss, lg), 1) == 0))
        p4 = jnp.where(lm[None], ops.bmm_tn(qe, ke), p4)
        lg -= 1
    intra = ops.bmm_nt(v, p4)
    o = inter + intra + jnp.sum(q * k, axis=1, keepdims=True) * v
    st_new = st * jnp.exp(btot) + ops.bmm_nt(k * jnp.exp(btot - b), v)
    y = o * lax.rsqrt(jnp.mean(o * o, axis=1, keepdims=True) + EPS) * og4 * (gx * _sigmoid(gx))
    return st_new, y.reshape(n, c).T


def _mla_pre(ops, cq, ckv, kpe, cos_t, sin_a, sin_b, qag, kvag, qg, kg, wq, wk, wv):
    cqn = _rms(cq, qag)
    ckvn = _rms(ckv, kvag)
    kper = ops.roll(kpe, QK_NOPE)
    qs, ks, vs = [], [], []
    for h in range(MLA_HEADS):
        qh = _slot_norm(ops.mm(cqn, wq[h]), qg, QK_DIM)
        qs.append(_rope(ops, qh, cos_t, sin_a, sin_b))
        kh = _slot_norm(ops.mm(ckvn, wk[h]) + kper, kg, QK_DIM)
        ks.append(_rope(ops, kh, cos_t, sin_a, sin_b))
        vs.append(ops.mm(ckvn, wv[h]))
    return qs, ks, vs


def _outproj(ops, x, ya, yb, o, mog, woa, wob, woc):
    acc = x + ops.mm(ya, woa) + ops.mm(yb, wob)
    for h in range(MLA_HEADS):
        acc = acc + ops.mm(_slot_norm(o[h], mog[h], V_DIM), woc[h])
    return acc


def _ffn_part(ops, x1, g2, w1p, w2p):
    a = ops.mm(_rms(x1, g2), w1p)
    r = jnp.maximum(a, 0.0)
    return ops.mm(r * r, w2p)


def _lower_bounds(r0, r1, r2, r3):
    mx = jnp.maximum(jnp.maximum(r0, r1), jnp.maximum(r2, r3))
    e0, e1, e2, e3 = jnp.exp(r0 - mx), jnp.exp(r1 - mx), jnp.exp(r2 - mx), jnp.exp(r3 - mx)
    inv = 1.0 / (e0 + e1 + e2 + e3)
    s1, s2, s3 = e1 * inv, e2 * inv, e3 * inv
    return jnp.zeros_like(r0), s1, s1 + s2, s1 + s2 + s3


def _cp(sem):
    return pltpu.CompilerParams(dimension_semantics=sem, vmem_limit_bytes=VMEM_LIMIT)


def _rows(tm, n):
    return pl.BlockSpec((tm, n), lambda i: (i, 0))


def _full(a):
    nd = len(a.shape)
    return pl.BlockSpec(a.shape, lambda *_: (0,) * nd, pipeline_mode=pl.Buffered(1))


def _sds(shape, dtype=F32):
    return jax.ShapeDtypeStruct(shape, dtype)


def _acc(ref, val, first):
    @pl.when(first)
    def _():
        ref[...] = val

    @pl.when(jnp.logical_not(first))
    def _():
        ref[...] = ref[...] + val


def _f32(ref):
    return ref[...].astype(F32)


MESH = pl.DeviceIdType.MESH
ANY = pl.BlockSpec(memory_space=pl.ANY)


class Rider:
    def __init__(self, arrays, out_shapes, aliases, sems, start, finish):
        self.arrays, self.out_shapes, self.aliases, self.sems = list(arrays), list(out_shapes), dict(aliases), list(sems)
        self.start, self.finish = start, finish


def run_rider(rider, name):
    n_in, n_out = len(rider.arrays), len(rider.out_shapes)

    def body(*refs):
        ins, outs, sems = refs[:n_in], refs[n_in:n_in + n_out], refs[n_in + n_out:]
        rider.start(ins, outs, sems)
        rider.finish(ins, outs, sems)

    return pl.pallas_call(
        body, name=name, in_specs=[ANY] * n_in, out_specs=[ANY] * n_out, out_shape=rider.out_shapes,
        input_output_aliases=rider.aliases, scratch_shapes=rider.sems,
    )(*rider.arrays)


def _merge_riders(riders):
    bounds, a0, o0, s0 = [], 0, 0, 0
    for r in riders:
        bounds.append((a0, o0, s0))
        a0, o0, s0 = a0 + len(r.arrays), o0 + len(r.out_shapes), s0 + len(r.sems)

    def part(k, ins, outs, sems):
        a, o, s = bounds[k]
        r = riders[k]
        return ins[a:a + len(r.arrays)], outs[o:o + len(r.out_shapes)], sems[s:s + len(r.sems)]

    return Rider(
        [x for r in riders for x in r.arrays], [x for r in riders for x in r.out_shapes],
        {bounds[k][0] + i: bounds[k][1] + o for k, r in enumerate(riders) for i, o in r.aliases.items()},
        [x for r in riders for x in r.sems],
        lambda *refs: [r.start(*part(k, *refs)) for k, r in enumerate(riders)],
        lambda *refs: [r.finish(*part(k, *refs)) for k, r in enumerate(riders)])


def _ride(compute, riders, *, name, grid, in_specs, out_specs, out_shape, operands, scratch_shapes=(), sem=None):
    single = not isinstance(out_shape, (list, tuple))
    if single:
        out_specs, out_shape = [out_specs], [out_shape]
    live = [r for r in riders if r is not None]
    if not live:
        res = pl.pallas_call(compute, name=name, grid=grid, in_specs=in_specs, out_specs=out_specs, out_shape=out_shape,
                             scratch_shapes=list(scratch_shapes), compiler_params=_cp(sem))(*operands)
        return (res[0] if single else res), [None] * len(riders)
    rider = live[0] if len(live) == 1 else _merge_riders(live)
    n_in, n_out, n_s = len(in_specs), len(out_specs), len(scratch_shapes)
    r_in, r_out = len(rider.arrays), len(rider.out_shapes)

    def body(*refs):
        ins, rins = refs[:n_in], refs[n_in:n_in + r_in]
        outs = refs[n_in + r_in:n_in + r_in + n_out]
        routs = refs[n_in + r_in + n_out:n_in + r_in + n_out + r_out]
        scr = refs[n_in + r_in + n_out + r_out:n_in + r_in + n_out + r_out + n_s]
        rsems = refs[n_in + r_in + n_out + r_out + n_s:]
        first = functools.reduce(jnp.logical_and, [pl.program_id(a) == 0 for a in range(len(grid))])
        last = functools.reduce(jnp.logical_and, [pl.program_id(a) == grid[a] - 1 for a in range(len(grid))])

        @pl.when(first)
        def _():
            rider.start(rins, routs, rsems)

        compute(*ins, *outs, *scr)

        @pl.when(last)
        def _():
            rider.finish(rins, routs, rsems)

    res = pl.pallas_call(
        body, name=name, grid=grid, in_specs=list(in_specs) + [ANY] * r_in, out_specs=list(out_specs) + [ANY] * r_out,
        out_shape=list(out_shape) + rider.out_shapes,
        input_output_aliases={n_in + k: n_out + v for k, v in rider.aliases.items()},
        scratch_shapes=list(scratch_shapes) + rider.sems, compiler_params=_cp(("arbitrary",) * len(grid)),
    )(*operands, *rider.arrays)
    main, rest, per_rider = res[:n_out], list(res[n_out:]), []
    for r in riders:
        per_rider.append(None if r is None else [rest.pop(0) for _ in r.out_shapes])
    return (main[0] if single else main), per_rider


def inproj_fwd(x, g1, wa, wb, wc):
    s, d = x.shape

    def body(x_ref, g_ref, wa_ref, wb_ref, wc_ref, pa_ref, pb_ref, pc_ref):
        pa, pb, pc = _inproj(PLAIN, x_ref[...], g_ref[...], wa_ref[...], wb_ref[...], wc_ref[...])
        pa_ref[...] = pa
        pb_ref[...] = pb
        pc_ref[...] = pc

    return pl.pallas_call(
        body, name="inproj_fwd", grid=(s // TM,),
        in_specs=[_rows(TM, d), _full(g1), _full(wa), _full(wb), _full(wc)],
        out_specs=[_rows(TM, wa.shape[1]), _rows(TM, wb.shape[1]), _rows(TM, wc.shape[1])],
        out_shape=[_sds((s, wa.shape[1])), _sds((s, wb.shape[1])), _sds((s, wc.shape[1]))],
        compiler_params=_cp(("parallel",)),
    )(x, g1, wa, wb, wc)


def inproj_bwd(x, g1, wa, wb, wc, dpa, dpb, dpc, dres):
    s, d = x.shape

    def body(x_ref, g_ref, wa_ref, wb_ref, wc_ref, dpa_ref, dpb_ref, dpc_ref, dres_ref,
             dx_ref, dg_ref, dwa_ref, dwb_ref, dwc_ref):
        first = pl.program_id(0) == 0
        _, vjp = jax.vjp(functools.partial(_inproj, AD), x_ref[...], g_ref[...],
                         _f32(wa_ref), _f32(wb_ref), _f32(wc_ref))
        dx, dg, dwa, dwb, dwc = vjp((dpa_ref[...], dpb_ref[...], dpc_ref[...]))
        dx_ref[...] = dx + dres_ref[...]
        _acc(dg_ref, dg, first)
        _acc(dwa_ref, dwa, first)
        _acc(dwb_ref, dwb, first)
        _acc(dwc_ref, dwc, first)

    return pl.pallas_call(
        body, name="inproj_bwd", grid=(s // TM,),
        in_specs=[_rows(TM, d), _full(g1), _full(wa), _full(wb), _full(wc),
                  _rows(TM, wa.shape[1]), _rows(TM, wb.shape[1]), _rows(TM, wc.shape[1]), _rows(TM, d)],
        out_specs=[_rows(TM, d), _full(g1), _full(wa), _full(wb), _full(wc)],
        out_shape=[_sds((s, d)), _sds(g1.shape), _sds(wa.shape), _sds(wb.shape), _sds(wc.shape)],
        compiler_params=_cp(("arbitrary",)),
    )(x, g1, wa, wb, wc, dpa, dpb, dpc, dres)


def gm_fwd(pa, vg, ws4, bs, og):
    s = pa.shape[0]
    w = pa.shape[1] // 2

    def body(pa_ref, vg_ref, ws_ref, bs_ref, og_ref, ya_ref):
        bsl = [bs_ref[h] for h in range(4)]
        for j in range(GM_CHUNKS):
            rows = slice(j * CHUNK, (j + 1) * CHUNK)
            ya_ref[rows, :] = _gm_chunk(PLAIN, pa_ref[rows, 0:w], pa_ref[rows, w:2 * w], vg_ref[...], ws_ref[...], bsl,
                                        og_ref[...])

    tm = GM_CHUNKS * CHUNK
    return pl.pallas_call(
        body, name="gm_fwd", grid=(s // tm,),
        in_specs=[_rows(tm, 2 * w), _full(vg), _full(ws4), _full(bs), _full(og)],
        out_specs=_rows(tm, w), out_shape=_sds((s, w)),
        compiler_params=_cp(("parallel",)),
    )(pa, vg, ws4, bs, og)


def gm_bwd(pa, vg, ws4, bs, og, dya):
    s = pa.shape[0]
    w = pa.shape[1] // 2

    def body(pa_ref, vg_ref, ws_ref, bs_ref, og_ref, dya_ref, dpa_ref, dvg_ref, dws_ref, dbs_ref, dog_ref):
        first = pl.program_id(0) == 0
        bsl = [bs_ref[h] for h in range(4)]
        tot = None
        for j in range(GM_CHUNKS):
            rows = slice(j * CHUNK, (j + 1) * CHUNK)
            _, vjp = jax.vjp(functools.partial(_gm_chunk, AD), pa_ref[rows, 0:w], pa_ref[rows, w:2 * w],
                             vg_ref[...], ws_ref[...], bsl, og_ref[...])
            du, dv, *dws = vjp(dya_ref[rows, :])
            dpa_ref[rows, 0:w] = du
            dpa_ref[rows, w:2 * w] = dv
            tot = dws if tot is None else jax.tree.map(jnp.add, tot, dws)
        dvg, dws, dbs, dog = tot
        _acc(dvg_ref, dvg, first)
        _acc(dws_ref, dws, first)
        _acc(dog_ref, dog, first)
        for h in range(4):
            _acc(dbs_ref.at[h], dbs[h], first)

    tm = GM_CHUNKS * CHUNK
    return pl.pallas_call(
        body, name="gm_bwd", grid=(s // tm,),
        in_specs=[_rows(tm, 2 * w), _full(vg), _full(ws4), _full(bs), _full(og), _rows(tm, w)],
        out_specs=[_rows(tm, 2 * w), _full(vg), _full(ws4), _full(bs), _full(og)],
        out_shape=[_sds((s, 2 * w)), _sds(vg.shape), _sds(ws4.shape), _sds(bs.shape), _sds(og.shape)],
        compiler_params=_cp(("arbitrary",)),
    )(pa, vg, ws4, bs, og, dya)


def hg_fwd(pb, lb, og):
    s = pb.shape[0]
    w = pb.shape[1] // 4
    tm = HG_CHUNKS * HG_CHUNK
    st_shape = (w // HEAD64, HEAD64, HEAD64)

    def body(pb_ref, lb_ref, og_ref, yb_ref, states_ref, st_ref):
        @pl.when(pl.program_id(0) == 0)
        def _():
            st_ref[...] = jnp.zeros_like(st_ref)

        st = st_ref[...]
        for j in range(HG_CHUNKS):
            rows = slice(j * HG_CHUNK, (j + 1) * HG_CHUNK)
            states_ref[j] = st
            st, y = _hg_chunk(PLAIN, st, pb_ref[rows, 0:w], pb_ref[rows, w:2 * w], pb_ref[rows, 2 * w:3 * w],
                              pb_ref[rows, 3 * w:4 * w], lb_ref[...], og_ref[...])
            yb_ref[rows, :] = y
        st_ref[...] = st

    return pl.pallas_call(
        body, name="hg_fwd", grid=(s // tm,),
        in_specs=[_rows(tm, 4 * w), _full(lb), _full(og)],
        out_specs=[_rows(tm, w), pl.BlockSpec((HG_CHUNKS,) + st_shape, lambda i: (i, 0, 0, 0))],
        out_shape=[_sds((s, w)), _sds((s // HG_CHUNK,) + st_shape)],
        scratch_shapes=[pltpu.VMEM(st_shape, F32)],
        compiler_params=_cp(("arbitrary",)),
    )(pb, lb, og)


def hg_bwd(pb, lb, og, states, dyb, riders=()):
    s = pb.shape[0]
    w = pb.shape[1] // 4
    tm = HG_CHUNKS * HG_CHUNK
    nc = s // tm
    st_shape = (w // HEAD64, HEAD64, HEAD64)

    def body(pb_ref, lb_ref, og_ref, states_ref, dyb_ref, dpb_ref, dlb_ref, dog_ref, dst_ref):
        first = pl.program_id(0) == 0

        @pl.when(first)
        def _():
            dst_ref[...] = jnp.zeros_like(dst_ref)

        dst, dlb, dog = dst_ref[...], None, None
        for j in reversed(range(HG_CHUNKS)):
            rows = slice(j * HG_CHUNK, (j + 1) * HG_CHUNK)
            _, vjp = jax.vjp(functools.partial(_hg_chunk, AD), states_ref[j], pb_ref[rows, 0:w], pb_ref[rows, w:2 * w],
                             pb_ref[rows, 2 * w:3 * w], pb_ref[rows, 3 * w:4 * w], lb_ref[...], og_ref[...])
            dst, dq, df, di, dg, dlb_j, dog_j = vjp((dst, dyb_ref[rows, :]))
            dpb_ref[rows, 0:w] = dq
            dpb_ref[rows, w:2 * w] = df
            dpb_ref[rows, 2 * w:3 * w] = di
            dpb_ref[rows, 3 * w:4 * w] = dg
            dlb = dlb_j if dlb is None else dlb + dlb_j
            dog = dog_j if dog is None else dog + dog_j
        dst_ref[...] = dst
        _acc(dlb_ref, dlb, first)
        _acc(dog_ref, dog, first)

    rev = lambda i: (nc - 1 - i, 0)
    return _ride(
        body, riders, name="hg_bwd", grid=(nc,),
        in_specs=[pl.BlockSpec((tm, 4 * w), rev), _full(lb), _full(og),
                  pl.BlockSpec((HG_CHUNKS,) + st_shape, lambda i: (nc - 1 - i, 0, 0, 0)), pl.BlockSpec((tm, w), rev)],
        out_specs=[pl.BlockSpec((tm, 4 * w), rev), _full(lb), _full(og)],
        out_shape=[_sds((s, 4 * w)), _sds(lb.shape), _sds(og.shape)],
        scratch_shapes=[pltpu.VMEM(st_shape, F32)],
        operands=(pb, lb, og, states, dyb), sem=("arbitrary",))


def lower_bounds_fwd(hlb):
    def body(h_ref, o_ref):
        outs = _lower_bounds(*[h_ref[pl.ds(i, 1), :] for i in range(DEPTH)])
        for i in range(DEPTH):
            o_ref[pl.ds(i, 1), :] = outs[i]

    return pl.pallas_call(body, name="lower_bounds_fwd", out_shape=_sds(hlb.shape))(hlb)


def lower_bounds_bwd(hlb, dlbs):
    def body(h_ref, d_ref, o_ref):
        _, vjp = jax.vjp(_lower_bounds, *[h_ref[pl.ds(i, 1), :] for i in range(DEPTH)])
        outs = vjp(tuple(d_ref[pl.ds(i, 1), :] for i in range(DEPTH)))
        for i in range(DEPTH):
            o_ref[pl.ds(i, 1), :] = outs[i]

    return pl.pallas_call(body, name="lower_bounds_bwd", out_shape=_sds(hlb.shape))(hlb, dlbs)


def _mla_pre_args(pc_ref, cos_ref, sa_ref, sb_ref, qag_ref, kvag_ref, qg_ref, kg_ref, wq_ref, wk_ref, wv_ref, cast):
    sl = lambda h: slice(h * SLOT, (h + 1) * SLOT)
    ld = (lambda r, h: r[:, sl(h)].astype(F32)) if cast else (lambda r, h: r[:, sl(h)])
    diff = (pc_ref[:, 0:Q_LORA], pc_ref[:, Q_LORA:Q_LORA + KV_LORA], pc_ref[:, Q_LORA + KV_LORA:Q_LORA + 2 * KV_LORA],
            qag_ref[...], kvag_ref[...], qg_ref[...], kg_ref[...],
            [ld(wq_ref, h) for h in range(MLA_HEADS)], [ld(wk_ref, h) for h in range(MLA_HEADS)],
            [ld(wv_ref, h) for h in range(MLA_HEADS)])
    tables = (cos_ref[...], sa_ref[...], sb_ref[...])
    return diff, tables


def _mla_pre_fn(ops, tables, cq, ckv, kpe, qag, kvag, qg, kg, wq, wk, wv):
    return _mla_pre(ops, cq, ckv, kpe, *tables, qag, kvag, qg, kg, wq, wk, wv)


def mla_pre_fwd(pc, cos_t, sin_a, sin_b, qag, kvag, qg, kg, wq, wk, wv):
    s = pc.shape[0]
    hw = MLA_HEADS * SLOT

    def body(pc_ref, cos_ref, sa_ref, sb_ref, qag_ref, kvag_ref, qg_ref, kg_ref, wq_ref, wk_ref, wv_ref,
             q_ref, k_ref, v_ref):
        diff, tables = _mla_pre_args(pc_ref, cos_ref, sa_ref, sb_ref, qag_ref, kvag_ref, qg_ref, kg_ref,
                                     wq_ref, wk_ref, wv_ref, False)
        qs, ks, vs = _mla_pre_fn(PLAIN, tables, *diff)
        ones_lane = (lax.broadcasted_iota(jnp.int32, (1, SLOT), 1) == V_DIM).astype(F32)
        for h in range(MLA_HEADS):
            q_ref[:, h * SLOT:(h + 1) * SLOT] = qs[h].astype(q_ref.dtype)
            k_ref[:, h * SLOT:(h + 1) * SLOT] = ks[h].astype(k_ref.dtype)
            v_ref[:, h * SLOT:(h + 1) * SLOT] = (vs[h] + ones_lane).astype(v_ref.dtype)

    return pl.pallas_call(
        body, name="mla_pre_fwd", grid=(s // TM,),
        in_specs=[_rows(TM, pc.shape[1]), _rows(TM, SLOT), _rows(TM, SLOT), _rows(TM, SLOT),
                  _full(qag), _full(kvag), _full(qg), _full(kg), _full(wq), _full(wk), _full(wv)],
        out_specs=[_rows(TM, hw)] * 3, out_shape=[_sds((s, hw), MXU_DTYPE)] * 3,
        compiler_params=_cp(("parallel",)),
    )(pc, cos_t, sin_a, sin_b, qag, kvag, qg, kg, wq, wk, wv)


def mla_pre_bwd(pc, cos_t, sin_a, sin_b, qag, kvag, qg, kg, wq, wk, wv, dq, dk, dv):
    s = pc.shape[0]
    hw = MLA_HEADS * SLOT

    def body(pc_ref, cos_ref, sa_ref, sb_ref, qag_ref, kvag_ref, qg_ref, kg_ref, wq_ref, wk_ref, wv_ref,
             dq_ref, dk_ref, dv_ref, dpc_ref, dqag_ref, dkvag_ref, dqg_ref, dkg_ref, dwq_ref, dwk_ref, dwv_ref):
        first = pl.program_id(0) == 0
        diff, tables = _mla_pre_args(pc_ref, cos_ref, sa_ref, sb_ref, qag_ref, kvag_ref, qg_ref, kg_ref,
                                     wq_ref, wk_ref, wv_ref, True)
        _, vjp = jax.vjp(functools.partial(_mla_pre_fn, AD, tables), *diff)
        sl = lambda h: slice(h * SLOT, (h + 1) * SLOT)
        cot = ([dq_ref[:, sl(h)] for h in range(MLA_HEADS)], [dk_ref[:, sl(h)] for h in range(MLA_HEADS)],
               [dv_ref[:, sl(h)] for h in range(MLA_HEADS)])
        dcq, dckv, dkpe, dqag, dkvag, dqg, dkg, dwq, dwk, dwv = vjp(cot)
        dpc_ref[:, 0:Q_LORA] = dcq
        dpc_ref[:, Q_LORA:Q_LORA + KV_LORA] = dckv
        dpc_ref[:, Q_LORA + KV_LORA:Q_LORA + 2 * KV_LORA] = dkpe
        _acc(dqag_ref, dqag, first)
        _acc(dkvag_ref, dkvag, first)
        _acc(dqg_ref, dqg, first)
        _acc(dkg_ref, dkg, first)
        for h in range(MLA_HEADS):
            _acc(dwq_ref.at[:, sl(h)], dwq[h], first)
            _acc(dwk_ref.at[:, sl(h)], dwk[h], first)
            _acc(dwv_ref.at[:, sl(h)], dwv[h], first)

    return pl.pallas_call(
        body, name="mla_pre_bwd", grid=(s // TM,),
        in_specs=[_rows(TM, pc.shape[1]), _rows(TM, SLOT), _rows(TM, SLOT), _rows(TM, SLOT),
                  _full(qag), _full(kvag), _full(qg), _full(kg), _full(wq), _full(wk), _full(wv),
                  _rows(TM, hw), _rows(TM, hw), _rows(TM, hw)],
        out_specs=[_rows(TM, pc.shape[1]), _full(qag), _full(kvag), _full(qg), _full(kg),
                   _full(wq), _full(wk), _full(wv)],
        out_shape=[_sds(pc.shape), _sds(qag.shape), _sds(kvag.shape), _sds(qg.shape), _sds(kg.shape),
                   _sds(wq.shape), _sds(wk.shape), _sds(wv.shape)],
        compiler_params=_cp(("arbitrary",)),
    )(pc, cos_t, sin_a, sin_b, qag, kvag, qg, kg, wq, wk, wv, dq, dk, dv)


ATT_SCALE = QK_DIM ** -0.5
NEG_BIG = -1e30


def attn_fwd(q, k, v, riders=()):
    s = q.shape[0]
    nq = s // TQ
    hp = ATT_HEADS_PER_STEP
    sl = lambda j: slice(j * SLOT, (j + 1) * SLOT)

    wide = ATT_WIDE // TQ

    def body(q_ref, k_ref, v_ref, o_ref, lse_ref):
        qi = pl.program_id(1)
        lane = lax.broadcasted_iota(jnp.int32, (1, SLOT), 1)
        qs = [q_ref[:, sl(j)] for j in range(hp)]

        def step(ki, carry, n_tiles, masked):
            rk = pl.ds(pl.multiple_of(ki * TQ, TQ), n_tiles * TQ)
            if masked:
                row = lax.broadcasted_iota(jnp.int32, (TQ, n_tiles * TQ), 0) + (n_tiles - 1) * TQ
                col = lax.broadcasted_iota(jnp.int32, (TQ, n_tiles * TQ), 1)
            out = []
            for j in range(hp):
                m, acc = carry[j]
                sc = _dot(qs[j], k_ref[rk, sl(j)], NT) * ATT_SCALE
                if masked:
                    sc = jnp.where(row >= col, sc, NEG_BIG)
                m_new = jnp.maximum(m, jnp.max(sc, axis=-1, keepdims=True))
                acc = jnp.exp(m - m_new) * acc + _dot(jnp.exp(sc - m_new), v_ref[rk, sl(j)], NN)
                out.append((m_new, acc))
            return tuple(out)

        def tail_single(cr):
            cr = lax.fori_loop(n_wide * wide, qi, lambda ki, c: step(ki, c, 1, False), cr)
            return step(qi, cr, 1, True)

        n_wide = qi // wide
        init = tuple((jnp.full((TQ, 1), NEG_BIG, F32), jnp.zeros((TQ, SLOT), F32)) for _ in range(hp))
        carry = lax.fori_loop(0, n_wide, lambda kw, cr: step(kw * wide, cr, wide, False), init)
        carry = lax.cond(qi % wide == wide - 1, lambda cr: step(qi - (wide - 1), cr, wide, True), tail_single, carry)
        for j in range(hp):
            m, acc = carry[j]
            l = jnp.sum(jnp.where(lane == V_DIM, acc, 0.0), axis=-1, keepdims=True)
            o_ref[:, sl(j)] = jnp.where(lane < V_DIM, acc / l, 0.0)
            lse_ref[j] = m + jnp.log(l)

    head_col = pl.BlockSpec((s, hp * SLOT), lambda g, i: (0, g))
    tile = pl.BlockSpec((TQ, hp * SLOT), lambda g, i: (i, g))
    return _ride(
        body, riders, name="attn_fwd", grid=(MLA_HEADS // hp, nq),
        in_specs=[tile, head_col, head_col],
        out_specs=[tile, pl.BlockSpec((hp, TQ, 1), lambda g, i: (g, i, 0))],
        out_shape=[_sds((s, MLA_HEADS * SLOT)), _sds((MLA_HEADS, s, 1))],
        operands=(q, k, v), sem=("parallel", "parallel"))


def attn_bwd(q, k, v, o, do, lse, riders=()):
    s = q.shape[0]
    nq = s // TQ
    hp = ATT_HEADS_PER_STEP
    sl = lambda j: slice(j * SLOT, (j + 1) * SLOT)
    wide = ATT_WIDE // TQ

    def body(q_ref, k_ref, v_ref, o_ref, do_ref, lse_ref, dq_ref, dk_ref, dv_ref, delta_ref):
        ki = pl.program_id(1)

        @pl.when(ki == 0)
        def _():
            dq_ref[...] = jnp.zeros_like(dq_ref)

            def prep(i, c):
                rows = pl.ds(pl.multiple_of(i * TQ, TQ), TQ)
                for j in range(hp):
                    delta_ref[j, rows, :] = jnp.sum(do_ref[rows, sl(j)] * o_ref[rows, sl(j)], axis=-1, keepdims=True)
                return c

            lax.fori_loop(0, nq, prep, 0)

        kks = [k_ref[:, sl(j)] for j in range(hp)]
        vvs = [v_ref[:, sl(j)] for j in range(hp)]

        def step(qi, carry, n_tiles, masked):
            rq = pl.ds(pl.multiple_of(qi * TQ, TQ), n_tiles * TQ)
            if masked:
                row = lax.broadcasted_iota(jnp.int32, (n_tiles * TQ, TQ), 0)
                col = lax.broadcasted_iota(jnp.int32, (n_tiles * TQ, TQ), 1)
            out = []
            for j in range(hp):
                dk, dv = carry[j]
                qq = q_ref[rq, sl(j)]
                dd = do_ref[rq, sl(j)]
                sc = _dot(qq, kks[j], NT) * ATT_SCALE
                if masked:
                    sc = jnp.where(row >= col, sc, NEG_BIG)
                p = jnp.exp(sc - lse_ref[j, rq, :])
                dv = dv + _dot(p, dd, TN)
                ds = p * (_dot(dd, vvs[j], NT) - delta_ref[j, rq, :]) * ATT_SCALE
                dk = dk + _dot(ds, qq, TN)
                dq_ref[rq, sl(j)] = dq_ref[rq, sl(j)] + _dot(ds, kks[j], NN)
                out.append((dk, dv))
            return tuple(out)

        def head_single(cr):
            cr = step(ki, cr, 1, True)
            return lax.fori_loop(ki + 1, first_wide * wide, lambda qi, c: step(qi, c, 1, False), cr)

        zero = jnp.zeros((TQ, SLOT), F32)
        first_wide = (ki + wide) // wide
        carry = tuple((zero, zero) for _ in range(hp))
        carry = lax.cond(ki % wide == 0, lambda cr: step(ki, cr, wide, True), head_single, carry)
        carry = lax.fori_loop(first_wide, nq // wide, lambda qw, cr: step(qw * wide, cr, wide, False), carry)
        for j in range(hp):
            dk_ref[:, sl(j)] = carry[j][0]
            dv_ref[:, sl(j)] = carry[j][1]

    head_col = pl.BlockSpec((s, hp * SLOT), lambda g, i: (0, g))
    tile = pl.BlockSpec((TQ, hp * SLOT), lambda g, i: (i, g))
    return _ride(
        body, riders, name="attn_bwd", grid=(MLA_HEADS // hp, nq),
        in_specs=[head_col, tile, tile, head_col, head_col, pl.BlockSpec((hp, s, 1), lambda g, i: (g, 0, 0))],
        out_specs=[head_col, tile, tile],
        out_shape=[_sds((s, MLA_HEADS * SLOT))] * 3,
        scratch_shapes=[pltpu.VMEM((hp, s, 1), F32)],
        operands=(q, k, v, o, do, lse), sem=("arbitrary", "arbitrary"))


def _outproj_args(ya_ref, yb_ref, o_ref, mog_ref, woa_ref, wob_ref, woc_ref, cast):
    sl = lambda h: slice(h * SLOT, (h + 1) * SLOT)
    ldw = (lambda r: r[...].astype(F32)) if cast else (lambda r: r[...])
    ldc = (lambda h: woc_ref[sl(h), :].astype(F32)) if cast else (lambda h: woc_ref[sl(h), :])
    return (ya_ref[...], yb_ref[...], [o_ref[:, sl(h)] for h in range(MLA_HEADS)],
            [mog_ref[:, sl(h)] for h in range(MLA_HEADS)], ldw(woa_ref), ldw(wob_ref),
            [ldc(h) for h in range(MLA_HEADS)])


def outproj_fwd(x, ya, yb, o, mog, woa, wob, woc):
    s, d = x.shape

    def body(x_ref, ya_ref, yb_ref, o_ref, mog_ref, woa_ref, wob_ref, woc_ref, x1_ref):
        x1_ref[...] = _outproj(PLAIN, x_ref[...], *_outproj_args(ya_ref, yb_ref, o_ref, mog_ref, woa_ref, wob_ref,
                                                                  woc_ref, False))

    return pl.pallas_call(
        body, name="outproj_fwd", grid=(s // TM,),
        in_specs=[_rows(TM, d), _rows(TM, ya.shape[1]), _rows(TM, yb.shape[1]), _rows(TM, o.shape[1]),
                  _full(mog), _full(woa), _full(wob), _full(woc)],
        out_specs=_rows(TM, d), out_shape=_sds((s, d)),
        compiler_params=_cp(("parallel",)),
    )(x, ya, yb, o, mog, woa, wob, woc)


def outproj_bwd(ya, yb, o, mog, woa, wob, woc, dx2, dx1p, riders=()):
    s, d = dx2.shape
    npart = dx1p.shape[0]

    def body(ya_ref, yb_ref, o_ref, mog_ref, woa_ref, wob_ref, woc_ref, dx2_ref, dx1p_ref,
             dx1_ref, dya_ref, dyb_ref, do_ref, dmog_ref, dwoa_ref, dwob_ref, dwoc_ref):
        first = pl.program_id(0) == 0
        sl = lambda h: slice(h * SLOT, (h + 1) * SLOT)
        dx1 = dx2_ref[...]
        for p in range(npart):
            dx1 = dx1 + dx1p_ref[p]
        dx1_ref[...] = dx1
        args = _outproj_args(ya_ref, yb_ref, o_ref, mog_ref, woa_ref, wob_ref, woc_ref, True)
        _, vjp = jax.vjp(lambda *a: _outproj(AD, jnp.zeros_like(dx1), *a), *args)
        dya, dyb, do, dmog, dwoa, dwob, dwoc = vjp(dx1)
        dya_ref[...] = dya
        dyb_ref[...] = dyb
        _acc(dwoa_ref, dwoa, first)
        _acc(dwob_ref, dwob, first)
        for h in range(MLA_HEADS):
            do_ref[:, sl(h)] = do[h]
            _acc(dmog_ref.at[:, sl(h)], dmog[h], first)
            _acc(dwoc_ref.at[sl(h), :], dwoc[h], first)

    return _ride(
        body, riders, name="outproj_bwd", grid=(s // TM,),
        in_specs=[_rows(TM, ya.shape[1]), _rows(TM, yb.shape[1]), _rows(TM, o.shape[1]),
                  _full(mog), _full(woa), _full(wob), _full(woc), _rows(TM, d),
                  pl.BlockSpec((npart, TM, d), lambda i: (0, i, 0))],
        out_specs=[_rows(TM, d), _rows(TM, ya.shape[1]), _rows(TM, yb.shape[1]), _rows(TM, o.shape[1]),
                   _full(mog), _full(woa), _full(wob), _full(woc)],
        out_shape=[_sds((s, d)), _sds(ya.shape), _sds(yb.shape), _sds(o.shape),
                   _sds(mog.shape), _sds(woa.shape), _sds(wob.shape), _sds(woc.shape)],
        operands=(ya, yb, o, mog, woa, wob, woc, dx2, dx1p), sem=("arbitrary",))


def ffn_fwd(x1, g2, w1, w2, riders=()):
    s, d = x1.shape
    npart, _, fs = w1.shape

    def body(x1_ref, g_ref, w1_ref, w2_ref, x2_ref):
        p = pl.program_id(1)
        x1v = x1_ref[...]
        part = _ffn_part(PLAIN, x1v, g_ref[...], w1_ref[...], w2_ref[...])

        @pl.when(p == 0)
        def _():
            x2_ref[...] = x1v + part

        @pl.when(p != 0)
        def _():
            x2_ref[...] = x2_ref[...] + part

    tm = TM_FFN
    return _ride(
        body, riders, name="ffn_fwd", grid=(s // tm, npart),
        in_specs=[pl.BlockSpec((tm, d), lambda i, p: (i, 0)), pl.BlockSpec(g2.shape, lambda i, p: (0, 0)),
                  pl.BlockSpec((None, d, fs), lambda i, p: (p, 0, 0)), pl.BlockSpec((None, fs, d), lambda i, p: (p, 0, 0))],
        out_specs=pl.BlockSpec((tm, d), lambda i, p: (i, 0)), out_shape=_sds((s, d)),
        operands=(x1, g2, w1, w2), sem=("parallel", "arbitrary"))


def ffn_bwd(x1, g2, w1, w2, dx2, riders=()):
    s, d = x1.shape
    npart, _, fs = w1.shape

    def body(x1_ref, g_ref, w1_ref, w2_ref, dx2_ref, dx1p_ref, dg_ref, dw1_ref, dw2_ref):
        p = pl.program_id(0)
        i = pl.program_id(1)
        _, vjp = jax.vjp(functools.partial(_ffn_part, AD), x1_ref[...], g_ref[...], _f32(w1_ref), _f32(w2_ref))
        dx1, dg, dw1, dw2 = vjp(dx2_ref[...])
        dx1p_ref[...] = dx1
        _acc(dg_ref, dg, (p == 0) & (i == 0))
        _acc(dw1_ref, dw1, i == 0)
        _acc(dw2_ref, dw2, i == 0)

    tm = TM_FFN
    return _ride(
        body, riders, name="ffn_bwd", grid=(npart, s // tm),
        in_specs=[pl.BlockSpec((tm, d), lambda p, i: (i, 0)), pl.BlockSpec(g2.shape, lambda p, i: (0, 0)),
                  pl.BlockSpec((None, d, fs), lambda p, i: (p, 0, 0)), pl.BlockSpec((None, fs, d), lambda p, i: (p, 0, 0)),
                  pl.BlockSpec((tm, d), lambda p, i: (i, 0))],
        out_specs=[pl.BlockSpec((None, tm, d), lambda p, i: (p, i, 0)), pl.BlockSpec(g2.shape, lambda p, i: (0, 0)),
                   pl.BlockSpec((None, d, fs), lambda p, i: (p, 0, 0)), pl.BlockSpec((None, fs, d), lambda p, i: (p, 0, 0))],
        out_shape=[_sds((npart, s, d)), _sds(g2.shape), _sds(w1.shape), _sds(w2.shape)],
        operands=(x1, g2, w1, w2, dx2), sem=("arbitrary", "arbitrary"))


def loss_head(y, target):
    s, d = y.shape

    def body(y_ref, t_ref, dy_ref, loss_ref):
        err = y_ref[...] - t_ref[...]
        dy_ref[...] = err * (1.0 / d)
        part = jnp.sum(jnp.sum(err * err, axis=-1, keepdims=True), axis=0, keepdims=True) * (0.5 / d)
        _acc(loss_ref, jnp.broadcast_to(part, loss_ref.shape), pl.program_id(0) == 0)

    return pl.pallas_call(
        body, name="loss_head", grid=(s // TM,),
        in_specs=[_rows(TM, d), _rows(TM, d)],
        out_specs=[_rows(TM, d), pl.BlockSpec((1, SLOT), lambda i: (0, 0))],
        out_shape=[_sds((s, d)), _sds((1, SLOT))],
        compiler_params=_cp(("arbitrary",)),
    )(y, target)


def _row_block(r):
    for b in (512, 256, 128, 64, 32, 16, 8):
        if r % b == 0:
            return b
    return r


def sum_cores(arrs, gots, half, me):
    n = len(arrs)

    def body(sp_ref, *refs):
        for i in range(n):
            a_ref, g_ref, wire_ref, own_ref = refs[i], refs[n + i], refs[2 * n + i], refs[3 * n + i]
            tot = a_ref[...] + g_ref[...]
            wire_ref[...] = tot.astype(wire_ref.dtype)

            @pl.when(pl.program_id(0) == sp_ref[1])
            def _(own_ref=own_ref, tot=tot):
                own_ref[...] = tot

    shapes = [a.shape[2:] for a in arrs]
    grid_spec = pltpu.PrefetchScalarGridSpec(
        num_scalar_prefetch=1, grid=(N_CHIPS,),
        in_specs=[pl.BlockSpec((None, None) + sh, lambda p, sp: (p, sp[0], 0, 0)) for sh in shapes]
        + [pl.BlockSpec((None,) + sh, lambda p, sp: (p, 0, 0)) for sh in shapes],
        out_specs=[pl.BlockSpec((None,) + sh, lambda p, sp: (p, 0, 0)) for sh in shapes]
        + [pl.BlockSpec(sh, lambda p, sp: (0, 0)) for sh in shapes])
    outs = pl.pallas_call(body, name="sum_cores", grid_spec=grid_spec,
                          out_shape=[_sds((N_CHIPS,) + sh, BF16) for sh in shapes] + [_sds(sh) for sh in shapes],
                          compiler_params=_cp(("arbitrary",)))(jnp.stack([half, me]).astype(jnp.int32), *arrs, *gots)
    return outs[:n], outs[n:]


SUM_STEPS = 4


def sum_chips(owns, recvs, half):
    n = len(owns)

    def body(sp_ref, *refs):
        del sp_ref
        for i in range(n):
            own_ref, out_ref = refs[4 * i], refs[4 * n + i]
            r0, r1, r2 = (refs[4 * i + 1 + j][...].astype(F32) for j in range(3))
            out_ref[...] = ((own_ref[...] + r0) + r1) + r2

    in_specs, out_specs, operands = [], [], []
    for own, recv in zip(owns, recvs):
        r, c = own.shape
        br = r // SUM_STEPS
        in_specs.append(pl.BlockSpec((br, c), lambda i, sp: (i, 0)))
        in_specs += [pl.BlockSpec((None, br, c), functools.partial(lambda i, sp, j: (j, i, 0), j=j)) for j in range(3)]
        out_specs.append(pl.BlockSpec((None, br, c), lambda i, sp: (sp[0], i, 0)))
        operands += [own, recv, recv, recv]
    grid_spec = pltpu.PrefetchScalarGridSpec(num_scalar_prefetch=1, grid=(SUM_STEPS,), in_specs=in_specs, out_specs=out_specs)
    return pl.pallas_call(body, name="sum_chips", grid_spec=grid_spec, out_shape=[_sds((2,) + o.shape) for o in owns],
                          compiler_params=_cp(("parallel",)))(half.reshape(1).astype(jnp.int32), *operands)


def adamw(w, g, m, v, name, riders=()):
    r, c = w.shape
    br = _row_block(r)
    c1 = 1.0 / (1.0 - ADAM_B1 ** ADAM_STEP)
    c2 = 1.0 / (1.0 - ADAM_B2 ** ADAM_STEP)

    def body(w_ref, g_ref, m_ref, v_ref, d_ref, nm_ref, nv_ref):
        gg = g_ref[...]
        nm = ADAM_B1 * m_ref[...] + (1.0 - ADAM_B1) * gg
        nv = ADAM_B2 * v_ref[...] + (1.0 - ADAM_B2) * (gg * gg)
        d_ref[...] = -ADAM_LR * ((nm * c1) / (jnp.sqrt(nv * c2) + ADAM_EPS) + ADAM_WD * w_ref[...])
        nm_ref[...] = nm
        nv_ref[...] = nv

    return _ride(body, riders, name=name, grid=(r // br,), in_specs=[_rows(br, c)] * 4, out_specs=[_rows(br, c)] * 3,
                 out_shape=[_sds((r, c))] * 3, operands=(w, g, m, v), sem=("parallel",))


def _place():
    x, y, c = lax.axis_index("x"), lax.axis_index("y"), lax.axis_index("c")
    chips = [(1 - x, y), (x, 1 - y), (1 - x, 1 - y)]
    return x, y, c, chips


def _remote(src, dst, send_sem, recv_sem, to):
    return pltpu.make_async_remote_copy(src_ref=src, dst_ref=dst, send_sem=send_sem, recv_sem=recv_sem,
                                        device_id=to, device_id_type=MESH)


def gather_rider(arrs):
    n = len(arrs)
    me_chip = 2 * lax.axis_index("x") + lax.axis_index("y")
    bufs = [lax.dynamic_update_index_in_dim(lax.empty((N_CHIPS,) + a.shape, a.dtype), a, me_chip, 0) for a in arrs]

    def plan(ins, outs, sems):
        send_sems, recv_sems = sems
        x, y, c, chips = _place()
        me = 2 * x + y
        half, other, sibling = pl.ds(2 * c, 2), pl.ds(2 - 2 * c, 2), (x, y, 1 - c)
        cp = lambda i, k, src, dst, to: _remote(src, dst, send_sems.at[i, k], recv_sems.at[i, k], to)
        pairs = [(i, j, cx, cy) for i in range(n) for j, (cx, cy) in enumerate(chips)]
        blk = lambda i, cx, cy, part: outs[i].at[2 * cx + cy, part]
        first = lambda: [cp(i, j, ins[i].at[half], outs[i].at[me, half], (cx, cy, c)) for i, j, cx, cy in pairs]
        landed = lambda: [cp(i, j, blk(i, cx, cy, half), blk(i, cx, cy, half), (cx, cy, c)) for i, j, cx, cy in pairs]
        passed = lambda: [cp(i, 3 + j, blk(i, cx, cy, half), blk(i, cx, cy, half), sibling) for i, j, cx, cy in pairs]
        from_sibling = lambda: [cp(i, 3 + j, blk(i, cx, cy, other), blk(i, cx, cy, other), sibling) for i, j, cx, cy in pairs]
        return first, landed, passed, from_sibling

    def start(ins, outs, sems):
        for cp in plan(ins, outs, sems)[0]():
            cp.start()

    def finish(ins, outs, sems):
        first, landed, passed, from_sibling = plan(ins, outs, sems)
        forwards = passed()
        for a, b in zip(landed(), forwards):
            a.wait_recv()
            b.start()
        for cp in from_sibling():
            cp.wait_recv()
        for cp in first() + forwards:
            cp.wait_send()

    return Rider(list(arrs) + bufs, [_sds((N_CHIPS,) + a.shape, a.dtype) for a in arrs], {n + i: i for i in range(n)},
                 [pltpu.SemaphoreType.DMA((n, 6)), pltpu.SemaphoreType.DMA((n, 6))], start, finish)


class Reducer:
    def __init__(self, arrs):
        self.a = list(arrs)
        self.n = len(self.a)
        self.c = lax.axis_index("c")
        self.me = 2 * lax.axis_index("x") + lax.axis_index("y")

    def swap_rider(self):
        n = self.n

        def plan(ins, outs, sems):
            x, y, c, _ = _place()
            return [_remote(ins[i].at[p, 1 - c], outs[i].at[p], sems[0].at[i, p], sems[1].at[i, p], (x, y, 1 - c))
                    for i in range(n) for p in range(N_CHIPS)]

        return Rider(self.a, [_sds((N_CHIPS,) + a.shape[2:]) for a in self.a], {},
                     [pltpu.SemaphoreType.DMA((n, N_CHIPS)), pltpu.SemaphoreType.DMA((n, N_CHIPS))],
                     lambda *r: [cp.start() for cp in plan(*r)], lambda *r: [cp.wait() for cp in plan(*r)])

    def after_swap(self, got):
        self.wire, self.own = sum_cores(self.a, got, self.c, self.me)

    def scatter_rider(self):
        n = self.n

        def plan(ins, outs, sems):
            x, y, c, chips = _place()
            return [_remote(ins[i].at[2 * cx + cy], outs[i].at[j], sems[0].at[i, j], sems[1].at[i, j], (cx, cy, c))
                    for i in range(n) for j, (cx, cy) in enumerate(chips)]

        return Rider(self.wire, [_sds((3,) + w.shape[1:], w.dtype) for w in self.wire], {},
                     [pltpu.SemaphoreType.DMA((n, 3)), pltpu.SemaphoreType.DMA((n, 3))],
                     lambda *r: [cp.start() for cp in plan(*r)], lambda *r: [cp.wait() for cp in plan(*r)])

    def after_scatter(self, recv):
        self.full = sum_chips(self.own, recv, self.c)

    def share_rider(self):
        n = self.n

        def plan(ins, outs, sems):
            x, y, c, _ = _place()
            return [_remote(ins[i].at[c], outs[i].at[c], sems[0].at[i], sems[1].at[i], (x, y, 1 - c)) for i in range(n)]

        return Rider(self.full, [_sds(f.shape) for f in self.full], {i: i for i in range(n)},
                     [pltpu.SemaphoreType.DMA((n,)), pltpu.SemaphoreType.DMA((n,))],
                     lambda *r: [cp.start() for cp in plan(*r)], lambda *r: [cp.wait() for cp in plan(*r)])

    def run(self):
        self.after_swap(run_rider(self.swap_rider(), "swap_halves"))
        self.after_scatter(run_rider(self.scatter_rider(), "scatter_chips"))
        return run_rider(self.share_rider(), "share_halves")


def _pad_slots(a, live):
    lead = a.shape[:-1]
    a = a.reshape(lead + (MLA_HEADS, live))
    a = jnp.pad(a, [(0, 0)] * len(lead) + [(0, 0), (0, SLOT - live)])
    return a.reshape(lead + (MLA_HEADS * SLOT,))


def _unpad_slots(a, live):
    lead = a.shape[:-1]
    return a.reshape(lead + (MLA_HEADS, SLOT))[..., :live].reshape(lead + (MLA_HEADS * live,))


def _rope_tables(positions, s):
    half = QK_ROPE // 2
    inv_freq = ROPE_THETA ** (-jnp.arange(half, dtype=F32) / half)
    ang = positions.reshape(s).astype(F32)[:, None] * inv_freq[None, :]
    cos, sin = jnp.cos(ang), jnp.sin(ang)
    one = jnp.ones((s, QK_NOPE), F32)
    z64, z16, z32 = jnp.zeros((s, QK_NOPE), F32), jnp.zeros((s, half), F32), jnp.zeros((s, SLOT - QK_DIM), F32)
    cos_t = jnp.concatenate([one, cos, cos, z32], axis=1)
    sin_a = jnp.concatenate([z64, -sin, z16, z32], axis=1)
    sin_b = jnp.concatenate([z64, z16, sin, z32], axis=1)
    return cos_t, sin_a, sin_b


def _layer_weights(full, small, l):
    w_in = jnp.concatenate([full["w_in"][p] for p in range(N_CHIPS)], axis=1)
    wc = jnp.pad(w_in[:, 1536:], ((0, 0), (0, 512 - (w_in.shape[1] - 1536))))
    w_uq = jnp.concatenate([full["mla_w_uq"][p] for p in range(N_CHIPS)], axis=1)
    w_ukv = jnp.concatenate([full["mla_w_ukv"][p] for p in range(N_CHIPS)], axis=1)
    ukv = w_ukv.reshape(KV_LORA, MLA_HEADS, QK_NOPE + V_DIM)
    w_out = jnp.concatenate([full["w_out"][p] for p in range(N_CHIPS)], axis=0)
    woc = w_out[512:].reshape(MLA_HEADS, V_DIM, D_MODEL)
    woc = jnp.pad(woc, ((0, 0), (0, SLOT - V_DIM), (0, 0))).reshape(MLA_HEADS * SLOT, D_MODEL)
    row = lambda a: a.reshape(1, -1)
    return dict(
        g1=row(small["norm1_gain"][l]), wa=w_in[:, :512], wb=w_in[:, 512:1536], wc=wc,
        vg=row(small["gm_v_gain"][l]), ws=small["gm_w_s"][l], bs=small["gm_b_s"][l].reshape(4, CHUNK, 1),
        gog=row(small["gm_out_gain"][l]), hog=small["hg_out_gain"][l].reshape(-1, 1),
        qag=row(small["mla_q_a_gain"][l]), kvag=row(small["mla_kv_a_gain"][l]),
        qg=row(jnp.pad(small["mla_q_gain"][l], (0, SLOT - QK_DIM))), kg=row(jnp.pad(small["mla_k_gain"][l], (0, SLOT - QK_DIM))),
        wq=_pad_slots(w_uq, QK_DIM), wk=_pad_slots(ukv[..., :QK_NOPE].reshape(KV_LORA, -1), QK_NOPE),
        wv=_pad_slots(ukv[..., QK_NOPE:].reshape(KV_LORA, -1), V_DIM),
        mog=row(_pad_slots(small["mla_out_gain"][l], V_DIM)),
        woa=w_out[:256], wob=w_out[256:512], woc=woc,
        g2=row(small["norm2_gain"][l]),
    )


def _shard_cols(a):
    r, c4 = a.shape
    return a.reshape(r, N_CHIPS, c4 // N_CHIPS).transpose(1, 0, 2)


def local_step(x, positions, target, small, comm):
    s = x.shape[0]
    cos_t, sin_a, sin_b = _rope_tables(positions, s)
    lbs = lower_bounds_fwd(small["hg_lower_bound"])
    lw, saved = [], []
    for l in range(DEPTH):
        w = _layer_weights(comm.part(l, "mix"), small, l)
        lw.append(w)
        lb = lbs[l].reshape(-1, 1)
        pa, pb, pc = inproj_fwd(x, w["g1"], w["wa"], w["wb"], w["wc"])
        ya = gm_fwd(pa, w["vg"], w["ws"], w["bs"], w["gog"])
        yb, states = hg_fwd(pb, lb, w["hog"])
        q, k, v = mla_pre_fwd(pc, cos_t, sin_a, sin_b, w["qag"], w["kvag"], w["qg"], w["kg"], w["wq"], w["wk"], w["wv"])
        rider = comm.gather_rider(l, "ffn")
        (o, lse), got = attn_fwd(q, k, v, [rider])
        comm.gathered(l, "ffn", got[0])
        x1 = outproj_fwd(x, ya, yb, o, w["mog"], w["woa"], w["wob"], w["woc"])
        ffn_w = comm.part(l, "ffn")
        w["w1"], w["w2"] = ffn_w["w_ff1"], ffn_w["w_ff2"]
        rider = comm.gather_rider(l + 1, "mix") if l + 1 < DEPTH else None
        x2, got = ffn_fwd(x1, w["g2"], w["w1"], w["w2"], [rider])
        comm.gathered(l + 1, "mix", got[0])
        saved.append(dict(x=x, pa=pa, pb=pb, pc=pc, ya=ya, yb=yb, states=states, q=q, k=k, v=v, o=o, lse=lse, x1=x1, lb=lb))
        x = x2
    dx, loss_part = loss_head(x, target)
    groups = [dict() for _ in range(DEPTH)]
    sm = {n: [None] * DEPTH for n in ("norm1_gain", "gm_v_gain", "gm_w_s", "gm_b_s", "gm_out_gain", "hg_out_gain",
                                       "mla_q_a_gain", "mla_kv_a_gain", "mla_q_gain", "mla_k_gain", "mla_out_gain",
                                       "norm2_gain")}
    dlbs = [None] * DEPTH
    halves = lambda g: g.reshape(N_CHIPS, 2, g.shape[1] // 2, g.shape[2])
    take = lambda red, f: None if red is None else f(red)
    red_mix = None
    for l in reversed(range(DEPTH)):
        w, a = lw[l], saved[l]
        (dx1p, dg2, dw1, dw2), got = ffn_bwd(a["x1"], w["g2"], w["w1"], w["w2"], dx, [take(red_mix, Reducer.swap_rider)])
        if red_mix:
            red_mix.after_swap(got[0])
        ffn_arrs = [halves(dw1), halves(dw2)]
        red_ffn = comm.reducer(ffn_arrs)
        (dx1, dya, dyb, do, dmog, dwoa, dwob, dwoc), got = outproj_bwd(
            a["ya"], a["yb"], a["o"], w["mog"], w["woa"], w["wob"], w["woc"], dx, dx1p, [take(red_ffn, Reducer.swap_rider)])
        if red_ffn:
            red_ffn.after_swap(got[0])
        (dq, dk, dv), got = attn_bwd(a["q"], a["k"], a["v"], a["o"], do, a["lse"],
                                     [take(red_mix, Reducer.scatter_rider), take(red_ffn, Reducer.scatter_rider)])
        for red, g in zip((red_mix, red_ffn), got):
            if red:
                red.after_scatter(g)
        dpc, dqag, dkvag, dqg, dkg, dwq, dwk, dwv = mla_pre_bwd(a["pc"], cos_t, sin_a, sin_b, w["qag"], w["kvag"], w["qg"],
                                                                  w["kg"], w["wq"], w["wk"], w["wv"], dq, dk, dv)
        (dpb, dlb, dhog), got = hg_bwd(a["pb"], a["lb"], w["hog"], a["states"], dyb,
                                       [take(red_mix, Reducer.share_rider), take(red_ffn, Reducer.share_rider)])
        if red_mix:
            groups[l + 1].update(zip(MIX, got[0]))
        groups[l].update(zip(FFN, got[1] if red_ffn else ffn_arrs))
        dpa, dvg, dws, dbs, dgog = gm_bwd(a["pa"], w["vg"], w["ws"], w["bs"], w["gog"], dya)
        dx, dg1, dwa, dwb, dwc = inproj_bwd(a["x"], w["g1"], w["wa"], w["wb"], w["wc"], dpa, dpb, dpc, dx1)
        dukv = jnp.concatenate([dwk.reshape(KV_LORA, MLA_HEADS, SLOT)[..., :QK_NOPE],
                                dwv.reshape(KV_LORA, MLA_HEADS, SLOT)[..., :V_DIM]], axis=-1)
        dwo = jnp.concatenate([dwoa, dwob, dwoc.reshape(MLA_HEADS, SLOT, D_MODEL)[:, :V_DIM].reshape(-1, D_MODEL)], axis=0)
        mix_arrs = [halves(_shard_cols(jnp.concatenate([dwa, dwb, dwc[:, :1952 - 1536]], axis=1))),
                    halves(_shard_cols(_unpad_slots(dwq, QK_DIM))), halves(_shard_cols(dukv.reshape(KV_LORA, -1))),
                    halves(dwo.reshape(N_CHIPS, -1, D_MODEL))]
        red_mix = comm.reducer(mix_arrs) if l > 0 else None
        if red_mix is None:
            groups[l].update(zip(MIX, mix_arrs))
        sm["norm1_gain"][l] = dg1[0]
        sm["gm_v_gain"][l] = dvg[0]
        sm["gm_w_s"][l] = dws
        sm["gm_b_s"][l] = dbs[..., 0]
        sm["gm_out_gain"][l] = dgog[0]
        sm["hg_out_gain"][l] = dhog[:, 0]
        sm["mla_q_a_gain"][l] = dqag[0]
        sm["mla_kv_a_gain"][l] = dkvag[0]
        sm["mla_q_gain"][l] = dqg[0, :QK_DIM]
        sm["mla_k_gain"][l] = dkg[0, :QK_DIM]
        sm["mla_out_gain"][l] = _unpad_slots(dmog[0], V_DIM)
        sm["norm2_gain"][l] = dg2[0]
        dlbs[l] = dlb[:, 0]
    sm["hg_lower_bound"] = [lower_bounds_bwd(small["hg_lower_bound"], jnp.stack(dlbs))]
    return loss_part, dx, groups, sm


MIX = ("w_in", "mla_w_uq", "mla_w_ukv", "w_out")
FFN = ("w_ff1", "w_ff2")
BIG = MIX + FFN
PARTS = {"mix": MIX, "ffn": FFN}
SMALL = ("norm1_gain", "gm_v_gain", "gm_w_s", "gm_b_s", "gm_out_gain", "hg_lower_bound", "hg_out_gain",
         "mla_q_a_gain", "mla_kv_a_gain", "mla_q_gain", "mla_k_gain", "mla_out_gain", "norm2_gain")
ORDER = ("norm1_gain", "w_in", "gm_v_gain", "gm_w_s", "gm_b_s", "gm_out_gain", "hg_lower_bound", "hg_out_gain",
         "mla_q_a_gain", "mla_w_uq", "mla_kv_a_gain", "mla_w_ukv", "mla_q_gain", "mla_k_gain", "mla_out_gain",
         "w_out", "norm2_gain", "w_ff1", "w_ff2")
PACK_ROWS = 320


def _pack(pieces):
    flat = jnp.concatenate([a.reshape(-1) for a in pieces])
    total = 2 * N_CHIPS * PACK_ROWS * SLOT
    return jnp.pad(flat, (0, total - flat.shape[0]))


def _unpack(flat, shapes):
    out, off = [], 0
    for sh in shapes:
        size = 1
        for d in sh:
            size *= d
        out.append(flat[off:off + size].reshape(sh))
        off += size
    return out


class ChipComm:
    def __init__(self, shards):
        self.shards = shards
        self.full = {}

    def gather_rider(self, l, part):
        return gather_rider([self.shards[n][l].astype(MXU_DTYPE).reshape(4, self.shards[n].shape[1] // 4, -1)
                             for n in PARTS[part]])

    def gathered(self, l, part, outs):
        if outs is not None:
            self.full[l, part] = {n: o.reshape((N_CHIPS,) + self.shards[n].shape[1:]) for n, o in zip(PARTS[part], outs)}

    def part(self, l, part):
        if (l, part) not in self.full:
            self.gathered(l, part, run_rider(self.gather_rider(l, part), "gather_weights"))
        return self.full[l, part]

    def reducer(self, arrs):
        return Reducer(arrs)


def kernel(x, positions, norm1_gain, w_in, gm_v_gain, gm_w_s, gm_b_s, gm_out_gain, hg_lower_bound, hg_out_gain, mla_q_a_gain, mla_w_uq, mla_kv_a_gain, mla_w_ukv, mla_q_gain, mla_k_gain, mla_out_gain, w_out, norm2_gain, w_ff1, w_ff2, loss_target, m_norm1_gain, m_w_in, m_gm_v_gain, m_gm_w_s, m_gm_b_s, m_gm_out_gain, m_hg_lower_bound, m_hg_out_gain, m_mla_q_a_gain, m_mla_w_uq, m_mla_kv_a_gain, m_mla_w_ukv, m_mla_q_gain, m_mla_k_gain, m_mla_out_gain, m_w_out, m_norm2_gain, m_w_ff1, m_w_ff2, v_norm1_gain, v_w_in, v_gm_v_gain, v_gm_w_s, v_gm_b_s, v_gm_out_gain, v_hg_lower_bound, v_hg_out_gain, v_mla_q_a_gain, v_mla_w_uq, v_mla_kv_a_gain, v_mla_w_ukv, v_mla_q_gain, v_mla_k_gain, v_mla_out_gain, v_w_out, v_norm2_gain, v_w_ff1, v_w_ff2):
    given = dict(locals())
    weights = {n: given[n] for n in ORDER}
    moms = {n: given["m_" + n] for n in ORDER}
    vars_ = {n: given["v_" + n] for n in ORDER}
    s, d = x.shape[1], x.shape[2]

    small = {n: weights[n] for n in SMALL}
    comm = ChipComm({n: weights[n] for n in BIG})
    loss_part, dx, groups, small_g = local_step(x.reshape(s, d), positions, loss_target.reshape(s, d), small, comm)
    loss = lax.psum(loss_part[0, 0], ("x", "y", "c"))

    grads, delta, new_m, new_v = {}, {}, {}, {}
    flat2 = lambda a: a.reshape(-1, a.shape[-1])

    def update(n, riders=()):
        outs, got = adamw(flat2(weights[n]), flat2(grads[n]), flat2(moms[n]), flat2(vars_[n]), "adamw_" + n, riders)
        delta[n], new_m[n], new_v[n] = [o.reshape(weights[n].shape) for o in outs]
        return got

    stacked = lambda n: jnp.stack([groups[l][n].reshape(weights[n].shape[1:]) for l in range(DEPTH)])
    pack_g = _pack([piece for n in SMALL for piece in small_g[n]]).reshape(N_CHIPS, 2, PACK_ROWS, SLOT)
    last = Reducer([groups[0][n] for n in MIX] + [pack_g])
    grads["w_ff1"], grads["w_ff2"] = stacked("w_ff1"), stacked("w_ff2")
    last.after_swap(update("w_ff1", [last.swap_rider()])[0])
    last.after_scatter(update("w_ff2", [last.scatter_rider()])[0])
    reduced = run_rider(last.share_rider(), "share_halves")
    groups[0].update(zip(MIX, reduced[:-1]))
    pack_full = run_rider(gather_rider([reduced[-1].reshape(4, PACK_ROWS // 2, SLOT)]), "gather_small")[0].reshape(-1)
    for n in MIX:
        grads[n] = stacked(n)
        update(n)
    grads.update(zip(SMALL, _unpack(pack_full, [weights[n].shape for n in SMALL])))
    for n in SMALL:
        update(n)

    return (loss, dx.reshape(x.shape), *[grads[n] for n in ORDER], *[delta[n] for n in ORDER],
            *[new_m[n] for n in ORDER], *[new_v[n] for n in ORDER])
```

```python
import functools

import jax
import jax.numpy as jnp
from jax import lax
from jax.experimental import pallas as pl
from jax.experimental.pallas import tpu as pltpu

F32 = jnp.float32
BF16 = jnp.bfloat16
MXU_DTYPE = BF16

D_MODEL = 1024
DEPTH = 4
CHUNK = 128
HG_CHUNK = 128
HG_CHUNKS = 2
GM_CHUNKS = 4
EPS = 1e-6
HEAD64 = 64
MLA_HEADS = 8
QK_NOPE = 64
QK_ROPE = 32
QK_DIM = 96
V_DIM = 64
Q_LORA = 256
KV_LORA = 128
SLOT = 128
ROPE_THETA = 10000.0
D_FF_SHARD = 1024
N_CHIPS = 4

ADAM_LR = 0.001
ADAM_B1 = 0.9
ADAM_B2 = 0.999
ADAM_EPS = 1e-08
ADAM_WD = 0.01
ADAM_STEP = 10

TM = 512
TM_FFN = 512
TQ = 256
ATT_HEADS_PER_STEP = 4
ATT_WIDE = 512
VMEM_LIMIT = 56 * 1024 * 1024

NN = (((1,), (0,)), ((), ()))
NT = (((1,), (1,)), ((), ()))
TN = (((0,), (0,)), ((), ()))
BNN = (((2,), (1,)), ((0,), (0,)))
BNT = (((2,), (2,)), ((0,), (0,)))
BTN = (((1,), (1,)), ((0,), (0,)))


def _dot(a, b, dims):
    return lax.dot_general(a.astype(MXU_DTYPE), b.astype(MXU_DTYPE), dims, preferred_element_type=F32)


def _hdot(a, b, dims=NN):
    return lax.dot_general(a, b, dims, precision=lax.Precision.HIGHEST, preferred_element_type=F32)


def _make_ad(dims, da_dims, da_swap, db_dims, db_swap):
    @jax.custom_vjp
    def f(a, b):
        return _dot(a, b, dims)

    def fwd(a, b):
        return _dot(a, b, dims), (a, b)

    def bwd(res, g):
        a, b = res
        da = _dot(b, g, da_dims) if da_swap else _dot(g, b, da_dims)
        db = _dot(g, a, db_dims) if db_swap else _dot(a, g, db_dims)
        return da, db

    f.defvjp(fwd, bwd)
    return f


@functools.partial(jax.custom_vjp, nondiff_argnums=(1,))
def _roll_ad(x, shift):
    return pltpu.roll(x, shift, 1)


def _roll_ad_fwd(x, shift):
    return pltpu.roll(x, shift, 1), None


def _roll_ad_bwd(shift, _, g):
    return (pltpu.roll(g, (g.shape[1] - shift) % g.shape[1], 1),)


_roll_ad.defvjp(_roll_ad_fwd, _roll_ad_bwd)


class _Ops:
    pass


PLAIN = _Ops()
PLAIN.mm = lambda a, b: _dot(a, b, NN)
PLAIN.bmm = lambda a, b: _dot(a, b, BNN)
PLAIN.bmm_nt = lambda a, b: _dot(a, b, BNT)
PLAIN.bmm_tn = lambda a, b: _dot(a, b, BTN)
PLAIN.roll = lambda x, s: pltpu.roll(x, s, 1)

AD = _Ops()
AD.mm = _make_ad(NN, NT, False, TN, False)
AD.bmm = _make_ad(BNN, BNT, False, BTN, False)
AD.bmm_nt = _make_ad(BNT, BNN, False, BTN, True)
AD.bmm_tn = _make_ad(BTN, BNT, True, BNN, False)
AD.roll = _roll_ad


def _sigmoid(x):
    return jax.nn.sigmoid(x)


def _gelu(x):
    return 0.5 * x * (1.0 + jnp.tanh(0.7978845608028654 * (x + 0.044715 * (x * x * x))))


def _rms(x, g):
    return x * lax.rsqrt(jnp.mean(x * x, axis=-1, keepdims=True) + EPS) * g


def _head_masks256():
    lane = lax.broadcasted_iota(jnp.int32, (1, 4 * HEAD64), 1)
    return [(jnp.right_shift(lane, 6) == h).astype(F32) for h in range(4)]


def _headnorm256(x, g):
    ms = jnp.zeros_like(x)
    sq = x * x
    for m in _head_masks256():
        ms = ms + m * (jnp.sum(sq * m, axis=-1, keepdims=True) * (1.0 / HEAD64))
    return x * lax.rsqrt(ms + EPS) * g


def _slot_norm(x, g, n):
    return x * lax.rsqrt(jnp.sum(x * x, axis=-1, keepdims=True) * (1.0 / n) + EPS) * g


def _rope(ops, x, cos_t, sin_a, sin_b):
    return x * cos_t + ops.roll(x, SLOT - QK_ROPE // 2) * sin_a + ops.roll(x, QK_ROPE // 2) * sin_b


def _inproj(ops, x, g1, wa, wb, wc):
    h = _rms(x, g1)
    return ops.mm(h, wa), ops.mm(h, wb), ops.mm(h, wc)


def _gm_chunk(ops, ur, vr, vg, ws4, bs, og):
    c = ur.shape[0]
    masks = _head_masks256()
    mh = jnp.concatenate([m[None] for m in masks], axis=0)
    u = _gelu(ur)
    v = _headnorm256(_gelu(vr), vg)
    t = lax.broadcasted_iota(jnp.int32, (c, c), 0)
    s = lax.broadcasted_iota(jnp.int32, (c, c), 1)
    w = jnp.where((t >= s)[None], ws4, 0.0)
    y = jnp.sum(ops.bmm(w, v[None] * mh), axis=0)
    for h in range(4):
        y = y + bs[h] * masks[h]
    return _headnorm256(u * y, og)


def _hg_chunk(ops, st, qr, fr, ir, gr, lb, og):
    c, n = qr.shape
    nh = n // HEAD64
    heads = lambda x: x.reshape(nh, HEAD64, x.shape[-1])
    tr = lambda x: heads(x.T)
    lb4, og4 = heads(lb), heads(og)
    qx = tr(qr)
    q = qx * _sigmoid(qx)
    f = lb4 + (1.0 - lb4) * _sigmoid(tr(fr))
    k = 1.0 - f
    logf = jnp.log(f)
    v = tr(ir)
    gx = tr(gr)
    s = lax.broadcasted_iota(jnp.int32, (c, c), 0)
    t = lax.broadcasted_iota(jnp.int32, (c, c), 1)
    tl = lax.broadcasted_iota(jnp.int32, (1, c), 1).reshape(1, 1, c)
    b2 = _hdot(logf.reshape(n, c), (s <= t).astype(F32))
    b = heads(b2)
    btot = jnp.sum(logf, axis=2, keepdims=True)
    inter = ops.bmm_tn(st, q * jnp.exp(b))
    p4 = jnp.zeros((nh, c, c), F32)
    tt, ss = s, t
    lg = c.bit_length() - 2
    while lg >= 0:
        m = 1 << lg
        bnd = jnp.left_shift(jnp.right_shift(t, lg + 1), lg + 1) + (m - 1)
        r = heads(_hdot(b2, (s == bnd).astype(F32)))
        right = jnp.bitwise_and(jnp.right_shift(tl, lg), 1) == 1
        qe = jnp.where(right, q * jnp.exp(jnp.where(right, b - r, 0.0)), 0.0)
        ke = jnp.where(right, 0.0, k * jnp.exp(jnp.where(right, 0.0, r - b)))
        lm = ((jnp.right_shift(tt, lg + 1) == jnp.right_shift(ss, lg + 1))
              & (jnp.bitwise_and(jnp.right_shift(tt, lg), 1) == 1)
              & (jnp.bitwise_and(jnp.right_shift(ss, lg), 1) == 0))
        p4 = jnp.where(lm[None], ops.bmm_tn(qe, ke), p4)
        lg -= 1
    intra = ops.bmm_nt(v, p4)
    o = inter + intra + jnp.sum(q * k, axis=1, keepdims=True) * v
    st_new = st * jnp.exp(btot) + ops.bmm_nt(k * jnp.exp(btot - b), v)
    y = o * lax.rsqrt(jnp.mean(o * o, axis=1, keepdims=True) + EPS) * og4 * (gx * _sigmoid(gx))
    return st_new, y.reshape(n, c).T


def _mla_pre(ops, cq, ckv, kpe, cos_t, sin_a, sin_b, qag, kvag, qg, kg, wq, wk, wv):
    cqn = _rms(cq, qag)
    ckvn = _rms(ckv, kvag)
    kper = ops.roll(kpe, QK_NOPE)
    qs, ks, vs = [], [], []
    for h in range(MLA_HEADS):
        qh = _slot_norm(ops.mm(cqn, wq[h]), qg, QK_DIM)
        qs.append(_rope(ops, qh, cos_t, sin_a, sin_b))
        kh = _slot_norm(ops.mm(ckvn, wk[h]) + kper, kg, QK_DIM)
        ks.append(_rope(ops, kh, cos_t, sin_a, sin_b))
        vs.append(ops.mm(ckvn, wv[h]))
    return qs, ks, vs


def _outproj(ops, x, ya, yb, o, mog, woa, wob, woc):
    acc = x + ops.mm(ya, woa) + ops.mm(yb, wob)
    for h in range(MLA_HEADS):
        acc = acc + ops.mm(_slot_norm(o[h], mog[h], V_DIM), woc[h])
    return acc


def _ffn_part(ops, x1, g2, w1p, w2p):
    a = ops.mm(_rms(x1, g2), w1p)
    r = jnp.maximum(a, 0.0)
    return ops.mm(r * r, w2p)


def _lower_bounds(r0, r1, r2, r3):
    mx = jnp.maximum(jnp.maximum(r0, r1), jnp.maximum(r2, r3))
    e0, e1, e2, e3 = jnp.exp(r0 - mx), jnp.exp(r1 - mx), jnp.exp(r2 - mx), jnp.exp(r3 - mx)
    inv = 1.0 / (e0 + e1 + e2 + e3)
    s1, s2, s3 = e1 * inv, e2 * inv, e3 * inv
    return jnp.zeros_like(r0), s1, s1 + s2, s1 + s2 + s3


def _cp(sem):
    return pltpu.CompilerParams(dimension_semantics=sem, vmem_limit_bytes=VMEM_LIMIT)


def _rows(tm, n):
    return pl.BlockSpec((tm, n), lambda i: (i, 0))


def _full(a):
    nd = len(a.shape)
    return pl.BlockSpec(a.shape, lambda *_: (0,) * nd, pipeline_mode=pl.Buffered(1))


def _sds(shape, dtype=F32):
    return jax.ShapeDtypeStruct(shape, dtype)


def _acc(ref, val, first):
    @pl.when(first)
    def _():
        ref[...] = val

    @pl.when(jnp.logical_not(first))
    def _():
        ref[...] = ref[...] + val


def _f32(ref):
    return ref[...].astype(F32)


MESH = pl.DeviceIdType.MESH
ANY = pl.BlockSpec(memory_space=pl.ANY)


class Rider:
    def __init__(self, arrays, out_shapes, aliases, sems, start, finish):
        self.arrays, self.out_shapes, self.aliases, self.sems = list(arrays), list(out_shapes), dict(aliases), list(sems)
        self.start, self.finish = start, finish


def run_rider(rider, name):
    n_in, n_out = len(rider.arrays), len(rider.out_shapes)

    def body(*refs):
        ins, outs, sems = refs[:n_in], refs[n_in:n_in + n_out], refs[n_in + n_out:]
        rider.start(ins, outs, sems)
        rider.finish(ins, outs, sems)

    return pl.pallas_call(
        body, name=name, in_specs=[ANY] * n_in, out_specs=[ANY] * n_out, out_shape=rider.out_shapes,
        input_output_aliases=rider.aliases, scratch_shapes=rider.sems,
    )(*rider.arrays)


def _merge_riders(riders):
    bounds, a0, o0, s0 = [], 0, 0, 0
    for r in riders:
        bounds.append((a0, o0, s0))
        a0, o0, s0 = a0 + len(r.arrays), o0 + len(r.out_shapes), s0 + len(r.sems)

    def part(k, ins, outs, sems):
        a, o, s = bounds[k]
        r = riders[k]
        return ins[a:a + len(r.arrays)], outs[o:o + len(r.out_shapes)], sems[s:s + len(r.sems)]

    return Rider(
        [x for r in riders for x in r.arrays], [x for r in riders for x in r.out_shapes],
        {bounds[k][0] + i: bounds[k][1] + o for k, r in enumerate(riders) for i, o in r.aliases.items()},
        [x for r in riders for x in r.sems],
        lambda *refs: [r.start(*part(k, *refs)) for k, r in enumerate(riders)],
        lambda *refs: [r.finish(*part(k, *refs)) for k, r in enumerate(riders)])


def _ride(compute, riders, *, name, grid, in_specs, out_specs, out_shape, operands, scratch_shapes=(), sem=None):
    single = not isinstance(out_shape, (list, tuple))
    if single:
        out_specs, out_shape = [out_specs], [out_shape]
    live = [r for r in riders if r is not None]
    if not live:
        res = pl.pallas_call(compute, name=name, grid=grid, in_specs=in_specs, out_specs=out_specs, out_shape=out_shape,
                             scratch_shapes=list(scratch_shapes), compiler_params=_cp(sem))(*operands)
        return (res[0] if single else res), [None] * len(riders)
    rider = live[0] if len(live) == 1 else _merge_riders(live)
    n_in, n_out, n_s = len(in_specs), len(out_specs), len(scratch_shapes)
    r_in, r_out = len(rider.arrays), len(rider.out_shapes)

    def body(*refs):
        ins, rins = refs[:n_in], refs[n_in:n_in + r_in]
        outs = refs[n_in + r_in:n_in + r_in + n_out]
        routs = refs[n_in + r_in + n_out:n_in + r_in + n_out + r_out]
        scr = refs[n_in + r_in + n_out + r_out:n_in + r_in + n_out + r_out + n_s]
        rsems = refs[n_in + r_in + n_out + r_out + n_s:]
        first = functools.reduce(jnp.logical_and, [pl.program_id(a) == 0 for a in range(len(grid))])
        last = functools.reduce(jnp.logical_and, [pl.program_id(a) == grid[a] - 1 for a in range(len(grid))])

        @pl.when(first)
        def _():
            rider.start(rins, routs, rsems)

        compute(*ins, *outs, *scr)

        @pl.when(last)
        def _():
            rider.finish(rins, routs, rsems)

    res = pl.pallas_call(
        body, name=name, grid=grid, in_specs=list(in_specs) + [ANY] * r_in, out_specs=list(out_specs) + [ANY] * r_out,
        out_shape=list(out_shape) + rider.out_shapes,
        input_output_aliases={n_in + k: n_out + v for k, v in rider.aliases.items()},
        scratch_shapes=list(scratch_shapes) + rider.sems, compiler_params=_cp(("arbitrary",) * len(grid)),
    )(*operands, *rider.arrays)
    main, rest, per_rider = res[:n_out], list(res[n_out:]), []
    for r in riders:
        per_rider.append(None if r is None else [rest.pop(0) for _ in r.out_shapes])
    return (main[0] if single else main), per_rider


def inproj_fwd(x, g1, wa, wb, wc):
    s, d = x.shape

    def body(x_ref, g_ref, wa_ref, wb_ref, wc_ref, pa_ref, pb_ref, pc_ref):
        pa, pb, pc = _inproj(PLAIN, x_ref[...], g_ref[...], wa_ref[...], wb_ref[...], wc_ref[...])
        pa_ref[...] = pa
        pb_ref[...] = pb
        pc_ref[...] = pc

    return pl.pallas_call(
        body, name="inproj_fwd", grid=(s // TM,),
        in_specs=[_rows(TM, d), _full(g1), _full(wa), _full(wb), _full(wc)],
        out_specs=[_rows(TM, wa.shape[1]), _rows(TM, wb.shape[1]), _rows(TM, wc.shape[1])],
        out_shape=[_sds((s, wa.shape[1])), _sds((s, wb.shape[1])), _sds((s, wc.shape[1]))],
        compiler_params=_cp(("parallel",)),
    )(x, g1, wa, wb, wc)


def inproj_bwd(x, g1, wa, wb, wc, dpa, dpb, dpc, dres):
    s, d = x.shape

    def body(x_ref, g_ref, wa_ref, wb_ref, wc_ref, dpa_ref, dpb_ref, dpc_ref, dres_ref,
             dx_ref, dg_ref, dwa_ref, dwb_ref, dwc_ref):
        first = pl.program_id(0) == 0
        _, vjp = jax.vjp(functools.partial(_inproj, AD), x_ref[...], g_ref[...],
                         _f32(wa_ref), _f32(wb_ref), _f32(wc_ref))
        dx, dg, dwa, dwb, dwc = vjp((dpa_ref[...], dpb_ref[...], dpc_ref[...]))
        dx_ref[...] = dx + dres_ref[...]
        _acc(dg_ref, dg, first)
        _acc(dwa_ref, dwa, first)
        _acc(dwb_ref, dwb, first)
        _acc(dwc_ref, dwc, first)

    return pl.pallas_call(
        body, name="inproj_bwd", grid=(s // TM,),
        in_specs=[_rows(TM, d), _full(g1), _full(wa), _full(wb), _full(wc),
                  _rows(TM, wa.shape[1]), _rows(TM, wb.shape[1]), _rows(TM, wc.shape[1]), _rows(TM, d)],
        out_specs=[_rows(TM, d), _full(g1), _full(wa), _full(wb), _full(wc)],
        out_shape=[_sds((s, d)), _sds(g1.shape), _sds(wa.shape), _sds(wb.shape), _sds(wc.shape)],
        compiler_params=_cp(("arbitrary",)),
    )(x, g1, wa, wb, wc, dpa, dpb, dpc, dres)


def gm_fwd(pa, vg, ws4, bs, og):
    s = pa.shape[0]
    w = pa.shape[1] // 2

    def body(pa_ref, vg_ref, ws_ref, bs_ref, og_ref, ya_ref):
        bsl = [bs_ref[h] for h in range(4)]
        for j in range(GM_CHUNKS):
            rows = slice(j * CHUNK, (j + 1) * CHUNK)
            ya_ref[rows, :] = _gm_chunk(PLAIN, pa_ref[rows, 0:w], pa_ref[rows, w:2 * w], vg_ref[...], ws_ref[...], bsl,
                                        og_ref[...])

    tm = GM_CHUNKS * CHUNK
    return pl.pallas_call(
        body, name="gm_fwd", grid=(s // tm,),
        in_specs=[_rows(tm, 2 * w), _full(vg), _full(ws4), _full(bs), _full(og)],
        out_specs=_rows(tm, w), out_shape=_sds((s, w)),
        compiler_params=_cp(("parallel",)),
    )(pa, vg, ws4, bs, og)


def gm_bwd(pa, vg, ws4, bs, og, dya):
    s = pa.shape[0]
    w = pa.shape[1] // 2

    def body(pa_ref, vg_ref, ws_ref, bs_ref, og_ref, dya_ref, dpa_ref, dvg_ref, dws_ref, dbs_ref, dog_ref):
        first = pl.program_id(0) == 0
        bsl = [bs_ref[h] for h in range(4)]
        tot = None
        for j in range(GM_CHUNKS):
            rows = slice(j * CHUNK, (j + 1) * CHUNK)
            _, vjp = jax.vjp(functools.partial(_gm_chunk, AD), pa_ref[rows, 0:w], pa_ref[rows, w:2 * w],
                             vg_ref[...], ws_ref[...], bsl, og_ref[...])
            du, dv, *dws = vjp(dya_ref[rows, :])
            dpa_ref[rows, 0:w] = du
            dpa_ref[rows, w:2 * w] = dv
            tot = dws if tot is None else jax.tree.map(jnp.add, tot, dws)
        dvg, dws, dbs, dog = tot
        _acc(dvg_ref, dvg, first)
        _acc(dws_ref, dws, first)
        _acc(dog_ref, dog, first)
        for h in range(4):
            _acc(dbs_ref.at[h], dbs[h], first)

    tm = GM_CHUNKS * CHUNK
    return pl.pallas_call(
        body, name="gm_bwd", grid=(s // tm,),
        in_specs=[_rows(tm, 2 * w), _full(vg), _full(ws4), _full(bs), _full(og), _rows(tm, w)],
        out_specs=[_rows(tm, 2 * w), _full(vg), _full(ws4), _full(bs), _full(og)],
        out_shape=[_sds((s, 2 * w)), _sds(vg.shape), _sds(ws4.shape), _sds(bs.shape), _sds(og.shape)],
        compiler_params=_cp(("arbitrary",)),
    )(pa, vg, ws4, bs, og, dya)


def hg_fwd(pb, lb, og):
    s = pb.shape[0]
    w = pb.shape[1] // 4
    tm = HG_CHUNKS * HG_CHUNK
    st_shape = (w // HEAD64, HEAD64, HEAD64)

    def body(pb_ref, lb_ref, og_ref, yb_ref, states_ref, st_ref):
        @pl.when(pl.program_id(0) == 0)
        def _():
            st_ref[...] = jnp.zeros_like(st_ref)

        st = st_ref[...]
        for j in range(HG_CHUNKS):
            rows = slice(j * HG_CHUNK, (j + 1) * HG_CHUNK)
            states_ref[j] = st
            st, y = _hg_chunk(PLAIN, st, pb_ref[rows, 0:w], pb_ref[rows, w:2 * w], pb_ref[rows, 2 * w:3 * w],
                              pb_ref[rows, 3 * w:4 * w], lb_ref[...], og_ref[...])
            yb_ref[rows, :] = y
        st_ref[...] = st

    return pl.pallas_call(
        body, name="hg_fwd", grid=(s // tm,),
        in_specs=[_rows(tm, 4 * w), _full(lb), _full(og)],
        out_specs=[_rows(tm, w), pl.BlockSpec((HG_CHUNKS,) + st_shape, lambda i: (i, 0, 0, 0))],
        out_shape=[_sds((s, w)), _sds((s // HG_CHUNK,) + st_shape)],
        scratch_shapes=[pltpu.VMEM(st_shape, F32)],
        compiler_params=_cp(("arbitrary",)),
    )(pb, lb, og)


def hg_bwd(pb, lb, og, states, dyb, riders=()):
    s = pb.shape[0]
    w = pb.shape[1] // 4
    tm = HG_CHUNKS * HG_CHUNK
    nc = s // tm
    st_shape = (w // HEAD64, HEAD64, HEAD64)

    def body(pb_ref, lb_ref, og_ref, states_ref, dyb_ref, dpb_ref, dlb_ref, dog_ref, dst_ref):
        first = pl.program_id(0) == 0

        @pl.when(first)
        def _():
            dst_ref[...] = jnp.zeros_like(dst_ref)

        dst, dlb, dog = dst_ref[...], None, None
        for j in reversed(range(HG_CHUNKS)):
            rows = slice(j * HG_CHUNK, (j + 1) * HG_CHUNK)
            _, vjp = jax.vjp(functools.partial(_hg_chunk, AD), states_ref[j], pb_ref[rows, 0:w], pb_ref[rows, w:2 * w],
                             pb_ref[rows, 2 * w:3 * w], pb_ref[rows, 3 * w:4 * w], lb_ref[...], og_ref[...])
            dst, dq, df, di, dg, dlb_j, dog_j = vjp((dst, dyb_ref[rows, :]))
            dpb_ref[rows, 0:w] = dq
            dpb_ref[rows, w:2 * w] = df
            dpb_ref[rows, 2 * w:3 * w] = di
            dpb_ref[rows, 3 * w:4 * w] = dg
            dlb = dlb_j if dlb is None else dlb + dlb_j
            dog = dog_j if dog is None else dog + dog_j
        dst_ref[...] = dst
        _acc(dlb_ref, dlb, first)
        _acc(dog_ref, dog, first)

    rev = lambda i: (nc - 1 - i, 0)
    return _ride(
        body, riders, name="hg_bwd", grid=(nc,),
        in_specs=[pl.BlockSpec((tm, 4 * w), rev), _full(lb), _full(og),
                  pl.BlockSpec((HG_CHUNKS,) + st_shape, lambda i: (nc - 1 - i, 0, 0, 0)), pl.BlockSpec((tm, w), rev)],
        out_specs=[pl.BlockSpec((tm, 4 * w), rev), _full(lb), _full(og)],
        out_shape=[_sds((s, 4 * w)), _sds(lb.shape), _sds(og.shape)],
        scratch_shapes=[pltpu.VMEM(st_shape, F32)],
        operands=(pb, lb, og, states, dyb), sem=("arbitrary",))


def lower_bounds_fwd(hlb):
    def body(h_ref, o_ref):
        outs = _lower_bounds(*[h_ref[pl.ds(i, 1), :] for i in range(DEPTH)])
        for i in range(DEPTH):
            o_ref[pl.ds(i, 1), :] = outs[i]

    return pl.pallas_call(body, name="lower_bounds_fwd", out_shape=_sds(hlb.shape))(hlb)


def lower_bounds_bwd(hlb, dlbs):
    def body(h_ref, d_ref, o_ref):
        _, vjp = jax.vjp(_lower_bounds, *[h_ref[pl.ds(i, 1), :] for i in range(DEPTH)])
        outs = vjp(tuple(d_ref[pl.ds(i, 1), :] for i in range(DEPTH)))
        for i in range(DEPTH):
            o_ref[pl.ds(i, 1), :] = outs[i]

    return pl.pallas_call(body, name="lower_bounds_bwd", out_shape=_sds(hlb.shape))(hlb, dlbs)


def _mla_pre_args(pc_ref, cos_ref, sa_ref, sb_ref, qag_ref, kvag_ref, qg_ref, kg_ref, wq_ref, wk_ref, wv_ref, cast):
    sl = lambda h: slice(h * SLOT, (h + 1) * SLOT)
    ld = (lambda r, h: r[:, sl(h)].astype(F32)) if cast else (lambda r, h: r[:, sl(h)])
    diff = (pc_ref[:, 0:Q_LORA], pc_ref[:, Q_LORA:Q_LORA + KV_LORA], pc_ref[:, Q_LORA + KV_LORA:Q_LORA + 2 * KV_LORA],
            qag_ref[...], kvag_ref[...], qg_ref[...], kg_ref[...],
            [ld(wq_ref, h) for h in range(MLA_HEADS)], [ld(wk_ref, h) for h in range(MLA_HEADS)],
            [ld(wv_ref, h) for h in range(MLA_HEADS)])
    tables = (cos_ref[...], sa_ref[...], sb_ref[...])
    return diff, tables


def _mla_pre_fn(ops, tables, cq, ckv, kpe, qag, kvag, qg, kg, wq, wk, wv):
    return _mla_pre(ops, cq, ckv, kpe, *tables, qag, kvag, qg, kg, wq, wk, wv)


def mla_pre_fwd(pc, cos_t, sin_a, sin_b, qag, kvag, qg, kg, wq, wk, wv):
    s = pc.shape[0]
    hw = MLA_HEADS * SLOT

    def body(pc_ref, cos_ref, sa_ref, sb_ref, qag_ref, kvag_ref, qg_ref, kg_ref, wq_ref, wk_ref, wv_ref,
             q_ref, k_ref, v_ref):
        diff, tables = _mla_pre_args(pc_ref, cos_ref, sa_ref, sb_ref, qag_ref, kvag_ref, qg_ref, kg_ref,
                                     wq_ref, wk_ref, wv_ref, False)
        qs, ks, vs = _mla_pre_fn(PLAIN, tables, *diff)
        ones_lane = (lax.broadcasted_iota(jnp.int32, (1, SLOT), 1) == V_DIM).astype(F32)
        for h in range(MLA_HEADS):
            q_ref[:, h * SLOT:(h + 1) * SLOT] = qs[h].astype(q_ref.dtype)
            k_ref[:, h * SLOT:(h + 1) * SLOT] = ks[h].astype(k_ref.dtype)
            v_ref[:, h * SLOT:(h + 1) * SLOT] = (vs[h] + ones_lane).astype(v_ref.dtype)

    return pl.pallas_call(
        body, name="mla_pre_fwd", grid=(s // TM,),
        in_specs=[_rows(TM, pc.shape[1]), _rows(TM, SLOT), _rows(TM, SLOT), _rows(TM, SLOT),
                  _full(qag), _full(kvag), _full(qg), _full(kg), _full(wq), _full(wk), _full(wv)],
        out_specs=[_rows(TM, hw)] * 3, out_shape=[_sds((s, hw), MXU_DTYPE)] * 3,
        compiler_params=_cp(("parallel",)),
    )(pc, cos_t, sin_a, sin_b, qag, kvag, qg, kg, wq, wk, wv)


def mla_pre_bwd(pc, cos_t, sin_a, sin_b, qag, kvag, qg, kg, wq, wk, wv, dq, dk, dv):
    s = pc.shape[0]
    hw = MLA_HEADS * SLOT

    def body(pc_ref, cos_ref, sa_ref, sb_ref, qag_ref, kvag_ref, qg_ref, kg_ref, wq_ref, wk_ref, wv_ref,
             dq_ref, dk_ref, dv_ref, dpc_ref, dqag_ref, dkvag_ref, dqg_ref, dkg_ref, dwq_ref, dwk_ref, dwv_ref):
        first = pl.program_id(0) == 0
        diff, tables = _mla_pre_args(pc_ref, cos_ref, sa_ref, sb_ref, qag_ref, kvag_ref, qg_ref, kg_ref,
                                     wq_ref, wk_ref, wv_ref, True)
        _, vjp = jax.vjp(functools.partial(_mla_pre_fn, AD, tables), *diff)
        sl = lambda h: slice(h * SLOT, (h + 1) * SLOT)
        cot = ([dq_ref[:, sl(h)] for h in range(MLA_HEADS)], [dk_ref[:, sl(h)] for h in range(MLA_HEADS)],
               [dv_ref[:, sl(h)] for h in range(MLA_HEADS)])
        dcq, dckv, dkpe, dqag, dkvag, dqg, dkg, dwq, dwk, dwv = vjp(cot)
        dpc_ref[:, 0:Q_LORA] = dcq
        dpc_ref[:, Q_LORA:Q_LORA + KV_LORA] = dckv
        dpc_ref[:, Q_LORA + KV_LORA:Q_LORA + 2 * KV_LORA] = dkpe
        _acc(dqag_ref, dqag, first)
        _acc(dkvag_ref, dkvag, first)
        _acc(dqg_ref, dqg, first)
        _acc(dkg_ref, dkg, first)
        for h in range(MLA_HEADS):
            _acc(dwq_ref.at[:, sl(h)], dwq[h], first)
            _acc(dwk_ref.at[:, sl(h)], dwk[h], first)
            _acc(dwv_ref.at[:, sl(h)], dwv[h], first)

    return pl.pallas_call(
        body, name="mla_pre_bwd", grid=(s // TM,),
        in_specs=[_rows(TM, pc.shape[1]), _rows(TM, SLOT), _rows(TM, SLOT), _rows(TM, SLOT),
                  _full(qag), _full(kvag), _full(qg), _full(kg), _full(wq), _full(wk), _full(wv),
                  _rows(TM, hw), _rows(TM, hw), _rows(TM, hw)],
        out_specs=[_rows(TM, pc.shape[1]), _full(qag), _full(kvag), _full(qg), _full(kg),
                   _full(wq), _full(wk), _full(wv)],
        out_shape=[_sds(pc.shape), _sds(qag.shape), _sds(kvag.shape), _sds(qg.shape), _sds(kg.shape),
                   _sds(wq.shape), _sds(wk.shape), _sds(wv.shape)],
        compiler_params=_cp(("arbitrary",)),
    )(pc, cos_t, sin_a, sin_b, qag, kvag, qg, kg, wq, wk, wv, dq, dk, dv)


ATT_SCALE = QK_DIM ** -0.5
NEG_BIG = -1e30


def attn_fwd(q, k, v, riders=()):
    s = q.shape[0]
    nq = s // TQ
    hp = ATT_HEADS_PER_STEP
    sl = lambda j: slice(j * SLOT, (j + 1) * SLOT)

    wide = ATT_WIDE // TQ

    def body(q_ref, k_ref, v_ref, o_ref, lse_ref):
        qi = pl.program_id(1)
        lane = lax.broadcasted_iota(jnp.int32, (1, SLOT), 1)
        qs = [q_ref[:, sl(j)] for j in range(hp)]

        def step(ki, carry, n_tiles, masked):
            rk = pl.ds(pl.multiple_of(ki * TQ, TQ), n_tiles * TQ)
            if masked:
                row = lax.broadcasted_iota(jnp.int32, (TQ, n_tiles * TQ), 0) + (n_tiles - 1) * TQ
                col = lax.broadcasted_iota(jnp.int32, (TQ, n_tiles * TQ), 1)
            out = []
            for j in range(hp):
                m, acc = carry[j]
                sc = _dot(qs[j], k_ref[rk, sl(j)], NT) * ATT_SCALE
                if masked:
                    sc = jnp.where(row >= col, sc, NEG_BIG)
                m_new = jnp.maximum(m, jnp.max(sc, axis=-1, keepdims=True))
                acc = jnp.exp(m - m_new) * acc + _dot(jnp.exp(sc - m_new), v_ref[rk, sl(j)], NN)
                out.append((m_new, acc))
            return tuple(out)

        def tail_single(cr):
            cr = lax.fori_loop(n_wide * wide, qi, lambda ki, c: step(ki, c, 1, False), cr)
            return step(qi, cr, 1, True)

        n_wide = qi // wide
        init = tuple((jnp.full((TQ, 1), NEG_BIG, F32), jnp.zeros((TQ, SLOT), F32)) for _ in range(hp))
        carry = lax.fori_loop(0, n_wide, lambda kw, cr: step(kw * wide, cr, wide, False), init)
        carry = lax.cond(qi % wide == wide - 1, lambda cr: step(qi - (wide - 1), cr, wide, True), tail_single, carry)
        for j in range(hp):
            m, acc = carry[j]
            l = jnp.sum(jnp.where(lane == V_DIM, acc, 0.0), axis=-1, keepdims=True)
            o_ref[:, sl(j)] = jnp.where(lane < V_DIM, acc / l, 0.0)
            lse_ref[j] = m + jnp.log(l)

    head_col = pl.BlockSpec((s, hp * SLOT), lambda g, i: (0, g))
    tile = pl.BlockSpec((TQ, hp * SLOT), lambda g, i: (i, g))
    return _ride(
        body, riders, name="attn_fwd", grid=(MLA_HEADS // hp, nq),
        in_specs=[tile, head_col, head_col],
        out_specs=[tile, pl.BlockSpec((hp, TQ, 1), lambda g, i: (g, i, 0))],
        out_shape=[_sds((s, MLA_HEADS * SLOT)), _sds((MLA_HEADS, s, 1))],
        operands=(q, k, v), sem=("parallel", "parallel"))


def attn_bwd(q, k, v, o, do, lse, riders=()):
    s = q.shape[0]
    nq = s // TQ
    hp = ATT_HEADS_PER_STEP
    sl = lambda j: slice(j * SLOT, (j + 1) * SLOT)
    wide = ATT_WIDE // TQ

    def body(q_ref, k_ref, v_ref, o_ref, do_ref, lse_ref, dq_ref, dk_ref, dv_ref, delta_ref):
        ki = pl.program_id(1)

        @pl.when(ki == 0)
        def _():
            dq_ref[...] = jnp.zeros_like(dq_ref)

            def prep(i, c):
                rows = pl.ds(pl.multiple_of(i * TQ, TQ), TQ)
                for j in range(hp):
                    delta_ref[j, rows, :] = jnp.sum(do_ref[rows, sl(j)] * o_ref[rows, sl(j)], axis=-1, keepdims=True)
                return c

            lax.fori_loop(0, nq, prep, 0)

        kks = [k_ref[:, sl(j)] for j in range(hp)]
        vvs = [v_ref[:, sl(j)] for j in range(hp)]

        def step(qi, carry, n_tiles, masked):
            rq = pl.ds(pl.multiple_of(qi * TQ, TQ), n_tiles * TQ)
            if masked:
                row = lax.broadcasted_iota(jnp.int32, (n_tiles * TQ, TQ), 0)
                col = lax.broadcasted_iota(jnp.int32, (n_tiles * TQ, TQ), 1)
            out = []
            for j in range(hp):
                dk, dv = carry[j]
                qq = q_ref[rq, sl(j)]
                dd = do_ref[rq, sl(j)]
                sc = _dot(qq, kks[j], NT) * ATT_SCALE
                if masked:
                    sc = jnp.where(row >= col, sc, NEG_BIG)
                p = jnp.exp(sc - lse_ref[j, rq, :])
                dv = dv + _dot(p, dd, TN)
                ds = p * (_dot(dd, vvs[j], NT) - delta_ref[j, rq, :]) * ATT_SCALE
                dk = dk + _dot(ds, qq, TN)
                dq_ref[rq, sl(j)] = dq_ref[rq, sl(j)] + _dot(ds, kks[j], NN)
                out.append((dk, dv))
            return tuple(out)

        def head_single(cr):
            cr = step(ki, cr, 1, True)
            return lax.fori_loop(ki + 1, first_wide * wide, lambda qi, c: step(qi, c, 1, False), cr)

        zero = jnp.zeros((TQ, SLOT), F32)
        first_wide = (ki + wide) // wide
        carry = tuple((zero, zero) for _ in range(hp))
        carry = lax.cond(ki % wide == 0, lambda cr: step(ki, cr, wide, True), head_single, carry)
        carry = lax.fori_loop(first_wide, nq // wide, lambda qw, cr: step(qw * wide, cr, wide, False), carry)
        for j in range(hp):
            dk_ref[:, sl(j)] = carry[j][0]
            dv_ref[:, sl(j)] = carry[j][1]

    head_col = pl.BlockSpec((s, hp * SLOT), lambda g, i: (0, g))
    tile = pl.BlockSpec((TQ, hp * SLOT), lambda g, i: (i, g))
    return _ride(
        body, riders, name="attn_bwd", grid=(MLA_HEADS // hp, nq),
        in_specs=[head_col, tile, tile, head_col, head_col, pl.BlockSpec((hp, s, 1), lambda g, i: (g, 0, 0))],
        out_specs=[head_col, tile, tile],
        out_shape=[_sds((s, MLA_HEADS * SLOT))] * 3,
        scratch_shapes=[pltpu.VMEM((hp, s, 1), F32)],
        operands=(q, k, v, o, do, lse), sem=("arbitrary", "arbitrary"))


def _outproj_args(ya_ref, yb_ref, o_ref, mog_ref, woa_ref, wob_ref, woc_ref, cast):
    sl = lambda h: slice(h * SLOT, (h + 1) * SLOT)
    ldw = (lambda r: r[...].astype(F32)) if cast else (lambda r: r[...])
    ldc = (lambda h: woc_ref[sl(h), :].astype(F32)) if cast else (lambda h: woc_ref[sl(h), :])
    return (ya_ref[...], yb_ref[...], [o_ref[:, sl(h)] for h in range(MLA_HEADS)],
            [mog_ref[:, sl(h)] for h in range(MLA_HEADS)], ldw(woa_ref), ldw(wob_ref),
            [ldc(h) for h in range(MLA_HEADS)])


def outproj_fwd(x, ya, yb, o, mog, woa, wob, woc):
    s, d = x.shape

    def body(x_ref, ya_ref, yb_ref, o_ref, mog_ref, woa_ref, wob_ref, woc_ref, x1_ref):
        x1_ref[...] = _outproj(PLAIN, x_ref[...], *_outproj_args(ya_ref, yb_ref, o_ref, mog_ref, woa_ref, wob_ref,
                                                                  woc_ref, False))

    return pl.pallas_call(
        body, name="outproj_fwd", grid=(s // TM,),
        in_specs=[_rows(TM, d), _rows(TM, ya.shape[1]), _rows(TM, yb.shape[1]), _rows(TM, o.shape[1]),
                  _full(mog), _full(woa), _full(wob), _full(woc)],
        out_specs=_rows(TM, d), out_shape=_sds((s, d)),
        compiler_params=_cp(("parallel",)),
    )(x, ya, yb, o, mog, woa, wob, woc)


def outproj_bwd(ya, yb, o, mog, woa, wob, woc, dx2, dx1p, riders=()):
    s, d = dx2.shape
    npart = dx1p.shape[0]

    def body(ya_ref, yb_ref, o_ref, mog_ref, woa_ref, wob_ref, woc_ref, dx2_ref, dx1p_ref,
             dx1_ref, dya_ref, dyb_ref, do_ref, dmog_ref, dwoa_ref, dwob_ref, dwoc_ref):
        first = pl.program_id(0) == 0
        sl = lambda h: slice(h * SLOT, (h + 1) * SLOT)
        dx1 = dx2_ref[...]
        for p in range(npart):
            dx1 = dx1 + dx1p_ref[p]
        dx1_ref[...] = dx1
        args = _outproj_args(ya_ref, yb_ref, o_ref, mog_ref, woa_ref, wob_ref, woc_ref, True)
        _, vjp = jax.vjp(lambda *a: _outproj(AD, jnp.zeros_like(dx1), *a), *args)
        dya, dyb, do, dmog, dwoa, dwob, dwoc = vjp(dx1)
        dya_ref[...] = dya
        dyb_ref[...] = dyb
        _acc(dwoa_ref, dwoa, first)
        _acc(dwob_ref, dwob, first)
        for h in range(MLA_HEADS):
            do_ref[:, sl(h)] = do[h]
            _acc(dmog_ref.at[:, sl(h)], dmog[h], first)
            _acc(dwoc_ref.at[sl(h), :], dwoc[h], first)

    return _ride(
        body, riders, name="outproj_bwd", grid=(s // TM,),
        in_specs=[_rows(TM, ya.shape[1]), _rows(TM, yb.shape[1]), _rows(TM, o.shape[1]),
                  _full(mog), _full(woa), _full(wob), _full(woc), _rows(TM, d),
                  pl.BlockSpec((npart, TM, d), lambda i: (0, i, 0))],
        out_specs=[_rows(TM, d), _rows(TM, ya.shape[1]), _rows(TM, yb.shape[1]), _rows(TM, o.shape[1]),
                   _full(mog), _full(woa), _full(wob), _full(woc)],
        out_shape=[_sds((s, d)), _sds(ya.shape), _sds(yb.shape), _sds(o.shape),
                   _sds(mog.shape), _sds(woa.shape), _sds(wob.shape), _sds(woc.shape)],
        operands=(ya, yb, o, mog, woa, wob, woc, dx2, dx1p), sem=("arbitrary",))


def ffn_fwd(x1, g2, w1, w2, riders=()):
    s, d = x1.shape
    npart, _, fs = w1.shape

    def body(x1_ref, g_ref, w1_ref, w2_ref, x2_ref):
        p = pl.program_id(1)
        x1v = x1_ref[...]
        part = _ffn_part(PLAIN, x1v, g_ref[...], w1_ref[...], w2_ref[...])

        @pl.when(p == 0)
        def _():
            x2_ref[...] = x1v + part

        @pl.when(p != 0)
        def _():
            x2_ref[...] = x2_ref[...] + part

    tm = min(2 * TM_FFN, s)
    return _ride(
        body, riders, name="ffn_fwd", grid=(s // tm, npart),
        in_specs=[pl.BlockSpec((tm, d), lambda i, p: (i, 0)), pl.BlockSpec(g2.shape, lambda i, p: (0, 0)),
                  pl.BlockSpec((None, d, fs), lambda i, p: (p, 0, 0)), pl.BlockSpec((None, fs, d), lambda i, p: (p, 0, 0))],
        out_specs=pl.BlockSpec((tm, d), lambda i, p: (i, 0)), out_shape=_sds((s, d)),
        operands=(x1, g2, w1, w2), sem=("parallel", "arbitrary"))


def ffn_bwd(x1, g2, w1, w2, dx2, riders=()):
    s, d = x1.shape
    npart, _, fs = w1.shape

    def body(x1_ref, g_ref, w1_ref, w2_ref, dx2_ref, dx1p_ref, dg_ref, dw1_ref, dw2_ref):
        p = pl.program_id(0)
        i = pl.program_id(1)
        _, vjp = jax.vjp(functools.partial(_ffn_part, AD), x1_ref[...], g_ref[...], _f32(w1_ref), _f32(w2_ref))
        dx1, dg, dw1, dw2 = vjp(dx2_ref[...])
        dx1p_ref[...] = dx1
        _acc(dg_ref, dg, (p == 0) & (i == 0))
        _acc(dw1_ref, dw1, i == 0)
        _acc(dw2_ref, dw2, i == 0)

    tm = TM_FFN
    return _ride(
        body, riders, name="ffn_bwd", grid=(npart, s // tm),
        in_specs=[pl.BlockSpec((tm, d), lambda p, i: (i, 0)), pl.BlockSpec(g2.shape, lambda p, i: (0, 0)),
                  pl.BlockSpec((None, d, fs), lambda p, i: (p, 0, 0)), pl.BlockSpec((None, fs, d), lambda p, i: (p, 0, 0)),
                  pl.BlockSpec((tm, d), lambda p, i: (i, 0))],
        out_specs=[pl.BlockSpec((None, tm, d), lambda p, i: (p, i, 0)), pl.BlockSpec(g2.shape, lambda p, i: (0, 0)),
                   pl.BlockSpec((None, d, fs), lambda p, i: (p, 0, 0)), pl.BlockSpec((None, fs, d), lambda p, i: (p, 0, 0))],
        out_shape=[_sds((npart, s, d)), _sds(g2.shape), _sds(w1.shape), _sds(w2.shape)],
        operands=(x1, g2, w1, w2, dx2), sem=("arbitrary", "arbitrary"))


def loss_head(y, target):
    s, d = y.shape

    def body(y_ref, t_ref, dy_ref, loss_ref):
        err = y_ref[...] - t_ref[...]
        dy_ref[...] = err * (1.0 / d)
        part = jnp.sum(jnp.sum(err * err, axis=-1, keepdims=True), axis=0, keepdims=True) * (0.5 / d)
        _acc(loss_ref, jnp.broadcast_to(part, loss_ref.shape), pl.program_id(0) == 0)

    return pl.pallas_call(
        body, name="loss_head", grid=(s // TM,),
        in_specs=[_rows(TM, d), _rows(TM, d)],
        out_specs=[_rows(TM, d), pl.BlockSpec((1, SLOT), lambda i: (0, 0))],
        out_shape=[_sds((s, d)), _sds((1, SLOT))],
        compiler_params=_cp(("arbitrary",)),
    )(y, target)


def _row_block(r):
    for b in (512, 256, 128, 64, 32, 16, 8):
        if r % b == 0:
            return b
    return r


def sum_cores(arrs, gots, half, me):
    n = len(arrs)

    def body(sp_ref, *refs):
        for i in range(n):
            a_ref, g_ref, wire_ref, own_ref = refs[i], refs[n + i], refs[2 * n + i], refs[3 * n + i]
            tot = a_ref[...] + g_ref[...]
            wire_ref[...] = tot.astype(wire_ref.dtype)

            @pl.when(pl.program_id(0) == sp_ref[1])
            def _(own_ref=own_ref, tot=tot):
                own_ref[...] = tot

    shapes = [a.shape[2:] for a in arrs]
    grid_spec = pltpu.PrefetchScalarGridSpec(
        num_scalar_prefetch=1, grid=(N_CHIPS,),
        in_specs=[pl.BlockSpec((None, None) + sh, lambda p, sp: (p, sp[0], 0, 0)) for sh in shapes]
        + [pl.BlockSpec((None,) + sh, lambda p, sp: (p, 0, 0)) for sh in shapes],
        out_specs=[pl.BlockSpec((None,) + sh, lambda p, sp: (p, 0, 0)) for sh in shapes]
        + [pl.BlockSpec(sh, lambda p, sp: (0, 0)) for sh in shapes])
    outs = pl.pallas_call(body, name="sum_cores", grid_spec=grid_spec,
                          out_shape=[_sds((N_CHIPS,) + sh, BF16) for sh in shapes] + [_sds(sh) for sh in shapes],
                          compiler_params=_cp(("arbitrary",)))(jnp.stack([half, me]).astype(jnp.int32), *arrs, *gots)
    return outs[:n], outs[n:]


SUM_STEPS = 4


def sum_chips(owns, recvs, half):
    n = len(owns)

    def body(sp_ref, *refs):
        del sp_ref
        for i in range(n):
            own_ref, out_ref = refs[4 * i], refs[4 * n + i]
            r0, r1, r2 = (refs[4 * i + 1 + j][...].astype(F32) for j in range(3))
            out_ref[...] = ((own_ref[...] + r0) + r1) + r2

    in_specs, out_specs, operands = [], [], []
    for own, recv in zip(owns, recvs):
        r, c = own.shape
        br = r // SUM_STEPS
        in_specs.append(pl.BlockSpec((br, c), lambda i, sp: (i, 0)))
        in_specs += [pl.BlockSpec((None, br, c), functools.partial(lambda i, sp, j: (j, i, 0), j=j)) for j in range(3)]
        out_specs.append(pl.BlockSpec((None, br, c), lambda i, sp: (sp[0], i, 0)))
        operands += [own, recv, recv, recv]
    grid_spec = pltpu.PrefetchScalarGridSpec(num_scalar_prefetch=1, grid=(SUM_STEPS,), in_specs=in_specs, out_specs=out_specs)
    return pl.pallas_call(body, name="sum_chips", grid_spec=grid_spec, out_shape=[_sds((2,) + o.shape) for o in owns],
                          compiler_params=_cp(("parallel",)))(half.reshape(1).astype(jnp.int32), *operands)


def adamw(w, g, m, v, name, riders=()):
    r, c = w.shape
    br = _row_block(r)
    c1 = 1.0 / (1.0 - ADAM_B1 ** ADAM_STEP)
    c2 = 1.0 / (1.0 - ADAM_B2 ** ADAM_STEP)

    def body(w_ref, g_ref, m_ref, v_ref, d_ref, nm_ref, nv_ref):
        gg = g_ref[...]
        nm = ADAM_B1 * m_ref[...] + (1.0 - ADAM_B1) * gg
        nv = ADAM_B2 * v_ref[...] + (1.0 - ADAM_B2) * (gg * gg)
        d_ref[...] = -ADAM_LR * ((nm * c1) / (jnp.sqrt(nv * c2) + ADAM_EPS) + ADAM_WD * w_ref[...])
        nm_ref[...] = nm
        nv_ref[...] = nv

    return _ride(body, riders, name=name, grid=(r // br,), in_specs=[_rows(br, c)] * 4, out_specs=[_rows(br, c)] * 3,
                 out_shape=[_sds((r, c))] * 3, operands=(w, g, m, v), sem=("parallel",))


def _place():
    x, y, c = lax.axis_index("x"), lax.axis_index("y"), lax.axis_index("c")
    chips = [(1 - x, y), (x, 1 - y), (1 - x, 1 - y)]
    return x, y, c, chips


def _remote(src, dst, send_sem, recv_sem, to):
    return pltpu.make_async_remote_copy(src_ref=src, dst_ref=dst, send_sem=send_sem, recv_sem=recv_sem,
                                        device_id=to, device_id_type=MESH)


def gather_rider(arrs):
    n = len(arrs)
    me_chip = 2 * lax.axis_index("x") + lax.axis_index("y")
    bufs = [lax.dynamic_update_index_in_dim(lax.empty((N_CHIPS,) + a.shape, a.dtype), a, me_chip, 0) for a in arrs]

    def plan(ins, outs, sems):
        send_sems, recv_sems = sems
        x, y, c, chips = _place()
        me = 2 * x + y
        half, other, sibling = pl.ds(2 * c, 2), pl.ds(2 - 2 * c, 2), (x, y, 1 - c)
        cp = lambda i, k, src, dst, to: _remote(src, dst, send_sems.at[i, k], recv_sems.at[i, k], to)
        pairs = [(i, j, cx, cy) for i in range(n) for j, (cx, cy) in enumerate(chips)]
        blk = lambda i, cx, cy, part: outs[i].at[2 * cx + cy, part]
        first = lambda: [cp(i, j, ins[i].at[half], outs[i].at[me, half], (cx, cy, c)) for i, j, cx, cy in pairs]
        landed = lambda: [cp(i, j, blk(i, cx, cy, half), blk(i, cx, cy, half), (cx, cy, c)) for i, j, cx, cy in pairs]
        passed = lambda: [cp(i, 3 + j, blk(i, cx, cy, half), blk(i, cx, cy, half), sibling) for i, j, cx, cy in pairs]
        from_sibling = lambda: [cp(i, 3 + j, blk(i, cx, cy, other), blk(i, cx, cy, other), sibling) for i, j, cx, cy in pairs]
        return first, landed, passed, from_sibling

    def start(ins, outs, sems):
        for cp in plan(ins, outs, sems)[0]():
            cp.start()

    def finish(ins, outs, sems):
        first, landed, passed, from_sibling = plan(ins, outs, sems)
        forwards = passed()
        for a, b in zip(landed(), forwards):
            a.wait_recv()
            b.start()
        for cp in from_sibling():
            cp.wait_recv()
        for cp in first() + forwards:
            cp.wait_send()

    return Rider(list(arrs) + bufs, [_sds((N_CHIPS,) + a.shape, a.dtype) for a in arrs], {n + i: i for i in range(n)},
                 [pltpu.SemaphoreType.DMA((n, 6)), pltpu.SemaphoreType.DMA((n, 6))], start, finish)


class Reducer:
    def __init__(self, arrs):
        self.a = list(arrs)
        self.n = len(self.a)
        self.c = lax.axis_index("c")
        self.me = 2 * lax.axis_index("x") + lax.axis_index("y")

    def swap_rider(self):
        n = self.n

        def plan(ins, outs, sems):
            x, y, c, _ = _place()
            return [_remote(ins[i].at[p, 1 - c], outs[i].at[p], sems[0].at[i, p], sems[1].at[i, p], (x, y, 1 - c))
                    for i in range(n) for p in range(N_CHIPS)]

        return Rider(self.a, [_sds((N_CHIPS,) + a.shape[2:]) for a in self.a], {},
                     [pltpu.SemaphoreType.DMA((n, N_CHIPS)), pltpu.SemaphoreType.DMA((n, N_CHIPS))],
                     lambda *r: [cp.start() for cp in plan(*r)], lambda *r: [cp.wait() for cp in plan(*r)])

    def after_swap(self, got):
        self.wire, self.own = sum_cores(self.a, got, self.c, self.me)

    def scatter_rider(self):
        n = self.n

        def plan(ins, outs, sems):
            x, y, c, chips = _place()
            return [_remote(ins[i].at[2 * cx + cy], outs[i].at[j], sems[0].at[i, j], sems[1].at[i, j], (cx, cy, c))
                    for i in range(n) for j, (cx, cy) in enumerate(chips)]

        return Rider(self.wire, [_sds((3,) + w.shape[1:], w.dtype) for w in self.wire], {},
                     [pltpu.SemaphoreType.DMA((n, 3)), pltpu.SemaphoreType.DMA((n, 3))],
                     lambda *r: [cp.start() for cp in plan(*r)], lambda *r: [cp.wait() for cp in plan(*r)])

    def after_scatter(self, recv):
        self.full = sum_chips(self.own, recv, self.c)

    def share_rider(self):
        n = self.n

        def plan(ins, outs, sems):
            x, y, c, _ = _place()
            return [_remote(ins[i].at[c], outs[i].at[c], sems[0].at[i], sems[1].at[i], (x, y, 1 - c)) for i in range(n)]

        return Rider(self.full, [_sds(f.shape) for f in self.full], {i: i for i in range(n)},
                     [pltpu.SemaphoreType.DMA((n,)), pltpu.SemaphoreType.DMA((n,))],
                     lambda *r: [cp.start() for cp in plan(*r)], lambda *r: [cp.wait() for cp in plan(*r)])

    def run(self):
        self.after_swap(run_rider(self.swap_rider(), "swap_halves"))
        self.after_scatter(run_rider(self.scatter_rider(), "scatter_chips"))
        return run_rider(self.share_rider(), "share_halves")


def _pad_slots(a, live):
    lead = a.shape[:-1]
    a = a.reshape(lead + (MLA_HEADS, live))
    a = jnp.pad(a, [(0, 0)] * len(lead) + [(0, 0), (0, SLOT - live)])
    return a.reshape(lead + (MLA_HEADS * SLOT,))


def _unpad_slots(a, live):
    lead = a.shape[:-1]
    return a.reshape(lead + (MLA_HEADS, SLOT))[..., :live].reshape(lead + (MLA_HEADS * live,))


def _rope_tables(positions, s):
    half = QK_ROPE // 2
    inv_freq = ROPE_THETA ** (-jnp.arange(half, dtype=F32) / half)
    ang = positions.reshape(s).astype(F32)[:, None] * inv_freq[None, :]
    cos, sin = jnp.cos(ang), jnp.sin(ang)
    one = jnp.ones((s, QK_NOPE), F32)
    z64, z16, z32 = jnp.zeros((s, QK_NOPE), F32), jnp.zeros((s, half), F32), jnp.zeros((s, SLOT - QK_DIM), F32)
    cos_t = jnp.concatenate([one, cos, cos, z32], axis=1)
    sin_a = jnp.concatenate([z64, -sin, z16, z32], axis=1)
    sin_b = jnp.concatenate([z64, z16, sin, z32], axis=1)
    return cos_t, sin_a, sin_b


def _layer_weights(full, small, l):
    w_in = jnp.concatenate([full["w_in"][p] for p in range(N_CHIPS)], axis=1)
    wc = jnp.pad(w_in[:, 1536:], ((0, 0), (0, 512 - (w_in.shape[1] - 1536))))
    w_uq = jnp.concatenate([full["mla_w_uq"][p] for p in range(N_CHIPS)], axis=1)
    w_ukv = jnp.concatenate([full["mla_w_ukv"][p] for p in range(N_CHIPS)], axis=1)
    ukv = w_ukv.reshape(KV_LORA, MLA_HEADS, QK_NOPE + V_DIM)
    w_out = jnp.concatenate([full["w_out"][p] for p in range(N_CHIPS)], axis=0)
    woc = w_out[512:].reshape(MLA_HEADS, V_DIM, D_MODEL)
    woc = jnp.pad(woc, ((0, 0), (0, SLOT - V_DIM), (0, 0))).reshape(MLA_HEADS * SLOT, D_MODEL)
    row = lambda a: a.reshape(1, -1)
    return dict(
        g1=row(small["norm1_gain"][l]), wa=w_in[:, :512], wb=w_in[:, 512:1536], wc=wc,
        vg=row(small["gm_v_gain"][l]), ws=small["gm_w_s"][l], bs=small["gm_b_s"][l].reshape(4, CHUNK, 1),
        gog=row(small["gm_out_gain"][l]), hog=small["hg_out_gain"][l].reshape(-1, 1),
        qag=row(small["mla_q_a_gain"][l]), kvag=row(small["mla_kv_a_gain"][l]),
        qg=row(jnp.pad(small["mla_q_gain"][l], (0, SLOT - QK_DIM))), kg=row(jnp.pad(small["mla_k_gain"][l], (0, SLOT - QK_DIM))),
        wq=_pad_slots(w_uq, QK_DIM), wk=_pad_slots(ukv[..., :QK_NOPE].reshape(KV_LORA, -1), QK_NOPE),
        wv=_pad_slots(ukv[..., QK_NOPE:].reshape(KV_LORA, -1), V_DIM),
        mog=row(_pad_slots(small["mla_out_gain"][l], V_DIM)),
        woa=w_out[:256], wob=w_out[256:512], woc=woc,
        g2=row(small["norm2_gain"][l]),
    )


def _shard_cols(a):
    r, c4 = a.shape
    return a.reshape(r, N_CHIPS, c4 // N_CHIPS).transpose(1, 0, 2)


def local_step(x, positions, target, small, comm):
    s = x.shape[0]
    cos_t, sin_a, sin_b = _rope_tables(positions, s)
    lbs = lower_bounds_fwd(small["hg_lower_bound"])
    lw, saved = [], []
    for l in range(DEPTH):
        w = _layer_weights(comm.part(l, "mix"), small, l)
        lw.append(w)
        lb = lbs[l].reshape(-1, 1)
        pa, pb, pc = inproj_fwd(x, w["g1"], w["wa"], w["wb"], w["wc"])
        ya = gm_fwd(pa, w["vg"], w["ws"], w["bs"], w["gog"])
        yb, states = hg_fwd(pb, lb, w["hog"])
        q, k, v = mla_pre_fwd(pc, cos_t, sin_a, sin_b, w["qag"], w["kvag"], w["qg"], w["kg"], w["wq"], w["wk"], w["wv"])
        rider = comm.gather_rider(l, "ffn")
        (o, lse), got = attn_fwd(q, k, v, [rider])
        comm.gathered(l, "ffn", got[0])
        x1 = outproj_fwd(x, ya, yb, o, w["mog"], w["woa"], w["wob"], w["woc"])
        ffn_w = comm.part(l, "ffn")
        w["w1"], w["w2"] = ffn_w["w_ff1"], ffn_w["w_ff2"]
        rider = comm.gather_rider(l + 1, "mix") if l + 1 < DEPTH else None
        x2, got = ffn_fwd(x1, w["g2"], w["w1"], w["w2"], [rider])
        comm.gathered(l + 1, "mix", got[0])
        saved.append(dict(x=x, pa=pa, pb=pb, pc=pc, ya=ya, yb=yb, states=states, q=q, k=k, v=v, o=o, lse=lse, x1=x1, lb=lb))
        x = x2
    dx, loss_part = loss_head(x, target)
    groups = [dict() for _ in range(DEPTH)]
    sm = {n: [None] * DEPTH for n in ("norm1_gain", "gm_v_gain", "gm_w_s", "gm_b_s", "gm_out_gain", "hg_out_gain",
                                       "mla_q_a_gain", "mla_kv_a_gain", "mla_q_gain", "mla_k_gain", "mla_out_gain",
                                       "norm2_gain")}
    dlbs = [None] * DEPTH
    halves = lambda g: g.reshape(N_CHIPS, 2, g.shape[1] // 2, g.shape[2])
    take = lambda red, f: None if red is None else f(red)
    red_mix = None
    for l in reversed(range(DEPTH)):
        w, a = lw[l], saved[l]
        (dx1p, dg2, dw1, dw2), got = ffn_bwd(a["x1"], w["g2"], w["w1"], w["w2"], dx, [take(red_mix, Reducer.swap_rider)])
        if red_mix:
            red_mix.after_swap(got[0])
        ffn_arrs = [halves(dw1), halves(dw2)]
        red_ffn = comm.reducer(ffn_arrs)
        (dx1, dya, dyb, do, dmog, dwoa, dwob, dwoc), got = outproj_bwd(
            a["ya"], a["yb"], a["o"], w["mog"], w["woa"], w["wob"], w["woc"], dx, dx1p, [take(red_ffn, Reducer.swap_rider)])
        if red_ffn:
            red_ffn.after_swap(got[0])
        (dq, dk, dv), got = attn_bwd(a["q"], a["k"], a["v"], a["o"], do, a["lse"],
                                     [take(red_mix, Reducer.scatter_rider), take(red_ffn, Reducer.scatter_rider)])
        for red, g in zip((red_mix, red_ffn), got):
            if red:
                red.after_scatter(g)
        dpc, dqag, dkvag, dqg, dkg, dwq, dwk, dwv = mla_pre_bwd(a["pc"], cos_t, sin_a, sin_b, w["qag"], w["kvag"], w["qg"],
                                                                  w["kg"], w["wq"], w["wk"], w["wv"], dq, dk, dv)
        (dpb, dlb, dhog), got = hg_bwd(a["pb"], a["lb"], w["hog"], a["states"], dyb,
                                       [take(red_mix, Reducer.share_rider), take(red_ffn, Reducer.share_rider)])
        if red_mix:
            groups[l + 1].update(zip(MIX, got[0]))
        groups[l].update(zip(FFN, got[1] if red_ffn else ffn_arrs))
        dpa, dvg, dws, dbs, dgog = gm_bwd(a["pa"], w["vg"], w["ws"], w["bs"], w["gog"], dya)
        dx, dg1, dwa, dwb, dwc = inproj_bwd(a["x"], w["g1"], w["wa"], w["wb"], w["wc"], dpa, dpb, dpc, dx1)
        dukv = jnp.concatenate([dwk.reshape(KV_LORA, MLA_HEADS, SLOT)[..., :QK_NOPE],
                                dwv.reshape(KV_LORA, MLA_HEADS, SLOT)[..., :V_DIM]], axis=-1)
        dwo = jnp.concatenate([dwoa, dwob, dwoc.reshape(MLA_HEADS, SLOT, D_MODEL)[:, :V_DIM].reshape(-1, D_MODEL)], axis=0)
        mix_arrs = [halves(_shard_cols(jnp.concatenate([dwa, dwb, dwc[:, :1952 - 1536]], axis=1))),
                    halves(_shard_cols(_unpad_slots(dwq, QK_DIM))), halves(_shard_cols(dukv.reshape(KV_LORA, -1))),
                    halves(dwo.reshape(N_CHIPS, -1, D_MODEL))]
        red_mix = comm.reducer(mix_arrs) if l > 0 else None
        if red_mix is None:
            groups[l].update(zip(MIX, mix_arrs))
        sm["norm1_gain"][l] = dg1[0]
        sm["gm_v_gain"][l] = dvg[0]
        sm["gm_w_s"][l] = dws
        sm["gm_b_s"][l] = dbs[..., 0]
        sm["gm_out_gain"][l] = dgog[0]
        sm["hg_out_gain"][l] = dhog[:, 0]
        sm["mla_q_a_gain"][l] = dqag[0]
        sm["mla_kv_a_gain"][l] = dkvag[0]
        sm["mla_q_gain"][l] = dqg[0, :QK_DIM]
        sm["mla_k_gain"][l] = dkg[0, :QK_DIM]
        sm["mla_out_gain"][l] = _unpad_slots(dmog[0], V_DIM)
        sm["norm2_gain"][l] = dg2[0]
        dlbs[l] = dlb[:, 0]
    sm["hg_lower_bound"] = [lower_bounds_bwd(small["hg_lower_bound"], jnp.stack(dlbs))]
    return loss_part, dx, groups, sm


MIX = ("w_in", "mla_w_uq", "mla_w_ukv", "w_out")
FFN = ("w_ff1", "w_ff2")
BIG = MIX + FFN
PARTS = {"mix": MIX, "ffn": FFN}
SMALL = ("norm1_gain", "gm_v_gain", "gm_w_s", "gm_b_s", "gm_out_gain", "hg_lower_bound", "hg_out_gain",
         "mla_q_a_gain", "mla_kv_a_gain", "mla_q_gain", "mla_k_gain", "mla_out_gain", "norm2_gain")
ORDER = ("norm1_gain", "w_in", "gm_v_gain", "gm_w_s", "gm_b_s", "gm_out_gain", "hg_lower_bound", "hg_out_gain",
         "mla_q_a_gain", "mla_w_uq", "mla_kv_a_gain", "mla_w_ukv", "mla_q_gain", "mla_k_gain", "mla_out_gain",
         "w_out", "norm2_gain", "w_ff1", "w_ff2")
PACK_ROWS = 320


def _pack(pieces):
    flat = jnp.concatenate([a.reshape(-1) for a in pieces])
    total = 2 * N_CHIPS * PACK_ROWS * SLOT
    return jnp.pad(flat, (0, total - flat.shape[0]))


def _unpack(flat, shapes):
    out, off = [], 0
    for sh in shapes:
        size = 1
        for d in sh:
            size *= d
        out.append(flat[off:off + size].reshape(sh))
        off += size
    return out


class ChipComm:
    def __init__(self, shards):
        self.shards = shards
        self.full = {}

    def gather_rider(self, l, part):
        return gather_rider([self.shards[n][l].astype(MXU_DTYPE).reshape(4, self.shards[n].shape[1] // 4, -1)
                             for n in PARTS[part]])

    def gathered(self, l, part, outs):
        if outs is not None:
            self.full[l, part] = {n: o.reshape((N_CHIPS,) + self.shards[n].shape[1:]) for n, o in zip(PARTS[part], outs)}

    def part(self, l, part):
        if (l, part) not in self.full:
            self.gathered(l, part, run_rider(self.gather_rider(l, part), "gather_weights"))
        return self.full[l, part]

    def reducer(self, arrs):
        return Reducer(arrs)


def kernel(x, positions, norm1_gain, w_in, gm_v_gain, gm_w_s, gm_b_s, gm_out_gain, hg_lower_bound, hg_out_gain, mla_q_a_gain, mla_w_uq, mla_kv_a_gain, mla_w_ukv, mla_q_gain, mla_k_gain, mla_out_gain, w_out, norm2_gain, w_ff1, w_ff2, loss_target, m_norm1_gain, m_w_in, m_gm_v_gain, m_gm_w_s, m_gm_b_s, m_gm_out_gain, m_hg_lower_bound, m_hg_out_gain, m_mla_q_a_gain, m_mla_w_uq, m_mla_kv_a_gain, m_mla_w_ukv, m_mla_q_gain, m_mla_k_gain, m_mla_out_gain, m_w_out, m_norm2_gain, m_w_ff1, m_w_ff2, v_norm1_gain, v_w_in, v_gm_v_gain, v_gm_w_s, v_gm_b_s, v_gm_out_gain, v_hg_lower_bound, v_hg_out_gain, v_mla_q_a_gain, v_mla_w_uq, v_mla_kv_a_gain, v_mla_w_ukv, v_mla_q_gain, v_mla_k_gain, v_mla_out_gain, v_w_out, v_norm2_gain, v_w_ff1, v_w_ff2):
    given = dict(locals())
    weights = {n: given[n] for n in ORDER}
    moms = {n: given["m_" + n] for n in ORDER}
    vars_ = {n: given["v_" + n] for n in ORDER}
    s, d = x.shape[1], x.shape[2]

    small = {n: weights[n] for n in SMALL}
    comm = ChipComm({n: weights[n] for n in BIG})
    loss_part, dx, groups, small_g = local_step(x.reshape(s, d), positions, loss_target.reshape(s, d), small, comm)
    loss = lax.psum(loss_part[0, 0], ("x", "y", "c"))

    pack_g = _pack([piece for n in SMALL for piece in small_g[n]]).reshape(N_CHIPS, 2, PACK_ROWS, SLOT)
    reduced = Reducer([groups[0][n] for n in MIX] + [pack_g]).run()
    groups[0].update(zip(MIX, reduced[:-1]))
    pack_full = run_rider(gather_rider([reduced[-1].reshape(4, PACK_ROWS // 2, SLOT)]), "gather_small")[0].reshape(-1)
    grads = {n: jnp.stack([groups[l][n].reshape(weights[n].shape[1:]) for l in range(DEPTH)]) for n in BIG}
    grads.update(zip(SMALL, _unpack(pack_full, [weights[n].shape for n in SMALL])))

    delta, new_m, new_v = {}, {}, {}
    flat2 = lambda a: a.reshape(-1, a.shape[-1])
    for n in ORDER:
        outs, _ = adamw(flat2(weights[n]), flat2(grads[n]), flat2(moms[n]), flat2(vars_[n]), "adamw_" + n)
        delta[n], new_m[n], new_v[n] = [o.reshape(weights[n].shape) for o in outs]

    return (loss, dx.reshape(x.shape), *[grads[n] for n in ORDER], *[delta[n] for n in ORDER],
            *[new_m[n] for n in ORDER], *[new_v[n] for n in ORDER])
```

```python
import functools

import jax
import jax.numpy as jnp
from jax import lax
from jax.experimental import pallas as pl
from jax.experimental.pallas import tpu as pltpu

F32 = jnp.float32
BF16 = jnp.bfloat16
MXU_DTYPE = BF16

D_MODEL = 1024
DEPTH = 4
CHUNK = 128
HG_CHUNK = 128
HG_CHUNKS = 2
GM_CHUNKS = 4
EPS = 1e-6
HEAD64 = 64
MLA_HEADS = 8
QK_NOPE = 64
QK_ROPE = 32
QK_DIM = 96
V_DIM = 64
Q_LORA = 256
KV_LORA = 128
SLOT = 128
ROPE_THETA = 10000.0
D_FF_SHARD = 1024
N_CHIPS = 4

ADAM_LR = 0.001
ADAM_B1 = 0.9
ADAM_B2 = 0.999
ADAM_EPS = 1e-08
ADAM_WD = 0.01
ADAM_STEP = 10

TM = 512
TM_FFN = 512
TQ = 256
ATT_HEADS_PER_STEP = 4
ATT_WIDE = 512
VMEM_LIMIT = 56 * 1024 * 1024

NN = (((1,), (0,)), ((), ()))
NT = (((1,), (1,)), ((), ()))
TN = (((0,), (0,)), ((), ()))
BNN = (((2,), (1,)), ((0,), (0,)))
BNT = (((2,), (2,)), ((0,), (0,)))
BTN = (((1,), (1,)), ((0,), (0,)))


def _dot(a, b, dims):
    return lax.dot_general(a.astype(MXU_DTYPE), b.astype(MXU_DTYPE), dims, preferred_element_type=F32)


def _hdot(a, b, dims=NN):
    return lax.dot_general(a, b, dims, precision=lax.Precision.HIGHEST, preferred_element_type=F32)


def _make_ad(dims, da_dims, da_swap, db_dims, db_swap):
    @jax.custom_vjp
    def f(a, b):
        return _dot(a, b, dims)

    def fwd(a, b):
        return _dot(a, b, dims), (a, b)

    def bwd(res, g):
        a, b = res
        da = _dot(b, g, da_dims) if da_swap else _dot(g, b, da_dims)
        db = _dot(g, a, db_dims) if db_swap else _dot(a, g, db_dims)
        return da, db

    f.defvjp(fwd, bwd)
    return f


@functools.partial(jax.custom_vjp, nondiff_argnums=(1,))
def _roll_ad(x, shift):
    return pltpu.roll(x, shift, 1)


def _roll_ad_fwd(x, shift):
    return pltpu.roll(x, shift, 1), None


def _roll_ad_bwd(shift, _, g):
    return (pltpu.roll(g, (g.shape[1] - shift) % g.shape[1], 1),)


_roll_ad.defvjp(_roll_ad_fwd, _roll_ad_bwd)


class _Ops:
    pass


PLAIN = _Ops()
PLAIN.mm = lambda a, b: _dot(a, b, NN)
PLAIN.bmm = lambda a, b: _dot(a, b, BNN)
PLAIN.bmm_nt = lambda a, b: _dot(a, b, BNT)
PLAIN.bmm_tn = lambda a, b: _dot(a, b, BTN)
PLAIN.roll = lambda x, s: pltpu.roll(x, s, 1)

AD = _Ops()
AD.mm = _make_ad(NN, NT, False, TN, False)
AD.bmm = _make_ad(BNN, BNT, False, BTN, False)
AD.bmm_nt = _make_ad(BNT, BNN, False, BTN, True)
AD.bmm_tn = _make_ad(BTN, BNT, True, BNN, False)
AD.roll = _roll_ad


def _sigmoid(x):
    return jax.nn.sigmoid(x)


def _gelu(x):
    return 0.5 * x * (1.0 + jnp.tanh(0.7978845608028654 * (x + 0.044715 * (x * x * x))))


def _rms(x, g):
    return x * lax.rsqrt(jnp.mean(x * x, axis=-1, keepdims=True) + EPS) * g


def _head_masks256():
    lane = lax.broadcasted_iota(jnp.int32, (1, 4 * HEAD64), 1)
    return [(jnp.right_shift(lane, 6) == h).astype(F32) for h in range(4)]


def _headnorm256(x, g):
    ms = jnp.zeros_like(x)
    sq = x * x
    for m in _head_masks256():
        ms = ms + m * (jnp.sum(sq * m, axis=-1, keepdims=True) * (1.0 / HEAD64))
    return x * lax.rsqrt(ms + EPS) * g


def _slot_norm(x, g, n):
    return x * lax.rsqrt(jnp.sum(x * x, axis=-1, keepdims=True) * (1.0 / n) + EPS) * g


def _rope(ops, x, cos_t, sin_a, sin_b):
    return x * cos_t + ops.roll(x, SLOT - QK_ROPE // 2) * sin_a + ops.roll(x, QK_ROPE // 2) * sin_b


def _inproj(ops, x, g1, wa, wb, wc):
    h = _rms(x, g1)
    return ops.mm(h, wa), ops.mm(h, wb), ops.mm(h, wc)


def _gm_chunk(ops, ur, vr, vg, ws4, bs, og):
    c = ur.shape[0]
    masks = _head_masks256()
    mh = jnp.concatenate([m[None] for m in masks], axis=0)
    u = _gelu(ur)
    v = _headnorm256(_gelu(vr), vg)
    t = lax.broadcasted_iota(jnp.int32, (c, c), 0)
    s = lax.broadcasted_iota(jnp.int32, (c, c), 1)
    w = jnp.where((t >= s)[None], ws4, 0.0)
    y = jnp.sum(ops.bmm(w, v[None] * mh), axis=0)
    for h in range(4):
        y = y + bs[h] * masks[h]
    return _headnorm256(u * y, og)


def _hg_chunk(ops, st, qr, fr, ir, gr, lb, og):
    c, n = qr.shape
    nh = n // HEAD64
    heads = lambda x: x.reshape(nh, HEAD64, x.shape[-1])
    tr = lambda x: heads(x.T)
    lb4, og4 = heads(lb), heads(og)
    qx = tr(qr)
    q = qx * _sigmoid(qx)
    f = lb4 + (1.0 - lb4) * _sigmoid(tr(fr))
    k = 1.0 - f
    logf = jnp.log(f)
    v = tr(ir)
    gx = tr(gr)
    s = lax.broadcasted_iota(jnp.int32, (c, c), 0)
    t = lax.broadcasted_iota(jnp.int32, (c, c), 1)
    tl = lax.broadcasted_iota(jnp.int32, (1, c), 1).reshape(1, 1, c)
    b2 = _hdot(logf.reshape(n, c), (s <= t).astype(F32))
    b = heads(b2)
    btot = jnp.sum(logf, axis=2, keepdims=True)
    inter = ops.bmm_tn(st, q * jnp.exp(b))
    p4 = jnp.zeros((nh, c, c), F32)
    tt, ss = s, t
    lg = c.bit_length() - 2
    while lg >= 0:
        m = 1 << lg
        bnd = jnp.left_shift(jnp.right_shift(t, lg + 1), lg + 1) + (m - 1)
        r = heads(_hdot(b2, (s == bnd).astype(F32)))
        right = jnp.bitwise_and(jnp.right_shift(tl, lg), 1) == 1
        qe = jnp.where(right, q * jnp.exp(jnp.where(right, b - r, 0.0)), 0.0)
        ke = jnp.where(right, 0.0, k * jnp.exp(jnp.where(right, 0.0, r - b)))
        lm = ((jnp.right_shift(tt, lg + 1) == jnp.right_shift(ss, lg + 1))
              & (jnp.bitwise_and(jnp.right_shift(tt, lg), 1) == 1)
              & (jnp.bitwise_and(jnp.right_shift(ss, lg), 1) == 0))
        p4 = jnp.where(lm[None], ops.bmm_tn(qe, ke), p4)
        lg -= 1
    intra = ops.bmm_nt(v, p4)
    o = inter + intra + jnp.sum(q * k, axis=1, keepdims=True) * v
    st_new = st * jnp.exp(btot) + ops.bmm_nt(k * jnp.exp(btot - b), v)
    y = o * lax.rsqrt(jnp.mean(o * o, axis=1, keepdims=True) + EPS) * og4 * (gx * _sigmoid(gx))
    return st_new, y.reshape(n, c).T


def _mla_pre(ops, cq, ckv, kpe, cos_t, sin_a, sin_b, qag, kvag, qg, kg, wq, wk, wv):
    cqn = _rms(cq, qag)
    ckvn = _rms(ckv, kvag)
    kper = ops.roll(kpe, QK_NOPE)
    qs, ks, vs = [], [], []
    for h in range(MLA_HEADS):
        qh = _slot_norm(ops.mm(cqn, wq[h]), qg, QK_DIM)
        qs.append(_rope(ops, qh, cos_t, sin_a, sin_b))
        kh = _slot_norm(ops.mm(ckvn, wk[h]) + kper, kg, QK_DIM)
        ks.append(_rope(ops, kh, cos_t, sin_a, sin_b))
        vs.append(ops.mm(ckvn, wv[h]))
    return qs, ks, vs


def _outproj(ops, x, ya, yb, o, mog, woa, wob, woc):
    acc = x + ops.mm(ya, woa) + ops.mm(yb, wob)
    for h in range(MLA_HEADS):
        acc = acc + ops.mm(_slot_norm(o[h], mog[h], V_DIM), woc[h])
    return acc


def _ffn_part(ops, x1, g2, w1p, w2p):
    a = ops.mm(_rms(x1, g2), w1p)
    r = jnp.maximum(a, 0.0)
    return ops.mm(r * r, w2p)


def _lower_bounds(r0, r1, r2, r3):
    mx = jnp.maximum(jnp.maximum(r0, r1), jnp.maximum(r2, r3))
    e0, e1, e2, e3 = jnp.exp(r0 - mx), jnp.exp(r1 - mx), jnp.exp(r2 - mx), jnp.exp(r3 - mx)
    inv = 1.0 / (e0 + e1 + e2 + e3)
    s1, s2, s3 = e1 * inv, e2 * inv, e3 * inv
    return jnp.zeros_like(r0), s1, s1 + s2, s1 + s2 + s3


def _cp(sem):
    return pltpu.CompilerParams(dimension_semantics=sem, vmem_limit_bytes=VMEM_LIMIT)


def _rows(tm, n):
    return pl.BlockSpec((tm, n), lambda i: (i, 0))


def _full(a):
    nd = len(a.shape)
    return pl.BlockSpec(a.shape, lambda *_: (0,) * nd, pipeline_mode=pl.Buffered(1))


def _sds(shape, dtype=F32):
    return jax.ShapeDtypeStruct(shape, dtype)


def _acc(ref, val, first):
    @pl.when(first)
    def _():
        ref[...] = val

    @pl.when(jnp.logical_not(first))
    def _():
        ref[...] = ref[...] + val


def _f32(ref):
    return ref[...].astype(F32)


MESH = pl.DeviceIdType.MESH
ANY = pl.BlockSpec(memory_space=pl.ANY)


class Rider:
    def __init__(self, arrays, out_shapes, aliases, sems, start, finish):
        self.arrays, self.out_shapes, self.aliases, self.sems = list(arrays), list(out_shapes), dict(aliases), list(sems)
        self.start, self.finish = start, finish


def run_rider(rider, name):
    n_in, n_out = len(rider.arrays), len(rider.out_shapes)

    def body(*refs):
        ins, outs, sems = refs[:n_in], refs[n_in:n_in + n_out], refs[n_in + n_out:]
        rider.start(ins, outs, sems)
        rider.finish(ins, outs, sems)

    return pl.pallas_call(
        body, name=name, in_specs=[ANY] * n_in, out_specs=[ANY] * n_out, out_shape=rider.out_shapes,
        input_output_aliases=rider.aliases, scratch_shapes=rider.sems,
    )(*rider.arrays)


def _merge_riders(riders):
    bounds, a0, o0, s0 = [], 0, 0, 0
    for r in riders:
        bounds.append((a0, o0, s0))
        a0, o0, s0 = a0 + len(r.arrays), o0 + len(r.out_shapes), s0 + len(r.sems)

    def part(k, ins, outs, sems):
        a, o, s = bounds[k]
        r = riders[k]
        return ins[a:a + len(r.arrays)], outs[o:o + len(r.out_shapes)], sems[s:s + len(r.sems)]

    return Rider(
        [x for r in riders for x in r.arrays], [x for r in riders for x in r.out_shapes],
        {bounds[k][0] + i: bounds[k][1] + o for k, r in enumerate(riders) for i, o in r.aliases.items()},
        [x for r in riders for x in r.sems],
        lambda *refs: [r.start(*part(k, *refs)) for k, r in enumerate(riders)],
        lambda *refs: [r.finish(*part(k, *refs)) for k, r in enumerate(riders)])


def _ride(compute, riders, *, name, grid, in_specs, out_specs, out_shape, operands, scratch_shapes=(), sem=None):
    single = not isinstance(out_shape, (list, tuple))
    if single:
        out_specs, out_shape = [out_specs], [out_shape]
    live = [r for r in riders if r is not None]
    if not live:
        res = pl.pallas_call(compute, name=name, grid=grid, in_specs=in_specs, out_specs=out_specs, out_shape=out_shape,
                             scratch_shapes=list(scratch_shapes), compiler_params=_cp(sem))(*operands)
        return (res[0] if single else res), [None] * len(riders)
    rider = live[0] if len(live) == 1 else _merge_riders(live)
    n_in, n_out, n_s = len(in_specs), len(out_specs), len(scratch_shapes)
    r_in, r_out = len(rider.arrays), len(rider.out_shapes)

    def body(*refs):
        ins, rins = refs[:n_in], refs[n_in:n_in + r_in]
        outs = refs[n_in + r_in:n_in + r_in + n_out]
        routs = refs[n_in + r_in + n_out:n_in + r_in + n_out + r_out]
        scr = refs[n_in + r_in + n_out + r_out:n_in + r_in + n_out + r_out + n_s]
        rsems = refs[n_in + r_in + n_out + r_out + n_s:]
        first = functools.reduce(jnp.logical_and, [pl.program_id(a) == 0 for a in range(len(grid))])
        last = functools.reduce(jnp.logical_and, [pl.program_id(a) == grid[a] - 1 for a in range(len(grid))])

        @pl.when(first)
        def _():
            rider.start(rins, routs, rsems)

        compute(*ins, *outs, *scr)

        @pl.when(last)
        def _():
            rider.finish(rins, routs, rsems)

    res = pl.pallas_call(
        body, name=name, grid=grid, in_specs=list(in_specs) + [ANY] * r_in, out_specs=list(out_specs) + [ANY] * r_out,
        out_shape=list(out_shape) + rider.out_shapes,
        input_output_aliases={n_in + k: n_out + v for k, v in rider.aliases.items()},
        scratch_shapes=list(scratch_shapes) + rider.sems, compiler_params=_cp(("arbitrary",) * len(grid)),
    )(*operands, *rider.arrays)
    main, rest, per_rider = res[:n_out], list(res[n_out:]), []
    for r in riders:
        per_rider.append(None if r is None else [rest.pop(0) for _ in r.out_shapes])
    return (main[0] if single else main), per_rider


def inproj_fwd(x, g1, wa, wb, wc):
    s, d = x.shape

    def body(x_ref, g_ref, wa_ref, wb_ref, wc_ref, pa_ref, pb_ref, pc_ref):
        pa, pb, pc = _inproj(PLAIN, x_ref[...], g_ref[...], wa_ref[...], wb_ref[...], wc_ref[...])
        pa_ref[...] = pa
        pb_ref[...] = pb
        pc_ref[...] = pc

    return pl.pallas_call(
        body, name="inproj_fwd", grid=(s // TM,),
        in_specs=[_rows(TM, d), _full(g1), _full(wa), _full(wb), _full(wc)],
        out_specs=[_rows(TM, wa.shape[1]), _rows(TM, wb.shape[1]), _rows(TM, wc.shape[1])],
        out_shape=[_sds((s, wa.shape[1])), _sds((s, wb.shape[1])), _sds((s, wc.shape[1]))],
        compiler_params=_cp(("parallel",)),
    )(x, g1, wa, wb, wc)


def inproj_bwd(x, g1, wa, wb, wc, dpa, dpb, dpc, dres):
    s, d = x.shape

    def body(x_ref, g_ref, wa_ref, wb_ref, wc_ref, dpa_ref, dpb_ref, dpc_ref, dres_ref,
             dx_ref, dg_ref, dwa_ref, dwb_ref, dwc_ref):
        first = pl.program_id(0) == 0
        _, vjp = jax.vjp(functools.partial(_inproj, AD), x_ref[...], g_ref[...],
                         _f32(wa_ref), _f32(wb_ref), _f32(wc_ref))
        dx, dg, dwa, dwb, dwc = vjp((dpa_ref[...], dpb_ref[...], dpc_ref[...]))
        dx_ref[...] = dx + dres_ref[...]
        _acc(dg_ref, dg, first)
        _acc(dwa_ref, dwa, first)
        _acc(dwb_ref, dwb, first)
        _acc(dwc_ref, dwc, first)

    return pl.pallas_call(
        body, name="inproj_bwd", grid=(s // TM,),
        in_specs=[_rows(TM, d), _full(g1), _full(wa), _full(wb), _full(wc),
                  _rows(TM, wa.shape[1]), _rows(TM, wb.shape[1]), _rows(TM, wc.shape[1]), _rows(TM, d)],
        out_specs=[_rows(TM, d), _full(g1), _full(wa), _full(wb), _full(wc)],
        out_shape=[_sds((s, d)), _sds(g1.shape), _sds(wa.shape), _sds(wb.shape), _sds(wc.shape)],
        compiler_params=_cp(("arbitrary",)),
    )(x, g1, wa, wb, wc, dpa, dpb, dpc, dres)


def gm_fwd(pa, vg, ws4, bs, og):
    s = pa.shape[0]
    w = pa.shape[1] // 2

    def body(pa_ref, vg_ref, ws_ref, bs_ref, og_ref, ya_ref):
        bsl = [bs_ref[h] for h in range(4)]
        for j in range(GM_CHUNKS):
            rows = slice(j * CHUNK, (j + 1) * CHUNK)
            ya_ref[rows, :] = _gm_chunk(PLAIN, pa_ref[rows, 0:w], pa_ref[rows, w:2 * w], vg_ref[...], ws_ref[...], bsl,
                                        og_ref[...])

    tm = GM_CHUNKS * CHUNK
    return pl.pallas_call(
        body, name="gm_fwd", grid=(s // tm,),
        in_specs=[_rows(tm, 2 * w), _full(vg), _full(ws4), _full(bs), _full(og)],
        out_specs=_rows(tm, w), out_shape=_sds((s, w)),
        compiler_params=_cp(("parallel",)),
    )(pa, vg, ws4, bs, og)


def gm_bwd(pa, vg, ws4, bs, og, dya):
    s = pa.shape[0]
    w = pa.shape[1] // 2

    def body(pa_ref, vg_ref, ws_ref, bs_ref, og_ref, dya_ref, dpa_ref, dvg_ref, dws_ref, dbs_ref, dog_ref):
        first = pl.program_id(0) == 0
        bsl = [bs_ref[h] for h in range(4)]
        tot = None
        for j in range(GM_CHUNKS):
            rows = slice(j * CHUNK, (j + 1) * CHUNK)
            _, vjp = jax.vjp(functools.partial(_gm_chunk, AD), pa_ref[rows, 0:w], pa_ref[rows, w:2 * w],
                             vg_ref[...], ws_ref[...], bsl, og_ref[...])
            du, dv, *dws = vjp(dya_ref[rows, :])
            dpa_ref[rows, 0:w] = du
            dpa_ref[rows, w:2 * w] = dv
            tot = dws if tot is None else jax.tree.map(jnp.add, tot, dws)
        dvg, dws, dbs, dog = tot
        _acc(dvg_ref, dvg, first)
        _acc(dws_ref, dws, first)
        _acc(dog_ref, dog, first)
        for h in range(4):
            _acc(dbs_ref.at[h], dbs[h], first)

    tm = GM_CHUNKS * CHUNK
    return pl.pallas_call(
        body, name="gm_bwd", grid=(s // tm,),
        in_specs=[_rows(tm, 2 * w), _full(vg), _full(ws4), _full(bs), _full(og), _rows(tm, w)],
        out_specs=[_rows(tm, 2 * w), _full(vg), _full(ws4), _full(bs), _full(og)],
        out_shape=[_sds((s, 2 * w)), _sds(vg.shape), _sds(ws4.shape), _sds(bs.shape), _sds(og.shape)],
        compiler_params=_cp(("arbitrary",)),
    )(pa, vg, ws4, bs, og, dya)


def hg_fwd(pb, lb, og, riders=()):
    s = pb.shape[0]
    w = pb.shape[1] // 4
    tm = HG_CHUNKS * HG_CHUNK
    st_shape = (w // HEAD64, HEAD64, HEAD64)

    def body(pb_ref, lb_ref, og_ref, yb_ref, states_ref, st_ref):
        @pl.when(pl.program_id(0) == 0)
        def _():
            st_ref[...] = jnp.zeros_like(st_ref)

        st = st_ref[...]
        for j in range(HG_CHUNKS):
            rows = slice(j * HG_CHUNK, (j + 1) * HG_CHUNK)
            states_ref[j] = st
            st, y = _hg_chunk(PLAIN, st, pb_ref[rows, 0:w], pb_ref[rows, w:2 * w], pb_ref[rows, 2 * w:3 * w],
                              pb_ref[rows, 3 * w:4 * w], lb_ref[...], og_ref[...])
            yb_ref[rows, :] = y
        st_ref[...] = st

    return _ride(
        body, riders, name="hg_fwd", grid=(s // tm,),
        in_specs=[_rows(tm, 4 * w), _full(lb), _full(og)],
        out_specs=[_rows(tm, w), pl.BlockSpec((HG_CHUNKS,) + st_shape, lambda i: (i, 0, 0, 0))],
        out_shape=[_sds((s, w)), _sds((s // HG_CHUNK,) + st_shape)],
        scratch_shapes=[pltpu.VMEM(st_shape, F32)],
        operands=(pb, lb, og), sem=("arbitrary",))


def hg_bwd(pb, lb, og, states, dyb, riders=()):
    s = pb.shape[0]
    w = pb.shape[1] // 4
    tm = HG_CHUNKS * HG_CHUNK
    nc = s // tm
    st_shape = (w // HEAD64, HEAD64, HEAD64)

    def body(pb_ref, lb_ref, og_ref, states_ref, dyb_ref, dpb_ref, dlb_ref, dog_ref, dst_ref):
        first = pl.program_id(0) == 0

        @pl.when(first)
        def _():
            dst_ref[...] = jnp.zeros_like(dst_ref)

        dst, dlb, dog = dst_ref[...], None, None
        for j in reversed(range(HG_CHUNKS)):
            rows = slice(j * HG_CHUNK, (j + 1) * HG_CHUNK)
            _, vjp = jax.vjp(functools.partial(_hg_chunk, AD), states_ref[j], pb_ref[rows, 0:w], pb_ref[rows, w:2 * w],
                             pb_ref[rows, 2 * w:3 * w], pb_ref[rows, 3 * w:4 * w], lb_ref[...], og_ref[...])
            dst, dq, df, di, dg, dlb_j, dog_j = vjp((dst, dyb_ref[rows, :]))
            dpb_ref[rows, 0:w] = dq
            dpb_ref[rows, w:2 * w] = df
            dpb_ref[rows, 2 * w:3 * w] = di
            dpb_ref[rows, 3 * w:4 * w] = dg
            dlb = dlb_j if dlb is None else dlb + dlb_j
            dog = dog_j if dog is None else dog + dog_j
        dst_ref[...] = dst
        _acc(dlb_ref, dlb, first)
        _acc(dog_ref, dog, first)

    rev = lambda i: (nc - 1 - i, 0)
    return _ride(
        body, riders, name="hg_bwd", grid=(nc,),
        in_specs=[pl.BlockSpec((tm, 4 * w), rev), _full(lb), _full(og),
                  pl.BlockSpec((HG_CHUNKS,) + st_shape, lambda i: (nc - 1 - i, 0, 0, 0)), pl.BlockSpec((tm, w), rev)],
        out_specs=[pl.BlockSpec((tm, 4 * w), rev), _full(lb), _full(og)],
        out_shape=[_sds((s, 4 * w)), _sds(lb.shape), _sds(og.shape)],
        scratch_shapes=[pltpu.VMEM(st_shape, F32)],
        operands=(pb, lb, og, states, dyb), sem=("arbitrary",))


def lower_bounds_fwd(hlb):
    def body(h_ref, o_ref):
        outs = _lower_bounds(*[h_ref[pl.ds(i, 1), :] for i in range(DEPTH)])
        for i in range(DEPTH):
            o_ref[pl.ds(i, 1), :] = outs[i]

    return pl.pallas_call(body, name="lower_bounds_fwd", out_shape=_sds(hlb.shape))(hlb)


def lower_bounds_bwd(hlb, dlbs):
    def body(h_ref, d_ref, o_ref):
        _, vjp = jax.vjp(_lower_bounds, *[h_ref[pl.ds(i, 1), :] for i in range(DEPTH)])
        outs = vjp(tuple(d_ref[pl.ds(i, 1), :] for i in range(DEPTH)))
        for i in range(DEPTH):
            o_ref[pl.ds(i, 1), :] = outs[i]

    return pl.pallas_call(body, name="lower_bounds_bwd", out_shape=_sds(hlb.shape))(hlb, dlbs)


def _mla_pre_args(pc_ref, cos_ref, sa_ref, sb_ref, qag_ref, kvag_ref, qg_ref, kg_ref, wq_ref, wk_ref, wv_ref, cast):
    sl = lambda h: slice(h * SLOT, (h + 1) * SLOT)
    ld = (lambda r, h: r[:, sl(h)].astype(F32)) if cast else (lambda r, h: r[:, sl(h)])
    diff = (pc_ref[:, 0:Q_LORA], pc_ref[:, Q_LORA:Q_LORA + KV_LORA], pc_ref[:, Q_LORA + KV_LORA:Q_LORA + 2 * KV_LORA],
            qag_ref[...], kvag_ref[...], qg_ref[...], kg_ref[...],
            [ld(wq_ref, h) for h in range(MLA_HEADS)], [ld(wk_ref, h) for h in range(MLA_HEADS)],
            [ld(wv_ref, h) for h in range(MLA_HEADS)])
    tables = (cos_ref[...], sa_ref[...], sb_ref[...])
    return diff, tables


def _mla_pre_fn(ops, tables, cq, ckv, kpe, qag, kvag, qg, kg, wq, wk, wv):
    return _mla_pre(ops, cq, ckv, kpe, *tables, qag, kvag, qg, kg, wq, wk, wv)


def mla_pre_fwd(pc, cos_t, sin_a, sin_b, qag, kvag, qg, kg, wq, wk, wv):
    s = pc.shape[0]
    hw = MLA_HEADS * SLOT

    def body(pc_ref, cos_ref, sa_ref, sb_ref, qag_ref, kvag_ref, qg_ref, kg_ref, wq_ref, wk_ref, wv_ref,
             q_ref, k_ref, v_ref):
        diff, tables = _mla_pre_args(pc_ref, cos_ref, sa_ref, sb_ref, qag_ref, kvag_ref, qg_ref, kg_ref,
                                     wq_ref, wk_ref, wv_ref, False)
        qs, ks, vs = _mla_pre_fn(PLAIN, tables, *diff)
        ones_lane = (lax.broadcasted_iota(jnp.int32, (1, SLOT), 1) == V_DIM).astype(F32)
        for h in range(MLA_HEADS):
            q_ref[:, h * SLOT:(h + 1) * SLOT] = qs[h].astype(q_ref.dtype)
            k_ref[:, h * SLOT:(h + 1) * SLOT] = ks[h].astype(k_ref.dtype)
            v_ref[:, h * SLOT:(h + 1) * SLOT] = (vs[h] + ones_lane).astype(v_ref.dtype)

    return pl.pallas_call(
        body, name="mla_pre_fwd", grid=(s // TM,),
        in_specs=[_rows(TM, pc.shape[1]), _rows(TM, SLOT), _rows(TM, SLOT), _rows(TM, SLOT),
                  _full(qag), _full(kvag), _full(qg), _full(kg), _full(wq), _full(wk), _full(wv)],
        out_specs=[_rows(TM, hw)] * 3, out_shape=[_sds((s, hw), MXU_DTYPE)] * 3,
        compiler_params=_cp(("parallel",)),
    )(pc, cos_t, sin_a, sin_b, qag, kvag, qg, kg, wq, wk, wv)


def mla_pre_bwd(pc, cos_t, sin_a, sin_b, qag, kvag, qg, kg, wq, wk, wv, dq, dk, dv):
    s = pc.shape[0]
    hw = MLA_HEADS * SLOT

    def body(pc_ref, cos_ref, sa_ref, sb_ref, qag_ref, kvag_ref, qg_ref, kg_ref, wq_ref, wk_ref, wv_ref,
             dq_ref, dk_ref, dv_ref, dpc_ref, dqag_ref, dkvag_ref, dqg_ref, dkg_ref, dwq_ref, dwk_ref, dwv_ref):
        first = pl.program_id(0) == 0
        diff, tables = _mla_pre_args(pc_ref, cos_ref, sa_ref, sb_ref, qag_ref, kvag_ref, qg_ref, kg_ref,
                                     wq_ref, wk_ref, wv_ref, True)
        _, vjp = jax.vjp(functools.partial(_mla_pre_fn, AD, tables), *diff)
        sl = lambda h: slice(h * SLOT, (h + 1) * SLOT)
        cot = ([dq_ref[:, sl(h)] for h in range(MLA_HEADS)], [dk_ref[:, sl(h)] for h in range(MLA_HEADS)],
               [dv_ref[:, sl(h)] for h in range(MLA_HEADS)])
        dcq, dckv, dkpe, dqag, dkvag, dqg, dkg, dwq, dwk, dwv = vjp(cot)
        dpc_ref[:, 0:Q_LORA] = dcq
        dpc_ref[:, Q_LORA:Q_LORA + KV_LORA] = dckv
        dpc_ref[:, Q_LORA + KV_LORA:Q_LORA + 2 * KV_LORA] = dkpe
        _acc(dqag_ref, dqag, first)
        _acc(dkvag_ref, dkvag, first)
        _acc(dqg_ref, dqg, first)
        _acc(dkg_ref, dkg, first)
        for h in range(MLA_HEADS):
            _acc(dwq_ref.at[:, sl(h)], dwq[h], first)
            _acc(dwk_ref.at[:, sl(h)], dwk[h], first)
            _acc(dwv_ref.at[:, sl(h)], dwv[h], first)

    return pl.pallas_call(
        body, name="mla_pre_bwd", grid=(s // TM,),
        in_specs=[_rows(TM, pc.shape[1]), _rows(TM, SLOT), _rows(TM, SLOT), _rows(TM, SLOT),
                  _full(qag), _full(kvag), _full(qg), _full(kg), _full(wq), _full(wk), _full(wv),
                  _rows(TM, hw), _rows(TM, hw), _rows(TM, hw)],
        out_specs=[_rows(TM, pc.shape[1]), _full(qag), _full(kvag), _full(qg), _full(kg),
                   _full(wq), _full(wk), _full(wv)],
        out_shape=[_sds(pc.shape), _sds(qag.shape), _sds(kvag.shape), _sds(qg.shape), _sds(kg.shape),
                   _sds(wq.shape), _sds(wk.shape), _sds(wv.shape)],
        compiler_params=_cp(("arbitrary",)),
    )(pc, cos_t, sin_a, sin_b, qag, kvag, qg, kg, wq, wk, wv, dq, dk, dv)


ATT_SCALE = QK_DIM ** -0.5
NEG_BIG = -1e30


def attn_fwd(q, k, v, riders=()):
    s = q.shape[0]
    nq = s // TQ
    hp = ATT_HEADS_PER_STEP
    sl = lambda j: slice(j * SLOT, (j + 1) * SLOT)

    wide = ATT_WIDE // TQ

    def body(q_ref, k_ref, v_ref, o_ref, lse_ref):
        qi = pl.program_id(1)
        lane = lax.broadcasted_iota(jnp.int32, (1, SLOT), 1)
        qs = [q_ref[:, sl(j)] for j in range(hp)]

        def step(ki, carry, n_tiles, masked):
            rk = pl.ds(pl.multiple_of(ki * TQ, TQ), n_tiles * TQ)
            if masked:
                row = lax.broadcasted_iota(jnp.int32, (TQ, n_tiles * TQ), 0) + (n_tiles - 1) * TQ
                col = lax.broadcasted_iota(jnp.int32, (TQ, n_tiles * TQ), 1)
            out = []
            for j in range(hp):
                m, acc = carry[j]
                sc = _dot(qs[j], k_ref[rk, sl(j)], NT) * ATT_SCALE
                if masked:
                    sc = jnp.where(row >= col, sc, NEG_BIG)
                m_new = jnp.maximum(m, jnp.max(sc, axis=-1, keepdims=True))
                acc = jnp.exp(m - m_new) * acc + _dot(jnp.exp(sc - m_new), v_ref[rk, sl(j)], NN)
                out.append((m_new, acc))
            return tuple(out)

        def tail_single(cr):
            cr = lax.fori_loop(n_wide * wide, qi, lambda ki, c: step(ki, c, 1, False), cr)
            return step(qi, cr, 1, True)

        n_wide = qi // wide
        init = tuple((jnp.full((TQ, 1), NEG_BIG, F32), jnp.zeros((TQ, SLOT), F32)) for _ in range(hp))
        carry = lax.fori_loop(0, n_wide, lambda kw, cr: step(kw * wide, cr, wide, False), init)
        carry = lax.cond(qi % wide == wide - 1, lambda cr: step(qi - (wide - 1), cr, wide, True), tail_single, carry)
        for j in range(hp):
            m, acc = carry[j]
            l = jnp.sum(jnp.where(lane == V_DIM, acc, 0.0), axis=-1, keepdims=True)
            o_ref[:, sl(j)] = jnp.where(lane < V_DIM, acc / l, 0.0)
            lse_ref[j] = m + jnp.log(l)

    head_col = pl.BlockSpec((s, hp * SLOT), lambda g, i: (0, g))
    tile = pl.BlockSpec((TQ, hp * SLOT), lambda g, i: (i, g))
    return _ride(
        body, riders, name="attn_fwd", grid=(MLA_HEADS // hp, nq),
        in_specs=[tile, head_col, head_col],
        out_specs=[tile, pl.BlockSpec((hp, TQ, 1), lambda g, i: (g, i, 0))],
        out_shape=[_sds((s, MLA_HEADS * SLOT)), _sds((MLA_HEADS, s, 1))],
        operands=(q, k, v), sem=("parallel", "parallel"))


def attn_bwd(q, k, v, o, do, lse, riders=()):
    s = q.shape[0]
    nq = s // TQ
    hp = ATT_HEADS_PER_STEP
    sl = lambda j: slice(j * SLOT, (j + 1) * SLOT)
    wide = ATT_WIDE // TQ

    def body(q_ref, k_ref, v_ref, o_ref, do_ref, lse_ref, dq_ref, dk_ref, dv_ref, delta_ref):
        ki = pl.program_id(1)

        @pl.when(ki == 0)
        def _():
            dq_ref[...] = jnp.zeros_like(dq_ref)

            def prep(i, c):
                rows = pl.ds(pl.multiple_of(i * TQ, TQ), TQ)
                for j in range(hp):
                    delta_ref[j, rows, :] = jnp.sum(do_ref[rows, sl(j)] * o_ref[rows, sl(j)], axis=-1, keepdims=True)
                return c

            lax.fori_loop(0, nq, prep, 0)

        kks = [k_ref[:, sl(j)] for j in range(hp)]
        vvs = [v_ref[:, sl(j)] for j in range(hp)]

        def step(qi, carry, n_tiles, masked):
            rq = pl.ds(pl.multiple_of(qi * TQ, TQ), n_tiles * TQ)
            if masked:
                row = lax.broadcasted_iota(jnp.int32, (n_tiles * TQ, TQ), 0)
                col = lax.broadcasted_iota(jnp.int32, (n_tiles * TQ, TQ), 1)
            out = []
            for j in range(hp):
                dk, dv = carry[j]
                qq = q_ref[rq, sl(j)]
                dd = do_ref[rq, sl(j)]
                sc = _dot(qq, kks[j], NT) * ATT_SCALE
                if masked:
                    sc = jnp.where(row >= col, sc, NEG_BIG)
                p = jnp.exp(sc - lse_ref[j, rq, :])
                dv = dv + _dot(p, dd, TN)
                ds = p * (_dot(dd, vvs[j], NT) - delta_ref[j, rq, :]) * ATT_SCALE
                dk = dk + _dot(ds, qq, TN)
                dq_ref[rq, sl(j)] = dq_ref[rq, sl(j)] + _dot(ds, kks[j], NN)
                out.append((dk, dv))
            return tuple(out)

        def head_single(cr):
            cr = step(ki, cr, 1, True)
            return lax.fori_loop(ki + 1, first_wide * wide, lambda qi, c: step(qi, c, 1, False), cr)

        zero = jnp.zeros((TQ, SLOT), F32)
        first_wide = (ki + wide) // wide
        carry = tuple((zero, zero) for _ in range(hp))
        carry = lax.cond(ki % wide == 0, lambda cr: step(ki, cr, wide, True), head_single, carry)
        carry = lax.fori_loop(first_wide, nq // wide, lambda qw, cr: step(qw * wide, cr, wide, False), carry)
        for j in range(hp):
            dk_ref[:, sl(j)] = carry[j][0]
            dv_ref[:, sl(j)] = carry[j][1]

    head_col = pl.BlockSpec((s, hp * SLOT), lambda g, i: (0, g))
    tile = pl.BlockSpec((TQ, hp * SLOT), lambda g, i: (i, g))
    return _ride(
        body, riders, name="attn_bwd", grid=(MLA_HEADS // hp, nq),
        in_specs=[head_col, tile, tile, head_col, head_col, pl.BlockSpec((hp, s, 1), lambda g, i: (g, 0, 0))],
        out_specs=[head_col, tile, tile],
        out_shape=[_sds((s, MLA_HEADS * SLOT))] * 3,
        scratch_shapes=[pltpu.VMEM((hp, s, 1), F32)],
        operands=(q, k, v, o, do, lse), sem=("arbitrary", "arbitrary"))


def _outproj_args(ya_ref, yb_ref, o_ref, mog_ref, woa_ref, wob_ref, woc_ref, cast):
    sl = lambda h: slice(h * SLOT, (h + 1) * SLOT)
    ldw = (lambda r: r[...].astype(F32)) if cast else (lambda r: r[...])
    ldc = (lambda h: woc_ref[sl(h), :].astype(F32)) if cast else (lambda h: woc_ref[sl(h), :])
    return (ya_ref[...], yb_ref[...], [o_ref[:, sl(h)] for h in range(MLA_HEADS)],
            [mog_ref[:, sl(h)] for h in range(MLA_HEADS)], ldw(woa_ref), ldw(wob_ref),
            [ldc(h) for h in range(MLA_HEADS)])


def outproj_fwd(x, ya, yb, o, mog, woa, wob, woc):
    s, d = x.shape

    def body(x_ref, ya_ref, yb_ref, o_ref, mog_ref, woa_ref, wob_ref, woc_ref, x1_ref):
        x1_ref[...] = _outproj(PLAIN, x_ref[...], *_outproj_args(ya_ref, yb_ref, o_ref, mog_ref, woa_ref, wob_ref,
                                                                  woc_ref, False))

    return pl.pallas_call(
        body, name="outproj_fwd", grid=(s // TM,),
        in_specs=[_rows(TM, d), _rows(TM, ya.shape[1]), _rows(TM, yb.shape[1]), _rows(TM, o.shape[1]),
                  _full(mog), _full(woa), _full(wob), _full(woc)],
        out_specs=_rows(TM, d), out_shape=_sds((s, d)),
        compiler_params=_cp(("parallel",)),
    )(x, ya, yb, o, mog, woa, wob, woc)


def outproj_bwd(ya, yb, o, mog, woa, wob, woc, dx2, dx1p, riders=()):
    s, d = dx2.shape
    npart = dx1p.shape[0]

    def body(ya_ref, yb_ref, o_ref, mog_ref, woa_ref, wob_ref, woc_ref, dx2_ref, dx1p_ref,
             dx1_ref, dya_ref, dyb_ref, do_ref, dmog_ref, dwoa_ref, dwob_ref, dwoc_ref):
        first = pl.program_id(0) == 0
        sl = lambda h: slice(h * SLOT, (h + 1) * SLOT)
        dx1 = dx2_ref[...]
        for p in range(npart):
            dx1 = dx1 + dx1p_ref[p]
        dx1_ref[...] = dx1
        args = _outproj_args(ya_ref, yb_ref, o_ref, mog_ref, woa_ref, wob_ref, woc_ref, True)
        _, vjp = jax.vjp(lambda *a: _outproj(AD, jnp.zeros_like(dx1), *a), *args)
        dya, dyb, do, dmog, dwoa, dwob, dwoc = vjp(dx1)
        dya_ref[...] = dya
        dyb_ref[...] = dyb
        _acc(dwoa_ref, dwoa, first)
        _acc(dwob_ref, dwob, first)
        for h in range(MLA_HEADS):
            do_ref[:, sl(h)] = do[h]
            _acc(dmog_ref.at[:, sl(h)], dmog[h], first)
            _acc(dwoc_ref.at[sl(h), :], dwoc[h], first)

    return _ride(
        body, riders, name="outproj_bwd", grid=(s // TM,),
        in_specs=[_rows(TM, ya.shape[1]), _rows(TM, yb.shape[1]), _rows(TM, o.shape[1]),
                  _full(mog), _full(woa), _full(wob), _full(woc), _rows(TM, d),
                  pl.BlockSpec((npart, TM, d), lambda i: (0, i, 0))],
        out_specs=[_rows(TM, d), _rows(TM, ya.shape[1]), _rows(TM, yb.shape[1]), _rows(TM, o.shape[1]),
                   _full(mog), _full(woa), _full(wob), _full(woc)],
        out_shape=[_sds((s, d)), _sds(ya.shape), _sds(yb.shape), _sds(o.shape),
                   _sds(mog.shape), _sds(woa.shape), _sds(wob.shape), _sds(woc.shape)],
        operands=(ya, yb, o, mog, woa, wob, woc, dx2, dx1p), sem=("arbitrary",))


def ffn_fwd(x1, g2, w1, w2, riders=()):
    s, d = x1.shape
    npart, _, fs = w1.shape

    def body(x1_ref, g_ref, w1_ref, w2_ref, x2_ref):
        p = pl.program_id(1)
        x1v = x1_ref[...]
        part = _ffn_part(PLAIN, x1v, g_ref[...], w1_ref[...], w2_ref[...])

        @pl.when(p == 0)
        def _():
            x2_ref[...] = x1v + part

        @pl.when(p != 0)
        def _():
            x2_ref[...] = x2_ref[...] + part

    tm = min(2 * TM_FFN, s)
    return _ride(
        body, riders, name="ffn_fwd", grid=(s // tm, npart),
        in_specs=[pl.BlockSpec((tm, d), lambda i, p: (i, 0)), pl.BlockSpec(g2.shape, lambda i, p: (0, 0)),
                  pl.BlockSpec((None, d, fs), lambda i, p: (p, 0, 0)), pl.BlockSpec((None, fs, d), lambda i, p: (p, 0, 0))],
        out_specs=pl.BlockSpec((tm, d), lambda i, p: (i, 0)), out_shape=_sds((s, d)),
        operands=(x1, g2, w1, w2), sem=("parallel", "arbitrary"))


def ffn_bwd(x1, g2, w1, w2, dx2, riders=()):
    s, d = x1.shape
    npart, _, fs = w1.shape

    def body(x1_ref, g_ref, w1_ref, w2_ref, dx2_ref, dx1p_ref, dg_ref, dw1_ref, dw2_ref):
        p = pl.program_id(0)
        i = pl.program_id(1)
        _, vjp = jax.vjp(functools.partial(_ffn_part, AD), x1_ref[...], g_ref[...], _f32(w1_ref), _f32(w2_ref))
        dx1, dg, dw1, dw2 = vjp(dx2_ref[...])
        dx1p_ref[...] = dx1
        _acc(dg_ref, dg, (p == 0) & (i == 0))
        _acc(dw1_ref, dw1, i == 0)
        _acc(dw2_ref, dw2, i == 0)

    tm = TM_FFN
    return _ride(
        body, riders, name="ffn_bwd", grid=(npart, s // tm),
        in_specs=[pl.BlockSpec((tm, d), lambda p, i: (i, 0)), pl.BlockSpec(g2.shape, lambda p, i: (0, 0)),
                  pl.BlockSpec((None, d, fs), lambda p, i: (p, 0, 0)), pl.BlockSpec((None, fs, d), lambda p, i: (p, 0, 0)),
                  pl.BlockSpec((tm, d), lambda p, i: (i, 0))],
        out_specs=[pl.BlockSpec((None, tm, d), lambda p, i: (p, i, 0)), pl.BlockSpec(g2.shape, lambda p, i: (0, 0)),
                   pl.BlockSpec((None, d, fs), lambda p, i: (p, 0, 0)), pl.BlockSpec((None, fs, d), lambda p, i: (p, 0, 0))],
        out_shape=[_sds((npart, s, d)), _sds(g2.shape), _sds(w1.shape), _sds(w2.shape)],
        operands=(x1, g2, w1, w2, dx2), sem=("arbitrary", "arbitrary"))


def loss_head(y, target):
    s, d = y.shape

    def body(y_ref, t_ref, dy_ref, loss_ref):
        err = y_ref[...] - t_ref[...]
        dy_ref[...] = err * (1.0 / d)
        part = jnp.sum(jnp.sum(err * err, axis=-1, keepdims=True), axis=0, keepdims=True) * (0.5 / d)
        _acc(loss_ref, jnp.broadcast_to(part, loss_ref.shape), pl.program_id(0) == 0)

    return pl.pallas_call(
        body, name="loss_head", grid=(s // TM,),
        in_specs=[_rows(TM, d), _rows(TM, d)],
        out_specs=[_rows(TM, d), pl.BlockSpec((1, SLOT), lambda i: (0, 0))],
        out_shape=[_sds((s, d)), _sds((1, SLOT))],
        compiler_params=_cp(("arbitrary",)),
    )(y, target)


def _row_block(r):
    for b in (512, 256, 128, 64, 32, 16, 8):
        if r % b == 0:
            return b
    return r


def sum_cores(arrs, gots, half, me):
    n = len(arrs)

    def body(sp_ref, *refs):
        for i in range(n):
            a_ref, g_ref, wire_ref, own_ref = refs[i], refs[n + i], refs[2 * n + i], refs[3 * n + i]
            tot = a_ref[...] + g_ref[...]
            wire_ref[...] = tot.astype(wire_ref.dtype)

            @pl.when(pl.program_id(0) == sp_ref[1])
            def _(own_ref=own_ref, tot=tot):
                own_ref[...] = tot

    shapes = [a.shape[2:] for a in arrs]
    grid_spec = pltpu.PrefetchScalarGridSpec(
        num_scalar_prefetch=1, grid=(N_CHIPS,),
        in_specs=[pl.BlockSpec((None, None) + sh, lambda p, sp: (p, sp[0], 0, 0)) for sh in shapes]
        + [pl.BlockSpec((None,) + sh, lambda p, sp: (p, 0, 0)) for sh in shapes],
        out_specs=[pl.BlockSpec((None,) + sh, lambda p, sp: (p, 0, 0)) for sh in shapes]
        + [pl.BlockSpec(sh, lambda p, sp: (0, 0)) for sh in shapes])
    outs = pl.pallas_call(body, name="sum_cores", grid_spec=grid_spec,
                          out_shape=[_sds((N_CHIPS,) + sh, BF16) for sh in shapes] + [_sds(sh) for sh in shapes],
                          compiler_params=_cp(("arbitrary",)))(jnp.stack([half, me]).astype(jnp.int32), *arrs, *gots)
    return outs[:n], outs[n:]


SUM_STEPS = 4


def sum_chips(owns, recvs, half):
    n = len(owns)

    def body(sp_ref, *refs):
        del sp_ref
        for i in range(n):
            own_ref, out_ref = refs[4 * i], refs[4 * n + i]
            r0, r1, r2 = (refs[4 * i + 1 + j][...].astype(F32) for j in range(3))
            out_ref[...] = ((own_ref[...] + r0) + r1) + r2

    in_specs, out_specs, operands = [], [], []
    for own, recv in zip(owns, recvs):
        r, c = own.shape
        br = r // SUM_STEPS
        in_specs.append(pl.BlockSpec((br, c), lambda i, sp: (i, 0)))
        in_specs += [pl.BlockSpec((None, br, c), functools.partial(lambda i, sp, j: (j, i, 0), j=j)) for j in range(3)]
        out_specs.append(pl.BlockSpec((None, br, c), lambda i, sp: (sp[0], i, 0)))
        operands += [own, recv, recv, recv]
    grid_spec = pltpu.PrefetchScalarGridSpec(num_scalar_prefetch=1, grid=(SUM_STEPS,), in_specs=in_specs, out_specs=out_specs)
    return pl.pallas_call(body, name="sum_chips", grid_spec=grid_spec, out_shape=[_sds((2,) + o.shape) for o in owns],
                          compiler_params=_cp(("parallel",)))(half.reshape(1).astype(jnp.int32), *operands)


def adamw(w, g, m, v, name, riders=()):
    r, c = w.shape
    br = _row_block(r)
    c1 = 1.0 / (1.0 - ADAM_B1 ** ADAM_STEP)
    c2 = 1.0 / (1.0 - ADAM_B2 ** ADAM_STEP)

    def body(w_ref, g_ref, m_ref, v_ref, d_ref, nm_ref, nv_ref):
        gg = g_ref[...]
        nm = ADAM_B1 * m_ref[...] + (1.0 - ADAM_B1) * gg
        nv = ADAM_B2 * v_ref[...] + (1.0 - ADAM_B2) * (gg * gg)
        d_ref[...] = -ADAM_LR * ((nm * c1) / (jnp.sqrt(nv * c2) + ADAM_EPS) + ADAM_WD * w_ref[...])
        nm_ref[...] = nm
        nv_ref[...] = nv

    return _ride(body, riders, name=name, grid=(r // br,), in_specs=[_rows(br, c)] * 4, out_specs=[_rows(br, c)] * 3,
                 out_shape=[_sds((r, c))] * 3, operands=(w, g, m, v), sem=("parallel",))


def _place():
    x, y, c = lax.axis_index("x"), lax.axis_index("y"), lax.axis_index("c")
    chips = [(1 - x, y), (x, 1 - y), (1 - x, 1 - y)]
    return x, y, c, chips


def _remote(src, dst, send_sem, recv_sem, to):
    return pltpu.make_async_remote_copy(src_ref=src, dst_ref=dst, send_sem=send_sem, recv_sem=recv_sem,
                                        device_id=to, device_id_type=MESH)


def gather_rider(arrs):
    n = len(arrs)
    me_chip = 2 * lax.axis_index("x") + lax.axis_index("y")
    bufs = [lax.dynamic_update_index_in_dim(lax.empty((N_CHIPS,) + a.shape, a.dtype), a, me_chip, 0) for a in arrs]

    def plan(ins, outs, sems):
        send_sems, recv_sems = sems
        x, y, c, chips = _place()
        me = 2 * x + y
        half, other, sibling = pl.ds(2 * c, 2), pl.ds(2 - 2 * c, 2), (x, y, 1 - c)
        cp = lambda i, k, src, dst, to: _remote(src, dst, send_sems.at[i, k], recv_sems.at[i, k], to)
        pairs = [(i, j, cx, cy) for i in range(n) for j, (cx, cy) in enumerate(chips)]
        blk = lambda i, cx, cy, part: outs[i].at[2 * cx + cy, part]
        first = lambda: [cp(i, j, ins[i].at[half], outs[i].at[me, half], (cx, cy, c)) for i, j, cx, cy in pairs]
        landed = lambda: [cp(i, j, blk(i, cx, cy, half), blk(i, cx, cy, half), (cx, cy, c)) for i, j, cx, cy in pairs]
        passed = lambda: [cp(i, 3 + j, blk(i, cx, cy, half), blk(i, cx, cy, half), sibling) for i, j, cx, cy in pairs]
        from_sibling = lambda: [cp(i, 3 + j, blk(i, cx, cy, other), blk(i, cx, cy, other), sibling) for i, j, cx, cy in pairs]
        return first, landed, passed, from_sibling

    def start(ins, outs, sems):
        for cp in plan(ins, outs, sems)[0]():
            cp.start()

    def finish(ins, outs, sems):
        first, landed, passed, from_sibling = plan(ins, outs, sems)
        forwards = passed()
        for a, b in zip(landed(), forwards):
            a.wait_recv()
            b.start()
        for cp in from_sibling():
            cp.wait_recv()
        for cp in first() + forwards:
            cp.wait_send()

    return Rider(list(arrs) + bufs, [_sds((N_CHIPS,) + a.shape, a.dtype) for a in arrs], {n + i: i for i in range(n)},
                 [pltpu.SemaphoreType.DMA((n, 6)), pltpu.SemaphoreType.DMA((n, 6))], start, finish)


class Reducer:
    def __init__(self, arrs):
        self.a = list(arrs)
        self.n = len(self.a)
        self.c = lax.axis_index("c")
        self.me = 2 * lax.axis_index("x") + lax.axis_index("y")

    def swap_rider(self):
        n = self.n

        def plan(ins, outs, sems):
            x, y, c, _ = _place()
            return [_remote(ins[i].at[p, 1 - c], outs[i].at[p], sems[0].at[i, p], sems[1].at[i, p], (x, y, 1 - c))
                    for i in range(n) for p in range(N_CHIPS)]

        return Rider(self.a, [_sds((N_CHIPS,) + a.shape[2:]) for a in self.a], {},
                     [pltpu.SemaphoreType.DMA((n, N_CHIPS)), pltpu.SemaphoreType.DMA((n, N_CHIPS))],
                     lambda *r: [cp.start() for cp in plan(*r)], lambda *r: [cp.wait() for cp in plan(*r)])

    def after_swap(self, got):
        self.wire, self.own = sum_cores(self.a, got, self.c, self.me)

    def scatter_rider(self):
        n = self.n

        def plan(ins, outs, sems):
            x, y, c, chips = _place()
            return [_remote(ins[i].at[2 * cx + cy], outs[i].at[j], sems[0].at[i, j], sems[1].at[i, j], (cx, cy, c))
                    for i in range(n) for j, (cx, cy) in enumerate(chips)]

        return Rider(self.wire, [_sds((3,) + w.shape[1:], w.dtype) for w in self.wire], {},
                     [pltpu.SemaphoreType.DMA((n, 3)), pltpu.SemaphoreType.DMA((n, 3))],
                     lambda *r: [cp.start() for cp in plan(*r)], lambda *r: [cp.wait() for cp in plan(*r)])

    def after_scatter(self, recv):
        self.full = sum_chips(self.own, recv, self.c)

    def share_rider(self):
        n = self.n

        def plan(ins, outs, sems):
            x, y, c, _ = _place()
            return [_remote(ins[i].at[c], outs[i].at[c], sems[0].at[i], sems[1].at[i], (x, y, 1 - c)) for i in range(n)]

        return Rider(self.full, [_sds(f.shape) for f in self.full], {i: i for i in range(n)},
                     [pltpu.SemaphoreType.DMA((n,)), pltpu.SemaphoreType.DMA((n,))],
                     lambda *r: [cp.start() for cp in plan(*r)], lambda *r: [cp.wait() for cp in plan(*r)])

    def run(self):
        self.after_swap(run_rider(self.swap_rider(), "swap_halves"))
        self.after_scatter(run_rider(self.scatter_rider(), "scatter_chips"))
        return run_rider(self.share_rider(), "share_halves")


def _pad_slots(a, live):
    lead = a.shape[:-1]
    a = a.reshape(lead + (MLA_HEADS, live))
    a = jnp.pad(a, [(0, 0)] * len(lead) + [(0, 0), (0, SLOT - live)])
    return a.reshape(lead + (MLA_HEADS * SLOT,))


def _unpad_slots(a, live):
    lead = a.shape[:-1]
    return a.reshape(lead + (MLA_HEADS, SLOT))[..., :live].reshape(lead + (MLA_HEADS * live,))


def _rope_tables(positions, s):
    half = QK_ROPE // 2
    inv_freq = ROPE_THETA ** (-jnp.arange(half, dtype=F32) / half)
    ang = positions.reshape(s).astype(F32)[:, None] * inv_freq[None, :]
    cos, sin = jnp.cos(ang), jnp.sin(ang)
    one = jnp.ones((s, QK_NOPE), F32)
    z64, z16, z32 = jnp.zeros((s, QK_NOPE), F32), jnp.zeros((s, half), F32), jnp.zeros((s, SLOT - QK_DIM), F32)
    cos_t = jnp.concatenate([one, cos, cos, z32], axis=1)
    sin_a = jnp.concatenate([z64, -sin, z16, z32], axis=1)
    sin_b = jnp.concatenate([z64, z16, sin, z32], axis=1)
    return cos_t, sin_a, sin_b


def _out_weights(full):
    w_out = jnp.concatenate([full["w_out"][p] for p in range(N_CHIPS)], axis=0)
    woc = w_out[512:].reshape(MLA_HEADS, V_DIM, D_MODEL)
    woc = jnp.pad(woc, ((0, 0), (0, SLOT - V_DIM), (0, 0))).reshape(MLA_HEADS * SLOT, D_MODEL)
    return dict(woa=w_out[:256], wob=w_out[256:512], woc=woc)


def _layer_weights(full, small, l):
    w_in = jnp.concatenate([full["w_in"][p] for p in range(N_CHIPS)], axis=1)
    wc = jnp.pad(w_in[:, 1536:], ((0, 0), (0, 512 - (w_in.shape[1] - 1536))))
    w_uq = jnp.concatenate([full["mla_w_uq"][p] for p in range(N_CHIPS)], axis=1)
    w_ukv = jnp.concatenate([full["mla_w_ukv"][p] for p in range(N_CHIPS)], axis=1)
    ukv = w_ukv.reshape(KV_LORA, MLA_HEADS, QK_NOPE + V_DIM)
    row = lambda a: a.reshape(1, -1)
    return dict(
        g1=row(small["norm1_gain"][l]), wa=w_in[:, :512], wb=w_in[:, 512:1536], wc=wc,
        vg=row(small["gm_v_gain"][l]), ws=small["gm_w_s"][l], bs=small["gm_b_s"][l].reshape(4, CHUNK, 1),
        gog=row(small["gm_out_gain"][l]), hog=small["hg_out_gain"][l].reshape(-1, 1),
        qag=row(small["mla_q_a_gain"][l]), kvag=row(small["mla_kv_a_gain"][l]),
        qg=row(jnp.pad(small["mla_q_gain"][l], (0, SLOT - QK_DIM))), kg=row(jnp.pad(small["mla_k_gain"][l], (0, SLOT - QK_DIM))),
        wq=_pad_slots(w_uq, QK_DIM), wk=_pad_slots(ukv[..., :QK_NOPE].reshape(KV_LORA, -1), QK_NOPE),
        wv=_pad_slots(ukv[..., QK_NOPE:].reshape(KV_LORA, -1), V_DIM),
        mog=row(_pad_slots(small["mla_out_gain"][l], V_DIM)),
        g2=row(small["norm2_gain"][l]),
    )


def _shard_cols(a):
    r, c4 = a.shape
    return a.reshape(r, N_CHIPS, c4 // N_CHIPS).transpose(1, 0, 2)


def local_step(x, positions, target, small, comm):
    s = x.shape[0]
    cos_t, sin_a, sin_b = _rope_tables(positions, s)
    lbs = lower_bounds_fwd(small["hg_lower_bound"])
    lw, saved = [], []
    for l in range(DEPTH):
        w = _layer_weights(comm.part(l, "in"), small, l)
        lw.append(w)
        lb = lbs[l].reshape(-1, 1)
        pa, pb, pc = inproj_fwd(x, w["g1"], w["wa"], w["wb"], w["wc"])
        ya = gm_fwd(pa, w["vg"], w["ws"], w["bs"], w["gog"])
        (yb, states), got = hg_fwd(pb, lb, w["hog"], [comm.gather_rider(l, "ff1")])
        comm.gathered(l, "ff1", got[0])
        q, k, v = mla_pre_fwd(pc, cos_t, sin_a, sin_b, w["qag"], w["kvag"], w["qg"], w["kg"], w["wq"], w["wk"], w["wv"])
        (o, lse), got = attn_fwd(q, k, v, [comm.gather_rider(l, "ff2"), comm.gather_rider(l, "out")])
        comm.gathered(l, "ff2", got[0])
        comm.gathered(l, "out", got[1])
        w.update(_out_weights(comm.part(l, "out")))
        x1 = outproj_fwd(x, ya, yb, o, w["mog"], w["woa"], w["wob"], w["woc"])
        w["w1"], w["w2"] = comm.part(l, "ff1")["w_ff1"], comm.part(l, "ff2")["w_ff2"]
        rider = comm.gather_rider(l + 1, "in") if l + 1 < DEPTH else None
        x2, got = ffn_fwd(x1, w["g2"], w["w1"], w["w2"], [rider])
        comm.gathered(l + 1, "in", got[0])
        saved.append(dict(x=x, pa=pa, pb=pb, pc=pc, ya=ya, yb=yb, states=states, q=q, k=k, v=v, o=o, lse=lse, x1=x1, lb=lb))
        x = x2
    dx, loss_part = loss_head(x, target)
    groups = [dict() for _ in range(DEPTH)]
    sm = {n: [None] * DEPTH for n in ("norm1_gain", "gm_v_gain", "gm_w_s", "gm_b_s", "gm_out_gain", "hg_out_gain",
                                       "mla_q_a_gain", "mla_kv_a_gain", "mla_q_gain", "mla_k_gain", "mla_out_gain",
                                       "norm2_gain")}
    dlbs = [None] * DEPTH
    halves = lambda g: g.reshape(N_CHIPS, 2, g.shape[1] // 2, g.shape[2])
    take = lambda red, f: None if red is None else f(red)
    red_mix = None
    for l in reversed(range(DEPTH)):
        w, a = lw[l], saved[l]
        (dx1p, dg2, dw1, dw2), got = ffn_bwd(a["x1"], w["g2"], w["w1"], w["w2"], dx, [take(red_mix, Reducer.swap_rider)])
        if red_mix:
            red_mix.after_swap(got[0])
        ffn_arrs = [halves(dw1), halves(dw2)]
        red_ffn = comm.reducer(ffn_arrs)
        (dx1, dya, dyb, do, dmog, dwoa, dwob, dwoc), got = outproj_bwd(
            a["ya"], a["yb"], a["o"], w["mog"], w["woa"], w["wob"], w["woc"], dx, dx1p,
            [take(red_ffn, Reducer.swap_rider), take(red_mix, Reducer.scatter_rider)])
        if red_ffn:
            red_ffn.after_swap(got[0])
        if red_mix:
            red_mix.after_scatter(got[1])
        (dq, dk, dv), got = attn_bwd(a["q"], a["k"], a["v"], a["o"], do, a["lse"],
                                     [take(red_ffn, Reducer.scatter_rider), take(red_mix, Reducer.share_rider)])
        if red_ffn:
            red_ffn.after_scatter(got[0])
        if red_mix:
            groups[l + 1].update(zip(MIX, got[1]))
        dpc, dqag, dkvag, dqg, dkg, dwq, dwk, dwv = mla_pre_bwd(a["pc"], cos_t, sin_a, sin_b, w["qag"], w["kvag"], w["qg"],
                                                                  w["kg"], w["wq"], w["wk"], w["wv"], dq, dk, dv)
        (dpb, dlb, dhog), got = hg_bwd(a["pb"], a["lb"], w["hog"], a["states"], dyb, [take(red_ffn, Reducer.share_rider)])
        groups[l].update(zip(FFN, got[0] if red_ffn else ffn_arrs))
        dpa, dvg, dws, dbs, dgog = gm_bwd(a["pa"], w["vg"], w["ws"], w["bs"], w["gog"], dya)
        dx, dg1, dwa, dwb, dwc = inproj_bwd(a["x"], w["g1"], w["wa"], w["wb"], w["wc"], dpa, dpb, dpc, dx1)
        dukv = jnp.concatenate([dwk.reshape(KV_LORA, MLA_HEADS, SLOT)[..., :QK_NOPE],
                                dwv.reshape(KV_LORA, MLA_HEADS, SLOT)[..., :V_DIM]], axis=-1)
        dwo = jnp.concatenate([dwoa, dwob, dwoc.reshape(MLA_HEADS, SLOT, D_MODEL)[:, :V_DIM].reshape(-1, D_MODEL)], axis=0)
        mix_arrs = [halves(_shard_cols(jnp.concatenate([dwa, dwb, dwc[:, :1952 - 1536]], axis=1))),
                    halves(_shard_cols(_unpad_slots(dwq, QK_DIM))), halves(_shard_cols(dukv.reshape(KV_LORA, -1))),
                    halves(dwo.reshape(N_CHIPS, -1, D_MODEL))]
        red_mix = comm.reducer(mix_arrs) if l > 0 else None
        if red_mix is None:
            groups[l].update(zip(MIX, mix_arrs))
        sm["norm1_gain"][l] = dg1[0]
        sm["gm_v_gain"][l] = dvg[0]
        sm["gm_w_s"][l] = dws
        sm["gm_b_s"][l] = dbs[..., 0]
        sm["gm_out_gain"][l] = dgog[0]
        sm["hg_out_gain"][l] = dhog[:, 0]
        sm["mla_q_a_gain"][l] = dqag[0]
        sm["mla_kv_a_gain"][l] = dkvag[0]
        sm["mla_q_gain"][l] = dqg[0, :QK_DIM]
        sm["mla_k_gain"][l] = dkg[0, :QK_DIM]
        sm["mla_out_gain"][l] = _unpad_slots(dmog[0], V_DIM)
        sm["norm2_gain"][l] = dg2[0]
        dlbs[l] = dlb[:, 0]
    sm["hg_lower_bound"] = [lower_bounds_bwd(small["hg_lower_bound"], jnp.stack(dlbs))]
    return loss_part, dx, groups, sm


MIX = ("w_in", "mla_w_uq", "mla_w_ukv", "w_out")
FFN = ("w_ff1", "w_ff2")
BIG = MIX + FFN
PARTS = {"in": ("w_in", "mla_w_uq", "mla_w_ukv"), "out": ("w_out",), "ff1": ("w_ff1",), "ff2": ("w_ff2",)}
SMALL = ("norm1_gain", "gm_v_gain", "gm_w_s", "gm_b_s", "gm_out_gain", "hg_lower_bound", "hg_out_gain",
         "mla_q_a_gain", "mla_kv_a_gain", "mla_q_gain", "mla_k_gain", "mla_out_gain", "norm2_gain")
ORDER = ("norm1_gain", "w_in", "gm_v_gain", "gm_w_s", "gm_b_s", "gm_out_gain", "hg_lower_bound", "hg_out_gain",
         "mla_q_a_gain", "mla_w_uq", "mla_kv_a_gain", "mla_w_ukv", "mla_q_gain", "mla_k_gain", "mla_out_gain",
         "w_out", "norm2_gain", "w_ff1", "w_ff2")
PACK_ROWS = 320


def _pack(pieces):
    flat = jnp.concatenate([a.reshape(-1) for a in pieces])
    total = 2 * N_CHIPS * PACK_ROWS * SLOT
    return jnp.pad(flat, (0, total - flat.shape[0]))


def _unpack(flat, shapes):
    out, off = [], 0
    for sh in shapes:
        size = 1
        for d in sh:
            size *= d
        out.append(flat[off:off + size].reshape(sh))
        off += size
    return out


class ChipComm:
    def __init__(self, shards):
        self.shards = shards
        self.full = {}

    def gather_rider(self, l, part):
        return gather_rider([self.shards[n][l].astype(MXU_DTYPE).reshape(4, self.shards[n].shape[1] // 4, -1)
                             for n in PARTS[part]])

    def gathered(self, l, part, outs):
        if outs is not None:
            self.full[l, part] = {n: o.reshape((N_CHIPS,) + self.shards[n].shape[1:]) for n, o in zip(PARTS[part], outs)}

    def part(self, l, part):
        if (l, part) not in self.full:
            self.gathered(l, part, run_rider(self.gather_rider(l, part), "gather_weights"))
        return self.full[l, part]

    def reducer(self, arrs):
        return Reducer(arrs)


def kernel(x, positions, norm1_gain, w_in, gm_v_gain, gm_w_s, gm_b_s, gm_out_gain, hg_lower_bound, hg_out_gain, mla_q_a_gain, mla_w_uq, mla_kv_a_gain, mla_w_ukv, mla_q_gain, mla_k_gain, mla_out_gain, w_out, norm2_gain, w_ff1, w_ff2, loss_target, m_norm1_gain, m_w_in, m_gm_v_gain, m_gm_w_s, m_gm_b_s, m_gm_out_gain, m_hg_lower_bound, m_hg_out_gain, m_mla_q_a_gain, m_mla_w_uq, m_mla_kv_a_gain, m_mla_w_ukv, m_mla_q_gain, m_mla_k_gain, m_mla_out_gain, m_w_out, m_norm2_gain, m_w_ff1, m_w_ff2, v_norm1_gain, v_w_in, v_gm_v_gain, v_gm_w_s, v_gm_b_s, v_gm_out_gain, v_hg_lower_bound, v_hg_out_gain, v_mla_q_a_gain, v_mla_w_uq, v_mla_kv_a_gain, v_mla_w_ukv, v_mla_q_gain, v_mla_k_gain, v_mla_out_gain, v_w_out, v_norm2_gain, v_w_ff1, v_w_ff2):
    given = dict(locals())
    weights = {n: given[n] for n in ORDER}
    moms = {n: given["m_" + n] for n in ORDER}
    vars_ = {n: given["v_" + n] for n in ORDER}
    s, d = x.shape[1], x.shape[2]

    small = {n: weights[n] for n in SMALL}
    comm = ChipComm({n: weights[n] for n in BIG})
    loss_part, dx, groups, small_g = local_step(x.reshape(s, d), positions, loss_target.reshape(s, d), small, comm)
    loss = lax.psum(loss_part[0, 0], ("x", "y", "c"))

    pack_g = _pack([jnp.stack(small_g[n]) for n in SMALL]).reshape(N_CHIPS, 2, PACK_ROWS, SLOT)
    reduced = Reducer([groups[0][n] for n in MIX] + [pack_g]).run()
    groups[0].update(zip(MIX, reduced[:-1]))
    pack_full = run_rider(gather_rider([reduced[-1].reshape(4, PACK_ROWS // 2, SLOT)]), "gather_small")[0].reshape(-1)
    grads = {n: jnp.stack([groups[l][n].reshape(weights[n].shape[1:]) for l in range(DEPTH)]) for n in BIG}
    grads.update(zip(SMALL, _unpack(pack_full, [weights[n].shape for n in SMALL])))

    delta, new_m, new_v = {}, {}, {}
    flat2 = lambda a: a.reshape(-1, a.shape[-1])
    for n in ORDER:
        outs, _ = adamw(flat2(weights[n]), flat2(grads[n]), flat2(moms[n]), flat2(vars_[n]), "adamw_" + n)
        delta[n], new_m[n], new_v[n] = [o.reshape(weights[n].shape) for o in outs]

    return (loss, dx.reshape(x.shape), *[grads[n] for n in ORDER], *[delta[n] for n in ORDER],
            *[new_m[n] for n in ORDER], *[new_v[n] for n in ORDER])
```

```python
import functools

import jax
import jax.numpy as jnp
from jax import lax
from jax.experimental import pallas as pl
from jax.experimental.pallas import tpu as pltpu

F32 = jnp.float32
BF16 = jnp.bfloat16
MXU_DTYPE = BF16

D_MODEL = 1024
DEPTH = 4
CHUNK = 128
HG_CHUNK = 128
HG_CHUNKS = 2
GM_CHUNKS = 4
EPS = 1e-6
HEAD64 = 64
MLA_HEADS = 8
QK_NOPE = 64
QK_ROPE = 32
QK_DIM = 96
V_DIM = 64
Q_LORA = 256
KV_LORA = 128
SLOT = 128
ROPE_THETA = 10000.0
D_FF_SHARD = 1024
N_CHIPS = 4

ADAM_LR = 0.001
ADAM_B1 = 0.9
ADAM_B2 = 0.999
ADAM_EPS = 1e-08
ADAM_WD = 0.01
ADAM_STEP = 10

TM = 512
TM_FFN = 512
TQ = 256
ATT_HEADS_PER_STEP = 4
ATT_WIDE = 512
VMEM_LIMIT = 56 * 1024 * 1024

NN = (((1,), (0,)), ((), ()))
NT = (((1,), (1,)), ((), ()))
TN = (((0,), (0,)), ((), ()))
BNN = (((2,), (1,)), ((0,), (0,)))
BNT = (((2,), (2,)), ((0,), (0,)))
BTN = (((1,), (1,)), ((0,), (0,)))


def _dot(a, b, dims):
    return lax.dot_general(a.astype(MXU_DTYPE), b.astype(MXU_DTYPE), dims, preferred_element_type=F32)


def _hdot(a, b, dims=NN):
    return lax.dot_general(a, b, dims, precision=lax.Precision.HIGHEST, preferred_element_type=F32)


def _make_ad(dims, da_dims, da_swap, db_dims, db_swap):
    @jax.custom_vjp
    def f(a, b):
        return _dot(a, b, dims)

    def fwd(a, b):
        return _dot(a, b, dims), (a, b)

    def bwd(res, g):
        a, b = res
        da = _dot(b, g, da_dims) if da_swap else _dot(g, b, da_dims)
        db = _dot(g, a, db_dims) if db_swap else _dot(a, g, db_dims)
        return da, db

    f.defvjp(fwd, bwd)
    return f


@functools.partial(jax.custom_vjp, nondiff_argnums=(1,))
def _roll_ad(x, shift):
    return pltpu.roll(x, shift, 1)


def _roll_ad_fwd(x, shift):
    return pltpu.roll(x, shift, 1), None


def _roll_ad_bwd(shift, _, g):
    return (pltpu.roll(g, (g.shape[1] - shift) % g.shape[1], 1),)


_roll_ad.defvjp(_roll_ad_fwd, _roll_ad_bwd)


class _Ops:
    pass


PLAIN = _Ops()
PLAIN.mm = lambda a, b: _dot(a, b, NN)
PLAIN.bmm = lambda a, b: _dot(a, b, BNN)
PLAIN.bmm_nt = lambda a, b: _dot(a, b, BNT)
PLAIN.bmm_tn = lambda a, b: _dot(a, b, BTN)
PLAIN.roll = lambda x, s: pltpu.roll(x, s, 1)

AD = _Ops()
AD.mm = _make_ad(NN, NT, False, TN, False)
AD.bmm = _make_ad(BNN, BNT, False, BTN, False)
AD.bmm_nt = _make_ad(BNT, BNN, False, BTN, True)
AD.bmm_tn = _make_ad(BTN, BNT, True, BNN, False)
AD.roll = _roll_ad


def _sigmoid(x):
    return jax.nn.sigmoid(x)


def _gelu(x):
    return 0.5 * x * (1.0 + jnp.tanh(0.7978845608028654 * (x + 0.044715 * (x * x * x))))


def _rms(x, g):
    return x * lax.rsqrt(jnp.mean(x * x, axis=-1, keepdims=True) + EPS) * g


def _head_masks256():
    lane = lax.broadcasted_iota(jnp.int32, (1, 4 * HEAD64), 1)
    return [(jnp.right_shift(lane, 6) == h).astype(F32) for h in range(4)]


def _headnorm256(x, g):
    ms = jnp.zeros_like(x)
    sq = x * x
    for m in _head_masks256():
        ms = ms + m * (jnp.sum(sq * m, axis=-1, keepdims=True) * (1.0 / HEAD64))
    return x * lax.rsqrt(ms + EPS) * g


def _slot_norm(x, g, n):
    return x * lax.rsqrt(jnp.sum(x * x, axis=-1, keepdims=True) * (1.0 / n) + EPS) * g


def _rope(ops, x, cos_t, sin_a, sin_b):
    return x * cos_t + ops.roll(x, SLOT - QK_ROPE // 2) * sin_a + ops.roll(x, QK_ROPE // 2) * sin_b


def _inproj(ops, x, g1, wa, wb, wc):
    h = _rms(x, g1)
    return ops.mm(h, wa), ops.mm(h, wb), ops.mm(h, wc)


def _gm_chunk(ops, ur, vr, vg, ws4, bs, og):
    c = ur.shape[0]
    masks = _head_masks256()
    mh = jnp.concatenate([m[None] for m in masks], axis=0)
    u = _gelu(ur)
    v = _headnorm256(_gelu(vr), vg)
    t = lax.broadcasted_iota(jnp.int32, (c, c), 0)
    s = lax.broadcasted_iota(jnp.int32, (c, c), 1)
    w = jnp.where((t >= s)[None], ws4, 0.0)
    y = jnp.sum(ops.bmm(w, v[None] * mh), axis=0)
    for h in range(4):
        y = y + bs[h] * masks[h]
    return _headnorm256(u * y, og)


def _hg_chunk(ops, st, qr, fr, ir, gr, lb, og):
    c, n = qr.shape
    nh = n // HEAD64
    heads = lambda x: x.reshape(nh, HEAD64, x.shape[-1])
    tr = lambda x: heads(x.T)
    lb4, og4 = heads(lb), heads(og)
    qx = tr(qr)
    q = qx * _sigmoid(qx)
    f = lb4 + (1.0 - lb4) * _sigmoid(tr(fr))
    k = 1.0 - f
    logf = jnp.log(f)
    v = tr(ir)
    gx = tr(gr)
    s = lax.broadcasted_iota(jnp.int32, (c, c), 0)
    t = lax.broadcasted_iota(jnp.int32, (c, c), 1)
    tl = lax.broadcasted_iota(jnp.int32, (1, c), 1).reshape(1, 1, c)
    b2 = _hdot(logf.reshape(n, c), (s <= t).astype(F32))
    b = heads(b2)
    btot = jnp.sum(logf, axis=2, keepdims=True)
    inter = ops.bmm_tn(st, q * jnp.exp(b))
    p4 = jnp.zeros((nh, c, c), F32)
    tt, ss = s, t
    lg = c.bit_length() - 2
    while lg >= 0:
        m = 1 << lg
        bnd = jnp.left_shift(jnp.right_shift(t, lg + 1), lg + 1) + (m - 1)
        r = heads(_hdot(b2, (s == bnd).astype(F32)))
        right = jnp.bitwise_and(jnp.right_shift(tl, lg), 1) == 1
        qe = jnp.where(right, q * jnp.exp(jnp.where(right, b - r, 0.0)), 0.0)
        ke = jnp.where(right, 0.0, k * jnp.exp(jnp.where(right, 0.0, r - b)))
        lm = ((jnp.right_shift(tt, lg + 1) == jnp.right_shift(ss, lg + 1))
              & (jnp.bitwise_and(jnp.right_shift(tt, lg), 1) == 1)
              & (jnp.bitwise_and(jnp.right_shift(ss, lg), 1) == 0))
        p4 = jnp.where(lm[None], ops.bmm_tn(qe, ke), p4)
        lg -= 1
    intra = ops.bmm_nt(v, p4)
    o = inter + intra + jnp.sum(q * k, axis=1, keepdims=True) * v
    st_new = st * jnp.exp(btot) + ops.bmm_nt(k * jnp.exp(btot - b), v)
    y = o * lax.rsqrt(jnp.mean(o * o, axis=1, keepdims=True) + EPS) * og4 * (gx * _sigmoid(gx))
    return st_new, y.reshape(n, c).T


def _mla_pre(ops, cq, ckv, kpe, cos_t, sin_a, sin_b, qag, kvag, qg, kg, wq, wk, wv):
    cqn = _rms(cq, qag)
    ckvn = _rms(ckv, kvag)
    kper = ops.roll(kpe, QK_NOPE)
    qs, ks, vs = [], [], []
    for h in range(MLA_HEADS):
        qh = _slot_norm(ops.mm(cqn, wq[h]), qg, QK_DIM)
        qs.append(_rope(ops, qh, cos_t, sin_a, sin_b))
        kh = _slot_norm(ops.mm(ckvn, wk[h]) + kper, kg, QK_DIM)
        ks.append(_rope(ops, kh, cos_t, sin_a, sin_b))
        vs.append(ops.mm(ckvn, wv[h]))
    return qs, ks, vs


def _outproj(ops, x, ya, yb, o, mog, woa, wob, woc):
    acc = x + ops.mm(ya, woa) + ops.mm(yb, wob)
    for h in range(MLA_HEADS):
        acc = acc + ops.mm(_slot_norm(o[h], mog[h], V_DIM), woc[h])
    return acc


def _lower_bounds(r0, r1, r2, r3):
    mx = jnp.maximum(jnp.maximum(r0, r1), jnp.maximum(r2, r3))
    e0, e1, e2, e3 = jnp.exp(r0 - mx), jnp.exp(r1 - mx), jnp.exp(r2 - mx), jnp.exp(r3 - mx)
    inv = 1.0 / (e0 + e1 + e2 + e3)
    s1, s2, s3 = e1 * inv, e2 * inv, e3 * inv
    return jnp.zeros_like(r0), s1, s1 + s2, s1 + s2 + s3


def _cp(sem):
    return pltpu.CompilerParams(dimension_semantics=sem, vmem_limit_bytes=VMEM_LIMIT)


def _rows(tm, n):
    return pl.BlockSpec((tm, n), lambda i: (i, 0))


def _full(a):
    nd = len(a.shape)
    return pl.BlockSpec(a.shape, lambda *_: (0,) * nd, pipeline_mode=pl.Buffered(1))


def _sds(shape, dtype=F32):
    return jax.ShapeDtypeStruct(shape, dtype)


def _acc(ref, val, first):
    @pl.when(first)
    def _():
        ref[...] = val

    @pl.when(jnp.logical_not(first))
    def _():
        ref[...] = ref[...] + val


def _f32(ref):
    return ref[...].astype(F32)


MESH = pl.DeviceIdType.MESH
ANY = pl.BlockSpec(memory_space=pl.ANY)


class Rider:
    def __init__(self, arrays, out_shapes, aliases, sems, start, finish):
        self.arrays, self.out_shapes, self.aliases, self.sems = list(arrays), list(out_shapes), dict(aliases), list(sems)
        self.start, self.finish = start, finish


def run_rider(rider, name):
    n_in, n_out = len(rider.arrays), len(rider.out_shapes)

    def body(*refs):
        ins, outs, sems = refs[:n_in], refs[n_in:n_in + n_out], refs[n_in + n_out:]
        rider.start(ins, outs, sems)
        rider.finish(ins, outs, sems)

    return pl.pallas_call(
        body, name=name, in_specs=[ANY] * n_in, out_specs=[ANY] * n_out, out_shape=rider.out_shapes,
        input_output_aliases=rider.aliases, scratch_shapes=rider.sems,
    )(*rider.arrays)


def _merge_riders(riders):
    bounds, a0, o0, s0 = [], 0, 0, 0
    for r in riders:
        bounds.append((a0, o0, s0))
        a0, o0, s0 = a0 + len(r.arrays), o0 + len(r.out_shapes), s0 + len(r.sems)

    def part(k, ins, outs, sems):
        a, o, s = bounds[k]
        r = riders[k]
        return ins[a:a + len(r.arrays)], outs[o:o + len(r.out_shapes)], sems[s:s + len(r.sems)]

    return Rider(
        [x for r in riders for x in r.arrays], [x for r in riders for x in r.out_shapes],
        {bounds[k][0] + i: bounds[k][1] + o for k, r in enumerate(riders) for i, o in r.aliases.items()},
        [x for r in riders for x in r.sems],
        lambda *refs: [r.start(*part(k, *refs)) for k, r in enumerate(riders)],
        lambda *refs: [r.finish(*part(k, *refs)) for k, r in enumerate(riders)])


def _ride(compute, riders, *, name, grid, in_specs, out_specs, out_shape, operands, scratch_shapes=(), sem=None):
    single = not isinstance(out_shape, (list, tuple))
    if single:
        out_specs, out_shape = [out_specs], [out_shape]
    live = [r for r in riders if r is not None]
    if not live:
        res = pl.pallas_call(compute, name=name, grid=grid, in_specs=in_specs, out_specs=out_specs, out_shape=out_shape,
                             scratch_shapes=list(scratch_shapes), compiler_params=_cp(sem))(*operands)
        return (res[0] if single else res), [None] * len(riders)
    rider = live[0] if len(live) == 1 else _merge_riders(live)
    n_in, n_out, n_s = len(in_specs), len(out_specs), len(scratch_shapes)
    r_in, r_out = len(rider.arrays), len(rider.out_shapes)

    def body(*refs):
        ins, rins = refs[:n_in], refs[n_in:n_in + r_in]
        outs = refs[n_in + r_in:n_in + r_in + n_out]
        routs = refs[n_in + r_in + n_out:n_in + r_in + n_out + r_out]
        scr = refs[n_in + r_in + n_out + r_out:n_in + r_in + n_out + r_out + n_s]
        rsems = refs[n_in + r_in + n_out + r_out + n_s:]
        first = functools.reduce(jnp.logical_and, [pl.program_id(a) == 0 for a in range(len(grid))])
        last = functools.reduce(jnp.logical_and, [pl.program_id(a) == grid[a] - 1 for a in range(len(grid))])

        @pl.when(first)
        def _():
            rider.start(rins, routs, rsems)

        compute(*ins, *outs, *scr)

        @pl.when(last)
        def _():
            rider.finish(rins, routs, rsems)

    res = pl.pallas_call(
        body, name=name, grid=grid, in_specs=list(in_specs) + [ANY] * r_in, out_specs=list(out_specs) + [ANY] * r_out,
        out_shape=list(out_shape) + rider.out_shapes,
        input_output_aliases={n_in + k: n_out + v for k, v in rider.aliases.items()},
        scratch_shapes=list(scratch_shapes) + rider.sems, compiler_params=_cp(("arbitrary",) * len(grid)),
    )(*operands, *rider.arrays)
    main, rest, per_rider = res[:n_out], list(res[n_out:]), []
    for r in riders:
        per_rider.append(None if r is None else [rest.pop(0) for _ in r.out_shapes])
    return (main[0] if single else main), per_rider


def inproj_fwd(x, g1, wa, wb, wc):
    s, d = x.shape

    def body(x_ref, g_ref, wa_ref, wb_ref, wc_ref, pa_ref, pb_ref, pc_ref):
        pa, pb, pc = _inproj(PLAIN, x_ref[...], g_ref[...], wa_ref[...], wb_ref[...], wc_ref[...])
        pa_ref[...] = pa
        pb_ref[...] = pb
        pc_ref[...] = pc

    return pl.pallas_call(
        body, name="inproj_fwd", grid=(s // TM,),
        in_specs=[_rows(TM, d), _full(g1), _full(wa), _full(wb), _full(wc)],
        out_specs=[_rows(TM, wa.shape[1]), _rows(TM, wb.shape[1]), _rows(TM, wc.shape[1])],
        out_shape=[_sds((s, wa.shape[1])), _sds((s, wb.shape[1])), _sds((s, wc.shape[1]))],
        compiler_params=_cp(("parallel",)),
    )(x, g1, wa, wb, wc)


def inproj_bwd(x, g1, wa, wb, wc, dpa, dpb, dpc, dres):
    s, d = x.shape

    def body(x_ref, g_ref, wa_ref, wb_ref, wc_ref, dpa_ref, dpb_ref, dpc_ref, dres_ref,
             dx_ref, dg_ref, dwa_ref, dwb_ref, dwc_ref):
        first = pl.program_id(0) == 0
        _, vjp = jax.vjp(functools.partial(_inproj, AD), x_ref[...], g_ref[...],
                         _f32(wa_ref), _f32(wb_ref), _f32(wc_ref))
        dx, dg, dwa, dwb, dwc = vjp((dpa_ref[...], dpb_ref[...], dpc_ref[...]))
        dx_ref[...] = dx + dres_ref[...]
        _acc(dg_ref, dg, first)
        _acc(dwa_ref, dwa, first)
        _acc(dwb_ref, dwb, first)
        _acc(dwc_ref, dwc, first)

    return pl.pallas_call(
        body, name="inproj_bwd", grid=(s // TM,),
        in_specs=[_rows(TM, d), _full(g1), _full(wa), _full(wb), _full(wc),
                  _rows(TM, wa.shape[1]), _rows(TM, wb.shape[1]), _rows(TM, wc.shape[1]), _rows(TM, d)],
        out_specs=[_rows(TM, d), _full(g1), _full(wa), _full(wb), _full(wc)],
        out_shape=[_sds((s, d)), _sds(g1.shape), _sds(wa.shape), _sds(wb.shape), _sds(wc.shape)],
        compiler_params=_cp(("arbitrary",)),
    )(x, g1, wa, wb, wc, dpa, dpb, dpc, dres)


def gm_fwd(pa, vg, ws4, bs, og):
    s = pa.shape[0]
    w = pa.shape[1] // 2

    def body(pa_ref, vg_ref, ws_ref, bs_ref, og_ref, ya_ref):
        bsl = [bs_ref[h] for h in range(4)]
        for j in range(GM_CHUNKS):
            rows = slice(j * CHUNK, (j + 1) * CHUNK)
            ya_ref[rows, :] = _gm_chunk(PLAIN, pa_ref[rows, 0:w], pa_ref[rows, w:2 * w], vg_ref[...], ws_ref[...], bsl,
                                        og_ref[...])

    tm = GM_CHUNKS * CHUNK
    return pl.pallas_call(
        body, name="gm_fwd", grid=(s // tm,),
        in_specs=[_rows(tm, 2 * w), _full(vg), _full(ws4), _full(bs), _full(og)],
        out_specs=_rows(tm, w), out_shape=_sds((s, w)),
        compiler_params=_cp(("parallel",)),
    )(pa, vg, ws4, bs, og)


def gm_bwd(pa, vg, ws4, bs, og, dya):
    s = pa.shape[0]
    w = pa.shape[1] // 2

    def body(pa_ref, vg_ref, ws_ref, bs_ref, og_ref, dya_ref, dpa_ref, dvg_ref, dws_ref, dbs_ref, dog_ref):
        first = pl.program_id(0) == 0
        bsl = [bs_ref[h] for h in range(4)]
        tot = None
        for j in range(GM_CHUNKS):
            rows = slice(j * CHUNK, (j + 1) * CHUNK)
            _, vjp = jax.vjp(functools.partial(_gm_chunk, AD), pa_ref[rows, 0:w], pa_ref[rows, w:2 * w],
                             vg_ref[...], ws_ref[...], bsl, og_ref[...])
            du, dv, *dws = vjp(dya_ref[rows, :])
            dpa_ref[rows, 0:w] = du
            dpa_ref[rows, w:2 * w] = dv
            tot = dws if tot is None else jax.tree.map(jnp.add, tot, dws)
        dvg, dws, dbs, dog = tot
        _acc(dvg_ref, dvg, first)
        _acc(dws_ref, dws, first)
        _acc(dog_ref, dog, first)
        for h in range(4):
            _acc(dbs_ref.at[h], dbs[h], first)

    tm = GM_CHUNKS * CHUNK
    return pl.pallas_call(
        body, name="gm_bwd", grid=(s // tm,),
        in_specs=[_rows(tm, 2 * w), _full(vg), _full(ws4), _full(bs), _full(og), _rows(tm, w)],
        out_specs=[_rows(tm, 2 * w), _full(vg), _full(ws4), _full(bs), _full(og)],
        out_shape=[_sds((s, 2 * w)), _sds(vg.shape), _sds(ws4.shape), _sds(bs.shape), _sds(og.shape)],
        compiler_params=_cp(("arbitrary",)),
    )(pa, vg, ws4, bs, og, dya)


def hg_fwd(pb, lb, og, riders=()):
    s = pb.shape[0]
    w = pb.shape[1] // 4
    tm = HG_CHUNKS * HG_CHUNK
    st_shape = (w // HEAD64, HEAD64, HEAD64)

    def body(pb_ref, lb_ref, og_ref, yb_ref, states_ref, st_ref):
        @pl.when(pl.program_id(0) == 0)
        def _():
            st_ref[...] = jnp.zeros_like(st_ref)

        st = st_ref[...]
        for j in range(HG_CHUNKS):
            rows = slice(j * HG_CHUNK, (j + 1) * HG_CHUNK)
            states_ref[j] = st
            st, y = _hg_chunk(PLAIN, st, pb_ref[rows, 0:w], pb_ref[rows, w:2 * w], pb_ref[rows, 2 * w:3 * w],
                              pb_ref[rows, 3 * w:4 * w], lb_ref[...], og_ref[...])
            yb_ref[rows, :] = y
        st_ref[...] = st

    return _ride(
        body, riders, name="hg_fwd", grid=(s // tm,),
        in_specs=[_rows(tm, 4 * w), _full(lb), _full(og)],
        out_specs=[_rows(tm, w), pl.BlockSpec((HG_CHUNKS,) + st_shape, lambda i: (i, 0, 0, 0))],
        out_shape=[_sds((s, w)), _sds((s // HG_CHUNK,) + st_shape)],
        scratch_shapes=[pltpu.VMEM(st_shape, F32)],
        operands=(pb, lb, og), sem=("arbitrary",))


def hg_bwd(pb, lb, og, states, dyb, riders=()):
    s = pb.shape[0]
    w = pb.shape[1] // 4
    tm = HG_CHUNKS * HG_CHUNK
    nc = s // tm
    st_shape = (w // HEAD64, HEAD64, HEAD64)

    def body(pb_ref, lb_ref, og_ref, states_ref, dyb_ref, dpb_ref, dlb_ref, dog_ref, dst_ref):
        first = pl.program_id(0) == 0

        @pl.when(first)
        def _():
            dst_ref[...] = jnp.zeros_like(dst_ref)

        dst, dlb, dog = dst_ref[...], None, None
        for j in reversed(range(HG_CHUNKS)):
            rows = slice(j * HG_CHUNK, (j + 1) * HG_CHUNK)
            _, vjp = jax.vjp(functools.partial(_hg_chunk, AD), states_ref[j], pb_ref[rows, 0:w], pb_ref[rows, w:2 * w],
                             pb_ref[rows, 2 * w:3 * w], pb_ref[rows, 3 * w:4 * w], lb_ref[...], og_ref[...])
            dst, dq, df, di, dg, dlb_j, dog_j = vjp((dst, dyb_ref[rows, :]))
            dpb_ref[rows, 0:w] = dq
            dpb_ref[rows, w:2 * w] = df
            dpb_ref[rows, 2 * w:3 * w] = di
            dpb_ref[rows, 3 * w:4 * w] = dg
            dlb = dlb_j if dlb is None else dlb + dlb_j
            dog = dog_j if dog is None else dog + dog_j
        dst_ref[...] = dst
        _acc(dlb_ref, dlb, first)
        _acc(dog_ref, dog, first)

    rev = lambda i: (nc - 1 - i, 0)
    return _ride(
        body, riders, name="hg_bwd", grid=(nc,),
        in_specs=[pl.BlockSpec((tm, 4 * w), rev), _full(lb), _full(og),
                  pl.BlockSpec((HG_CHUNKS,) + st_shape, lambda i: (nc - 1 - i, 0, 0, 0)), pl.BlockSpec((tm, w), rev)],
        out_specs=[pl.BlockSpec((tm, 4 * w), rev), _full(lb), _full(og)],
        out_shape=[_sds((s, 4 * w)), _sds(lb.shape), _sds(og.shape)],
        scratch_shapes=[pltpu.VMEM(st_shape, F32)],
        operands=(pb, lb, og, states, dyb), sem=("arbitrary",))


def lower_bounds_fwd(hlb):
    def body(h_ref, o_ref):
        outs = _lower_bounds(*[h_ref[pl.ds(i, 1), :] for i in range(DEPTH)])
        for i in range(DEPTH):
            o_ref[pl.ds(i, 1), :] = outs[i]

    return pl.pallas_call(body, name="lower_bounds_fwd", out_shape=_sds(hlb.shape))(hlb)


def lower_bounds_bwd(hlb, dlbs):
    def body(h_ref, d_ref, o_ref):
        _, vjp = jax.vjp(_lower_bounds, *[h_ref[pl.ds(i, 1), :] for i in range(DEPTH)])
        outs = vjp(tuple(d_ref[pl.ds(i, 1), :] for i in range(DEPTH)))
        for i in range(DEPTH):
            o_ref[pl.ds(i, 1), :] = outs[i]

    return pl.pallas_call(body, name="lower_bounds_bwd", out_shape=_sds(hlb.shape))(hlb, dlbs)


def _mla_pre_args(pc_ref, cos_ref, sa_ref, sb_ref, qag_ref, kvag_ref, qg_ref, kg_ref, wq_ref, wk_ref, wv_ref, cast):
    sl = lambda h: slice(h * SLOT, (h + 1) * SLOT)
    ld = (lambda r, h: r[:, sl(h)].astype(F32)) if cast else (lambda r, h: r[:, sl(h)])
    diff = (pc_ref[:, 0:Q_LORA], pc_ref[:, Q_LORA:Q_LORA + KV_LORA], pc_ref[:, Q_LORA + KV_LORA:Q_LORA + 2 * KV_LORA],
            qag_ref[...], kvag_ref[...], qg_ref[...], kg_ref[...],
            [ld(wq_ref, h) for h in range(MLA_HEADS)], [ld(wk_ref, h) for h in range(MLA_HEADS)],
            [ld(wv_ref, h) for h in range(MLA_HEADS)])
    tables = (cos_ref[...], sa_ref[...], sb_ref[...])
    return diff, tables


def _mla_pre_fn(ops, tables, cq, ckv, kpe, qag, kvag, qg, kg, wq, wk, wv):
    return _mla_pre(ops, cq, ckv, kpe, *tables, qag, kvag, qg, kg, wq, wk, wv)


def mla_pre_fwd(pc, cos_t, sin_a, sin_b, qag, kvag, qg, kg, wq, wk, wv):
    s = pc.shape[0]
    hw = MLA_HEADS * SLOT

    def body(pc_ref, cos_ref, sa_ref, sb_ref, qag_ref, kvag_ref, qg_ref, kg_ref, wq_ref, wk_ref, wv_ref,
             q_ref, k_ref, v_ref):
        diff, tables = _mla_pre_args(pc_ref, cos_ref, sa_ref, sb_ref, qag_ref, kvag_ref, qg_ref, kg_ref,
                                     wq_ref, wk_ref, wv_ref, False)
        qs, ks, vs = _mla_pre_fn(PLAIN, tables, *diff)
        ones_lane = (lax.broadcasted_iota(jnp.int32, (1, SLOT), 1) == V_DIM).astype(F32)
        for h in range(MLA_HEADS):
            q_ref[:, h * SLOT:(h + 1) * SLOT] = qs[h].astype(q_ref.dtype)
            k_ref[:, h * SLOT:(h + 1) * SLOT] = ks[h].astype(k_ref.dtype)
            v_ref[:, h * SLOT:(h + 1) * SLOT] = (vs[h] + ones_lane).astype(v_ref.dtype)

    return pl.pallas_call(
        body, name="mla_pre_fwd", grid=(s // TM,),
        in_specs=[_rows(TM, pc.shape[1]), _rows(TM, SLOT), _rows(TM, SLOT), _rows(TM, SLOT),
                  _full(qag), _full(kvag), _full(qg), _full(kg), _full(wq), _full(wk), _full(wv)],
        out_specs=[_rows(TM, hw)] * 3, out_shape=[_sds((s, hw), MXU_DTYPE)] * 3,
        compiler_params=_cp(("parallel",)),
    )(pc, cos_t, sin_a, sin_b, qag, kvag, qg, kg, wq, wk, wv)


def mla_pre_bwd(pc, cos_t, sin_a, sin_b, qag, kvag, qg, kg, wq, wk, wv, dq, dk, dv):
    s = pc.shape[0]
    hw = MLA_HEADS * SLOT

    def body(pc_ref, cos_ref, sa_ref, sb_ref, qag_ref, kvag_ref, qg_ref, kg_ref, wq_ref, wk_ref, wv_ref,
             dq_ref, dk_ref, dv_ref, dpc_ref, dqag_ref, dkvag_ref, dqg_ref, dkg_ref, dwq_ref, dwk_ref, dwv_ref):
        first = pl.program_id(0) == 0
        diff, tables = _mla_pre_args(pc_ref, cos_ref, sa_ref, sb_ref, qag_ref, kvag_ref, qg_ref, kg_ref,
                                     wq_ref, wk_ref, wv_ref, True)
        _, vjp = jax.vjp(functools.partial(_mla_pre_fn, AD, tables), *diff)
        sl = lambda h: slice(h * SLOT, (h + 1) * SLOT)
        cot = ([dq_ref[:, sl(h)] for h in range(MLA_HEADS)], [dk_ref[:, sl(h)] for h in range(MLA_HEADS)],
               [dv_ref[:, sl(h)] for h in range(MLA_HEADS)])
        dcq, dckv, dkpe, dqag, dkvag, dqg, dkg, dwq, dwk, dwv = vjp(cot)
        dpc_ref[:, 0:Q_LORA] = dcq
        dpc_ref[:, Q_LORA:Q_LORA + KV_LORA] = dckv
        dpc_ref[:, Q_LORA + KV_LORA:Q_LORA + 2 * KV_LORA] = dkpe
        _acc(dqag_ref, dqag, first)
        _acc(dkvag_ref, dkvag, first)
        _acc(dqg_ref, dqg, first)
        _acc(dkg_ref, dkg, first)
        for h in range(MLA_HEADS):
            _acc(dwq_ref.at[:, sl(h)], dwq[h], first)
            _acc(dwk_ref.at[:, sl(h)], dwk[h], first)
            _acc(dwv_ref.at[:, sl(h)], dwv[h], first)

    return pl.pallas_call(
        body, name="mla_pre_bwd", grid=(s // TM,),
        in_specs=[_rows(TM, pc.shape[1]), _rows(TM, SLOT), _rows(TM, SLOT), _rows(TM, SLOT),
                  _full(qag), _full(kvag), _full(qg), _full(kg), _full(wq), _full(wk), _full(wv),
                  _rows(TM, hw), _rows(TM, hw), _rows(TM, hw)],
        out_specs=[_rows(TM, pc.shape[1]), _full(qag), _full(kvag), _full(qg), _full(kg),
                   _full(wq), _full(wk), _full(wv)],
        out_shape=[_sds(pc.shape), _sds(qag.shape), _sds(kvag.shape), _sds(qg.shape), _sds(kg.shape),
                   _sds(wq.shape), _sds(wk.shape), _sds(wv.shape)],
        compiler_params=_cp(("arbitrary",)),
    )(pc, cos_t, sin_a, sin_b, qag, kvag, qg, kg, wq, wk, wv, dq, dk, dv)


ATT_SCALE = QK_DIM ** -0.5
NEG_BIG = -1e30


def attn_fwd(q, k, v, riders=()):
    s = q.shape[0]
    nq = s // TQ
    hp = ATT_HEADS_PER_STEP
    sl = lambda j: slice(j * SLOT, (j + 1) * SLOT)

    wide = ATT_WIDE // TQ

    def body(q_ref, k_ref, v_ref, o_ref, lse_ref):
        qi = pl.program_id(1)
        lane = lax.broadcasted_iota(jnp.int32, (1, SLOT), 1)
        qs = [q_ref[:, sl(j)] for j in range(hp)]

        def step(ki, carry, n_tiles, masked):
            rk = pl.ds(pl.multiple_of(ki * TQ, TQ), n_tiles * TQ)
            if masked:
                row = lax.broadcasted_iota(jnp.int32, (TQ, n_tiles * TQ), 0) + (n_tiles - 1) * TQ
                col = lax.broadcasted_iota(jnp.int32, (TQ, n_tiles * TQ), 1)
            out = []
            for j in range(hp):
                m, acc = carry[j]
                sc = _dot(qs[j], k_ref[rk, sl(j)], NT) * ATT_SCALE
                if masked:
                    sc = jnp.where(row >= col, sc, NEG_BIG)
                m_new = jnp.maximum(m, jnp.max(sc, axis=-1, keepdims=True))
                acc = jnp.exp(m - m_new) * acc + _dot(jnp.exp(sc - m_new), v_ref[rk, sl(j)], NN)
                out.append((m_new, acc))
            return tuple(out)

        def tail_single(cr):
            cr = lax.fori_loop(n_wide * wide, qi, lambda ki, c: step(ki, c, 1, False), cr)
            return step(qi, cr, 1, True)

        n_wide = qi // wide
        init = tuple((jnp.full((TQ, 1), NEG_BIG, F32), jnp.zeros((TQ, SLOT), F32)) for _ in range(hp))
        carry = lax.fori_loop(0, n_wide, lambda kw, cr: step(kw * wide, cr, wide, False), init)
        carry = lax.cond(qi % wide == wide - 1, lambda cr: step(qi - (wide - 1), cr, wide, True), tail_single, carry)
        for j in range(hp):
            m, acc = carry[j]
            l = jnp.sum(jnp.where(lane == V_DIM, acc, 0.0), axis=-1, keepdims=True)
            o_ref[:, sl(j)] = jnp.where(lane < V_DIM, acc / l, 0.0)
            lse_ref[j] = m + jnp.log(l)

    head_col = pl.BlockSpec((s, hp * SLOT), lambda g, i: (0, g))
    tile = pl.BlockSpec((TQ, hp * SLOT), lambda g, i: (i, g))
    return _ride(
        body, riders, name="attn_fwd", grid=(MLA_HEADS // hp, nq),
        in_specs=[tile, head_col, head_col],
        out_specs=[tile, pl.BlockSpec((hp, TQ, 1), lambda g, i: (g, i, 0))],
        out_shape=[_sds((s, MLA_HEADS * SLOT)), _sds((MLA_HEADS, s, 1))],
        operands=(q, k, v), sem=("parallel", "parallel"))


def attn_bwd(q, k, v, o, do, lse, riders=()):
    s = q.shape[0]
    nq = s // TQ
    hp = ATT_HEADS_PER_STEP
    sl = lambda j: slice(j * SLOT, (j + 1) * SLOT)
    wide = ATT_WIDE // TQ

    def body(q_ref, k_ref, v_ref, o_ref, do_ref, lse_ref, dq_ref, dk_ref, dv_ref, delta_ref):
        ki = pl.program_id(1)

        @pl.when(ki == 0)
        def _():
            dq_ref[...] = jnp.zeros_like(dq_ref)

            def prep(i, c):
                rows = pl.ds(pl.multiple_of(i * TQ, TQ), TQ)
                for j in range(hp):
                    delta_ref[j, rows, :] = jnp.sum(do_ref[rows, sl(j)] * o_ref[rows, sl(j)], axis=-1, keepdims=True)
                return c

            lax.fori_loop(0, nq, prep, 0)

        kks = [k_ref[:, sl(j)] for j in range(hp)]
        vvs = [v_ref[:, sl(j)] for j in range(hp)]

        def step(qi, carry, n_tiles, masked):
            rq = pl.ds(pl.multiple_of(qi * TQ, TQ), n_tiles * TQ)
            if masked:
                row = lax.broadcasted_iota(jnp.int32, (n_tiles * TQ, TQ), 0)
                col = lax.broadcasted_iota(jnp.int32, (n_tiles * TQ, TQ), 1)
            out = []
            for j in range(hp):
                dk, dv = carry[j]
                qq = q_ref[rq, sl(j)]
                dd = do_ref[rq, sl(j)]
                sc = _dot(qq, kks[j], NT) * ATT_SCALE
                if masked:
                    sc = jnp.where(row >= col, sc, NEG_BIG)
                p = jnp.exp(sc - lse_ref[j, rq, :])
                dv = dv + _dot(p, dd, TN)
                ds = p * (_dot(dd, vvs[j], NT) - delta_ref[j, rq, :]) * ATT_SCALE
                dk = dk + _dot(ds, qq, TN)
                dq_ref[rq, sl(j)] = dq_ref[rq, sl(j)] + _dot(ds, kks[j], NN)
                out.append((dk, dv))
            return tuple(out)

        def head_single(cr):
            cr = step(ki, cr, 1, True)
            return lax.fori_loop(ki + 1, first_wide * wide, lambda qi, c: step(qi, c, 1, False), cr)

        zero = jnp.zeros((TQ, SLOT), F32)
        first_wide = (ki + wide) // wide
        carry = tuple((zero, zero) for _ in range(hp))
        carry = lax.cond(ki % wide == 0, lambda cr: step(ki, cr, wide, True), head_single, carry)
        carry = lax.fori_loop(first_wide, nq // wide, lambda qw, cr: step(qw * wide, cr, wide, False), carry)
        for j in range(hp):
            dk_ref[:, sl(j)] = carry[j][0]
            dv_ref[:, sl(j)] = carry[j][1]

    head_col = pl.BlockSpec((s, hp * SLOT), lambda g, i: (0, g))
    tile = pl.BlockSpec((TQ, hp * SLOT), lambda g, i: (i, g))
    return _ride(
        body, riders, name="attn_bwd", grid=(MLA_HEADS // hp, nq),
        in_specs=[head_col, tile, tile, head_col, head_col, pl.BlockSpec((hp, s, 1), lambda g, i: (g, 0, 0))],
        out_specs=[head_col, tile, tile],
        out_shape=[_sds((s, MLA_HEADS * SLOT))] * 3,
        scratch_shapes=[pltpu.VMEM((hp, s, 1), F32)],
        operands=(q, k, v, o, do, lse), sem=("arbitrary", "arbitrary"))


def _outproj_args(ya_ref, yb_ref, o_ref, mog_ref, woa_ref, wob_ref, woc_ref, cast):
    sl = lambda h: slice(h * SLOT, (h + 1) * SLOT)
    ldw = (lambda r: r[...].astype(F32)) if cast else (lambda r: r[...])
    ldc = (lambda h: woc_ref[sl(h), :].astype(F32)) if cast else (lambda h: woc_ref[sl(h), :])
    return (ya_ref[...], yb_ref[...], [o_ref[:, sl(h)] for h in range(MLA_HEADS)],
            [mog_ref[:, sl(h)] for h in range(MLA_HEADS)], ldw(woa_ref), ldw(wob_ref),
            [ldc(h) for h in range(MLA_HEADS)])


def outproj_fwd(x, ya, yb, o, mog, woa, wob, woc):
    s, d = x.shape

    def body(x_ref, ya_ref, yb_ref, o_ref, mog_ref, woa_ref, wob_ref, woc_ref, x1_ref):
        x1_ref[...] = _outproj(PLAIN, x_ref[...], *_outproj_args(ya_ref, yb_ref, o_ref, mog_ref, woa_ref, wob_ref,
                                                                  woc_ref, False))

    return pl.pallas_call(
        body, name="outproj_fwd", grid=(s // TM,),
        in_specs=[_rows(TM, d), _rows(TM, ya.shape[1]), _rows(TM, yb.shape[1]), _rows(TM, o.shape[1]),
                  _full(mog), _full(woa), _full(wob), _full(woc)],
        out_specs=_rows(TM, d), out_shape=_sds((s, d)),
        compiler_params=_cp(("parallel",)),
    )(x, ya, yb, o, mog, woa, wob, woc)


def outproj_bwd(ya, yb, o, mog, woa, wob, woc, dx2, dx1p, riders=()):
    s, d = dx2.shape
    npart = dx1p.shape[0]

    def body(ya_ref, yb_ref, o_ref, mog_ref, woa_ref, wob_ref, woc_ref, dx2_ref, dx1p_ref,
             dx1_ref, dya_ref, dyb_ref, do_ref, dmog_ref, dwoa_ref, dwob_ref, dwoc_ref):
        first = pl.program_id(0) == 0
        sl = lambda h: slice(h * SLOT, (h + 1) * SLOT)
        dx1 = dx2_ref[...]
        for p in range(npart):
            dx1 = dx1 + dx1p_ref[p]
        dx1_ref[...] = dx1
        args = _outproj_args(ya_ref, yb_ref, o_ref, mog_ref, woa_ref, wob_ref, woc_ref, True)
        _, vjp = jax.vjp(lambda *a: _outproj(AD, jnp.zeros_like(dx1), *a), *args)
        dya, dyb, do, dmog, dwoa, dwob, dwoc = vjp(dx1)
        dya_ref[...] = dya
        dyb_ref[...] = dyb
        _acc(dwoa_ref, dwoa, first)
        _acc(dwob_ref, dwob, first)
        for h in range(MLA_HEADS):
            do_ref[:, sl(h)] = do[h]
            _acc(dmog_ref.at[:, sl(h)], dmog[h], first)
            _acc(dwoc_ref.at[sl(h), :], dwoc[h], first)

    return _ride(
        body, riders, name="outproj_bwd", grid=(s // TM,),
        in_specs=[_rows(TM, ya.shape[1]), _rows(TM, yb.shape[1]), _rows(TM, o.shape[1]),
                  _full(mog), _full(woa), _full(wob), _full(woc), _rows(TM, d),
                  pl.BlockSpec((npart, TM, d), lambda i: (0, i, 0))],
        out_specs=[_rows(TM, d), _rows(TM, ya.shape[1]), _rows(TM, yb.shape[1]), _rows(TM, o.shape[1]),
                   _full(mog), _full(woa), _full(wob), _full(woc)],
        out_shape=[_sds((s, d)), _sds(ya.shape), _sds(yb.shape), _sds(o.shape),
                   _sds(mog.shape), _sds(woa.shape), _sds(wob.shape), _sds(woc.shape)],
        operands=(ya, yb, o, mog, woa, wob, woc, dx2, dx1p), sem=("arbitrary",))


def ffn_fwd(x1, g2, w1, w2, riders=()):
    s, d = x1.shape
    npart, _, fs = w1.shape

    def body(x1_ref, g_ref, w1_ref, w2_ref, x2_ref, r_ref):
        p = pl.program_id(1)
        x1v = x1_ref[...]
        r = jnp.maximum(PLAIN.mm(_rms(x1v, g_ref[...]), w1_ref[...]), 0.0)
        r_ref[...] = r.astype(r_ref.dtype)
        part = PLAIN.mm(r * r, w2_ref[...])

        @pl.when(p == 0)
        def _():
            x2_ref[...] = x1v + part

        @pl.when(p != 0)
        def _():
            x2_ref[...] = x2_ref[...] + part

    tm = min(2 * TM_FFN, s)
    return _ride(
        body, riders, name="ffn_fwd", grid=(s // tm, npart),
        in_specs=[pl.BlockSpec((tm, d), lambda i, p: (i, 0)), pl.BlockSpec(g2.shape, lambda i, p: (0, 0)),
                  pl.BlockSpec((None, d, fs), lambda i, p: (p, 0, 0)), pl.BlockSpec((None, fs, d), lambda i, p: (p, 0, 0))],
        out_specs=[pl.BlockSpec((tm, d), lambda i, p: (i, 0)), pl.BlockSpec((tm, fs), lambda i, p: (i, p))],
        out_shape=[_sds((s, d)), _sds((s, npart * fs), MXU_DTYPE)],
        operands=(x1, g2, w1, w2), sem=("parallel", "arbitrary"))


def ffn_bwd(x1, g2, w1, w2, r, dx2, riders=()):
    s, d = x1.shape
    npart, _, fs = w1.shape

    def body(x1_ref, g_ref, w1_ref, w2_ref, r_ref, dx2_ref, dx1p_ref, dg_ref, dw1_ref, dw2_ref):
        p = pl.program_id(0)
        i = pl.program_id(1)
        h2, vjp_norm = jax.vjp(_rms, x1_ref[...], g_ref[...])
        rr = r_ref[...].astype(F32)
        dy = dx2_ref[...]
        da = _dot(dy, w2_ref[...], NT) * (2.0 * rr)
        dx1, dg = vjp_norm(_dot(da, w1_ref[...], NT))
        dx1p_ref[...] = dx1
        _acc(dg_ref, dg, (p == 0) & (i == 0))
        _acc(dw1_ref, _dot(h2, da, TN), i == 0)
        _acc(dw2_ref, _dot(rr * rr, dy, TN), i == 0)

    tm = TM_FFN
    return _ride(
        body, riders, name="ffn_bwd", grid=(npart, s // tm),
        in_specs=[pl.BlockSpec((tm, d), lambda p, i: (i, 0)), pl.BlockSpec(g2.shape, lambda p, i: (0, 0)),
                  pl.BlockSpec((None, d, fs), lambda p, i: (p, 0, 0)), pl.BlockSpec((None, fs, d), lambda p, i: (p, 0, 0)),
                  pl.BlockSpec((tm, fs), lambda p, i: (i, p)), pl.BlockSpec((tm, d), lambda p, i: (i, 0))],
        out_specs=[pl.BlockSpec((None, tm, d), lambda p, i: (p, i, 0)), pl.BlockSpec(g2.shape, lambda p, i: (0, 0)),
                   pl.BlockSpec((None, d, fs), lambda p, i: (p, 0, 0)), pl.BlockSpec((None, fs, d), lambda p, i: (p, 0, 0))],
        out_shape=[_sds((npart, s, d)), _sds(g2.shape), _sds(w1.shape), _sds(w2.shape)],
        operands=(x1, g2, w1, w2, r, dx2), sem=("arbitrary", "arbitrary"))


def loss_head(y, target):
    s, d = y.shape

    def body(y_ref, t_ref, dy_ref, loss_ref):
        err = y_ref[...] - t_ref[...]
        dy_ref[...] = err * (1.0 / d)
        part = jnp.sum(jnp.sum(err * err, axis=-1, keepdims=True), axis=0, keepdims=True) * (0.5 / d)
        _acc(loss_ref, jnp.broadcast_to(part, loss_ref.shape), pl.program_id(0) == 0)

    return pl.pallas_call(
        body, name="loss_head", grid=(s // TM,),
        in_specs=[_rows(TM, d), _rows(TM, d)],
        out_specs=[_rows(TM, d), pl.BlockSpec((1, SLOT), lambda i: (0, 0))],
        out_shape=[_sds((s, d)), _sds((1, SLOT))],
        compiler_params=_cp(("arbitrary",)),
    )(y, target)


def _row_block(r):
    for b in (512, 256, 128, 64, 32, 16, 8):
        if r % b == 0:
            return b
    return r


def sum_cores(arrs, gots, half, me):
    n = len(arrs)

    def body(sp_ref, *refs):
        for i in range(n):
            a_ref, g_ref, wire_ref, own_ref = refs[i], refs[n + i], refs[2 * n + i], refs[3 * n + i]
            tot = a_ref[...] + g_ref[...]
            wire_ref[...] = tot.astype(wire_ref.dtype)

            @pl.when(pl.program_id(0) == sp_ref[1])
            def _(own_ref=own_ref, tot=tot):
                own_ref[...] = tot

    shapes = [a.shape[2:] for a in arrs]
    grid_spec = pltpu.PrefetchScalarGridSpec(
        num_scalar_prefetch=1, grid=(N_CHIPS,),
        in_specs=[pl.BlockSpec((None, None) + sh, lambda p, sp: (p, sp[0], 0, 0)) for sh in shapes]
        + [pl.BlockSpec((None,) + sh, lambda p, sp: (p, 0, 0)) for sh in shapes],
        out_specs=[pl.BlockSpec((None,) + sh, lambda p, sp: (p, 0, 0)) for sh in shapes]
        + [pl.BlockSpec(sh, lambda p, sp: (0, 0)) for sh in shapes])
    outs = pl.pallas_call(body, name="sum_cores", grid_spec=grid_spec,
                          out_shape=[_sds((N_CHIPS,) + sh, BF16) for sh in shapes] + [_sds(sh) for sh in shapes],
                          compiler_params=_cp(("arbitrary",)))(jnp.stack([half, me]).astype(jnp.int32), *arrs, *gots)
    return outs[:n], outs[n:]


SUM_STEPS = 4


def sum_chips(owns, recvs, half):
    n = len(owns)

    def body(sp_ref, *refs):
        del sp_ref
        for i in range(n):
            own_ref, out_ref = refs[4 * i], refs[4 * n + i]
            r0, r1, r2 = (refs[4 * i + 1 + j][...].astype(F32) for j in range(3))
            out_ref[...] = ((own_ref[...] + r0) + r1) + r2

    in_specs, out_specs, operands = [], [], []
    for own, recv in zip(owns, recvs):
        r, c = own.shape
        br = r // SUM_STEPS
        in_specs.append(pl.BlockSpec((br, c), lambda i, sp: (i, 0)))
        in_specs += [pl.BlockSpec((None, br, c), functools.partial(lambda i, sp, j: (j, i, 0), j=j)) for j in range(3)]
        out_specs.append(pl.BlockSpec((None, br, c), lambda i, sp: (sp[0], i, 0)))
        operands += [own, recv, recv, recv]
    grid_spec = pltpu.PrefetchScalarGridSpec(num_scalar_prefetch=1, grid=(SUM_STEPS,), in_specs=in_specs, out_specs=out_specs)
    return pl.pallas_call(body, name="sum_chips", grid_spec=grid_spec, out_shape=[_sds((2,) + o.shape) for o in owns],
                          compiler_params=_cp(("parallel",)))(half.reshape(1).astype(jnp.int32), *operands)


def adamw(w, g, m, v, name, riders=()):
    r, c = w.shape
    br = _row_block(r)
    c1 = 1.0 / (1.0 - ADAM_B1 ** ADAM_STEP)
    c2 = 1.0 / (1.0 - ADAM_B2 ** ADAM_STEP)

    def body(w_ref, g_ref, m_ref, v_ref, d_ref, nm_ref, nv_ref):
        gg = g_ref[...]
        nm = ADAM_B1 * m_ref[...] + (1.0 - ADAM_B1) * gg
        nv = ADAM_B2 * v_ref[...] + (1.0 - ADAM_B2) * (gg * gg)
        d_ref[...] = -ADAM_LR * ((nm * c1) / (jnp.sqrt(nv * c2) + ADAM_EPS) + ADAM_WD * w_ref[...])
        nm_ref[...] = nm
        nv_ref[...] = nv

    return _ride(body, riders, name=name, grid=(r // br,), in_specs=[_rows(br, c)] * 4, out_specs=[_rows(br, c)] * 3,
                 out_shape=[_sds((r, c))] * 3, operands=(w, g, m, v), sem=("parallel",))


def _place():
    x, y, c = lax.axis_index("x"), lax.axis_index("y"), lax.axis_index("c")
    chips = [(1 - x, y), (x, 1 - y), (1 - x, 1 - y)]
    return x, y, c, chips


def _remote(src, dst, send_sem, recv_sem, to):
    return pltpu.make_async_remote_copy(src_ref=src, dst_ref=dst, send_sem=send_sem, recv_sem=recv_sem,
                                        device_id=to, device_id_type=MESH)


def gather_rider(arrs):
    n = len(arrs)
    me_chip = 2 * lax.axis_index("x") + lax.axis_index("y")
    bufs = [lax.dynamic_update_index_in_dim(lax.empty((N_CHIPS,) + a.shape, a.dtype), a, me_chip, 0) for a in arrs]

    def plan(ins, outs, sems):
        send_sems, recv_sems = sems
        x, y, c, chips = _place()
        me = 2 * x + y
        half, other, sibling = pl.ds(2 * c, 2), pl.ds(2 - 2 * c, 2), (x, y, 1 - c)
        cp = lambda i, k, src, dst, to: _remote(src, dst, send_sems.at[i, k], recv_sems.at[i, k], to)
        pairs = [(i, j, cx, cy) for i in range(n) for j, (cx, cy) in enumerate(chips)]
        blk = lambda i, cx, cy, part: outs[i].at[2 * cx + cy, part]
        first = lambda: [cp(i, j, ins[i].at[half], outs[i].at[me, half], (cx, cy, c)) for i, j, cx, cy in pairs]
        landed = lambda: [cp(i, j, blk(i, cx, cy, half), blk(i, cx, cy, half), (cx, cy, c)) for i, j, cx, cy in pairs]
        passed = lambda: [cp(i, 3 + j, blk(i, cx, cy, half), blk(i, cx, cy, half), sibling) for i, j, cx, cy in pairs]
        from_sibling = lambda: [cp(i, 3 + j, blk(i, cx, cy, other), blk(i, cx, cy, other), sibling) for i, j, cx, cy in pairs]
        return first, landed, passed, from_sibling

    def start(ins, outs, sems):
        for cp in plan(ins, outs, sems)[0]():
            cp.start()

    def finish(ins, outs, sems):
        first, landed, passed, from_sibling = plan(ins, outs, sems)
        forwards = passed()
        for a, b in zip(landed(), forwards):
            a.wait_recv()
            b.start()
        for cp in from_sibling():
            cp.wait_recv()
        for cp in first() + forwards:
            cp.wait_send()

    return Rider(list(arrs) + bufs, [_sds((N_CHIPS,) + a.shape, a.dtype) for a in arrs], {n + i: i for i in range(n)},
                 [pltpu.SemaphoreType.DMA((n, 6)), pltpu.SemaphoreType.DMA((n, 6))], start, finish)


class Reducer:
    def __init__(self, arrs):
        self.a = list(arrs)
        self.n = len(self.a)
        self.c = lax.axis_index("c")
        self.me = 2 * lax.axis_index("x") + lax.axis_index("y")

    def swap_rider(self):
        n = self.n

        def plan(ins, outs, sems):
            x, y, c, _ = _place()
            return [_remote(ins[i].at[p, 1 - c], outs[i].at[p], sems[0].at[i, p], sems[1].at[i, p], (x, y, 1 - c))
                    for i in range(n) for p in range(N_CHIPS)]

        return Rider(self.a, [_sds((N_CHIPS,) + a.shape[2:]) for a in self.a], {},
                     [pltpu.SemaphoreType.DMA((n, N_CHIPS)), pltpu.SemaphoreType.DMA((n, N_CHIPS))],
                     lambda *r: [cp.start() for cp in plan(*r)], lambda *r: [cp.wait() for cp in plan(*r)])

    def after_swap(self, got):
        self.wire, self.own = sum_cores(self.a, got, self.c, self.me)

    def scatter_rider(self):
        n = self.n

        def plan(ins, outs, sems):
            x, y, c, chips = _place()
            return [_remote(ins[i].at[2 * cx + cy], outs[i].at[j], sems[0].at[i, j], sems[1].at[i, j], (cx, cy, c))
                    for i in range(n) for j, (cx, cy) in enumerate(chips)]

        return Rider(self.wire, [_sds((3,) + w.shape[1:], w.dtype) for w in self.wire], {},
                     [pltpu.SemaphoreType.DMA((n, 3)), pltpu.SemaphoreType.DMA((n, 3))],
                     lambda *r: [cp.start() for cp in plan(*r)], lambda *r: [cp.wait() for cp in plan(*r)])

    def after_scatter(self, recv):
        self.full = sum_chips(self.own, recv, self.c)

    def share_rider(self):
        n = self.n

        def plan(ins, outs, sems):
            x, y, c, _ = _place()
            return [_remote(ins[i].at[c], outs[i].at[c], sems[0].at[i], sems[1].at[i], (x, y, 1 - c)) for i in range(n)]

        return Rider(self.full, [_sds(f.shape) for f in self.full], {i: i for i in range(n)},
                     [pltpu.SemaphoreType.DMA((n,)), pltpu.SemaphoreType.DMA((n,))],
                     lambda *r: [cp.start() for cp in plan(*r)], lambda *r: [cp.wait() for cp in plan(*r)])

    def run(self):
        self.after_swap(run_rider(self.swap_rider(), "swap_halves"))
        self.after_scatter(run_rider(self.scatter_rider(), "scatter_chips"))
        return run_rider(self.share_rider(), "share_halves")


def _pad_slots(a, live):
    lead = a.shape[:-1]
    a = a.reshape(lead + (MLA_HEADS, live))
    a = jnp.pad(a, [(0, 0)] * len(lead) + [(0, 0), (0, SLOT - live)])
    return a.reshape(lead + (MLA_HEADS * SLOT,))


def _unpad_slots(a, live):
    lead = a.shape[:-1]
    return a.reshape(lead + (MLA_HEADS, SLOT))[..., :live].reshape(lead + (MLA_HEADS * live,))


def _rope_tables(positions, s):
    half = QK_ROPE // 2
    inv_freq = ROPE_THETA ** (-jnp.arange(half, dtype=F32) / half)
    ang = positions.reshape(s).astype(F32)[:, None] * inv_freq[None, :]
    cos, sin = jnp.cos(ang), jnp.sin(ang)
    one = jnp.ones((s, QK_NOPE), F32)
    z64, z16, z32 = jnp.zeros((s, QK_NOPE), F32), jnp.zeros((s, half), F32), jnp.zeros((s, SLOT - QK_DIM), F32)
    cos_t = jnp.concatenate([one, cos, cos, z32], axis=1)
    sin_a = jnp.concatenate([z64, -sin, z16, z32], axis=1)
    sin_b = jnp.concatenate([z64, z16, sin, z32], axis=1)
    return cos_t, sin_a, sin_b


def _out_weights(full):
    w_out = jnp.concatenate([full["w_out"][p] for p in range(N_CHIPS)], axis=0)
    woc = w_out[512:].reshape(MLA_HEADS, V_DIM, D_MODEL)
    woc = jnp.pad(woc, ((0, 0), (0, SLOT - V_DIM), (0, 0))).reshape(MLA_HEADS * SLOT, D_MODEL)
    return dict(woa=w_out[:256], wob=w_out[256:512], woc=woc)


def _layer_weights(full, small, l):
    w_in = jnp.concatenate([full["w_in"][p] for p in range(N_CHIPS)], axis=1)
    wc = jnp.pad(w_in[:, 1536:], ((0, 0), (0, 512 - (w_in.shape[1] - 1536))))
    w_uq = jnp.concatenate([full["mla_w_uq"][p] for p in range(N_CHIPS)], axis=1)
    w_ukv = jnp.concatenate([full["mla_w_ukv"][p] for p in range(N_CHIPS)], axis=1)
    ukv = w_ukv.reshape(KV_LORA, MLA_HEADS, QK_NOPE + V_DIM)
    row = lambda a: a.reshape(1, -1)
    return dict(
        g1=row(small["norm1_gain"][l]), wa=w_in[:, :512], wb=w_in[:, 512:1536], wc=wc,
        vg=row(small["gm_v_gain"][l]), ws=small["gm_w_s"][l], bs=small["gm_b_s"][l].reshape(4, CHUNK, 1),
        gog=row(small["gm_out_gain"][l]), hog=small["hg_out_gain"][l].reshape(-1, 1),
        qag=row(small["mla_q_a_gain"][l]), kvag=row(small["mla_kv_a_gain"][l]),
        qg=row(jnp.pad(small["mla_q_gain"][l], (0, SLOT - QK_DIM))), kg=row(jnp.pad(small["mla_k_gain"][l], (0, SLOT - QK_DIM))),
        wq=_pad_slots(w_uq, QK_DIM), wk=_pad_slots(ukv[..., :QK_NOPE].reshape(KV_LORA, -1), QK_NOPE),
        wv=_pad_slots(ukv[..., QK_NOPE:].reshape(KV_LORA, -1), V_DIM),
        mog=row(_pad_slots(small["mla_out_gain"][l], V_DIM)),
        g2=row(small["norm2_gain"][l]),
    )


def _shard_cols(a):
    r, c4 = a.shape
    return a.reshape(r, N_CHIPS, c4 // N_CHIPS).transpose(1, 0, 2)


def local_step(x, positions, target, small, comm):
    s = x.shape[0]
    cos_t, sin_a, sin_b = _rope_tables(positions, s)
    lbs = lower_bounds_fwd(small["hg_lower_bound"])
    lw, saved = [], []
    for l in range(DEPTH):
        w = _layer_weights(comm.part(l, "in"), small, l)
        lw.append(w)
        lb = lbs[l].reshape(-1, 1)
        pa, pb, pc = inproj_fwd(x, w["g1"], w["wa"], w["wb"], w["wc"])
        ya = gm_fwd(pa, w["vg"], w["ws"], w["bs"], w["gog"])
        (yb, states), got = hg_fwd(pb, lb, w["hog"], [comm.gather_rider(l, "ff1")])
        comm.gathered(l, "ff1", got[0])
        q, k, v = mla_pre_fwd(pc, cos_t, sin_a, sin_b, w["qag"], w["kvag"], w["qg"], w["kg"], w["wq"], w["wk"], w["wv"])
        (o, lse), got = attn_fwd(q, k, v, [comm.gather_rider(l, "ff2"), comm.gather_rider(l, "out")])
        comm.gathered(l, "ff2", got[0])
        comm.gathered(l, "out", got[1])
        w.update(_out_weights(comm.part(l, "out")))
        x1 = outproj_fwd(x, ya, yb, o, w["mog"], w["woa"], w["wob"], w["woc"])
        w["w1"], w["w2"] = comm.part(l, "ff1")["w_ff1"], comm.part(l, "ff2")["w_ff2"]
        rider = comm.gather_rider(l + 1, "in") if l + 1 < DEPTH else None
        (x2, r), got = ffn_fwd(x1, w["g2"], w["w1"], w["w2"], [rider])
        comm.gathered(l + 1, "in", got[0])
        saved.append(dict(x=x, pa=pa, pb=pb, pc=pc, ya=ya, yb=yb, states=states, q=q, k=k, v=v, o=o, lse=lse, x1=x1, r=r, lb=lb))
        x = x2
    dx, loss_part = loss_head(x, target)
    groups = [dict() for _ in range(DEPTH)]
    sm = {n: [None] * DEPTH for n in ("norm1_gain", "gm_v_gain", "gm_w_s", "gm_b_s", "gm_out_gain", "hg_out_gain",
                                       "mla_q_a_gain", "mla_kv_a_gain", "mla_q_gain", "mla_k_gain", "mla_out_gain",
                                       "norm2_gain")}
    dlbs = [None] * DEPTH
    halves = lambda g: g.reshape(N_CHIPS, 2, g.shape[1] // 2, g.shape[2])
    take = lambda red, f: None if red is None else f(red)
    red_mix = None
    for l in reversed(range(DEPTH)):
        w, a = lw[l], saved[l]
        (dx1p, dg2, dw1, dw2), got = ffn_bwd(a["x1"], w["g2"], w["w1"], w["w2"], a["r"], dx,
                                             [take(red_mix, Reducer.swap_rider)])
        if red_mix:
            red_mix.after_swap(got[0])
        ffn_arrs = [halves(dw1), halves(dw2)]
        red_ffn = comm.reducer(ffn_arrs)
        (dx1, dya, dyb, do, dmog, dwoa, dwob, dwoc), got = outproj_bwd(
            a["ya"], a["yb"], a["o"], w["mog"], w["woa"], w["wob"], w["woc"], dx, dx1p,
            [take(red_ffn, Reducer.swap_rider), take(red_mix, Reducer.scatter_rider)])
        if red_ffn:
            red_ffn.after_swap(got[0])
        if red_mix:
            red_mix.after_scatter(got[1])
        (dq, dk, dv), got = attn_bwd(a["q"], a["k"], a["v"], a["o"], do, a["lse"],
                                     [take(red_ffn, Reducer.scatter_rider), take(red_mix, Reducer.share_rider)])
        if red_ffn:
            red_ffn.after_scatter(got[0])
        if red_mix:
            groups[l + 1].update(zip(MIX, got[1]))
        dpc, dqag, dkvag, dqg, dkg, dwq, dwk, dwv = mla_pre_bwd(a["pc"], cos_t, sin_a, sin_b, w["qag"], w["kvag"], w["qg"],
                                                                  w["kg"], w["wq"], w["wk"], w["wv"], dq, dk, dv)
        (dpb, dlb, dhog), got = hg_bwd(a["pb"], a["lb"], w["hog"], a["states"], dyb, [take(red_ffn, Reducer.share_rider)])
        groups[l].update(zip(FFN, got[0] if red_ffn else ffn_arrs))
        dpa, dvg, dws, dbs, dgog = gm_bwd(a["pa"], w["vg"], w["ws"], w["bs"], w["gog"], dya)
        dx, dg1, dwa, dwb, dwc = inproj_bwd(a["x"], w["g1"], w["wa"], w["wb"], w["wc"], dpa, dpb, dpc, dx1)
        dukv = jnp.concatenate([dwk.reshape(KV_LORA, MLA_HEADS, SLOT)[..., :QK_NOPE],
                                dwv.reshape(KV_LORA, MLA_HEADS, SLOT)[..., :V_DIM]], axis=-1)
        dwo = jnp.concatenate([dwoa, dwob, dwoc.reshape(MLA_HEADS, SLOT, D_MODEL)[:, :V_DIM].reshape(-1, D_MODEL)], axis=0)
        mix_arrs = [halves(_shard_cols(jnp.concatenate([dwa, dwb, dwc[:, :1952 - 1536]], axis=1))),
                    halves(_shard_cols(_unpad_slots(dwq, QK_DIM))), halves(_shard_cols(dukv.reshape(KV_LORA, -1))),
                    halves(dwo.reshape(N_CHIPS, -1, D_MODEL))]
        red_mix = comm.reducer(mix_arrs) if l > 0 else None
        if red_mix is None:
            groups[l].update(zip(MIX, mix_arrs))
        sm["norm1_gain"][l] = dg1[0]
        sm["gm_v_gain"][l] = dvg[0]
        sm["gm_w_s"][l] = dws
        sm["gm_b_s"][l] = dbs[..., 0]
        sm["gm_out_gain"][l] = dgog[0]
        sm["hg_out_gain"][l] = dhog[:, 0]
        sm["mla_q_a_gain"][l] = dqag[0]
        sm["mla_kv_a_gain"][l] = dkvag[0]
        sm["mla_q_gain"][l] = dqg[0, :QK_DIM]
        sm["mla_k_gain"][l] = dkg[0, :QK_DIM]
        sm["mla_out_gain"][l] = _unpad_slots(dmog[0], V_DIM)
        sm["norm2_gain"][l] = dg2[0]
        dlbs[l] = dlb[:, 0]
    sm["hg_lower_bound"] = [lower_bounds_bwd(small["hg_lower_bound"], jnp.stack(dlbs))]
    return loss_part, dx, groups, sm


MIX = ("w_in", "mla_w_uq", "mla_w_ukv", "w_out")
FFN = ("w_ff1", "w_ff2")
BIG = MIX + FFN
PARTS = {"in": ("w_in", "mla_w_uq", "mla_w_ukv"), "out": ("w_out",), "ff1": ("w_ff1",), "ff2": ("w_ff2",)}
SMALL = ("norm1_gain", "gm_v_gain", "gm_w_s", "gm_b_s", "gm_out_gain", "hg_lower_bound", "hg_out_gain",
         "mla_q_a_gain", "mla_kv_a_gain", "mla_q_gain", "mla_k_gain", "mla_out_gain", "norm2_gain")
ORDER = ("norm1_gain", "w_in", "gm_v_gain", "gm_w_s", "gm_b_s", "gm_out_gain", "hg_lower_bound", "hg_out_gain",
         "mla_q_a_gain", "mla_w_uq", "mla_kv_a_gain", "mla_w_ukv", "mla_q_gain", "mla_k_gain", "mla_out_gain",
         "w_out", "norm2_gain", "w_ff1", "w_ff2")
PACK_ROWS = 320


def _pack(pieces):
    flat = jnp.concatenate([a.reshape(-1) for a in pieces])
    total = 2 * N_CHIPS * PACK_ROWS * SLOT
    return jnp.pad(flat, (0, total - flat.shape[0]))


def _unpack(flat, shapes):
    out, off = [], 0
    for sh in shapes:
        size = 1
        for d in sh:
            size *= d
        out.append(flat[off:off + size].reshape(sh))
        off += size
    return out


class ChipComm:
    def __init__(self, shards):
        self.shards = shards
        self.full = {}

    def gather_rider(self, l, part):
        return gather_rider([self.shards[n][l].astype(MXU_DTYPE).reshape(4, self.shards[n].shape[1] // 4, -1)
                             for n in PARTS[part]])

    def gathered(self, l, part, outs):
        if outs is not None:
            self.full[l, part] = {n: o.reshape((N_CHIPS,) + self.shards[n].shape[1:]) for n, o in zip(PARTS[part], outs)}

    def part(self, l, part):
        if (l, part) not in self.full:
            self.gathered(l, part, run_rider(self.gather_rider(l, part), "gather_weights"))
        return self.full[l, part]

    def reducer(self, arrs):
        return Reducer(arrs)


def kernel(x, positions, norm1_gain, w_in, gm_v_gain, gm_w_s, gm_b_s, gm_out_gain, hg_lower_bound, hg_out_gain, mla_q_a_gain, mla_w_uq, mla_kv_a_gain, mla_w_ukv, mla_q_gain, mla_k_gain, mla_out_gain, w_out, norm2_gain, w_ff1, w_ff2, loss_target, m_norm1_gain, m_w_in, m_gm_v_gain, m_gm_w_s, m_gm_b_s, m_gm_out_gain, m_hg_lower_bound, m_hg_out_gain, m_mla_q_a_gain, m_mla_w_uq, m_mla_kv_a_gain, m_mla_w_ukv, m_mla_q_gain, m_mla_k_gain, m_mla_out_gain, m_w_out, m_norm2_gain, m_w_ff1, m_w_ff2, v_norm1_gain, v_w_in, v_gm_v_gain, v_gm_w_s, v_gm_b_s, v_gm_out_gain, v_hg_lower_bound, v_hg_out_gain, v_mla_q_a_gain, v_mla_w_uq, v_mla_kv_a_gain, v_mla_w_ukv, v_mla_q_gain, v_mla_k_gain, v_mla_out_gain, v_w_out, v_norm2_gain, v_w_ff1, v_w_ff2):
    given = dict(locals())
    weights = {n: given[n] for n in ORDER}
    moms = {n: given["m_" + n] for n in ORDER}
    vars_ = {n: given["v_" + n] for n in ORDER}
    s, d = x.shape[1], x.shape[2]

    small = {n: weights[n] for n in SMALL}
    comm = ChipComm({n: weights[n] for n in BIG})
    loss_part, dx, groups, small_g = local_step(x.reshape(s, d), positions, loss_target.reshape(s, d), small, comm)
    loss = lax.psum(loss_part[0, 0], ("x", "y", "c"))

    pack_g = _pack([jnp.stack(small_g[n]) for n in SMALL]).reshape(N_CHIPS, 2, PACK_ROWS, SLOT)
    reduced = Reducer([groups[0][n] for n in MIX] + [pack_g]).run()
    groups[0].update(zip(MIX, reduced[:-1]))
    pack_full = run_rider(gather_rider([reduced[-1].reshape(4, PACK_ROWS // 2, SLOT)]), "gather_small")[0].reshape(-1)
    grads = {n: jnp.stack([groups[l][n].reshape(weights[n].shape[1:]) for l in range(DEPTH)]) for n in BIG}
    grads.update(zip(SMALL, _unpack(pack_full, [weights[n].shape for n in SMALL])))

    delta, new_m, new_v = {}, {}, {}
    flat2 = lambda a: a.reshape(-1, a.shape[-1])
    for n in ORDER:
        outs, _ = adamw(flat2(weights[n]), flat2(grads[n]), flat2(moms[n]), flat2(vars_[n]), "adamw_" + n)
        delta[n], new_m[n], new_v[n] = [o.reshape(weights[n].shape) for o in outs]

    return (loss, dx.reshape(x.shape), *[grads[n] for n in ORDER], *[delta[n] for n in ORDER],
            *[new_m[n] for n in ORDER], *[new_v[n] for n in ORDER])
```

```python
import functools

import jax
import jax.numpy as jnp
from jax import lax
from jax.experimental import pallas as pl
from jax.experimental.pallas import tpu as pltpu

F32 = jnp.float32
BF16 = jnp.bfloat16
MXU_DTYPE = BF16

D_MODEL = 1024
DEPTH = 4
CHUNK = 128
HG_CHUNK = 128
HG_CHUNKS = 2
GM_CHUNKS = 4
EPS = 1e-6
HEAD64 = 64
MLA_HEADS = 8
QK_NOPE = 64
QK_ROPE = 32
QK_DIM = 96
V_DIM = 64
Q_LORA = 256
KV_LORA = 128
SLOT = 128
ROPE_THETA = 10000.0
D_FF_SHARD = 1024
N_CHIPS = 4

ADAM_LR = 0.001
ADAM_B1 = 0.9
ADAM_B2 = 0.999
ADAM_EPS = 1e-08
ADAM_WD = 0.01
ADAM_STEP = 10

TM = 512
TM_FFN = 512
TQ = 256
ATT_HEADS_PER_STEP = 4
ATT_WIDE = 512
VMEM_LIMIT = 56 * 1024 * 1024

NN = (((1,), (0,)), ((), ()))
NT = (((1,), (1,)), ((), ()))
TN = (((0,), (0,)), ((), ()))
BNN = (((2,), (1,)), ((0,), (0,)))
BNT = (((2,), (2,)), ((0,), (0,)))
BTN = (((1,), (1,)), ((0,), (0,)))


def _dot(a, b, dims):
    return lax.dot_general(a.astype(MXU_DTYPE), b.astype(MXU_DTYPE), dims, preferred_element_type=F32)


def _hdot(a, b, dims=NN):
    return lax.dot_general(a, b, dims, precision=lax.Precision.HIGHEST, preferred_element_type=F32)


def _make_ad(dims, da_dims, da_swap, db_dims, db_swap):
    @jax.custom_vjp
    def f(a, b):
        return _dot(a, b, dims)

    def fwd(a, b):
        return _dot(a, b, dims), (a, b)

    def bwd(res, g):
        a, b = res
        da = _dot(b, g, da_dims) if da_swap else _dot(g, b, da_dims)
        db = _dot(g, a, db_dims) if db_swap else _dot(a, g, db_dims)
        return da, db

    f.defvjp(fwd, bwd)
    return f


@functools.partial(jax.custom_vjp, nondiff_argnums=(1,))
def _roll_ad(x, shift):
    return pltpu.roll(x, shift, 1)


def _roll_ad_fwd(x, shift):
    return pltpu.roll(x, shift, 1), None


def _roll_ad_bwd(shift, _, g):
    return (pltpu.roll(g, (g.shape[1] - shift) % g.shape[1], 1),)


_roll_ad.defvjp(_roll_ad_fwd, _roll_ad_bwd)


class _Ops:
    pass


PLAIN = _Ops()
PLAIN.mm = lambda a, b: _dot(a, b, NN)
PLAIN.bmm = lambda a, b: _dot(a, b, BNN)
PLAIN.bmm_nt = lambda a, b: _dot(a, b, BNT)
PLAIN.bmm_tn = lambda a, b: _dot(a, b, BTN)
PLAIN.roll = lambda x, s: pltpu.roll(x, s, 1)

AD = _Ops()
AD.mm = _make_ad(NN, NT, False, TN, False)
AD.bmm = _make_ad(BNN, BNT, False, BTN, False)
AD.bmm_nt = _make_ad(BNT, BNN, False, BTN, True)
AD.bmm_tn = _make_ad(BTN, BNT, True, BNN, False)
AD.roll = _roll_ad


def _sigmoid(x):
    return jax.nn.sigmoid(x)


def _gelu(x):
    return 0.5 * x * (1.0 + jnp.tanh(0.7978845608028654 * (x + 0.044715 * (x * x * x))))


def _rms(x, g):
    return x * lax.rsqrt(jnp.mean(x * x, axis=-1, keepdims=True) + EPS) * g


def _head_masks256():
    lane = lax.broadcasted_iota(jnp.int32, (1, 4 * HEAD64), 1)
    return [(jnp.right_shift(lane, 6) == h).astype(F32) for h in range(4)]


def _headnorm256(x, g):
    ms = jnp.zeros_like(x)
    sq = x * x
    for m in _head_masks256():
        ms = ms + m * (jnp.sum(sq * m, axis=-1, keepdims=True) * (1.0 / HEAD64))
    return x * lax.rsqrt(ms + EPS) * g


def _slot_norm(x, g, n):
    return x * lax.rsqrt(jnp.sum(x * x, axis=-1, keepdims=True) * (1.0 / n) + EPS) * g


def _rope(ops, x, cos_t, sin_a, sin_b):
    return x * cos_t + ops.roll(x, SLOT - QK_ROPE // 2) * sin_a + ops.roll(x, QK_ROPE // 2) * sin_b


def _inproj(ops, x, g1, wa, wb, wc):
    h = _rms(x, g1)
    return ops.mm(h, wa), ops.mm(h, wb), ops.mm(h, wc)


def _gm_chunk(ops, ur, vr, vg, ws4, bs, og):
    c = ur.shape[0]
    masks = _head_masks256()
    mh = jnp.concatenate([m[None] for m in masks], axis=0)
    u = _gelu(ur)
    v = _headnorm256(_gelu(vr), vg)
    t = lax.broadcasted_iota(jnp.int32, (c, c), 0)
    s = lax.broadcasted_iota(jnp.int32, (c, c), 1)
    w = jnp.where((t >= s)[None], ws4, 0.0)
    y = jnp.sum(ops.bmm(w, v[None] * mh), axis=0)
    for h in range(4):
        y = y + bs[h] * masks[h]
    return _headnorm256(u * y, og)


def _hg_chunk(ops, st, qr, fr, ir, gr, lb, og):
    c, n = qr.shape
    nh = n // HEAD64
    heads = lambda x: x.reshape(nh, HEAD64, x.shape[-1])
    tr = lambda x: heads(x.T)
    lb4, og4 = heads(lb), heads(og)
    qx = tr(qr)
    q = qx * _sigmoid(qx)
    f = lb4 + (1.0 - lb4) * _sigmoid(tr(fr))
    k = 1.0 - f
    logf = jnp.log(f)
    v = tr(ir)
    gx = tr(gr)
    s = lax.broadcasted_iota(jnp.int32, (c, c), 0)
    t = lax.broadcasted_iota(jnp.int32, (c, c), 1)
    tl = lax.broadcasted_iota(jnp.int32, (1, c), 1).reshape(1, 1, c)
    b2 = _hdot(logf.reshape(n, c), (s <= t).astype(F32))
    b = heads(b2)
    btot = jnp.sum(logf, axis=2, keepdims=True)
    inter = ops.bmm_tn(st, q * jnp.exp(b))
    p4 = jnp.zeros((nh, c, c), F32)
    tt, ss = s, t
    lg = c.bit_length() - 2
    while lg >= 0:
        m = 1 << lg
        bnd = jnp.left_shift(jnp.right_shift(t, lg + 1), lg + 1) + (m - 1)
        r = heads(_hdot(b2, (s == bnd).astype(F32)))
        right = jnp.bitwise_and(jnp.right_shift(tl, lg), 1) == 1
        qe = jnp.where(right, q * jnp.exp(jnp.where(right, b - r, 0.0)), 0.0)
        ke = jnp.where(right, 0.0, k * jnp.exp(jnp.where(right, 0.0, r - b)))
        lm = ((jnp.right_shift(tt, lg + 1) == jnp.right_shift(ss, lg + 1))
              & (jnp.bitwise_and(jnp.right_shift(tt, lg), 1) == 1)
              & (jnp.bitwise_and(jnp.right_shift(ss, lg), 1) == 0))
        p4 = jnp.where(lm[None], ops.bmm_tn(qe, ke), p4)
        lg -= 1
    intra = ops.bmm_nt(v, p4)
    o = inter + intra + jnp.sum(q * k, axis=1, keepdims=True) * v
    st_new = st * jnp.exp(btot) + ops.bmm_nt(k * jnp.exp(btot - b), v)
    y = o * lax.rsqrt(jnp.mean(o * o, axis=1, keepdims=True) + EPS) * og4 * (gx * _sigmoid(gx))
    return st_new, y.reshape(n, c).T


def _mla_pre(ops, cq, ckv, kpe, cos_t, sin_a, sin_b, qag, kvag, qg, kg, wq, wk, wv):
    cqn = _rms(cq, qag)
    ckvn = _rms(ckv, kvag)
    kper = ops.roll(kpe, QK_NOPE)
    qs, ks, vs = [], [], []
    for h in range(MLA_HEADS):
        qh = _slot_norm(ops.mm(cqn, wq[h]), qg, QK_DIM)
        qs.append(_rope(ops, qh, cos_t, sin_a, sin_b))
        kh = _slot_norm(ops.mm(ckvn, wk[h]) + kper, kg, QK_DIM)
        ks.append(_rope(ops, kh, cos_t, sin_a, sin_b))
        vs.append(ops.mm(ckvn, wv[h]))
    return qs, ks, vs


def _outproj(ops, x, ya, yb, o, mog, woa, wob, woc):
    acc = x + ops.mm(ya, woa) + ops.mm(yb, wob)
    for h in range(MLA_HEADS):
        acc = acc + ops.mm(_slot_norm(o[h], mog[h], V_DIM), woc[h])
    return acc


def _lower_bounds(r0, r1, r2, r3):
    mx = jnp.maximum(jnp.maximum(r0, r1), jnp.maximum(r2, r3))
    e0, e1, e2, e3 = jnp.exp(r0 - mx), jnp.exp(r1 - mx), jnp.exp(r2 - mx), jnp.exp(r3 - mx)
    inv = 1.0 / (e0 + e1 + e2 + e3)
    s1, s2, s3 = e1 * inv, e2 * inv, e3 * inv
    return jnp.zeros_like(r0), s1, s1 + s2, s1 + s2 + s3


def _cp(sem):
    return pltpu.CompilerParams(dimension_semantics=sem, vmem_limit_bytes=VMEM_LIMIT)


def _rows(tm, n):
    return pl.BlockSpec((tm, n), lambda i: (i, 0))


def _full(a):
    nd = len(a.shape)
    return pl.BlockSpec(a.shape, lambda *_: (0,) * nd, pipeline_mode=pl.Buffered(1))


def _sds(shape, dtype=F32):
    return jax.ShapeDtypeStruct(shape, dtype)


def _acc(ref, val, first):
    @pl.when(first)
    def _():
        ref[...] = val

    @pl.when(jnp.logical_not(first))
    def _():
        ref[...] = ref[...] + val


def _f32(ref):
    return ref[...].astype(F32)


MESH = pl.DeviceIdType.MESH
ANY = pl.BlockSpec(memory_space=pl.ANY)


class Rider:
    def __init__(self, arrays, out_shapes, aliases, sems, start, finish):
        self.arrays, self.out_shapes, self.aliases, self.sems = list(arrays), list(out_shapes), dict(aliases), list(sems)
        self.start, self.finish = start, finish


def run_rider(rider, name):
    n_in, n_out = len(rider.arrays), len(rider.out_shapes)

    def body(*refs):
        ins, outs, sems = refs[:n_in], refs[n_in:n_in + n_out], refs[n_in + n_out:]
        rider.start(ins, outs, sems)
        rider.finish(ins, outs, sems)

    return pl.pallas_call(
        body, name=name, in_specs=[ANY] * n_in, out_specs=[ANY] * n_out, out_shape=rider.out_shapes,
        input_output_aliases=rider.aliases, scratch_shapes=rider.sems,
    )(*rider.arrays)


def _merge_riders(riders):
    bounds, a0, o0, s0 = [], 0, 0, 0
    for r in riders:
        bounds.append((a0, o0, s0))
        a0, o0, s0 = a0 + len(r.arrays), o0 + len(r.out_shapes), s0 + len(r.sems)

    def part(k, ins, outs, sems):
        a, o, s = bounds[k]
        r = riders[k]
        return ins[a:a + len(r.arrays)], outs[o:o + len(r.out_shapes)], sems[s:s + len(r.sems)]

    return Rider(
        [x for r in riders for x in r.arrays], [x for r in riders for x in r.out_shapes],
        {bounds[k][0] + i: bounds[k][1] + o for k, r in enumerate(riders) for i, o in r.aliases.items()},
        [x for r in riders for x in r.sems],
        lambda *refs: [r.start(*part(k, *refs)) for k, r in enumerate(riders)],
        lambda *refs: [r.finish(*part(k, *refs)) for k, r in enumerate(riders)])


def _ride(compute, riders, *, name, grid, in_specs, out_specs, out_shape, operands, scratch_shapes=(), sem=None):
    single = not isinstance(out_shape, (list, tuple))
    if single:
        out_specs, out_shape = [out_specs], [out_shape]
    live = [r for r in riders if r is not None]
    if not live:
        res = pl.pallas_call(compute, name=name, grid=grid, in_specs=in_specs, out_specs=out_specs, out_shape=out_shape,
                             scratch_shapes=list(scratch_shapes), compiler_params=_cp(sem))(*operands)
        return (res[0] if single else res), [None] * len(riders)
    rider = live[0] if len(live) == 1 else _merge_riders(live)
    n_in, n_out, n_s = len(in_specs), len(out_specs), len(scratch_shapes)
    r_in, r_out = len(rider.arrays), len(rider.out_shapes)

    def body(*refs):
        ins, rins = refs[:n_in], refs[n_in:n_in + r_in]
        outs = refs[n_in + r_in:n_in + r_in + n_out]
        routs = refs[n_in + r_in + n_out:n_in + r_in + n_out + r_out]
        scr = refs[n_in + r_in + n_out + r_out:n_in + r_in + n_out + r_out + n_s]
        rsems = refs[n_in + r_in + n_out + r_out + n_s:]
        first = functools.reduce(jnp.logical_and, [pl.program_id(a) == 0 for a in range(len(grid))])
        last = functools.reduce(jnp.logical_and, [pl.program_id(a) == grid[a] - 1 for a in range(len(grid))])

        @pl.when(first)
        def _():
            rider.start(rins, routs, rsems)

        compute(*ins, *outs, *scr)

        @pl.when(last)
        def _():
            rider.finish(rins, routs, rsems)

    res = pl.pallas_call(
        body, name=name, grid=grid, in_specs=list(in_specs) + [ANY] * r_in, out_specs=list(out_specs) + [ANY] * r_out,
        out_shape=list(out_shape) + rider.out_shapes,
        input_output_aliases={n_in + k: n_out + v for k, v in rider.aliases.items()},
        scratch_shapes=list(scratch_shapes) + rider.sems, compiler_params=_cp(("arbitrary",) * len(grid)),
    )(*operands, *rider.arrays)
    main, rest, per_rider = res[:n_out], list(res[n_out:]), []
    for r in riders:
        per_rider.append(None if r is None else [rest.pop(0) for _ in r.out_shapes])
    return (main[0] if single else main), per_rider


def inproj_fwd(x, g1, wa, wb, wc):
    s, d = x.shape

    def body(x_ref, g_ref, wa_ref, wb_ref, wc_ref, pa_ref, pb_ref, pc_ref):
        pa, pb, pc = _inproj(PLAIN, x_ref[...], g_ref[...], wa_ref[...], wb_ref[...], wc_ref[...])
        pa_ref[...] = pa
        pb_ref[...] = pb
        pc_ref[...] = pc

    return pl.pallas_call(
        body, name="inproj_fwd", grid=(s // TM,),
        in_specs=[_rows(TM, d), _full(g1), _full(wa), _full(wb), _full(wc)],
        out_specs=[_rows(TM, wa.shape[1]), _rows(TM, wb.shape[1]), _rows(TM, wc.shape[1])],
        out_shape=[_sds((s, wa.shape[1])), _sds((s, wb.shape[1])), _sds((s, wc.shape[1]))],
        compiler_params=_cp(("parallel",)),
    )(x, g1, wa, wb, wc)


def inproj_bwd(x, g1, wa, wb, wc, dpa, dpb, dpc, dres, riders=()):
    s, d = x.shape

    def body(x_ref, g_ref, wa_ref, wb_ref, wc_ref, dpa_ref, dpb_ref, dpc_ref, dres_ref,
             dx_ref, dg_ref, dwa_ref, dwb_ref, dwc_ref):
        first = pl.program_id(0) == 0
        _, vjp = jax.vjp(functools.partial(_inproj, AD), x_ref[...], g_ref[...],
                         _f32(wa_ref), _f32(wb_ref), _f32(wc_ref))
        dx, dg, dwa, dwb, dwc = vjp((dpa_ref[...], dpb_ref[...], dpc_ref[...]))
        dx_ref[...] = dx + dres_ref[...]
        _acc(dg_ref, dg, first)
        _acc(dwa_ref, dwa, first)
        _acc(dwb_ref, dwb, first)
        _acc(dwc_ref, dwc, first)

    return _ride(
        body, riders, name="inproj_bwd", grid=(s // TM,),
        in_specs=[_rows(TM, d), _full(g1), _full(wa), _full(wb), _full(wc),
                  _rows(TM, wa.shape[1]), _rows(TM, wb.shape[1]), _rows(TM, wc.shape[1]), _rows(TM, d)],
        out_specs=[_rows(TM, d), _full(g1), _full(wa), _full(wb), _full(wc)],
        out_shape=[_sds((s, d)), _sds(g1.shape), _sds(wa.shape), _sds(wb.shape), _sds(wc.shape)],
        operands=(x, g1, wa, wb, wc, dpa, dpb, dpc, dres), sem=("arbitrary",))


def gm_fwd(pa, vg, ws4, bs, og):
    s = pa.shape[0]
    w = pa.shape[1] // 2

    def body(pa_ref, vg_ref, ws_ref, bs_ref, og_ref, ya_ref):
        bsl = [bs_ref[h] for h in range(4)]
        for j in range(GM_CHUNKS):
            rows = slice(j * CHUNK, (j + 1) * CHUNK)
            ya_ref[rows, :] = _gm_chunk(PLAIN, pa_ref[rows, 0:w], pa_ref[rows, w:2 * w], vg_ref[...], ws_ref[...], bsl,
                                        og_ref[...])

    tm = GM_CHUNKS * CHUNK
    return pl.pallas_call(
        body, name="gm_fwd", grid=(s // tm,),
        in_specs=[_rows(tm, 2 * w), _full(vg), _full(ws4), _full(bs), _full(og)],
        out_specs=_rows(tm, w), out_shape=_sds((s, w)),
        compiler_params=_cp(("parallel",)),
    )(pa, vg, ws4, bs, og)


def gm_bwd(pa, vg, ws4, bs, og, dya):
    s = pa.shape[0]
    w = pa.shape[1] // 2

    def body(pa_ref, vg_ref, ws_ref, bs_ref, og_ref, dya_ref, dpa_ref, dvg_ref, dws_ref, dbs_ref, dog_ref):
        first = pl.program_id(0) == 0
        bsl = [bs_ref[h] for h in range(4)]
        tot = None
        for j in range(GM_CHUNKS):
            rows = slice(j * CHUNK, (j + 1) * CHUNK)
            _, vjp = jax.vjp(functools.partial(_gm_chunk, AD), pa_ref[rows, 0:w], pa_ref[rows, w:2 * w],
                             vg_ref[...], ws_ref[...], bsl, og_ref[...])
            du, dv, *dws = vjp(dya_ref[rows, :])
            dpa_ref[rows, 0:w] = du
            dpa_ref[rows, w:2 * w] = dv
            tot = dws if tot is None else jax.tree.map(jnp.add, tot, dws)
        dvg, dws, dbs, dog = tot
        _acc(dvg_ref, dvg, first)
        _acc(dws_ref, dws, first)
        _acc(dog_ref, dog, first)
        for h in range(4):
            _acc(dbs_ref.at[h], dbs[h], first)

    tm = GM_CHUNKS * CHUNK
    return pl.pallas_call(
        body, name="gm_bwd", grid=(s // tm,),
        in_specs=[_rows(tm, 2 * w), _full(vg), _full(ws4), _full(bs), _full(og), _rows(tm, w)],
        out_specs=[_rows(tm, 2 * w), _full(vg), _full(ws4), _full(bs), _full(og)],
        out_shape=[_sds((s, 2 * w)), _sds(vg.shape), _sds(ws4.shape), _sds(bs.shape), _sds(og.shape)],
        compiler_params=_cp(("arbitrary",)),
    )(pa, vg, ws4, bs, og, dya)


def hg_fwd(pb, lb, og, riders=()):
    s = pb.shape[0]
    w = pb.shape[1] // 4
    tm = HG_CHUNKS * HG_CHUNK
    st_shape = (w // HEAD64, HEAD64, HEAD64)

    def body(pb_ref, lb_ref, og_ref, yb_ref, states_ref, st_ref):
        @pl.when(pl.program_id(0) == 0)
        def _():
            st_ref[...] = jnp.zeros_like(st_ref)

        st = st_ref[...]
        for j in range(HG_CHUNKS):
            rows = slice(j * HG_CHUNK, (j + 1) * HG_CHUNK)
            states_ref[j] = st
            st, y = _hg_chunk(PLAIN, st, pb_ref[rows, 0:w], pb_ref[rows, w:2 * w], pb_ref[rows, 2 * w:3 * w],
                              pb_ref[rows, 3 * w:4 * w], lb_ref[...], og_ref[...])
            yb_ref[rows, :] = y
        st_ref[...] = st

    return _ride(
        body, riders, name="hg_fwd", grid=(s // tm,),
        in_specs=[_rows(tm, 4 * w), _full(lb), _full(og)],
        out_specs=[_rows(tm, w), pl.BlockSpec((HG_CHUNKS,) + st_shape, lambda i: (i, 0, 0, 0))],
        out_shape=[_sds((s, w)), _sds((s // HG_CHUNK,) + st_shape)],
        scratch_shapes=[pltpu.VMEM(st_shape, F32)],
        operands=(pb, lb, og), sem=("arbitrary",))


def hg_bwd(pb, lb, og, states, dyb, riders=()):
    s = pb.shape[0]
    w = pb.shape[1] // 4
    tm = HG_CHUNKS * HG_CHUNK
    nc = s // tm
    st_shape = (w // HEAD64, HEAD64, HEAD64)

    def body(pb_ref, lb_ref, og_ref, states_ref, dyb_ref, dpb_ref, dlb_ref, dog_ref, dst_ref):
        first = pl.program_id(0) == 0

        @pl.when(first)
        def _():
            dst_ref[...] = jnp.zeros_like(dst_ref)

        dst, dlb, dog = dst_ref[...], None, None
        for j in reversed(range(HG_CHUNKS)):
            rows = slice(j * HG_CHUNK, (j + 1) * HG_CHUNK)
            _, vjp = jax.vjp(functools.partial(_hg_chunk, AD), states_ref[j], pb_ref[rows, 0:w], pb_ref[rows, w:2 * w],
                             pb_ref[rows, 2 * w:3 * w], pb_ref[rows, 3 * w:4 * w], lb_ref[...], og_ref[...])
            dst, dq, df, di, dg, dlb_j, dog_j = vjp((dst, dyb_ref[rows, :]))
            dpb_ref[rows, 0:w] = dq
            dpb_ref[rows, w:2 * w] = df
            dpb_ref[rows, 2 * w:3 * w] = di
            dpb_ref[rows, 3 * w:4 * w] = dg
            dlb = dlb_j if dlb is None else dlb + dlb_j
            dog = dog_j if dog is None else dog + dog_j
        dst_ref[...] = dst
        _acc(dlb_ref, dlb, first)
        _acc(dog_ref, dog, first)

    rev = lambda i: (nc - 1 - i, 0)
    return _ride(
        body, riders, name="hg_bwd", grid=(nc,),
        in_specs=[pl.BlockSpec((tm, 4 * w), rev), _full(lb), _full(og),
                  pl.BlockSpec((HG_CHUNKS,) + st_shape, lambda i: (nc - 1 - i, 0, 0, 0)), pl.BlockSpec((tm, w), rev)],
        out_specs=[pl.BlockSpec((tm, 4 * w), rev), _full(lb), _full(og)],
        out_shape=[_sds((s, 4 * w)), _sds(lb.shape), _sds(og.shape)],
        scratch_shapes=[pltpu.VMEM(st_shape, F32)],
        operands=(pb, lb, og, states, dyb), sem=("arbitrary",))


def lower_bounds_fwd(hlb):
    def body(h_ref, o_ref):
        outs = _lower_bounds(*[h_ref[pl.ds(i, 1), :] for i in range(DEPTH)])
        for i in range(DEPTH):
            o_ref[pl.ds(i, 1), :] = outs[i]

    return pl.pallas_call(body, name="lower_bounds_fwd", out_shape=_sds(hlb.shape))(hlb)


def lower_bounds_bwd(hlb, dlbs):
    def body(h_ref, d_ref, o_ref):
        _, vjp = jax.vjp(_lower_bounds, *[h_ref[pl.ds(i, 1), :] for i in range(DEPTH)])
        outs = vjp(tuple(d_ref[pl.ds(i, 1), :] for i in range(DEPTH)))
        for i in range(DEPTH):
            o_ref[pl.ds(i, 1), :] = outs[i]

    return pl.pallas_call(body, name="lower_bounds_bwd", out_shape=_sds(hlb.shape))(hlb, dlbs)


def _mla_pre_args(pc_ref, cos_ref, sa_ref, sb_ref, qag_ref, kvag_ref, qg_ref, kg_ref, wq_ref, wk_ref, wv_ref, cast):
    sl = lambda h: slice(h * SLOT, (h + 1) * SLOT)
    ld = (lambda r, h: r[:, sl(h)].astype(F32)) if cast else (lambda r, h: r[:, sl(h)])
    diff = (pc_ref[:, 0:Q_LORA], pc_ref[:, Q_LORA:Q_LORA + KV_LORA], pc_ref[:, Q_LORA + KV_LORA:Q_LORA + 2 * KV_LORA],
            qag_ref[...], kvag_ref[...], qg_ref[...], kg_ref[...],
            [ld(wq_ref, h) for h in range(MLA_HEADS)], [ld(wk_ref, h) for h in range(MLA_HEADS)],
            [ld(wv_ref, h) for h in range(MLA_HEADS)])
    tables = (cos_ref[...], sa_ref[...], sb_ref[...])
    return diff, tables


def _mla_pre_fn(ops, tables, cq, ckv, kpe, qag, kvag, qg, kg, wq, wk, wv):
    return _mla_pre(ops, cq, ckv, kpe, *tables, qag, kvag, qg, kg, wq, wk, wv)


def mla_pre_fwd(pc, cos_t, sin_a, sin_b, qag, kvag, qg, kg, wq, wk, wv, riders=()):
    s = pc.shape[0]
    hw = MLA_HEADS * SLOT

    def body(pc_ref, cos_ref, sa_ref, sb_ref, qag_ref, kvag_ref, qg_ref, kg_ref, wq_ref, wk_ref, wv_ref,
             q_ref, k_ref, v_ref):
        diff, tables = _mla_pre_args(pc_ref, cos_ref, sa_ref, sb_ref, qag_ref, kvag_ref, qg_ref, kg_ref,
                                     wq_ref, wk_ref, wv_ref, False)
        qs, ks, vs = _mla_pre_fn(PLAIN, tables, *diff)
        ones_lane = (lax.broadcasted_iota(jnp.int32, (1, SLOT), 1) == V_DIM).astype(F32)
        for h in range(MLA_HEADS):
            q_ref[:, h * SLOT:(h + 1) * SLOT] = qs[h].astype(q_ref.dtype)
            k_ref[:, h * SLOT:(h + 1) * SLOT] = ks[h].astype(k_ref.dtype)
            v_ref[:, h * SLOT:(h + 1) * SLOT] = (vs[h] + ones_lane).astype(v_ref.dtype)

    return _ride(
        body, riders, name="mla_pre_fwd", grid=(s // TM,),
        in_specs=[_rows(TM, pc.shape[1]), _rows(TM, SLOT), _rows(TM, SLOT), _rows(TM, SLOT),
                  _full(qag), _full(kvag), _full(qg), _full(kg), _full(wq), _full(wk), _full(wv)],
        out_specs=[_rows(TM, hw)] * 3, out_shape=[_sds((s, hw), MXU_DTYPE)] * 3,
        operands=(pc, cos_t, sin_a, sin_b, qag, kvag, qg, kg, wq, wk, wv), sem=("parallel",))


def mla_pre_bwd(pc, cos_t, sin_a, sin_b, qag, kvag, qg, kg, wq, wk, wv, dq, dk, dv):
    s = pc.shape[0]
    hw = MLA_HEADS * SLOT

    def body(pc_ref, cos_ref, sa_ref, sb_ref, qag_ref, kvag_ref, qg_ref, kg_ref, wq_ref, wk_ref, wv_ref,
             dq_ref, dk_ref, dv_ref, dpc_ref, dqag_ref, dkvag_ref, dqg_ref, dkg_ref, dwq_ref, dwk_ref, dwv_ref):
        first = pl.program_id(0) == 0
        diff, tables = _mla_pre_args(pc_ref, cos_ref, sa_ref, sb_ref, qag_ref, kvag_ref, qg_ref, kg_ref,
                                     wq_ref, wk_ref, wv_ref, True)
        _, vjp = jax.vjp(functools.partial(_mla_pre_fn, AD, tables), *diff)
        sl = lambda h: slice(h * SLOT, (h + 1) * SLOT)
        cot = ([dq_ref[:, sl(h)] for h in range(MLA_HEADS)], [dk_ref[:, sl(h)] for h in range(MLA_HEADS)],
               [dv_ref[:, sl(h)] for h in range(MLA_HEADS)])
        dcq, dckv, dkpe, dqag, dkvag, dqg, dkg, dwq, dwk, dwv = vjp(cot)
        dpc_ref[:, 0:Q_LORA] = dcq
        dpc_ref[:, Q_LORA:Q_LORA + KV_LORA] = dckv
        dpc_ref[:, Q_LORA + KV_LORA:Q_LORA + 2 * KV_LORA] = dkpe
        _acc(dqag_ref, dqag, first)
        _acc(dkvag_ref, dkvag, first)
        _acc(dqg_ref, dqg, first)
        _acc(dkg_ref, dkg, first)
        for h in range(MLA_HEADS):
            _acc(dwq_ref.at[:, sl(h)], dwq[h], first)
            _acc(dwk_ref.at[:, sl(h)], dwk[h], first)
            _acc(dwv_ref.at[:, sl(h)], dwv[h], first)

    return pl.pallas_call(
        body, name="mla_pre_bwd", grid=(s // TM,),
        in_specs=[_rows(TM, pc.shape[1]), _rows(TM, SLOT), _rows(TM, SLOT), _rows(TM, SLOT),
                  _full(qag), _full(kvag), _full(qg), _full(kg), _full(wq), _full(wk), _full(wv),
                  _rows(TM, hw), _rows(TM, hw), _rows(TM, hw)],
        out_specs=[_rows(TM, pc.shape[1]), _full(qag), _full(kvag), _full(qg), _full(kg),
                   _full(wq), _full(wk), _full(wv)],
        out_shape=[_sds(pc.shape), _sds(qag.shape), _sds(kvag.shape), _sds(qg.shape), _sds(kg.shape),
                   _sds(wq.shape), _sds(wk.shape), _sds(wv.shape)],
        compiler_params=_cp(("arbitrary",)),
    )(pc, cos_t, sin_a, sin_b, qag, kvag, qg, kg, wq, wk, wv, dq, dk, dv)


ATT_SCALE = QK_DIM ** -0.5
NEG_BIG = -1e30


def attn_fwd(q, k, v, riders=()):
    s = q.shape[0]
    nq = s // TQ
    hp = ATT_HEADS_PER_STEP
    sl = lambda j: slice(j * SLOT, (j + 1) * SLOT)

    wide = ATT_WIDE // TQ

    def body(q_ref, k_ref, v_ref, o_ref, lse_ref):
        qi = pl.program_id(1)
        lane = lax.broadcasted_iota(jnp.int32, (1, SLOT), 1)
        qs = [q_ref[:, sl(j)] for j in range(hp)]

        def step(ki, carry, n_tiles, masked):
            rk = pl.ds(pl.multiple_of(ki * TQ, TQ), n_tiles * TQ)
            if masked:
                row = lax.broadcasted_iota(jnp.int32, (TQ, n_tiles * TQ), 0) + (n_tiles - 1) * TQ
                col = lax.broadcasted_iota(jnp.int32, (TQ, n_tiles * TQ), 1)
            out = []
            for j in range(hp):
                m, acc = carry[j]
                sc = _dot(qs[j], k_ref[rk, sl(j)], NT) * ATT_SCALE
                if masked:
                    sc = jnp.where(row >= col, sc, NEG_BIG)
                m_new = jnp.maximum(m, jnp.max(sc, axis=-1, keepdims=True))
                acc = jnp.exp(m - m_new) * acc + _dot(jnp.exp(sc - m_new), v_ref[rk, sl(j)], NN)
                out.append((m_new, acc))
            return tuple(out)

        def tail_single(cr):
            cr = lax.fori_loop(n_wide * wide, qi, lambda ki, c: step(ki, c, 1, False), cr)
            return step(qi, cr, 1, True)

        n_wide = qi // wide
        init = tuple((jnp.full((TQ, 1), NEG_BIG, F32), jnp.zeros((TQ, SLOT), F32)) for _ in range(hp))
        carry = lax.fori_loop(0, n_wide, lambda kw, cr: step(kw * wide, cr, wide, False), init)
        carry = lax.cond(qi % wide == wide - 1, lambda cr: step(qi - (wide - 1), cr, wide, True), tail_single, carry)
        for j in range(hp):
            m, acc = carry[j]
            l = jnp.sum(jnp.where(lane == V_DIM, acc, 0.0), axis=-1, keepdims=True)
            o_ref[:, sl(j)] = jnp.where(lane < V_DIM, acc / l, 0.0)
            lse_ref[j] = m + jnp.log(l)

    head_col = pl.BlockSpec((s, hp * SLOT), lambda g, i: (0, g))
    tile = pl.BlockSpec((TQ, hp * SLOT), lambda g, i: (i, g))
    return _ride(
        body, riders, name="attn_fwd", grid=(MLA_HEADS // hp, nq),
        in_specs=[tile, head_col, head_col],
        out_specs=[tile, pl.BlockSpec((hp, TQ, 1), lambda g, i: (g, i, 0))],
        out_shape=[_sds((s, MLA_HEADS * SLOT)), _sds((MLA_HEADS, s, 1))],
        operands=(q, k, v), sem=("parallel", "parallel"))


def attn_bwd(q, k, v, o, do, lse, riders=()):
    s = q.shape[0]
    nq = s // TQ
    hp = ATT_HEADS_PER_STEP
    sl = lambda j: slice(j * SLOT, (j + 1) * SLOT)
    wide = ATT_WIDE // TQ

    def body(q_ref, k_ref, v_ref, o_ref, do_ref, lse_ref, dq_ref, dk_ref, dv_ref, delta_ref):
        ki = pl.program_id(1)

        @pl.when(ki == 0)
        def _():
            dq_ref[...] = jnp.zeros_like(dq_ref)

            def prep(i, c):
                rows = pl.ds(pl.multiple_of(i * TQ, TQ), TQ)
                for j in range(hp):
                    delta_ref[j, rows, :] = jnp.sum(do_ref[rows, sl(j)] * o_ref[rows, sl(j)], axis=-1, keepdims=True)
                return c

            lax.fori_loop(0, nq, prep, 0)

        kks = [k_ref[:, sl(j)] for j in range(hp)]
        vvs = [v_ref[:, sl(j)] for j in range(hp)]

        def step(qi, carry, n_tiles, masked):
            rq = pl.ds(pl.multiple_of(qi * TQ, TQ), n_tiles * TQ)
            if masked:
                row = lax.broadcasted_iota(jnp.int32, (n_tiles * TQ, TQ), 0)
                col = lax.broadcasted_iota(jnp.int32, (n_tiles * TQ, TQ), 1)
            out = []
            for j in range(hp):
                dk, dv = carry[j]
                qq = q_ref[rq, sl(j)]
                dd = do_ref[rq, sl(j)]
                sc = _dot(qq, kks[j], NT) * ATT_SCALE
                if masked:
                    sc = jnp.where(row >= col, sc, NEG_BIG)
                p = jnp.exp(sc - lse_ref[j, rq, :])
                dv = dv + _dot(p, dd, TN)
                ds = p * (_dot(dd, vvs[j], NT) - delta_ref[j, rq, :]) * ATT_SCALE
                dk = dk + _dot(ds, qq, TN)
                dq_ref[rq, sl(j)] = dq_ref[rq, sl(j)] + _dot(ds, kks[j], NN)
                out.append((dk, dv))
            return tuple(out)

        def head_single(cr):
            cr = step(ki, cr, 1, True)
            return lax.fori_loop(ki + 1, first_wide * wide, lambda qi, c: step(qi, c, 1, False), cr)

        zero = jnp.zeros((TQ, SLOT), F32)
        first_wide = (ki + wide) // wide
        carry = tuple((zero, zero) for _ in range(hp))
        carry = lax.cond(ki % wide == 0, lambda cr: step(ki, cr, wide, True), head_single, carry)
        carry = lax.fori_loop(first_wide, nq // wide, lambda qw, cr: step(qw * wide, cr, wide, False), carry)
        for j in range(hp):
            dk_ref[:, sl(j)] = carry[j][0]
            dv_ref[:, sl(j)] = carry[j][1]

    head_col = pl.BlockSpec((s, hp * SLOT), lambda g, i: (0, g))
    tile = pl.BlockSpec((TQ, hp * SLOT), lambda g, i: (i, g))
    return _ride(
        body, riders, name="attn_bwd", grid=(MLA_HEADS // hp, nq),
        in_specs=[head_col, tile, tile, head_col, head_col, pl.BlockSpec((hp, s, 1), lambda g, i: (g, 0, 0))],
        out_specs=[head_col, tile, tile],
        out_shape=[_sds((s, MLA_HEADS * SLOT))] * 3,
        scratch_shapes=[pltpu.VMEM((hp, s, 1), F32)],
        operands=(q, k, v, o, do, lse), sem=("arbitrary", "arbitrary"))


def _outproj_args(ya_ref, yb_ref, o_ref, mog_ref, woa_ref, wob_ref, woc_ref, cast):
    sl = lambda h: slice(h * SLOT, (h + 1) * SLOT)
    ldw = (lambda r: r[...].astype(F32)) if cast else (lambda r: r[...])
    ldc = (lambda h: woc_ref[sl(h), :].astype(F32)) if cast else (lambda h: woc_ref[sl(h), :])
    return (ya_ref[...], yb_ref[...], [o_ref[:, sl(h)] for h in range(MLA_HEADS)],
            [mog_ref[:, sl(h)] for h in range(MLA_HEADS)], ldw(woa_ref), ldw(wob_ref),
            [ldc(h) for h in range(MLA_HEADS)])


def outproj_fwd(x, ya, yb, o, mog, woa, wob, woc):
    s, d = x.shape

    def body(x_ref, ya_ref, yb_ref, o_ref, mog_ref, woa_ref, wob_ref, woc_ref, x1_ref):
        x1_ref[...] = _outproj(PLAIN, x_ref[...], *_outproj_args(ya_ref, yb_ref, o_ref, mog_ref, woa_ref, wob_ref,
                                                                  woc_ref, False))

    return pl.pallas_call(
        body, name="outproj_fwd", grid=(s // TM,),
        in_specs=[_rows(TM, d), _rows(TM, ya.shape[1]), _rows(TM, yb.shape[1]), _rows(TM, o.shape[1]),
                  _full(mog), _full(woa), _full(wob), _full(woc)],
        out_specs=_rows(TM, d), out_shape=_sds((s, d)),
        compiler_params=_cp(("parallel",)),
    )(x, ya, yb, o, mog, woa, wob, woc)


def outproj_bwd(ya, yb, o, mog, woa, wob, woc, dx2, dx1p, riders=()):
    s, d = dx2.shape
    npart = dx1p.shape[0]

    def body(ya_ref, yb_ref, o_ref, mog_ref, woa_ref, wob_ref, woc_ref, dx2_ref, dx1p_ref,
             dx1_ref, dya_ref, dyb_ref, do_ref, dmog_ref, dwoa_ref, dwob_ref, dwoc_ref):
        first = pl.program_id(0) == 0
        sl = lambda h: slice(h * SLOT, (h + 1) * SLOT)
        dx1 = dx2_ref[...]
        for p in range(npart):
            dx1 = dx1 + dx1p_ref[p]
        dx1_ref[...] = dx1
        args = _outproj_args(ya_ref, yb_ref, o_ref, mog_ref, woa_ref, wob_ref, woc_ref, True)
        _, vjp = jax.vjp(lambda *a: _outproj(AD, jnp.zeros_like(dx1), *a), *args)
        dya, dyb, do, dmog, dwoa, dwob, dwoc = vjp(dx1)
        dya_ref[...] = dya
        dyb_ref[...] = dyb
        _acc(dwoa_ref, dwoa, first)
        _acc(dwob_ref, dwob, first)
        for h in range(MLA_HEADS):
            do_ref[:, sl(h)] = do[h]
            _acc(dmog_ref.at[:, sl(h)], dmog[h], first)
            _acc(dwoc_ref.at[sl(h), :], dwoc[h], first)

    return _ride(
        body, riders, name="outproj_bwd", grid=(s // TM,),
        in_specs=[_rows(TM, ya.shape[1]), _rows(TM, yb.shape[1]), _rows(TM, o.shape[1]),
                  _full(mog), _full(woa), _full(wob), _full(woc), _rows(TM, d),
                  pl.BlockSpec((npart, TM, d), lambda i: (0, i, 0))],
        out_specs=[_rows(TM, d), _rows(TM, ya.shape[1]), _rows(TM, yb.shape[1]), _rows(TM, o.shape[1]),
                   _full(mog), _full(woa), _full(wob), _full(woc)],
        out_shape=[_sds((s, d)), _sds(ya.shape), _sds(yb.shape), _sds(o.shape),
                   _sds(mog.shape), _sds(woa.shape), _sds(wob.shape), _sds(woc.shape)],
        operands=(ya, yb, o, mog, woa, wob, woc, dx2, dx1p), sem=("arbitrary",))


def ffn_fwd(x1, g2, w1, w2, riders=()):
    s, d = x1.shape
    npart, _, fs = w1.shape

    def body(x1_ref, g_ref, w1_ref, w2_ref, x2_ref, r_ref):
        p = pl.program_id(1)
        x1v = x1_ref[...]
        r = jnp.maximum(PLAIN.mm(_rms(x1v, g_ref[...]), w1_ref[...]), 0.0)
        r_ref[...] = r.astype(r_ref.dtype)
        part = PLAIN.mm(r * r, w2_ref[...])

        @pl.when(p == 0)
        def _():
            x2_ref[...] = x1v + part

        @pl.when(p != 0)
        def _():
            x2_ref[...] = x2_ref[...] + part

    tm = min(2 * TM_FFN, s)
    return _ride(
        body, riders, name="ffn_fwd", grid=(s // tm, npart),
        in_specs=[pl.BlockSpec((tm, d), lambda i, p: (i, 0)), pl.BlockSpec(g2.shape, lambda i, p: (0, 0)),
                  pl.BlockSpec((None, d, fs), lambda i, p: (p, 0, 0)), pl.BlockSpec((None, fs, d), lambda i, p: (p, 0, 0))],
        out_specs=[pl.BlockSpec((tm, d), lambda i, p: (i, 0)), pl.BlockSpec((tm, fs), lambda i, p: (i, p))],
        out_shape=[_sds((s, d)), _sds((s, npart * fs), MXU_DTYPE)],
        operands=(x1, g2, w1, w2), sem=("parallel", "arbitrary"))


def ffn_bwd(x1, g2, w1, w2, r, dx2, riders=()):
    s, d = x1.shape
    npart, _, fs = w1.shape

    def body(x1_ref, g_ref, w1_ref, w2_ref, r_ref, dx2_ref, dx1p_ref, dg_ref, dw1_ref, dw2_ref):
        p = pl.program_id(0)
        i = pl.program_id(1)
        h2, vjp_norm = jax.vjp(_rms, x1_ref[...], g_ref[...])
        rr = r_ref[...].astype(F32)
        dy = dx2_ref[...]
        da = _dot(dy, w2_ref[...], NT) * (2.0 * rr)
        dx1, dg = vjp_norm(_dot(da, w1_ref[...], NT))
        dx1p_ref[...] = dx1
        _acc(dg_ref, dg, (p == 0) & (i == 0))
        _acc(dw1_ref, _dot(h2, da, TN), i == 0)
        _acc(dw2_ref, _dot(rr * rr, dy, TN), i == 0)

    tm = TM_FFN
    return _ride(
        body, riders, name="ffn_bwd", grid=(npart, s // tm),
        in_specs=[pl.BlockSpec((tm, d), lambda p, i: (i, 0)), pl.BlockSpec(g2.shape, lambda p, i: (0, 0)),
                  pl.BlockSpec((None, d, fs), lambda p, i: (p, 0, 0)), pl.BlockSpec((None, fs, d), lambda p, i: (p, 0, 0)),
                  pl.BlockSpec((tm, fs), lambda p, i: (i, p)), pl.BlockSpec((tm, d), lambda p, i: (i, 0))],
        out_specs=[pl.BlockSpec((None, tm, d), lambda p, i: (p, i, 0)), pl.BlockSpec(g2.shape, lambda p, i: (0, 0)),
                   pl.BlockSpec((None, d, fs), lambda p, i: (p, 0, 0)), pl.BlockSpec((None, fs, d), lambda p, i: (p, 0, 0))],
        out_shape=[_sds((npart, s, d)), _sds(g2.shape), _sds(w1.shape), _sds(w2.shape)],
        operands=(x1, g2, w1, w2, r, dx2), sem=("arbitrary", "arbitrary"))


def loss_head(y, target):
    s, d = y.shape

    def body(y_ref, t_ref, dy_ref, loss_ref):
        err = y_ref[...] - t_ref[...]
        dy_ref[...] = err * (1.0 / d)
        part = jnp.sum(jnp.sum(err * err, axis=-1, keepdims=True), axis=0, keepdims=True) * (0.5 / d)
        _acc(loss_ref, jnp.broadcast_to(part, loss_ref.shape), pl.program_id(0) == 0)

    return pl.pallas_call(
        body, name="loss_head", grid=(s // TM,),
        in_specs=[_rows(TM, d), _rows(TM, d)],
        out_specs=[_rows(TM, d), pl.BlockSpec((1, SLOT), lambda i: (0, 0))],
        out_shape=[_sds((s, d)), _sds((1, SLOT))],
        compiler_params=_cp(("arbitrary",)),
    )(y, target)


def _row_block(r):
    for b in (512, 256, 128, 64, 32, 16, 8):
        if r % b == 0:
            return b
    return r


def sum_cores(arrs, gots, half, me):
    n = len(arrs)

    def body(sp_ref, *refs):
        for i in range(n):
            a_ref, g_ref, wire_ref, own_ref = refs[i], refs[n + i], refs[2 * n + i], refs[3 * n + i]
            tot = a_ref[...] + g_ref[...]
            wire_ref[...] = tot.astype(wire_ref.dtype)

            @pl.when(pl.program_id(0) == sp_ref[1])
            def _(own_ref=own_ref, tot=tot):
                own_ref[...] = tot

    shapes = [a.shape[2:] for a in arrs]
    grid_spec = pltpu.PrefetchScalarGridSpec(
        num_scalar_prefetch=1, grid=(N_CHIPS,),
        in_specs=[pl.BlockSpec((None, None) + sh, lambda p, sp: (p, sp[0], 0, 0)) for sh in shapes]
        + [pl.BlockSpec((None,) + sh, lambda p, sp: (p, 0, 0)) for sh in shapes],
        out_specs=[pl.BlockSpec((None,) + sh, lambda p, sp: (p, 0, 0)) for sh in shapes]
        + [pl.BlockSpec(sh, lambda p, sp: (0, 0)) for sh in shapes])
    outs = pl.pallas_call(body, name="sum_cores", grid_spec=grid_spec,
                          out_shape=[_sds((N_CHIPS,) + sh, BF16) for sh in shapes] + [_sds(sh) for sh in shapes],
                          compiler_params=_cp(("arbitrary",)))(jnp.stack([half, me]).astype(jnp.int32), *arrs, *gots)
    return outs[:n], outs[n:]


SUM_STEPS = 4


def sum_chips(owns, recvs, half):
    n = len(owns)

    def body(sp_ref, *refs):
        del sp_ref
        for i in range(n):
            own_ref, out_ref = refs[4 * i], refs[4 * n + i]
            r0, r1, r2 = (refs[4 * i + 1 + j][...].astype(F32) for j in range(3))
            out_ref[...] = ((own_ref[...] + r0) + r1) + r2

    in_specs, out_specs, operands = [], [], []
    for own, recv in zip(owns, recvs):
        r, c = own.shape
        br = r // SUM_STEPS
        in_specs.append(pl.BlockSpec((br, c), lambda i, sp: (i, 0)))
        in_specs += [pl.BlockSpec((None, br, c), functools.partial(lambda i, sp, j: (j, i, 0), j=j)) for j in range(3)]
        out_specs.append(pl.BlockSpec((None, br, c), lambda i, sp: (sp[0], i, 0)))
        operands += [own, recv, recv, recv]
    grid_spec = pltpu.PrefetchScalarGridSpec(num_scalar_prefetch=1, grid=(SUM_STEPS,), in_specs=in_specs, out_specs=out_specs)
    return pl.pallas_call(body, name="sum_chips", grid_spec=grid_spec, out_shape=[_sds((2,) + o.shape) for o in owns],
                          compiler_params=_cp(("parallel",)))(half.reshape(1).astype(jnp.int32), *operands)


def adamw(w, g, m, v, name, riders=()):
    r, c = w.shape
    br = _row_block(r)
    c1 = 1.0 / (1.0 - ADAM_B1 ** ADAM_STEP)
    c2 = 1.0 / (1.0 - ADAM_B2 ** ADAM_STEP)

    def body(w_ref, g_ref, m_ref, v_ref, d_ref, nm_ref, nv_ref):
        gg = g_ref[...]
        nm = ADAM_B1 * m_ref[...] + (1.0 - ADAM_B1) * gg
        nv = ADAM_B2 * v_ref[...] + (1.0 - ADAM_B2) * (gg * gg)
        d_ref[...] = -ADAM_LR * ((nm * c1) / (jnp.sqrt(nv * c2) + ADAM_EPS) + ADAM_WD * w_ref[...])
        nm_ref[...] = nm
        nv_ref[...] = nv

    return _ride(body, riders, name=name, grid=(r // br,), in_specs=[_rows(br, c)] * 4, out_specs=[_rows(br, c)] * 3,
                 out_shape=[_sds((r, c))] * 3, operands=(w, g, m, v), sem=("parallel",))


def _place():
    x, y, c = lax.axis_index("x"), lax.axis_index("y"), lax.axis_index("c")
    chips = [(1 - x, y), (x, 1 - y), (1 - x, 1 - y)]
    return x, y, c, chips


def _remote(src, dst, send_sem, recv_sem, to):
    return pltpu.make_async_remote_copy(src_ref=src, dst_ref=dst, send_sem=send_sem, recv_sem=recv_sem,
                                        device_id=to, device_id_type=MESH)


def gather_rider(arrs):
    n = len(arrs)
    me_chip = 2 * lax.axis_index("x") + lax.axis_index("y")
    bufs = [lax.dynamic_update_index_in_dim(lax.empty((N_CHIPS,) + a.shape, a.dtype), a, me_chip, 0) for a in arrs]

    def plan(ins, outs, sems):
        send_sems, recv_sems = sems
        x, y, c, chips = _place()
        me = 2 * x + y
        half, other, sibling = pl.ds(2 * c, 2), pl.ds(2 - 2 * c, 2), (x, y, 1 - c)
        cp = lambda i, k, src, dst, to: _remote(src, dst, send_sems.at[i, k], recv_sems.at[i, k], to)
        pairs = [(i, j, cx, cy) for i in range(n) for j, (cx, cy) in enumerate(chips)]
        blk = lambda i, cx, cy, part: outs[i].at[2 * cx + cy, part]
        first = lambda: [cp(i, j, ins[i].at[half], outs[i].at[me, half], (cx, cy, c)) for i, j, cx, cy in pairs]
        landed = lambda: [cp(i, j, blk(i, cx, cy, half), blk(i, cx, cy, half), (cx, cy, c)) for i, j, cx, cy in pairs]
        passed = lambda: [cp(i, 3 + j, blk(i, cx, cy, half), blk(i, cx, cy, half), sibling) for i, j, cx, cy in pairs]
        from_sibling = lambda: [cp(i, 3 + j, blk(i, cx, cy, other), blk(i, cx, cy, other), sibling) for i, j, cx, cy in pairs]
        return first, landed, passed, from_sibling

    def start(ins, outs, sems):
        for cp in plan(ins, outs, sems)[0]():
            cp.start()

    def finish(ins, outs, sems):
        first, landed, passed, from_sibling = plan(ins, outs, sems)
        forwards = passed()
        for a, b in zip(landed(), forwards):
            a.wait_recv()
            b.start()
        for cp in from_sibling():
            cp.wait_recv()
        for cp in first() + forwards:
            cp.wait_send()

    return Rider(list(arrs) + bufs, [_sds((N_CHIPS,) + a.shape, a.dtype) for a in arrs], {n + i: i for i in range(n)},
                 [pltpu.SemaphoreType.DMA((n, 6)), pltpu.SemaphoreType.DMA((n, 6))], start, finish)


class Reducer:
    def __init__(self, arrs):
        self.a = list(arrs)
        self.n = len(self.a)
        self.c = lax.axis_index("c")
        self.me = 2 * lax.axis_index("x") + lax.axis_index("y")

    def swap_rider(self):
        n = self.n

        def plan(ins, outs, sems):
            x, y, c, _ = _place()
            return [_remote(ins[i].at[p, 1 - c], outs[i].at[p], sems[0].at[i, p], sems[1].at[i, p], (x, y, 1 - c))
                    for i in range(n) for p in range(N_CHIPS)]

        return Rider(self.a, [_sds((N_CHIPS,) + a.shape[2:]) for a in self.a], {},
                     [pltpu.SemaphoreType.DMA((n, N_CHIPS)), pltpu.SemaphoreType.DMA((n, N_CHIPS))],
                     lambda *r: [cp.start() for cp in plan(*r)], lambda *r: [cp.wait() for cp in plan(*r)])

    def after_swap(self, got):
        self.wire, self.own = sum_cores(self.a, got, self.c, self.me)

    def scatter_rider(self):
        n = self.n

        def plan(ins, outs, sems):
            x, y, c, chips = _place()
            return [_remote(ins[i].at[2 * cx + cy], outs[i].at[j], sems[0].at[i, j], sems[1].at[i, j], (cx, cy, c))
                    for i in range(n) for j, (cx, cy) in enumerate(chips)]

        return Rider(self.wire, [_sds((3,) + w.shape[1:], w.dtype) for w in self.wire], {},
                     [pltpu.SemaphoreType.DMA((n, 3)), pltpu.SemaphoreType.DMA((n, 3))],
                     lambda *r: [cp.start() for cp in plan(*r)], lambda *r: [cp.wait() for cp in plan(*r)])

    def after_scatter(self, recv):
        self.full = sum_chips(self.own, recv, self.c)

    def share_rider(self):
        n = self.n

        def plan(ins, outs, sems):
            x, y, c, _ = _place()
            return [_remote(ins[i].at[c], outs[i].at[c], sems[0].at[i], sems[1].at[i], (x, y, 1 - c)) for i in range(n)]

        return Rider(self.full, [_sds(f.shape) for f in self.full], {i: i for i in range(n)},
                     [pltpu.SemaphoreType.DMA((n,)), pltpu.SemaphoreType.DMA((n,))],
                     lambda *r: [cp.start() for cp in plan(*r)], lambda *r: [cp.wait() for cp in plan(*r)])

    def run(self):
        self.after_swap(run_rider(self.swap_rider(), "swap_halves"))
        self.after_scatter(run_rider(self.scatter_rider(), "scatter_chips"))
        return run_rider(self.share_rider(), "share_halves")


def _pad_slots(a, live):
    lead = a.shape[:-1]
    a = a.reshape(lead + (MLA_HEADS, live))
    a = jnp.pad(a, [(0, 0)] * len(lead) + [(0, 0), (0, SLOT - live)])
    return a.reshape(lead + (MLA_HEADS * SLOT,))


def _unpad_slots(a, live):
    lead = a.shape[:-1]
    return a.reshape(lead + (MLA_HEADS, SLOT))[..., :live].reshape(lead + (MLA_HEADS * live,))


def _rope_tables(positions, s):
    half = QK_ROPE // 2
    inv_freq = ROPE_THETA ** (-jnp.arange(half, dtype=F32) / half)
    ang = positions.reshape(s).astype(F32)[:, None] * inv_freq[None, :]
    cos, sin = jnp.cos(ang), jnp.sin(ang)
    one = jnp.ones((s, QK_NOPE), F32)
    z64, z16, z32 = jnp.zeros((s, QK_NOPE), F32), jnp.zeros((s, half), F32), jnp.zeros((s, SLOT - QK_DIM), F32)
    cos_t = jnp.concatenate([one, cos, cos, z32], axis=1)
    sin_a = jnp.concatenate([z64, -sin, z16, z32], axis=1)
    sin_b = jnp.concatenate([z64, z16, sin, z32], axis=1)
    return cos_t, sin_a, sin_b


def _out_weights(full):
    w_out = jnp.concatenate([full["w_out"][p] for p in range(N_CHIPS)], axis=0)
    woc = w_out[512:].reshape(MLA_HEADS, V_DIM, D_MODEL)
    woc = jnp.pad(woc, ((0, 0), (0, SLOT - V_DIM), (0, 0))).reshape(MLA_HEADS * SLOT, D_MODEL)
    return dict(woa=w_out[:256], wob=w_out[256:512], woc=woc)


def _layer_weights(full, small, l):
    w_in = jnp.concatenate([full["w_in"][p] for p in range(N_CHIPS)], axis=1)
    wc = jnp.pad(w_in[:, 1536:], ((0, 0), (0, 512 - (w_in.shape[1] - 1536))))
    w_uq = jnp.concatenate([full["mla_w_uq"][p] for p in range(N_CHIPS)], axis=1)
    w_ukv = jnp.concatenate([full["mla_w_ukv"][p] for p in range(N_CHIPS)], axis=1)
    ukv = w_ukv.reshape(KV_LORA, MLA_HEADS, QK_NOPE + V_DIM)
    row = lambda a: a.reshape(1, -1)
    return dict(
        g1=row(small["norm1_gain"][l]), wa=w_in[:, :512], wb=w_in[:, 512:1536], wc=wc,
        vg=row(small["gm_v_gain"][l]), ws=small["gm_w_s"][l], bs=small["gm_b_s"][l].reshape(4, CHUNK, 1),
        gog=row(small["gm_out_gain"][l]), hog=small["hg_out_gain"][l].reshape(-1, 1),
        qag=row(small["mla_q_a_gain"][l]), kvag=row(small["mla_kv_a_gain"][l]),
        qg=row(jnp.pad(small["mla_q_gain"][l], (0, SLOT - QK_DIM))), kg=row(jnp.pad(small["mla_k_gain"][l], (0, SLOT - QK_DIM))),
        wq=_pad_slots(w_uq, QK_DIM), wk=_pad_slots(ukv[..., :QK_NOPE].reshape(KV_LORA, -1), QK_NOPE),
        wv=_pad_slots(ukv[..., QK_NOPE:].reshape(KV_LORA, -1), V_DIM),
        mog=row(_pad_slots(small["mla_out_gain"][l], V_DIM)),
        g2=row(small["norm2_gain"][l]),
    )


def _shard_cols(a):
    r, c4 = a.shape
    return a.reshape(r, N_CHIPS, c4 // N_CHIPS).transpose(1, 0, 2)


def local_step(x, positions, target, small, comm):
    s = x.shape[0]
    cos_t, sin_a, sin_b = _rope_tables(positions, s)
    lbs = lower_bounds_fwd(small["hg_lower_bound"])
    lw, saved = [], []
    for l in range(DEPTH):
        w = _layer_weights(comm.part(l, "in"), small, l)
        lw.append(w)
        lb = lbs[l].reshape(-1, 1)
        pa, pb, pc = inproj_fwd(x, w["g1"], w["wa"], w["wb"], w["wc"])
        ya = gm_fwd(pa, w["vg"], w["ws"], w["bs"], w["gog"])
        (yb, states), got = hg_fwd(pb, lb, w["hog"], [comm.gather_rider(l, "ff1")])
        comm.gathered(l, "ff1", got[0])
        (q, k, v), got = mla_pre_fwd(pc, cos_t, sin_a, sin_b, w["qag"], w["kvag"], w["qg"], w["kg"], w["wq"], w["wk"], w["wv"],
                                     [comm.gather_rider(l, "out")])
        comm.gathered(l, "out", got[0])
        (o, lse), got = attn_fwd(q, k, v, [comm.gather_rider(l, "ff2")])
        comm.gathered(l, "ff2", got[0])
        w.update(_out_weights(comm.part(l, "out")))
        x1 = outproj_fwd(x, ya, yb, o, w["mog"], w["woa"], w["wob"], w["woc"])
        w["w1"], w["w2"] = comm.part(l, "ff1")["w_ff1"], comm.part(l, "ff2")["w_ff2"]
        rider = comm.gather_rider(l + 1, "in") if l + 1 < DEPTH else None
        (x2, r), got = ffn_fwd(x1, w["g2"], w["w1"], w["w2"], [rider])
        comm.gathered(l + 1, "in", got[0])
        saved.append(dict(x=x, pa=pa, pb=pb, pc=pc, ya=ya, yb=yb, states=states, q=q, k=k, v=v, o=o, lse=lse, x1=x1, r=r, lb=lb))
        x = x2
    dx, loss_part = loss_head(x, target)
    groups = [dict() for _ in range(DEPTH)]
    sm = {n: [None] * DEPTH for n in ("norm1_gain", "gm_v_gain", "gm_w_s", "gm_b_s", "gm_out_gain", "hg_out_gain",
                                       "mla_q_a_gain", "mla_kv_a_gain", "mla_q_gain", "mla_k_gain", "mla_out_gain",
                                       "norm2_gain")}
    dlbs = [None] * DEPTH
    halves = lambda g: g.reshape(N_CHIPS, 2, g.shape[1] // 2, g.shape[2])
    take = lambda red, f: None if red is None else f(red)
    red_mix = None
    for l in reversed(range(DEPTH)):
        w, a = lw[l], saved[l]
        (dx1p, dg2, dw1, dw2), got = ffn_bwd(a["x1"], w["g2"], w["w1"], w["w2"], a["r"], dx,
                                             [take(red_mix, Reducer.swap_rider)])
        if red_mix:
            red_mix.after_swap(got[0])
        ffn_arrs = [halves(dw1), halves(dw2)]
        red_f1, red_f2 = comm.reducer(ffn_arrs[:1]), comm.reducer(ffn_arrs[1:])
        (dx1, dya, dyb, do, dmog, dwoa, dwob, dwoc), got = outproj_bwd(
            a["ya"], a["yb"], a["o"], w["mog"], w["woa"], w["wob"], w["woc"], dx, dx1p,
            [take(red_f1, Reducer.swap_rider), take(red_f2, Reducer.swap_rider), take(red_mix, Reducer.scatter_rider)])
        for red, g, step in zip((red_f1, red_f2, red_mix), got, (Reducer.after_swap, Reducer.after_swap, Reducer.after_scatter)):
            if red:
                step(red, g)
        (dq, dk, dv), got = attn_bwd(a["q"], a["k"], a["v"], a["o"], do, a["lse"],
                                     [take(red_f1, Reducer.scatter_rider), take(red_mix, Reducer.share_rider)])
        if red_f1:
            red_f1.after_scatter(got[0])
        if red_mix:
            groups[l + 1].update(zip(MIX, got[1]))
        dpc, dqag, dkvag, dqg, dkg, dwq, dwk, dwv = mla_pre_bwd(a["pc"], cos_t, sin_a, sin_b, w["qag"], w["kvag"], w["qg"],
                                                                  w["kg"], w["wq"], w["wk"], w["wv"], dq, dk, dv)
        (dpb, dlb, dhog), got = hg_bwd(a["pb"], a["lb"], w["hog"], a["states"], dyb,
                                       [take(red_f2, Reducer.scatter_rider), take(red_f1, Reducer.share_rider)])
        if red_f2:
            red_f2.after_scatter(got[0])
        groups[l]["w_ff1"] = got[1][0] if red_f1 else ffn_arrs[0]
        dpa, dvg, dws, dbs, dgog = gm_bwd(a["pa"], w["vg"], w["ws"], w["bs"], w["gog"], dya)
        (dx, dg1, dwa, dwb, dwc), got = inproj_bwd(a["x"], w["g1"], w["wa"], w["wb"], w["wc"], dpa, dpb, dpc, dx1,
                                                   [take(red_f2, Reducer.share_rider)])
        groups[l]["w_ff2"] = got[0][0] if red_f2 else ffn_arrs[1]
        dukv = jnp.concatenate([dwk.reshape(KV_LORA, MLA_HEADS, SLOT)[..., :QK_NOPE],
                                dwv.reshape(KV_LORA, MLA_HEADS, SLOT)[..., :V_DIM]], axis=-1)
        dwo = jnp.concatenate([dwoa, dwob, dwoc.reshape(MLA_HEADS, SLOT, D_MODEL)[:, :V_DIM].reshape(-1, D_MODEL)], axis=0)
        mix_arrs = [halves(_shard_cols(jnp.concatenate([dwa, dwb, dwc[:, :1952 - 1536]], axis=1))),
                    halves(_shard_cols(_unpad_slots(dwq, QK_DIM))), halves(_shard_cols(dukv.reshape(KV_LORA, -1))),
                    halves(dwo.reshape(N_CHIPS, -1, D_MODEL))]
        red_mix = comm.reducer(mix_arrs) if l > 0 else None
        if red_mix is None:
            groups[l].update(zip(MIX, mix_arrs))
        sm["norm1_gain"][l] = dg1[0]
        sm["gm_v_gain"][l] = dvg[0]
        sm["gm_w_s"][l] = dws
        sm["gm_b_s"][l] = dbs[..., 0]
        sm["gm_out_gain"][l] = dgog[0]
        sm["hg_out_gain"][l] = dhog[:, 0]
        sm["mla_q_a_gain"][l] = dqag[0]
        sm["mla_kv_a_gain"][l] = dkvag[0]
        sm["mla_q_gain"][l] = dqg[0, :QK_DIM]
        sm["mla_k_gain"][l] = dkg[0, :QK_DIM]
        sm["mla_out_gain"][l] = _unpad_slots(dmog[0], V_DIM)
        sm["norm2_gain"][l] = dg2[0]
        dlbs[l] = dlb[:, 0]
    sm["hg_lower_bound"] = [lower_bounds_bwd(small["hg_lower_bound"], jnp.stack(dlbs))]
    return loss_part, dx, groups, sm


MIX = ("w_in", "mla_w_uq", "mla_w_ukv", "w_out")
FFN = ("w_ff1", "w_ff2")
BIG = MIX + FFN
PARTS = {"in": ("w_in", "mla_w_uq", "mla_w_ukv"), "out": ("w_out",), "ff1": ("w_ff1",), "ff2": ("w_ff2",)}
SMALL = ("norm1_gain", "gm_v_gain", "gm_w_s", "gm_b_s", "gm_out_gain", "hg_lower_bound", "hg_out_gain",
         "mla_q_a_gain", "mla_kv_a_gain", "mla_q_gain", "mla_k_gain", "mla_out_gain", "norm2_gain")
ORDER = ("norm1_gain", "w_in", "gm_v_gain", "gm_w_s", "gm_b_s", "gm_out_gain", "hg_lower_bound", "hg_out_gain",
         "mla_q_a_gain", "mla_w_uq", "mla_kv_a_gain", "mla_w_ukv", "mla_q_gain", "mla_k_gain", "mla_out_gain",
         "w_out", "norm2_gain", "w_ff1", "w_ff2")
PACK_ROWS = 320


def _pack(pieces):
    flat = jnp.concatenate([a.reshape(-1) for a in pieces])
    total = 2 * N_CHIPS * PACK_ROWS * SLOT
    return jnp.pad(flat, (0, total - flat.shape[0]))


def _unpack(flat, shapes):
    out, off = [], 0
    for sh in shapes:
        size = 1
        for d in sh:
            size *= d
        out.append(flat[off:off + size].reshape(sh))
        off += size
    return out


class ChipComm:
    def __init__(self, shards):
        self.shards = shards
        self.full = {}

    def gather_rider(self, l, part):
        return gather_rider([self.shards[n][l].astype(MXU_DTYPE).reshape(4, self.shards[n].shape[1] // 4, -1)
                             for n in PARTS[part]])

    def gathered(self, l, part, outs):
        if outs is not None:
            self.full[l, part] = {n: o.reshape((N_CHIPS,) + self.shards[n].shape[1:]) for n, o in zip(PARTS[part], outs)}

    def part(self, l, part):
        if (l, part) not in self.full:
            self.gathered(l, part, run_rider(self.gather_rider(l, part), "gather_weights"))
        return self.full[l, part]

    def reducer(self, arrs):
        return Reducer(arrs)


def kernel(x, positions, norm1_gain, w_in, gm_v_gain, gm_w_s, gm_b_s, gm_out_gain, hg_lower_bound, hg_out_gain, mla_q_a_gain, mla_w_uq, mla_kv_a_gain, mla_w_ukv, mla_q_gain, mla_k_gain, mla_out_gain, w_out, norm2_gain, w_ff1, w_ff2, loss_target, m_norm1_gain, m_w_in, m_gm_v_gain, m_gm_w_s, m_gm_b_s, m_gm_out_gain, m_hg_lower_bound, m_hg_out_gain, m_mla_q_a_gain, m_mla_w_uq, m_mla_kv_a_gain, m_mla_w_ukv, m_mla_q_gain, m_mla_k_gain, m_mla_out_gain, m_w_out, m_norm2_gain, m_w_ff1, m_w_ff2, v_norm1_gain, v_w_in, v_gm_v_gain, v_gm_w_s, v_gm_b_s, v_gm_out_gain, v_hg_lower_bound, v_hg_out_gain, v_mla_q_a_gain, v_mla_w_uq, v_mla_kv_a_gain, v_mla_w_ukv, v_mla_q_gain, v_mla_k_gain, v_mla_out_gain, v_w_out, v_norm2_gain, v_w_ff1, v_w_ff2):
    given = dict(locals())
    weights = {n: given[n] for n in ORDER}
    moms = {n: given["m_" + n] for n in ORDER}
    vars_ = {n: given["v_" + n] for n in ORDER}
    s, d = x.shape[1], x.shape[2]

    small = {n: weights[n] for n in SMALL}
    comm = ChipComm({n: weights[n] for n in BIG})
    loss_part, dx, groups, small_g = local_step(x.reshape(s, d), positions, loss_target.reshape(s, d), small, comm)
    loss = lax.psum(loss_part[0, 0], ("x", "y", "c"))

    pack_g = _pack([jnp.stack(small_g[n]) for n in SMALL]).reshape(N_CHIPS, 2, PACK_ROWS, SLOT)
    reduced = Reducer([groups[0][n] for n in MIX] + [pack_g]).run()
    groups[0].update(zip(MIX, reduced[:-1]))
    pack_full = run_rider(gather_rider([reduced[-1].reshape(4, PACK_ROWS // 2, SLOT)]), "gather_small")[0].reshape(-1)
    grads = {n: jnp.stack([groups[l][n].reshape(weights[n].shape[1:]) for l in range(DEPTH)]) for n in BIG}
    grads.update(zip(SMALL, _unpack(pack_full, [weights[n].shape for n in SMALL])))

    delta, new_m, new_v = {}, {}, {}
    flat2 = lambda a: a.reshape(-1, a.shape[-1])
    for n in ORDER:
        outs, _ = adamw(flat2(weights[n]), flat2(grads[n]), flat2(moms[n]), flat2(vars_[n]), "adamw_" + n)
        delta[n], new_m[n], new_v[n] = [o.reshape(weights[n].shape) for o in outs]

    return (loss, dx.reshape(x.shape), *[grads[n] for n in ORDER], *[delta[n] for n in ORDER],
            *[new_m[n] for n in ORDER], *[new_v[n] for n in ORDER])
```

```python
import functools

import jax
import jax.numpy as jnp
from jax import lax
from jax.experimental import pallas as pl
from jax.experimental.pallas import tpu as pltpu

F32 = jnp.float32
BF16 = jnp.bfloat16
MXU_DTYPE = BF16

D_MODEL = 1024
DEPTH = 4
CHUNK = 128
HG_CHUNK = 128
HG_CHUNKS = 2
GM_CHUNKS = 4
EPS = 1e-6
HEAD64 = 64
MLA_HEADS = 8
QK_NOPE = 64
QK_ROPE = 32
QK_DIM = 96
V_DIM = 64
Q_LORA = 256
KV_LORA = 128
SLOT = 128
ROPE_THETA = 10000.0
D_FF_SHARD = 1024
N_CHIPS = 4

ADAM_LR = 0.001
ADAM_B1 = 0.9
ADAM_B2 = 0.999
ADAM_EPS = 1e-08
ADAM_WD = 0.01
ADAM_STEP = 10

TM = 512
TM_FFN = 512
TQ = 256
ATT_HEADS_PER_STEP = 4
ATT_WIDE = 512
VMEM_LIMIT = 56 * 1024 * 1024

NN = (((1,), (0,)), ((), ()))
NT = (((1,), (1,)), ((), ()))
TN = (((0,), (0,)), ((), ()))
BNN = (((2,), (1,)), ((0,), (0,)))
BNT = (((2,), (2,)), ((0,), (0,)))
BTN = (((1,), (1,)), ((0,), (0,)))


def _dot(a, b, dims):
    return lax.dot_general(a.astype(MXU_DTYPE), b.astype(MXU_DTYPE), dims, preferred_element_type=F32)


def _hdot(a, b, dims=NN):
    return lax.dot_general(a, b, dims, precision=lax.Precision.HIGHEST, preferred_element_type=F32)


def _make_ad(dims, da_dims, da_swap, db_dims, db_swap):
    @jax.custom_vjp
    def f(a, b):
        return _dot(a, b, dims)

    def fwd(a, b):
        return _dot(a, b, dims), (a, b)

    def bwd(res, g):
        a, b = res
        da = _dot(b, g, da_dims) if da_swap else _dot(g, b, da_dims)
        db = _dot(g, a, db_dims) if db_swap else _dot(a, g, db_dims)
        return da, db

    f.defvjp(fwd, bwd)
    return f


@functools.partial(jax.custom_vjp, nondiff_argnums=(1,))
def _roll_ad(x, shift):
    return pltpu.roll(x, shift, 1)


def _roll_ad_fwd(x, shift):
    return pltpu.roll(x, shift, 1), None


def _roll_ad_bwd(shift, _, g):
    return (pltpu.roll(g, (g.shape[1] - shift) % g.shape[1], 1),)


_roll_ad.defvjp(_roll_ad_fwd, _roll_ad_bwd)


class _Ops:
    pass


PLAIN = _Ops()
PLAIN.mm = lambda a, b: _dot(a, b, NN)
PLAIN.bmm = lambda a, b: _dot(a, b, BNN)
PLAIN.bmm_nt = lambda a, b: _dot(a, b, BNT)
PLAIN.bmm_tn = lambda a, b: _dot(a, b, BTN)
PLAIN.roll = lambda x, s: pltpu.roll(x, s, 1)

AD = _Ops()
AD.mm = _make_ad(NN, NT, False, TN, False)
AD.bmm = _make_ad(BNN, BNT, False, BTN, False)
AD.bmm_nt = _make_ad(BNT, BNN, False, BTN, True)
AD.bmm_tn = _make_ad(BTN, BNT, True, BNN, False)
AD.roll = _roll_ad


def _sigmoid(x):
    return jax.nn.sigmoid(x)


def _gelu(x):
    return 0.5 * x * (1.0 + jnp.tanh(0.7978845608028654 * (x + 0.044715 * (x * x * x))))


def _rms(x, g):
    return x * lax.rsqrt(jnp.mean(x * x, axis=-1, keepdims=True) + EPS) * g


def _head_masks256():
    lane = lax.broadcasted_iota(jnp.int32, (1, 4 * HEAD64), 1)
    return [(jnp.right_shift(lane, 6) == h).astype(F32) for h in range(4)]


def _headnorm256(x, g):
    ms = jnp.zeros_like(x)
    sq = x * x
    for m in _head_masks256():
        ms = ms + m * (jnp.sum(sq * m, axis=-1, keepdims=True) * (1.0 / HEAD64))
    return x * lax.rsqrt(ms + EPS) * g


def _slot_norm(x, g, n):
    return x * lax.rsqrt(jnp.sum(x * x, axis=-1, keepdims=True) * (1.0 / n) + EPS) * g


def _rope(ops, x, cos_t, sin_a, sin_b):
    return x * cos_t + ops.roll(x, SLOT - QK_ROPE // 2) * sin_a + ops.roll(x, QK_ROPE // 2) * sin_b


def _inproj(ops, x, g1, wa, wb, wc):
    h = _rms(x, g1)
    return ops.mm(h, wa), ops.mm(h, wb), ops.mm(h, wc)


def _gm_chunk(ops, ur, vr, vg, ws4, bs, og):
    c = ur.shape[0]
    masks = _head_masks256()
    mh = jnp.concatenate([m[None] for m in masks], axis=0)
    u = _gelu(ur)
    v = _headnorm256(_gelu(vr), vg)
    t = lax.broadcasted_iota(jnp.int32, (c, c), 0)
    s = lax.broadcasted_iota(jnp.int32, (c, c), 1)
    w = jnp.where((t >= s)[None], ws4, 0.0)
    y = jnp.sum(ops.bmm(w, v[None] * mh), axis=0)
    for h in range(4):
        y = y + bs[h] * masks[h]
    return _headnorm256(u * y, og)


def _hg_chunk(ops, st, qr, fr, ir, gr, lb, og):
    c, n = qr.shape
    nh = n // HEAD64
    heads = lambda x: x.reshape(nh, HEAD64, x.shape[-1])
    tr = lambda x: heads(x.T)
    lb4, og4 = heads(lb), heads(og)
    qx = tr(qr)
    q = qx * _sigmoid(qx)
    f = lb4 + (1.0 - lb4) * _sigmoid(tr(fr))
    k = 1.0 - f
    logf = jnp.log(f)
    v = tr(ir)
    gx = tr(gr)
    s = lax.broadcasted_iota(jnp.int32, (c, c), 0)
    t = lax.broadcasted_iota(jnp.int32, (c, c), 1)
    tl = lax.broadcasted_iota(jnp.int32, (1, c), 1).reshape(1, 1, c)
    b2 = _hdot(logf.reshape(n, c), (s <= t).astype(F32))
    b = heads(b2)
    btot = jnp.sum(logf, axis=2, keepdims=True)
    inter = ops.bmm_tn(st, q * jnp.exp(b))
    p4 = jnp.zeros((nh, c, c), F32)
    tt, ss = s, t
    lg = c.bit_length() - 2
    while lg >= 0:
        m = 1 << lg
        bnd = jnp.left_shift(jnp.right_shift(t, lg + 1), lg + 1) + (m - 1)
        r = heads(_hdot(b2, (s == bnd).astype(F32)))
        right = jnp.bitwise_and(jnp.right_shift(tl, lg), 1) == 1
        qe = jnp.where(right, q * jnp.exp(jnp.where(right, b - r, 0.0)), 0.0)
        ke = jnp.where(right, 0.0, k * jnp.exp(jnp.where(right, 0.0, r - b)))
        lm = ((jnp.right_shift(tt, lg + 1) == jnp.right_shift(ss, lg + 1))
              & (jnp.bitwise_and(jnp.right_shift(tt, lg), 1) == 1)
              & (jnp.bitwise_and(jnp.right_shift(ss, lg), 1) == 0))
        p4 = jnp.where(lm[None], ops.bmm_tn(qe, ke), p4)
        lg -= 1
    intra = ops.bmm_nt(v, p4)
    o = inter + intra + jnp.sum(q * k, axis=1, keepdims=True) * v
    st_new = st * jnp.exp(btot) + ops.bmm_nt(k * jnp.exp(btot - b), v)
    y = o * lax.rsqrt(jnp.mean(o * o, axis=1, keepdims=True) + EPS) * og4 * (gx * _sigmoid(gx))
    return st_new, y.reshape(n, c).T


def _mla_pre(ops, cq, ckv, kpe, cos_t, sin_a, sin_b, qag, kvag, qg, kg, wq, wk, wv):
    cqn = _rms(cq, qag)
    ckvn = _rms(ckv, kvag)
    kper = ops.roll(kpe, QK_NOPE)
    qs, ks, vs = [], [], []
    for h in range(MLA_HEADS):
        qh = _slot_norm(ops.mm(cqn, wq[h]), qg, QK_DIM)
        qs.append(_rope(ops, qh, cos_t, sin_a, sin_b))
        kh = _slot_norm(ops.mm(ckvn, wk[h]) + kper, kg, QK_DIM)
        ks.append(_rope(ops, kh, cos_t, sin_a, sin_b))
        vs.append(ops.mm(ckvn, wv[h]))
    return qs, ks, vs


def _outproj(ops, x, ya, yb, o, mog, woa, wob, woc):
    yc = jnp.concatenate([_slot_norm(o[h], mog[h], V_DIM) for h in range(MLA_HEADS)], axis=1)
    return x + ops.mm(ya, woa) + ops.mm(yb, wob) + ops.mm(yc, woc)


def _lower_bounds(r0, r1, r2, r3):
    mx = jnp.maximum(jnp.maximum(r0, r1), jnp.maximum(r2, r3))
    e0, e1, e2, e3 = jnp.exp(r0 - mx), jnp.exp(r1 - mx), jnp.exp(r2 - mx), jnp.exp(r3 - mx)
    inv = 1.0 / (e0 + e1 + e2 + e3)
    s1, s2, s3 = e1 * inv, e2 * inv, e3 * inv
    return jnp.zeros_like(r0), s1, s1 + s2, s1 + s2 + s3


def _cp(sem):
    return pltpu.CompilerParams(dimension_semantics=sem, vmem_limit_bytes=VMEM_LIMIT)


def _rows(tm, n):
    return pl.BlockSpec((tm, n), lambda i: (i, 0))


def _full(a):
    nd = len(a.shape)
    return pl.BlockSpec(a.shape, lambda *_: (0,) * nd, pipeline_mode=pl.Buffered(1))


def _sds(shape, dtype=F32):
    return jax.ShapeDtypeStruct(shape, dtype)


def _acc(ref, val, first):
    @pl.when(first)
    def _():
        ref[...] = val

    @pl.when(jnp.logical_not(first))
    def _():
        ref[...] = ref[...] + val


def _f32(ref):
    return ref[...].astype(F32)


MESH = pl.DeviceIdType.MESH
ANY = pl.BlockSpec(memory_space=pl.ANY)


class Rider:
    def __init__(self, arrays, out_shapes, aliases, sems, start, finish):
        self.arrays, self.out_shapes, self.aliases, self.sems = list(arrays), list(out_shapes), dict(aliases), list(sems)
        self.start, self.finish = start, finish


def run_rider(rider, name):
    n_in, n_out = len(rider.arrays), len(rider.out_shapes)

    def body(*refs):
        ins, outs, sems = refs[:n_in], refs[n_in:n_in + n_out], refs[n_in + n_out:]
        rider.start(ins, outs, sems)
        rider.finish(ins, outs, sems)

    return pl.pallas_call(
        body, name=name, in_specs=[ANY] * n_in, out_specs=[ANY] * n_out, out_shape=rider.out_shapes,
        input_output_aliases=rider.aliases, scratch_shapes=rider.sems,
    )(*rider.arrays)


def _merge_riders(riders):
    bounds, a0, o0, s0 = [], 0, 0, 0
    for r in riders:
        bounds.append((a0, o0, s0))
        a0, o0, s0 = a0 + len(r.arrays), o0 + len(r.out_shapes), s0 + len(r.sems)

    def part(k, ins, outs, sems):
        a, o, s = bounds[k]
        r = riders[k]
        return ins[a:a + len(r.arrays)], outs[o:o + len(r.out_shapes)], sems[s:s + len(r.sems)]

    return Rider(
        [x for r in riders for x in r.arrays], [x for r in riders for x in r.out_shapes],
        {bounds[k][0] + i: bounds[k][1] + o for k, r in enumerate(riders) for i, o in r.aliases.items()},
        [x for r in riders for x in r.sems],
        lambda *refs: [r.start(*part(k, *refs)) for k, r in enumerate(riders)],
        lambda *refs: [r.finish(*part(k, *refs)) for k, r in enumerate(riders)])


def _ride(compute, riders, *, name, grid, in_specs, out_specs, out_shape, operands, scratch_shapes=(), sem=None):
    single = not isinstance(out_shape, (list, tuple))
    if single:
        out_specs, out_shape = [out_specs], [out_shape]
    live = [r for r in riders if r is not None]
    if not live:
        res = pl.pallas_call(compute, name=name, grid=grid, in_specs=in_specs, out_specs=out_specs, out_shape=out_shape,
                             scratch_shapes=list(scratch_shapes), compiler_params=_cp(sem))(*operands)
        return (res[0] if single else res), [None] * len(riders)
    rider = live[0] if len(live) == 1 else _merge_riders(live)
    n_in, n_out, n_s = len(in_specs), len(out_specs), len(scratch_shapes)
    r_in, r_out = len(rider.arrays), len(rider.out_shapes)

    def body(*refs):
        ins, rins = refs[:n_in], refs[n_in:n_in + r_in]
        outs = refs[n_in + r_in:n_in + r_in + n_out]
        routs = refs[n_in + r_in + n_out:n_in + r_in + n_out + r_out]
        scr = refs[n_in + r_in + n_out + r_out:n_in + r_in + n_out + r_out + n_s]
        rsems = refs[n_in + r_in + n_out + r_out + n_s:]
        first = functools.reduce(jnp.logical_and, [pl.program_id(a) == 0 for a in range(len(grid))])
        last = functools.reduce(jnp.logical_and, [pl.program_id(a) == grid[a] - 1 for a in range(len(grid))])

        @pl.when(first)
        def _():
            rider.start(rins, routs, rsems)

        compute(*ins, *outs, *scr)

        @pl.when(last)
        def _():
            rider.finish(rins, routs, rsems)

    res = pl.pallas_call(
        body, name=name, grid=grid, in_specs=list(in_specs) + [ANY] * r_in, out_specs=list(out_specs) + [ANY] * r_out,
        out_shape=list(out_shape) + rider.out_shapes,
        input_output_aliases={n_in + k: n_out + v for k, v in rider.aliases.items()},
        scratch_shapes=list(scratch_shapes) + rider.sems, compiler_params=_cp(("arbitrary",) * len(grid)),
    )(*operands, *rider.arrays)
    main, rest, per_rider = res[:n_out], list(res[n_out:]), []
    for r in riders:
        per_rider.append(None if r is None else [rest.pop(0) for _ in r.out_shapes])
    return (main[0] if single else main), per_rider


def inproj_fwd(x, g1, wa, wb, wc):
    s, d = x.shape

    def body(x_ref, g_ref, wa_ref, wb_ref, wc_ref, pa_ref, pb_ref, pc_ref):
        pa, pb, pc = _inproj(PLAIN, x_ref[...], g_ref[...], wa_ref[...], wb_ref[...], wc_ref[...])
        pa_ref[...] = pa
        pb_ref[...] = pb
        pc_ref[...] = pc

    return pl.pallas_call(
        body, name="inproj_fwd", grid=(s // TM,),
        in_specs=[_rows(TM, d), _full(g1), _full(wa), _full(wb), _full(wc)],
        out_specs=[_rows(TM, wa.shape[1]), _rows(TM, wb.shape[1]), _rows(TM, wc.shape[1])],
        out_shape=[_sds((s, wa.shape[1])), _sds((s, wb.shape[1])), _sds((s, wc.shape[1]))],
        compiler_params=_cp(("parallel",)),
    )(x, g1, wa, wb, wc)


def inproj_bwd(x, g1, wa, wb, wc, dpa, dpb, dpc, dres):
    s, d = x.shape

    def body(x_ref, g_ref, wa_ref, wb_ref, wc_ref, dpa_ref, dpb_ref, dpc_ref, dres_ref,
             dx_ref, dg_ref, dwa_ref, dwb_ref, dwc_ref):
        first = pl.program_id(0) == 0
        _, vjp = jax.vjp(functools.partial(_inproj, AD), x_ref[...], g_ref[...],
                         _f32(wa_ref), _f32(wb_ref), _f32(wc_ref))
        dx, dg, dwa, dwb, dwc = vjp((dpa_ref[...], dpb_ref[...], dpc_ref[...]))
        dx_ref[...] = dx + dres_ref[...]
        _acc(dg_ref, dg, first)
        _acc(dwa_ref, dwa, first)
        _acc(dwb_ref, dwb, first)
        _acc(dwc_ref, dwc, first)

    return pl.pallas_call(
        body, name="inproj_bwd", grid=(s // TM,),
        in_specs=[_rows(TM, d), _full(g1), _full(wa), _full(wb), _full(wc),
                  _rows(TM, wa.shape[1]), _rows(TM, wb.shape[1]), _rows(TM, wc.shape[1]), _rows(TM, d)],
        out_specs=[_rows(TM, d), _full(g1), _full(wa), _full(wb), _full(wc)],
        out_shape=[_sds((s, d)), _sds(g1.shape), _sds(wa.shape), _sds(wb.shape), _sds(wc.shape)],
        compiler_params=_cp(("arbitrary",)),
    )(x, g1, wa, wb, wc, dpa, dpb, dpc, dres)


def gm_fwd(pa, vg, ws4, bs, og):
    s = pa.shape[0]
    w = pa.shape[1] // 2

    def body(pa_ref, vg_ref, ws_ref, bs_ref, og_ref, ya_ref):
        bsl = [bs_ref[h] for h in range(4)]
        for j in range(GM_CHUNKS):
            rows = slice(j * CHUNK, (j + 1) * CHUNK)
            ya_ref[rows, :] = _gm_chunk(PLAIN, pa_ref[rows, 0:w], pa_ref[rows, w:2 * w], vg_ref[...], ws_ref[...], bsl,
                                        og_ref[...])

    tm = GM_CHUNKS * CHUNK
    return pl.pallas_call(
        body, name="gm_fwd", grid=(s // tm,),
        in_specs=[_rows(tm, 2 * w), _full(vg), _full(ws4), _full(bs), _full(og)],
        out_specs=_rows(tm, w), out_shape=_sds((s, w)),
        compiler_params=_cp(("parallel",)),
    )(pa, vg, ws4, bs, og)


def gm_bwd(pa, vg, ws4, bs, og, dya):
    s = pa.shape[0]
    w = pa.shape[1] // 2

    def body(pa_ref, vg_ref, ws_ref, bs_ref, og_ref, dya_ref, dpa_ref, dvg_ref, dws_ref, dbs_ref, dog_ref):
        first = pl.program_id(0) == 0
        bsl = [bs_ref[h] for h in range(4)]
        tot = None
        for j in range(GM_CHUNKS):
            rows = slice(j * CHUNK, (j + 1) * CHUNK)
            _, vjp = jax.vjp(functools.partial(_gm_chunk, AD), pa_ref[rows, 0:w], pa_ref[rows, w:2 * w],
                             vg_ref[...], ws_ref[...], bsl, og_ref[...])
            du, dv, *dws = vjp(dya_ref[rows, :])
            dpa_ref[rows, 0:w] = du
            dpa_ref[rows, w:2 * w] = dv
            tot = dws if tot is None else jax.tree.map(jnp.add, tot, dws)
        dvg, dws, dbs, dog = tot
        _acc(dvg_ref, dvg, first)
        _acc(dws_ref, dws, first)
        _acc(dog_ref, dog, first)
        for h in range(4):
            _acc(dbs_ref.at[h], dbs[h], first)

    tm = GM_CHUNKS * CHUNK
    return pl.pallas_call(
        body, name="gm_bwd", grid=(s // tm,),
        in_specs=[_rows(tm, 2 * w), _full(vg), _full(ws4), _full(bs), _full(og), _rows(tm, w)],
        out_specs=[_rows(tm, 2 * w), _full(vg), _full(ws4), _full(bs), _full(og)],
        out_shape=[_sds((s, 2 * w)), _sds(vg.shape), _sds(ws4.shape), _sds(bs.shape), _sds(og.shape)],
        compiler_params=_cp(("arbitrary",)),
    )(pa, vg, ws4, bs, og, dya)


def hg_fwd(pb, lb, og, riders=()):
    s = pb.shape[0]
    w = pb.shape[1] // 4
    tm = HG_CHUNKS * HG_CHUNK
    st_shape = (w // HEAD64, HEAD64, HEAD64)

    def body(pb_ref, lb_ref, og_ref, yb_ref, states_ref, st_ref):
        @pl.when(pl.program_id(0) == 0)
        def _():
            st_ref[...] = jnp.zeros_like(st_ref)

        st = st_ref[...]
        for j in range(HG_CHUNKS):
            rows = slice(j * HG_CHUNK, (j + 1) * HG_CHUNK)
            states_ref[j] = st
            st, y = _hg_chunk(PLAIN, st, pb_ref[rows, 0:w], pb_ref[rows, w:2 * w], pb_ref[rows, 2 * w:3 * w],
                              pb_ref[rows, 3 * w:4 * w], lb_ref[...], og_ref[...])
            yb_ref[rows, :] = y
        st_ref[...] = st

    return _ride(
        body, riders, name="hg_fwd", grid=(s // tm,),
        in_specs=[_rows(tm, 4 * w), _full(lb), _full(og)],
        out_specs=[_rows(tm, w), pl.BlockSpec((HG_CHUNKS,) + st_shape, lambda i: (i, 0, 0, 0))],
        out_shape=[_sds((s, w)), _sds((s // HG_CHUNK,) + st_shape)],
        scratch_shapes=[pltpu.VMEM(st_shape, F32)],
        operands=(pb, lb, og), sem=("arbitrary",))


def hg_bwd(pb, lb, og, states, dyb, riders=()):
    s = pb.shape[0]
    w = pb.shape[1] // 4
    tm = HG_CHUNKS * HG_CHUNK
    nc = s // tm
    st_shape = (w // HEAD64, HEAD64, HEAD64)

    def body(pb_ref, lb_ref, og_ref, states_ref, dyb_ref, dpb_ref, dlb_ref, dog_ref, dst_ref):
        first = pl.program_id(0) == 0

        @pl.when(first)
        def _():
            dst_ref[...] = jnp.zeros_like(dst_ref)

        dst, dlb, dog = dst_ref[...], None, None
        for j in reversed(range(HG_CHUNKS)):
            rows = slice(j * HG_CHUNK, (j + 1) * HG_CHUNK)
            _, vjp = jax.vjp(functools.partial(_hg_chunk, AD), states_ref[j], pb_ref[rows, 0:w], pb_ref[rows, w:2 * w],
                             pb_ref[rows, 2 * w:3 * w], pb_ref[rows, 3 * w:4 * w], lb_ref[...], og_ref[...])
            dst, dq, df, di, dg, dlb_j, dog_j = vjp((dst, dyb_ref[rows, :]))
            dpb_ref[rows, 0:w] = dq
            dpb_ref[rows, w:2 * w] = df
            dpb_ref[rows, 2 * w:3 * w] = di
            dpb_ref[rows, 3 * w:4 * w] = dg
            dlb = dlb_j if dlb is None else dlb + dlb_j
            dog = dog_j if dog is None else dog + dog_j
        dst_ref[...] = dst
        _acc(dlb_ref, dlb, first)
        _acc(dog_ref, dog, first)

    rev = lambda i: (nc - 1 - i, 0)
    return _ride(
        body, riders, name="hg_bwd", grid=(nc,),
        in_specs=[pl.BlockSpec((tm, 4 * w), rev), _full(lb), _full(og),
                  pl.BlockSpec((HG_CHUNKS,) + st_shape, lambda i: (nc - 1 - i, 0, 0, 0)), pl.BlockSpec((tm, w), rev)],
        out_specs=[pl.BlockSpec((tm, 4 * w), rev), _full(lb), _full(og)],
        out_shape=[_sds((s, 4 * w)), _sds(lb.shape), _sds(og.shape)],
        scratch_shapes=[pltpu.VMEM(st_shape, F32)],
        operands=(pb, lb, og, states, dyb), sem=("arbitrary",))


def lower_bounds_fwd(hlb):
    def body(h_ref, o_ref):
        outs = _lower_bounds(*[h_ref[pl.ds(i, 1), :] for i in range(DEPTH)])
        for i in range(DEPTH):
            o_ref[pl.ds(i, 1), :] = outs[i]

    return pl.pallas_call(body, name="lower_bounds_fwd", out_shape=_sds(hlb.shape))(hlb)


def lower_bounds_bwd(hlb, dlbs):
    def body(h_ref, d_ref, o_ref):
        _, vjp = jax.vjp(_lower_bounds, *[h_ref[pl.ds(i, 1), :] for i in range(DEPTH)])
        outs = vjp(tuple(d_ref[pl.ds(i, 1), :] for i in range(DEPTH)))
        for i in range(DEPTH):
            o_ref[pl.ds(i, 1), :] = outs[i]

    return pl.pallas_call(body, name="lower_bounds_bwd", out_shape=_sds(hlb.shape))(hlb, dlbs)


def _mla_pre_args(pc_ref, cos_ref, sa_ref, sb_ref, qag_ref, kvag_ref, qg_ref, kg_ref, wq_ref, wk_ref, wv_ref, cast):
    sl = lambda h: slice(h * SLOT, (h + 1) * SLOT)
    ld = (lambda r, h: r[:, sl(h)].astype(F32)) if cast else (lambda r, h: r[:, sl(h)])
    diff = (pc_ref[:, 0:Q_LORA], pc_ref[:, Q_LORA:Q_LORA + KV_LORA], pc_ref[:, Q_LORA + KV_LORA:Q_LORA + 2 * KV_LORA],
            qag_ref[...], kvag_ref[...], qg_ref[...], kg_ref[...],
            [ld(wq_ref, h) for h in range(MLA_HEADS)], [ld(wk_ref, h) for h in range(MLA_HEADS)],
            [ld(wv_ref, h) for h in range(MLA_HEADS)])
    tables = (cos_ref[...], sa_ref[...], sb_ref[...])
    return diff, tables


def _mla_pre_fn(ops, tables, cq, ckv, kpe, qag, kvag, qg, kg, wq, wk, wv):
    return _mla_pre(ops, cq, ckv, kpe, *tables, qag, kvag, qg, kg, wq, wk, wv)


def mla_pre_fwd(pc, cos_t, sin_a, sin_b, qag, kvag, qg, kg, wq, wk, wv):
    s = pc.shape[0]
    hw = MLA_HEADS * SLOT

    def body(pc_ref, cos_ref, sa_ref, sb_ref, qag_ref, kvag_ref, qg_ref, kg_ref, wq_ref, wk_ref, wv_ref,
             q_ref, k_ref, v_ref):
        diff, tables = _mla_pre_args(pc_ref, cos_ref, sa_ref, sb_ref, qag_ref, kvag_ref, qg_ref, kg_ref,
                                     wq_ref, wk_ref, wv_ref, False)
        qs, ks, vs = _mla_pre_fn(PLAIN, tables, *diff)
        ones_lane = (lax.broadcasted_iota(jnp.int32, (1, SLOT), 1) == V_DIM).astype(F32)
        for h in range(MLA_HEADS):
            q_ref[:, h * SLOT:(h + 1) * SLOT] = qs[h].astype(q_ref.dtype)
            k_ref[:, h * SLOT:(h + 1) * SLOT] = ks[h].astype(k_ref.dtype)
            v_ref[:, h * SLOT:(h + 1) * SLOT] = (vs[h] + ones_lane).astype(v_ref.dtype)

    return pl.pallas_call(
        body, name="mla_pre_fwd", grid=(s // TM,),
        in_specs=[_rows(TM, pc.shape[1]), _rows(TM, SLOT), _rows(TM, SLOT), _rows(TM, SLOT),
                  _full(qag), _full(kvag), _full(qg), _full(kg), _full(wq), _full(wk), _full(wv)],
        out_specs=[_rows(TM, hw)] * 3, out_shape=[_sds((s, hw), MXU_DTYPE)] * 3,
        compiler_params=_cp(("parallel",)),
    )(pc, cos_t, sin_a, sin_b, qag, kvag, qg, kg, wq, wk, wv)


def mla_pre_bwd(pc, cos_t, sin_a, sin_b, qag, kvag, qg, kg, wq, wk, wv, dq, dk, dv):
    s = pc.shape[0]
    hw = MLA_HEADS * SLOT

    def body(pc_ref, cos_ref, sa_ref, sb_ref, qag_ref, kvag_ref, qg_ref, kg_ref, wq_ref, wk_ref, wv_ref,
             dq_ref, dk_ref, dv_ref, dpc_ref, dqag_ref, dkvag_ref, dqg_ref, dkg_ref, dwq_ref, dwk_ref, dwv_ref):
        first = pl.program_id(0) == 0
        diff, tables = _mla_pre_args(pc_ref, cos_ref, sa_ref, sb_ref, qag_ref, kvag_ref, qg_ref, kg_ref,
                                     wq_ref, wk_ref, wv_ref, True)
        _, vjp = jax.vjp(functools.partial(_mla_pre_fn, AD, tables), *diff)
        sl = lambda h: slice(h * SLOT, (h + 1) * SLOT)
        cot = ([dq_ref[:, sl(h)] for h in range(MLA_HEADS)], [dk_ref[:, sl(h)] for h in range(MLA_HEADS)],
               [dv_ref[:, sl(h)] for h in range(MLA_HEADS)])
        dcq, dckv, dkpe, dqag, dkvag, dqg, dkg, dwq, dwk, dwv = vjp(cot)
        dpc_ref[:, 0:Q_LORA] = dcq
        dpc_ref[:, Q_LORA:Q_LORA + KV_LORA] = dckv
        dpc_ref[:, Q_LORA + KV_LORA:Q_LORA + 2 * KV_LORA] = dkpe
        _acc(dqag_ref, dqag, first)
        _acc(dkvag_ref, dkvag, first)
        _acc(dqg_ref, dqg, first)
        _acc(dkg_ref, dkg, first)
        for h in range(MLA_HEADS):
            _acc(dwq_ref.at[:, sl(h)], dwq[h], first)
            _acc(dwk_ref.at[:, sl(h)], dwk[h], first)
            _acc(dwv_ref.at[:, sl(h)], dwv[h], first)

    return pl.pallas_call(
        body, name="mla_pre_bwd", grid=(s // TM,),
        in_specs=[_rows(TM, pc.shape[1]), _rows(TM, SLOT), _rows(TM, SLOT), _rows(TM, SLOT),
                  _full(qag), _full(kvag), _full(qg), _full(kg), _full(wq), _full(wk), _full(wv),
                  _rows(TM, hw), _rows(TM, hw), _rows(TM, hw)],
        out_specs=[_rows(TM, pc.shape[1]), _full(qag), _full(kvag), _full(qg), _full(kg),
                   _full(wq), _full(wk), _full(wv)],
        out_shape=[_sds(pc.shape), _sds(qag.shape), _sds(kvag.shape), _sds(qg.shape), _sds(kg.shape),
                   _sds(wq.shape), _sds(wk.shape), _sds(wv.shape)],
        compiler_params=_cp(("arbitrary",)),
    )(pc, cos_t, sin_a, sin_b, qag, kvag, qg, kg, wq, wk, wv, dq, dk, dv)


ATT_SCALE = QK_DIM ** -0.5
NEG_BIG = -1e30


def attn_fwd(q, k, v, riders=()):
    s = q.shape[0]
    nq = s // TQ
    hp = ATT_HEADS_PER_STEP
    sl = lambda j: slice(j * SLOT, (j + 1) * SLOT)

    wide = ATT_WIDE // TQ

    def body(q_ref, k_ref, v_ref, o_ref, lse_ref):
        qi = pl.program_id(1)
        lane = lax.broadcasted_iota(jnp.int32, (1, SLOT), 1)
        qs = [q_ref[:, sl(j)] for j in range(hp)]

        def step(ki, carry, n_tiles, masked):
            rk = pl.ds(pl.multiple_of(ki * TQ, TQ), n_tiles * TQ)
            if masked:
                row = lax.broadcasted_iota(jnp.int32, (TQ, n_tiles * TQ), 0) + (n_tiles - 1) * TQ
                col = lax.broadcasted_iota(jnp.int32, (TQ, n_tiles * TQ), 1)
            out = []
            for j in range(hp):
                m, acc = carry[j]
                sc = _dot(qs[j], k_ref[rk, sl(j)], NT) * ATT_SCALE
                if masked:
                    sc = jnp.where(row >= col, sc, NEG_BIG)
                m_new = jnp.maximum(m, jnp.max(sc, axis=-1, keepdims=True))
                acc = jnp.exp(m - m_new) * acc + _dot(jnp.exp(sc - m_new), v_ref[rk, sl(j)], NN)
                out.append((m_new, acc))
            return tuple(out)

        def tail_single(cr):
            cr = lax.fori_loop(n_wide * wide, qi, lambda ki, c: step(ki, c, 1, False), cr)
            return step(qi, cr, 1, True)

        n_wide = qi // wide
        init = tuple((jnp.full((TQ, 1), NEG_BIG, F32), jnp.zeros((TQ, SLOT), F32)) for _ in range(hp))
        carry = lax.fori_loop(0, n_wide, lambda kw, cr: step(kw * wide, cr, wide, False), init)
        carry = lax.cond(qi % wide == wide - 1, lambda cr: step(qi - (wide - 1), cr, wide, True), tail_single, carry)
        for j in range(hp):
            m, acc = carry[j]
            l = jnp.sum(jnp.where(lane == V_DIM, acc, 0.0), axis=-1, keepdims=True)
            o_ref[:, sl(j)] = jnp.where(lane < V_DIM, acc / l, 0.0)
            lse_ref[j] = m + jnp.log(l)

    head_col = pl.BlockSpec((s, hp * SLOT), lambda g, i: (0, g))
    tile = pl.BlockSpec((TQ, hp * SLOT), lambda g, i: (i, g))
    return _ride(
        body, riders, name="attn_fwd", grid=(MLA_HEADS // hp, nq),
        in_specs=[tile, head_col, head_col],
        out_specs=[tile, pl.BlockSpec((hp, TQ, 1), lambda g, i: (g, i, 0))],
        out_shape=[_sds((s, MLA_HEADS * SLOT)), _sds((MLA_HEADS, s, 1))],
        operands=(q, k, v), sem=("parallel", "parallel"))


def attn_bwd(q, k, v, o, do, lse, riders=()):
    s = q.shape[0]
    nq = s // TQ
    hp = ATT_HEADS_PER_STEP
    sl = lambda j: slice(j * SLOT, (j + 1) * SLOT)
    wide = ATT_WIDE // TQ

    def body(q_ref, k_ref, v_ref, o_ref, do_ref, lse_ref, dq_ref, dk_ref, dv_ref, delta_ref):
        ki = pl.program_id(1)

        @pl.when(ki == 0)
        def _():
            dq_ref[...] = jnp.zeros_like(dq_ref)

            def prep(i, c):
                rows = pl.ds(pl.multiple_of(i * TQ, TQ), TQ)
                for j in range(hp):
                    delta_ref[j, rows, :] = jnp.sum(do_ref[rows, sl(j)] * o_ref[rows, sl(j)], axis=-1, keepdims=True)
                return c

            lax.fori_loop(0, nq, prep, 0)

        kks = [k_ref[:, sl(j)] for j in range(hp)]
        vvs = [v_ref[:, sl(j)] for j in range(hp)]

        def step(qi, carry, n_tiles, masked):
            rq = pl.ds(pl.multiple_of(qi * TQ, TQ), n_tiles * TQ)
            if masked:
                row = lax.broadcasted_iota(jnp.int32, (n_tiles * TQ, TQ), 0)
                col = lax.broadcasted_iota(jnp.int32, (n_tiles * TQ, TQ), 1)
            out = []
            for j in range(hp):
                dk, dv = carry[j]
                qq = q_ref[rq, sl(j)]
                dd = do_ref[rq, sl(j)]
                sc = _dot(qq, kks[j], NT) * ATT_SCALE
                if masked:
                    sc = jnp.where(row >= col, sc, NEG_BIG)
                p = jnp.exp(sc - lse_ref[j, rq, :])
                dv = dv + _dot(p, dd, TN)
                ds = p * (_dot(dd, vvs[j], NT) - delta_ref[j, rq, :]) * ATT_SCALE
                dk = dk + _dot(ds, qq, TN)
                dq_ref[rq, sl(j)] = dq_ref[rq, sl(j)] + _dot(ds, kks[j], NN)
                out.append((dk, dv))
            return tuple(out)

        def head_single(cr):
            cr = step(ki, cr, 1, True)
            return lax.fori_loop(ki + 1, first_wide * wide, lambda qi, c: step(qi, c, 1, False), cr)

        zero = jnp.zeros((TQ, SLOT), F32)
        first_wide = (ki + wide) // wide
        carry = tuple((zero, zero) for _ in range(hp))
        carry = lax.cond(ki % wide == 0, lambda cr: step(ki, cr, wide, True), head_single, carry)
        carry = lax.fori_loop(first_wide, nq // wide, lambda qw, cr: step(qw * wide, cr, wide, False), carry)
        for j in range(hp):
            dk_ref[:, sl(j)] = carry[j][0]
            dv_ref[:, sl(j)] = carry[j][1]

    head_col = pl.BlockSpec((s, hp * SLOT), lambda g, i: (0, g))
    tile = pl.BlockSpec((TQ, hp * SLOT), lambda g, i: (i, g))
    return _ride(
        body, riders, name="attn_bwd", grid=(MLA_HEADS // hp, nq),
        in_specs=[head_col, tile, tile, head_col, head_col, pl.BlockSpec((hp, s, 1), lambda g, i: (g, 0, 0))],
        out_specs=[head_col, tile, tile],
        out_shape=[_sds((s, MLA_HEADS * SLOT))] * 3,
        scratch_shapes=[pltpu.VMEM((hp, s, 1), F32)],
        operands=(q, k, v, o, do, lse), sem=("arbitrary", "arbitrary"))


def _outproj_args(ya_ref, yb_ref, o_ref, mog_ref, woa_ref, wob_ref, woc_ref, cast):
    sl = lambda h: slice(h * SLOT, (h + 1) * SLOT)
    ldw = (lambda r: r[...].astype(F32)) if cast else (lambda r: r[...])
    return (ya_ref[...], yb_ref[...], [o_ref[:, sl(h)] for h in range(MLA_HEADS)],
            [mog_ref[:, sl(h)] for h in range(MLA_HEADS)], ldw(woa_ref), ldw(wob_ref), ldw(woc_ref))


def outproj_fwd(x, ya, yb, o, mog, woa, wob, woc):
    s, d = x.shape

    def body(x_ref, ya_ref, yb_ref, o_ref, mog_ref, woa_ref, wob_ref, woc_ref, x1_ref):
        x1_ref[...] = _outproj(PLAIN, x_ref[...], *_outproj_args(ya_ref, yb_ref, o_ref, mog_ref, woa_ref, wob_ref,
                                                                  woc_ref, False))

    return pl.pallas_call(
        body, name="outproj_fwd", grid=(s // TM,),
        in_specs=[_rows(TM, d), _rows(TM, ya.shape[1]), _rows(TM, yb.shape[1]), _rows(TM, o.shape[1]),
                  _full(mog), _full(woa), _full(wob), _full(woc)],
        out_specs=_rows(TM, d), out_shape=_sds((s, d)),
        compiler_params=_cp(("parallel",)),
    )(x, ya, yb, o, mog, woa, wob, woc)


def outproj_bwd(ya, yb, o, mog, woa, wob, woc, dx2, dx1p, riders=()):
    s, d = dx2.shape
    npart = dx1p.shape[0]

    def body(ya_ref, yb_ref, o_ref, mog_ref, woa_ref, wob_ref, woc_ref, dx2_ref, dx1p_ref,
             dx1_ref, dya_ref, dyb_ref, do_ref, dmog_ref, dwoa_ref, dwob_ref, dwoc_ref):
        first = pl.program_id(0) == 0
        sl = lambda h: slice(h * SLOT, (h + 1) * SLOT)
        dx1 = dx2_ref[...]
        for p in range(npart):
            dx1 = dx1 + dx1p_ref[p]
        dx1_ref[...] = dx1
        args = _outproj_args(ya_ref, yb_ref, o_ref, mog_ref, woa_ref, wob_ref, woc_ref, True)
        _, vjp = jax.vjp(lambda *a: _outproj(AD, jnp.zeros_like(dx1), *a), *args)
        dya, dyb, do, dmog, dwoa, dwob, dwoc = vjp(dx1)
        dya_ref[...] = dya
        dyb_ref[...] = dyb
        _acc(dwoa_ref, dwoa, first)
        _acc(dwob_ref, dwob, first)
        _acc(dwoc_ref, dwoc, first)
        for h in range(MLA_HEADS):
            do_ref[:, sl(h)] = do[h]
            _acc(dmog_ref.at[:, sl(h)], dmog[h], first)

    return _ride(
        body, riders, name="outproj_bwd", grid=(s // TM,),
        in_specs=[_rows(TM, ya.shape[1]), _rows(TM, yb.shape[1]), _rows(TM, o.shape[1]),
                  _full(mog), _full(woa), _full(wob), _full(woc), _rows(TM, d),
                  pl.BlockSpec((npart, TM, d), lambda i: (0, i, 0))],
        out_specs=[_rows(TM, d), _rows(TM, ya.shape[1]), _rows(TM, yb.shape[1]), _rows(TM, o.shape[1]),
                   _full(mog), _full(woa), _full(wob), _full(woc)],
        out_shape=[_sds((s, d)), _sds(ya.shape), _sds(yb.shape), _sds(o.shape),
                   _sds(mog.shape), _sds(woa.shape), _sds(wob.shape), _sds(woc.shape)],
        operands=(ya, yb, o, mog, woa, wob, woc, dx2, dx1p), sem=("arbitrary",))


def ffn_fwd(x1, g2, w1, w2, riders=()):
    s, d = x1.shape
    npart, _, fs = w1.shape

    def body(x1_ref, g_ref, w1_ref, w2_ref, x2_ref, r_ref):
        p = pl.program_id(1)
        x1v = x1_ref[...]
        r = jnp.maximum(PLAIN.mm(_rms(x1v, g_ref[...]), w1_ref[...]), 0.0)
        r_ref[...] = r.astype(r_ref.dtype)
        part = PLAIN.mm(r * r, w2_ref[...])

        @pl.when(p == 0)
        def _():
            x2_ref[...] = x1v + part

        @pl.when(p != 0)
        def _():
            x2_ref[...] = x2_ref[...] + part

    tm = min(2 * TM_FFN, s)
    return _ride(
        body, riders, name="ffn_fwd", grid=(s // tm, npart),
        in_specs=[pl.BlockSpec((tm, d), lambda i, p: (i, 0)), pl.BlockSpec(g2.shape, lambda i, p: (0, 0)),
                  pl.BlockSpec((None, d, fs), lambda i, p: (p, 0, 0)), pl.BlockSpec((None, fs, d), lambda i, p: (p, 0, 0))],
        out_specs=[pl.BlockSpec((tm, d), lambda i, p: (i, 0)), pl.BlockSpec((tm, fs), lambda i, p: (i, p))],
        out_shape=[_sds((s, d)), _sds((s, npart * fs), MXU_DTYPE)],
        operands=(x1, g2, w1, w2), sem=("parallel", "arbitrary"))


def ffn_bwd(x1, g2, w1, w2, r, dx2, riders=()):
    s, d = x1.shape
    npart, _, fs = w1.shape

    def body(x1_ref, g_ref, w1_ref, w2_ref, r_ref, dx2_ref, dx1p_ref, dg_ref, dw1_ref, dw2_ref):
        p = pl.program_id(0)
        i = pl.program_id(1)
        h2, vjp_norm = jax.vjp(_rms, x1_ref[...], g_ref[...])
        rr = r_ref[...].astype(F32)
        dy = dx2_ref[...]
        da = _dot(dy, w2_ref[...], NT) * (2.0 * rr)
        dx1, dg = vjp_norm(_dot(da, w1_ref[...], NT))
        dx1p_ref[...] = dx1
        _acc(dg_ref, dg, (p == 0) & (i == 0))
        _acc(dw1_ref, _dot(h2, da, TN), i == 0)
        _acc(dw2_ref, _dot(rr * rr, dy, TN), i == 0)

    tm = TM_FFN
    return _ride(
        body, riders, name="ffn_bwd", grid=(npart, s // tm),
        in_specs=[pl.BlockSpec((tm, d), lambda p, i: (i, 0)), pl.BlockSpec(g2.shape, lambda p, i: (0, 0)),
                  pl.BlockSpec((None, d, fs), lambda p, i: (p, 0, 0)), pl.BlockSpec((None, fs, d), lambda p, i: (p, 0, 0)),
                  pl.BlockSpec((tm, fs), lambda p, i: (i, p)), pl.BlockSpec((tm, d), lambda p, i: (i, 0))],
        out_specs=[pl.BlockSpec((None, tm, d), lambda p, i: (p, i, 0)), pl.BlockSpec(g2.shape, lambda p, i: (0, 0)),
                   pl.BlockSpec((None, d, fs), lambda p, i: (p, 0, 0)), pl.BlockSpec((None, fs, d), lambda p, i: (p, 0, 0))],
        out_shape=[_sds((npart, s, d)), _sds(g2.shape), _sds(w1.shape), _sds(w2.shape)],
        operands=(x1, g2, w1, w2, r, dx2), sem=("arbitrary", "arbitrary"))


def loss_head(y, target):
    s, d = y.shape

    def body(y_ref, t_ref, dy_ref, loss_ref):
        err = y_ref[...] - t_ref[...]
        dy_ref[...] = err * (1.0 / d)
        part = jnp.sum(jnp.sum(err * err, axis=-1, keepdims=True), axis=0, keepdims=True) * (0.5 / d)
        _acc(loss_ref, jnp.broadcast_to(part, loss_ref.shape), pl.program_id(0) == 0)

    return pl.pallas_call(
        body, name="loss_head", grid=(s // TM,),
        in_specs=[_rows(TM, d), _rows(TM, d)],
        out_specs=[_rows(TM, d), pl.BlockSpec((1, SLOT), lambda i: (0, 0))],
        out_shape=[_sds((s, d)), _sds((1, SLOT))],
        compiler_params=_cp(("arbitrary",)),
    )(y, target)


def _row_block(r):
    for b in (512, 256, 128, 64, 32, 16, 8):
        if r % b == 0:
            return b
    return r


def sum_cores(arrs, gots, half, me):
    n = len(arrs)

    def body(sp_ref, *refs):
        for i in range(n):
            a_ref, g_ref, wire_ref, own_ref = refs[i], refs[n + i], refs[2 * n + i], refs[3 * n + i]
            tot = a_ref[...] + g_ref[...]
            wire_ref[...] = tot.astype(wire_ref.dtype)

            @pl.when(pl.program_id(0) == sp_ref[1])
            def _(own_ref=own_ref, tot=tot):
                own_ref[...] = tot

    shapes = [a.shape[2:] for a in arrs]
    grid_spec = pltpu.PrefetchScalarGridSpec(
        num_scalar_prefetch=1, grid=(N_CHIPS,),
        in_specs=[pl.BlockSpec((None, None) + sh, lambda p, sp: (p, sp[0], 0, 0)) for sh in shapes]
        + [pl.BlockSpec((None,) + sh, lambda p, sp: (p, 0, 0)) for sh in shapes],
        out_specs=[pl.BlockSpec((None,) + sh, lambda p, sp: (p, 0, 0)) for sh in shapes]
        + [pl.BlockSpec(sh, lambda p, sp: (0, 0)) for sh in shapes])
    outs = pl.pallas_call(body, name="sum_cores", grid_spec=grid_spec,
                          out_shape=[_sds((N_CHIPS,) + sh, BF16) for sh in shapes] + [_sds(sh) for sh in shapes],
                          compiler_params=_cp(("arbitrary",)))(jnp.stack([half, me]).astype(jnp.int32), *arrs, *gots)
    return outs[:n], outs[n:]


SUM_STEPS = 4


def sum_chips(owns, recvs, half, places):
    n = len(owns)
    dests = [d for _, _, d in places if d is not None]

    def body(sp_ref, *refs):
        del sp_ref
        for i in range(n):
            own_ref, out_ref = refs[4 * i], refs[4 * n + len(dests) + i]
            r0, r1, r2 = (refs[4 * i + 1 + j][...].astype(F32) for j in range(3))
            out_ref[...] = ((own_ref[...] + r0) + r1) + r2

    in_specs, out_specs, operands, aliases = [], [], [], {}
    for i, (own, recv, (layer, _, dest)) in enumerate(zip(owns, recvs, places)):
        r, c = own.shape
        br = r // SUM_STEPS
        in_specs.append(pl.BlockSpec((br, c), lambda i, sp: (i, 0)))
        in_specs += [pl.BlockSpec((None, br, c), functools.partial(lambda i, sp, j: (j, i, 0), j=j)) for j in range(3)]
        out_specs.append(pl.BlockSpec((None, None, br, c), functools.partial(lambda i, sp, l: (l, sp[0], i, 0), l=layer)))
        operands += [own, recv, recv, recv]
        if dest is not None:
            aliases[1 + 4 * n + len(aliases)] = i
    grid_spec = pltpu.PrefetchScalarGridSpec(num_scalar_prefetch=1, grid=(SUM_STEPS,), in_specs=in_specs + [ANY] * len(dests),
                                             out_specs=out_specs)
    return pl.pallas_call(body, name="sum_chips", grid_spec=grid_spec, input_output_aliases=aliases,
                          out_shape=[_sds((nl, 2) + o.shape) for o, (_, nl, _) in zip(owns, places)],
                          compiler_params=_cp(("parallel",)))(half.reshape(1).astype(jnp.int32), *operands, *dests)


def adamw(w, g, m, v, name, riders=()):
    r, c = w.shape
    br = _row_block(r)
    c1 = 1.0 / (1.0 - ADAM_B1 ** ADAM_STEP)
    c2 = 1.0 / (1.0 - ADAM_B2 ** ADAM_STEP)

    def body(w_ref, g_ref, m_ref, v_ref, d_ref, nm_ref, nv_ref):
        gg = g_ref[...]
        nm = ADAM_B1 * m_ref[...] + (1.0 - ADAM_B1) * gg
        nv = ADAM_B2 * v_ref[...] + (1.0 - ADAM_B2) * (gg * gg)
        d_ref[...] = -ADAM_LR * ((nm * c1) / (jnp.sqrt(nv * c2) + ADAM_EPS) + ADAM_WD * w_ref[...])
        nm_ref[...] = nm
        nv_ref[...] = nv

    return _ride(body, riders, name=name, grid=(r // br,), in_specs=[_rows(br, c)] * 4, out_specs=[_rows(br, c)] * 3,
                 out_shape=[_sds((r, c))] * 3, operands=(w, g, m, v), sem=("parallel",))


def _place():
    x, y, c = lax.axis_index("x"), lax.axis_index("y"), lax.axis_index("c")
    chips = [(1 - x, y), (x, 1 - y), (1 - x, 1 - y)]
    return x, y, c, chips


def _remote(src, dst, send_sem, recv_sem, to):
    return pltpu.make_async_remote_copy(src_ref=src, dst_ref=dst, send_sem=send_sem, recv_sem=recv_sem,
                                        device_id=to, device_id_type=MESH)


def gather_rider(arrs):
    n = len(arrs)
    me_chip = 2 * lax.axis_index("x") + lax.axis_index("y")
    bufs = [lax.dynamic_update_index_in_dim(lax.empty((N_CHIPS,) + a.shape, a.dtype), a, me_chip, 0) for a in arrs]

    def plan(ins, outs, sems):
        send_sems, recv_sems = sems
        x, y, c, chips = _place()
        me = 2 * x + y
        half, other, sibling = pl.ds(2 * c, 2), pl.ds(2 - 2 * c, 2), (x, y, 1 - c)
        cp = lambda i, k, src, dst, to: _remote(src, dst, send_sems.at[i, k], recv_sems.at[i, k], to)
        pairs = [(i, j, cx, cy) for i in range(n) for j, (cx, cy) in enumerate(chips)]
        blk = lambda i, cx, cy, part: outs[i].at[2 * cx + cy, part]
        first = lambda: [cp(i, j, ins[i].at[half], outs[i].at[me, half], (cx, cy, c)) for i, j, cx, cy in pairs]
        landed = lambda: [cp(i, j, blk(i, cx, cy, half), blk(i, cx, cy, half), (cx, cy, c)) for i, j, cx, cy in pairs]
        passed = lambda: [cp(i, 3 + j, blk(i, cx, cy, half), blk(i, cx, cy, half), sibling) for i, j, cx, cy in pairs]
        from_sibling = lambda: [cp(i, 3 + j, blk(i, cx, cy, other), blk(i, cx, cy, other), sibling) for i, j, cx, cy in pairs]
        return first, landed, passed, from_sibling

    def start(ins, outs, sems):
        for cp in plan(ins, outs, sems)[0]():
            cp.start()

    def finish(ins, outs, sems):
        first, landed, passed, from_sibling = plan(ins, outs, sems)
        forwards = passed()
        for a, b in zip(landed(), forwards):
            a.wait_recv()
            b.start()
        for cp in from_sibling():
            cp.wait_recv()
        for cp in first() + forwards:
            cp.wait_send()

    return Rider(list(arrs) + bufs, [_sds((N_CHIPS,) + a.shape, a.dtype) for a in arrs], {n + i: i for i in range(n)},
                 [pltpu.SemaphoreType.DMA((n, 6)), pltpu.SemaphoreType.DMA((n, 6))], start, finish)


class Reducer:
    def __init__(self, arrs, places=None):
        self.a = list(arrs)
        self.n = len(self.a)
        self.places = list(places) if places is not None else [(0, 1, None)] * self.n
        self.c = lax.axis_index("c")
        self.me = 2 * lax.axis_index("x") + lax.axis_index("y")

    def swap_rider(self):
        n = self.n

        def plan(ins, outs, sems):
            x, y, c, _ = _place()
            return [_remote(ins[i].at[p, 1 - c], outs[i].at[p], sems[0].at[i, p], sems[1].at[i, p], (x, y, 1 - c))
                    for i in range(n) for p in range(N_CHIPS)]

        return Rider(self.a, [_sds((N_CHIPS,) + a.shape[2:]) for a in self.a], {},
                     [pltpu.SemaphoreType.DMA((n, N_CHIPS)), pltpu.SemaphoreType.DMA((n, N_CHIPS))],
                     lambda *r: [cp.start() for cp in plan(*r)], lambda *r: [cp.wait() for cp in plan(*r)])

    def after_swap(self, got):
        self.wire, self.own = sum_cores(self.a, got, self.c, self.me)

    def scatter_rider(self):
        n = self.n

        def plan(ins, outs, sems):
            x, y, c, chips = _place()
            return [_remote(ins[i].at[2 * cx + cy], outs[i].at[j], sems[0].at[i, j], sems[1].at[i, j], (cx, cy, c))
                    for i in range(n) for j, (cx, cy) in enumerate(chips)]

        return Rider(self.wire, [_sds((3,) + w.shape[1:], w.dtype) for w in self.wire], {},
                     [pltpu.SemaphoreType.DMA((n, 3)), pltpu.SemaphoreType.DMA((n, 3))],
                     lambda *r: [cp.start() for cp in plan(*r)], lambda *r: [cp.wait() for cp in plan(*r)])

    def after_scatter(self, recv):
        self.full = sum_chips(self.own, recv, self.c, self.places)

    def share_rider(self):
        n = self.n
        layers = [layer for layer, _, _ in self.places]

        def plan(ins, outs, sems):
            x, y, c, _ = _place()
            return [_remote(ins[i].at[layers[i], c], outs[i].at[layers[i], c], sems[0].at[i], sems[1].at[i], (x, y, 1 - c))
                    for i in range(n)]

        return Rider(self.full, [_sds(f.shape) for f in self.full], {i: i for i in range(n)},
                     [pltpu.SemaphoreType.DMA((n,)), pltpu.SemaphoreType.DMA((n,))],
                     lambda *r: [cp.start() for cp in plan(*r)], lambda *r: [cp.wait() for cp in plan(*r)])

    def run(self):
        self.after_swap(run_rider(self.swap_rider(), "swap_halves"))
        self.after_scatter(run_rider(self.scatter_rider(), "scatter_chips"))
        return run_rider(self.share_rider(), "share_halves")


def _pad_slots(a, live):
    lead = a.shape[:-1]
    a = a.reshape(lead + (MLA_HEADS, live))
    a = jnp.pad(a, [(0, 0)] * len(lead) + [(0, 0), (0, SLOT - live)])
    return a.reshape(lead + (MLA_HEADS * SLOT,))


def _unpad_slots(a, live):
    lead = a.shape[:-1]
    return a.reshape(lead + (MLA_HEADS, SLOT))[..., :live].reshape(lead + (MLA_HEADS * live,))


def _rope_tables(positions, s):
    half = QK_ROPE // 2
    inv_freq = ROPE_THETA ** (-jnp.arange(half, dtype=F32) / half)
    ang = positions.reshape(s).astype(F32)[:, None] * inv_freq[None, :]
    cos, sin = jnp.cos(ang), jnp.sin(ang)
    one = jnp.ones((s, QK_NOPE), F32)
    z64, z16, z32 = jnp.zeros((s, QK_NOPE), F32), jnp.zeros((s, half), F32), jnp.zeros((s, SLOT - QK_DIM), F32)
    cos_t = jnp.concatenate([one, cos, cos, z32], axis=1)
    sin_a = jnp.concatenate([z64, -sin, z16, z32], axis=1)
    sin_b = jnp.concatenate([z64, z16, sin, z32], axis=1)
    return cos_t, sin_a, sin_b


def _out_weights(full):
    w_out = jnp.concatenate([full["w_out"][p] for p in range(N_CHIPS)], axis=0)
    woc = w_out[512:].reshape(MLA_HEADS, V_DIM, D_MODEL)
    woc = jnp.pad(woc, ((0, 0), (0, SLOT - V_DIM), (0, 0))).reshape(MLA_HEADS * SLOT, D_MODEL)
    return dict(woa=w_out[:256], wob=w_out[256:512], woc=woc)


def _layer_weights(full, small, l):
    w_in = jnp.concatenate([full["w_in"][p] for p in range(N_CHIPS)], axis=1)
    wc = jnp.pad(w_in[:, 1536:], ((0, 0), (0, 512 - (w_in.shape[1] - 1536))))
    w_uq = jnp.concatenate([full["mla_w_uq"][p] for p in range(N_CHIPS)], axis=1)
    w_ukv = jnp.concatenate([full["mla_w_ukv"][p] for p in range(N_CHIPS)], axis=1)
    ukv = w_ukv.reshape(KV_LORA, MLA_HEADS, QK_NOPE + V_DIM)
    row = lambda a: a.reshape(1, -1)
    return dict(
        g1=row(small["norm1_gain"][l]), wa=w_in[:, :512], wb=w_in[:, 512:1536], wc=wc,
        vg=row(small["gm_v_gain"][l]), ws=small["gm_w_s"][l], bs=small["gm_b_s"][l].reshape(4, CHUNK, 1),
        gog=row(small["gm_out_gain"][l]), hog=small["hg_out_gain"][l].reshape(-1, 1),
        qag=row(small["mla_q_a_gain"][l]), kvag=row(small["mla_kv_a_gain"][l]),
        qg=row(jnp.pad(small["mla_q_gain"][l], (0, SLOT - QK_DIM))), kg=row(jnp.pad(small["mla_k_gain"][l], (0, SLOT - QK_DIM))),
        wq=_pad_slots(w_uq, QK_DIM), wk=_pad_slots(ukv[..., :QK_NOPE].reshape(KV_LORA, -1), QK_NOPE),
        wv=_pad_slots(ukv[..., QK_NOPE:].reshape(KV_LORA, -1), V_DIM),
        mog=row(_pad_slots(small["mla_out_gain"][l], V_DIM)),
        g2=row(small["norm2_gain"][l]),
    )


def _shard_cols(a):
    r, c4 = a.shape
    return a.reshape(r, N_CHIPS, c4 // N_CHIPS).transpose(1, 0, 2)


def local_step(x, positions, target, small, comm):
    s = x.shape[0]
    cos_t, sin_a, sin_b = _rope_tables(positions, s)
    lbs = lower_bounds_fwd(small["hg_lower_bound"])
    lw, saved = [], []
    for l in range(DEPTH):
        w = _layer_weights(comm.part(l, "in"), small, l)
        lw.append(w)
        lb = lbs[l].reshape(-1, 1)
        pa, pb, pc = inproj_fwd(x, w["g1"], w["wa"], w["wb"], w["wc"])
        ya = gm_fwd(pa, w["vg"], w["ws"], w["bs"], w["gog"])
        (yb, states), got = hg_fwd(pb, lb, w["hog"], [comm.gather_rider(l, "ff1")])
        comm.gathered(l, "ff1", got[0])
        q, k, v = mla_pre_fwd(pc, cos_t, sin_a, sin_b, w["qag"], w["kvag"], w["qg"], w["kg"], w["wq"], w["wk"], w["wv"])
        (o, lse), got = attn_fwd(q, k, v, [comm.gather_rider(l, "ff2"), comm.gather_rider(l, "out")])
        comm.gathered(l, "ff2", got[0])
        comm.gathered(l, "out", got[1])
        w.update(_out_weights(comm.part(l, "out")))
        x1 = outproj_fwd(x, ya, yb, o, w["mog"], w["woa"], w["wob"], w["woc"])
        w["w1"], w["w2"] = comm.part(l, "ff1")["w_ff1"], comm.part(l, "ff2")["w_ff2"]
        rider = comm.gather_rider(l + 1, "in") if l + 1 < DEPTH else None
        (x2, r), got = ffn_fwd(x1, w["g2"], w["w1"], w["w2"], [rider])
        comm.gathered(l + 1, "in", got[0])
        saved.append(dict(x=x, pa=pa, pb=pb, pc=pc, ya=ya, yb=yb, states=states, q=q, k=k, v=v, o=o, lse=lse, x1=x1, r=r, lb=lb))
        x = x2
    dx, loss_part = loss_head(x, target)
    groups = [dict() for _ in range(DEPTH)]
    sm = {n: [None] * DEPTH for n in ("norm1_gain", "gm_v_gain", "gm_w_s", "gm_b_s", "gm_out_gain", "hg_out_gain",
                                       "mla_q_a_gain", "mla_kv_a_gain", "mla_q_gain", "mla_k_gain", "mla_out_gain",
                                       "norm2_gain")}
    dlbs = [None] * DEPTH
    halves = lambda g: g.reshape(N_CHIPS, 2, g.shape[1] // 2, g.shape[2])
    take = lambda red, f: None if red is None else f(red)
    red_mix = None
    for l in reversed(range(DEPTH)):
        w, a = lw[l], saved[l]
        (dx1p, dg2, dw1, dw2), got = ffn_bwd(a["x1"], w["g2"], w["w1"], w["w2"], a["r"], dx,
                                             [take(red_mix, Reducer.swap_rider)])
        if red_mix:
            red_mix.after_swap(got[0])
        ffn_arrs = [halves(dw1), halves(dw2)]
        red_ffn = comm.reducer(ffn_arrs, FFN, l)
        (dx1, dya, dyb, do, dmog, dwoa, dwob, dwoc), got = outproj_bwd(
            a["ya"], a["yb"], a["o"], w["mog"], w["woa"], w["wob"], w["woc"], dx, dx1p,
            [take(red_ffn, Reducer.swap_rider), take(red_mix, Reducer.scatter_rider)])
        if red_ffn:
            red_ffn.after_swap(got[0])
        if red_mix:
            red_mix.after_scatter(got[1])
        (dq, dk, dv), got = attn_bwd(a["q"], a["k"], a["v"], a["o"], do, a["lse"],
                                     [take(red_ffn, Reducer.scatter_rider), take(red_mix, Reducer.share_rider)])
        if red_ffn:
            red_ffn.after_scatter(got[0])
        if red_mix:
            comm.reduced(MIX, got[1])
        dpc, dqag, dkvag, dqg, dkg, dwq, dwk, dwv = mla_pre_bwd(a["pc"], cos_t, sin_a, sin_b, w["qag"], w["kvag"], w["qg"],
                                                                  w["kg"], w["wq"], w["wk"], w["wv"], dq, dk, dv)
        (dpb, dlb, dhog), got = hg_bwd(a["pb"], a["lb"], w["hog"], a["states"], dyb, [take(red_ffn, Reducer.share_rider)])
        if red_ffn:
            comm.reduced(FFN, got[0])
        else:
            groups[l].update(zip(FFN, ffn_arrs))
        dpa, dvg, dws, dbs, dgog = gm_bwd(a["pa"], w["vg"], w["ws"], w["bs"], w["gog"], dya)
        dx, dg1, dwa, dwb, dwc = inproj_bwd(a["x"], w["g1"], w["wa"], w["wb"], w["wc"], dpa, dpb, dpc, dx1)
        dukv = jnp.concatenate([dwk.reshape(KV_LORA, MLA_HEADS, SLOT)[..., :QK_NOPE],
                                dwv.reshape(KV_LORA, MLA_HEADS, SLOT)[..., :V_DIM]], axis=-1)
        dwo = jnp.concatenate([dwoa, dwob, dwoc.reshape(MLA_HEADS, SLOT, D_MODEL)[:, :V_DIM].reshape(-1, D_MODEL)], axis=0)
        mix_arrs = [halves(_shard_cols(jnp.concatenate([dwa, dwb, dwc[:, :1952 - 1536]], axis=1))),
                    halves(_shard_cols(_unpad_slots(dwq, QK_DIM))), halves(_shard_cols(dukv.reshape(KV_LORA, -1))),
                    halves(dwo.reshape(N_CHIPS, -1, D_MODEL))]
        red_mix = comm.reducer(mix_arrs, MIX, l) if l > 0 else None
        if red_mix is None:
            groups[l].update(zip(MIX, mix_arrs))
        sm["norm1_gain"][l] = dg1[0]
        sm["gm_v_gain"][l] = dvg[0]
        sm["gm_w_s"][l] = dws
        sm["gm_b_s"][l] = dbs[..., 0]
        sm["gm_out_gain"][l] = dgog[0]
        sm["hg_out_gain"][l] = dhog[:, 0]
        sm["mla_q_a_gain"][l] = dqag[0]
        sm["mla_kv_a_gain"][l] = dkvag[0]
        sm["mla_q_gain"][l] = dqg[0, :QK_DIM]
        sm["mla_k_gain"][l] = dkg[0, :QK_DIM]
        sm["mla_out_gain"][l] = _unpad_slots(dmog[0], V_DIM)
        sm["norm2_gain"][l] = dg2[0]
        dlbs[l] = dlb[:, 0]
    sm["hg_lower_bound"] = [lower_bounds_bwd(small["hg_lower_bound"], jnp.stack(dlbs))]
    return loss_part, dx, groups, sm


MIX = ("w_in", "mla_w_uq", "mla_w_ukv", "w_out")
FFN = ("w_ff1", "w_ff2")
BIG = MIX + FFN
PARTS = {"in": ("w_in", "mla_w_uq", "mla_w_ukv"), "out": ("w_out",), "ff1": ("w_ff1",), "ff2": ("w_ff2",)}
SMALL = ("norm1_gain", "gm_v_gain", "gm_w_s", "gm_b_s", "gm_out_gain", "hg_lower_bound", "hg_out_gain",
         "mla_q_a_gain", "mla_kv_a_gain", "mla_q_gain", "mla_k_gain", "mla_out_gain", "norm2_gain")
ORDER = ("norm1_gain", "w_in", "gm_v_gain", "gm_w_s", "gm_b_s", "gm_out_gain", "hg_lower_bound", "hg_out_gain",
         "mla_q_a_gain", "mla_w_uq", "mla_kv_a_gain", "mla_w_ukv", "mla_q_gain", "mla_k_gain", "mla_out_gain",
         "w_out", "norm2_gain", "w_ff1", "w_ff2")
PACK_ROWS = 320


def _pack(pieces):
    flat = jnp.concatenate([a.reshape(-1) for a in pieces])
    total = 2 * N_CHIPS * PACK_ROWS * SLOT
    return jnp.pad(flat, (0, total - flat.shape[0]))


def _unpack(flat, shapes):
    out, off = [], 0
    for sh in shapes:
        size = 1
        for d in sh:
            size *= d
        out.append(flat[off:off + size].reshape(sh))
        off += size
    return out


class ChipComm:
    def __init__(self, shards):
        self.shards = shards
        self.full = {}
        self.grads = {}

    def gather_rider(self, l, part):
        return gather_rider([self.shards[n][l].astype(MXU_DTYPE).reshape(4, self.shards[n].shape[1] // 4, -1)
                             for n in PARTS[part]])

    def gathered(self, l, part, outs):
        if outs is not None:
            self.full[l, part] = {n: o.reshape((N_CHIPS,) + self.shards[n].shape[1:]) for n, o in zip(PARTS[part], outs)}

    def part(self, l, part):
        if (l, part) not in self.full:
            self.gathered(l, part, run_rider(self.gather_rider(l, part), "gather_weights"))
        return self.full[l, part]

    def places(self, names, l):
        return [(l, self.shards[n].shape[0], self.grads.get(n)) for n in names]

    def reducer(self, arrs, names, l):
        return Reducer(arrs, self.places(names, l))

    def reduced(self, names, outs):
        self.grads.update(zip(names, outs))


def kernel(x, positions, norm1_gain, w_in, gm_v_gain, gm_w_s, gm_b_s, gm_out_gain, hg_lower_bound, hg_out_gain, mla_q_a_gain, mla_w_uq, mla_kv_a_gain, mla_w_ukv, mla_q_gain, mla_k_gain, mla_out_gain, w_out, norm2_gain, w_ff1, w_ff2, loss_target, m_norm1_gain, m_w_in, m_gm_v_gain, m_gm_w_s, m_gm_b_s, m_gm_out_gain, m_hg_lower_bound, m_hg_out_gain, m_mla_q_a_gain, m_mla_w_uq, m_mla_kv_a_gain, m_mla_w_ukv, m_mla_q_gain, m_mla_k_gain, m_mla_out_gain, m_w_out, m_norm2_gain, m_w_ff1, m_w_ff2, v_norm1_gain, v_w_in, v_gm_v_gain, v_gm_w_s, v_gm_b_s, v_gm_out_gain, v_hg_lower_bound, v_hg_out_gain, v_mla_q_a_gain, v_mla_w_uq, v_mla_kv_a_gain, v_mla_w_ukv, v_mla_q_gain, v_mla_k_gain, v_mla_out_gain, v_w_out, v_norm2_gain, v_w_ff1, v_w_ff2):
    given = dict(locals())
    weights = {n: given[n] for n in ORDER}
    moms = {n: given["m_" + n] for n in ORDER}
    vars_ = {n: given["v_" + n] for n in ORDER}
    s, d = x.shape[1], x.shape[2]

    small = {n: weights[n] for n in SMALL}
    comm = ChipComm({n: weights[n] for n in BIG})
    loss_part, dx, groups, small_g = local_step(x.reshape(s, d), positions, loss_target.reshape(s, d), small, comm)
    loss = lax.psum(loss_part[0, 0], ("x", "y", "c"))

    pack_g = _pack([jnp.stack(small_g[n]) for n in SMALL]).reshape(N_CHIPS, 2, PACK_ROWS, SLOT)
    reduced = Reducer([groups[0][n] for n in MIX] + [pack_g], comm.places(MIX, 0) + [(0, 1, None)]).run()
    comm.reduced(MIX, reduced[:-1])
    pack_full = run_rider(gather_rider([reduced[-1].reshape(4, PACK_ROWS // 2, SLOT)]), "gather_small")[0].reshape(-1)
    grads = {n: comm.grads[n].reshape(weights[n].shape) for n in BIG}
    grads.update(zip(SMALL, _unpack(pack_full, [weights[n].shape for n in SMALL])))

    delta, new_m, new_v = {}, {}, {}
    flat2 = lambda a: a.reshape(-1, a.shape[-1])
    for n in ORDER:
        outs, _ = adamw(flat2(weights[n]), flat2(grads[n]), flat2(moms[n]), flat2(vars_[n]), "adamw_" + n)
        delta[n], new_m[n], new_v[n] = [o.reshape(weights[n].shape) for o in outs]

    return (loss, dx.reshape(x.shape), *[grads[n] for n in ORDER], *[delta[n] for n in ORDER],
            *[new_m[n] for n in ORDER], *[new_v[n] for n in ORDER])
```

```python
import functools

import jax
import jax.numpy as jnp
from jax import lax
from jax.experimental import pallas as pl
from jax.experimental.pallas import tpu as pltpu

F32 = jnp.float32
BF16 = jnp.bfloat16
MXU_DTYPE = BF16

D_MODEL = 1024
DEPTH = 4
CHUNK = 128
HG_CHUNK = 128
HG_CHUNKS = 4
GM_CHUNKS = 4
EPS = 1e-6
HEAD64 = 64
MLA_HEADS = 8
QK_NOPE = 64
QK_ROPE = 32
QK_DIM = 96
V_DIM = 64
Q_LORA = 256
KV_LORA = 128
SLOT = 128
ROPE_THETA = 10000.0
N_CHIPS = 4

ADAM_LR = 0.001
ADAM_B1 = 0.9
ADAM_B2 = 0.999
ADAM_EPS = 1e-08
ADAM_WD = 0.01
ADAM_STEP = 10

TM = 512
TM_FFN = 512
TQ = 256
ATT_HEADS_PER_STEP = 4
ATT_WIDE = 512
V7X_VMEM_BYTES = 64 * 1024 * 1024
VMEM_LIMIT = V7X_VMEM_BYTES * 7 // 8

NN = (((1,), (0,)), ((), ()))
NT = (((1,), (1,)), ((), ()))
TN = (((0,), (0,)), ((), ()))
BNN = (((2,), (1,)), ((0,), (0,)))
BNT = (((2,), (2,)), ((0,), (0,)))
BTN = (((1,), (1,)), ((0,), (0,)))


def _dot(a, b, dims):
    return lax.dot_general(a.astype(MXU_DTYPE), b.astype(MXU_DTYPE), dims, preferred_element_type=F32)


def _hdot(a, b, dims=NN):
    return lax.dot_general(a, b, dims, precision=lax.Precision.HIGHEST, preferred_element_type=F32)


def _make_ad(dims, da_dims, da_swap, db_dims, db_swap):
    @jax.custom_vjp
    def f(a, b):
        return _dot(a, b, dims)

    def fwd(a, b):
        return _dot(a, b, dims), (a, b)

    def bwd(res, g):
        a, b = res
        da = _dot(b, g, da_dims) if da_swap else _dot(g, b, da_dims)
        db = _dot(g, a, db_dims) if db_swap else _dot(a, g, db_dims)
        return da, db

    f.defvjp(fwd, bwd)
    return f


@functools.partial(jax.custom_vjp, nondiff_argnums=(1,))
def _roll_ad(x, shift):
    return pltpu.roll(x, shift, 1)


def _roll_ad_fwd(x, shift):
    return pltpu.roll(x, shift, 1), None


def _roll_ad_bwd(shift, _, g):
    return (pltpu.roll(g, (g.shape[1] - shift) % g.shape[1], 1),)


_roll_ad.defvjp(_roll_ad_fwd, _roll_ad_bwd)


class _Ops:
    pass


PLAIN = _Ops()
PLAIN.mm = lambda a, b: _dot(a, b, NN)
PLAIN.bmm = lambda a, b: _dot(a, b, BNN)
PLAIN.bmm_nt = lambda a, b: _dot(a, b, BNT)
PLAIN.bmm_tn = lambda a, b: _dot(a, b, BTN)
PLAIN.roll = lambda x, s: pltpu.roll(x, s, 1)

AD = _Ops()
AD.mm = _make_ad(NN, NT, False, TN, False)
AD.bmm = _make_ad(BNN, BNT, False, BTN, False)
AD.bmm_nt = _make_ad(BNT, BNN, False, BTN, True)
AD.bmm_tn = _make_ad(BTN, BNT, True, BNN, False)
AD.roll = _roll_ad


def _sigmoid(x):
    return jax.nn.sigmoid(x)


def _gelu(x):
    return 0.5 * x * (1.0 + jnp.tanh(0.7978845608028654 * (x + 0.044715 * (x * x * x))))


def _rms(x, g):
    return x * lax.rsqrt(jnp.mean(x * x, axis=-1, keepdims=True) + EPS) * g


def _head_masks256():
    lane = lax.broadcasted_iota(jnp.int32, (1, 4 * HEAD64), 1)
    return [(jnp.right_shift(lane, 6) == h).astype(F32) for h in range(4)]


def _headnorm256(x, g):
    ms = jnp.zeros_like(x)
    sq = x * x
    for m in _head_masks256():
        ms = ms + m * (jnp.sum(sq * m, axis=-1, keepdims=True) * (1.0 / HEAD64))
    return x * lax.rsqrt(ms + EPS) * g


def _slot_norm(x, g, n):
    return x * lax.rsqrt(jnp.sum(x * x, axis=-1, keepdims=True) * (1.0 / n) + EPS) * g


def _rope(ops, x, cos_t, sin_a, sin_b):
    return x * cos_t + ops.roll(x, SLOT - QK_ROPE // 2) * sin_a + ops.roll(x, QK_ROPE // 2) * sin_b


def _inproj(ops, x, g1, wa, wb, wc):
    h = _rms(x, g1)
    return ops.mm(h, wa), ops.mm(h, wb), ops.mm(h, wc)


def _gm_chunk(ops, ur, vr, vg, ws4, bs, og):
    c = ur.shape[0]
    masks = _head_masks256()
    mh = jnp.concatenate([m[None] for m in masks], axis=0)
    u = _gelu(ur)
    v = _headnorm256(_gelu(vr), vg)
    t = lax.broadcasted_iota(jnp.int32, (c, c), 0)
    s = lax.broadcasted_iota(jnp.int32, (c, c), 1)
    w = jnp.where((t >= s)[None], ws4, 0.0)
    y = jnp.sum(ops.bmm(w, v[None] * mh), axis=0)
    for h in range(4):
        y = y + bs[h] * masks[h]
    return _headnorm256(u * y, og)


def _hg_chunk(ops, st, qr, fr, ir, gr, lb, og):
    c, n = qr.shape
    nh = n // HEAD64
    heads = lambda x: x.reshape(nh, HEAD64, x.shape[-1])
    tr = lambda x: heads(x.T)
    lb4, og4 = heads(lb), heads(og)
    qx = tr(qr)
    q = qx * _sigmoid(qx)
    f = lb4 + (1.0 - lb4) * _sigmoid(tr(fr))
    k = 1.0 - f
    logf = jnp.log(f)
    v = tr(ir)
    gx = tr(gr)
    s = lax.broadcasted_iota(jnp.int32, (c, c), 0)
    t = lax.broadcasted_iota(jnp.int32, (c, c), 1)
    tl = lax.broadcasted_iota(jnp.int32, (1, c), 1).reshape(1, 1, c)
    b2 = _hdot(logf.reshape(n, c), (s <= t).astype(F32))
    b = heads(b2)
    btot = jnp.sum(logf, axis=2, keepdims=True)
    inter = ops.bmm_tn(st, q * jnp.exp(b))
    p4 = jnp.zeros((nh, c, c), F32)
    tt, ss = s, t
    lg = c.bit_length() - 2
    while lg >= 0:
        m = 1 << lg
        bnd = jnp.left_shift(jnp.right_shift(t, lg + 1), lg + 1) + (m - 1)
        r = heads(_hdot(b2, (s == bnd).astype(F32)))
        right = jnp.bitwise_and(jnp.right_shift(tl, lg), 1) == 1
        qe = jnp.where(right, q * jnp.exp(jnp.where(right, b - r, 0.0)), 0.0)
        ke = jnp.where(right, 0.0, k * jnp.exp(jnp.where(right, 0.0, r - b)))
        lm = ((jnp.right_shift(tt, lg + 1) == jnp.right_shift(ss, lg + 1))
              & (jnp.bitwise_and(jnp.right_shift(tt, lg), 1) == 1)
              & (jnp.bitwise_and(jnp.right_shift(ss, lg), 1) == 0))
        p4 = jnp.where(lm[None], ops.bmm_tn(qe, ke), p4)
        lg -= 1
    intra = ops.bmm_nt(v, p4)
    o = inter + intra + jnp.sum(q * k, axis=1, keepdims=True) * v
    st_new = st * jnp.exp(btot) + ops.bmm_nt(k * jnp.exp(btot - b), v)
    y = o * lax.rsqrt(jnp.mean(o * o, axis=1, keepdims=True) + EPS) * og4 * (gx * _sigmoid(gx))
    return st_new, y.reshape(n, c).T


def _mla_pre(ops, cq, ckv, kpe, cos_t, sin_a, sin_b, qag, kvag, qg, kg, wq, wk, wv):
    cqn = _rms(cq, qag)
    ckvn = _rms(ckv, kvag)
    kper = ops.roll(kpe, QK_NOPE)
    qs, ks, vs = [], [], []
    for h in range(MLA_HEADS):
        qh = _slot_norm(ops.mm(cqn, wq[h]), qg, QK_DIM)
        qs.append(_rope(ops, qh, cos_t, sin_a, sin_b))
        kh = _slot_norm(ops.mm(ckvn, wk[h]) + kper, kg, QK_DIM)
        ks.append(_rope(ops, kh, cos_t, sin_a, sin_b))
        vs.append(ops.mm(ckvn, wv[h]))
    return qs, ks, vs


def _outproj(ops, x, ya, yb, o, mog, woa, wob, woc):
    yc = jnp.concatenate([_slot_norm(o[h], mog[h], V_DIM) for h in range(MLA_HEADS)], axis=1)
    return x + ops.mm(ya, woa) + ops.mm(yb, wob) + ops.mm(yc, woc)


def _lower_bounds(r0, r1, r2, r3):
    mx = jnp.maximum(jnp.maximum(r0, r1), jnp.maximum(r2, r3))
    e0, e1, e2, e3 = jnp.exp(r0 - mx), jnp.exp(r1 - mx), jnp.exp(r2 - mx), jnp.exp(r3 - mx)
    inv = 1.0 / (e0 + e1 + e2 + e3)
    s1, s2, s3 = e1 * inv, e2 * inv, e3 * inv
    return jnp.zeros_like(r0), s1, s1 + s2, s1 + s2 + s3


def _cp(sem):
    return pltpu.CompilerParams(dimension_semantics=sem, vmem_limit_bytes=VMEM_LIMIT)


def _rows(tm, n):
    return pl.BlockSpec((tm, n), lambda i: (i, 0))


def _full(a):
    nd = len(a.shape)
    return pl.BlockSpec(a.shape, lambda *_: (0,) * nd, pipeline_mode=pl.Buffered(1))


def _sds(shape, dtype=F32):
    return jax.ShapeDtypeStruct(shape, dtype)


def _acc(ref, val, first):
    @pl.when(first)
    def _():
        ref[...] = val

    @pl.when(jnp.logical_not(first))
    def _():
        ref[...] = ref[...] + val


def _f32(ref):
    return ref[...].astype(F32)


MESH = pl.DeviceIdType.MESH
ANY = pl.BlockSpec(memory_space=pl.ANY)


class Rider:
    def __init__(self, arrays, out_shapes, aliases, sems, start, finish):
        self.arrays, self.out_shapes, self.aliases, self.sems = list(arrays), list(out_shapes), dict(aliases), list(sems)
        self.start, self.finish = start, finish


def run_rider(rider, name):
    n_in, n_out = len(rider.arrays), len(rider.out_shapes)

    def body(*refs):
        ins, outs, sems = refs[:n_in], refs[n_in:n_in + n_out], refs[n_in + n_out:]
        rider.start(ins, outs, sems)
        rider.finish(ins, outs, sems)

    return pl.pallas_call(
        body, name=name, in_specs=[ANY] * n_in, out_specs=[ANY] * n_out, out_shape=rider.out_shapes,
        input_output_aliases=rider.aliases, scratch_shapes=rider.sems,
    )(*rider.arrays)


def _merge_riders(riders):
    bounds, a0, o0, s0 = [], 0, 0, 0
    for r in riders:
        bounds.append((a0, o0, s0))
        a0, o0, s0 = a0 + len(r.arrays), o0 + len(r.out_shapes), s0 + len(r.sems)

    def part(k, ins, outs, sems):
        a, o, s = bounds[k]
        r = riders[k]
        return ins[a:a + len(r.arrays)], outs[o:o + len(r.out_shapes)], sems[s:s + len(r.sems)]

    return Rider(
        [x for r in riders for x in r.arrays], [x for r in riders for x in r.out_shapes],
        {bounds[k][0] + i: bounds[k][1] + o for k, r in enumerate(riders) for i, o in r.aliases.items()},
        [x for r in riders for x in r.sems],
        lambda *refs: [r.start(*part(k, *refs)) for k, r in enumerate(riders)],
        lambda *refs: [r.finish(*part(k, *refs)) for k, r in enumerate(riders)])


def _ride(compute, riders, *, name, grid, in_specs, out_specs, out_shape, operands, scratch_shapes=(), sem=None):
    single = not isinstance(out_shape, (list, tuple))
    if single:
        out_specs, out_shape = [out_specs], [out_shape]
    live = [r for r in riders if r is not None]
    if not live:
        res = pl.pallas_call(compute, name=name, grid=grid, in_specs=in_specs, out_specs=out_specs, out_shape=out_shape,
                             scratch_shapes=list(scratch_shapes), compiler_params=_cp(sem))(*operands)
        return (res[0] if single else res), [None] * len(riders)
    rider = live[0] if len(live) == 1 else _merge_riders(live)
    n_in, n_out, n_s = len(in_specs), len(out_specs), len(scratch_shapes)
    r_in, r_out = len(rider.arrays), len(rider.out_shapes)

    def body(*refs):
        ins, rins = refs[:n_in], refs[n_in:n_in + r_in]
        outs = refs[n_in + r_in:n_in + r_in + n_out]
        routs = refs[n_in + r_in + n_out:n_in + r_in + n_out + r_out]
        scr = refs[n_in + r_in + n_out + r_out:n_in + r_in + n_out + r_out + n_s]
        rsems = refs[n_in + r_in + n_out + r_out + n_s:]
        first = functools.reduce(jnp.logical_and, [pl.program_id(a) == 0 for a in range(len(grid))])
        last = functools.reduce(jnp.logical_and, [pl.program_id(a) == grid[a] - 1 for a in range(len(grid))])

        @pl.when(first)
        def _():
            rider.start(rins, routs, rsems)

        compute(*ins, *outs, *scr)

        @pl.when(last)
        def _():
            rider.finish(rins, routs, rsems)

    res = pl.pallas_call(
        body, name=name, grid=grid, in_specs=list(in_specs) + [ANY] * r_in, out_specs=list(out_specs) + [ANY] * r_out,
        out_shape=list(out_shape) + rider.out_shapes,
        input_output_aliases={n_in + k: n_out + v for k, v in rider.aliases.items()},
        scratch_shapes=list(scratch_shapes) + rider.sems, compiler_params=_cp(("arbitrary",) * len(grid)),
    )(*operands, *rider.arrays)
    main, rest, per_rider = res[:n_out], list(res[n_out:]), []
    for r in riders:
        per_rider.append(None if r is None else [rest.pop(0) for _ in r.out_shapes])
    return (main[0] if single else main), per_rider


def inproj_fwd(x, g1, wa, wb, wc):
    s, d = x.shape

    def body(x_ref, g_ref, wa_ref, wb_ref, wc_ref, pa_ref, pb_ref, pc_ref):
        pa, pb, pc = _inproj(PLAIN, x_ref[...], g_ref[...], wa_ref[...], wb_ref[...], wc_ref[...])
        pa_ref[...] = pa
        pb_ref[...] = pb
        pc_ref[...] = pc

    return pl.pallas_call(
        body, name="inproj_fwd", grid=(s // TM,),
        in_specs=[_rows(TM, d), _full(g1), _full(wa), _full(wb), _full(wc)],
        out_specs=[_rows(TM, wa.shape[1]), _rows(TM, wb.shape[1]), _rows(TM, wc.shape[1])],
        out_shape=[_sds((s, wa.shape[1])), _sds((s, wb.shape[1])), _sds((s, wc.shape[1]))],
        compiler_params=_cp(("parallel",)),
    )(x, g1, wa, wb, wc)


def inproj_bwd(x, g1, wa, wb, wc, dpa, dpb, dpc, dres):
    s, d = x.shape

    def body(x_ref, g_ref, wa_ref, wb_ref, wc_ref, dpa_ref, dpb_ref, dpc_ref, dres_ref,
             dx_ref, dg_ref, dwa_ref, dwb_ref, dwc_ref):
        first = pl.program_id(0) == 0
        _, vjp = jax.vjp(functools.partial(_inproj, AD), x_ref[...], g_ref[...],
                         _f32(wa_ref), _f32(wb_ref), _f32(wc_ref))
        dx, dg, dwa, dwb, dwc = vjp((dpa_ref[...], dpb_ref[...], dpc_ref[...]))
        dx_ref[...] = dx + dres_ref[...]
        _acc(dg_ref, dg, first)
        _acc(dwa_ref, dwa, first)
        _acc(dwb_ref, dwb, first)
        _acc(dwc_ref, dwc, first)

    return pl.pallas_call(
        body, name="inproj_bwd", grid=(s // TM,),
        in_specs=[_rows(TM, d), _full(g1), _full(wa), _full(wb), _full(wc),
                  _rows(TM, wa.shape[1]), _rows(TM, wb.shape[1]), _rows(TM, wc.shape[1]), _rows(TM, d)],
        out_specs=[_rows(TM, d), _full(g1), _full(wa), _full(wb), _full(wc)],
        out_shape=[_sds((s, d)), _sds(g1.shape), _sds(wa.shape), _sds(wb.shape), _sds(wc.shape)],
        compiler_params=_cp(("arbitrary",)),
    )(x, g1, wa, wb, wc, dpa, dpb, dpc, dres)


def gm_fwd(pa, vg, ws4, bs, og):
    s = pa.shape[0]
    w = pa.shape[1] // 2

    def body(pa_ref, vg_ref, ws_ref, bs_ref, og_ref, ya_ref):
        bsl = [bs_ref[h] for h in range(4)]
        for j in range(GM_CHUNKS):
            rows = slice(j * CHUNK, (j + 1) * CHUNK)
            ya_ref[rows, :] = _gm_chunk(PLAIN, pa_ref[rows, 0:w], pa_ref[rows, w:2 * w], vg_ref[...], ws_ref[...], bsl,
                                        og_ref[...])

    tm = GM_CHUNKS * CHUNK
    return pl.pallas_call(
        body, name="gm_fwd", grid=(s // tm,),
        in_specs=[_rows(tm, 2 * w), _full(vg), _full(ws4), _full(bs), _full(og)],
        out_specs=_rows(tm, w), out_shape=_sds((s, w)),
        compiler_params=_cp(("parallel",)),
    )(pa, vg, ws4, bs, og)


def gm_bwd(pa, vg, ws4, bs, og, dya):
    s = pa.shape[0]
    w = pa.shape[1] // 2

    def body(pa_ref, vg_ref, ws_ref, bs_ref, og_ref, dya_ref, dpa_ref, dvg_ref, dws_ref, dbs_ref, dog_ref):
        first = pl.program_id(0) == 0
        bsl = [bs_ref[h] for h in range(4)]
        tot = None
        for j in range(GM_CHUNKS):
            rows = slice(j * CHUNK, (j + 1) * CHUNK)
            _, vjp = jax.vjp(functools.partial(_gm_chunk, AD), pa_ref[rows, 0:w], pa_ref[rows, w:2 * w],
                             vg_ref[...], ws_ref[...], bsl, og_ref[...])
            du, dv, *dws = vjp(dya_ref[rows, :])
            dpa_ref[rows, 0:w] = du
            dpa_ref[rows, w:2 * w] = dv
            tot = dws if tot is None else jax.tree.map(jnp.add, tot, dws)
        dvg, dws, dbs, dog = tot
        _acc(dvg_ref, dvg, first)
        _acc(dws_ref, dws, first)
        _acc(dog_ref, dog, first)
        for h in range(4):
            _acc(dbs_ref.at[h], dbs[h], first)

    tm = GM_CHUNKS * CHUNK
    return pl.pallas_call(
        body, name="gm_bwd", grid=(s // tm,),
        in_specs=[_rows(tm, 2 * w), _full(vg), _full(ws4), _full(bs), _full(og), _rows(tm, w)],
        out_specs=[_rows(tm, 2 * w), _full(vg), _full(ws4), _full(bs), _full(og)],
        out_shape=[_sds((s, 2 * w)), _sds(vg.shape), _sds(ws4.shape), _sds(bs.shape), _sds(og.shape)],
        compiler_params=_cp(("arbitrary",)),
    )(pa, vg, ws4, bs, og, dya)


def hg_fwd(pb, lb, og, riders=()):
    s = pb.shape[0]
    w = pb.shape[1] // 4
    tm = HG_CHUNKS * HG_CHUNK
    st_shape = (w // HEAD64, HEAD64, HEAD64)

    def body(pb_ref, lb_ref, og_ref, yb_ref, states_ref, st_ref):
        @pl.when(pl.program_id(0) == 0)
        def _():
            st_ref[...] = jnp.zeros_like(st_ref)

        st = st_ref[...]
        for j in range(HG_CHUNKS):
            rows = slice(j * HG_CHUNK, (j + 1) * HG_CHUNK)
            states_ref[j] = st
            st, y = _hg_chunk(PLAIN, st, pb_ref[rows, 0:w], pb_ref[rows, w:2 * w], pb_ref[rows, 2 * w:3 * w],
                              pb_ref[rows, 3 * w:4 * w], lb_ref[...], og_ref[...])
            yb_ref[rows, :] = y
        st_ref[...] = st

    return _ride(
        body, riders, name="hg_fwd", grid=(s // tm,),
        in_specs=[_rows(tm, 4 * w), _full(lb), _full(og)],
        out_specs=[_rows(tm, w), pl.BlockSpec((HG_CHUNKS,) + st_shape, lambda i: (i, 0, 0, 0))],
        out_shape=[_sds((s, w)), _sds((s // HG_CHUNK,) + st_shape)],
        scratch_shapes=[pltpu.VMEM(st_shape, F32)],
        operands=(pb, lb, og), sem=("arbitrary",))


def hg_bwd(pb, lb, og, states, dyb, riders=()):
    s = pb.shape[0]
    w = pb.shape[1] // 4
    tm = HG_CHUNKS * HG_CHUNK
    nc = s // tm
    st_shape = (w // HEAD64, HEAD64, HEAD64)

    def body(pb_ref, lb_ref, og_ref, states_ref, dyb_ref, dpb_ref, dlb_ref, dog_ref, dst_ref):
        first = pl.program_id(0) == 0

        @pl.when(first)
        def _():
            dst_ref[...] = jnp.zeros_like(dst_ref)

        dst, dlb, dog = dst_ref[...], None, None
        for j in reversed(range(HG_CHUNKS)):
            rows = slice(j * HG_CHUNK, (j + 1) * HG_CHUNK)
            _, vjp = jax.vjp(functools.partial(_hg_chunk, AD), states_ref[j], pb_ref[rows, 0:w], pb_ref[rows, w:2 * w],
                             pb_ref[rows, 2 * w:3 * w], pb_ref[rows, 3 * w:4 * w], lb_ref[...], og_ref[...])
            dst, dq, df, di, dg, dlb_j, dog_j = vjp((dst, dyb_ref[rows, :]))
            dpb_ref[rows, 0:w] = dq
            dpb_ref[rows, w:2 * w] = df
            dpb_ref[rows, 2 * w:3 * w] = di
            dpb_ref[rows, 3 * w:4 * w] = dg
            dlb = dlb_j if dlb is None else dlb + dlb_j
            dog = dog_j if dog is None else dog + dog_j
        dst_ref[...] = dst
        _acc(dlb_ref, dlb, first)
        _acc(dog_ref, dog, first)

    rev = lambda i: (nc - 1 - i, 0)
    return _ride(
        body, riders, name="hg_bwd", grid=(nc,),
        in_specs=[pl.BlockSpec((tm, 4 * w), rev), _full(lb), _full(og),
                  pl.BlockSpec((HG_CHUNKS,) + st_shape, lambda i: (nc - 1 - i, 0, 0, 0)), pl.BlockSpec((tm, w), rev)],
        out_specs=[pl.BlockSpec((tm, 4 * w), rev), _full(lb), _full(og)],
        out_shape=[_sds((s, 4 * w)), _sds(lb.shape), _sds(og.shape)],
        scratch_shapes=[pltpu.VMEM(st_shape, F32)],
        operands=(pb, lb, og, states, dyb), sem=("arbitrary",))


def lower_bounds_fwd(hlb):
    def body(h_ref, o_ref):
        outs = _lower_bounds(*[h_ref[pl.ds(i, 1), :] for i in range(DEPTH)])
        for i in range(DEPTH):
            o_ref[pl.ds(i, 1), :] = outs[i]

    return pl.pallas_call(body, name="lower_bounds_fwd", out_shape=_sds(hlb.shape))(hlb)


def lower_bounds_bwd(hlb, dlbs):
    def body(h_ref, d_ref, o_ref):
        _, vjp = jax.vjp(_lower_bounds, *[h_ref[pl.ds(i, 1), :] for i in range(DEPTH)])
        outs = vjp(tuple(d_ref[pl.ds(i, 1), :] for i in range(DEPTH)))
        for i in range(DEPTH):
            o_ref[pl.ds(i, 1), :] = outs[i]

    return pl.pallas_call(body, name="lower_bounds_bwd", out_shape=_sds(hlb.shape))(hlb, dlbs)


def _mla_pre_args(pc_ref, cos_ref, sa_ref, sb_ref, qag_ref, kvag_ref, qg_ref, kg_ref, wq_ref, wk_ref, wv_ref, cast):
    sl = lambda h: slice(h * SLOT, (h + 1) * SLOT)
    ld = (lambda r, h: r[:, sl(h)].astype(F32)) if cast else (lambda r, h: r[:, sl(h)])
    diff = (pc_ref[:, 0:Q_LORA], pc_ref[:, Q_LORA:Q_LORA + KV_LORA], pc_ref[:, Q_LORA + KV_LORA:Q_LORA + 2 * KV_LORA],
            qag_ref[...], kvag_ref[...], qg_ref[...], kg_ref[...],
            [ld(wq_ref, h) for h in range(MLA_HEADS)], [ld(wk_ref, h) for h in range(MLA_HEADS)],
            [ld(wv_ref, h) for h in range(MLA_HEADS)])
    tables = (cos_ref[...], sa_ref[...], sb_ref[...])
    return diff, tables


def _mla_pre_fn(ops, tables, cq, ckv, kpe, qag, kvag, qg, kg, wq, wk, wv):
    return _mla_pre(ops, cq, ckv, kpe, *tables, qag, kvag, qg, kg, wq, wk, wv)


def mla_pre_fwd(pc, cos_t, sin_a, sin_b, qag, kvag, qg, kg, wq, wk, wv):
    s = pc.shape[0]
    hw = MLA_HEADS * SLOT

    def body(pc_ref, cos_ref, sa_ref, sb_ref, qag_ref, kvag_ref, qg_ref, kg_ref, wq_ref, wk_ref, wv_ref,
             q_ref, k_ref, v_ref):
        diff, tables = _mla_pre_args(pc_ref, cos_ref, sa_ref, sb_ref, qag_ref, kvag_ref, qg_ref, kg_ref,
                                     wq_ref, wk_ref, wv_ref, False)
        qs, ks, vs = _mla_pre_fn(PLAIN, tables, *diff)
        ones_lane = (lax.broadcasted_iota(jnp.int32, (1, SLOT), 1) == V_DIM).astype(F32)
        for h in range(MLA_HEADS):
            q_ref[:, h * SLOT:(h + 1) * SLOT] = qs[h].astype(q_ref.dtype)
            k_ref[:, h * SLOT:(h + 1) * SLOT] = ks[h].astype(k_ref.dtype)
            v_ref[:, h * SLOT:(h + 1) * SLOT] = (vs[h] + ones_lane).astype(v_ref.dtype)

    return pl.pallas_call(
        body, name="mla_pre_fwd", grid=(s // TM,),
        in_specs=[_rows(TM, pc.shape[1]), _rows(TM, SLOT), _rows(TM, SLOT), _rows(TM, SLOT),
                  _full(qag), _full(kvag), _full(qg), _full(kg), _full(wq), _full(wk), _full(wv)],
        out_specs=[_rows(TM, hw)] * 3, out_shape=[_sds((s, hw), MXU_DTYPE)] * 3,
        compiler_params=_cp(("parallel",)),
    )(pc, cos_t, sin_a, sin_b, qag, kvag, qg, kg, wq, wk, wv)


def mla_pre_bwd(pc, cos_t, sin_a, sin_b, qag, kvag, qg, kg, wq, wk, wv, dq, dk, dv):
    s = pc.shape[0]
    hw = MLA_HEADS * SLOT

    def body(pc_ref, cos_ref, sa_ref, sb_ref, qag_ref, kvag_ref, qg_ref, kg_ref, wq_ref, wk_ref, wv_ref,
             dq_ref, dk_ref, dv_ref, dpc_ref, dqag_ref, dkvag_ref, dqg_ref, dkg_ref, dwq_ref, dwk_ref, dwv_ref):
        first = pl.program_id(0) == 0
        diff, tables = _mla_pre_args(pc_ref, cos_ref, sa_ref, sb_ref, qag_ref, kvag_ref, qg_ref, kg_ref,
                                     wq_ref, wk_ref, wv_ref, True)
        _, vjp = jax.vjp(functools.partial(_mla_pre_fn, AD, tables), *diff)
        sl = lambda h: slice(h * SLOT, (h + 1) * SLOT)
        cot = ([dq_ref[:, sl(h)] for h in range(MLA_HEADS)], [dk_ref[:, sl(h)] for h in range(MLA_HEADS)],
               [dv_ref[:, sl(h)] for h in range(MLA_HEADS)])
        dcq, dckv, dkpe, dqag, dkvag, dqg, dkg, dwq, dwk, dwv = vjp(cot)
        dpc_ref[:, 0:Q_LORA] = dcq
        dpc_ref[:, Q_LORA:Q_LORA + KV_LORA] = dckv
        dpc_ref[:, Q_LORA + KV_LORA:Q_LORA + 2 * KV_LORA] = dkpe
        _acc(dqag_ref, dqag, first)
        _acc(dkvag_ref, dkvag, first)
        _acc(dqg_ref, dqg, first)
        _acc(dkg_ref, dkg, first)
        for h in range(MLA_HEADS):
            _acc(dwq_ref.at[:, sl(h)], dwq[h], first)
            _acc(dwk_ref.at[:, sl(h)], dwk[h], first)
            _acc(dwv_ref.at[:, sl(h)], dwv[h], first)

    return pl.pallas_call(
        body, name="mla_pre_bwd", grid=(s // TM,),
        in_specs=[_rows(TM, pc.shape[1]), _rows(TM, SLOT), _rows(TM, SLOT), _rows(TM, SLOT),
                  _full(qag), _full(kvag), _full(qg), _full(kg), _full(wq), _full(wk), _full(wv),
                  _rows(TM, hw), _rows(TM, hw), _rows(TM, hw)],
        out_specs=[_rows(TM, pc.shape[1]), _full(qag), _full(kvag), _full(qg), _full(kg),
                   _full(wq), _full(wk), _full(wv)],
        out_shape=[_sds(pc.shape), _sds(qag.shape), _sds(kvag.shape), _sds(qg.shape), _sds(kg.shape),
                   _sds(wq.shape), _sds(wk.shape), _sds(wv.shape)],
        compiler_params=_cp(("arbitrary",)),
    )(pc, cos_t, sin_a, sin_b, qag, kvag, qg, kg, wq, wk, wv, dq, dk, dv)


ATT_SCALE = QK_DIM ** -0.5
NEG_BIG = -1e30


def attn_fwd(q, k, v, riders=()):
    s = q.shape[0]
    nq = s // TQ
    hp = ATT_HEADS_PER_STEP
    sl = lambda j: slice(j * SLOT, (j + 1) * SLOT)

    wide = ATT_WIDE // TQ

    def body(q_ref, k_ref, v_ref, o_ref, lse_ref):
        qi = pl.program_id(1)
        lane = lax.broadcasted_iota(jnp.int32, (1, SLOT), 1)
        qs = [q_ref[:, sl(j)] for j in range(hp)]

        def step(ki, carry, n_tiles, masked):
            rk = pl.ds(pl.multiple_of(ki * TQ, TQ), n_tiles * TQ)
            if masked:
                row = lax.broadcasted_iota(jnp.int32, (TQ, n_tiles * TQ), 0) + (n_tiles - 1) * TQ
                col = lax.broadcasted_iota(jnp.int32, (TQ, n_tiles * TQ), 1)
            out = []
            for j in range(hp):
                m, acc = carry[j]
                sc = _dot(qs[j], k_ref[rk, sl(j)], NT) * ATT_SCALE
                if masked:
                    sc = jnp.where(row >= col, sc, NEG_BIG)
                m_new = jnp.maximum(m, jnp.max(sc, axis=-1, keepdims=True))
                acc = jnp.exp(m - m_new) * acc + _dot(jnp.exp(sc - m_new), v_ref[rk, sl(j)], NN)
                out.append((m_new, acc))
            return tuple(out)

        def tail_single(cr):
            cr = lax.fori_loop(n_wide * wide, qi, lambda ki, c: step(ki, c, 1, False), cr)
            return step(qi, cr, 1, True)

        n_wide = qi // wide
        init = tuple((jnp.full((TQ, 1), NEG_BIG, F32), jnp.zeros((TQ, SLOT), F32)) for _ in range(hp))
        carry = lax.fori_loop(0, n_wide, lambda kw, cr: step(kw * wide, cr, wide, False), init)
        carry = lax.cond(qi % wide == wide - 1, lambda cr: step(qi - (wide - 1), cr, wide, True), tail_single, carry)
        for j in range(hp):
            m, acc = carry[j]
            l = jnp.sum(jnp.where(lane == V_DIM, acc, 0.0), axis=-1, keepdims=True)
            o_ref[:, sl(j)] = jnp.where(lane < V_DIM, acc / l, 0.0)
            lse_ref[j] = m + jnp.log(l)

    head_col = pl.BlockSpec((s, hp * SLOT), lambda g, i: (0, g))
    tile = pl.BlockSpec((TQ, hp * SLOT), lambda g, i: (i, g))
    return _ride(
        body, riders, name="attn_fwd", grid=(MLA_HEADS // hp, nq),
        in_specs=[tile, head_col, head_col],
        out_specs=[tile, pl.BlockSpec((hp, TQ, 1), lambda g, i: (g, i, 0))],
        out_shape=[_sds((s, MLA_HEADS * SLOT)), _sds((MLA_HEADS, s, 1))],
        operands=(q, k, v), sem=("parallel", "parallel"))


def attn_bwd(q, k, v, o, do, lse, riders=()):
    s = q.shape[0]
    nq = s // TQ
    hp = ATT_HEADS_PER_STEP
    sl = lambda j: slice(j * SLOT, (j + 1) * SLOT)
    wide = ATT_WIDE // TQ

    def body(q_ref, k_ref, v_ref, o_ref, do_ref, lse_ref, dq_ref, dk_ref, dv_ref, delta_ref):
        ki = pl.program_id(1)

        @pl.when(ki == 0)
        def _():
            dq_ref[...] = jnp.zeros_like(dq_ref)

            def prep(i, c):
                rows = pl.ds(pl.multiple_of(i * TQ, TQ), TQ)
                for j in range(hp):
                    delta_ref[j, rows, :] = jnp.sum(do_ref[rows, sl(j)] * o_ref[rows, sl(j)], axis=-1, keepdims=True)
                return c

            lax.fori_loop(0, nq, prep, 0)

        kks = [k_ref[:, sl(j)] for j in range(hp)]
        vvs = [v_ref[:, sl(j)] for j in range(hp)]

        def step(qi, carry, n_tiles, masked):
            rq = pl.ds(pl.multiple_of(qi * TQ, TQ), n_tiles * TQ)
            if masked:
                row = lax.broadcasted_iota(jnp.int32, (n_tiles * TQ, TQ), 0)
                col = lax.broadcasted_iota(jnp.int32, (n_tiles * TQ, TQ), 1)
            out = []
            for j in range(hp):
                dk, dv = carry[j]
                qq = q_ref[rq, sl(j)]
                dd = do_ref[rq, sl(j)]
                sc = _dot(qq, kks[j], NT) * ATT_SCALE
                if masked:
                    sc = jnp.where(row >= col, sc, NEG_BIG)
                p = jnp.exp(sc - lse_ref[j, rq, :])
                dv = dv + _dot(p, dd, TN)
                ds = p * (_dot(dd, vvs[j], NT) - delta_ref[j, rq, :]) * ATT_SCALE
                dk = dk + _dot(ds, qq, TN)
                dq_ref[rq, sl(j)] = dq_ref[rq, sl(j)] + _dot(ds, kks[j], NN)
                out.append((dk, dv))
            return tuple(out)

        def head_single(cr):
            cr = step(ki, cr, 1, True)
            return lax.fori_loop(ki + 1, first_wide * wide, lambda qi, c: step(qi, c, 1, False), cr)

        zero = jnp.zeros((TQ, SLOT), F32)
        first_wide = (ki + wide) // wide
        carry = tuple((zero, zero) for _ in range(hp))
        carry = lax.cond(ki % wide == 0, lambda cr: step(ki, cr, wide, True), head_single, carry)
        carry = lax.fori_loop(first_wide, nq // wide, lambda qw, cr: step(qw * wide, cr, wide, False), carry)
        for j in range(hp):
            dk_ref[:, sl(j)] = carry[j][0]
            dv_ref[:, sl(j)] = carry[j][1]

    head_col = pl.BlockSpec((s, hp * SLOT), lambda g, i: (0, g))
    tile = pl.BlockSpec((TQ, hp * SLOT), lambda g, i: (i, g))
    return _ride(
        body, riders, name="attn_bwd", grid=(MLA_HEADS // hp, nq),
        in_specs=[head_col, tile, tile, head_col, head_col, pl.BlockSpec((hp, s, 1), lambda g, i: (g, 0, 0))],
        out_specs=[head_col, tile, tile],
        out_shape=[_sds((s, MLA_HEADS * SLOT))] * 3,
        scratch_shapes=[pltpu.VMEM((hp, s, 1), F32)],
        operands=(q, k, v, o, do, lse), sem=("arbitrary", "arbitrary"))


def _outproj_args(ya_ref, yb_ref, o_ref, mog_ref, woa_ref, wob_ref, woc_ref, cast):
    sl = lambda h: slice(h * SLOT, (h + 1) * SLOT)
    ldw = (lambda r: r[...].astype(F32)) if cast else (lambda r: r[...])
    return (ya_ref[...], yb_ref[...], [o_ref[:, sl(h)] for h in range(MLA_HEADS)],
            [mog_ref[:, sl(h)] for h in range(MLA_HEADS)], ldw(woa_ref), ldw(wob_ref), ldw(woc_ref))


def outproj_fwd(x, ya, yb, o, mog, woa, wob, woc):
    s, d = x.shape

    def body(x_ref, ya_ref, yb_ref, o_ref, mog_ref, woa_ref, wob_ref, woc_ref, x1_ref):
        x1_ref[...] = _outproj(PLAIN, x_ref[...], *_outproj_args(ya_ref, yb_ref, o_ref, mog_ref, woa_ref, wob_ref,
                                                                  woc_ref, False))

    return pl.pallas_call(
        body, name="outproj_fwd", grid=(s // TM,),
        in_specs=[_rows(TM, d), _rows(TM, ya.shape[1]), _rows(TM, yb.shape[1]), _rows(TM, o.shape[1]),
                  _full(mog), _full(woa), _full(wob), _full(woc)],
        out_specs=_rows(TM, d), out_shape=_sds((s, d)),
        compiler_params=_cp(("parallel",)),
    )(x, ya, yb, o, mog, woa, wob, woc)


def outproj_bwd(ya, yb, o, mog, woa, wob, woc, dx2, dx1p, riders=()):
    s, d = dx2.shape
    npart = dx1p.shape[0]

    def body(ya_ref, yb_ref, o_ref, mog_ref, woa_ref, wob_ref, woc_ref, dx2_ref, dx1p_ref,
             dx1_ref, dya_ref, dyb_ref, do_ref, dmog_ref, dwoa_ref, dwob_ref, dwoc_ref):
        first = pl.program_id(0) == 0
        sl = lambda h: slice(h * SLOT, (h + 1) * SLOT)
        dx1 = dx2_ref[...]
        for p in range(npart):
            dx1 = dx1 + dx1p_ref[p]
        dx1_ref[...] = dx1
        args = _outproj_args(ya_ref, yb_ref, o_ref, mog_ref, woa_ref, wob_ref, woc_ref, True)
        _, vjp = jax.vjp(lambda *a: _outproj(AD, jnp.zeros_like(dx1), *a), *args)
        dya, dyb, do, dmog, dwoa, dwob, dwoc = vjp(dx1)
        dya_ref[...] = dya
        dyb_ref[...] = dyb
        _acc(dwoa_ref, dwoa, first)
        _acc(dwob_ref, dwob, first)
        _acc(dwoc_ref, dwoc, first)
        for h in range(MLA_HEADS):
            do_ref[:, sl(h)] = do[h]
            _acc(dmog_ref.at[:, sl(h)], dmog[h], first)

    return _ride(
        body, riders, name="outproj_bwd", grid=(s // TM,),
        in_specs=[_rows(TM, ya.shape[1]), _rows(TM, yb.shape[1]), _rows(TM, o.shape[1]),
                  _full(mog), _full(woa), _full(wob), _full(woc), _rows(TM, d),
                  pl.BlockSpec((npart, TM, d), lambda i: (0, i, 0))],
        out_specs=[_rows(TM, d), _rows(TM, ya.shape[1]), _rows(TM, yb.shape[1]), _rows(TM, o.shape[1]),
                   _full(mog), _full(woa), _full(wob), _full(woc)],
        out_shape=[_sds((s, d)), _sds(ya.shape), _sds(yb.shape), _sds(o.shape),
                   _sds(mog.shape), _sds(woa.shape), _sds(wob.shape), _sds(woc.shape)],
        operands=(ya, yb, o, mog, woa, wob, woc, dx2, dx1p), sem=("arbitrary",))


def ffn_fwd(x1, g2, w1, w2, riders=()):
    s, d = x1.shape
    npart, _, fs = w1.shape

    def body(x1_ref, g_ref, w1_ref, w2_ref, x2_ref, r_ref):
        p = pl.program_id(1)
        x1v = x1_ref[...]
        r = jnp.maximum(PLAIN.mm(_rms(x1v, g_ref[...]), w1_ref[...]), 0.0)
        r_ref[...] = r.astype(r_ref.dtype)
        part = PLAIN.mm(r * r, w2_ref[...])

        @pl.when(p == 0)
        def _():
            x2_ref[...] = x1v + part

        @pl.when(p != 0)
        def _():
            x2_ref[...] = x2_ref[...] + part

    tm = min(2 * TM_FFN, s)
    return _ride(
        body, riders, name="ffn_fwd", grid=(s // tm, npart),
        in_specs=[pl.BlockSpec((tm, d), lambda i, p: (i, 0)), pl.BlockSpec(g2.shape, lambda i, p: (0, 0)),
                  pl.BlockSpec((None, d, fs), lambda i, p: (p, 0, 0)), pl.BlockSpec((None, fs, d), lambda i, p: (p, 0, 0))],
        out_specs=[pl.BlockSpec((tm, d), lambda i, p: (i, 0)), pl.BlockSpec((tm, fs), lambda i, p: (i, p))],
        out_shape=[_sds((s, d)), _sds((s, npart * fs), MXU_DTYPE)],
        operands=(x1, g2, w1, w2), sem=("parallel", "arbitrary"))


def ffn_bwd(x1, g2, w1, w2, r, dx2, riders=()):
    s, d = x1.shape
    npart, _, fs = w1.shape

    def body(x1_ref, g_ref, w1_ref, w2_ref, r_ref, dx2_ref, dx1p_ref, dg_ref, dw1_ref, dw2_ref):
        p = pl.program_id(0)
        i = pl.program_id(1)
        h2, vjp_norm = jax.vjp(_rms, x1_ref[...], g_ref[...])
        rr = r_ref[...].astype(F32)
        dy = dx2_ref[...]
        da = _dot(dy, w2_ref[...], NT) * (2.0 * rr)
        dx1, dg = vjp_norm(_dot(da, w1_ref[...], NT))
        dx1p_ref[...] = dx1
        _acc(dg_ref, dg, (p == 0) & (i == 0))
        _acc(dw1_ref, _dot(h2, da, TN), i == 0)
        _acc(dw2_ref, _dot(rr * rr, dy, TN), i == 0)

    tm = TM_FFN
    return _ride(
        body, riders, name="ffn_bwd", grid=(npart, s // tm),
        in_specs=[pl.BlockSpec((tm, d), lambda p, i: (i, 0)), pl.BlockSpec(g2.shape, lambda p, i: (0, 0)),
                  pl.BlockSpec((None, d, fs), lambda p, i: (p, 0, 0)), pl.BlockSpec((None, fs, d), lambda p, i: (p, 0, 0)),
                  pl.BlockSpec((tm, fs), lambda p, i: (i, p)), pl.BlockSpec((tm, d), lambda p, i: (i, 0))],
        out_specs=[pl.BlockSpec((None, tm, d), lambda p, i: (p, i, 0)), pl.BlockSpec(g2.shape, lambda p, i: (0, 0)),
                   pl.BlockSpec((None, d, fs), lambda p, i: (p, 0, 0)), pl.BlockSpec((None, fs, d), lambda p, i: (p, 0, 0))],
        out_shape=[_sds((npart, s, d)), _sds(g2.shape), _sds(w1.shape), _sds(w2.shape)],
        operands=(x1, g2, w1, w2, r, dx2), sem=("arbitrary", "arbitrary"))


def loss_head(y, target):
    s, d = y.shape

    def body(y_ref, t_ref, dy_ref, loss_ref):
        err = y_ref[...] - t_ref[...]
        dy_ref[...] = err * (1.0 / d)
        part = jnp.sum(jnp.sum(err * err, axis=-1, keepdims=True), axis=0, keepdims=True) * (0.5 / d)
        _acc(loss_ref, jnp.broadcast_to(part, loss_ref.shape), pl.program_id(0) == 0)

    return pl.pallas_call(
        body, name="loss_head", grid=(s // TM,),
        in_specs=[_rows(TM, d), _rows(TM, d)],
        out_specs=[_rows(TM, d), pl.BlockSpec((1, SLOT), lambda i: (0, 0))],
        out_shape=[_sds((s, d)), _sds((1, SLOT))],
        compiler_params=_cp(("arbitrary",)),
    )(y, target)


def _row_block(r):
    for b in (512, 256, 128, 64, 32, 16, 8):
        if r % b == 0:
            return b
    return r


def sum_cores(arrs, gots, half, me):
    n = len(arrs)

    def body(sp_ref, *refs):
        for i in range(n):
            a_ref, g_ref, wire_ref, own_ref = refs[i], refs[n + i], refs[2 * n + i], refs[3 * n + i]
            tot = a_ref[...] + g_ref[...]
            wire_ref[...] = tot.astype(wire_ref.dtype)

            @pl.when(pl.program_id(0) == sp_ref[1])
            def _(own_ref=own_ref, tot=tot):
                own_ref[...] = tot

    shapes = [a.shape[2:] for a in arrs]
    grid_spec = pltpu.PrefetchScalarGridSpec(
        num_scalar_prefetch=1, grid=(N_CHIPS,),
        in_specs=[pl.BlockSpec((None, None) + sh, lambda p, sp: (p, sp[0], 0, 0)) for sh in shapes]
        + [pl.BlockSpec((None,) + sh, lambda p, sp: (p, 0, 0)) for sh in shapes],
        out_specs=[pl.BlockSpec((None,) + sh, lambda p, sp: (p, 0, 0)) for sh in shapes]
        + [pl.BlockSpec(sh, lambda p, sp: (0, 0)) for sh in shapes])
    outs = pl.pallas_call(body, name="sum_cores", grid_spec=grid_spec,
                          out_shape=[_sds((N_CHIPS,) + sh, BF16) for sh in shapes] + [_sds(sh) for sh in shapes],
                          compiler_params=_cp(("arbitrary",)))(jnp.stack([half, me]).astype(jnp.int32), *arrs, *gots)
    return outs[:n], outs[n:]


SUM_STEPS = 4


def sum_chips(owns, recvs, half, places):
    n = len(owns)
    dests = [d for _, _, d in places if d is not None]

    def body(sp_ref, *refs):
        del sp_ref
        for i in range(n):
            own_ref, out_ref = refs[4 * i], refs[4 * n + len(dests) + i]
            r0, r1, r2 = (refs[4 * i + 1 + j][...].astype(F32) for j in range(3))
            out_ref[...] = ((own_ref[...] + r0) + r1) + r2

    in_specs, out_specs, operands, aliases = [], [], [], {}
    for i, (own, recv, (layer, _, dest)) in enumerate(zip(owns, recvs, places)):
        r, c = own.shape
        br = r // SUM_STEPS
        in_specs.append(pl.BlockSpec((br, c), lambda i, sp: (i, 0)))
        in_specs += [pl.BlockSpec((None, br, c), functools.partial(lambda i, sp, j: (j, i, 0), j=j)) for j in range(3)]
        out_specs.append(pl.BlockSpec((None, None, br, c), functools.partial(lambda i, sp, l: (l, sp[0], i, 0), l=layer)))
        operands += [own, recv, recv, recv]
        if dest is not None:
            aliases[1 + 4 * n + len(aliases)] = i
    grid_spec = pltpu.PrefetchScalarGridSpec(num_scalar_prefetch=1, grid=(SUM_STEPS,), in_specs=in_specs + [ANY] * len(dests),
                                             out_specs=out_specs)
    return pl.pallas_call(body, name="sum_chips", grid_spec=grid_spec, input_output_aliases=aliases,
                          out_shape=[_sds((nl, 2) + o.shape) for o, (_, nl, _) in zip(owns, places)],
                          compiler_params=_cp(("parallel",)))(half.reshape(1).astype(jnp.int32), *operands, *dests)


def adamw(w, g, m, v, name, riders=()):
    r, c = w.shape
    br = _row_block(r)
    c1 = 1.0 / (1.0 - ADAM_B1 ** ADAM_STEP)
    c2 = 1.0 / (1.0 - ADAM_B2 ** ADAM_STEP)

    def body(w_ref, g_ref, m_ref, v_ref, d_ref, nm_ref, nv_ref):
        gg = g_ref[...]
        nm = ADAM_B1 * m_ref[...] + (1.0 - ADAM_B1) * gg
        nv = ADAM_B2 * v_ref[...] + (1.0 - ADAM_B2) * (gg * gg)
        d_ref[...] = -ADAM_LR * ((nm * c1) / (jnp.sqrt(nv * c2) + ADAM_EPS) + ADAM_WD * w_ref[...])
        nm_ref[...] = nm
        nv_ref[...] = nv

    return _ride(body, riders, name=name, grid=(r // br,), in_specs=[_rows(br, c)] * 4, out_specs=[_rows(br, c)] * 3,
                 out_shape=[_sds((r, c))] * 3, operands=(w, g, m, v), sem=("parallel",))


def _place():
    x, y, c = lax.axis_index("x"), lax.axis_index("y"), lax.axis_index("c")
    chips = [(1 - x, y), (x, 1 - y), (1 - x, 1 - y)]
    return x, y, c, chips


def _remote(src, dst, send_sem, recv_sem, to):
    return pltpu.make_async_remote_copy(src_ref=src, dst_ref=dst, send_sem=send_sem, recv_sem=recv_sem,
                                        device_id=to, device_id_type=MESH)


def gather_rider(arrs):
    n = len(arrs)
    me_chip = 2 * lax.axis_index("x") + lax.axis_index("y")
    bufs = [lax.dynamic_update_index_in_dim(lax.empty((N_CHIPS,) + a.shape, a.dtype), a, me_chip, 0) for a in arrs]

    def plan(ins, outs, sems):
        send_sems, recv_sems = sems
        x, y, c, chips = _place()
        me = 2 * x + y
        half, other, sibling = pl.ds(2 * c, 2), pl.ds(2 - 2 * c, 2), (x, y, 1 - c)
        cp = lambda i, k, src, dst, to: _remote(src, dst, send_sems.at[i, k], recv_sems.at[i, k], to)
        pairs = [(i, j, cx, cy) for i in range(n) for j, (cx, cy) in enumerate(chips)]
        blk = lambda i, cx, cy, part: outs[i].at[2 * cx + cy, part]
        first = lambda: [cp(i, j, ins[i].at[half], outs[i].at[me, half], (cx, cy, c)) for i, j, cx, cy in pairs]
        landed = lambda: [cp(i, j, blk(i, cx, cy, half), blk(i, cx, cy, half), (cx, cy, c)) for i, j, cx, cy in pairs]
        passed = lambda: [cp(i, 3 + j, blk(i, cx, cy, half), blk(i, cx, cy, half), sibling) for i, j, cx, cy in pairs]
        from_sibling = lambda: [cp(i, 3 + j, blk(i, cx, cy, other), blk(i, cx, cy, other), sibling) for i, j, cx, cy in pairs]
        return first, landed, passed, from_sibling

    def start(ins, outs, sems):
        for cp in plan(ins, outs, sems)[0]():
            cp.start()

    def finish(ins, outs, sems):
        first, landed, passed, from_sibling = plan(ins, outs, sems)
        forwards = passed()
        for a, b in zip(landed(), forwards):
            a.wait_recv()
            b.start()
        for cp in from_sibling():
            cp.wait_recv()
        for cp in first() + forwards:
            cp.wait_send()

    return Rider(list(arrs) + bufs, [_sds((N_CHIPS,) + a.shape, a.dtype) for a in arrs], {n + i: i for i in range(n)},
                 [pltpu.SemaphoreType.DMA((n, 6)), pltpu.SemaphoreType.DMA((n, 6))], start, finish)


class Reducer:
    def __init__(self, arrs, places=None):
        self.a = list(arrs)
        self.n = len(self.a)
        self.places = list(places) if places is not None else [(0, 1, None)] * self.n
        self.c = lax.axis_index("c")
        self.me = 2 * lax.axis_index("x") + lax.axis_index("y")

    def swap_rider(self):
        n = self.n

        def plan(ins, outs, sems):
            x, y, c, _ = _place()
            return [_remote(ins[i].at[p, 1 - c], outs[i].at[p], sems[0].at[i, p], sems[1].at[i, p], (x, y, 1 - c))
                    for i in range(n) for p in range(N_CHIPS)]

        return Rider(self.a, [_sds((N_CHIPS,) + a.shape[2:]) for a in self.a], {},
                     [pltpu.SemaphoreType.DMA((n, N_CHIPS)), pltpu.SemaphoreType.DMA((n, N_CHIPS))],
                     lambda *r: [cp.start() for cp in plan(*r)], lambda *r: [cp.wait() for cp in plan(*r)])

    def after_swap(self, got):
        self.wire, self.own = sum_cores(self.a, got, self.c, self.me)

    def scatter_rider(self):
        n = self.n

        def plan(ins, outs, sems):
            x, y, c, chips = _place()
            return [_remote(ins[i].at[2 * cx + cy], outs[i].at[j], sems[0].at[i, j], sems[1].at[i, j], (cx, cy, c))
                    for i in range(n) for j, (cx, cy) in enumerate(chips)]

        return Rider(self.wire, [_sds((3,) + w.shape[1:], w.dtype) for w in self.wire], {},
                     [pltpu.SemaphoreType.DMA((n, 3)), pltpu.SemaphoreType.DMA((n, 3))],
                     lambda *r: [cp.start() for cp in plan(*r)], lambda *r: [cp.wait() for cp in plan(*r)])

    def after_scatter(self, recv):
        self.full = sum_chips(self.own, recv, self.c, self.places)

    def share_rider(self):
        n = self.n
        layers = [layer for layer, _, _ in self.places]

        def plan(ins, outs, sems):
            x, y, c, _ = _place()
            return [_remote(ins[i].at[layers[i], c], outs[i].at[layers[i], c], sems[0].at[i], sems[1].at[i], (x, y, 1 - c))
                    for i in range(n)]

        return Rider(self.full, [_sds(f.shape) for f in self.full], {i: i for i in range(n)},
                     [pltpu.SemaphoreType.DMA((n,)), pltpu.SemaphoreType.DMA((n,))],
                     lambda *r: [cp.start() for cp in plan(*r)], lambda *r: [cp.wait() for cp in plan(*r)])

    def run(self):
        self.after_swap(run_rider(self.swap_rider(), "swap_halves"))
        self.after_scatter(run_rider(self.scatter_rider(), "scatter_chips"))
        return run_rider(self.share_rider(), "share_halves")


def _pad_slots(a, live):
    lead = a.shape[:-1]
    a = a.reshape(lead + (MLA_HEADS, live))
    a = jnp.pad(a, [(0, 0)] * len(lead) + [(0, 0), (0, SLOT - live)])
    return a.reshape(lead + (MLA_HEADS * SLOT,))


def _unpad_slots(a, live):
    lead = a.shape[:-1]
    return a.reshape(lead + (MLA_HEADS, SLOT))[..., :live].reshape(lead + (MLA_HEADS * live,))


def _rope_tables(positions, s):
    half = QK_ROPE // 2
    inv_freq = ROPE_THETA ** (-jnp.arange(half, dtype=F32) / half)
    ang = positions.reshape(s).astype(F32)[:, None] * inv_freq[None, :]
    cos, sin = jnp.cos(ang), jnp.sin(ang)
    one = jnp.ones((s, QK_NOPE), F32)
    z64, z16, z32 = jnp.zeros((s, QK_NOPE), F32), jnp.zeros((s, half), F32), jnp.zeros((s, SLOT - QK_DIM), F32)
    cos_t = jnp.concatenate([one, cos, cos, z32], axis=1)
    sin_a = jnp.concatenate([z64, -sin, z16, z32], axis=1)
    sin_b = jnp.concatenate([z64, z16, sin, z32], axis=1)
    return cos_t, sin_a, sin_b


def _out_weights(full):
    w_out = jnp.concatenate([full["w_out"][p] for p in range(N_CHIPS)], axis=0)
    woc = w_out[512:].reshape(MLA_HEADS, V_DIM, D_MODEL)
    woc = jnp.pad(woc, ((0, 0), (0, SLOT - V_DIM), (0, 0))).reshape(MLA_HEADS * SLOT, D_MODEL)
    return dict(woa=w_out[:256], wob=w_out[256:512], woc=woc)


def _layer_weights(full, small, l):
    w_in = jnp.concatenate([full["w_in"][p] for p in range(N_CHIPS)], axis=1)
    wc = jnp.pad(w_in[:, 1536:], ((0, 0), (0, 512 - (w_in.shape[1] - 1536))))
    w_uq = jnp.concatenate([full["mla_w_uq"][p] for p in range(N_CHIPS)], axis=1)
    w_ukv = jnp.concatenate([full["mla_w_ukv"][p] for p in range(N_CHIPS)], axis=1)
    ukv = w_ukv.reshape(KV_LORA, MLA_HEADS, QK_NOPE + V_DIM)
    row = lambda a: a.reshape(1, -1)
    return dict(
        g1=row(small["norm1_gain"][l]), wa=w_in[:, :512], wb=w_in[:, 512:1536], wc=wc,
        vg=row(small["gm_v_gain"][l]), ws=small["gm_w_s"][l], bs=small["gm_b_s"][l].reshape(4, CHUNK, 1),
        gog=row(small["gm_out_gain"][l]), hog=small["hg_out_gain"][l].reshape(-1, 1),
        qag=row(small["mla_q_a_gain"][l]), kvag=row(small["mla_kv_a_gain"][l]),
        qg=row(jnp.pad(small["mla_q_gain"][l], (0, SLOT - QK_DIM))), kg=row(jnp.pad(small["mla_k_gain"][l], (0, SLOT - QK_DIM))),
        wq=_pad_slots(w_uq, QK_DIM), wk=_pad_slots(ukv[..., :QK_NOPE].reshape(KV_LORA, -1), QK_NOPE),
        wv=_pad_slots(ukv[..., QK_NOPE:].reshape(KV_LORA, -1), V_DIM),
        mog=row(_pad_slots(small["mla_out_gain"][l], V_DIM)),
        g2=row(small["norm2_gain"][l]),
    )


def _shard_cols(a):
    r, c4 = a.shape
    return a.reshape(r, N_CHIPS, c4 // N_CHIPS).transpose(1, 0, 2)


def local_step(x, positions, target, small, comm):
    s = x.shape[0]
    cos_t, sin_a, sin_b = _rope_tables(positions, s)
    lbs = lower_bounds_fwd(small["hg_lower_bound"])
    lw, saved = [], []
    for l in range(DEPTH):
        w = _layer_weights(comm.part(l, "in"), small, l)
        lw.append(w)
        lb = lbs[l].reshape(-1, 1)
        pa, pb, pc = inproj_fwd(x, w["g1"], w["wa"], w["wb"], w["wc"])
        ya = gm_fwd(pa, w["vg"], w["ws"], w["bs"], w["gog"])
        (yb, states), got = hg_fwd(pb, lb, w["hog"], [comm.gather_rider(l, "ff1")])
        comm.gathered(l, "ff1", got[0])
        q, k, v = mla_pre_fwd(pc, cos_t, sin_a, sin_b, w["qag"], w["kvag"], w["qg"], w["kg"], w["wq"], w["wk"], w["wv"])
        (o, lse), got = attn_fwd(q, k, v, [comm.gather_rider(l, "ff2"), comm.gather_rider(l, "out")])
        comm.gathered(l, "ff2", got[0])
        comm.gathered(l, "out", got[1])
        w.update(_out_weights(comm.part(l, "out")))
        x1 = outproj_fwd(x, ya, yb, o, w["mog"], w["woa"], w["wob"], w["woc"])
        w["w1"], w["w2"] = comm.part(l, "ff1")["w_ff1"], comm.part(l, "ff2")["w_ff2"]
        rider = comm.gather_rider(l + 1, "in") if l + 1 < DEPTH else None
        (x2, r), got = ffn_fwd(x1, w["g2"], w["w1"], w["w2"], [rider])
        comm.gathered(l + 1, "in", got[0])
        saved.append(dict(x=x, pa=pa, pb=pb, pc=pc, ya=ya, yb=yb, states=states, q=q, k=k, v=v, o=o, lse=lse, x1=x1, r=r, lb=lb))
        x = x2
    dx, loss_part = loss_head(x, target)
    groups = [dict() for _ in range(DEPTH)]
    sm = {n: [None] * DEPTH for n in ("norm1_gain", "gm_v_gain", "gm_w_s", "gm_b_s", "gm_out_gain", "hg_out_gain",
                                       "mla_q_a_gain", "mla_kv_a_gain", "mla_q_gain", "mla_k_gain", "mla_out_gain",
                                       "norm2_gain")}
    dlbs = [None] * DEPTH
    halves = lambda g: g.reshape(N_CHIPS, 2, g.shape[1] // 2, g.shape[2])
    take = lambda red, f: None if red is None else f(red)
    red_mix = None
    for l in reversed(range(DEPTH)):
        w, a = lw[l], saved[l]
        (dx1p, dg2, dw1, dw2), got = ffn_bwd(a["x1"], w["g2"], w["w1"], w["w2"], a["r"], dx,
                                             [take(red_mix, Reducer.swap_rider)])
        if red_mix:
            red_mix.after_swap(got[0])
        ffn_arrs = [halves(dw1), halves(dw2)]
        red_ffn = comm.reducer(ffn_arrs, FFN, l)
        (dx1, dya, dyb, do, dmog, dwoa, dwob, dwoc), got = outproj_bwd(
            a["ya"], a["yb"], a["o"], w["mog"], w["woa"], w["wob"], w["woc"], dx, dx1p,
            [take(red_ffn, Reducer.swap_rider), take(red_mix, Reducer.scatter_rider)])
        if red_ffn:
            red_ffn.after_swap(got[0])
        if red_mix:
            red_mix.after_scatter(got[1])
        (dq, dk, dv), got = attn_bwd(a["q"], a["k"], a["v"], a["o"], do, a["lse"],
                                     [take(red_ffn, Reducer.scatter_rider), take(red_mix, Reducer.share_rider)])
        if red_ffn:
            red_ffn.after_scatter(got[0])
        if red_mix:
            comm.reduced(MIX, got[1])
        dpc, dqag, dkvag, dqg, dkg, dwq, dwk, dwv = mla_pre_bwd(a["pc"], cos_t, sin_a, sin_b, w["qag"], w["kvag"], w["qg"],
                                                                  w["kg"], w["wq"], w["wk"], w["wv"], dq, dk, dv)
        (dpb, dlb, dhog), got = hg_bwd(a["pb"], a["lb"], w["hog"], a["states"], dyb, [take(red_ffn, Reducer.share_rider)])
        if red_ffn:
            comm.reduced(FFN, got[0])
        else:
            groups[l].update(zip(FFN, ffn_arrs))
        dpa, dvg, dws, dbs, dgog = gm_bwd(a["pa"], w["vg"], w["ws"], w["bs"], w["gog"], dya)
        dx, dg1, dwa, dwb, dwc = inproj_bwd(a["x"], w["g1"], w["wa"], w["wb"], w["wc"], dpa, dpb, dpc, dx1)
        dukv = jnp.concatenate([dwk.reshape(KV_LORA, MLA_HEADS, SLOT)[..., :QK_NOPE],
                                dwv.reshape(KV_LORA, MLA_HEADS, SLOT)[..., :V_DIM]], axis=-1)
        dwo = jnp.concatenate([dwoa, dwob, dwoc.reshape(MLA_HEADS, SLOT, D_MODEL)[:, :V_DIM].reshape(-1, D_MODEL)], axis=0)
        mix_arrs = [halves(_shard_cols(jnp.concatenate([dwa, dwb, dwc[:, :1952 - 1536]], axis=1))),
                    halves(_shard_cols(_unpad_slots(dwq, QK_DIM))), halves(_shard_cols(dukv.reshape(KV_LORA, -1))),
                    halves(dwo.reshape(N_CHIPS, -1, D_MODEL))]
        red_mix = comm.reducer(mix_arrs, MIX, l) if l > 0 else None
        if red_mix is None:
            groups[l].update(zip(MIX, mix_arrs))
        sm["norm1_gain"][l] = dg1[0]
        sm["gm_v_gain"][l] = dvg[0]
        sm["gm_w_s"][l] = dws
        sm["gm_b_s"][l] = dbs[..., 0]
        sm["gm_out_gain"][l] = dgog[0]
        sm["hg_out_gain"][l] = dhog[:, 0]
        sm["mla_q_a_gain"][l] = dqag[0]
        sm["mla_kv_a_gain"][l] = dkvag[0]
        sm["mla_q_gain"][l] = dqg[0, :QK_DIM]
        sm["mla_k_gain"][l] = dkg[0, :QK_DIM]
        sm["mla_out_gain"][l] = _unpad_slots(dmog[0], V_DIM)
        sm["norm2_gain"][l] = dg2[0]
        dlbs[l] = dlb[:, 0]
    sm["hg_lower_bound"] = [lower_bounds_bwd(small["hg_lower_bound"], jnp.stack(dlbs))]
    return loss_part, dx, groups, sm


MIX = ("w_in", "mla_w_uq", "mla_w_ukv", "w_out")
FFN = ("w_ff1", "w_ff2")
BIG = MIX + FFN
PARTS = {"in": ("w_in", "mla_w_uq", "mla_w_ukv"), "out": ("w_out",), "ff1": ("w_ff1",), "ff2": ("w_ff2",)}
SMALL = ("norm1_gain", "gm_v_gain", "gm_w_s", "gm_b_s", "gm_out_gain", "hg_lower_bound", "hg_out_gain",
         "mla_q_a_gain", "mla_kv_a_gain", "mla_q_gain", "mla_k_gain", "mla_out_gain", "norm2_gain")
ORDER = ("norm1_gain", "w_in", "gm_v_gain", "gm_w_s", "gm_b_s", "gm_out_gain", "hg_lower_bound", "hg_out_gain",
         "mla_q_a_gain", "mla_w_uq", "mla_kv_a_gain", "mla_w_ukv", "mla_q_gain", "mla_k_gain", "mla_out_gain",
         "w_out", "norm2_gain", "w_ff1", "w_ff2")
PACK_ROWS = 320


def _pack(pieces):
    flat = jnp.concatenate([a.reshape(-1) for a in pieces])
    total = 2 * N_CHIPS * PACK_ROWS * SLOT
    return jnp.pad(flat, (0, total - flat.shape[0]))


def _unpack(flat, shapes):
    out, off = [], 0
    for sh in shapes:
        size = 1
        for d in sh:
            size *= d
        out.append(flat[off:off + size].reshape(sh))
        off += size
    return out


class ChipComm:
    def __init__(self, shards):
        self.shards = shards
        self.full = {}
        self.grads = {}

    def gather_rider(self, l, part):
        return gather_rider([self.shards[n][l].astype(MXU_DTYPE).reshape(4, self.shards[n].shape[1] // 4, -1)
                             for n in PARTS[part]])

    def gathered(self, l, part, outs):
        if outs is not None:
            self.full[l, part] = {n: o.reshape((N_CHIPS,) + self.shards[n].shape[1:]) for n, o in zip(PARTS[part], outs)}

    def part(self, l, part):
        if (l, part) not in self.full:
            self.gathered(l, part, run_rider(self.gather_rider(l, part), "gather_weights"))
        return self.full[l, part]

    def places(self, names, l):
        return [(l, self.shards[n].shape[0], self.grads.get(n)) for n in names]

    def reducer(self, arrs, names, l):
        return Reducer(arrs, self.places(names, l))

    def reduced(self, names, outs):
        self.grads.update(zip(names, outs))


def kernel(x, positions, norm1_gain, w_in, gm_v_gain, gm_w_s, gm_b_s, gm_out_gain, hg_lower_bound, hg_out_gain, mla_q_a_gain, mla_w_uq, mla_kv_a_gain, mla_w_ukv, mla_q_gain, mla_k_gain, mla_out_gain, w_out, norm2_gain, w_ff1, w_ff2, loss_target, m_norm1_gain, m_w_in, m_gm_v_gain, m_gm_w_s, m_gm_b_s, m_gm_out_gain, m_hg_lower_bound, m_hg_out_gain, m_mla_q_a_gain, m_mla_w_uq, m_mla_kv_a_gain, m_mla_w_ukv, m_mla_q_gain, m_mla_k_gain, m_mla_out_gain, m_w_out, m_norm2_gain, m_w_ff1, m_w_ff2, v_norm1_gain, v_w_in, v_gm_v_gain, v_gm_w_s, v_gm_b_s, v_gm_out_gain, v_hg_lower_bound, v_hg_out_gain, v_mla_q_a_gain, v_mla_w_uq, v_mla_kv_a_gain, v_mla_w_ukv, v_mla_q_gain, v_mla_k_gain, v_mla_out_gain, v_w_out, v_norm2_gain, v_w_ff1, v_w_ff2):
    given = dict(locals())
    weights = {n: given[n] for n in ORDER}
    moms = {n: given["m_" + n] for n in ORDER}
    vars_ = {n: given["v_" + n] for n in ORDER}
    s, d = x.shape[1], x.shape[2]

    small = {n: weights[n] for n in SMALL}
    comm = ChipComm({n: weights[n] for n in BIG})
    loss_part, dx, groups, small_g = local_step(x.reshape(s, d), positions, loss_target.reshape(s, d), small, comm)
    loss = lax.psum(loss_part[0, 0], ("x", "y", "c"))

    pack_g = _pack([jnp.stack(small_g[n]) for n in SMALL]).reshape(N_CHIPS, 2, PACK_ROWS, SLOT)
    reduced = Reducer([groups[0][n] for n in MIX] + [pack_g], comm.places(MIX, 0) + [(0, 1, None)]).run()
    comm.reduced(MIX, reduced[:-1])
    pack_full = run_rider(gather_rider([reduced[-1].reshape(4, PACK_ROWS // 2, SLOT)]), "gather_small")[0].reshape(-1)
    grads = {n: comm.grads[n].reshape(weights[n].shape) for n in BIG}
    grads.update(zip(SMALL, _unpack(pack_full, [weights[n].shape for n in SMALL])))

    delta, new_m, new_v = {}, {}, {}
    flat2 = lambda a: a.reshape(-1, a.shape[-1])
    for n in ORDER:
        outs, _ = adamw(flat2(weights[n]), flat2(grads[n]), flat2(moms[n]), flat2(vars_[n]), "adamw_" + n)
        delta[n], new_m[n], new_v[n] = [o.reshape(weights[n].shape) for o in outs]

    return (loss, dx.reshape(x.shape), *[grads[n] for n in ORDER], *[delta[n] for n in ORDER],
            *[new_m[n] for n in ORDER], *[new_v[n] for n in ORDER])
```

```python
import functools

import jax
import jax.numpy as jnp
from jax import lax
from jax.experimental import pallas as pl
from jax.experimental.pallas import tpu as pltpu

F32 = jnp.float32
BF16 = jnp.bfloat16
MXU_DTYPE = BF16

D_MODEL = 1024
DEPTH = 4
CHUNK = 128
HG_CHUNK = 128
HG_CHUNKS = 4
GM_CHUNKS = 4
EPS = 1e-6
HEAD64 = 64
MLA_HEADS = 8
QK_NOPE = 64
QK_ROPE = 32
QK_DIM = 96
V_DIM = 64
Q_LORA = 256
KV_LORA = 128
SLOT = 128
ROPE_THETA = 10000.0
N_CHIPS = 4

ADAM_LR = 0.001
ADAM_B1 = 0.9
ADAM_B2 = 0.999
ADAM_EPS = 1e-08
ADAM_WD = 0.01
ADAM_STEP = 10

TM = 512
TM_FFN = 512
TQ = 256
ATT_HEADS_PER_STEP = 4
ATT_WIDE = 512
V7X_VMEM_BYTES = 64 * 1024 * 1024
VMEM_LIMIT = V7X_VMEM_BYTES * 7 // 8

NN = (((1,), (0,)), ((), ()))
NT = (((1,), (1,)), ((), ()))
TN = (((0,), (0,)), ((), ()))
BNN = (((2,), (1,)), ((0,), (0,)))
BNT = (((2,), (2,)), ((0,), (0,)))
BTN = (((1,), (1,)), ((0,), (0,)))


def _dot(a, b, dims):
    return lax.dot_general(a.astype(MXU_DTYPE), b.astype(MXU_DTYPE), dims, preferred_element_type=F32)


def _hdot(a, b, dims=NN):
    return lax.dot_general(a, b, dims, precision=lax.Precision.HIGHEST, preferred_element_type=F32)


def _make_ad(dims, da_dims, da_swap, db_dims, db_swap):
    @jax.custom_vjp
    def f(a, b):
        return _dot(a, b, dims)

    def fwd(a, b):
        return _dot(a, b, dims), (a, b)

    def bwd(res, g):
        a, b = res
        da = _dot(b, g, da_dims) if da_swap else _dot(g, b, da_dims)
        db = _dot(g, a, db_dims) if db_swap else _dot(a, g, db_dims)
        return da, db

    f.defvjp(fwd, bwd)
    return f


@functools.partial(jax.custom_vjp, nondiff_argnums=(1,))
def _roll_ad(x, shift):
    return pltpu.roll(x, shift, 1)


def _roll_ad_fwd(x, shift):
    return pltpu.roll(x, shift, 1), None


def _roll_ad_bwd(shift, _, g):
    return (pltpu.roll(g, (g.shape[1] - shift) % g.shape[1], 1),)


_roll_ad.defvjp(_roll_ad_fwd, _roll_ad_bwd)


class _Ops:
    pass


PLAIN = _Ops()
PLAIN.mm = lambda a, b: _dot(a, b, NN)
PLAIN.bmm = lambda a, b: _dot(a, b, BNN)
PLAIN.bmm_nt = lambda a, b: _dot(a, b, BNT)
PLAIN.bmm_tn = lambda a, b: _dot(a, b, BTN)
PLAIN.roll = lambda x, s: pltpu.roll(x, s, 1)

AD = _Ops()
AD.mm = _make_ad(NN, NT, False, TN, False)
AD.bmm = _make_ad(BNN, BNT, False, BTN, False)
AD.bmm_nt = _make_ad(BNT, BNN, False, BTN, True)
AD.bmm_tn = _make_ad(BTN, BNT, True, BNN, False)
AD.roll = _roll_ad


def _sigmoid(x):
    return jax.nn.sigmoid(x)


def _gelu(x):
    return 0.5 * x * (1.0 + jnp.tanh(0.7978845608028654 * (x + 0.044715 * (x * x * x))))


def _rms(x, g):
    return x * lax.rsqrt(jnp.mean(x * x, axis=-1, keepdims=True) + EPS) * g


def _head_masks256():
    lane = lax.broadcasted_iota(jnp.int32, (1, 4 * HEAD64), 1)
    return [(jnp.right_shift(lane, 6) == h).astype(F32) for h in range(4)]


def _headnorm256(x, g):
    ms = jnp.zeros_like(x)
    sq = x * x
    for m in _head_masks256():
        ms = ms + m * (jnp.sum(sq * m, axis=-1, keepdims=True) * (1.0 / HEAD64))
    return x * lax.rsqrt(ms + EPS) * g


def _slot_norm(x, g, n):
    return x * lax.rsqrt(jnp.sum(x * x, axis=-1, keepdims=True) * (1.0 / n) + EPS) * g


def _rope(ops, x, cos_t, sin_a, sin_b):
    return x * cos_t + ops.roll(x, SLOT - QK_ROPE // 2) * sin_a + ops.roll(x, QK_ROPE // 2) * sin_b


def _inproj(ops, x, g1, wa, wb, wc):
    h = _rms(x, g1)
    return ops.mm(h, wa), ops.mm(h, wb), ops.mm(h, wc)


def _gm_chunk(ops, ur, vr, vg, ws4, bs, og):
    c = ur.shape[0]
    masks = _head_masks256()
    mh = jnp.concatenate([m[None] for m in masks], axis=0)
    u = _gelu(ur)
    v = _headnorm256(_gelu(vr), vg)
    t = lax.broadcasted_iota(jnp.int32, (c, c), 0)
    s = lax.broadcasted_iota(jnp.int32, (c, c), 1)
    w = jnp.where((t >= s)[None], ws4, 0.0)
    y = jnp.sum(ops.bmm(w, v[None] * mh), axis=0)
    for h in range(4):
        y = y + bs[h] * masks[h]
    return _headnorm256(u * y, og)


def _hg_chunk(ops, st, qr, fr, ir, gr, lb, og):
    c, n = qr.shape
    nh = n // HEAD64
    heads = lambda x: x.reshape(nh, HEAD64, x.shape[-1])
    tr = lambda x: heads(x.T)
    lb4, og4 = heads(lb), heads(og)
    qx = tr(qr)
    q = qx * _sigmoid(qx)
    f = lb4 + (1.0 - lb4) * _sigmoid(tr(fr))
    k = 1.0 - f
    logf = jnp.log(f)
    v = tr(ir)
    gx = tr(gr)
    s = lax.broadcasted_iota(jnp.int32, (c, c), 0)
    t = lax.broadcasted_iota(jnp.int32, (c, c), 1)
    tl = lax.broadcasted_iota(jnp.int32, (1, c), 1).reshape(1, 1, c)
    b2 = _hdot(logf.reshape(n, c), (s <= t).astype(F32))
    b = heads(b2)
    btot = jnp.sum(logf, axis=2, keepdims=True)
    inter = ops.bmm_tn(st, q * jnp.exp(b))
    p4 = jnp.zeros((nh, c, c), F32)
    tt, ss = s, t
    lg = c.bit_length() - 2
    while lg >= 0:
        m = 1 << lg
        bnd = jnp.left_shift(jnp.right_shift(t, lg + 1), lg + 1) + (m - 1)
        r = heads(_hdot(b2, (s == bnd).astype(F32)))
        right = jnp.bitwise_and(jnp.right_shift(tl, lg), 1) == 1
        qe = jnp.where(right, q * jnp.exp(jnp.where(right, b - r, 0.0)), 0.0)
        ke = jnp.where(right, 0.0, k * jnp.exp(jnp.where(right, 0.0, r - b)))
        lm = ((jnp.right_shift(tt, lg + 1) == jnp.right_shift(ss, lg + 1))
              & (jnp.bitwise_and(jnp.right_shift(tt, lg), 1) == 1)
              & (jnp.bitwise_and(jnp.right_shift(ss, lg), 1) == 0))
        p4 = jnp.where(lm[None], ops.bmm_tn(qe, ke), p4)
        lg -= 1
    intra = ops.bmm_nt(v, p4)
    o = inter + intra + jnp.sum(q * k, axis=1, keepdims=True) * v
    st_new = st * jnp.exp(btot) + ops.bmm_nt(k * jnp.exp(btot - b), v)
    y = o * lax.rsqrt(jnp.mean(o * o, axis=1, keepdims=True) + EPS) * og4 * (gx * _sigmoid(gx))
    return st_new, y.reshape(n, c).T


def _mla_pre(ops, cq, ckv, kpe, cos_t, sin_a, sin_b, qag, kvag, qg, kg, wq, wk, wv):
    cqn = _rms(cq, qag)
    ckvn = _rms(ckv, kvag)
    kper = ops.roll(kpe, QK_NOPE)
    qs, ks, vs = [], [], []
    for h in range(MLA_HEADS):
        qh = _slot_norm(ops.mm(cqn, wq[h]), qg, QK_DIM)
        qs.append(_rope(ops, qh, cos_t, sin_a, sin_b))
        kh = _slot_norm(ops.mm(ckvn, wk[h]) + kper, kg, QK_DIM)
        ks.append(_rope(ops, kh, cos_t, sin_a, sin_b))
        vs.append(ops.mm(ckvn, wv[h]))
    return qs, ks, vs


def _outproj(ops, x, ya, yb, o, mog, woa, wob, woc):
    yc = jnp.concatenate([_slot_norm(o[h], mog[h], V_DIM) for h in range(MLA_HEADS)], axis=1)
    return x + ops.mm(ya, woa) + ops.mm(yb, wob) + ops.mm(yc, woc)


def _lower_bounds(r0, r1, r2, r3):
    mx = jnp.maximum(jnp.maximum(r0, r1), jnp.maximum(r2, r3))
    e0, e1, e2, e3 = jnp.exp(r0 - mx), jnp.exp(r1 - mx), jnp.exp(r2 - mx), jnp.exp(r3 - mx)
    inv = 1.0 / (e0 + e1 + e2 + e3)
    s1, s2, s3 = e1 * inv, e2 * inv, e3 * inv
    return jnp.zeros_like(r0), s1, s1 + s2, s1 + s2 + s3


def _cp(sem):
    return pltpu.CompilerParams(dimension_semantics=sem, vmem_limit_bytes=VMEM_LIMIT)


def _rows(tm, n):
    return pl.BlockSpec((tm, n), lambda i: (i, 0))


def _full(a):
    nd = len(a.shape)
    return pl.BlockSpec(a.shape, lambda *_: (0,) * nd, pipeline_mode=pl.Buffered(1))


def _sds(shape, dtype=F32):
    return jax.ShapeDtypeStruct(shape, dtype)


def _acc(ref, val, first):
    @pl.when(first)
    def _():
        ref[...] = val

    @pl.when(jnp.logical_not(first))
    def _():
        ref[...] = ref[...] + val


def _f32(ref):
    return ref[...].astype(F32)


MESH = pl.DeviceIdType.MESH
ANY = pl.BlockSpec(memory_space=pl.ANY)


class Rider:
    def __init__(self, arrays, out_shapes, aliases, sems, start, finish):
        self.arrays, self.out_shapes, self.aliases, self.sems = list(arrays), list(out_shapes), dict(aliases), list(sems)
        self.start, self.finish = start, finish


def run_rider(rider, name):
    n_in, n_out = len(rider.arrays), len(rider.out_shapes)

    def body(*refs):
        ins, outs, sems = refs[:n_in], refs[n_in:n_in + n_out], refs[n_in + n_out:]
        rider.start(ins, outs, sems)
        rider.finish(ins, outs, sems)

    return pl.pallas_call(
        body, name=name, in_specs=[ANY] * n_in, out_specs=[ANY] * n_out, out_shape=rider.out_shapes,
        input_output_aliases=rider.aliases, scratch_shapes=rider.sems,
    )(*rider.arrays)


def _merge_riders(riders):
    bounds, a0, o0, s0 = [], 0, 0, 0
    for r in riders:
        bounds.append((a0, o0, s0))
        a0, o0, s0 = a0 + len(r.arrays), o0 + len(r.out_shapes), s0 + len(r.sems)

    def part(k, ins, outs, sems):
        a, o, s = bounds[k]
        r = riders[k]
        return ins[a:a + len(r.arrays)], outs[o:o + len(r.out_shapes)], sems[s:s + len(r.sems)]

    return Rider(
        [x for r in riders for x in r.arrays], [x for r in riders for x in r.out_shapes],
        {bounds[k][0] + i: bounds[k][1] + o for k, r in enumerate(riders) for i, o in r.aliases.items()},
        [x for r in riders for x in r.sems],
        lambda *refs: [r.start(*part(k, *refs)) for k, r in enumerate(riders)],
        lambda *refs: [r.finish(*part(k, *refs)) for k, r in enumerate(riders)])


def _ride(compute, riders, *, name, grid, in_specs, out_specs, out_shape, operands, scratch_shapes=(), sem=None):
    single = not isinstance(out_shape, (list, tuple))
    if single:
        out_specs, out_shape = [out_specs], [out_shape]
    live = [r for r in riders if r is not None]
    if not live:
        res = pl.pallas_call(compute, name=name, grid=grid, in_specs=in_specs, out_specs=out_specs, out_shape=out_shape,
                             scratch_shapes=list(scratch_shapes), compiler_params=_cp(sem))(*operands)
        return (res[0] if single else res), [None] * len(riders)
    rider = live[0] if len(live) == 1 else _merge_riders(live)
    n_in, n_out, n_s = len(in_specs), len(out_specs), len(scratch_shapes)
    r_in, r_out = len(rider.arrays), len(rider.out_shapes)

    def body(*refs):
        ins, rins = refs[:n_in], refs[n_in:n_in + r_in]
        outs = refs[n_in + r_in:n_in + r_in + n_out]
        routs = refs[n_in + r_in + n_out:n_in + r_in + n_out + r_out]
        scr = refs[n_in + r_in + n_out + r_out:n_in + r_in + n_out + r_out + n_s]
        rsems = refs[n_in + r_in + n_out + r_out + n_s:]
        first = functools.reduce(jnp.logical_and, [pl.program_id(a) == 0 for a in range(len(grid))])
        last = functools.reduce(jnp.logical_and, [pl.program_id(a) == grid[a] - 1 for a in range(len(grid))])

        @pl.when(first)
        def _():
            rider.start(rins, routs, rsems)

        compute(*ins, *outs, *scr)

        @pl.when(last)
        def _():
            rider.finish(rins, routs, rsems)

    res = pl.pallas_call(
        body, name=name, grid=grid, in_specs=list(in_specs) + [ANY] * r_in, out_specs=list(out_specs) + [ANY] * r_out,
        out_shape=list(out_shape) + rider.out_shapes,
        input_output_aliases={n_in + k: n_out + v for k, v in rider.aliases.items()},
        scratch_shapes=list(scratch_shapes) + rider.sems, compiler_params=_cp(("arbitrary",) * len(grid)),
    )(*operands, *rider.arrays)
    main, rest, per_rider = res[:n_out], list(res[n_out:]), []
    for r in riders:
        per_rider.append(None if r is None else [rest.pop(0) for _ in r.out_shapes])
    return (main[0] if single else main), per_rider


def inproj_fwd(x, g1, wa, wb, wc):
    s, d = x.shape

    def body(x_ref, g_ref, wa_ref, wb_ref, wc_ref, pa_ref, pb_ref, pc_ref):
        pa, pb, pc = _inproj(PLAIN, x_ref[...], g_ref[...], wa_ref[...], wb_ref[...], wc_ref[...])
        pa_ref[...] = pa
        pb_ref[...] = pb
        pc_ref[...] = pc

    return pl.pallas_call(
        body, name="inproj_fwd", grid=(s // TM,),
        in_specs=[_rows(TM, d), _full(g1), _full(wa), _full(wb), _full(wc)],
        out_specs=[_rows(TM, wa.shape[1]), _rows(TM, wb.shape[1]), _rows(TM, wc.shape[1])],
        out_shape=[_sds((s, wa.shape[1])), _sds((s, wb.shape[1])), _sds((s, wc.shape[1]))],
        compiler_params=_cp(("parallel",)),
    )(x, g1, wa, wb, wc)


def inproj_bwd(x, g1, wa, wb, wc, dpa, dpb, dpc, dres):
    s, d = x.shape

    def body(x_ref, g_ref, wa_ref, wb_ref, wc_ref, dpa_ref, dpb_ref, dpc_ref, dres_ref,
             dx_ref, dg_ref, dwa_ref, dwb_ref, dwc_ref):
        first = pl.program_id(0) == 0
        _, vjp = jax.vjp(functools.partial(_inproj, AD), x_ref[...], g_ref[...],
                         _f32(wa_ref), _f32(wb_ref), _f32(wc_ref))
        dx, dg, dwa, dwb, dwc = vjp((dpa_ref[...], dpb_ref[...], dpc_ref[...]))
        dx_ref[...] = dx + dres_ref[...]
        _acc(dg_ref, dg, first)
        _acc(dwa_ref, dwa, first)
        _acc(dwb_ref, dwb, first)
        _acc(dwc_ref, dwc, first)

    return pl.pallas_call(
        body, name="inproj_bwd", grid=(s // TM,),
        in_specs=[_rows(TM, d), _full(g1), _full(wa), _full(wb), _full(wc),
                  _rows(TM, wa.shape[1]), _rows(TM, wb.shape[1]), _rows(TM, wc.shape[1]), _rows(TM, d)],
        out_specs=[_rows(TM, d), _full(g1), _full(wa), _full(wb), _full(wc)],
        out_shape=[_sds((s, d)), _sds(g1.shape), _sds(wa.shape), _sds(wb.shape), _sds(wc.shape)],
        compiler_params=_cp(("arbitrary",)),
    )(x, g1, wa, wb, wc, dpa, dpb, dpc, dres)


def gm_fwd(pa, vg, ws4, bs, og):
    s = pa.shape[0]
    w = pa.shape[1] // 2

    def body(pa_ref, vg_ref, ws_ref, bs_ref, og_ref, ya_ref):
        bsl = [bs_ref[h] for h in range(4)]
        for j in range(GM_CHUNKS):
            rows = slice(j * CHUNK, (j + 1) * CHUNK)
            ya_ref[rows, :] = _gm_chunk(PLAIN, pa_ref[rows, 0:w], pa_ref[rows, w:2 * w], vg_ref[...], ws_ref[...], bsl,
                                        og_ref[...])

    tm = GM_CHUNKS * CHUNK
    return pl.pallas_call(
        body, name="gm_fwd", grid=(s // tm,),
        in_specs=[_rows(tm, 2 * w), _full(vg), _full(ws4), _full(bs), _full(og)],
        out_specs=_rows(tm, w), out_shape=_sds((s, w)),
        compiler_params=_cp(("parallel",)),
    )(pa, vg, ws4, bs, og)


def gm_bwd(pa, vg, ws4, bs, og, dya):
    s = pa.shape[0]
    w = pa.shape[1] // 2

    def body(pa_ref, vg_ref, ws_ref, bs_ref, og_ref, dya_ref, dpa_ref, dvg_ref, dws_ref, dbs_ref, dog_ref):
        first = pl.program_id(0) == 0
        bsl = [bs_ref[h] for h in range(4)]
        tot = None
        for j in range(GM_CHUNKS):
            rows = slice(j * CHUNK, (j + 1) * CHUNK)
            _, vjp = jax.vjp(functools.partial(_gm_chunk, AD), pa_ref[rows, 0:w], pa_ref[rows, w:2 * w],
                             vg_ref[...], ws_ref[...], bsl, og_ref[...])
            du, dv, *dws = vjp(dya_ref[rows, :])
            dpa_ref[rows, 0:w] = du
            dpa_ref[rows, w:2 * w] = dv
            tot = dws if tot is None else jax.tree.map(jnp.add, tot, dws)
        dvg, dws, dbs, dog = tot
        _acc(dvg_ref, dvg, first)
        _acc(dws_ref, dws, first)
        _acc(dog_ref, dog, first)
        for h in range(4):
            _acc(dbs_ref.at[h], dbs[h], first)

    tm = GM_CHUNKS * CHUNK
    return pl.pallas_call(
        body, name="gm_bwd", grid=(s // tm,),
        in_specs=[_rows(tm, 2 * w), _full(vg), _full(ws4), _full(bs), _full(og), _rows(tm, w)],
        out_specs=[_rows(tm, 2 * w), _full(vg), _full(ws4), _full(bs), _full(og)],
        out_shape=[_sds((s, 2 * w)), _sds(vg.shape), _sds(ws4.shape), _sds(bs.shape), _sds(og.shape)],
        compiler_params=_cp(("arbitrary",)),
    )(pa, vg, ws4, bs, og, dya)


def hg_fwd(pb, lb, og, riders=()):
    s = pb.shape[0]
    w = pb.shape[1] // 4
    tm = HG_CHUNKS * HG_CHUNK
    st_shape = (w // HEAD64, HEAD64, HEAD64)

    def body(pb_ref, lb_ref, og_ref, yb_ref, states_ref, st_ref):
        @pl.when(pl.program_id(0) == 0)
        def _():
            st_ref[...] = jnp.zeros_like(st_ref)

        st = st_ref[...]
        for j in range(HG_CHUNKS):
            rows = slice(j * HG_CHUNK, (j + 1) * HG_CHUNK)
            states_ref[j] = st
            st, y = _hg_chunk(PLAIN, st, pb_ref[rows, 0:w], pb_ref[rows, w:2 * w], pb_ref[rows, 2 * w:3 * w],
                              pb_ref[rows, 3 * w:4 * w], lb_ref[...], og_ref[...])
            yb_ref[rows, :] = y
        st_ref[...] = st

    return _ride(
        body, riders, name="hg_fwd", grid=(s // tm,),
        in_specs=[_rows(tm, 4 * w), _full(lb), _full(og)],
        out_specs=[_rows(tm, w), pl.BlockSpec((HG_CHUNKS,) + st_shape, lambda i: (i, 0, 0, 0))],
        out_shape=[_sds((s, w)), _sds((s // HG_CHUNK,) + st_shape)],
        scratch_shapes=[pltpu.VMEM(st_shape, F32)],
        operands=(pb, lb, og), sem=("arbitrary",))


def hg_bwd(pb, lb, og, states, dyb, riders=()):
    s = pb.shape[0]
    w = pb.shape[1] // 4
    tm = HG_CHUNKS * HG_CHUNK
    nc = s // tm
    st_shape = (w // HEAD64, HEAD64, HEAD64)

    def body(pb_ref, lb_ref, og_ref, states_ref, dyb_ref, dpb_ref, dlb_ref, dog_ref, dst_ref):
        first = pl.program_id(0) == 0

        @pl.when(first)
        def _():
            dst_ref[...] = jnp.zeros_like(dst_ref)

        dst, dlb, dog = dst_ref[...], None, None
        for j in reversed(range(HG_CHUNKS)):
            rows = slice(j * HG_CHUNK, (j + 1) * HG_CHUNK)
            _, vjp = jax.vjp(functools.partial(_hg_chunk, AD), states_ref[j], pb_ref[rows, 0:w], pb_ref[rows, w:2 * w],
                             pb_ref[rows, 2 * w:3 * w], pb_ref[rows, 3 * w:4 * w], lb_ref[...], og_ref[...])
            dst, dq, df, di, dg, dlb_j, dog_j = vjp((dst, dyb_ref[rows, :]))
            dpb_ref[rows, 0:w] = dq
            dpb_ref[rows, w:2 * w] = df
            dpb_ref[rows, 2 * w:3 * w] = di
            dpb_ref[rows, 3 * w:4 * w] = dg
            dlb = dlb_j if dlb is None else dlb + dlb_j
            dog = dog_j if dog is None else dog + dog_j
        dst_ref[...] = dst
        _acc(dlb_ref, dlb, first)
        _acc(dog_ref, dog, first)

    rev = lambda i: (nc - 1 - i, 0)
    return _ride(
        body, riders, name="hg_bwd", grid=(nc,),
        in_specs=[pl.BlockSpec((tm, 4 * w), rev), _full(lb), _full(og),
                  pl.BlockSpec((HG_CHUNKS,) + st_shape, lambda i: (nc - 1 - i, 0, 0, 0)), pl.BlockSpec((tm, w), rev)],
        out_specs=[pl.BlockSpec((tm, 4 * w), rev), _full(lb), _full(og)],
        out_shape=[_sds((s, 4 * w)), _sds(lb.shape), _sds(og.shape)],
        scratch_shapes=[pltpu.VMEM(st_shape, F32)],
        operands=(pb, lb, og, states, dyb), sem=("arbitrary",))


def lower_bounds_fwd(hlb):
    def body(h_ref, o_ref):
        outs = _lower_bounds(*[h_ref[pl.ds(i, 1), :] for i in range(DEPTH)])
        for i in range(DEPTH):
            o_ref[pl.ds(i, 1), :] = outs[i]

    return pl.pallas_call(body, name="lower_bounds_fwd", out_shape=_sds(hlb.shape))(hlb)


def lower_bounds_bwd(hlb, dlbs):
    def body(h_ref, d_ref, o_ref):
        _, vjp = jax.vjp(_lower_bounds, *[h_ref[pl.ds(i, 1), :] for i in range(DEPTH)])
        outs = vjp(tuple(d_ref[pl.ds(i, 1), :] for i in range(DEPTH)))
        for i in range(DEPTH):
            o_ref[pl.ds(i, 1), :] = outs[i]

    return pl.pallas_call(body, name="lower_bounds_bwd", out_shape=_sds(hlb.shape))(hlb, dlbs)


def _mla_pre_args(pc_ref, cos_ref, sa_ref, sb_ref, qag_ref, kvag_ref, qg_ref, kg_ref, wq_ref, wk_ref, wv_ref, cast):
    sl = lambda h: slice(h * SLOT, (h + 1) * SLOT)
    ld = (lambda r, h: r[:, sl(h)].astype(F32)) if cast else (lambda r, h: r[:, sl(h)])
    diff = (pc_ref[:, 0:Q_LORA], pc_ref[:, Q_LORA:Q_LORA + KV_LORA], pc_ref[:, Q_LORA + KV_LORA:Q_LORA + 2 * KV_LORA],
            qag_ref[...], kvag_ref[...], qg_ref[...], kg_ref[...],
            [ld(wq_ref, h) for h in range(MLA_HEADS)], [ld(wk_ref, h) for h in range(MLA_HEADS)],
            [ld(wv_ref, h) for h in range(MLA_HEADS)])
    tables = (cos_ref[...], sa_ref[...], sb_ref[...])
    return diff, tables


def _mla_pre_fn(ops, tables, cq, ckv, kpe, qag, kvag, qg, kg, wq, wk, wv):
    return _mla_pre(ops, cq, ckv, kpe, *tables, qag, kvag, qg, kg, wq, wk, wv)


def mla_pre_fwd(pc, cos_t, sin_a, sin_b, qag, kvag, qg, kg, wq, wk, wv):
    s = pc.shape[0]
    hw = MLA_HEADS * SLOT

    def body(pc_ref, cos_ref, sa_ref, sb_ref, qag_ref, kvag_ref, qg_ref, kg_ref, wq_ref, wk_ref, wv_ref,
             q_ref, k_ref, v_ref):
        diff, tables = _mla_pre_args(pc_ref, cos_ref, sa_ref, sb_ref, qag_ref, kvag_ref, qg_ref, kg_ref,
                                     wq_ref, wk_ref, wv_ref, False)
        qs, ks, vs = _mla_pre_fn(PLAIN, tables, *diff)
        ones_lane = (lax.broadcasted_iota(jnp.int32, (1, SLOT), 1) == V_DIM).astype(F32)
        for h in range(MLA_HEADS):
            q_ref[:, h * SLOT:(h + 1) * SLOT] = qs[h].astype(q_ref.dtype)
            k_ref[:, h * SLOT:(h + 1) * SLOT] = ks[h].astype(k_ref.dtype)
            v_ref[:, h * SLOT:(h + 1) * SLOT] = (vs[h] + ones_lane).astype(v_ref.dtype)

    return pl.pallas_call(
        body, name="mla_pre_fwd", grid=(s // TM,),
        in_specs=[_rows(TM, pc.shape[1]), _rows(TM, SLOT), _rows(TM, SLOT), _rows(TM, SLOT),
                  _full(qag), _full(kvag), _full(qg), _full(kg), _full(wq), _full(wk), _full(wv)],
        out_specs=[_rows(TM, hw)] * 3, out_shape=[_sds((s, hw), MXU_DTYPE)] * 3,
        compiler_params=_cp(("parallel",)),
    )(pc, cos_t, sin_a, sin_b, qag, kvag, qg, kg, wq, wk, wv)


def mla_pre_bwd(pc, cos_t, sin_a, sin_b, qag, kvag, qg, kg, wq, wk, wv, dq, dk, dv):
    s = pc.shape[0]
    hw = MLA_HEADS * SLOT

    def body(pc_ref, cos_ref, sa_ref, sb_ref, qag_ref, kvag_ref, qg_ref, kg_ref, wq_ref, wk_ref, wv_ref,
             dq_ref, dk_ref, dv_ref, dpc_ref, dqag_ref, dkvag_ref, dqg_ref, dkg_ref, dwq_ref, dwk_ref, dwv_ref):
        first = pl.program_id(0) == 0
        diff, tables = _mla_pre_args(pc_ref, cos_ref, sa_ref, sb_ref, qag_ref, kvag_ref, qg_ref, kg_ref,
                                     wq_ref, wk_ref, wv_ref, True)
        _, vjp = jax.vjp(functools.partial(_mla_pre_fn, AD, tables), *diff)
        sl = lambda h: slice(h * SLOT, (h + 1) * SLOT)
        cot = ([dq_ref[:, sl(h)] for h in range(MLA_HEADS)], [dk_ref[:, sl(h)] for h in range(MLA_HEADS)],
               [dv_ref[:, sl(h)] for h in range(MLA_HEADS)])
        dcq, dckv, dkpe, dqag, dkvag, dqg, dkg, dwq, dwk, dwv = vjp(cot)
        dpc_ref[:, 0:Q_LORA] = dcq
        dpc_ref[:, Q_LORA:Q_LORA + KV_LORA] = dckv
        dpc_ref[:, Q_LORA + KV_LORA:Q_LORA + 2 * KV_LORA] = dkpe
        _acc(dqag_ref, dqag, first)
        _acc(dkvag_ref, dkvag, first)
        _acc(dqg_ref, dqg, first)
        _acc(dkg_ref, dkg, first)
        for h in range(MLA_HEADS):
            _acc(dwq_ref.at[:, sl(h)], dwq[h], first)
            _acc(dwk_ref.at[:, sl(h)], dwk[h], first)
            _acc(dwv_ref.at[:, sl(h)], dwv[h], first)

    return pl.pallas_call(
        body, name="mla_pre_bwd", grid=(s // TM,),
        in_specs=[_rows(TM, pc.shape[1]), _rows(TM, SLOT), _rows(TM, SLOT), _rows(TM, SLOT),
                  _full(qag), _full(kvag), _full(qg), _full(kg), _full(wq), _full(wk), _full(wv),
                  _rows(TM, hw), _rows(TM, hw), _rows(TM, hw)],
        out_specs=[_rows(TM, pc.shape[1]), _full(qag), _full(kvag), _full(qg), _full(kg),
                   _full(wq), _full(wk), _full(wv)],
        out_shape=[_sds(pc.shape), _sds(qag.shape), _sds(kvag.shape), _sds(qg.shape), _sds(kg.shape),
                   _sds(wq.shape), _sds(wk.shape), _sds(wv.shape)],
        compiler_params=_cp(("arbitrary",)),
    )(pc, cos_t, sin_a, sin_b, qag, kvag, qg, kg, wq, wk, wv, dq, dk, dv)


ATT_SCALE = QK_DIM ** -0.5
NEG_BIG = -1e30


def attn_fwd(q, k, v, riders=()):
    s = q.shape[0]
    nq = s // TQ
    hp = ATT_HEADS_PER_STEP
    sl = lambda j: slice(j * SLOT, (j + 1) * SLOT)

    wide = ATT_WIDE // TQ

    def body(q_ref, k_ref, v_ref, o_ref, lse_ref):
        qi = pl.program_id(1)
        lane = lax.broadcasted_iota(jnp.int32, (1, SLOT), 1)
        qs = [q_ref[:, sl(j)] for j in range(hp)]

        def step(ki, carry, n_tiles, masked):
            rk = pl.ds(pl.multiple_of(ki * TQ, TQ), n_tiles * TQ)
            if masked:
                row = lax.broadcasted_iota(jnp.int32, (TQ, n_tiles * TQ), 0) + (n_tiles - 1) * TQ
                col = lax.broadcasted_iota(jnp.int32, (TQ, n_tiles * TQ), 1)
            out = []
            for j in range(hp):
                m, acc = carry[j]
                sc = _dot(qs[j], k_ref[rk, sl(j)], NT) * ATT_SCALE
                if masked:
                    sc = jnp.where(row >= col, sc, NEG_BIG)
                m_new = jnp.maximum(m, jnp.max(sc, axis=-1, keepdims=True))
                acc = jnp.exp(m - m_new) * acc + _dot(jnp.exp(sc - m_new), v_ref[rk, sl(j)], NN)
                out.append((m_new, acc))
            return tuple(out)

        def tail_single(cr):
            cr = lax.fori_loop(n_wide * wide, qi, lambda ki, c: step(ki, c, 1, False), cr)
            return step(qi, cr, 1, True)

        n_wide = qi // wide
        init = tuple((jnp.full((TQ, 1), NEG_BIG, F32), jnp.zeros((TQ, SLOT), F32)) for _ in range(hp))
        carry = lax.fori_loop(0, n_wide, lambda kw, cr: step(kw * wide, cr, wide, False), init)
        carry = lax.cond(qi % wide == wide - 1, lambda cr: step(qi - (wide - 1), cr, wide, True), tail_single, carry)
        for j in range(hp):
            m, acc = carry[j]
            l = jnp.sum(jnp.where(lane == V_DIM, acc, 0.0), axis=-1, keepdims=True)
            o_ref[:, sl(j)] = jnp.where(lane < V_DIM, acc / l, 0.0)
            lse_ref[j] = m + jnp.log(l)

    head_col = pl.BlockSpec((s, hp * SLOT), lambda g, i: (0, g))
    tile = pl.BlockSpec((TQ, hp * SLOT), lambda g, i: (i, g))
    return _ride(
        body, riders, name="attn_fwd", grid=(MLA_HEADS // hp, nq),
        in_specs=[tile, head_col, head_col],
        out_specs=[tile, pl.BlockSpec((hp, TQ, 1), lambda g, i: (g, i, 0))],
        out_shape=[_sds((s, MLA_HEADS * SLOT)), _sds((MLA_HEADS, s, 1))],
        operands=(q, k, v), sem=("parallel", "parallel"))


def attn_bwd(q, k, v, o, do, lse, riders=()):
    s = q.shape[0]
    nq = s // TQ
    hp = ATT_HEADS_PER_STEP
    sl = lambda j: slice(j * SLOT, (j + 1) * SLOT)
    wide = ATT_WIDE // TQ

    def body(q_ref, k_ref, v_ref, o_ref, do_ref, lse_ref, dq_ref, dk_ref, dv_ref, delta_ref):
        ki = pl.program_id(1)

        @pl.when(ki == 0)
        def _():
            dq_ref[...] = jnp.zeros_like(dq_ref)

            def prep(i, c):
                rows = pl.ds(pl.multiple_of(i * TQ, TQ), TQ)
                for j in range(hp):
                    delta_ref[j, rows, :] = jnp.sum(do_ref[rows, sl(j)] * o_ref[rows, sl(j)], axis=-1, keepdims=True)
                return c

            lax.fori_loop(0, nq, prep, 0)

        kks = [k_ref[:, sl(j)] for j in range(hp)]
        vvs = [v_ref[:, sl(j)] for j in range(hp)]

        def step(qi, carry, n_tiles, masked):
            rq = pl.ds(pl.multiple_of(qi * TQ, TQ), n_tiles * TQ)
            if masked:
                row = lax.broadcasted_iota(jnp.int32, (n_tiles * TQ, TQ), 0)
                col = lax.broadcasted_iota(jnp.int32, (n_tiles * TQ, TQ), 1)
            out = []
            for j in range(hp):
                dk, dv = carry[j]
                qq = q_ref[rq, sl(j)]
                dd = do_ref[rq, sl(j)]
                sc = _dot(qq, kks[j], NT) * ATT_SCALE
                if masked:
                    sc = jnp.where(row >= col, sc, NEG_BIG)
                p = jnp.exp(sc - lse_ref[j, rq, :])
                dv = dv + _dot(p, dd, TN)
                ds = p * (_dot(dd, vvs[j], NT) - delta_ref[j, rq, :]) * ATT_SCALE
                dk = dk + _dot(ds, qq, TN)
                dq_ref[rq, sl(j)] = dq_ref[rq, sl(j)] + _dot(ds, kks[j], NN)
                out.append((dk, dv))
            return tuple(out)

        def head_single(cr):
            cr = step(ki, cr, 1, True)
            return lax.fori_loop(ki + 1, first_wide * wide, lambda qi, c: step(qi, c, 1, False), cr)

        zero = jnp.zeros((TQ, SLOT), F32)
        first_wide = (ki + wide) // wide
        carry = tuple((zero, zero) for _ in range(hp))
        carry = lax.cond(ki % wide == 0, lambda cr: step(ki, cr, wide, True), head_single, carry)
        carry = lax.fori_loop(first_wide, nq // wide, lambda qw, cr: step(qw * wide, cr, wide, False), carry)
        for j in range(hp):
            dk_ref[:, sl(j)] = carry[j][0]
            dv_ref[:, sl(j)] = carry[j][1]

    head_col = pl.BlockSpec((s, hp * SLOT), lambda g, i: (0, g))
    tile = pl.BlockSpec((TQ, hp * SLOT), lambda g, i: (i, g))
    return _ride(
        body, riders, name="attn_bwd", grid=(MLA_HEADS // hp, nq),
        in_specs=[head_col, tile, tile, head_col, head_col, pl.BlockSpec((hp, s, 1), lambda g, i: (g, 0, 0))],
        out_specs=[head_col, tile, tile],
        out_shape=[_sds((s, MLA_HEADS * SLOT))] * 3,
        scratch_shapes=[pltpu.VMEM((hp, s, 1), F32)],
        operands=(q, k, v, o, do, lse), sem=("arbitrary", "arbitrary"))


def _outproj_args(ya_ref, yb_ref, o_ref, mog_ref, woa_ref, wob_ref, woc_ref, cast):
    sl = lambda h: slice(h * SLOT, (h + 1) * SLOT)
    ldw = (lambda r: r[...].astype(F32)) if cast else (lambda r: r[...])
    return (ya_ref[...], yb_ref[...], [o_ref[:, sl(h)] for h in range(MLA_HEADS)],
            [mog_ref[:, sl(h)] for h in range(MLA_HEADS)], ldw(woa_ref), ldw(wob_ref), ldw(woc_ref))


def outproj_fwd(x, ya, yb, o, mog, woa, wob, woc):
    s, d = x.shape

    def body(x_ref, ya_ref, yb_ref, o_ref, mog_ref, woa_ref, wob_ref, woc_ref, x1_ref):
        x1_ref[...] = _outproj(PLAIN, x_ref[...], *_outproj_args(ya_ref, yb_ref, o_ref, mog_ref, woa_ref, wob_ref,
                                                                  woc_ref, False))

    return pl.pallas_call(
        body, name="outproj_fwd", grid=(s // TM,),
        in_specs=[_rows(TM, d), _rows(TM, ya.shape[1]), _rows(TM, yb.shape[1]), _rows(TM, o.shape[1]),
                  _full(mog), _full(woa), _full(wob), _full(woc)],
        out_specs=_rows(TM, d), out_shape=_sds((s, d)),
        compiler_params=_cp(("parallel",)),
    )(x, ya, yb, o, mog, woa, wob, woc)


def outproj_bwd(ya, yb, o, mog, woa, wob, woc, dx2, dx1p, riders=()):
    s, d = dx2.shape
    npart = dx1p.shape[0]

    def body(ya_ref, yb_ref, o_ref, mog_ref, woa_ref, wob_ref, woc_ref, dx2_ref, dx1p_ref,
             dx1_ref, dya_ref, dyb_ref, do_ref, dmog_ref, dwoa_ref, dwob_ref, dwoc_ref):
        first = pl.program_id(0) == 0
        sl = lambda h: slice(h * SLOT, (h + 1) * SLOT)
        dx1 = dx2_ref[...]
        for p in range(npart):
            dx1 = dx1 + dx1p_ref[p]
        dx1_ref[...] = dx1
        args = _outproj_args(ya_ref, yb_ref, o_ref, mog_ref, woa_ref, wob_ref, woc_ref, True)
        _, vjp = jax.vjp(lambda *a: _outproj(AD, jnp.zeros_like(dx1), *a), *args)
        dya, dyb, do, dmog, dwoa, dwob, dwoc = vjp(dx1)
        dya_ref[...] = dya
        dyb_ref[...] = dyb
        _acc(dwoa_ref, dwoa, first)
        _acc(dwob_ref, dwob, first)
        _acc(dwoc_ref, dwoc, first)
        for h in range(MLA_HEADS):
            do_ref[:, sl(h)] = do[h]
            _acc(dmog_ref.at[:, sl(h)], dmog[h], first)

    return _ride(
        body, riders, name="outproj_bwd", grid=(s // TM,),
        in_specs=[_rows(TM, ya.shape[1]), _rows(TM, yb.shape[1]), _rows(TM, o.shape[1]),
                  _full(mog), _full(woa), _full(wob), _full(woc), _rows(TM, d),
                  pl.BlockSpec((npart, TM, d), lambda i: (0, i, 0))],
        out_specs=[_rows(TM, d), _rows(TM, ya.shape[1]), _rows(TM, yb.shape[1]), _rows(TM, o.shape[1]),
                   _full(mog), _full(woa), _full(wob), _full(woc)],
        out_shape=[_sds((s, d)), _sds(ya.shape), _sds(yb.shape), _sds(o.shape),
                   _sds(mog.shape), _sds(woa.shape), _sds(wob.shape), _sds(woc.shape)],
        operands=(ya, yb, o, mog, woa, wob, woc, dx2, dx1p), sem=("arbitrary",))


def ffn_fwd(x1, g2, w1, w2, riders=()):
    s, d = x1.shape
    npart, _, fs = w1.shape

    def body(x1_ref, g_ref, w1_ref, w2_ref, x2_ref, r_ref):
        p = pl.program_id(1)
        x1v = x1_ref[...]
        r = jnp.maximum(PLAIN.mm(_rms(x1v, g_ref[...]), w1_ref[...]), 0.0)
        r_ref[...] = r.astype(r_ref.dtype)
        part = PLAIN.mm(r * r, w2_ref[...])

        @pl.when(p == 0)
        def _():
            x2_ref[...] = x1v + part

        @pl.when(p != 0)
        def _():
            x2_ref[...] = x2_ref[...] + part

    tm = min(2 * TM_FFN, s)
    return _ride(
        body, riders, name="ffn_fwd", grid=(s // tm, npart),
        in_specs=[pl.BlockSpec((tm, d), lambda i, p: (i, 0)), pl.BlockSpec(g2.shape, lambda i, p: (0, 0)),
                  pl.BlockSpec((None, d, fs), lambda i, p: (p, 0, 0)), pl.BlockSpec((None, fs, d), lambda i, p: (p, 0, 0))],
        out_specs=[pl.BlockSpec((tm, d), lambda i, p: (i, 0)), pl.BlockSpec((tm, fs), lambda i, p: (i, p))],
        out_shape=[_sds((s, d)), _sds((s, npart * fs), MXU_DTYPE)],
        operands=(x1, g2, w1, w2), sem=("parallel", "arbitrary"))


def ffn_bwd(x1, g2, w1, w2, r, dx2, riders=()):
    s, d = x1.shape
    npart, _, fs = w1.shape

    def body(x1_ref, g_ref, w1_ref, w2_ref, r_ref, dx2_ref, dx1p_ref, dg_ref, dw1_ref, dw2_ref):
        p = pl.program_id(0)
        i = pl.program_id(1)
        h2, vjp_norm = jax.vjp(_rms, x1_ref[...], g_ref[...])
        rr = r_ref[...].astype(F32)
        dy = dx2_ref[...]
        da = _dot(dy, w2_ref[...], NT) * (2.0 * rr)
        dx1, dg = vjp_norm(_dot(da, w1_ref[...], NT))
        dx1p_ref[...] = dx1
        _acc(dg_ref, dg, (p == 0) & (i == 0))
        _acc(dw1_ref, _dot(h2, da, TN), i == 0)
        _acc(dw2_ref, _dot(rr * rr, dy, TN), i == 0)

    tm = TM_FFN
    return _ride(
        body, riders, name="ffn_bwd", grid=(npart, s // tm),
        in_specs=[pl.BlockSpec((tm, d), lambda p, i: (i, 0)), pl.BlockSpec(g2.shape, lambda p, i: (0, 0)),
                  pl.BlockSpec((None, d, fs), lambda p, i: (p, 0, 0)), pl.BlockSpec((None, fs, d), lambda p, i: (p, 0, 0)),
                  pl.BlockSpec((tm, fs), lambda p, i: (i, p)), pl.BlockSpec((tm, d), lambda p, i: (i, 0))],
        out_specs=[pl.BlockSpec((None, tm, d), lambda p, i: (p, i, 0)), pl.BlockSpec(g2.shape, lambda p, i: (0, 0)),
                   pl.BlockSpec((None, d, fs), lambda p, i: (p, 0, 0)), pl.BlockSpec((None, fs, d), lambda p, i: (p, 0, 0))],
        out_shape=[_sds((npart, s, d)), _sds(g2.shape), _sds(w1.shape), _sds(w2.shape)],
        operands=(x1, g2, w1, w2, r, dx2), sem=("arbitrary", "arbitrary"))


def loss_head(y, target):
    s, d = y.shape

    def body(y_ref, t_ref, dy_ref, loss_ref):
        err = y_ref[...] - t_ref[...]
        dy_ref[...] = err * (1.0 / d)
        part = jnp.sum(jnp.sum(err * err, axis=-1, keepdims=True), axis=0, keepdims=True) * (0.5 / d)
        _acc(loss_ref, jnp.broadcast_to(part, loss_ref.shape), pl.program_id(0) == 0)

    return pl.pallas_call(
        body, name="loss_head", grid=(s // TM,),
        in_specs=[_rows(TM, d), _rows(TM, d)],
        out_specs=[_rows(TM, d), pl.BlockSpec((1, SLOT), lambda i: (0, 0))],
        out_shape=[_sds((s, d)), _sds((1, SLOT))],
        compiler_params=_cp(("arbitrary",)),
    )(y, target)


def _row_block(r):
    for b in (512, 256, 128, 64, 32, 16, 8):
        if r % b == 0:
            return b
    return r


def sum_cores(arrs, gots, half, me):
    n = len(arrs)

    def body(sp_ref, *refs):
        for i in range(n):
            a_ref, g_ref, wire_ref, own_ref = refs[i], refs[n + i], refs[2 * n + i], refs[3 * n + i]
            tot = a_ref[...] + g_ref[...]
            wire_ref[...] = tot.astype(wire_ref.dtype)

            @pl.when(pl.program_id(0) == sp_ref[1])
            def _(own_ref=own_ref, tot=tot):
                own_ref[...] = tot

    shapes = [a.shape[2:] for a in arrs]
    grid_spec = pltpu.PrefetchScalarGridSpec(
        num_scalar_prefetch=1, grid=(N_CHIPS,),
        in_specs=[pl.BlockSpec((None, None) + sh, lambda p, sp: (p, sp[0], 0, 0)) for sh in shapes]
        + [pl.BlockSpec((None,) + sh, lambda p, sp: (p, 0, 0)) for sh in shapes],
        out_specs=[pl.BlockSpec((None,) + sh, lambda p, sp: (p, 0, 0)) for sh in shapes]
        + [pl.BlockSpec(sh, lambda p, sp: (0, 0)) for sh in shapes])
    outs = pl.pallas_call(body, name="sum_cores", grid_spec=grid_spec,
                          out_shape=[_sds((N_CHIPS,) + sh, BF16) for sh in shapes] + [_sds(sh) for sh in shapes],
                          compiler_params=_cp(("arbitrary",)))(jnp.stack([half, me]).astype(jnp.int32), *arrs, *gots)
    return outs[:n], outs[n:]


SUM_STEPS = 4


def sum_chips(owns, recvs, half, places):
    n = len(owns)
    dests = [d for _, _, d in places if d is not None]

    def body(sp_ref, *refs):
        del sp_ref
        for i in range(n):
            own_ref, out_ref = refs[4 * i], refs[4 * n + len(dests) + i]
            r0, r1, r2 = (refs[4 * i + 1 + j][...].astype(F32) for j in range(3))
            out_ref[...] = ((own_ref[...] + r0) + r1) + r2

    in_specs, out_specs, operands, aliases = [], [], [], {}
    for i, (own, recv, (layer, _, dest)) in enumerate(zip(owns, recvs, places)):
        r, c = own.shape
        br = r // SUM_STEPS
        in_specs.append(pl.BlockSpec((br, c), lambda i, sp: (i, 0)))
        in_specs += [pl.BlockSpec((None, br, c), functools.partial(lambda i, sp, j: (j, i, 0), j=j)) for j in range(3)]
        out_specs.append(pl.BlockSpec((None, None, br, c), functools.partial(lambda i, sp, l: (l, sp[0], i, 0), l=layer)))
        operands += [own, recv, recv, recv]
        if dest is not None:
            aliases[1 + 4 * n + len(aliases)] = i
    grid_spec = pltpu.PrefetchScalarGridSpec(num_scalar_prefetch=1, grid=(SUM_STEPS,), in_specs=in_specs + [ANY] * len(dests),
                                             out_specs=out_specs)
    return pl.pallas_call(body, name="sum_chips", grid_spec=grid_spec, input_output_aliases=aliases,
                          out_shape=[_sds((nl, 2) + o.shape) for o, (_, nl, _) in zip(owns, places)],
                          compiler_params=_cp(("parallel",)))(half.reshape(1).astype(jnp.int32), *operands, *dests)


def adamw(w, g, m, v, name, riders=()):
    r, c = w.shape
    br = _row_block(r)
    c1 = 1.0 / (1.0 - ADAM_B1 ** ADAM_STEP)
    c2 = 1.0 / (1.0 - ADAM_B2 ** ADAM_STEP)

    def body(w_ref, g_ref, m_ref, v_ref, d_ref, nm_ref, nv_ref):
        gg = g_ref[...]
        nm = ADAM_B1 * m_ref[...] + (1.0 - ADAM_B1) * gg
        nv = ADAM_B2 * v_ref[...] + (1.0 - ADAM_B2) * (gg * gg)
        d_ref[...] = -ADAM_LR * ((nm * c1) / (jnp.sqrt(nv * c2) + ADAM_EPS) + ADAM_WD * w_ref[...])
        nm_ref[...] = nm
        nv_ref[...] = nv

    return _ride(body, riders, name=name, grid=(r // br,), in_specs=[_rows(br, c)] * 4, out_specs=[_rows(br, c)] * 3,
                 out_shape=[_sds((r, c))] * 3, operands=(w, g, m, v), sem=("parallel",))


def _place():
    x, y, c = lax.axis_index("x"), lax.axis_index("y"), lax.axis_index("c")
    chips = [(1 - x, y), (x, 1 - y), (1 - x, 1 - y)]
    return x, y, c, chips


def _remote(src, dst, send_sem, recv_sem, to):
    return pltpu.make_async_remote_copy(src_ref=src, dst_ref=dst, send_sem=send_sem, recv_sem=recv_sem,
                                        device_id=to, device_id_type=MESH)


def gather_rider(arrs):
    n = len(arrs)
    me_chip = 2 * lax.axis_index("x") + lax.axis_index("y")
    bufs = [lax.dynamic_update_index_in_dim(lax.empty((N_CHIPS,) + a.shape, a.dtype), a, me_chip, 0) for a in arrs]

    def plan(ins, outs, sems):
        send_sems, recv_sems = sems
        x, y, c, chips = _place()
        me = 2 * x + y
        half, other, sibling = pl.ds(2 * c, 2), pl.ds(2 - 2 * c, 2), (x, y, 1 - c)
        cp = lambda i, k, src, dst, to: _remote(src, dst, send_sems.at[i, k], recv_sems.at[i, k], to)
        pairs = [(i, j, cx, cy) for i in range(n) for j, (cx, cy) in enumerate(chips)]
        blk = lambda i, cx, cy, part: outs[i].at[2 * cx + cy, part]
        first = lambda: [cp(i, j, ins[i].at[half], outs[i].at[me, half], (cx, cy, c)) for i, j, cx, cy in pairs]
        landed = lambda: [cp(i, j, blk(i, cx, cy, half), blk(i, cx, cy, half), (cx, cy, c)) for i, j, cx, cy in pairs]
        passed = lambda: [cp(i, 3 + j, blk(i, cx, cy, half), blk(i, cx, cy, half), sibling) for i, j, cx, cy in pairs]
        from_sibling = lambda: [cp(i, 3 + j, blk(i, cx, cy, other), blk(i, cx, cy, other), sibling) for i, j, cx, cy in pairs]
        return first, landed, passed, from_sibling

    def start(ins, outs, sems):
        for cp in plan(ins, outs, sems)[0]():
            cp.start()

    def finish(ins, outs, sems):
        first, landed, passed, from_sibling = plan(ins, outs, sems)
        forwards = passed()
        for a, b in zip(landed(), forwards):
            a.wait_recv()
            b.start()
        for cp in from_sibling():
            cp.wait_recv()
        for cp in first() + forwards:
            cp.wait_send()

    return Rider(list(arrs) + bufs, [_sds((N_CHIPS,) + a.shape, a.dtype) for a in arrs], {n + i: i for i in range(n)},
                 [pltpu.SemaphoreType.DMA((n, 6)), pltpu.SemaphoreType.DMA((n, 6))], start, finish)


class Reducer:
    def __init__(self, arrs, places=None):
        self.a = list(arrs)
        self.n = len(self.a)
        self.places = places if places is not None else [(0, 1, None)] * self.n
        self.c = lax.axis_index("c")
        self.me = 2 * lax.axis_index("x") + lax.axis_index("y")

    def swap_rider(self):
        n = self.n

        def plan(ins, outs, sems):
            x, y, c, _ = _place()
            return [_remote(ins[i].at[p, 1 - c], outs[i].at[p], sems[0].at[i, p], sems[1].at[i, p], (x, y, 1 - c))
                    for i in range(n) for p in range(N_CHIPS)]

        return Rider(self.a, [_sds((N_CHIPS,) + a.shape[2:]) for a in self.a], {},
                     [pltpu.SemaphoreType.DMA((n, N_CHIPS)), pltpu.SemaphoreType.DMA((n, N_CHIPS))],
                     lambda *r: [cp.start() for cp in plan(*r)], lambda *r: [cp.wait() for cp in plan(*r)])

    def after_swap(self, got):
        self.wire, self.own = sum_cores(self.a, got, self.c, self.me)

    def scatter_rider(self):
        n = self.n

        def plan(ins, outs, sems):
            x, y, c, chips = _place()
            return [_remote(ins[i].at[2 * cx + cy], outs[i].at[j], sems[0].at[i, j], sems[1].at[i, j], (cx, cy, c))
                    for i in range(n) for j, (cx, cy) in enumerate(chips)]

        return Rider(self.wire, [_sds((3,) + w.shape[1:], w.dtype) for w in self.wire], {},
                     [pltpu.SemaphoreType.DMA((n, 3)), pltpu.SemaphoreType.DMA((n, 3))],
                     lambda *r: [cp.start() for cp in plan(*r)], lambda *r: [cp.wait() for cp in plan(*r)])

    def after_scatter(self, recv):
        if callable(self.places):
            self.places = self.places()
        self.full = sum_chips(self.own, recv, self.c, self.places)

    def share_rider(self):
        n = self.n
        layers = [layer for layer, _, _ in self.places]

        def plan(ins, outs, sems):
            x, y, c, _ = _place()
            return [_remote(ins[i].at[layers[i], c], outs[i].at[layers[i], c], sems[0].at[i], sems[1].at[i], (x, y, 1 - c))
                    for i in range(n)]

        return Rider(self.full, [_sds(f.shape) for f in self.full], {i: i for i in range(n)},
                     [pltpu.SemaphoreType.DMA((n,)), pltpu.SemaphoreType.DMA((n,))],
                     lambda *r: [cp.start() for cp in plan(*r)], lambda *r: [cp.wait() for cp in plan(*r)])

    def run(self):
        self.after_swap(run_rider(self.swap_rider(), "swap_halves"))
        self.after_scatter(run_rider(self.scatter_rider(), "scatter_chips"))
        return run_rider(self.share_rider(), "share_halves")


def _pad_slots(a, live):
    lead = a.shape[:-1]
    a = a.reshape(lead + (MLA_HEADS, live))
    a = jnp.pad(a, [(0, 0)] * len(lead) + [(0, 0), (0, SLOT - live)])
    return a.reshape(lead + (MLA_HEADS * SLOT,))


def _unpad_slots(a, live):
    lead = a.shape[:-1]
    return a.reshape(lead + (MLA_HEADS, SLOT))[..., :live].reshape(lead + (MLA_HEADS * live,))


def _rope_tables(positions, s):
    half = QK_ROPE // 2
    inv_freq = ROPE_THETA ** (-jnp.arange(half, dtype=F32) / half)
    ang = positions.reshape(s).astype(F32)[:, None] * inv_freq[None, :]
    cos, sin = jnp.cos(ang), jnp.sin(ang)
    one = jnp.ones((s, QK_NOPE), F32)
    z64, z16, z32 = jnp.zeros((s, QK_NOPE), F32), jnp.zeros((s, half), F32), jnp.zeros((s, SLOT - QK_DIM), F32)
    cos_t = jnp.concatenate([one, cos, cos, z32], axis=1)
    sin_a = jnp.concatenate([z64, -sin, z16, z32], axis=1)
    sin_b = jnp.concatenate([z64, z16, sin, z32], axis=1)
    return cos_t, sin_a, sin_b


def _out_weights(full):
    w_out = jnp.concatenate([full["w_out"][p] for p in range(N_CHIPS)], axis=0)
    woc = w_out[512:].reshape(MLA_HEADS, V_DIM, D_MODEL)
    woc = jnp.pad(woc, ((0, 0), (0, SLOT - V_DIM), (0, 0))).reshape(MLA_HEADS * SLOT, D_MODEL)
    return dict(woa=w_out[:256], wob=w_out[256:512], woc=woc)


def _layer_weights(full, small, l):
    w_in = jnp.concatenate([full["w_in"][p] for p in range(N_CHIPS)], axis=1)
    wc = jnp.pad(w_in[:, 1536:], ((0, 0), (0, 512 - (w_in.shape[1] - 1536))))
    w_uq = jnp.concatenate([full["mla_w_uq"][p] for p in range(N_CHIPS)], axis=1)
    w_ukv = jnp.concatenate([full["mla_w_ukv"][p] for p in range(N_CHIPS)], axis=1)
    ukv = w_ukv.reshape(KV_LORA, MLA_HEADS, QK_NOPE + V_DIM)
    row = lambda a: a.reshape(1, -1)
    return dict(
        g1=row(small["norm1_gain"][l]), wa=w_in[:, :512], wb=w_in[:, 512:1536], wc=wc,
        vg=row(small["gm_v_gain"][l]), ws=small["gm_w_s"][l], bs=small["gm_b_s"][l].reshape(4, CHUNK, 1),
        gog=row(small["gm_out_gain"][l]), hog=small["hg_out_gain"][l].reshape(-1, 1),
        qag=row(small["mla_q_a_gain"][l]), kvag=row(small["mla_kv_a_gain"][l]),
        qg=row(jnp.pad(small["mla_q_gain"][l], (0, SLOT - QK_DIM))), kg=row(jnp.pad(small["mla_k_gain"][l], (0, SLOT - QK_DIM))),
        wq=_pad_slots(w_uq, QK_DIM), wk=_pad_slots(ukv[..., :QK_NOPE].reshape(KV_LORA, -1), QK_NOPE),
        wv=_pad_slots(ukv[..., QK_NOPE:].reshape(KV_LORA, -1), V_DIM),
        mog=row(_pad_slots(small["mla_out_gain"][l], V_DIM)),
        g2=row(small["norm2_gain"][l]),
    )


def _shard_cols(a):
    r, c4 = a.shape
    return a.reshape(r, N_CHIPS, c4 // N_CHIPS).transpose(1, 0, 2)


def local_step(x, positions, target, small, comm):
    s = x.shape[0]
    cos_t, sin_a, sin_b = _rope_tables(positions, s)
    lbs = lower_bounds_fwd(small["hg_lower_bound"])
    lw, saved = [], []
    for l in range(DEPTH):
        w = _layer_weights(comm.part(l, "in"), small, l)
        lw.append(w)
        lb = lbs[l].reshape(-1, 1)
        pa, pb, pc = inproj_fwd(x, w["g1"], w["wa"], w["wb"], w["wc"])
        ya = gm_fwd(pa, w["vg"], w["ws"], w["bs"], w["gog"])
        (yb, states), got = hg_fwd(pb, lb, w["hog"], [comm.gather_rider(l, "ff1")])
        comm.gathered(l, "ff1", got[0])
        q, k, v = mla_pre_fwd(pc, cos_t, sin_a, sin_b, w["qag"], w["kvag"], w["qg"], w["kg"], w["wq"], w["wk"], w["wv"])
        (o, lse), got = attn_fwd(q, k, v, [comm.gather_rider(l, "ff2"), comm.gather_rider(l, "out")])
        comm.gathered(l, "ff2", got[0])
        comm.gathered(l, "out", got[1])
        w.update(_out_weights(comm.part(l, "out")))
        x1 = outproj_fwd(x, ya, yb, o, w["mog"], w["woa"], w["wob"], w["woc"])
        w["w1"], w["w2"] = comm.part(l, "ff1")["w_ff1"], comm.part(l, "ff2")["w_ff2"]
        rider = comm.gather_rider(l + 1, "in") if l + 1 < DEPTH else None
        (x2, r), got = ffn_fwd(x1, w["g2"], w["w1"], w["w2"], [rider])
        comm.gathered(l + 1, "in", got[0])
        saved.append(dict(x=x, pa=pa, pb=pb, pc=pc, ya=ya, yb=yb, states=states, q=q, k=k, v=v, o=o, lse=lse, x1=x1, r=r, lb=lb))
        x = x2
    dx, loss_part = loss_head(x, target)
    groups = [dict() for _ in range(DEPTH)]
    sm = {n: [None] * DEPTH for n in ("norm1_gain", "gm_v_gain", "gm_w_s", "gm_b_s", "gm_out_gain", "hg_out_gain",
                                       "mla_q_a_gain", "mla_kv_a_gain", "mla_q_gain", "mla_k_gain", "mla_out_gain",
                                       "norm2_gain")}
    dlbs = [None] * DEPTH
    halves = lambda g: g.reshape(N_CHIPS, 2, g.shape[1] // 2, g.shape[2])
    take = lambda red, f: None if red is None else f(red)
    red_mix = red_late = None
    for l in reversed(range(DEPTH)):
        w, a = lw[l], saved[l]
        (dx1p, dg2, dw1, dw2), got = ffn_bwd(a["x1"], w["g2"], w["w1"], w["w2"], a["r"], dx,
                                             [take(red_mix, Reducer.swap_rider), take(red_late, Reducer.share_rider)])
        if red_mix:
            red_mix.after_swap(got[0])
        if red_late:
            comm.reduced(MIX, got[1])
        ffn_arrs = [halves(dw1), halves(dw2)]
        red_ffn = comm.reducer(ffn_arrs, FFN, l)
        (dx1, dya, dyb, do, dmog, dwoa, dwob, dwoc), got = outproj_bwd(
            a["ya"], a["yb"], a["o"], w["mog"], w["woa"], w["wob"], w["woc"], dx, dx1p, [take(red_ffn, Reducer.swap_rider)])
        if red_ffn:
            red_ffn.after_swap(got[0])
        (dq, dk, dv), got = attn_bwd(a["q"], a["k"], a["v"], a["o"], do, a["lse"], [take(red_ffn, Reducer.scatter_rider)])
        if red_ffn:
            red_ffn.after_scatter(got[0])
        dpc, dqag, dkvag, dqg, dkg, dwq, dwk, dwv = mla_pre_bwd(a["pc"], cos_t, sin_a, sin_b, w["qag"], w["kvag"], w["qg"],
                                                                  w["kg"], w["wq"], w["wk"], w["wv"], dq, dk, dv)
        (dpb, dlb, dhog), got = hg_bwd(a["pb"], a["lb"], w["hog"], a["states"], dyb,
                                       [take(red_ffn, Reducer.share_rider), take(red_mix, Reducer.scatter_rider)])
        if red_ffn:
            comm.reduced(FFN, got[0])
        else:
            groups[l].update(zip(FFN, ffn_arrs))
        if red_mix:
            red_mix.after_scatter(got[1])
        red_late = red_mix
        dpa, dvg, dws, dbs, dgog = gm_bwd(a["pa"], w["vg"], w["ws"], w["bs"], w["gog"], dya)
        dx, dg1, dwa, dwb, dwc = inproj_bwd(a["x"], w["g1"], w["wa"], w["wb"], w["wc"], dpa, dpb, dpc, dx1)
        dukv = jnp.concatenate([dwk.reshape(KV_LORA, MLA_HEADS, SLOT)[..., :QK_NOPE],
                                dwv.reshape(KV_LORA, MLA_HEADS, SLOT)[..., :V_DIM]], axis=-1)
        dwo = jnp.concatenate([dwoa, dwob, dwoc.reshape(MLA_HEADS, SLOT, D_MODEL)[:, :V_DIM].reshape(-1, D_MODEL)], axis=0)
        mix_arrs = [halves(_shard_cols(jnp.concatenate([dwa, dwb, dwc[:, :1952 - 1536]], axis=1))),
                    halves(_shard_cols(_unpad_slots(dwq, QK_DIM))), halves(_shard_cols(dukv.reshape(KV_LORA, -1))),
                    halves(dwo.reshape(N_CHIPS, -1, D_MODEL))]
        red_mix = comm.reducer(mix_arrs, MIX, l) if l > 0 else None
        if red_mix is None:
            groups[l].update(zip(MIX, mix_arrs))
        sm["norm1_gain"][l] = dg1[0]
        sm["gm_v_gain"][l] = dvg[0]
        sm["gm_w_s"][l] = dws
        sm["gm_b_s"][l] = dbs[..., 0]
        sm["gm_out_gain"][l] = dgog[0]
        sm["hg_out_gain"][l] = dhog[:, 0]
        sm["mla_q_a_gain"][l] = dqag[0]
        sm["mla_kv_a_gain"][l] = dkvag[0]
        sm["mla_q_gain"][l] = dqg[0, :QK_DIM]
        sm["mla_k_gain"][l] = dkg[0, :QK_DIM]
        sm["mla_out_gain"][l] = _unpad_slots(dmog[0], V_DIM)
        sm["norm2_gain"][l] = dg2[0]
        dlbs[l] = dlb[:, 0]
    sm["hg_lower_bound"] = [lower_bounds_bwd(small["hg_lower_bound"], jnp.stack(dlbs))]
    return loss_part, dx, groups, sm, red_late


MIX = ("w_in", "mla_w_uq", "mla_w_ukv", "w_out")
FFN = ("w_ff1", "w_ff2")
BIG = MIX + FFN
PARTS = {"in": ("w_in", "mla_w_uq", "mla_w_ukv"), "out": ("w_out",), "ff1": ("w_ff1",), "ff2": ("w_ff2",)}
SMALL = ("norm1_gain", "gm_v_gain", "gm_w_s", "gm_b_s", "gm_out_gain", "hg_lower_bound", "hg_out_gain",
         "mla_q_a_gain", "mla_kv_a_gain", "mla_q_gain", "mla_k_gain", "mla_out_gain", "norm2_gain")
ORDER = ("norm1_gain", "w_in", "gm_v_gain", "gm_w_s", "gm_b_s", "gm_out_gain", "hg_lower_bound", "hg_out_gain",
         "mla_q_a_gain", "mla_w_uq", "mla_kv_a_gain", "mla_w_ukv", "mla_q_gain", "mla_k_gain", "mla_out_gain",
         "w_out", "norm2_gain", "w_ff1", "w_ff2")
PACK_ROWS = 320


def _pack(pieces):
    flat = jnp.concatenate([a.reshape(-1) for a in pieces])
    total = 2 * N_CHIPS * PACK_ROWS * SLOT
    return jnp.pad(flat, (0, total - flat.shape[0]))


def _unpack(flat, shapes):
    out, off = [], 0
    for sh in shapes:
        size = 1
        for d in sh:
            size *= d
        out.append(flat[off:off + size].reshape(sh))
        off += size
    return out


class ChipComm:
    def __init__(self, shards):
        self.shards = shards
        self.full = {}
        self.grads = {}

    def gather_rider(self, l, part):
        return gather_rider([self.shards[n][l].astype(MXU_DTYPE).reshape(4, self.shards[n].shape[1] // 4, -1)
                             for n in PARTS[part]])

    def gathered(self, l, part, outs):
        if outs is not None:
            self.full[l, part] = {n: o.reshape((N_CHIPS,) + self.shards[n].shape[1:]) for n, o in zip(PARTS[part], outs)}

    def part(self, l, part):
        if (l, part) not in self.full:
            self.gathered(l, part, run_rider(self.gather_rider(l, part), "gather_weights"))
        return self.full[l, part]

    def places(self, names, l):
        return [(l, self.shards[n].shape[0], self.grads.get(n)) for n in names]

    def reducer(self, arrs, names, l):
        return Reducer(arrs, lambda: self.places(names, l))

    def reduced(self, names, outs):
        self.grads.update(zip(names, outs))


def kernel(x, positions, norm1_gain, w_in, gm_v_gain, gm_w_s, gm_b_s, gm_out_gain, hg_lower_bound, hg_out_gain, mla_q_a_gain, mla_w_uq, mla_kv_a_gain, mla_w_ukv, mla_q_gain, mla_k_gain, mla_out_gain, w_out, norm2_gain, w_ff1, w_ff2, loss_target, m_norm1_gain, m_w_in, m_gm_v_gain, m_gm_w_s, m_gm_b_s, m_gm_out_gain, m_hg_lower_bound, m_hg_out_gain, m_mla_q_a_gain, m_mla_w_uq, m_mla_kv_a_gain, m_mla_w_ukv, m_mla_q_gain, m_mla_k_gain, m_mla_out_gain, m_w_out, m_norm2_gain, m_w_ff1, m_w_ff2, v_norm1_gain, v_w_in, v_gm_v_gain, v_gm_w_s, v_gm_b_s, v_gm_out_gain, v_hg_lower_bound, v_hg_out_gain, v_mla_q_a_gain, v_mla_w_uq, v_mla_kv_a_gain, v_mla_w_ukv, v_mla_q_gain, v_mla_k_gain, v_mla_out_gain, v_w_out, v_norm2_gain, v_w_ff1, v_w_ff2):
    given = dict(locals())
    weights = {n: given[n] for n in ORDER}
    moms = {n: given["m_" + n] for n in ORDER}
    vars_ = {n: given["v_" + n] for n in ORDER}
    s, d = x.shape[1], x.shape[2]

    small = {n: weights[n] for n in SMALL}
    comm = ChipComm({n: weights[n] for n in BIG})
    loss_part, dx, groups, small_g, red_late = local_step(x.reshape(s, d), positions, loss_target.reshape(s, d), small, comm)
    loss = lax.psum(loss_part[0, 0], ("x", "y", "c"))

    pack_g = _pack([jnp.stack(small_g[n]) for n in SMALL]).reshape(N_CHIPS, 2, PACK_ROWS, SLOT)
    last = Reducer([groups[0][n] for n in MIX] + [pack_g], lambda: comm.places(MIX, 0) + [(0, 1, None)])
    swap = last.swap_rider()
    outs = run_rider(_merge_riders([swap, red_late.share_rider()]), "swap_halves")
    last.after_swap(outs[:len(swap.out_shapes)])
    comm.reduced(MIX, outs[len(swap.out_shapes):])
    last.after_scatter(run_rider(last.scatter_rider(), "scatter_chips"))
    reduced = run_rider(last.share_rider(), "share_halves")
    comm.reduced(MIX, reduced[:-1])
    pack_full = run_rider(gather_rider([reduced[-1].reshape(4, PACK_ROWS // 2, SLOT)]), "gather_small")[0].reshape(-1)
    grads = {n: comm.grads[n].reshape(weights[n].shape) for n in BIG}
    grads.update(zip(SMALL, _unpack(pack_full, [weights[n].shape for n in SMALL])))

    delta, new_m, new_v = {}, {}, {}
    flat2 = lambda a: a.reshape(-1, a.shape[-1])
    for n in ORDER:
        outs, _ = adamw(flat2(weights[n]), flat2(grads[n]), flat2(moms[n]), flat2(vars_[n]), "adamw_" + n)
        delta[n], new_m[n], new_v[n] = [o.reshape(weights[n].shape) for o in outs]

    return (loss, dx.reshape(x.shape), *[grads[n] for n in ORDER], *[delta[n] for n in ORDER],
            *[new_m[n] for n in ORDER], *[new_v[n] for n in ORDER])
```

```python
import functools

import jax
import jax.numpy as jnp
from jax import lax
from jax.experimental import pallas as pl
from jax.experimental.pallas import tpu as pltpu

F32 = jnp.float32
BF16 = jnp.bfloat16
MXU_DTYPE = BF16

D_MODEL = 1024
DEPTH = 4
CHUNK = 128
HG_CHUNK = 128
HG_CHUNKS = 4
GM_CHUNKS = 4
EPS = 1e-6
HEAD64 = 64
MLA_HEADS = 8
QK_NOPE = 64
QK_ROPE = 32
QK_DIM = 96
V_DIM = 64
Q_LORA = 256
KV_LORA = 128
SLOT = 128
ROPE_THETA = 10000.0
N_CHIPS = 4

ADAM_LR = 0.001
ADAM_B1 = 0.9
ADAM_B2 = 0.999
ADAM_EPS = 1e-08
ADAM_WD = 0.01
ADAM_STEP = 10

TM = 512
TM_FFN = 512
TQ = 256
ATT_HEADS_PER_STEP = 4
ATT_WIDE = 512
V7X_VMEM_BYTES = 64 * 1024 * 1024
VMEM_LIMIT = V7X_VMEM_BYTES * 7 // 8

NN = (((1,), (0,)), ((), ()))
NT = (((1,), (1,)), ((), ()))
TN = (((0,), (0,)), ((), ()))
BNN = (((2,), (1,)), ((0,), (0,)))
BNT = (((2,), (2,)), ((0,), (0,)))
BTN = (((1,), (1,)), ((0,), (0,)))


def _dot(a, b, dims):
    return lax.dot_general(a.astype(MXU_DTYPE), b.astype(MXU_DTYPE), dims, preferred_element_type=F32)


def _hdot(a, b, dims=NN):
    return lax.dot_general(a, b, dims, precision=lax.Precision.HIGHEST, preferred_element_type=F32)


def _make_ad(dims, da_dims, da_swap, db_dims, db_swap):
    @jax.custom_vjp
    def f(a, b):
        return _dot(a, b, dims)

    def fwd(a, b):
        return _dot(a, b, dims), (a, b)

    def bwd(res, g):
        a, b = res
        da = _dot(b, g, da_dims) if da_swap else _dot(g, b, da_dims)
        db = _dot(g, a, db_dims) if db_swap else _dot(a, g, db_dims)
        return da, db

    f.defvjp(fwd, bwd)
    return f


@functools.partial(jax.custom_vjp, nondiff_argnums=(1,))
def _roll_ad(x, shift):
    return pltpu.roll(x, shift, 1)


def _roll_ad_fwd(x, shift):
    return pltpu.roll(x, shift, 1), None


def _roll_ad_bwd(shift, _, g):
    return (pltpu.roll(g, (g.shape[1] - shift) % g.shape[1], 1),)


_roll_ad.defvjp(_roll_ad_fwd, _roll_ad_bwd)


class _Ops:
    pass


PLAIN = _Ops()
PLAIN.mm = lambda a, b: _dot(a, b, NN)
PLAIN.bmm = lambda a, b: _dot(a, b, BNN)
PLAIN.bmm_nt = lambda a, b: _dot(a, b, BNT)
PLAIN.bmm_tn = lambda a, b: _dot(a, b, BTN)
PLAIN.roll = lambda x, s: pltpu.roll(x, s, 1)

AD = _Ops()
AD.mm = _make_ad(NN, NT, False, TN, False)
AD.bmm = _make_ad(BNN, BNT, False, BTN, False)
AD.bmm_nt = _make_ad(BNT, BNN, False, BTN, True)
AD.bmm_tn = _make_ad(BTN, BNT, True, BNN, False)
AD.roll = _roll_ad


def _sigmoid(x):
    return jax.nn.sigmoid(x)


def _gelu(x):
    return 0.5 * x * (1.0 + jnp.tanh(0.7978845608028654 * (x + 0.044715 * (x * x * x))))


def _rms(x, g):
    return x * lax.rsqrt(jnp.mean(x * x, axis=-1, keepdims=True) + EPS) * g


def _head_masks256():
    lane = lax.broadcasted_iota(jnp.int32, (1, 4 * HEAD64), 1)
    return [(jnp.right_shift(lane, 6) == h).astype(F32) for h in range(4)]


def _headnorm256(x, g):
    ms = jnp.zeros_like(x)
    sq = x * x
    for m in _head_masks256():
        ms = ms + m * (jnp.sum(sq * m, axis=-1, keepdims=True) * (1.0 / HEAD64))
    return x * lax.rsqrt(ms + EPS) * g


def _slot_norm(x, g, n):
    return x * lax.rsqrt(jnp.sum(x * x, axis=-1, keepdims=True) * (1.0 / n) + EPS) * g


def _rope(ops, x, cos_t, sin_a, sin_b):
    return x * cos_t + ops.roll(x, SLOT - QK_ROPE // 2) * sin_a + ops.roll(x, QK_ROPE // 2) * sin_b


def _inproj(ops, x, g1, wa, wb, wc):
    h = _rms(x, g1)
    return ops.mm(h, wa), ops.mm(h, wb), ops.mm(h, wc)


def _gm_chunk(ops, ur, vr, vg, ws4, bs, og):
    c = ur.shape[0]
    masks = _head_masks256()
    mh = jnp.concatenate([m[None] for m in masks], axis=0)
    u = _gelu(ur)
    v = _headnorm256(_gelu(vr), vg)
    t = lax.broadcasted_iota(jnp.int32, (c, c), 0)
    s = lax.broadcasted_iota(jnp.int32, (c, c), 1)
    w = jnp.where((t >= s)[None], ws4, 0.0)
    y = jnp.sum(ops.bmm(w, v[None] * mh), axis=0)
    for h in range(4):
        y = y + bs[h] * masks[h]
    return _headnorm256(u * y, og)


def _hg_chunk(ops, st, qr, fr, ir, gr, lb, og):
    c, n = qr.shape
    nh = n // HEAD64
    heads = lambda x: x.reshape(nh, HEAD64, x.shape[-1])
    tr = lambda x: heads(x.T)
    lb4, og4 = heads(lb), heads(og)
    qx = tr(qr)
    q = qx * _sigmoid(qx)
    f = lb4 + (1.0 - lb4) * _sigmoid(tr(fr))
    k = 1.0 - f
    logf = jnp.log(f)
    v = tr(ir)
    gx = tr(gr)
    s = lax.broadcasted_iota(jnp.int32, (c, c), 0)
    t = lax.broadcasted_iota(jnp.int32, (c, c), 1)
    tl = lax.broadcasted_iota(jnp.int32, (1, c), 1).reshape(1, 1, c)
    b2 = _hdot(logf.reshape(n, c), (s <= t).astype(F32))
    b = heads(b2)
    btot = jnp.sum(logf, axis=2, keepdims=True)
    inter = ops.bmm_tn(st, q * jnp.exp(b))
    p4 = jnp.zeros((nh, c, c), F32)
    tt, ss = s, t
    lg = c.bit_length() - 2
    while lg >= 0:
        m = 1 << lg
        bnd = jnp.left_shift(jnp.right_shift(t, lg + 1), lg + 1) + (m - 1)
        r = heads(_hdot(b2, (s == bnd).astype(F32)))
        right = jnp.bitwise_and(jnp.right_shift(tl, lg), 1) == 1
        qe = jnp.where(right, q * jnp.exp(jnp.where(right, b - r, 0.0)), 0.0)
        ke = jnp.where(right, 0.0, k * jnp.exp(jnp.where(right, 0.0, r - b)))
        lm = ((jnp.right_shift(tt, lg + 1) == jnp.right_shift(ss, lg + 1))
              & (jnp.bitwise_and(jnp.right_shift(tt, lg), 1) == 1)
              & (jnp.bitwise_and(jnp.right_shift(ss, lg), 1) == 0))
        p4 = jnp.where(lm[None], ops.bmm_tn(qe, ke), p4)
        lg -= 1
    intra = ops.bmm_nt(v, p4)
    o = inter + intra + jnp.sum(q * k, axis=1, keepdims=True) * v
    st_new = st * jnp.exp(btot) + ops.bmm_nt(k * jnp.exp(btot - b), v)
    y = o * lax.rsqrt(jnp.mean(o * o, axis=1, keepdims=True) + EPS) * og4 * (gx * _sigmoid(gx))
    return st_new, y.reshape(n, c).T


def _mla_pre(ops, cq, ckv, kpe, cos_t, sin_a, sin_b, qag, kvag, qg, kg, wq, wk, wv):
    cqn = _rms(cq, qag)
    ckvn = _rms(ckv, kvag)
    kper = ops.roll(kpe, QK_NOPE)
    qs, ks, vs = [], [], []
    for h in range(MLA_HEADS):
        qh = _slot_norm(ops.mm(cqn, wq[h]), qg, QK_DIM)
        qs.append(_rope(ops, qh, cos_t, sin_a, sin_b))
        kh = _slot_norm(ops.mm(ckvn, wk[h]) + kper, kg, QK_DIM)
        ks.append(_rope(ops, kh, cos_t, sin_a, sin_b))
        vs.append(ops.mm(ckvn, wv[h]))
    return qs, ks, vs


def _outproj(ops, x, ya, yb, o, mog, woa, wob, woc):
    yc = jnp.concatenate([_slot_norm(o[h], mog[h], V_DIM) for h in range(MLA_HEADS)], axis=1)
    return x + ops.mm(ya, woa) + ops.mm(yb, wob) + ops.mm(yc, woc)


def _lower_bounds(r0, r1, r2, r3):
    mx = jnp.maximum(jnp.maximum(r0, r1), jnp.maximum(r2, r3))
    e0, e1, e2, e3 = jnp.exp(r0 - mx), jnp.exp(r1 - mx), jnp.exp(r2 - mx), jnp.exp(r3 - mx)
    inv = 1.0 / (e0 + e1 + e2 + e3)
    s1, s2, s3 = e1 * inv, e2 * inv, e3 * inv
    return jnp.zeros_like(r0), s1, s1 + s2, s1 + s2 + s3


def _cp(sem):
    return pltpu.CompilerParams(dimension_semantics=sem, vmem_limit_bytes=VMEM_LIMIT)


def _rows(tm, n):
    return pl.BlockSpec((tm, n), lambda i: (i, 0))


def _full(a):
    nd = len(a.shape)
    return pl.BlockSpec(a.shape, lambda *_: (0,) * nd, pipeline_mode=pl.Buffered(1))


def _sds(shape, dtype=F32):
    return jax.ShapeDtypeStruct(shape, dtype)


def _acc(ref, val, first):
    @pl.when(first)
    def _():
        ref[...] = val

    @pl.when(jnp.logical_not(first))
    def _():
        ref[...] = ref[...] + val


def _f32(ref):
    return ref[...].astype(F32)


MESH = pl.DeviceIdType.MESH
ANY = pl.BlockSpec(memory_space=pl.ANY)


class Rider:
    def __init__(self, arrays, out_shapes, aliases, sems, start, finish):
        self.arrays, self.out_shapes, self.aliases, self.sems = list(arrays), list(out_shapes), dict(aliases), list(sems)
        self.start, self.finish = start, finish


def run_rider(rider, name):
    n_in, n_out = len(rider.arrays), len(rider.out_shapes)

    def body(*refs):
        ins, outs, sems = refs[:n_in], refs[n_in:n_in + n_out], refs[n_in + n_out:]
        rider.start(ins, outs, sems)
        rider.finish(ins, outs, sems)

    return pl.pallas_call(
        body, name=name, in_specs=[ANY] * n_in, out_specs=[ANY] * n_out, out_shape=rider.out_shapes,
        input_output_aliases=rider.aliases, scratch_shapes=rider.sems,
    )(*rider.arrays)


def _merge_riders(riders):
    bounds, a0, o0, s0 = [], 0, 0, 0
    for r in riders:
        bounds.append((a0, o0, s0))
        a0, o0, s0 = a0 + len(r.arrays), o0 + len(r.out_shapes), s0 + len(r.sems)

    def part(k, ins, outs, sems):
        a, o, s = bounds[k]
        r = riders[k]
        return ins[a:a + len(r.arrays)], outs[o:o + len(r.out_shapes)], sems[s:s + len(r.sems)]

    return Rider(
        [x for r in riders for x in r.arrays], [x for r in riders for x in r.out_shapes],
        {bounds[k][0] + i: bounds[k][1] + o for k, r in enumerate(riders) for i, o in r.aliases.items()},
        [x for r in riders for x in r.sems],
        lambda *refs: [r.start(*part(k, *refs)) for k, r in enumerate(riders)],
        lambda *refs: [r.finish(*part(k, *refs)) for k, r in enumerate(riders)])


def _ride(compute, riders, *, name, grid, in_specs, out_specs, out_shape, operands, scratch_shapes=(), sem=None):
    single = not isinstance(out_shape, (list, tuple))
    if single:
        out_specs, out_shape = [out_specs], [out_shape]
    live = [r for r in riders if r is not None]
    if not live:
        res = pl.pallas_call(compute, name=name, grid=grid, in_specs=in_specs, out_specs=out_specs, out_shape=out_shape,
                             scratch_shapes=list(scratch_shapes), compiler_params=_cp(sem))(*operands)
        return (res[0] if single else res), [None] * len(riders)
    rider = live[0] if len(live) == 1 else _merge_riders(live)
    n_in, n_out, n_s = len(in_specs), len(out_specs), len(scratch_shapes)
    r_in, r_out = len(rider.arrays), len(rider.out_shapes)

    def body(*refs):
        ins, rins = refs[:n_in], refs[n_in:n_in + r_in]
        outs = refs[n_in + r_in:n_in + r_in + n_out]
        routs = refs[n_in + r_in + n_out:n_in + r_in + n_out + r_out]
        scr = refs[n_in + r_in + n_out + r_out:n_in + r_in + n_out + r_out + n_s]
        rsems = refs[n_in + r_in + n_out + r_out + n_s:]
        first = functools.reduce(jnp.logical_and, [pl.program_id(a) == 0 for a in range(len(grid))])
        last = functools.reduce(jnp.logical_and, [pl.program_id(a) == grid[a] - 1 for a in range(len(grid))])

        @pl.when(first)
        def _():
            rider.start(rins, routs, rsems)

        compute(*ins, *outs, *scr)

        @pl.when(last)
        def _():
            rider.finish(rins, routs, rsems)

    res = pl.pallas_call(
        body, name=name, grid=grid, in_specs=list(in_specs) + [ANY] * r_in, out_specs=list(out_specs) + [ANY] * r_out,
        out_shape=list(out_shape) + rider.out_shapes,
        input_output_aliases={n_in + k: n_out + v for k, v in rider.aliases.items()},
        scratch_shapes=list(scratch_shapes) + rider.sems, compiler_params=_cp(("arbitrary",) * len(grid)),
    )(*operands, *rider.arrays)
    main, rest, per_rider = res[:n_out], list(res[n_out:]), []
    for r in riders:
        per_rider.append(None if r is None else [rest.pop(0) for _ in r.out_shapes])
    return (main[0] if single else main), per_rider


def inproj_fwd(x, g1, wa, wb, wc):
    s, d = x.shape

    def body(x_ref, g_ref, wa_ref, wb_ref, wc_ref, pa_ref, pb_ref, pc_ref):
        pa, pb, pc = _inproj(PLAIN, x_ref[...], g_ref[...], wa_ref[...], wb_ref[...], wc_ref[...])
        pa_ref[...] = pa
        pb_ref[...] = pb
        pc_ref[...] = pc

    return pl.pallas_call(
        body, name="inproj_fwd", grid=(s // TM,),
        in_specs=[_rows(TM, d), _full(g1), _full(wa), _full(wb), _full(wc)],
        out_specs=[_rows(TM, wa.shape[1]), _rows(TM, wb.shape[1]), _rows(TM, wc.shape[1])],
        out_shape=[_sds((s, wa.shape[1])), _sds((s, wb.shape[1])), _sds((s, wc.shape[1]))],
        compiler_params=_cp(("parallel",)),
    )(x, g1, wa, wb, wc)


def inproj_bwd(x, g1, wa, wb, wc, dpa, dpb, dpc, dres):
    s, d = x.shape

    def body(x_ref, g_ref, wa_ref, wb_ref, wc_ref, dpa_ref, dpb_ref, dpc_ref, dres_ref,
             dx_ref, dg_ref, dwa_ref, dwb_ref, dwc_ref):
        first = pl.program_id(0) == 0
        _, vjp = jax.vjp(functools.partial(_inproj, AD), x_ref[...], g_ref[...],
                         _f32(wa_ref), _f32(wb_ref), _f32(wc_ref))
        dx, dg, dwa, dwb, dwc = vjp((dpa_ref[...], dpb_ref[...], dpc_ref[...]))
        dx_ref[...] = dx + dres_ref[...]
        _acc(dg_ref, dg, first)
        _acc(dwa_ref, dwa, first)
        _acc(dwb_ref, dwb, first)
        _acc(dwc_ref, dwc, first)

    return pl.pallas_call(
        body, name="inproj_bwd", grid=(s // TM,),
        in_specs=[_rows(TM, d), _full(g1), _full(wa), _full(wb), _full(wc),
                  _rows(TM, wa.shape[1]), _rows(TM, wb.shape[1]), _rows(TM, wc.shape[1]), _rows(TM, d)],
        out_specs=[_rows(TM, d), _full(g1), _full(wa), _full(wb), _full(wc)],
        out_shape=[_sds((s, d)), _sds(g1.shape), _sds(wa.shape), _sds(wb.shape), _sds(wc.shape)],
        compiler_params=_cp(("arbitrary",)),
    )(x, g1, wa, wb, wc, dpa, dpb, dpc, dres)


def gm_fwd(pa, vg, ws4, bs, og):
    s = pa.shape[0]
    w = pa.shape[1] // 2

    def body(pa_ref, vg_ref, ws_ref, bs_ref, og_ref, ya_ref):
        bsl = [bs_ref[h] for h in range(4)]
        for j in range(GM_CHUNKS):
            rows = slice(j * CHUNK, (j + 1) * CHUNK)
            ya_ref[rows, :] = _gm_chunk(PLAIN, pa_ref[rows, 0:w], pa_ref[rows, w:2 * w], vg_ref[...], ws_ref[...], bsl,
                                        og_ref[...])

    tm = GM_CHUNKS * CHUNK
    return pl.pallas_call(
        body, name="gm_fwd", grid=(s // tm,),
        in_specs=[_rows(tm, 2 * w), _full(vg), _full(ws4), _full(bs), _full(og)],
        out_specs=_rows(tm, w), out_shape=_sds((s, w)),
        compiler_params=_cp(("parallel",)),
    )(pa, vg, ws4, bs, og)


def gm_bwd(pa, vg, ws4, bs, og, dya):
    s = pa.shape[0]
    w = pa.shape[1] // 2

    def body(pa_ref, vg_ref, ws_ref, bs_ref, og_ref, dya_ref, dpa_ref, dvg_ref, dws_ref, dbs_ref, dog_ref):
        first = pl.program_id(0) == 0
        bsl = [bs_ref[h] for h in range(4)]
        tot = None
        for j in range(GM_CHUNKS):
            rows = slice(j * CHUNK, (j + 1) * CHUNK)
            _, vjp = jax.vjp(functools.partial(_gm_chunk, AD), pa_ref[rows, 0:w], pa_ref[rows, w:2 * w],
                             vg_ref[...], ws_ref[...], bsl, og_ref[...])
            du, dv, *dws = vjp(dya_ref[rows, :])
            dpa_ref[rows, 0:w] = du
            dpa_ref[rows, w:2 * w] = dv
            tot = dws if tot is None else jax.tree.map(jnp.add, tot, dws)
        dvg, dws, dbs, dog = tot
        _acc(dvg_ref, dvg, first)
        _acc(dws_ref, dws, first)
        _acc(dog_ref, dog, first)
        for h in range(4):
            _acc(dbs_ref.at[h], dbs[h], first)

    tm = GM_CHUNKS * CHUNK
    return pl.pallas_call(
        body, name="gm_bwd", grid=(s // tm,),
        in_specs=[_rows(tm, 2 * w), _full(vg), _full(ws4), _full(bs), _full(og), _rows(tm, w)],
        out_specs=[_rows(tm, 2 * w), _full(vg), _full(ws4), _full(bs), _full(og)],
        out_shape=[_sds((s, 2 * w)), _sds(vg.shape), _sds(ws4.shape), _sds(bs.shape), _sds(og.shape)],
        compiler_params=_cp(("arbitrary",)),
    )(pa, vg, ws4, bs, og, dya)


def hg_fwd(pb, lb, og, riders=()):
    s = pb.shape[0]
    w = pb.shape[1] // 4
    tm = HG_CHUNKS * HG_CHUNK
    st_shape = (w // HEAD64, HEAD64, HEAD64)

    def body(pb_ref, lb_ref, og_ref, yb_ref, states_ref, st_ref):
        @pl.when(pl.program_id(0) == 0)
        def _():
            st_ref[...] = jnp.zeros_like(st_ref)

        st = st_ref[...]
        for j in range(HG_CHUNKS):
            rows = slice(j * HG_CHUNK, (j + 1) * HG_CHUNK)
            states_ref[j] = st
            st, y = _hg_chunk(PLAIN, st, pb_ref[rows, 0:w], pb_ref[rows, w:2 * w], pb_ref[rows, 2 * w:3 * w],
                              pb_ref[rows, 3 * w:4 * w], lb_ref[...], og_ref[...])
            yb_ref[rows, :] = y
        st_ref[...] = st

    return _ride(
        body, riders, name="hg_fwd", grid=(s // tm,),
        in_specs=[_rows(tm, 4 * w), _full(lb), _full(og)],
        out_specs=[_rows(tm, w), pl.BlockSpec((HG_CHUNKS,) + st_shape, lambda i: (i, 0, 0, 0))],
        out_shape=[_sds((s, w)), _sds((s // HG_CHUNK,) + st_shape)],
        scratch_shapes=[pltpu.VMEM(st_shape, F32)],
        operands=(pb, lb, og), sem=("arbitrary",))


def hg_bwd(pb, lb, og, states, dyb, riders=()):
    s = pb.shape[0]
    w = pb.shape[1] // 4
    tm = HG_CHUNKS * HG_CHUNK
    nc = s // tm
    st_shape = (w // HEAD64, HEAD64, HEAD64)

    def body(pb_ref, lb_ref, og_ref, states_ref, dyb_ref, dpb_ref, dlb_ref, dog_ref, dst_ref):
        first = pl.program_id(0) == 0

        @pl.when(first)
        def _():
            dst_ref[...] = jnp.zeros_like(dst_ref)

        dst, dlb, dog = dst_ref[...], None, None
        for j in reversed(range(HG_CHUNKS)):
            rows = slice(j * HG_CHUNK, (j + 1) * HG_CHUNK)
            _, vjp = jax.vjp(functools.partial(_hg_chunk, AD), states_ref[j], pb_ref[rows, 0:w], pb_ref[rows, w:2 * w],
                             pb_ref[rows, 2 * w:3 * w], pb_ref[rows, 3 * w:4 * w], lb_ref[...], og_ref[...])
            dst, dq, df, di, dg, dlb_j, dog_j = vjp((dst, dyb_ref[rows, :]))
            dpb_ref[rows, 0:w] = dq
            dpb_ref[rows, w:2 * w] = df
            dpb_ref[rows, 2 * w:3 * w] = di
            dpb_ref[rows, 3 * w:4 * w] = dg
            dlb = dlb_j if dlb is None else dlb + dlb_j
            dog = dog_j if dog is None else dog + dog_j
        dst_ref[...] = dst
        _acc(dlb_ref, dlb, first)
        _acc(dog_ref, dog, first)

    rev = lambda i: (nc - 1 - i, 0)
    return _ride(
        body, riders, name="hg_bwd", grid=(nc,),
        in_specs=[pl.BlockSpec((tm, 4 * w), rev), _full(lb), _full(og),
                  pl.BlockSpec((HG_CHUNKS,) + st_shape, lambda i: (nc - 1 - i, 0, 0, 0)), pl.BlockSpec((tm, w), rev)],
        out_specs=[pl.BlockSpec((tm, 4 * w), rev), _full(lb), _full(og)],
        out_shape=[_sds((s, 4 * w)), _sds(lb.shape), _sds(og.shape)],
        scratch_shapes=[pltpu.VMEM(st_shape, F32)],
        operands=(pb, lb, og, states, dyb), sem=("arbitrary",))


def lower_bounds_fwd(hlb):
    def body(h_ref, o_ref):
        outs = _lower_bounds(*[h_ref[pl.ds(i, 1), :] for i in range(DEPTH)])
        for i in range(DEPTH):
            o_ref[pl.ds(i, 1), :] = outs[i]

    return pl.pallas_call(body, name="lower_bounds_fwd", out_shape=_sds(hlb.shape))(hlb)


def lower_bounds_bwd(hlb, dlbs):
    def body(h_ref, d_ref, o_ref):
        _, vjp = jax.vjp(_lower_bounds, *[h_ref[pl.ds(i, 1), :] for i in range(DEPTH)])
        outs = vjp(tuple(d_ref[pl.ds(i, 1), :] for i in range(DEPTH)))
        for i in range(DEPTH):
            o_ref[pl.ds(i, 1), :] = outs[i]

    return pl.pallas_call(body, name="lower_bounds_bwd", out_shape=_sds(hlb.shape))(hlb, dlbs)


def _mla_pre_args(pc_ref, cos_ref, sa_ref, sb_ref, qag_ref, kvag_ref, qg_ref, kg_ref, wq_ref, wk_ref, wv_ref, cast):
    sl = lambda h: slice(h * SLOT, (h + 1) * SLOT)
    ld = (lambda r, h: r[:, sl(h)].astype(F32)) if cast else (lambda r, h: r[:, sl(h)])
    diff = (pc_ref[:, 0:Q_LORA], pc_ref[:, Q_LORA:Q_LORA + KV_LORA], pc_ref[:, Q_LORA + KV_LORA:Q_LORA + 2 * KV_LORA],
            qag_ref[...], kvag_ref[...], qg_ref[...], kg_ref[...],
            [ld(wq_ref, h) for h in range(MLA_HEADS)], [ld(wk_ref, h) for h in range(MLA_HEADS)],
            [ld(wv_ref, h) for h in range(MLA_HEADS)])
    tables = (cos_ref[...], sa_ref[...], sb_ref[...])
    return diff, tables


def _mla_pre_fn(ops, tables, cq, ckv, kpe, qag, kvag, qg, kg, wq, wk, wv):
    return _mla_pre(ops, cq, ckv, kpe, *tables, qag, kvag, qg, kg, wq, wk, wv)


def mla_pre_fwd(pc, cos_t, sin_a, sin_b, qag, kvag, qg, kg, wq, wk, wv):
    s = pc.shape[0]
    hw = MLA_HEADS * SLOT

    def body(pc_ref, cos_ref, sa_ref, sb_ref, qag_ref, kvag_ref, qg_ref, kg_ref, wq_ref, wk_ref, wv_ref,
             q_ref, k_ref, v_ref):
        diff, tables = _mla_pre_args(pc_ref, cos_ref, sa_ref, sb_ref, qag_ref, kvag_ref, qg_ref, kg_ref,
                                     wq_ref, wk_ref, wv_ref, False)
        qs, ks, vs = _mla_pre_fn(PLAIN, tables, *diff)
        ones_lane = (lax.broadcasted_iota(jnp.int32, (1, SLOT), 1) == V_DIM).astype(F32)
        for h in range(MLA_HEADS):
            q_ref[:, h * SLOT:(h + 1) * SLOT] = qs[h].astype(q_ref.dtype)
            k_ref[:, h * SLOT:(h + 1) * SLOT] = ks[h].astype(k_ref.dtype)
            v_ref[:, h * SLOT:(h + 1) * SLOT] = (vs[h] + ones_lane).astype(v_ref.dtype)

    return pl.pallas_call(
        body, name="mla_pre_fwd", grid=(s // TM,),
        in_specs=[_rows(TM, pc.shape[1]), _rows(TM, SLOT), _rows(TM, SLOT), _rows(TM, SLOT),
                  _full(qag), _full(kvag), _full(qg), _full(kg), _full(wq), _full(wk), _full(wv)],
        out_specs=[_rows(TM, hw)] * 3, out_shape=[_sds((s, hw), MXU_DTYPE)] * 3,
        compiler_params=_cp(("parallel",)),
    )(pc, cos_t, sin_a, sin_b, qag, kvag, qg, kg, wq, wk, wv)


def mla_pre_bwd(pc, cos_t, sin_a, sin_b, qag, kvag, qg, kg, wq, wk, wv, dq, dk, dv):
    s = pc.shape[0]
    hw = MLA_HEADS * SLOT

    def body(pc_ref, cos_ref, sa_ref, sb_ref, qag_ref, kvag_ref, qg_ref, kg_ref, wq_ref, wk_ref, wv_ref,
             dq_ref, dk_ref, dv_ref, dpc_ref, dqag_ref, dkvag_ref, dqg_ref, dkg_ref, dwq_ref, dwk_ref, dwv_ref):
        first = pl.program_id(0) == 0
        diff, tables = _mla_pre_args(pc_ref, cos_ref, sa_ref, sb_ref, qag_ref, kvag_ref, qg_ref, kg_ref,
                                     wq_ref, wk_ref, wv_ref, True)
        _, vjp = jax.vjp(functools.partial(_mla_pre_fn, AD, tables), *diff)
        sl = lambda h: slice(h * SLOT, (h + 1) * SLOT)
        cot = ([dq_ref[:, sl(h)] for h in range(MLA_HEADS)], [dk_ref[:, sl(h)] for h in range(MLA_HEADS)],
               [dv_ref[:, sl(h)] for h in range(MLA_HEADS)])
        dcq, dckv, dkpe, dqag, dkvag, dqg, dkg, dwq, dwk, dwv = vjp(cot)
        dpc_ref[:, 0:Q_LORA] = dcq
        dpc_ref[:, Q_LORA:Q_LORA + KV_LORA] = dckv
        dpc_ref[:, Q_LORA + KV_LORA:Q_LORA + 2 * KV_LORA] = dkpe
        _acc(dqag_ref, dqag, first)
        _acc(dkvag_ref, dkvag, first)
        _acc(dqg_ref, dqg, first)
        _acc(dkg_ref, dkg, first)
        for h in range(MLA_HEADS):
            _acc(dwq_ref.at[:, sl(h)], dwq[h], first)
            _acc(dwk_ref.at[:, sl(h)], dwk[h], first)
            _acc(dwv_ref.at[:, sl(h)], dwv[h], first)

    return pl.pallas_call(
        body, name="mla_pre_bwd", grid=(s // TM,),
        in_specs=[_rows(TM, pc.shape[1]), _rows(TM, SLOT), _rows(TM, SLOT), _rows(TM, SLOT),
                  _full(qag), _full(kvag), _full(qg), _full(kg), _full(wq), _full(wk), _full(wv),
                  _rows(TM, hw), _rows(TM, hw), _rows(TM, hw)],
        out_specs=[_rows(TM, pc.shape[1]), _full(qag), _full(kvag), _full(qg), _full(kg),
                   _full(wq), _full(wk), _full(wv)],
        out_shape=[_sds(pc.shape), _sds(qag.shape), _sds(kvag.shape), _sds(qg.shape), _sds(kg.shape),
                   _sds(wq.shape), _sds(wk.shape), _sds(wv.shape)],
        compiler_params=_cp(("arbitrary",)),
    )(pc, cos_t, sin_a, sin_b, qag, kvag, qg, kg, wq, wk, wv, dq, dk, dv)


ATT_SCALE = QK_DIM ** -0.5
NEG_BIG = -1e30


def attn_fwd(q, k, v, riders=()):
    s = q.shape[0]
    nq = s // TQ
    hp = ATT_HEADS_PER_STEP
    sl = lambda j: slice(j * SLOT, (j + 1) * SLOT)

    wide = ATT_WIDE // TQ

    def body(q_ref, k_ref, v_ref, o_ref, lse_ref):
        qi = pl.program_id(1)
        lane = lax.broadcasted_iota(jnp.int32, (1, SLOT), 1)
        qs = [q_ref[:, sl(j)] for j in range(hp)]

        def step(ki, carry, n_tiles, masked):
            rk = pl.ds(pl.multiple_of(ki * TQ, TQ), n_tiles * TQ)
            if masked:
                row = lax.broadcasted_iota(jnp.int32, (TQ, n_tiles * TQ), 0) + (n_tiles - 1) * TQ
                col = lax.broadcasted_iota(jnp.int32, (TQ, n_tiles * TQ), 1)
            out = []
            for j in range(hp):
                m, acc = carry[j]
                sc = _dot(qs[j], k_ref[rk, sl(j)], NT) * ATT_SCALE
                if masked:
                    sc = jnp.where(row >= col, sc, NEG_BIG)
                m_new = jnp.maximum(m, jnp.max(sc, axis=-1, keepdims=True))
                acc = jnp.exp(m - m_new) * acc + _dot(jnp.exp(sc - m_new), v_ref[rk, sl(j)], NN)
                out.append((m_new, acc))
            return tuple(out)

        def tail_single(cr):
            cr = lax.fori_loop(n_wide * wide, qi, lambda ki, c: step(ki, c, 1, False), cr)
            return step(qi, cr, 1, True)

        n_wide = qi // wide
        init = tuple((jnp.full((TQ, 1), NEG_BIG, F32), jnp.zeros((TQ, SLOT), F32)) for _ in range(hp))
        carry = lax.fori_loop(0, n_wide, lambda kw, cr: step(kw * wide, cr, wide, False), init)
        carry = lax.cond(qi % wide == wide - 1, lambda cr: step(qi - (wide - 1), cr, wide, True), tail_single, carry)
        for j in range(hp):
            m, acc = carry[j]
            l = jnp.sum(jnp.where(lane == V_DIM, acc, 0.0), axis=-1, keepdims=True)
            o_ref[:, sl(j)] = jnp.where(lane < V_DIM, acc / l, 0.0)
            lse_ref[j] = m + jnp.log(l)

    head_col = pl.BlockSpec((s, hp * SLOT), lambda g, i: (0, g))
    tile = pl.BlockSpec((TQ, hp * SLOT), lambda g, i: (i, g))
    return _ride(
        body, riders, name="attn_fwd", grid=(MLA_HEADS // hp, nq),
        in_specs=[tile, head_col, head_col],
        out_specs=[tile, pl.BlockSpec((hp, TQ, 1), lambda g, i: (g, i, 0))],
        out_shape=[_sds((s, MLA_HEADS * SLOT)), _sds((MLA_HEADS, s, 1))],
        operands=(q, k, v), sem=("parallel", "parallel"))


def attn_bwd(q, k, v, o, do, lse, riders=()):
    s = q.shape[0]
    nq = s // TQ
    hp = ATT_HEADS_PER_STEP
    sl = lambda j: slice(j * SLOT, (j + 1) * SLOT)
    wide = ATT_WIDE // TQ

    def body(q_ref, k_ref, v_ref, o_ref, do_ref, lse_ref, dq_ref, dk_ref, dv_ref, delta_ref):
        ki = pl.program_id(1)

        @pl.when(ki == 0)
        def _():
            dq_ref[...] = jnp.zeros_like(dq_ref)

            def prep(i, c):
                rows = pl.ds(pl.multiple_of(i * TQ, TQ), TQ)
                for j in range(hp):
                    delta_ref[j, rows, :] = jnp.sum(do_ref[rows, sl(j)] * o_ref[rows, sl(j)], axis=-1, keepdims=True)
                return c

            lax.fori_loop(0, nq, prep, 0)

        kks = [k_ref[:, sl(j)] for j in range(hp)]
        vvs = [v_ref[:, sl(j)] for j in range(hp)]

        def step(qi, carry, n_tiles, masked):
            rq = pl.ds(pl.multiple_of(qi * TQ, TQ), n_tiles * TQ)
            if masked:
                row = lax.broadcasted_iota(jnp.int32, (n_tiles * TQ, TQ), 0)
                col = lax.broadcasted_iota(jnp.int32, (n_tiles * TQ, TQ), 1)
            out = []
            for j in range(hp):
                dk, dv = carry[j]
                qq = q_ref[rq, sl(j)]
                dd = do_ref[rq, sl(j)]
                sc = _dot(qq, kks[j], NT) * ATT_SCALE
                if masked:
                    sc = jnp.where(row >= col, sc, NEG_BIG)
                p = jnp.exp(sc - lse_ref[j, rq, :])
                dv = dv + _dot(p, dd, TN)
                ds = p * (_dot(dd, vvs[j], NT) - delta_ref[j, rq, :]) * ATT_SCALE
                dk = dk + _dot(ds, qq, TN)
                dq_ref[rq, sl(j)] = dq_ref[rq, sl(j)] + _dot(ds, kks[j], NN)
                out.append((dk, dv))
            return tuple(out)

        def head_single(cr):
            cr = step(ki, cr, 1, True)
            return lax.fori_loop(ki + 1, first_wide * wide, lambda qi, c: step(qi, c, 1, False), cr)

        zero = jnp.zeros((TQ, SLOT), F32)
        first_wide = (ki + wide) // wide
        carry = tuple((zero, zero) for _ in range(hp))
        carry = lax.cond(ki % wide == 0, lambda cr: step(ki, cr, wide, True), head_single, carry)
        carry = lax.fori_loop(first_wide, nq // wide, lambda qw, cr: step(qw * wide, cr, wide, False), carry)
        for j in range(hp):
            dk_ref[:, sl(j)] = carry[j][0]
            dv_ref[:, sl(j)] = carry[j][1]

    head_col = pl.BlockSpec((s, hp * SLOT), lambda g, i: (0, g))
    tile = pl.BlockSpec((TQ, hp * SLOT), lambda g, i: (i, g))
    return _ride(
        body, riders, name="attn_bwd", grid=(MLA_HEADS // hp, nq),
        in_specs=[head_col, tile, tile, head_col, head_col, pl.BlockSpec((hp, s, 1), lambda g, i: (g, 0, 0))],
        out_specs=[head_col, tile, tile],
        out_shape=[_sds((s, MLA_HEADS * SLOT))] * 3,
        scratch_shapes=[pltpu.VMEM((hp, s, 1), F32)],
        operands=(q, k, v, o, do, lse), sem=("arbitrary", "arbitrary"))


def _outproj_args(ya_ref, yb_ref, o_ref, mog_ref, woa_ref, wob_ref, woc_ref, cast):
    sl = lambda h: slice(h * SLOT, (h + 1) * SLOT)
    ldw = (lambda r: r[...].astype(F32)) if cast else (lambda r: r[...])
    return (ya_ref[...], yb_ref[...], [o_ref[:, sl(h)] for h in range(MLA_HEADS)],
            [mog_ref[:, sl(h)] for h in range(MLA_HEADS)], ldw(woa_ref), ldw(wob_ref), ldw(woc_ref))


def outproj_fwd(x, ya, yb, o, mog, woa, wob, woc):
    s, d = x.shape

    def body(x_ref, ya_ref, yb_ref, o_ref, mog_ref, woa_ref, wob_ref, woc_ref, x1_ref):
        x1_ref[...] = _outproj(PLAIN, x_ref[...], *_outproj_args(ya_ref, yb_ref, o_ref, mog_ref, woa_ref, wob_ref,
                                                                  woc_ref, False))

    return pl.pallas_call(
        body, name="outproj_fwd", grid=(s // TM,),
        in_specs=[_rows(TM, d), _rows(TM, ya.shape[1]), _rows(TM, yb.shape[1]), _rows(TM, o.shape[1]),
                  _full(mog), _full(woa), _full(wob), _full(woc)],
        out_specs=_rows(TM, d), out_shape=_sds((s, d)),
        compiler_params=_cp(("parallel",)),
    )(x, ya, yb, o, mog, woa, wob, woc)


def outproj_bwd(ya, yb, o, mog, woa, wob, woc, dx2, dx1p, riders=()):
    s, d = dx2.shape
    npart = dx1p.shape[0]

    def body(ya_ref, yb_ref, o_ref, mog_ref, woa_ref, wob_ref, woc_ref, dx2_ref, dx1p_ref,
             dx1_ref, dya_ref, dyb_ref, do_ref, dmog_ref, dwoa_ref, dwob_ref, dwoc_ref):
        first = pl.program_id(0) == 0
        sl = lambda h: slice(h * SLOT, (h + 1) * SLOT)
        dx1 = dx2_ref[...]
        for p in range(npart):
            dx1 = dx1 + dx1p_ref[p]
        dx1_ref[...] = dx1
        args = _outproj_args(ya_ref, yb_ref, o_ref, mog_ref, woa_ref, wob_ref, woc_ref, True)
        _, vjp = jax.vjp(lambda *a: _outproj(AD, jnp.zeros_like(dx1), *a), *args)
        dya, dyb, do, dmog, dwoa, dwob, dwoc = vjp(dx1)
        dya_ref[...] = dya
        dyb_ref[...] = dyb
        _acc(dwoa_ref, dwoa, first)
        _acc(dwob_ref, dwob, first)
        _acc(dwoc_ref, dwoc, first)
        for h in range(MLA_HEADS):
            do_ref[:, sl(h)] = do[h]
            _acc(dmog_ref.at[:, sl(h)], dmog[h], first)

    return _ride(
        body, riders, name="outproj_bwd", grid=(s // TM,),
        in_specs=[_rows(TM, ya.shape[1]), _rows(TM, yb.shape[1]), _rows(TM, o.shape[1]),
                  _full(mog), _full(woa), _full(wob), _full(woc), _rows(TM, d),
                  pl.BlockSpec((npart, TM, d), lambda i: (0, i, 0))],
        out_specs=[_rows(TM, d), _rows(TM, ya.shape[1]), _rows(TM, yb.shape[1]), _rows(TM, o.shape[1]),
                   _full(mog), _full(woa), _full(wob), _full(woc)],
        out_shape=[_sds((s, d)), _sds(ya.shape), _sds(yb.shape), _sds(o.shape),
                   _sds(mog.shape), _sds(woa.shape), _sds(wob.shape), _sds(woc.shape)],
        operands=(ya, yb, o, mog, woa, wob, woc, dx2, dx1p), sem=("arbitrary",))


def ffn_fwd(x1, g2, w1, w2, riders=()):
    s, d = x1.shape
    npart, _, fs = w1.shape

    def body(x1_ref, g_ref, w1_ref, w2_ref, x2_ref, r_ref):
        p = pl.program_id(1)
        x1v = x1_ref[...]
        r = jnp.maximum(PLAIN.mm(_rms(x1v, g_ref[...]), w1_ref[...]), 0.0)
        r_ref[...] = r.astype(r_ref.dtype)
        part = PLAIN.mm(r * r, w2_ref[...])

        @pl.when(p == 0)
        def _():
            x2_ref[...] = x1v + part

        @pl.when(p != 0)
        def _():
            x2_ref[...] = x2_ref[...] + part

    tm = min(2 * TM_FFN, s)
    return _ride(
        body, riders, name="ffn_fwd", grid=(s // tm, npart),
        in_specs=[pl.BlockSpec((tm, d), lambda i, p: (i, 0)), pl.BlockSpec(g2.shape, lambda i, p: (0, 0)),
                  pl.BlockSpec((None, d, fs), lambda i, p: (p, 0, 0)), pl.BlockSpec((None, fs, d), lambda i, p: (p, 0, 0))],
        out_specs=[pl.BlockSpec((tm, d), lambda i, p: (i, 0)), pl.BlockSpec((tm, fs), lambda i, p: (i, p))],
        out_shape=[_sds((s, d)), _sds((s, npart * fs), MXU_DTYPE)],
        operands=(x1, g2, w1, w2), sem=("parallel", "arbitrary"))


def ffn_bwd(x1, g2, w1, w2, r, dx2, riders=()):
    s, d = x1.shape
    npart, _, fs = w1.shape
    tm = TM_FFN

    def body(x1_ref, g_ref, w1_ref, w2_ref, r_ref, dx2_ref, dx1p_ref, dg_ref, dw1_ref, dw2_ref, dw1b_ref, dw2b_ref):
        p = pl.program_id(0)
        i = pl.program_id(1)
        h2, vjp_norm = jax.vjp(_rms, x1_ref[...], g_ref[...])
        rr = r_ref[...].astype(F32)
        dy = dx2_ref[...]
        da = _dot(dy, w2_ref[...], NT) * (2.0 * rr)
        dx1, dg = vjp_norm(_dot(da, w1_ref[...], NT))
        dx1p_ref[...] = dx1
        _acc(dg_ref, dg, (p == 0) & (i == 0))
        _acc(dw1_ref, _dot(h2, da, TN), i == 0)
        _acc(dw2_ref, _dot(rr * rr, dy, TN), i == 0)

        @pl.when(i == s // tm - 1)
        def _():
            dw1b_ref[...] = dw1_ref[...].astype(BF16)
            dw2b_ref[...] = dw2_ref[...].astype(BF16)

    return _ride(
        body, riders, name="ffn_bwd", grid=(npart, s // tm),
        in_specs=[pl.BlockSpec((tm, d), lambda p, i: (i, 0)), pl.BlockSpec(g2.shape, lambda p, i: (0, 0)),
                  pl.BlockSpec((None, d, fs), lambda p, i: (p, 0, 0)), pl.BlockSpec((None, fs, d), lambda p, i: (p, 0, 0)),
                  pl.BlockSpec((tm, fs), lambda p, i: (i, p)), pl.BlockSpec((tm, d), lambda p, i: (i, 0))],
        out_specs=[pl.BlockSpec((None, tm, d), lambda p, i: (p, i, 0)), pl.BlockSpec(g2.shape, lambda p, i: (0, 0)),
                   pl.BlockSpec((None, d, fs), lambda p, i: (p, 0, 0)), pl.BlockSpec((None, fs, d), lambda p, i: (p, 0, 0)),
                   pl.BlockSpec((None, d, fs), lambda p, i: (p, 0, 0)), pl.BlockSpec((None, fs, d), lambda p, i: (p, 0, 0))],
        out_shape=[_sds((npart, s, d)), _sds(g2.shape), _sds(w1.shape), _sds(w2.shape),
                   _sds(w1.shape, BF16), _sds(w2.shape, BF16)],
        operands=(x1, g2, w1, w2, r, dx2), sem=("arbitrary", "arbitrary"))


def loss_head(y, target):
    s, d = y.shape

    def body(y_ref, t_ref, dy_ref, loss_ref):
        err = y_ref[...] - t_ref[...]
        dy_ref[...] = err * (1.0 / d)
        part = jnp.sum(jnp.sum(err * err, axis=-1, keepdims=True), axis=0, keepdims=True) * (0.5 / d)
        _acc(loss_ref, jnp.broadcast_to(part, loss_ref.shape), pl.program_id(0) == 0)

    return pl.pallas_call(
        body, name="loss_head", grid=(s // TM,),
        in_specs=[_rows(TM, d), _rows(TM, d)],
        out_specs=[_rows(TM, d), pl.BlockSpec((1, SLOT), lambda i: (0, 0))],
        out_shape=[_sds((s, d)), _sds((1, SLOT))],
        compiler_params=_cp(("arbitrary",)),
    )(y, target)


def _row_block(r):
    for b in (512, 256, 128, 64, 32, 16, 8):
        if r % b == 0:
            return b
    return r


def sum_cores(arrs, gots, half, me):
    n = len(arrs)

    def body(sp_ref, *refs):
        for i in range(n):
            a_ref, g_ref, wire_ref, own_ref = refs[i], refs[n + i], refs[2 * n + i], refs[3 * n + i]
            tot = a_ref[...] + g_ref[...]
            wire_ref[...] = tot.astype(wire_ref.dtype)

            @pl.when(pl.program_id(0) == sp_ref[1])
            def _(own_ref=own_ref, tot=tot):
                own_ref[...] = tot

    shapes = [a.shape[2:] for a in arrs]
    grid_spec = pltpu.PrefetchScalarGridSpec(
        num_scalar_prefetch=1, grid=(N_CHIPS,),
        in_specs=[pl.BlockSpec((None, None) + sh, lambda p, sp: (p, sp[0], 0, 0)) for sh in shapes]
        + [pl.BlockSpec((None,) + sh, lambda p, sp: (p, 0, 0)) for sh in shapes],
        out_specs=[pl.BlockSpec((None,) + sh, lambda p, sp: (p, 0, 0)) for sh in shapes]
        + [pl.BlockSpec(sh, lambda p, sp: (0, 0)) for sh in shapes])
    outs = pl.pallas_call(body, name="sum_cores", grid_spec=grid_spec,
                          out_shape=[_sds((N_CHIPS,) + sh, BF16) for sh in shapes] + [_sds(sh) for sh in shapes],
                          compiler_params=_cp(("arbitrary",)))(jnp.stack([half, me]).astype(jnp.int32), *arrs, *gots)
    return outs[:n], outs[n:]


SUM_STEPS = 4


def sum_chips(owns, recvs, half, places):
    n = len(owns)
    dests = [d for _, _, d in places if d is not None]

    def body(sp_ref, *refs):
        del sp_ref
        for i in range(n):
            own_ref, out_ref = refs[4 * i], refs[4 * n + len(dests) + i]
            r0, r1, r2 = (refs[4 * i + 1 + j][...].astype(F32) for j in range(3))
            out_ref[...] = ((own_ref[...] + r0) + r1) + r2

    in_specs, out_specs, operands, aliases = [], [], [], {}
    for i, (own, recv, (layer, _, dest)) in enumerate(zip(owns, recvs, places)):
        r, c = own.shape
        br = r // SUM_STEPS
        in_specs.append(pl.BlockSpec((br, c), lambda i, sp: (i, 0)))
        in_specs += [pl.BlockSpec((None, br, c), functools.partial(lambda i, sp, j: (j, i, 0), j=j)) for j in range(3)]
        out_specs.append(pl.BlockSpec((None, None, br, c), functools.partial(lambda i, sp, l: (l, sp[0], i, 0), l=layer)))
        operands += [own, recv, recv, recv]
        if dest is not None:
            aliases[1 + 4 * n + len(aliases)] = i
    grid_spec = pltpu.PrefetchScalarGridSpec(num_scalar_prefetch=1, grid=(SUM_STEPS,), in_specs=in_specs + [ANY] * len(dests),
                                             out_specs=out_specs)
    return pl.pallas_call(body, name="sum_chips", grid_spec=grid_spec, input_output_aliases=aliases,
                          out_shape=[_sds((nl, 2) + o.shape) for o, (_, nl, _) in zip(owns, places)],
                          compiler_params=_cp(("parallel",)))(half.reshape(1).astype(jnp.int32), *operands, *dests)


def adamw(w, g, m, v, name, riders=()):
    r, c = w.shape
    br = _row_block(r)
    c1 = 1.0 / (1.0 - ADAM_B1 ** ADAM_STEP)
    c2 = 1.0 / (1.0 - ADAM_B2 ** ADAM_STEP)

    def body(w_ref, g_ref, m_ref, v_ref, d_ref, nm_ref, nv_ref):
        gg = g_ref[...]
        nm = ADAM_B1 * m_ref[...] + (1.0 - ADAM_B1) * gg
        nv = ADAM_B2 * v_ref[...] + (1.0 - ADAM_B2) * (gg * gg)
        d_ref[...] = -ADAM_LR * ((nm * c1) / (jnp.sqrt(nv * c2) + ADAM_EPS) + ADAM_WD * w_ref[...])
        nm_ref[...] = nm
        nv_ref[...] = nv

    return _ride(body, riders, name=name, grid=(r // br,), in_specs=[_rows(br, c)] * 4, out_specs=[_rows(br, c)] * 3,
                 out_shape=[_sds((r, c))] * 3, operands=(w, g, m, v), sem=("parallel",))


def _place():
    x, y, c = lax.axis_index("x"), lax.axis_index("y"), lax.axis_index("c")
    chips = [(1 - x, y), (x, 1 - y), (1 - x, 1 - y)]
    return x, y, c, chips


def _remote(src, dst, send_sem, recv_sem, to):
    return pltpu.make_async_remote_copy(src_ref=src, dst_ref=dst, send_sem=send_sem, recv_sem=recv_sem,
                                        device_id=to, device_id_type=MESH)


def gather_rider(arrs):
    n = len(arrs)
    me_chip = 2 * lax.axis_index("x") + lax.axis_index("y")
    bufs = [lax.dynamic_update_index_in_dim(lax.empty((N_CHIPS,) + a.shape, a.dtype), a, me_chip, 0) for a in arrs]

    def plan(ins, outs, sems):
        send_sems, recv_sems = sems
        x, y, c, chips = _place()
        me = 2 * x + y
        half, other, sibling = pl.ds(2 * c, 2), pl.ds(2 - 2 * c, 2), (x, y, 1 - c)
        cp = lambda i, k, src, dst, to: _remote(src, dst, send_sems.at[i, k], recv_sems.at[i, k], to)
        pairs = [(i, j, cx, cy) for i in range(n) for j, (cx, cy) in enumerate(chips)]
        blk = lambda i, cx, cy, part: outs[i].at[2 * cx + cy, part]
        first = lambda: [cp(i, j, ins[i].at[half], outs[i].at[me, half], (cx, cy, c)) for i, j, cx, cy in pairs]
        landed = lambda: [cp(i, j, blk(i, cx, cy, half), blk(i, cx, cy, half), (cx, cy, c)) for i, j, cx, cy in pairs]
        passed = lambda: [cp(i, 3 + j, blk(i, cx, cy, half), blk(i, cx, cy, half), sibling) for i, j, cx, cy in pairs]
        from_sibling = lambda: [cp(i, 3 + j, blk(i, cx, cy, other), blk(i, cx, cy, other), sibling) for i, j, cx, cy in pairs]
        return first, landed, passed, from_sibling

    def start(ins, outs, sems):
        for cp in plan(ins, outs, sems)[0]():
            cp.start()

    def finish(ins, outs, sems):
        first, landed, passed, from_sibling = plan(ins, outs, sems)
        forwards = passed()
        for a, b in zip(landed(), forwards):
            a.wait_recv()
            b.start()
        for cp in from_sibling():
            cp.wait_recv()
        for cp in first() + forwards:
            cp.wait_send()

    return Rider(list(arrs) + bufs, [_sds((N_CHIPS,) + a.shape, a.dtype) for a in arrs], {n + i: i for i in range(n)},
                 [pltpu.SemaphoreType.DMA((n, 6)), pltpu.SemaphoreType.DMA((n, 6))], start, finish)


class Reducer:
    def __init__(self, arrs, places=None, send=None):
        self.a = list(arrs)
        self.send = list(send) if send is not None else self.a
        self.n = len(self.a)
        self.places = places if places is not None else [(0, 1, None)] * self.n
        self.c = lax.axis_index("c")
        self.me = 2 * lax.axis_index("x") + lax.axis_index("y")

    def swap_rider(self):
        n = self.n

        def plan(ins, outs, sems):
            x, y, c, _ = _place()
            return [_remote(ins[i].at[p, 1 - c], outs[i].at[p], sems[0].at[i, p], sems[1].at[i, p], (x, y, 1 - c))
                    for i in range(n) for p in range(N_CHIPS)]

        return Rider(self.send, [_sds((N_CHIPS,) + a.shape[2:], a.dtype) for a in self.send], {},
                     [pltpu.SemaphoreType.DMA((n, N_CHIPS)), pltpu.SemaphoreType.DMA((n, N_CHIPS))],
                     lambda *r: [cp.start() for cp in plan(*r)], lambda *r: [cp.wait() for cp in plan(*r)])

    def after_swap(self, got):
        self.wire, self.own = sum_cores(self.a, got, self.c, self.me)

    def scatter_rider(self):
        n = self.n

        def plan(ins, outs, sems):
            x, y, c, chips = _place()
            return [_remote(ins[i].at[2 * cx + cy], outs[i].at[j], sems[0].at[i, j], sems[1].at[i, j], (cx, cy, c))
                    for i in range(n) for j, (cx, cy) in enumerate(chips)]

        return Rider(self.wire, [_sds((3,) + w.shape[1:], w.dtype) for w in self.wire], {},
                     [pltpu.SemaphoreType.DMA((n, 3)), pltpu.SemaphoreType.DMA((n, 3))],
                     lambda *r: [cp.start() for cp in plan(*r)], lambda *r: [cp.wait() for cp in plan(*r)])

    def after_scatter(self, recv):
        if callable(self.places):
            self.places = self.places()
        self.full = sum_chips(self.own, recv, self.c, self.places)

    def share_rider(self):
        n = self.n
        layers = [layer for layer, _, _ in self.places]

        def plan(ins, outs, sems):
            x, y, c, _ = _place()
            return [_remote(ins[i].at[layers[i], c], outs[i].at[layers[i], c], sems[0].at[i], sems[1].at[i], (x, y, 1 - c))
                    for i in range(n)]

        return Rider(self.full, [_sds(f.shape) for f in self.full], {i: i for i in range(n)},
                     [pltpu.SemaphoreType.DMA((n,)), pltpu.SemaphoreType.DMA((n,))],
                     lambda *r: [cp.start() for cp in plan(*r)], lambda *r: [cp.wait() for cp in plan(*r)])

    def run(self):
        self.after_swap(run_rider(self.swap_rider(), "swap_halves"))
        self.after_scatter(run_rider(self.scatter_rider(), "scatter_chips"))
        return run_rider(self.share_rider(), "share_halves")


def _pad_slots(a, live):
    lead = a.shape[:-1]
    a = a.reshape(lead + (MLA_HEADS, live))
    a = jnp.pad(a, [(0, 0)] * len(lead) + [(0, 0), (0, SLOT - live)])
    return a.reshape(lead + (MLA_HEADS * SLOT,))


def _unpad_slots(a, live):
    lead = a.shape[:-1]
    return a.reshape(lead + (MLA_HEADS, SLOT))[..., :live].reshape(lead + (MLA_HEADS * live,))


def _rope_tables(positions, s):
    half = QK_ROPE // 2
    inv_freq = ROPE_THETA ** (-jnp.arange(half, dtype=F32) / half)
    ang = positions.reshape(s).astype(F32)[:, None] * inv_freq[None, :]
    cos, sin = jnp.cos(ang), jnp.sin(ang)
    one = jnp.ones((s, QK_NOPE), F32)
    z64, z16, z32 = jnp.zeros((s, QK_NOPE), F32), jnp.zeros((s, half), F32), jnp.zeros((s, SLOT - QK_DIM), F32)
    cos_t = jnp.concatenate([one, cos, cos, z32], axis=1)
    sin_a = jnp.concatenate([z64, -sin, z16, z32], axis=1)
    sin_b = jnp.concatenate([z64, z16, sin, z32], axis=1)
    return cos_t, sin_a, sin_b


def _out_weights(full):
    w_out = jnp.concatenate([full["w_out"][p] for p in range(N_CHIPS)], axis=0)
    woc = w_out[512:].reshape(MLA_HEADS, V_DIM, D_MODEL)
    woc = jnp.pad(woc, ((0, 0), (0, SLOT - V_DIM), (0, 0))).reshape(MLA_HEADS * SLOT, D_MODEL)
    return dict(woa=w_out[:256], wob=w_out[256:512], woc=woc)


def _layer_weights(full, small, l):
    w_in = jnp.concatenate([full["w_in"][p] for p in range(N_CHIPS)], axis=1)
    wc = jnp.pad(w_in[:, 1536:], ((0, 0), (0, 512 - (w_in.shape[1] - 1536))))
    w_uq = jnp.concatenate([full["mla_w_uq"][p] for p in range(N_CHIPS)], axis=1)
    w_ukv = jnp.concatenate([full["mla_w_ukv"][p] for p in range(N_CHIPS)], axis=1)
    ukv = w_ukv.reshape(KV_LORA, MLA_HEADS, QK_NOPE + V_DIM)
    row = lambda a: a.reshape(1, -1)
    return dict(
        g1=row(small["norm1_gain"][l]), wa=w_in[:, :512], wb=w_in[:, 512:1536], wc=wc,
        vg=row(small["gm_v_gain"][l]), ws=small["gm_w_s"][l], bs=small["gm_b_s"][l].reshape(4, CHUNK, 1),
        gog=row(small["gm_out_gain"][l]), hog=small["hg_out_gain"][l].reshape(-1, 1),
        qag=row(small["mla_q_a_gain"][l]), kvag=row(small["mla_kv_a_gain"][l]),
        qg=row(jnp.pad(small["mla_q_gain"][l], (0, SLOT - QK_DIM))), kg=row(jnp.pad(small["mla_k_gain"][l], (0, SLOT - QK_DIM))),
        wq=_pad_slots(w_uq, QK_DIM), wk=_pad_slots(ukv[..., :QK_NOPE].reshape(KV_LORA, -1), QK_NOPE),
        wv=_pad_slots(ukv[..., QK_NOPE:].reshape(KV_LORA, -1), V_DIM),
        mog=row(_pad_slots(small["mla_out_gain"][l], V_DIM)),
        g2=row(small["norm2_gain"][l]),
    )


def _shard_cols(a):
    r, c4 = a.shape
    return a.reshape(r, N_CHIPS, c4 // N_CHIPS).transpose(1, 0, 2)


def local_step(x, positions, target, small, comm):
    s = x.shape[0]
    cos_t, sin_a, sin_b = _rope_tables(positions, s)
    lbs = lower_bounds_fwd(small["hg_lower_bound"])
    lw, saved = [], []
    for l in range(DEPTH):
        w = _layer_weights(comm.part(l, "in"), small, l)
        lw.append(w)
        lb = lbs[l].reshape(-1, 1)
        pa, pb, pc = inproj_fwd(x, w["g1"], w["wa"], w["wb"], w["wc"])
        ya = gm_fwd(pa, w["vg"], w["ws"], w["bs"], w["gog"])
        (yb, states), got = hg_fwd(pb, lb, w["hog"], [comm.gather_rider(l, "ff1")])
        comm.gathered(l, "ff1", got[0])
        q, k, v = mla_pre_fwd(pc, cos_t, sin_a, sin_b, w["qag"], w["kvag"], w["qg"], w["kg"], w["wq"], w["wk"], w["wv"])
        (o, lse), got = attn_fwd(q, k, v, [comm.gather_rider(l, "ff2"), comm.gather_rider(l, "out")])
        comm.gathered(l, "ff2", got[0])
        comm.gathered(l, "out", got[1])
        w.update(_out_weights(comm.part(l, "out")))
        x1 = outproj_fwd(x, ya, yb, o, w["mog"], w["woa"], w["wob"], w["woc"])
        w["w1"], w["w2"] = comm.part(l, "ff1")["w_ff1"], comm.part(l, "ff2")["w_ff2"]
        rider = comm.gather_rider(l + 1, "in") if l + 1 < DEPTH else None
        (x2, r), got = ffn_fwd(x1, w["g2"], w["w1"], w["w2"], [rider])
        comm.gathered(l + 1, "in", got[0])
        saved.append(dict(x=x, pa=pa, pb=pb, pc=pc, ya=ya, yb=yb, states=states, q=q, k=k, v=v, o=o, lse=lse, x1=x1, r=r, lb=lb))
        x = x2
    dx, loss_part = loss_head(x, target)
    groups = [dict() for _ in range(DEPTH)]
    sm = {n: [None] * DEPTH for n in ("norm1_gain", "gm_v_gain", "gm_w_s", "gm_b_s", "gm_out_gain", "hg_out_gain",
                                       "mla_q_a_gain", "mla_kv_a_gain", "mla_q_gain", "mla_k_gain", "mla_out_gain",
                                       "norm2_gain")}
    dlbs = [None] * DEPTH
    halves = lambda g: g.reshape(N_CHIPS, 2, g.shape[1] // 2, g.shape[2])
    take = lambda red, f: None if red is None else f(red)
    red_mix = red_late = None
    for l in reversed(range(DEPTH)):
        w, a = lw[l], saved[l]
        (dx1p, dg2, dw1, dw2, dw1b, dw2b), got = ffn_bwd(a["x1"], w["g2"], w["w1"], w["w2"], a["r"], dx,
                                             [take(red_mix, Reducer.swap_rider), take(red_late, Reducer.share_rider)])
        if red_mix:
            red_mix.after_swap(got[0])
        if red_late:
            comm.reduced(MIX, got[1])
        ffn_arrs = [halves(dw1), halves(dw2)]
        red_ffn = comm.reducer(ffn_arrs, FFN, l, send=[halves(dw1b), halves(dw2b)])
        (dx1, dya, dyb, do, dmog, dwoa, dwob, dwoc), got = outproj_bwd(
            a["ya"], a["yb"], a["o"], w["mog"], w["woa"], w["wob"], w["woc"], dx, dx1p, [take(red_ffn, Reducer.swap_rider)])
        if red_ffn:
            red_ffn.after_swap(got[0])
        (dq, dk, dv), got = attn_bwd(a["q"], a["k"], a["v"], a["o"], do, a["lse"], [take(red_ffn, Reducer.scatter_rider)])
        if red_ffn:
            red_ffn.after_scatter(got[0])
        dpc, dqag, dkvag, dqg, dkg, dwq, dwk, dwv = mla_pre_bwd(a["pc"], cos_t, sin_a, sin_b, w["qag"], w["kvag"], w["qg"],
                                                                  w["kg"], w["wq"], w["wk"], w["wv"], dq, dk, dv)
        (dpb, dlb, dhog), got = hg_bwd(a["pb"], a["lb"], w["hog"], a["states"], dyb,
                                       [take(red_ffn, Reducer.share_rider), take(red_mix, Reducer.scatter_rider)])
        if red_ffn:
            comm.reduced(FFN, got[0])
        else:
            groups[l].update(zip(FFN, ffn_arrs))
        if red_mix:
            red_mix.after_scatter(got[1])
        red_late = red_mix
        dpa, dvg, dws, dbs, dgog = gm_bwd(a["pa"], w["vg"], w["ws"], w["bs"], w["gog"], dya)
        dx, dg1, dwa, dwb, dwc = inproj_bwd(a["x"], w["g1"], w["wa"], w["wb"], w["wc"], dpa, dpb, dpc, dx1)
        dukv = jnp.concatenate([dwk.reshape(KV_LORA, MLA_HEADS, SLOT)[..., :QK_NOPE],
                                dwv.reshape(KV_LORA, MLA_HEADS, SLOT)[..., :V_DIM]], axis=-1)
        dwo = jnp.concatenate([dwoa, dwob, dwoc.reshape(MLA_HEADS, SLOT, D_MODEL)[:, :V_DIM].reshape(-1, D_MODEL)], axis=0)
        mix_arrs = [halves(_shard_cols(jnp.concatenate([dwa, dwb, dwc[:, :1952 - 1536]], axis=1))),
                    halves(_shard_cols(_unpad_slots(dwq, QK_DIM))), halves(_shard_cols(dukv.reshape(KV_LORA, -1))),
                    halves(dwo.reshape(N_CHIPS, -1, D_MODEL))]
        red_mix = comm.reducer(mix_arrs, MIX, l) if l > 0 else None
        if red_mix is None:
            groups[l].update(zip(MIX, mix_arrs))
        sm["norm1_gain"][l] = dg1[0]
        sm["gm_v_gain"][l] = dvg[0]
        sm["gm_w_s"][l] = dws
        sm["gm_b_s"][l] = dbs[..., 0]
        sm["gm_out_gain"][l] = dgog[0]
        sm["hg_out_gain"][l] = dhog[:, 0]
        sm["mla_q_a_gain"][l] = dqag[0]
        sm["mla_kv_a_gain"][l] = dkvag[0]
        sm["mla_q_gain"][l] = dqg[0, :QK_DIM]
        sm["mla_k_gain"][l] = dkg[0, :QK_DIM]
        sm["mla_out_gain"][l] = _unpad_slots(dmog[0], V_DIM)
        sm["norm2_gain"][l] = dg2[0]
        dlbs[l] = dlb[:, 0]
    sm["hg_lower_bound"] = [lower_bounds_bwd(small["hg_lower_bound"], jnp.stack(dlbs))]
    return loss_part, dx, groups, sm, red_late


MIX = ("w_in", "mla_w_uq", "mla_w_ukv", "w_out")
FFN = ("w_ff1", "w_ff2")
BIG = MIX + FFN
PARTS = {"in": ("w_in", "mla_w_uq", "mla_w_ukv"), "out": ("w_out",), "ff1": ("w_ff1",), "ff2": ("w_ff2",)}
SMALL = ("norm1_gain", "gm_v_gain", "gm_w_s", "gm_b_s", "gm_out_gain", "hg_lower_bound", "hg_out_gain",
         "mla_q_a_gain", "mla_kv_a_gain", "mla_q_gain", "mla_k_gain", "mla_out_gain", "norm2_gain")
ORDER = ("norm1_gain", "w_in", "gm_v_gain", "gm_w_s", "gm_b_s", "gm_out_gain", "hg_lower_bound", "hg_out_gain",
         "mla_q_a_gain", "mla_w_uq", "mla_kv_a_gain", "mla_w_ukv", "mla_q_gain", "mla_k_gain", "mla_out_gain",
         "w_out", "norm2_gain", "w_ff1", "w_ff2")
PACK_ROWS = 320


def _pack(pieces):
    flat = jnp.concatenate([a.reshape(-1) for a in pieces])
    total = 2 * N_CHIPS * PACK_ROWS * SLOT
    return jnp.pad(flat, (0, total - flat.shape[0]))


def _unpack(flat, shapes):
    out, off = [], 0
    for sh in shapes:
        size = 1
        for d in sh:
            size *= d
        out.append(flat[off:off + size].reshape(sh))
        off += size
    return out


class ChipComm:
    def __init__(self, shards):
        self.shards = shards
        self.full = {}
        self.grads = {}

    def gather_rider(self, l, part):
        return gather_rider([self.shards[n][l].astype(MXU_DTYPE).reshape(4, self.shards[n].shape[1] // 4, -1)
                             for n in PARTS[part]])

    def gathered(self, l, part, outs):
        if outs is not None:
            self.full[l, part] = {n: o.reshape((N_CHIPS,) + self.shards[n].shape[1:]) for n, o in zip(PARTS[part], outs)}

    def part(self, l, part):
        if (l, part) not in self.full:
            self.gathered(l, part, run_rider(self.gather_rider(l, part), "gather_weights"))
        return self.full[l, part]

    def places(self, names, l):
        return [(l, self.shards[n].shape[0], self.grads.get(n)) for n in names]

    def reducer(self, arrs, names, l, send=None):
        return Reducer(arrs, lambda: self.places(names, l), send)

    def reduced(self, names, outs):
        self.grads.update(zip(names, outs))


def kernel(x, positions, norm1_gain, w_in, gm_v_gain, gm_w_s, gm_b_s, gm_out_gain, hg_lower_bound, hg_out_gain, mla_q_a_gain, mla_w_uq, mla_kv_a_gain, mla_w_ukv, mla_q_gain, mla_k_gain, mla_out_gain, w_out, norm2_gain, w_ff1, w_ff2, loss_target, m_norm1_gain, m_w_in, m_gm_v_gain, m_gm_w_s, m_gm_b_s, m_gm_out_gain, m_hg_lower_bound, m_hg_out_gain, m_mla_q_a_gain, m_mla_w_uq, m_mla_kv_a_gain, m_mla_w_ukv, m_mla_q_gain, m_mla_k_gain, m_mla_out_gain, m_w_out, m_norm2_gain, m_w_ff1, m_w_ff2, v_norm1_gain, v_w_in, v_gm_v_gain, v_gm_w_s, v_gm_b_s, v_gm_out_gain, v_hg_lower_bound, v_hg_out_gain, v_mla_q_a_gain, v_mla_w_uq, v_mla_kv_a_gain, v_mla_w_ukv, v_mla_q_gain, v_mla_k_gain, v_mla_out_gain, v_w_out, v_norm2_gain, v_w_ff1, v_w_ff2):
    given = dict(locals())
    weights = {n: given[n] for n in ORDER}
    moms = {n: given["m_" + n] for n in ORDER}
    vars_ = {n: given["v_" + n] for n in ORDER}
    s, d = x.shape[1], x.shape[2]

    small = {n: weights[n] for n in SMALL}
    comm = ChipComm({n: weights[n] for n in BIG})
    loss_part, dx, groups, small_g, red_late = local_step(x.reshape(s, d), positions, loss_target.reshape(s, d), small, comm)
    loss = lax.psum(loss_part[0, 0], ("x", "y", "c"))

    pack_g = _pack([jnp.stack(small_g[n]) for n in SMALL]).reshape(N_CHIPS, 2, PACK_ROWS, SLOT)
    last = Reducer([groups[0][n] for n in MIX] + [pack_g], lambda: comm.places(MIX, 0) + [(0, 1, None)])
    swap = last.swap_rider()
    outs = run_rider(_merge_riders([swap, red_late.share_rider()]), "swap_halves")
    last.after_swap(outs[:len(swap.out_shapes)])
    comm.reduced(MIX, outs[len(swap.out_shapes):])
    last.after_scatter(run_rider(last.scatter_rider(), "scatter_chips"))
    reduced = run_rider(last.share_rider(), "share_halves")
    comm.reduced(MIX, reduced[:-1])
    pack_full = run_rider(gather_rider([reduced[-1].reshape(4, PACK_ROWS // 2, SLOT)]), "gather_small")[0].reshape(-1)
    grads = {n: comm.grads[n].reshape(weights[n].shape) for n in BIG}
    grads.update(zip(SMALL, _unpack(pack_full, [weights[n].shape for n in SMALL])))

    delta, new_m, new_v = {}, {}, {}
    flat2 = lambda a: a.reshape(-1, a.shape[-1])
    for n in ORDER:
        outs, _ = adamw(flat2(weights[n]), flat2(grads[n]), flat2(moms[n]), flat2(vars_[n]), "adamw_" + n)
        delta[n], new_m[n], new_v[n] = [o.reshape(weights[n].shape) for o in outs]

    return (loss, dx.reshape(x.shape), *[grads[n] for n in ORDER], *[delta[n] for n in ORDER],
            *[new_m[n] for n in ORDER], *[new_v[n] for n in ORDER])
```

```python
import functools

import jax
import jax.numpy as jnp
from jax import lax
from jax.experimental import pallas as pl
from jax.experimental.pallas import tpu as pltpu

F32 = jnp.float32
BF16 = jnp.bfloat16
MXU_DTYPE = BF16

D_MODEL = 1024
DEPTH = 4
CHUNK = 128
HG_CHUNK = 128
HG_CHUNKS = 4
GM_CHUNKS = 4
EPS = 1e-6
HEAD64 = 64
MLA_HEADS = 8
QK_NOPE = 64
QK_ROPE = 32
QK_DIM = 96
V_DIM = 64
Q_LORA = 256
KV_LORA = 128
SLOT = 128
ROPE_THETA = 10000.0
N_CHIPS = 4

ADAM_LR = 0.001
ADAM_B1 = 0.9
ADAM_B2 = 0.999
ADAM_EPS = 1e-08
ADAM_WD = 0.01
ADAM_STEP = 10

TM = 512
TM_FFN = 512
TQ = 256
ATT_HEADS_PER_STEP = 4
ATT_WIDE = 512
V7X_VMEM_BYTES = 64 * 1024 * 1024
VMEM_LIMIT = V7X_VMEM_BYTES * 7 // 8

NN = (((1,), (0,)), ((), ()))
NT = (((1,), (1,)), ((), ()))
TN = (((0,), (0,)), ((), ()))
BNN = (((2,), (1,)), ((0,), (0,)))
BNT = (((2,), (2,)), ((0,), (0,)))
BTN = (((1,), (1,)), ((0,), (0,)))


def _dot(a, b, dims):
    return lax.dot_general(a.astype(MXU_DTYPE), b.astype(MXU_DTYPE), dims, preferred_element_type=F32)


def _hdot(a, b, dims=NN):
    return lax.dot_general(a, b, dims, precision=lax.Precision.HIGHEST, preferred_element_type=F32)


def _make_ad(dims, da_dims, da_swap, db_dims, db_swap):
    @jax.custom_vjp
    def f(a, b):
        return _dot(a, b, dims)

    def fwd(a, b):
        return _dot(a, b, dims), (a, b)

    def bwd(res, g):
        a, b = res
        da = _dot(b, g, da_dims) if da_swap else _dot(g, b, da_dims)
        db = _dot(g, a, db_dims) if db_swap else _dot(a, g, db_dims)
        return da, db

    f.defvjp(fwd, bwd)
    return f


@functools.partial(jax.custom_vjp, nondiff_argnums=(1,))
def _roll_ad(x, shift):
    return pltpu.roll(x, shift, 1)


def _roll_ad_fwd(x, shift):
    return pltpu.roll(x, shift, 1), None


def _roll_ad_bwd(shift, _, g):
    return (pltpu.roll(g, (g.shape[1] - shift) % g.shape[1], 1),)


_roll_ad.defvjp(_roll_ad_fwd, _roll_ad_bwd)


class _Ops:
    pass


PLAIN = _Ops()
PLAIN.mm = lambda a, b: _dot(a, b, NN)
PLAIN.bmm = lambda a, b: _dot(a, b, BNN)
PLAIN.bmm_nt = lambda a, b: _dot(a, b, BNT)
PLAIN.bmm_tn = lambda a, b: _dot(a, b, BTN)
PLAIN.roll = lambda x, s: pltpu.roll(x, s, 1)

AD = _Ops()
AD.mm = _make_ad(NN, NT, False, TN, False)
AD.bmm = _make_ad(BNN, BNT, False, BTN, False)
AD.bmm_nt = _make_ad(BNT, BNN, False, BTN, True)
AD.bmm_tn = _make_ad(BTN, BNT, True, BNN, False)
AD.roll = _roll_ad


def _sigmoid(x):
    return jax.nn.sigmoid(x)


def _gelu(x):
    return 0.5 * x * (1.0 + jnp.tanh(0.7978845608028654 * (x + 0.044715 * (x * x * x))))


def _rms(x, g):
    return x * lax.rsqrt(jnp.mean(x * x, axis=-1, keepdims=True) + EPS) * g


def _head_masks256():
    lane = lax.broadcasted_iota(jnp.int32, (1, 4 * HEAD64), 1)
    return [(jnp.right_shift(lane, 6) == h).astype(F32) for h in range(4)]


def _headnorm256(x, g):
    ms = jnp.zeros_like(x)
    sq = x * x
    for m in _head_masks256():
        ms = ms + m * (jnp.sum(sq * m, axis=-1, keepdims=True) * (1.0 / HEAD64))
    return x * lax.rsqrt(ms + EPS) * g


def _slot_norm(x, g, n):
    return x * lax.rsqrt(jnp.sum(x * x, axis=-1, keepdims=True) * (1.0 / n) + EPS) * g


def _rope(ops, x, cos_t, sin_a, sin_b):
    return x * cos_t + ops.roll(x, SLOT - QK_ROPE // 2) * sin_a + ops.roll(x, QK_ROPE // 2) * sin_b


def _inproj(ops, x, g1, wa, wb, wc):
    h = _rms(x, g1)
    return ops.mm(h, wa), ops.mm(h, wb), ops.mm(h, wc)


def _gm_chunk(ops, ur, vr, vg, ws4, bs, og):
    c = ur.shape[0]
    masks = _head_masks256()
    mh = jnp.concatenate([m[None] for m in masks], axis=0)
    u = _gelu(ur)
    v = _headnorm256(_gelu(vr), vg)
    t = lax.broadcasted_iota(jnp.int32, (c, c), 0)
    s = lax.broadcasted_iota(jnp.int32, (c, c), 1)
    w = jnp.where((t >= s)[None], ws4, 0.0)
    y = jnp.sum(ops.bmm(w, v[None] * mh), axis=0)
    for h in range(4):
        y = y + bs[h] * masks[h]
    return _headnorm256(u * y, og)


def _hg_chunk(ops, st, qr, fr, ir, gr, lb, og):
    c, n = qr.shape
    nh = n // HEAD64
    heads = lambda x: x.reshape(nh, HEAD64, x.shape[-1])
    tr = lambda x: heads(x.T)
    lb4, og4 = heads(lb), heads(og)
    qx = tr(qr)
    q = qx * _sigmoid(qx)
    f = lb4 + (1.0 - lb4) * _sigmoid(tr(fr))
    k = 1.0 - f
    logf = jnp.log(f)
    v = tr(ir)
    gx = tr(gr)
    s = lax.broadcasted_iota(jnp.int32, (c, c), 0)
    t = lax.broadcasted_iota(jnp.int32, (c, c), 1)
    tl = lax.broadcasted_iota(jnp.int32, (1, c), 1).reshape(1, 1, c)
    b2 = _hdot(logf.reshape(n, c), (s <= t).astype(F32))
    b = heads(b2)
    btot = jnp.sum(logf, axis=2, keepdims=True)
    inter = ops.bmm_tn(st, q * jnp.exp(b))
    p4 = jnp.zeros((nh, c, c), F32)
    tt, ss = s, t
    lg = c.bit_length() - 2
    while lg >= 0:
        m = 1 << lg
        bnd = jnp.left_shift(jnp.right_shift(t, lg + 1), lg + 1) + (m - 1)
        r = heads(_hdot(b2, (s == bnd).astype(F32)))
        right = jnp.bitwise_and(jnp.right_shift(tl, lg), 1) == 1
        qe = jnp.where(right, q * jnp.exp(jnp.where(right, b - r, 0.0)), 0.0)
        ke = jnp.where(right, 0.0, k * jnp.exp(jnp.where(right, 0.0, r - b)))
        lm = ((jnp.right_shift(tt, lg + 1) == jnp.right_shift(ss, lg + 1))
              & (jnp.bitwise_and(jnp.right_shift(tt, lg), 1) == 1)
              & (jnp.bitwise_and(jnp.right_shift(ss, lg), 1) == 0))
        p4 = jnp.where(lm[None], ops.bmm_tn(qe, ke), p4)
        lg -= 1
    intra = ops.bmm_nt(v, p4)
    o = inter + intra + jnp.sum(q * k, axis=1, keepdims=True) * v
    st_new = st * jnp.exp(btot) + ops.bmm_nt(k * jnp.exp(btot - b), v)
    y = o * lax.rsqrt(jnp.mean(o * o, axis=1, keepdims=True) + EPS) * og4 * (gx * _sigmoid(gx))
    return st_new, y.reshape(n, c).T


def _mla_pre(ops, cq, ckv, kpe, cos_t, sin_a, sin_b, qag, kvag, qg, kg, wq, wk, wv):
    cqn = _rms(cq, qag)
    ckvn = _rms(ckv, kvag)
    kper = ops.roll(kpe, QK_NOPE)
    qs, ks, vs = [], [], []
    for h in range(MLA_HEADS):
        qh = _slot_norm(ops.mm(cqn, wq[h]), qg, QK_DIM)
        qs.append(_rope(ops, qh, cos_t, sin_a, sin_b))
        kh = _slot_norm(ops.mm(ckvn, wk[h]) + kper, kg, QK_DIM)
        ks.append(_rope(ops, kh, cos_t, sin_a, sin_b))
        vs.append(ops.mm(ckvn, wv[h]))
    return qs, ks, vs


def _outproj(ops, x, ya, yb, o, mog, woa, wob, woc):
    yc = jnp.concatenate([_slot_norm(o[h], mog[h], V_DIM) for h in range(MLA_HEADS)], axis=1)
    return x + ops.mm(ya, woa) + ops.mm(yb, wob) + ops.mm(yc, woc)


def _lower_bounds(r0, r1, r2, r3):
    mx = jnp.maximum(jnp.maximum(r0, r1), jnp.maximum(r2, r3))
    e0, e1, e2, e3 = jnp.exp(r0 - mx), jnp.exp(r1 - mx), jnp.exp(r2 - mx), jnp.exp(r3 - mx)
    inv = 1.0 / (e0 + e1 + e2 + e3)
    s1, s2, s3 = e1 * inv, e2 * inv, e3 * inv
    return jnp.zeros_like(r0), s1, s1 + s2, s1 + s2 + s3


def _cp(sem):
    return pltpu.CompilerParams(dimension_semantics=sem, vmem_limit_bytes=VMEM_LIMIT)


def _rows(tm, n):
    return pl.BlockSpec((tm, n), lambda i: (i, 0))


def _full(a):
    nd = len(a.shape)
    return pl.BlockSpec(a.shape, lambda *_: (0,) * nd, pipeline_mode=pl.Buffered(1))


def _sds(shape, dtype=F32):
    return jax.ShapeDtypeStruct(shape, dtype)


def _acc(ref, val, first):
    @pl.when(first)
    def _():
        ref[...] = val

    @pl.when(jnp.logical_not(first))
    def _():
        ref[...] = ref[...] + val


def _f32(ref):
    return ref[...].astype(F32)


MESH = pl.DeviceIdType.MESH
ANY = pl.BlockSpec(memory_space=pl.ANY)


class Rider:
    def __init__(self, arrays, out_shapes, aliases, sems, start, finish, peers):
        self.arrays, self.out_shapes, self.aliases, self.sems = list(arrays), list(out_shapes), dict(aliases), list(sems)
        self.start, self.finish = start, finish
        self.peers = frozenset(peers)


BARRIER_IDS = {frozenset({"sibling"}): 0, frozenset({"chips"}): 1, frozenset({"sibling", "chips"}): 2}


def _entry_barrier(kinds):
    x, y, c = lax.axis_index("x"), lax.axis_index("y"), lax.axis_index("c")
    peers = [(x, y, 1 - c)] if "sibling" in kinds else []
    if "chips" in kinds:
        peers += [(1 - x, y, c), (x, 1 - y, c), (1 - x, 1 - y, c)]
    barrier = pltpu.get_barrier_semaphore()
    for peer in peers:
        pl.semaphore_signal(barrier, inc=1, device_id=peer, device_id_type=MESH)
    pl.semaphore_wait(barrier, len(peers))


def run_rider(rider, name):
    n_in, n_out = len(rider.arrays), len(rider.out_shapes)

    def body(*refs):
        ins, outs, sems = refs[:n_in], refs[n_in:n_in + n_out], refs[n_in + n_out:]
        _entry_barrier(rider.peers)
        rider.start(ins, outs, sems)
        rider.finish(ins, outs, sems)

    return pl.pallas_call(
        body, name=name, in_specs=[ANY] * n_in, out_specs=[ANY] * n_out, out_shape=rider.out_shapes,
        input_output_aliases=rider.aliases, scratch_shapes=rider.sems,
        compiler_params=pltpu.CompilerParams(collective_id=BARRIER_IDS[rider.peers]),
    )(*rider.arrays)


def _merge_riders(riders):
    bounds, a0, o0, s0 = [], 0, 0, 0
    for r in riders:
        bounds.append((a0, o0, s0))
        a0, o0, s0 = a0 + len(r.arrays), o0 + len(r.out_shapes), s0 + len(r.sems)

    def part(k, ins, outs, sems):
        a, o, s = bounds[k]
        r = riders[k]
        return ins[a:a + len(r.arrays)], outs[o:o + len(r.out_shapes)], sems[s:s + len(r.sems)]

    return Rider(
        [x for r in riders for x in r.arrays], [x for r in riders for x in r.out_shapes],
        {bounds[k][0] + i: bounds[k][1] + o for k, r in enumerate(riders) for i, o in r.aliases.items()},
        [x for r in riders for x in r.sems],
        lambda *refs: [r.start(*part(k, *refs)) for k, r in enumerate(riders)],
        lambda *refs: [r.finish(*part(k, *refs)) for k, r in enumerate(riders)],
        frozenset().union(*[r.peers for r in riders]))


def _ride(compute, riders, *, name, grid, in_specs, out_specs, out_shape, operands, scratch_shapes=(), sem=None):
    single = not isinstance(out_shape, (list, tuple))
    if single:
        out_specs, out_shape = [out_specs], [out_shape]
    live = [r for r in riders if r is not None]
    if not live:
        res = pl.pallas_call(compute, name=name, grid=grid, in_specs=in_specs, out_specs=out_specs, out_shape=out_shape,
                             scratch_shapes=list(scratch_shapes), compiler_params=_cp(sem))(*operands)
        return (res[0] if single else res), [None] * len(riders)
    rider = live[0] if len(live) == 1 else _merge_riders(live)
    n_in, n_out, n_s = len(in_specs), len(out_specs), len(scratch_shapes)
    r_in, r_out = len(rider.arrays), len(rider.out_shapes)

    def body(*refs):
        ins, rins = refs[:n_in], refs[n_in:n_in + r_in]
        outs = refs[n_in + r_in:n_in + r_in + n_out]
        routs = refs[n_in + r_in + n_out:n_in + r_in + n_out + r_out]
        scr = refs[n_in + r_in + n_out + r_out:n_in + r_in + n_out + r_out + n_s]
        rsems = refs[n_in + r_in + n_out + r_out + n_s:]
        first = functools.reduce(jnp.logical_and, [pl.program_id(a) == 0 for a in range(len(grid))])
        last = functools.reduce(jnp.logical_and, [pl.program_id(a) == grid[a] - 1 for a in range(len(grid))])

        @pl.when(first)
        def _():
            _entry_barrier(rider.peers)
            rider.start(rins, routs, rsems)

        compute(*ins, *outs, *scr)

        @pl.when(last)
        def _():
            rider.finish(rins, routs, rsems)

    res = pl.pallas_call(
        body, name=name, grid=grid, in_specs=list(in_specs) + [ANY] * r_in, out_specs=list(out_specs) + [ANY] * r_out,
        out_shape=list(out_shape) + rider.out_shapes,
        input_output_aliases={n_in + k: n_out + v for k, v in rider.aliases.items()},
        scratch_shapes=list(scratch_shapes) + rider.sems,
        compiler_params=pltpu.CompilerParams(dimension_semantics=("arbitrary",) * len(grid), vmem_limit_bytes=VMEM_LIMIT,
                                             collective_id=BARRIER_IDS[rider.peers]),
    )(*operands, *rider.arrays)
    main, rest, per_rider = res[:n_out], list(res[n_out:]), []
    for r in riders:
        per_rider.append(None if r is None else [rest.pop(0) for _ in r.out_shapes])
    return (main[0] if single else main), per_rider


def inproj_fwd(x, g1, wa, wb, wc):
    s, d = x.shape

    def body(x_ref, g_ref, wa_ref, wb_ref, wc_ref, pa_ref, pb_ref, pc_ref):
        pa, pb, pc = _inproj(PLAIN, x_ref[...], g_ref[...], wa_ref[...], wb_ref[...], wc_ref[...])
        pa_ref[...] = pa
        pb_ref[...] = pb
        pc_ref[...] = pc

    return pl.pallas_call(
        body, name="inproj_fwd", grid=(s // TM,),
        in_specs=[_rows(TM, d), _full(g1), _full(wa), _full(wb), _full(wc)],
        out_specs=[_rows(TM, wa.shape[1]), _rows(TM, wb.shape[1]), _rows(TM, wc.shape[1])],
        out_shape=[_sds((s, wa.shape[1])), _sds((s, wb.shape[1])), _sds((s, wc.shape[1]))],
        compiler_params=_cp(("parallel",)),
    )(x, g1, wa, wb, wc)


def inproj_bwd(x, g1, wa, wb, wc, dpa, dpb, dpc, dres):
    s, d = x.shape

    def body(x_ref, g_ref, wa_ref, wb_ref, wc_ref, dpa_ref, dpb_ref, dpc_ref, dres_ref,
             dx_ref, dg_ref, dwa_ref, dwb_ref, dwc_ref):
        first = pl.program_id(0) == 0
        _, vjp = jax.vjp(functools.partial(_inproj, AD), x_ref[...], g_ref[...],
                         _f32(wa_ref), _f32(wb_ref), _f32(wc_ref))
        dx, dg, dwa, dwb, dwc = vjp((dpa_ref[...], dpb_ref[...], dpc_ref[...]))
        dx_ref[...] = dx + dres_ref[...]
        _acc(dg_ref, dg, first)
        _acc(dwa_ref, dwa, first)
        _acc(dwb_ref, dwb, first)
        _acc(dwc_ref, dwc, first)

    return pl.pallas_call(
        body, name="inproj_bwd", grid=(s // TM,),
        in_specs=[_rows(TM, d), _full(g1), _full(wa), _full(wb), _full(wc),
                  _rows(TM, wa.shape[1]), _rows(TM, wb.shape[1]), _rows(TM, wc.shape[1]), _rows(TM, d)],
        out_specs=[_rows(TM, d), _full(g1), _full(wa), _full(wb), _full(wc)],
        out_shape=[_sds((s, d)), _sds(g1.shape), _sds(wa.shape), _sds(wb.shape), _sds(wc.shape)],
        compiler_params=_cp(("arbitrary",)),
    )(x, g1, wa, wb, wc, dpa, dpb, dpc, dres)


def gm_fwd(pa, vg, ws4, bs, og):
    s = pa.shape[0]
    w = pa.shape[1] // 2

    def body(pa_ref, vg_ref, ws_ref, bs_ref, og_ref, ya_ref):
        bsl = [bs_ref[h] for h in range(4)]
        for j in range(GM_CHUNKS):
            rows = slice(j * CHUNK, (j + 1) * CHUNK)
            ya_ref[rows, :] = _gm_chunk(PLAIN, pa_ref[rows, 0:w], pa_ref[rows, w:2 * w], vg_ref[...], ws_ref[...], bsl,
                                        og_ref[...])

    tm = GM_CHUNKS * CHUNK
    return pl.pallas_call(
        body, name="gm_fwd", grid=(s // tm,),
        in_specs=[_rows(tm, 2 * w), _full(vg), _full(ws4), _full(bs), _full(og)],
        out_specs=_rows(tm, w), out_shape=_sds((s, w)),
        compiler_params=_cp(("parallel",)),
    )(pa, vg, ws4, bs, og)


def gm_bwd(pa, vg, ws4, bs, og, dya):
    s = pa.shape[0]
    w = pa.shape[1] // 2

    def body(pa_ref, vg_ref, ws_ref, bs_ref, og_ref, dya_ref, dpa_ref, dvg_ref, dws_ref, dbs_ref, dog_ref):
        first = pl.program_id(0) == 0
        bsl = [bs_ref[h] for h in range(4)]
        tot = None
        for j in range(GM_CHUNKS):
            rows = slice(j * CHUNK, (j + 1) * CHUNK)
            _, vjp = jax.vjp(functools.partial(_gm_chunk, AD), pa_ref[rows, 0:w], pa_ref[rows, w:2 * w],
                             vg_ref[...], ws_ref[...], bsl, og_ref[...])
            du, dv, *dws = vjp(dya_ref[rows, :])
            dpa_ref[rows, 0:w] = du
            dpa_ref[rows, w:2 * w] = dv
            tot = dws if tot is None else jax.tree.map(jnp.add, tot, dws)
        dvg, dws, dbs, dog = tot
        _acc(dvg_ref, dvg, first)
        _acc(dws_ref, dws, first)
        _acc(dog_ref, dog, first)
        for h in range(4):
            _acc(dbs_ref.at[h], dbs[h], first)

    tm = GM_CHUNKS * CHUNK
    return pl.pallas_call(
        body, name="gm_bwd", grid=(s // tm,),
        in_specs=[_rows(tm, 2 * w), _full(vg), _full(ws4), _full(bs), _full(og), _rows(tm, w)],
        out_specs=[_rows(tm, 2 * w), _full(vg), _full(ws4), _full(bs), _full(og)],
        out_shape=[_sds((s, 2 * w)), _sds(vg.shape), _sds(ws4.shape), _sds(bs.shape), _sds(og.shape)],
        compiler_params=_cp(("arbitrary",)),
    )(pa, vg, ws4, bs, og, dya)


def hg_fwd(pb, lb, og, riders=()):
    s = pb.shape[0]
    w = pb.shape[1] // 4
    tm = HG_CHUNKS * HG_CHUNK
    st_shape = (w // HEAD64, HEAD64, HEAD64)

    def body(pb_ref, lb_ref, og_ref, yb_ref, states_ref, st_ref):
        @pl.when(pl.program_id(0) == 0)
        def _():
            st_ref[...] = jnp.zeros_like(st_ref)

        st = st_ref[...]
        for j in range(HG_CHUNKS):
            rows = slice(j * HG_CHUNK, (j + 1) * HG_CHUNK)
            states_ref[j] = st
            st, y = _hg_chunk(PLAIN, st, pb_ref[rows, 0:w], pb_ref[rows, w:2 * w], pb_ref[rows, 2 * w:3 * w],
                              pb_ref[rows, 3 * w:4 * w], lb_ref[...], og_ref[...])
            yb_ref[rows, :] = y
        st_ref[...] = st

    return _ride(
        body, riders, name="hg_fwd", grid=(s // tm,),
        in_specs=[_rows(tm, 4 * w), _full(lb), _full(og)],
        out_specs=[_rows(tm, w), pl.BlockSpec((HG_CHUNKS,) + st_shape, lambda i: (i, 0, 0, 0))],
        out_shape=[_sds((s, w)), _sds((s // HG_CHUNK,) + st_shape)],
        scratch_shapes=[pltpu.VMEM(st_shape, F32)],
        operands=(pb, lb, og), sem=("arbitrary",))


def hg_bwd(pb, lb, og, states, dyb, riders=()):
    s = pb.shape[0]
    w = pb.shape[1] // 4
    tm = HG_CHUNKS * HG_CHUNK
    nc = s // tm
    st_shape = (w // HEAD64, HEAD64, HEAD64)

    def body(pb_ref, lb_ref, og_ref, states_ref, dyb_ref, dpb_ref, dlb_ref, dog_ref, dst_ref):
        first = pl.program_id(0) == 0

        @pl.when(first)
        def _():
            dst_ref[...] = jnp.zeros_like(dst_ref)

        dst, dlb, dog = dst_ref[...], None, None
        for j in reversed(range(HG_CHUNKS)):
            rows = slice(j * HG_CHUNK, (j + 1) * HG_CHUNK)
            _, vjp = jax.vjp(functools.partial(_hg_chunk, AD), states_ref[j], pb_ref[rows, 0:w], pb_ref[rows, w:2 * w],
                             pb_ref[rows, 2 * w:3 * w], pb_ref[rows, 3 * w:4 * w], lb_ref[...], og_ref[...])
            dst, dq, df, di, dg, dlb_j, dog_j = vjp((dst, dyb_ref[rows, :]))
            dpb_ref[rows, 0:w] = dq
            dpb_ref[rows, w:2 * w] = df
            dpb_ref[rows, 2 * w:3 * w] = di
            dpb_ref[rows, 3 * w:4 * w] = dg
            dlb = dlb_j if dlb is None else dlb + dlb_j
            dog = dog_j if dog is None else dog + dog_j
        dst_ref[...] = dst
        _acc(dlb_ref, dlb, first)
        _acc(dog_ref, dog, first)

    rev = lambda i: (nc - 1 - i, 0)
    return _ride(
        body, riders, name="hg_bwd", grid=(nc,),
        in_specs=[pl.BlockSpec((tm, 4 * w), rev), _full(lb), _full(og),
                  pl.BlockSpec((HG_CHUNKS,) + st_shape, lambda i: (nc - 1 - i, 0, 0, 0)), pl.BlockSpec((tm, w), rev)],
        out_specs=[pl.BlockSpec((tm, 4 * w), rev), _full(lb), _full(og)],
        out_shape=[_sds((s, 4 * w)), _sds(lb.shape), _sds(og.shape)],
        scratch_shapes=[pltpu.VMEM(st_shape, F32)],
        operands=(pb, lb, og, states, dyb), sem=("arbitrary",))


def lower_bounds_fwd(hlb):
    def body(h_ref, o_ref):
        outs = _lower_bounds(*[h_ref[pl.ds(i, 1), :] for i in range(DEPTH)])
        for i in range(DEPTH):
            o_ref[pl.ds(i, 1), :] = outs[i]

    return pl.pallas_call(body, name="lower_bounds_fwd", out_shape=_sds(hlb.shape))(hlb)


def lower_bounds_bwd(hlb, dlbs):
    def body(h_ref, d_ref, o_ref):
        _, vjp = jax.vjp(_lower_bounds, *[h_ref[pl.ds(i, 1), :] for i in range(DEPTH)])
        outs = vjp(tuple(d_ref[pl.ds(i, 1), :] for i in range(DEPTH)))
        for i in range(DEPTH):
            o_ref[pl.ds(i, 1), :] = outs[i]

    return pl.pallas_call(body, name="lower_bounds_bwd", out_shape=_sds(hlb.shape))(hlb, dlbs)


def _mla_pre_args(pc_ref, cos_ref, sa_ref, sb_ref, qag_ref, kvag_ref, qg_ref, kg_ref, wq_ref, wk_ref, wv_ref, cast):
    sl = lambda h: slice(h * SLOT, (h + 1) * SLOT)
    ld = (lambda r, h: r[:, sl(h)].astype(F32)) if cast else (lambda r, h: r[:, sl(h)])
    diff = (pc_ref[:, 0:Q_LORA], pc_ref[:, Q_LORA:Q_LORA + KV_LORA], pc_ref[:, Q_LORA + KV_LORA:Q_LORA + 2 * KV_LORA],
            qag_ref[...], kvag_ref[...], qg_ref[...], kg_ref[...],
            [ld(wq_ref, h) for h in range(MLA_HEADS)], [ld(wk_ref, h) for h in range(MLA_HEADS)],
            [ld(wv_ref, h) for h in range(MLA_HEADS)])
    tables = (cos_ref[...], sa_ref[...], sb_ref[...])
    return diff, tables


def _mla_pre_fn(ops, tables, cq, ckv, kpe, qag, kvag, qg, kg, wq, wk, wv):
    return _mla_pre(ops, cq, ckv, kpe, *tables, qag, kvag, qg, kg, wq, wk, wv)


def mla_pre_fwd(pc, cos_t, sin_a, sin_b, qag, kvag, qg, kg, wq, wk, wv):
    s = pc.shape[0]
    hw = MLA_HEADS * SLOT

    def body(pc_ref, cos_ref, sa_ref, sb_ref, qag_ref, kvag_ref, qg_ref, kg_ref, wq_ref, wk_ref, wv_ref,
             q_ref, k_ref, v_ref):
        diff, tables = _mla_pre_args(pc_ref, cos_ref, sa_ref, sb_ref, qag_ref, kvag_ref, qg_ref, kg_ref,
                                     wq_ref, wk_ref, wv_ref, False)
        qs, ks, vs = _mla_pre_fn(PLAIN, tables, *diff)
        ones_lane = (lax.broadcasted_iota(jnp.int32, (1, SLOT), 1) == V_DIM).astype(F32)
        for h in range(MLA_HEADS):
            q_ref[:, h * SLOT:(h + 1) * SLOT] = qs[h].astype(q_ref.dtype)
            k_ref[:, h * SLOT:(h + 1) * SLOT] = ks[h].astype(k_ref.dtype)
            v_ref[:, h * SLOT:(h + 1) * SLOT] = (vs[h] + ones_lane).astype(v_ref.dtype)

    return pl.pallas_call(
        body, name="mla_pre_fwd", grid=(s // TM,),
        in_specs=[_rows(TM, pc.shape[1]), _rows(TM, SLOT), _rows(TM, SLOT), _rows(TM, SLOT),
                  _full(qag), _full(kvag), _full(qg), _full(kg), _full(wq), _full(wk), _full(wv)],
        out_specs=[_rows(TM, hw)] * 3, out_shape=[_sds((s, hw), MXU_DTYPE)] * 3,
        compiler_params=_cp(("parallel",)),
    )(pc, cos_t, sin_a, sin_b, qag, kvag, qg, kg, wq, wk, wv)


def mla_pre_bwd(pc, cos_t, sin_a, sin_b, qag, kvag, qg, kg, wq, wk, wv, dq, dk, dv):
    s = pc.shape[0]
    hw = MLA_HEADS * SLOT

    def body(pc_ref, cos_ref, sa_ref, sb_ref, qag_ref, kvag_ref, qg_ref, kg_ref, wq_ref, wk_ref, wv_ref,
             dq_ref, dk_ref, dv_ref, dpc_ref, dqag_ref, dkvag_ref, dqg_ref, dkg_ref, dwq_ref, dwk_ref, dwv_ref):
        first = pl.program_id(0) == 0
        diff, tables = _mla_pre_args(pc_ref, cos_ref, sa_ref, sb_ref, qag_ref, kvag_ref, qg_ref, kg_ref,
                                     wq_ref, wk_ref, wv_ref, True)
        _, vjp = jax.vjp(functools.partial(_mla_pre_fn, AD, tables), *diff)
        sl = lambda h: slice(h * SLOT, (h + 1) * SLOT)
        cot = ([dq_ref[:, sl(h)] for h in range(MLA_HEADS)], [dk_ref[:, sl(h)] for h in range(MLA_HEADS)],
               [dv_ref[:, sl(h)] for h in range(MLA_HEADS)])
        dcq, dckv, dkpe, dqag, dkvag, dqg, dkg, dwq, dwk, dwv = vjp(cot)
        dpc_ref[:, 0:Q_LORA] = dcq
        dpc_ref[:, Q_LORA:Q_LORA + KV_LORA] = dckv
        dpc_ref[:, Q_LORA + KV_LORA:Q_LORA + 2 * KV_LORA] = dkpe
        _acc(dqag_ref, dqag, first)
        _acc(dkvag_ref, dkvag, first)
        _acc(dqg_ref, dqg, first)
        _acc(dkg_ref, dkg, first)
        for h in range(MLA_HEADS):
            _acc(dwq_ref.at[:, sl(h)], dwq[h], first)
            _acc(dwk_ref.at[:, sl(h)], dwk[h], first)
            _acc(dwv_ref.at[:, sl(h)], dwv[h], first)

    return pl.pallas_call(
        body, name="mla_pre_bwd", grid=(s // TM,),
        in_specs=[_rows(TM, pc.shape[1]), _rows(TM, SLOT), _rows(TM, SLOT), _rows(TM, SLOT),
                  _full(qag), _full(kvag), _full(qg), _full(kg), _full(wq), _full(wk), _full(wv),
                  _rows(TM, hw), _rows(TM, hw), _rows(TM, hw)],
        out_specs=[_rows(TM, pc.shape[1]), _full(qag), _full(kvag), _full(qg), _full(kg),
                   _full(wq), _full(wk), _full(wv)],
        out_shape=[_sds(pc.shape), _sds(qag.shape), _sds(kvag.shape), _sds(qg.shape), _sds(kg.shape),
                   _sds(wq.shape), _sds(wk.shape), _sds(wv.shape)],
        compiler_params=_cp(("arbitrary",)),
    )(pc, cos_t, sin_a, sin_b, qag, kvag, qg, kg, wq, wk, wv, dq, dk, dv)


ATT_SCALE = QK_DIM ** -0.5
NEG_BIG = -1e30


def attn_fwd(q, k, v, riders=()):
    s = q.shape[0]
    nq = s // TQ
    hp = ATT_HEADS_PER_STEP
    sl = lambda j: slice(j * SLOT, (j + 1) * SLOT)

    wide = ATT_WIDE // TQ

    def body(q_ref, k_ref, v_ref, o_ref, lse_ref):
        qi = pl.program_id(1)
        lane = lax.broadcasted_iota(jnp.int32, (1, SLOT), 1)
        qs = [q_ref[:, sl(j)] for j in range(hp)]

        def step(ki, carry, n_tiles, masked):
            rk = pl.ds(pl.multiple_of(ki * TQ, TQ), n_tiles * TQ)
            if masked:
                row = lax.broadcasted_iota(jnp.int32, (TQ, n_tiles * TQ), 0) + (n_tiles - 1) * TQ
                col = lax.broadcasted_iota(jnp.int32, (TQ, n_tiles * TQ), 1)
            out = []
            for j in range(hp):
                m, acc = carry[j]
                sc = _dot(qs[j], k_ref[rk, sl(j)], NT) * ATT_SCALE
                if masked:
                    sc = jnp.where(row >= col, sc, NEG_BIG)
                m_new = jnp.maximum(m, jnp.max(sc, axis=-1, keepdims=True))
                acc = jnp.exp(m - m_new) * acc + _dot(jnp.exp(sc - m_new), v_ref[rk, sl(j)], NN)
                out.append((m_new, acc))
            return tuple(out)

        def tail_single(cr):
            cr = lax.fori_loop(n_wide * wide, qi, lambda ki, c: step(ki, c, 1, False), cr)
            return step(qi, cr, 1, True)

        n_wide = qi // wide
        init = tuple((jnp.full((TQ, 1), NEG_BIG, F32), jnp.zeros((TQ, SLOT), F32)) for _ in range(hp))
        carry = lax.fori_loop(0, n_wide, lambda kw, cr: step(kw * wide, cr, wide, False), init)
        carry = lax.cond(qi % wide == wide - 1, lambda cr: step(qi - (wide - 1), cr, wide, True), tail_single, carry)
        for j in range(hp):
            m, acc = carry[j]
            l = jnp.sum(jnp.where(lane == V_DIM, acc, 0.0), axis=-1, keepdims=True)
            o_ref[:, sl(j)] = jnp.where(lane < V_DIM, acc / l, 0.0)
            lse_ref[j] = m + jnp.log(l)

    head_col = pl.BlockSpec((s, hp * SLOT), lambda g, i: (0, g))
    tile = pl.BlockSpec((TQ, hp * SLOT), lambda g, i: (i, g))
    return _ride(
        body, riders, name="attn_fwd", grid=(MLA_HEADS // hp, nq),
        in_specs=[tile, head_col, head_col],
        out_specs=[tile, pl.BlockSpec((hp, TQ, 1), lambda g, i: (g, i, 0))],
        out_shape=[_sds((s, MLA_HEADS * SLOT)), _sds((MLA_HEADS, s, 1))],
        operands=(q, k, v), sem=("parallel", "parallel"))


def attn_bwd(q, k, v, o, do, lse, riders=()):
    s = q.shape[0]
    nq = s // TQ
    hp = ATT_HEADS_PER_STEP
    sl = lambda j: slice(j * SLOT, (j + 1) * SLOT)
    wide = ATT_WIDE // TQ

    def body(q_ref, k_ref, v_ref, o_ref, do_ref, lse_ref, dq_ref, dk_ref, dv_ref, delta_ref):
        ki = pl.program_id(1)

        @pl.when(ki == 0)
        def _():
            dq_ref[...] = jnp.zeros_like(dq_ref)

            def prep(i, c):
                rows = pl.ds(pl.multiple_of(i * TQ, TQ), TQ)
                for j in range(hp):
                    delta_ref[j, rows, :] = jnp.sum(do_ref[rows, sl(j)] * o_ref[rows, sl(j)], axis=-1, keepdims=True)
                return c

            lax.fori_loop(0, nq, prep, 0)

        kks = [k_ref[:, sl(j)] for j in range(hp)]
        vvs = [v_ref[:, sl(j)] for j in range(hp)]

        def step(qi, carry, n_tiles, masked):
            rq = pl.ds(pl.multiple_of(qi * TQ, TQ), n_tiles * TQ)
            if masked:
                row = lax.broadcasted_iota(jnp.int32, (n_tiles * TQ, TQ), 0)
                col = lax.broadcasted_iota(jnp.int32, (n_tiles * TQ, TQ), 1)
            out = []
            for j in range(hp):
                dk, dv = carry[j]
                qq = q_ref[rq, sl(j)]
                dd = do_ref[rq, sl(j)]
                sc = _dot(qq, kks[j], NT) * ATT_SCALE
                if masked:
                    sc = jnp.where(row >= col, sc, NEG_BIG)
                p = jnp.exp(sc - lse_ref[j, rq, :])
                dv = dv + _dot(p, dd, TN)
                ds = p * (_dot(dd, vvs[j], NT) - delta_ref[j, rq, :]) * ATT_SCALE
                dk = dk + _dot(ds, qq, TN)
                dq_ref[rq, sl(j)] = dq_ref[rq, sl(j)] + _dot(ds, kks[j], NN)
                out.append((dk, dv))
            return tuple(out)

        def head_single(cr):
            cr = step(ki, cr, 1, True)
            return lax.fori_loop(ki + 1, first_wide * wide, lambda qi, c: step(qi, c, 1, False), cr)

        zero = jnp.zeros((TQ, SLOT), F32)
        first_wide = (ki + wide) // wide
        carry = tuple((zero, zero) for _ in range(hp))
        carry = lax.cond(ki % wide == 0, lambda cr: step(ki, cr, wide, True), head_single, carry)
        carry = lax.fori_loop(first_wide, nq // wide, lambda qw, cr: step(qw * wide, cr, wide, False), carry)
        for j in range(hp):
            dk_ref[:, sl(j)] = carry[j][0]
            dv_ref[:, sl(j)] = carry[j][1]

    head_col = pl.BlockSpec((s, hp * SLOT), lambda g, i: (0, g))
    tile = pl.BlockSpec((TQ, hp * SLOT), lambda g, i: (i, g))
    return _ride(
        body, riders, name="attn_bwd", grid=(MLA_HEADS // hp, nq),
        in_specs=[head_col, tile, tile, head_col, head_col, pl.BlockSpec((hp, s, 1), lambda g, i: (g, 0, 0))],
        out_specs=[head_col, tile, tile],
        out_shape=[_sds((s, MLA_HEADS * SLOT))] * 3,
        scratch_shapes=[pltpu.VMEM((hp, s, 1), F32)],
        operands=(q, k, v, o, do, lse), sem=("arbitrary", "arbitrary"))


def _outproj_args(ya_ref, yb_ref, o_ref, mog_ref, woa_ref, wob_ref, woc_ref, cast):
    sl = lambda h: slice(h * SLOT, (h + 1) * SLOT)
    ldw = (lambda r: r[...].astype(F32)) if cast else (lambda r: r[...])
    return (ya_ref[...], yb_ref[...], [o_ref[:, sl(h)] for h in range(MLA_HEADS)],
            [mog_ref[:, sl(h)] for h in range(MLA_HEADS)], ldw(woa_ref), ldw(wob_ref), ldw(woc_ref))


def outproj_fwd(x, ya, yb, o, mog, woa, wob, woc):
    s, d = x.shape

    def body(x_ref, ya_ref, yb_ref, o_ref, mog_ref, woa_ref, wob_ref, woc_ref, x1_ref):
        x1_ref[...] = _outproj(PLAIN, x_ref[...], *_outproj_args(ya_ref, yb_ref, o_ref, mog_ref, woa_ref, wob_ref,
                                                                  woc_ref, False))

    return pl.pallas_call(
        body, name="outproj_fwd", grid=(s // TM,),
        in_specs=[_rows(TM, d), _rows(TM, ya.shape[1]), _rows(TM, yb.shape[1]), _rows(TM, o.shape[1]),
                  _full(mog), _full(woa), _full(wob), _full(woc)],
        out_specs=_rows(TM, d), out_shape=_sds((s, d)),
        compiler_params=_cp(("parallel",)),
    )(x, ya, yb, o, mog, woa, wob, woc)


def outproj_bwd(ya, yb, o, mog, woa, wob, woc, dx2, dx1p, riders=()):
    s, d = dx2.shape
    npart = dx1p.shape[0]

    def body(ya_ref, yb_ref, o_ref, mog_ref, woa_ref, wob_ref, woc_ref, dx2_ref, dx1p_ref,
             dx1_ref, dya_ref, dyb_ref, do_ref, dmog_ref, dwoa_ref, dwob_ref, dwoc_ref):
        first = pl.program_id(0) == 0
        sl = lambda h: slice(h * SLOT, (h + 1) * SLOT)
        dx1 = dx2_ref[...]
        for p in range(npart):
            dx1 = dx1 + dx1p_ref[p]
        dx1_ref[...] = dx1
        args = _outproj_args(ya_ref, yb_ref, o_ref, mog_ref, woa_ref, wob_ref, woc_ref, True)
        _, vjp = jax.vjp(lambda *a: _outproj(AD, jnp.zeros_like(dx1), *a), *args)
        dya, dyb, do, dmog, dwoa, dwob, dwoc = vjp(dx1)
        dya_ref[...] = dya
        dyb_ref[...] = dyb
        _acc(dwoa_ref, dwoa, first)
        _acc(dwob_ref, dwob, first)
        _acc(dwoc_ref, dwoc, first)
        for h in range(MLA_HEADS):
            do_ref[:, sl(h)] = do[h]
            _acc(dmog_ref.at[:, sl(h)], dmog[h], first)

    return _ride(
        body, riders, name="outproj_bwd", grid=(s // TM,),
        in_specs=[_rows(TM, ya.shape[1]), _rows(TM, yb.shape[1]), _rows(TM, o.shape[1]),
                  _full(mog), _full(woa), _full(wob), _full(woc), _rows(TM, d),
                  pl.BlockSpec((npart, TM, d), lambda i: (0, i, 0))],
        out_specs=[_rows(TM, d), _rows(TM, ya.shape[1]), _rows(TM, yb.shape[1]), _rows(TM, o.shape[1]),
                   _full(mog), _full(woa), _full(wob), _full(woc)],
        out_shape=[_sds((s, d)), _sds(ya.shape), _sds(yb.shape), _sds(o.shape),
                   _sds(mog.shape), _sds(woa.shape), _sds(wob.shape), _sds(woc.shape)],
        operands=(ya, yb, o, mog, woa, wob, woc, dx2, dx1p), sem=("arbitrary",))


def ffn_fwd(x1, g2, w1, w2, riders=()):
    s, d = x1.shape
    npart, _, fs = w1.shape

    def body(x1_ref, g_ref, w1_ref, w2_ref, x2_ref, r_ref):
        p = pl.program_id(1)
        x1v = x1_ref[...]
        r = jnp.maximum(PLAIN.mm(_rms(x1v, g_ref[...]), w1_ref[...]), 0.0)
        r_ref[...] = r.astype(r_ref.dtype)
        part = PLAIN.mm(r * r, w2_ref[...])

        @pl.when(p == 0)
        def _():
            x2_ref[...] = x1v + part

        @pl.when(p != 0)
        def _():
            x2_ref[...] = x2_ref[...] + part

    tm = min(2 * TM_FFN, s)
    return _ride(
        body, riders, name="ffn_fwd", grid=(s // tm, npart),
        in_specs=[pl.BlockSpec((tm, d), lambda i, p: (i, 0)), pl.BlockSpec(g2.shape, lambda i, p: (0, 0)),
                  pl.BlockSpec((None, d, fs), lambda i, p: (p, 0, 0)), pl.BlockSpec((None, fs, d), lambda i, p: (p, 0, 0))],
        out_specs=[pl.BlockSpec((tm, d), lambda i, p: (i, 0)), pl.BlockSpec((tm, fs), lambda i, p: (i, p))],
        out_shape=[_sds((s, d)), _sds((s, npart * fs), MXU_DTYPE)],
        operands=(x1, g2, w1, w2), sem=("parallel", "arbitrary"))


def ffn_bwd(x1, g2, w1, w2, r, dx2, riders=()):
    s, d = x1.shape
    npart, _, fs = w1.shape
    tm = TM_FFN

    def body(x1_ref, g_ref, w1_ref, w2_ref, r_ref, dx2_ref, dx1p_ref, dg_ref, dw1_ref, dw2_ref, dw1b_ref, dw2b_ref):
        p = pl.program_id(0)
        i = pl.program_id(1)
        h2, vjp_norm = jax.vjp(_rms, x1_ref[...], g_ref[...])
        rr = r_ref[...].astype(F32)
        dy = dx2_ref[...]
        da = _dot(dy, w2_ref[...], NT) * (2.0 * rr)
        dx1, dg = vjp_norm(_dot(da, w1_ref[...], NT))
        dx1p_ref[...] = dx1
        _acc(dg_ref, dg, (p == 0) & (i == 0))
        _acc(dw1_ref, _dot(h2, da, TN), i == 0)
        _acc(dw2_ref, _dot(rr * rr, dy, TN), i == 0)

        @pl.when(i == s // tm - 1)
        def _():
            dw1b_ref[...] = dw1_ref[...].astype(BF16)
            dw2b_ref[...] = dw2_ref[...].astype(BF16)

    return _ride(
        body, riders, name="ffn_bwd", grid=(npart, s // tm),
        in_specs=[pl.BlockSpec((tm, d), lambda p, i: (i, 0)), pl.BlockSpec(g2.shape, lambda p, i: (0, 0)),
                  pl.BlockSpec((None, d, fs), lambda p, i: (p, 0, 0)), pl.BlockSpec((None, fs, d), lambda p, i: (p, 0, 0)),
                  pl.BlockSpec((tm, fs), lambda p, i: (i, p)), pl.BlockSpec((tm, d), lambda p, i: (i, 0))],
        out_specs=[pl.BlockSpec((None, tm, d), lambda p, i: (p, i, 0)), pl.BlockSpec(g2.shape, lambda p, i: (0, 0)),
                   pl.BlockSpec((None, d, fs), lambda p, i: (p, 0, 0)), pl.BlockSpec((None, fs, d), lambda p, i: (p, 0, 0)),
                   pl.BlockSpec((None, d, fs), lambda p, i: (p, 0, 0)), pl.BlockSpec((None, fs, d), lambda p, i: (p, 0, 0))],
        out_shape=[_sds((npart, s, d)), _sds(g2.shape), _sds(w1.shape), _sds(w2.shape),
                   _sds(w1.shape, BF16), _sds(w2.shape, BF16)],
        operands=(x1, g2, w1, w2, r, dx2), sem=("arbitrary", "arbitrary"))


def loss_head(y, target):
    s, d = y.shape

    def body(y_ref, t_ref, dy_ref, loss_ref):
        err = y_ref[...] - t_ref[...]
        dy_ref[...] = err * (1.0 / d)
        part = jnp.sum(jnp.sum(err * err, axis=-1, keepdims=True), axis=0, keepdims=True) * (0.5 / d)
        _acc(loss_ref, jnp.broadcast_to(part, loss_ref.shape), pl.program_id(0) == 0)

    return pl.pallas_call(
        body, name="loss_head", grid=(s // TM,),
        in_specs=[_rows(TM, d), _rows(TM, d)],
        out_specs=[_rows(TM, d), pl.BlockSpec((1, SLOT), lambda i: (0, 0))],
        out_shape=[_sds((s, d)), _sds((1, SLOT))],
        compiler_params=_cp(("arbitrary",)),
    )(y, target)


def _row_block(r):
    for b in (512, 256, 128, 64, 32, 16, 8):
        if r % b == 0:
            return b
    return r


def sum_cores(arrs, gots, half, me):
    n = len(arrs)

    def body(sp_ref, *refs):
        for i in range(n):
            a_ref, g_ref, wire_ref, own_ref = refs[i], refs[n + i], refs[2 * n + i], refs[3 * n + i]
            tot = a_ref[...] + g_ref[...]
            wire_ref[...] = tot.astype(wire_ref.dtype)

            @pl.when(pl.program_id(0) == sp_ref[1])
            def _(own_ref=own_ref, tot=tot):
                own_ref[...] = tot

    shapes = [a.shape[2:] for a in arrs]
    grid_spec = pltpu.PrefetchScalarGridSpec(
        num_scalar_prefetch=1, grid=(N_CHIPS,),
        in_specs=[pl.BlockSpec((None, None) + sh, lambda p, sp: (p, sp[0], 0, 0)) for sh in shapes]
        + [pl.BlockSpec((None,) + sh, lambda p, sp: (p, 0, 0)) for sh in shapes],
        out_specs=[pl.BlockSpec((None,) + sh, lambda p, sp: (p, 0, 0)) for sh in shapes]
        + [pl.BlockSpec(sh, lambda p, sp: (0, 0)) for sh in shapes])
    outs = pl.pallas_call(body, name="sum_cores", grid_spec=grid_spec,
                          out_shape=[_sds((N_CHIPS,) + sh, BF16) for sh in shapes] + [_sds(sh) for sh in shapes],
                          compiler_params=_cp(("arbitrary",)))(jnp.stack([half, me]).astype(jnp.int32), *arrs, *gots)
    return outs[:n], outs[n:]


SUM_STEPS = 4


def sum_chips(owns, recvs, half, places):
    n = len(owns)
    dests = [d for _, _, d in places if d is not None]

    def body(sp_ref, *refs):
        del sp_ref
        for i in range(n):
            own_ref, out_ref = refs[4 * i], refs[4 * n + len(dests) + i]
            r0, r1, r2 = (refs[4 * i + 1 + j][...].astype(F32) for j in range(3))
            out_ref[...] = ((own_ref[...] + r0) + r1) + r2

    in_specs, out_specs, operands, aliases = [], [], [], {}
    for i, (own, recv, (layer, _, dest)) in enumerate(zip(owns, recvs, places)):
        r, c = own.shape
        br = r // SUM_STEPS
        in_specs.append(pl.BlockSpec((br, c), lambda i, sp: (i, 0)))
        in_specs += [pl.BlockSpec((None, br, c), functools.partial(lambda i, sp, j: (j, i, 0), j=j)) for j in range(3)]
        out_specs.append(pl.BlockSpec((None, None, br, c), functools.partial(lambda i, sp, l: (l, sp[0], i, 0), l=layer)))
        operands += [own, recv, recv, recv]
        if dest is not None:
            aliases[1 + 4 * n + len(aliases)] = i
    grid_spec = pltpu.PrefetchScalarGridSpec(num_scalar_prefetch=1, grid=(SUM_STEPS,), in_specs=in_specs + [ANY] * len(dests),
                                             out_specs=out_specs)
    return pl.pallas_call(body, name="sum_chips", grid_spec=grid_spec, input_output_aliases=aliases,
                          out_shape=[_sds((nl, 2) + o.shape) for o, (_, nl, _) in zip(owns, places)],
                          compiler_params=_cp(("parallel",)))(half.reshape(1).astype(jnp.int32), *operands, *dests)


def adamw(w, g, m, v, name, riders=()):
    r, c = w.shape
    br = _row_block(r)
    c1 = 1.0 / (1.0 - ADAM_B1 ** ADAM_STEP)
    c2 = 1.0 / (1.0 - ADAM_B2 ** ADAM_STEP)

    def body(w_ref, g_ref, m_ref, v_ref, d_ref, nm_ref, nv_ref):
        gg = g_ref[...]
        nm = ADAM_B1 * m_ref[...] + (1.0 - ADAM_B1) * gg
        nv = ADAM_B2 * v_ref[...] + (1.0 - ADAM_B2) * (gg * gg)
        d_ref[...] = -ADAM_LR * ((nm * c1) / (jnp.sqrt(nv * c2) + ADAM_EPS) + ADAM_WD * w_ref[...])
        nm_ref[...] = nm
        nv_ref[...] = nv

    return _ride(body, riders, name=name, grid=(r // br,), in_specs=[_rows(br, c)] * 4, out_specs=[_rows(br, c)] * 3,
                 out_shape=[_sds((r, c))] * 3, operands=(w, g, m, v), sem=("parallel",))


def _place():
    x, y, c = lax.axis_index("x"), lax.axis_index("y"), lax.axis_index("c")
    chips = [(1 - x, y), (x, 1 - y), (1 - x, 1 - y)]
    return x, y, c, chips


def _remote(src, dst, send_sem, recv_sem, to):
    return pltpu.make_async_remote_copy(src_ref=src, dst_ref=dst, send_sem=send_sem, recv_sem=recv_sem,
                                        device_id=to, device_id_type=MESH)


def gather_rider(arrs):
    n = len(arrs)
    me_chip = 2 * lax.axis_index("x") + lax.axis_index("y")
    bufs = [lax.dynamic_update_index_in_dim(lax.empty((N_CHIPS,) + a.shape, a.dtype), a, me_chip, 0) for a in arrs]

    def plan(ins, outs, sems):
        send_sems, recv_sems = sems
        x, y, c, chips = _place()
        me = 2 * x + y
        half, other, sibling = pl.ds(2 * c, 2), pl.ds(2 - 2 * c, 2), (x, y, 1 - c)
        cp = lambda i, k, src, dst, to: _remote(src, dst, send_sems.at[i, k], recv_sems.at[i, k], to)
        pairs = [(i, j, cx, cy) for i in range(n) for j, (cx, cy) in enumerate(chips)]
        blk = lambda i, cx, cy, part: outs[i].at[2 * cx + cy, part]
        first = lambda: [cp(i, j, ins[i].at[half], outs[i].at[me, half], (cx, cy, c)) for i, j, cx, cy in pairs]
        landed = lambda: [cp(i, j, blk(i, cx, cy, half), blk(i, cx, cy, half), (cx, cy, c)) for i, j, cx, cy in pairs]
        passed = lambda: [cp(i, 3 + j, blk(i, cx, cy, half), blk(i, cx, cy, half), sibling) for i, j, cx, cy in pairs]
        from_sibling = lambda: [cp(i, 3 + j, blk(i, cx, cy, other), blk(i, cx, cy, other), sibling) for i, j, cx, cy in pairs]
        return first, landed, passed, from_sibling

    def start(ins, outs, sems):
        for cp in plan(ins, outs, sems)[0]():
            cp.start()

    def finish(ins, outs, sems):
        first, landed, passed, from_sibling = plan(ins, outs, sems)
        forwards = passed()
        for a, b in zip(landed(), forwards):
            a.wait_recv()
            b.start()
        for cp in from_sibling():
            cp.wait_recv()
        for cp in first() + forwards:
            cp.wait_send()

    return Rider(list(arrs) + bufs, [_sds((N_CHIPS,) + a.shape, a.dtype) for a in arrs], {n + i: i for i in range(n)},
                 [pltpu.SemaphoreType.DMA((n, 6)), pltpu.SemaphoreType.DMA((n, 6))], start, finish, {"sibling", "chips"})


class Reducer:
    def __init__(self, arrs, places=None, send=None):
        self.a = list(arrs)
        self.send = list(send) if send is not None else self.a
        self.n = len(self.a)
        self.places = places if places is not None else [(0, 1, None)] * self.n
        self.c = lax.axis_index("c")
        self.me = 2 * lax.axis_index("x") + lax.axis_index("y")

    def swap_rider(self):
        n = self.n

        def plan(ins, outs, sems):
            x, y, c, _ = _place()
            return [_remote(ins[i].at[p, 1 - c], outs[i].at[p], sems[0].at[i, p], sems[1].at[i, p], (x, y, 1 - c))
                    for i in range(n) for p in range(N_CHIPS)]

        return Rider(self.send, [_sds((N_CHIPS,) + a.shape[2:], a.dtype) for a in self.send], {},
                     [pltpu.SemaphoreType.DMA((n, N_CHIPS)), pltpu.SemaphoreType.DMA((n, N_CHIPS))],
                     lambda *r: [cp.start() for cp in plan(*r)], lambda *r: [cp.wait() for cp in plan(*r)], {"sibling"})

    def after_swap(self, got):
        self.wire, self.own = sum_cores(self.a, got, self.c, self.me)

    def scatter_rider(self):
        n = self.n

        def plan(ins, outs, sems):
            x, y, c, chips = _place()
            return [_remote(ins[i].at[2 * cx + cy], outs[i].at[j], sems[0].at[i, j], sems[1].at[i, j], (cx, cy, c))
                    for i in range(n) for j, (cx, cy) in enumerate(chips)]

        return Rider(self.wire, [_sds((3,) + w.shape[1:], w.dtype) for w in self.wire], {},
                     [pltpu.SemaphoreType.DMA((n, 3)), pltpu.SemaphoreType.DMA((n, 3))],
                     lambda *r: [cp.start() for cp in plan(*r)], lambda *r: [cp.wait() for cp in plan(*r)], {"chips"})

    def after_scatter(self, recv):
        if callable(self.places):
            self.places = self.places()
        self.full = sum_chips(self.own, recv, self.c, self.places)

    def share_rider(self):
        n = self.n
        layers = [layer for layer, _, _ in self.places]

        def plan(ins, outs, sems):
            x, y, c, _ = _place()
            return [_remote(ins[i].at[layers[i], c], outs[i].at[layers[i], c], sems[0].at[i], sems[1].at[i], (x, y, 1 - c))
                    for i in range(n)]

        return Rider(self.full, [_sds(f.shape) for f in self.full], {i: i for i in range(n)},
                     [pltpu.SemaphoreType.DMA((n,)), pltpu.SemaphoreType.DMA((n,))],
                     lambda *r: [cp.start() for cp in plan(*r)], lambda *r: [cp.wait() for cp in plan(*r)], {"sibling"})

    def run(self):
        self.after_swap(run_rider(self.swap_rider(), "swap_halves"))
        self.after_scatter(run_rider(self.scatter_rider(), "scatter_chips"))
        return run_rider(self.share_rider(), "share_halves")


def _pad_slots(a, live):
    lead = a.shape[:-1]
    a = a.reshape(lead + (MLA_HEADS, live))
    a = jnp.pad(a, [(0, 0)] * len(lead) + [(0, 0), (0, SLOT - live)])
    return a.reshape(lead + (MLA_HEADS * SLOT,))


def _unpad_slots(a, live):
    lead = a.shape[:-1]
    return a.reshape(lead + (MLA_HEADS, SLOT))[..., :live].reshape(lead + (MLA_HEADS * live,))


def _rope_tables(positions, s):
    half = QK_ROPE // 2
    inv_freq = ROPE_THETA ** (-jnp.arange(half, dtype=F32) / half)
    ang = positions.reshape(s).astype(F32)[:, None] * inv_freq[None, :]
    cos, sin = jnp.cos(ang), jnp.sin(ang)
    one = jnp.ones((s, QK_NOPE), F32)
    z64, z16, z32 = jnp.zeros((s, QK_NOPE), F32), jnp.zeros((s, half), F32), jnp.zeros((s, SLOT - QK_DIM), F32)
    cos_t = jnp.concatenate([one, cos, cos, z32], axis=1)
    sin_a = jnp.concatenate([z64, -sin, z16, z32], axis=1)
    sin_b = jnp.concatenate([z64, z16, sin, z32], axis=1)
    return cos_t, sin_a, sin_b


def _out_weights(full):
    w_out = jnp.concatenate([full["w_out"][p] for p in range(N_CHIPS)], axis=0)
    woc = w_out[512:].reshape(MLA_HEADS, V_DIM, D_MODEL)
    woc = jnp.pad(woc, ((0, 0), (0, SLOT - V_DIM), (0, 0))).reshape(MLA_HEADS * SLOT, D_MODEL)
    return dict(woa=w_out[:256], wob=w_out[256:512], woc=woc)


def _layer_weights(full, small, l):
    w_in = jnp.concatenate([full["w_in"][p] for p in range(N_CHIPS)], axis=1)
    wc = jnp.pad(w_in[:, 1536:], ((0, 0), (0, 512 - (w_in.shape[1] - 1536))))
    w_uq = jnp.concatenate([full["mla_w_uq"][p] for p in range(N_CHIPS)], axis=1)
    w_ukv = jnp.concatenate([full["mla_w_ukv"][p] for p in range(N_CHIPS)], axis=1)
    ukv = w_ukv.reshape(KV_LORA, MLA_HEADS, QK_NOPE + V_DIM)
    row = lambda a: a.reshape(1, -1)
    return dict(
        g1=row(small["norm1_gain"][l]), wa=w_in[:, :512], wb=w_in[:, 512:1536], wc=wc,
        vg=row(small["gm_v_gain"][l]), ws=small["gm_w_s"][l], bs=small["gm_b_s"][l].reshape(4, CHUNK, 1),
        gog=row(small["gm_out_gain"][l]), hog=small["hg_out_gain"][l].reshape(-1, 1),
        qag=row(small["mla_q_a_gain"][l]), kvag=row(small["mla_kv_a_gain"][l]),
        qg=row(jnp.pad(small["mla_q_gain"][l], (0, SLOT - QK_DIM))), kg=row(jnp.pad(small["mla_k_gain"][l], (0, SLOT - QK_DIM))),
        wq=_pad_slots(w_uq, QK_DIM), wk=_pad_slots(ukv[..., :QK_NOPE].reshape(KV_LORA, -1), QK_NOPE),
        wv=_pad_slots(ukv[..., QK_NOPE:].reshape(KV_LORA, -1), V_DIM),
        mog=row(_pad_slots(small["mla_out_gain"][l], V_DIM)),
        g2=row(small["norm2_gain"][l]),
    )


def _shard_cols(a):
    r, c4 = a.shape
    return a.reshape(r, N_CHIPS, c4 // N_CHIPS).transpose(1, 0, 2)


def local_step(x, positions, target, small, comm):
    s = x.shape[0]
    cos_t, sin_a, sin_b = _rope_tables(positions, s)
    lbs = lower_bounds_fwd(small["hg_lower_bound"])
    lw, saved = [], []
    for l in range(DEPTH):
        w = _layer_weights(comm.part(l, "in"), small, l)
        lw.append(w)
        lb = lbs[l].reshape(-1, 1)
        pa, pb, pc = inproj_fwd(x, w["g1"], w["wa"], w["wb"], w["wc"])
        ya = gm_fwd(pa, w["vg"], w["ws"], w["bs"], w["gog"])
        (yb, states), got = hg_fwd(pb, lb, w["hog"], [comm.gather_rider(l, "ff1")])
        comm.gathered(l, "ff1", got[0])
        q, k, v = mla_pre_fwd(pc, cos_t, sin_a, sin_b, w["qag"], w["kvag"], w["qg"], w["kg"], w["wq"], w["wk"], w["wv"])
        (o, lse), got = attn_fwd(q, k, v, [comm.gather_rider(l, "ff2"), comm.gather_rider(l, "out")])
        comm.gathered(l, "ff2", got[0])
        comm.gathered(l, "out", got[1])
        w.update(_out_weights(comm.part(l, "out")))
        x1 = outproj_fwd(x, ya, yb, o, w["mog"], w["woa"], w["wob"], w["woc"])
        w["w1"], w["w2"] = comm.part(l, "ff1")["w_ff1"], comm.part(l, "ff2")["w_ff2"]
        rider = comm.gather_rider(l + 1, "in") if l + 1 < DEPTH else None
        (x2, r), got = ffn_fwd(x1, w["g2"], w["w1"], w["w2"], [rider])
        comm.gathered(l + 1, "in", got[0])
        saved.append(dict(x=x, pa=pa, pb=pb, pc=pc, ya=ya, yb=yb, states=states, q=q, k=k, v=v, o=o, lse=lse, x1=x1, r=r, lb=lb))
        x = x2
    dx, loss_part = loss_head(x, target)
    groups = [dict() for _ in range(DEPTH)]
    sm = {n: [None] * DEPTH for n in ("norm1_gain", "gm_v_gain", "gm_w_s", "gm_b_s", "gm_out_gain", "hg_out_gain",
                                       "mla_q_a_gain", "mla_kv_a_gain", "mla_q_gain", "mla_k_gain", "mla_out_gain",
                                       "norm2_gain")}
    dlbs = [None] * DEPTH
    halves = lambda g: g.reshape(N_CHIPS, 2, g.shape[1] // 2, g.shape[2])
    take = lambda red, f: None if red is None else f(red)
    red_mix = red_late = None
    for l in reversed(range(DEPTH)):
        w, a = lw[l], saved[l]
        (dx1p, dg2, dw1, dw2, dw1b, dw2b), got = ffn_bwd(a["x1"], w["g2"], w["w1"], w["w2"], a["r"], dx,
                                             [take(red_mix, Reducer.swap_rider), take(red_late, Reducer.share_rider)])
        if red_mix:
            red_mix.after_swap(got[0])
        if red_late:
            comm.reduced(MIX, got[1])
        ffn_arrs = [halves(dw1), halves(dw2)]
        red_ffn = comm.reducer(ffn_arrs, FFN, l, send=[halves(dw1b), halves(dw2b)])
        (dx1, dya, dyb, do, dmog, dwoa, dwob, dwoc), got = outproj_bwd(
            a["ya"], a["yb"], a["o"], w["mog"], w["woa"], w["wob"], w["woc"], dx, dx1p, [take(red_ffn, Reducer.swap_rider)])
        if red_ffn:
            red_ffn.after_swap(got[0])
        (dq, dk, dv), got = attn_bwd(a["q"], a["k"], a["v"], a["o"], do, a["lse"], [take(red_ffn, Reducer.scatter_rider)])
        if red_ffn:
            red_ffn.after_scatter(got[0])
        dpc, dqag, dkvag, dqg, dkg, dwq, dwk, dwv = mla_pre_bwd(a["pc"], cos_t, sin_a, sin_b, w["qag"], w["kvag"], w["qg"],
                                                                  w["kg"], w["wq"], w["wk"], w["wv"], dq, dk, dv)
        (dpb, dlb, dhog), got = hg_bwd(a["pb"], a["lb"], w["hog"], a["states"], dyb,
                                       [take(red_ffn, Reducer.share_rider), take(red_mix, Reducer.scatter_rider)])
        if red_ffn:
            comm.reduced(FFN, got[0])
        else:
            groups[l].update(zip(FFN, ffn_arrs))
        if red_mix:
            red_mix.after_scatter(got[1])
        red_late = red_mix
        dpa, dvg, dws, dbs, dgog = gm_bwd(a["pa"], w["vg"], w["ws"], w["bs"], w["gog"], dya)
        dx, dg1, dwa, dwb, dwc = inproj_bwd(a["x"], w["g1"], w["wa"], w["wb"], w["wc"], dpa, dpb, dpc, dx1)
        dukv = jnp.concatenate([dwk.reshape(KV_LORA, MLA_HEADS, SLOT)[..., :QK_NOPE],
                                dwv.reshape(KV_LORA, MLA_HEADS, SLOT)[..., :V_DIM]], axis=-1)
        dwo = jnp.concatenate([dwoa, dwob, dwoc.reshape(MLA_HEADS, SLOT, D_MODEL)[:, :V_DIM].reshape(-1, D_MODEL)], axis=0)
        mix_arrs = [halves(_shard_cols(jnp.concatenate([dwa, dwb, dwc[:, :1952 - 1536]], axis=1))),
                    halves(_shard_cols(_unpad_slots(dwq, QK_DIM))), halves(_shard_cols(dukv.reshape(KV_LORA, -1))),
                    halves(dwo.reshape(N_CHIPS, -1, D_MODEL))]
        red_mix = comm.reducer(mix_arrs, MIX, l) if l > 0 else None
        if red_mix is None:
            groups[l].update(zip(MIX, mix_arrs))
        sm["norm1_gain"][l] = dg1[0]
        sm["gm_v_gain"][l] = dvg[0]
        sm["gm_w_s"][l] = dws
        sm["gm_b_s"][l] = dbs[..., 0]
        sm["gm_out_gain"][l] = dgog[0]
        sm["hg_out_gain"][l] = dhog[:, 0]
        sm["mla_q_a_gain"][l] = dqag[0]
        sm["mla_kv_a_gain"][l] = dkvag[0]
        sm["mla_q_gain"][l] = dqg[0, :QK_DIM]
        sm["mla_k_gain"][l] = dkg[0, :QK_DIM]
        sm["mla_out_gain"][l] = _unpad_slots(dmog[0], V_DIM)
        sm["norm2_gain"][l] = dg2[0]
        dlbs[l] = dlb[:, 0]
    sm["hg_lower_bound"] = [lower_bounds_bwd(small["hg_lower_bound"], jnp.stack(dlbs))]
    return loss_part, dx, groups, sm, red_late


MIX = ("w_in", "mla_w_uq", "mla_w_ukv", "w_out")
FFN = ("w_ff1", "w_ff2")
BIG = MIX + FFN
PARTS = {"in": ("w_in", "mla_w_uq", "mla_w_ukv"), "out": ("w_out",), "ff1": ("w_ff1",), "ff2": ("w_ff2",)}
SMALL = ("norm1_gain", "gm_v_gain", "gm_w_s", "gm_b_s", "gm_out_gain", "hg_lower_bound", "hg_out_gain",
         "mla_q_a_gain", "mla_kv_a_gain", "mla_q_gain", "mla_k_gain", "mla_out_gain", "norm2_gain")
ORDER = ("norm1_gain", "w_in", "gm_v_gain", "gm_w_s", "gm_b_s", "gm_out_gain", "hg_lower_bound", "hg_out_gain",
         "mla_q_a_gain", "mla_w_uq", "mla_kv_a_gain", "mla_w_ukv", "mla_q_gain", "mla_k_gain", "mla_out_gain",
         "w_out", "norm2_gain", "w_ff1", "w_ff2")
PACK_ROWS = 320


def _pack(pieces):
    flat = jnp.concatenate([a.reshape(-1) for a in pieces])
    total = 2 * N_CHIPS * PACK_ROWS * SLOT
    return jnp.pad(flat, (0, total - flat.shape[0]))


def _unpack(flat, shapes):
    out, off = [], 0
    for sh in shapes:
        size = 1
        for d in sh:
            size *= d
        out.append(flat[off:off + size].reshape(sh))
        off += size
    return out


class ChipComm:
    def __init__(self, shards):
        self.shards = shards
        self.full = {}
        self.grads = {}

    def gather_rider(self, l, part):
        return gather_rider([self.shards[n][l].astype(MXU_DTYPE).reshape(4, self.shards[n].shape[1] // 4, -1)
                             for n in PARTS[part]])

    def gathered(self, l, part, outs):
        if outs is not None:
            self.full[l, part] = {n: o.reshape((N_CHIPS,) + self.shards[n].shape[1:]) for n, o in zip(PARTS[part], outs)}

    def part(self, l, part):
        if (l, part) not in self.full:
            self.gathered(l, part, run_rider(self.gather_rider(l, part), "gather_weights"))
        return self.full[l, part]

    def places(self, names, l):
        return [(l, self.shards[n].shape[0], self.grads.get(n)) for n in names]

    def reducer(self, arrs, names, l, send=None):
        return Reducer(arrs, lambda: self.places(names, l), send)

    def reduced(self, names, outs):
        self.grads.update(zip(names, outs))


def kernel(x, positions, norm1_gain, w_in, gm_v_gain, gm_w_s, gm_b_s, gm_out_gain, hg_lower_bound, hg_out_gain, mla_q_a_gain, mla_w_uq, mla_kv_a_gain, mla_w_ukv, mla_q_gain, mla_k_gain, mla_out_gain, w_out, norm2_gain, w_ff1, w_ff2, loss_target, m_norm1_gain, m_w_in, m_gm_v_gain, m_gm_w_s, m_gm_b_s, m_gm_out_gain, m_hg_lower_bound, m_hg_out_gain, m_mla_q_a_gain, m_mla_w_uq, m_mla_kv_a_gain, m_mla_w_ukv, m_mla_q_gain, m_mla_k_gain, m_mla_out_gain, m_w_out, m_norm2_gain, m_w_ff1, m_w_ff2, v_norm1_gain, v_w_in, v_gm_v_gain, v_gm_w_s, v_gm_b_s, v_gm_out_gain, v_hg_lower_bound, v_hg_out_gain, v_mla_q_a_gain, v_mla_w_uq, v_mla_kv_a_gain, v_mla_w_ukv, v_mla_q_gain, v_mla_k_gain, v_mla_out_gain, v_w_out, v_norm2_gain, v_w_ff1, v_w_ff2):
    given = dict(locals())
    weights = {n: given[n] for n in ORDER}
    moms = {n: given["m_" + n] for n in ORDER}
    vars_ = {n: given["v_" + n] for n in ORDER}
    s, d = x.shape[1], x.shape[2]

    small = {n: weights[n] for n in SMALL}
    comm = ChipComm({n: weights[n] for n in BIG})
    loss_part, dx, groups, small_g, red_late = local_step(x.reshape(s, d), positions, loss_target.reshape(s, d), small, comm)
    loss = lax.psum(loss_part[0, 0], ("x", "y", "c"))

    pack_g = _pack([jnp.stack(small_g[n]) for n in SMALL]).reshape(N_CHIPS, 2, PACK_ROWS, SLOT)
    last = Reducer([groups[0][n] for n in MIX] + [pack_g], lambda: comm.places(MIX, 0) + [(0, 1, None)])
    swap = last.swap_rider()
    outs = run_rider(_merge_riders([swap, red_late.share_rider()]), "swap_halves")
    last.after_swap(outs[:len(swap.out_shapes)])
    comm.reduced(MIX, outs[len(swap.out_shapes):])
    last.after_scatter(run_rider(last.scatter_rider(), "scatter_chips"))
    reduced = run_rider(last.share_rider(), "share_halves")
    comm.reduced(MIX, reduced[:-1])
    pack_full = run_rider(gather_rider([reduced[-1].reshape(4, PACK_ROWS // 2, SLOT)]), "gather_small")[0].reshape(-1)
    grads = {n: comm.grads[n].reshape(weights[n].shape) for n in BIG}
    grads.update(zip(SMALL, _unpack(pack_full, [weights[n].shape for n in SMALL])))

    delta, new_m, new_v = {}, {}, {}
    flat2 = lambda a: a.reshape(-1, a.shape[-1])
    for n in ORDER:
        outs, _ = adamw(flat2(weights[n]), flat2(grads[n]), flat2(moms[n]), flat2(vars_[n]), "adamw_" + n)
        delta[n], new_m[n], new_v[n] = [o.reshape(weights[n].shape) for o in outs]

    return (loss, dx.reshape(x.shape), *[grads[n] for n in ORDER], *[delta[n] for n in ORDER],
            *[new_m[n] for n in ORDER], *[new_v[n] for n in ORDER])
```

```python
import functools

import jax
import jax.numpy as jnp
from jax import lax
from jax.experimental import pallas as pl
from jax.experimental.pallas import tpu as pltpu

F32 = jnp.float32
BF16 = jnp.bfloat16
MXU_DTYPE = BF16

D_MODEL = 1024
DEPTH = 4
CHUNK = 128
HG_CHUNK = 128
HG_CHUNKS = 4
GM_CHUNKS = 4
EPS = 1e-6
HEAD64 = 64
MLA_HEADS = 8
QK_NOPE = 64
QK_ROPE = 32
QK_DIM = 96
V_DIM = 64
Q_LORA = 256
KV_LORA = 128
SLOT = 128
ROPE_THETA = 10000.0
N_CHIPS = 4

ADAM_LR = 0.001
ADAM_B1 = 0.9
ADAM_B2 = 0.999
ADAM_EPS = 1e-08
ADAM_WD = 0.01
ADAM_STEP = 10

TM = 512
TM_FFN = 512
TQ = 256
ATT_HEADS_PER_STEP = 4
ATT_WIDE = 512
V7X_VMEM_BYTES = 64 * 1024 * 1024
VMEM_LIMIT = V7X_VMEM_BYTES * 7 // 8

NN = (((1,), (0,)), ((), ()))
NT = (((1,), (1,)), ((), ()))
TN = (((0,), (0,)), ((), ()))
BNN = (((2,), (1,)), ((0,), (0,)))
BNT = (((2,), (2,)), ((0,), (0,)))
BTN = (((1,), (1,)), ((0,), (0,)))


def _dot(a, b, dims):
    return lax.dot_general(a.astype(MXU_DTYPE), b.astype(MXU_DTYPE), dims, preferred_element_type=F32)


def _hdot(a, b, dims=NN):
    return lax.dot_general(a, b, dims, precision=lax.Precision.HIGHEST, preferred_element_type=F32)


def _make_ad(dims, da_dims, da_swap, db_dims, db_swap):
    @jax.custom_vjp
    def f(a, b):
        return _dot(a, b, dims)

    def fwd(a, b):
        return _dot(a, b, dims), (a, b)

    def bwd(res, g):
        a, b = res
        da = _dot(b, g, da_dims) if da_swap else _dot(g, b, da_dims)
        db = _dot(g, a, db_dims) if db_swap else _dot(a, g, db_dims)
        return da, db

    f.defvjp(fwd, bwd)
    return f


@functools.partial(jax.custom_vjp, nondiff_argnums=(1,))
def _roll_ad(x, shift):
    return pltpu.roll(x, shift, 1)


def _roll_ad_fwd(x, shift):
    return pltpu.roll(x, shift, 1), None


def _roll_ad_bwd(shift, _, g):
    return (pltpu.roll(g, (g.shape[1] - shift) % g.shape[1], 1),)


_roll_ad.defvjp(_roll_ad_fwd, _roll_ad_bwd)


class _Ops:
    pass


PLAIN = _Ops()
PLAIN.mm = lambda a, b: _dot(a, b, NN)
PLAIN.bmm = lambda a, b: _dot(a, b, BNN)
PLAIN.bmm_nt = lambda a, b: _dot(a, b, BNT)
PLAIN.bmm_tn = lambda a, b: _dot(a, b, BTN)
PLAIN.roll = lambda x, s: pltpu.roll(x, s, 1)

AD = _Ops()
AD.mm = _make_ad(NN, NT, False, TN, False)
AD.bmm = _make_ad(BNN, BNT, False, BTN, False)
AD.bmm_nt = _make_ad(BNT, BNN, False, BTN, True)
AD.bmm_tn = _make_ad(BTN, BNT, True, BNN, False)
AD.roll = _roll_ad


def _sigmoid(x):
    return jax.nn.sigmoid(x)


def _gelu(x):
    return 0.5 * x * (1.0 + jnp.tanh(0.7978845608028654 * (x + 0.044715 * (x * x * x))))


def _rms(x, g):
    return x * lax.rsqrt(jnp.mean(x * x, axis=-1, keepdims=True) + EPS) * g


def _head_masks256():
    lane = lax.broadcasted_iota(jnp.int32, (1, 4 * HEAD64), 1)
    return [(jnp.right_shift(lane, 6) == h).astype(F32) for h in range(4)]


def _headnorm256(x, g):
    ms = jnp.zeros_like(x)
    sq = x * x
    for m in _head_masks256():
        ms = ms + m * (jnp.sum(sq * m, axis=-1, keepdims=True) * (1.0 / HEAD64))
    return x * lax.rsqrt(ms + EPS) * g


def _slot_norm(x, g, n):
    return x * lax.rsqrt(jnp.sum(x * x, axis=-1, keepdims=True) * (1.0 / n) + EPS) * g


def _rope(ops, x, cos_t, sin_a, sin_b):
    return x * cos_t + ops.roll(x, SLOT - QK_ROPE // 2) * sin_a + ops.roll(x, QK_ROPE // 2) * sin_b


def _inproj(ops, x, g1, wa, wb, wc):
    h = _rms(x, g1)
    return ops.mm(h, wa), ops.mm(h, wb), ops.mm(h, wc)


def _gm_chunk(ops, ur, vr, vg, ws4, bs, og):
    c = ur.shape[0]
    masks = _head_masks256()
    mh = jnp.concatenate([m[None] for m in masks], axis=0)
    u = _gelu(ur)
    v = _headnorm256(_gelu(vr), vg)
    t = lax.broadcasted_iota(jnp.int32, (c, c), 0)
    s = lax.broadcasted_iota(jnp.int32, (c, c), 1)
    w = jnp.where((t >= s)[None], ws4, 0.0)
    y = jnp.sum(ops.bmm(w, v[None] * mh), axis=0)
    for h in range(4):
        y = y + bs[h] * masks[h]
    return _headnorm256(u * y, og)


def _hg_chunk(ops, st, qr, fr, ir, gr, lb, og):
    c, n = qr.shape
    nh = n // HEAD64
    heads = lambda x: x.reshape(nh, HEAD64, x.shape[-1])
    tr = lambda x: heads(x.T)
    lb4, og4 = heads(lb), heads(og)
    qx = tr(qr)
    q = qx * _sigmoid(qx)
    f = lb4 + (1.0 - lb4) * _sigmoid(tr(fr))
    k = 1.0 - f
    logf = jnp.log(f)
    v = tr(ir)
    gx = tr(gr)
    s = lax.broadcasted_iota(jnp.int32, (c, c), 0)
    t = lax.broadcasted_iota(jnp.int32, (c, c), 1)
    tl = lax.broadcasted_iota(jnp.int32, (1, c), 1).reshape(1, 1, c)
    b2 = _hdot(logf.reshape(n, c), (s <= t).astype(F32))
    b = heads(b2)
    btot = jnp.sum(logf, axis=2, keepdims=True)
    inter = ops.bmm_tn(st, q * jnp.exp(b))
    p4 = jnp.zeros((nh, c, c), F32)
    tt, ss = s, t
    lg = c.bit_length() - 2
    while lg >= 0:
        m = 1 << lg
        bnd = jnp.left_shift(jnp.right_shift(t, lg + 1), lg + 1) + (m - 1)
        r = heads(_hdot(b2, (s == bnd).astype(F32)))
        right = jnp.bitwise_and(jnp.right_shift(tl, lg), 1) == 1
        qe = jnp.where(right, q * jnp.exp(jnp.where(right, b - r, 0.0)), 0.0)
        ke = jnp.where(right, 0.0, k * jnp.exp(jnp.where(right, 0.0, r - b)))
        lm = ((jnp.right_shift(tt, lg + 1) == jnp.right_shift(ss, lg + 1))
              & (jnp.bitwise_and(jnp.right_shift(tt, lg), 1) == 1)
              & (jnp.bitwise_and(jnp.right_shift(ss, lg), 1) == 0))
        p4 = jnp.where(lm[None], ops.bmm_tn(qe, ke), p4)
        lg -= 1
    intra = ops.bmm_nt(v, p4)
    o = inter + intra + jnp.sum(q * k, axis=1, keepdims=True) * v
    st_new = st * jnp.exp(btot) + ops.bmm_nt(k * jnp.exp(btot - b), v)
    y = o * lax.rsqrt(jnp.mean(o * o, axis=1, keepdims=True) + EPS) * og4 * (gx * _sigmoid(gx))
    return st_new, y.reshape(n, c).T


def _mla_pre(ops, cq, ckv, kpe, cos_t, sin_a, sin_b, qag, kvag, qg, kg, wq, wk, wv):
    cqn = _rms(cq, qag)
    ckvn = _rms(ckv, kvag)
    kper = ops.roll(kpe, QK_NOPE)
    qs, ks, vs = [], [], []
    for h in range(MLA_HEADS):
        qh = _slot_norm(ops.mm(cqn, wq[h]), qg, QK_DIM)
        qs.append(_rope(ops, qh, cos_t, sin_a, sin_b))
        kh = _slot_norm(ops.mm(ckvn, wk[h]) + kper, kg, QK_DIM)
        ks.append(_rope(ops, kh, cos_t, sin_a, sin_b))
        vs.append(ops.mm(ckvn, wv[h]))
    return qs, ks, vs


def _outproj(ops, x, ya, yb, o, mog, woa, wob, woc):
    yc = jnp.concatenate([_slot_norm(o[h], mog[h], V_DIM) for h in range(MLA_HEADS)], axis=1)
    return x + ops.mm(ya, woa) + ops.mm(yb, wob) + ops.mm(yc, woc)


def _lower_bounds(r0, r1, r2, r3):
    mx = jnp.maximum(jnp.maximum(r0, r1), jnp.maximum(r2, r3))
    e0, e1, e2, e3 = jnp.exp(r0 - mx), jnp.exp(r1 - mx), jnp.exp(r2 - mx), jnp.exp(r3 - mx)
    inv = 1.0 / (e0 + e1 + e2 + e3)
    s1, s2, s3 = e1 * inv, e2 * inv, e3 * inv
    return jnp.zeros_like(r0), s1, s1 + s2, s1 + s2 + s3


def _cp(sem):
    return pltpu.CompilerParams(dimension_semantics=sem, vmem_limit_bytes=VMEM_LIMIT)


def _rows(tm, n):
    return pl.BlockSpec((tm, n), lambda i: (i, 0))


def _full(a):
    nd = len(a.shape)
    return pl.BlockSpec(a.shape, lambda *_: (0,) * nd, pipeline_mode=pl.Buffered(1))


def _sds(shape, dtype=F32):
    return jax.ShapeDtypeStruct(shape, dtype)


def _acc(ref, val, first):
    @pl.when(first)
    def _():
        ref[...] = val

    @pl.when(jnp.logical_not(first))
    def _():
        ref[...] = ref[...] + val


def _f32(ref):
    return ref[...].astype(F32)


MESH = pl.DeviceIdType.MESH
ANY = pl.BlockSpec(memory_space=pl.ANY)


class Rider:
    def __init__(self, arrays, out_shapes, aliases, sems, start, finish, peers):
        self.arrays, self.out_shapes, self.aliases, self.sems = list(arrays), list(out_shapes), dict(aliases), list(sems)
        self.start, self.finish = start, finish
        self.peers = frozenset(peers)


BARRIER_IDS = {frozenset({"sibling"}): 0, frozenset({"chips"}): 1, frozenset({"sibling", "chips"}): 2}


def _entry_barrier(kinds):
    x, y, c = lax.axis_index("x"), lax.axis_index("y"), lax.axis_index("c")
    peers = [(x, y, 1 - c)] if "sibling" in kinds else []
    if "chips" in kinds:
        peers += [(1 - x, y, c), (x, 1 - y, c), (1 - x, 1 - y, c)]
    barrier = pltpu.get_barrier_semaphore()
    for peer in peers:
        pl.semaphore_signal(barrier, inc=1, device_id=peer, device_id_type=MESH)
    pl.semaphore_wait(barrier, len(peers))


def run_rider(rider, name):
    n_in, n_out = len(rider.arrays), len(rider.out_shapes)

    def body(*refs):
        ins, outs, sems = refs[:n_in], refs[n_in:n_in + n_out], refs[n_in + n_out:]
        _entry_barrier(rider.peers)
        rider.start(ins, outs, sems)
        rider.finish(ins, outs, sems)

    return pl.pallas_call(
        body, name=name, in_specs=[ANY] * n_in, out_specs=[ANY] * n_out, out_shape=rider.out_shapes,
        input_output_aliases=rider.aliases, scratch_shapes=rider.sems,
        compiler_params=pltpu.CompilerParams(collective_id=BARRIER_IDS[rider.peers]),
    )(*rider.arrays)


def _merge_riders(riders):
    bounds, a0, o0, s0 = [], 0, 0, 0
    for r in riders:
        bounds.append((a0, o0, s0))
        a0, o0, s0 = a0 + len(r.arrays), o0 + len(r.out_shapes), s0 + len(r.sems)

    def part(k, ins, outs, sems):
        a, o, s = bounds[k]
        r = riders[k]
        return ins[a:a + len(r.arrays)], outs[o:o + len(r.out_shapes)], sems[s:s + len(r.sems)]

    return Rider(
        [x for r in riders for x in r.arrays], [x for r in riders for x in r.out_shapes],
        {bounds[k][0] + i: bounds[k][1] + o for k, r in enumerate(riders) for i, o in r.aliases.items()},
        [x for r in riders for x in r.sems],
        lambda *refs: [r.start(*part(k, *refs)) for k, r in enumerate(riders)],
        lambda *refs: [r.finish(*part(k, *refs)) for k, r in enumerate(riders)],
        frozenset().union(*[r.peers for r in riders]))


def _ride(compute, riders, *, name, grid, in_specs, out_specs, out_shape, operands, scratch_shapes=(), sem=None):
    single = not isinstance(out_shape, (list, tuple))
    if single:
        out_specs, out_shape = [out_specs], [out_shape]
    live = [r for r in riders if r is not None]
    if not live:
        res = pl.pallas_call(compute, name=name, grid=grid, in_specs=in_specs, out_specs=out_specs, out_shape=out_shape,
                             scratch_shapes=list(scratch_shapes), compiler_params=_cp(sem))(*operands)
        return (res[0] if single else res), [None] * len(riders)
    rider = live[0] if len(live) == 1 else _merge_riders(live)
    n_in, n_out, n_s = len(in_specs), len(out_specs), len(scratch_shapes)
    r_in, r_out = len(rider.arrays), len(rider.out_shapes)

    def body(*refs):
        ins, rins = refs[:n_in], refs[n_in:n_in + r_in]
        outs = refs[n_in + r_in:n_in + r_in + n_out]
        routs = refs[n_in + r_in + n_out:n_in + r_in + n_out + r_out]
        scr = refs[n_in + r_in + n_out + r_out:n_in + r_in + n_out + r_out + n_s]
        rsems = refs[n_in + r_in + n_out + r_out + n_s:]
        first = functools.reduce(jnp.logical_and, [pl.program_id(a) == 0 for a in range(len(grid))])
        last = functools.reduce(jnp.logical_and, [pl.program_id(a) == grid[a] - 1 for a in range(len(grid))])

        @pl.when(first)
        def _():
            _entry_barrier(rider.peers)
            rider.start(rins, routs, rsems)

        compute(*ins, *outs, *scr)

        @pl.when(last)
        def _():
            rider.finish(rins, routs, rsems)

    res = pl.pallas_call(
        body, name=name, grid=grid, in_specs=list(in_specs) + [ANY] * r_in, out_specs=list(out_specs) + [ANY] * r_out,
        out_shape=list(out_shape) + rider.out_shapes,
        input_output_aliases={n_in + k: n_out + v for k, v in rider.aliases.items()},
        scratch_shapes=list(scratch_shapes) + rider.sems,
        compiler_params=pltpu.CompilerParams(dimension_semantics=("arbitrary",) * len(grid), vmem_limit_bytes=VMEM_LIMIT,
                                             collective_id=BARRIER_IDS[rider.peers]),
    )(*operands, *rider.arrays)
    main, rest, per_rider = res[:n_out], list(res[n_out:]), []
    for r in riders:
        per_rider.append(None if r is None else [rest.pop(0) for _ in r.out_shapes])
    return (main[0] if single else main), per_rider


def inproj_fwd(x, g1, wa, wb, wc):
    s, d = x.shape

    def body(x_ref, g_ref, wa_ref, wb_ref, wc_ref, pa_ref, pb_ref, pc_ref):
        pa, pb, pc = _inproj(PLAIN, x_ref[...], g_ref[...], wa_ref[...], wb_ref[...], wc_ref[...])
        pa_ref[...] = pa
        pb_ref[...] = pb
        pc_ref[...] = pc

    return pl.pallas_call(
        body, name="inproj_fwd", grid=(s // TM,),
        in_specs=[_rows(TM, d), _full(g1), _full(wa), _full(wb), _full(wc)],
        out_specs=[_rows(TM, wa.shape[1]), _rows(TM, wb.shape[1]), _rows(TM, wc.shape[1])],
        out_shape=[_sds((s, wa.shape[1])), _sds((s, wb.shape[1])), _sds((s, wc.shape[1]))],
        compiler_params=_cp(("parallel",)),
    )(x, g1, wa, wb, wc)


def inproj_bwd(x, g1, wa, wb, wc, dpa, dpb, dpc, dres):
    s, d = x.shape

    def body(x_ref, g_ref, wa_ref, wb_ref, wc_ref, dpa_ref, dpb_ref, dpc_ref, dres_ref,
             dx_ref, dg_ref, dwa_ref, dwb_ref, dwc_ref):
        first = pl.program_id(0) == 0
        _, vjp = jax.vjp(functools.partial(_inproj, AD), x_ref[...], g_ref[...],
                         _f32(wa_ref), _f32(wb_ref), _f32(wc_ref))
        dx, dg, dwa, dwb, dwc = vjp((dpa_ref[...], dpb_ref[...], dpc_ref[...]))
        dx_ref[...] = dx + dres_ref[...]
        _acc(dg_ref, dg, first)
        _acc(dwa_ref, dwa, first)
        _acc(dwb_ref, dwb, first)
        _acc(dwc_ref, dwc, first)

    return pl.pallas_call(
        body, name="inproj_bwd", grid=(s // TM,),
        in_specs=[_rows(TM, d), _full(g1), _full(wa), _full(wb), _full(wc),
                  _rows(TM, wa.shape[1]), _rows(TM, wb.shape[1]), _rows(TM, wc.shape[1]), _rows(TM, d)],
        out_specs=[_rows(TM, d), _full(g1), _full(wa), _full(wb), _full(wc)],
        out_shape=[_sds((s, d)), _sds(g1.shape), _sds(wa.shape), _sds(wb.shape), _sds(wc.shape)],
        compiler_params=_cp(("arbitrary",)),
    )(x, g1, wa, wb, wc, dpa, dpb, dpc, dres)


def gm_fwd(pa, vg, ws4, bs, og):
    s = pa.shape[0]
    w = pa.shape[1] // 2

    def body(pa_ref, vg_ref, ws_ref, bs_ref, og_ref, ya_ref):
        bsl = [bs_ref[h] for h in range(4)]
        for j in range(GM_CHUNKS):
            rows = slice(j * CHUNK, (j + 1) * CHUNK)
            ya_ref[rows, :] = _gm_chunk(PLAIN, pa_ref[rows, 0:w], pa_ref[rows, w:2 * w], vg_ref[...], ws_ref[...], bsl,
                                        og_ref[...])

    tm = GM_CHUNKS * CHUNK
    return pl.pallas_call(
        body, name="gm_fwd", grid=(s // tm,),
        in_specs=[_rows(tm, 2 * w), _full(vg), _full(ws4), _full(bs), _full(og)],
        out_specs=_rows(tm, w), out_shape=_sds((s, w)),
        compiler_params=_cp(("parallel",)),
    )(pa, vg, ws4, bs, og)


def gm_bwd(pa, vg, ws4, bs, og, dya):
    s = pa.shape[0]
    w = pa.shape[1] // 2

    def body(pa_ref, vg_ref, ws_ref, bs_ref, og_ref, dya_ref, dpa_ref, dvg_ref, dws_ref, dbs_ref, dog_ref):
        first = pl.program_id(0) == 0
        bsl = [bs_ref[h] for h in range(4)]
        tot = None
        for j in range(GM_CHUNKS):
            rows = slice(j * CHUNK, (j + 1) * CHUNK)
            _, vjp = jax.vjp(functools.partial(_gm_chunk, AD), pa_ref[rows, 0:w], pa_ref[rows, w:2 * w],
                             vg_ref[...], ws_ref[...], bsl, og_ref[...])
            du, dv, *dws = vjp(dya_ref[rows, :])
            dpa_ref[rows, 0:w] = du
            dpa_ref[rows, w:2 * w] = dv
            tot = dws if tot is None else jax.tree.map(jnp.add, tot, dws)
        dvg, dws, dbs, dog = tot
        _acc(dvg_ref, dvg, first)
        _acc(dws_ref, dws, first)
        _acc(dog_ref, dog, first)
        for h in range(4):
            _acc(dbs_ref.at[h], dbs[h], first)

    tm = GM_CHUNKS * CHUNK
    return pl.pallas_call(
        body, name="gm_bwd", grid=(s // tm,),
        in_specs=[_rows(tm, 2 * w), _full(vg), _full(ws4), _full(bs), _full(og), _rows(tm, w)],
        out_specs=[_rows(tm, 2 * w), _full(vg), _full(ws4), _full(bs), _full(og)],
        out_shape=[_sds((s, 2 * w)), _sds(vg.shape), _sds(ws4.shape), _sds(bs.shape), _sds(og.shape)],
        compiler_params=_cp(("arbitrary",)),
    )(pa, vg, ws4, bs, og, dya)


def hg_fwd(pb, lb, og, riders=()):
    s = pb.shape[0]
    w = pb.shape[1] // 4
    tm = HG_CHUNKS * HG_CHUNK
    st_shape = (w // HEAD64, HEAD64, HEAD64)

    def body(pb_ref, lb_ref, og_ref, yb_ref, states_ref, st_ref):
        @pl.when(pl.program_id(0) == 0)
        def _():
            st_ref[...] = jnp.zeros_like(st_ref)

        st = st_ref[...]
        for j in range(HG_CHUNKS):
            rows = slice(j * HG_CHUNK, (j + 1) * HG_CHUNK)
            states_ref[j] = st
            st, y = _hg_chunk(PLAIN, st, pb_ref[rows, 0:w], pb_ref[rows, w:2 * w], pb_ref[rows, 2 * w:3 * w],
                              pb_ref[rows, 3 * w:4 * w], lb_ref[...], og_ref[...])
            yb_ref[rows, :] = y
        st_ref[...] = st

    return _ride(
        body, riders, name="hg_fwd", grid=(s // tm,),
        in_specs=[_rows(tm, 4 * w), _full(lb), _full(og)],
        out_specs=[_rows(tm, w), pl.BlockSpec((HG_CHUNKS,) + st_shape, lambda i: (i, 0, 0, 0))],
        out_shape=[_sds((s, w)), _sds((s // HG_CHUNK,) + st_shape)],
        scratch_shapes=[pltpu.VMEM(st_shape, F32)],
        operands=(pb, lb, og), sem=("arbitrary",))


def hg_bwd(pb, lb, og, states, dyb, riders=()):
    s = pb.shape[0]
    w = pb.shape[1] // 4
    tm = HG_CHUNKS * HG_CHUNK
    nc = s // tm
    st_shape = (w // HEAD64, HEAD64, HEAD64)

    def body(pb_ref, lb_ref, og_ref, states_ref, dyb_ref, dpb_ref, dlb_ref, dog_ref, dst_ref):
        first = pl.program_id(0) == 0

        @pl.when(first)
        def _():
            dst_ref[...] = jnp.zeros_like(dst_ref)

        dst, dlb, dog = dst_ref[...], None, None
        for j in reversed(range(HG_CHUNKS)):
            rows = slice(j * HG_CHUNK, (j + 1) * HG_CHUNK)
            _, vjp = jax.vjp(functools.partial(_hg_chunk, AD), states_ref[j], pb_ref[rows, 0:w], pb_ref[rows, w:2 * w],
                             pb_ref[rows, 2 * w:3 * w], pb_ref[rows, 3 * w:4 * w], lb_ref[...], og_ref[...])
            dst, dq, df, di, dg, dlb_j, dog_j = vjp((dst, dyb_ref[rows, :]))
            dpb_ref[rows, 0:w] = dq
            dpb_ref[rows, w:2 * w] = df
            dpb_ref[rows, 2 * w:3 * w] = di
            dpb_ref[rows, 3 * w:4 * w] = dg
            dlb = dlb_j if dlb is None else dlb + dlb_j
            dog = dog_j if dog is None else dog + dog_j
        dst_ref[...] = dst
        _acc(dlb_ref, dlb, first)
        _acc(dog_ref, dog, first)

    rev = lambda i: (nc - 1 - i, 0)
    return _ride(
        body, riders, name="hg_bwd", grid=(nc,),
        in_specs=[pl.BlockSpec((tm, 4 * w), rev), _full(lb), _full(og),
                  pl.BlockSpec((HG_CHUNKS,) + st_shape, lambda i: (nc - 1 - i, 0, 0, 0)), pl.BlockSpec((tm, w), rev)],
        out_specs=[pl.BlockSpec((tm, 4 * w), rev), _full(lb), _full(og)],
        out_shape=[_sds((s, 4 * w)), _sds(lb.shape), _sds(og.shape)],
        scratch_shapes=[pltpu.VMEM(st_shape, F32)],
        operands=(pb, lb, og, states, dyb), sem=("arbitrary",))


def lower_bounds_fwd(hlb):
    def body(h_ref, o_ref):
        outs = _lower_bounds(*[h_ref[pl.ds(i, 1), :] for i in range(DEPTH)])
        for i in range(DEPTH):
            o_ref[pl.ds(i, 1), :] = outs[i]

    return pl.pallas_call(body, name="lower_bounds_fwd", out_shape=_sds(hlb.shape))(hlb)


def lower_bounds_bwd(hlb, dlbs):
    def body(h_ref, d_ref, o_ref):
        _, vjp = jax.vjp(_lower_bounds, *[h_ref[pl.ds(i, 1), :] for i in range(DEPTH)])
        outs = vjp(tuple(d_ref[pl.ds(i, 1), :] for i in range(DEPTH)))
        for i in range(DEPTH):
            o_ref[pl.ds(i, 1), :] = outs[i]

    return pl.pallas_call(body, name="lower_bounds_bwd", out_shape=_sds(hlb.shape))(hlb, dlbs)


def _mla_pre_args(pc_ref, cos_ref, sa_ref, sb_ref, qag_ref, kvag_ref, qg_ref, kg_ref, wq_ref, wk_ref, wv_ref, cast):
    sl = lambda h: slice(h * SLOT, (h + 1) * SLOT)
    ld = (lambda r, h: r[:, sl(h)].astype(F32)) if cast else (lambda r, h: r[:, sl(h)])
    diff = (pc_ref[:, 0:Q_LORA], pc_ref[:, Q_LORA:Q_LORA + KV_LORA], pc_ref[:, Q_LORA + KV_LORA:Q_LORA + 2 * KV_LORA],
            qag_ref[...], kvag_ref[...], qg_ref[...], kg_ref[...],
            [ld(wq_ref, h) for h in range(MLA_HEADS)], [ld(wk_ref, h) for h in range(MLA_HEADS)],
            [ld(wv_ref, h) for h in range(MLA_HEADS)])
    tables = (cos_ref[...], sa_ref[...], sb_ref[...])
    return diff, tables


def _mla_pre_fn(ops, tables, cq, ckv, kpe, qag, kvag, qg, kg, wq, wk, wv):
    return _mla_pre(ops, cq, ckv, kpe, *tables, qag, kvag, qg, kg, wq, wk, wv)


def mla_pre_fwd(pc, cos_t, sin_a, sin_b, qag, kvag, qg, kg, wq, wk, wv):
    s = pc.shape[0]
    hw = MLA_HEADS * SLOT

    def body(pc_ref, cos_ref, sa_ref, sb_ref, qag_ref, kvag_ref, qg_ref, kg_ref, wq_ref, wk_ref, wv_ref,
             q_ref, k_ref, v_ref):
        diff, tables = _mla_pre_args(pc_ref, cos_ref, sa_ref, sb_ref, qag_ref, kvag_ref, qg_ref, kg_ref,
                                     wq_ref, wk_ref, wv_ref, False)
        qs, ks, vs = _mla_pre_fn(PLAIN, tables, *diff)
        ones_lane = (lax.broadcasted_iota(jnp.int32, (1, SLOT), 1) == V_DIM).astype(F32)
        for h in range(MLA_HEADS):
            q_ref[:, h * SLOT:(h + 1) * SLOT] = qs[h].astype(q_ref.dtype)
            k_ref[:, h * SLOT:(h + 1) * SLOT] = ks[h].astype(k_ref.dtype)
            v_ref[:, h * SLOT:(h + 1) * SLOT] = (vs[h] + ones_lane).astype(v_ref.dtype)

    return pl.pallas_call(
        body, name="mla_pre_fwd", grid=(s // TM,),
        in_specs=[_rows(TM, pc.shape[1]), _rows(TM, SLOT), _rows(TM, SLOT), _rows(TM, SLOT),
                  _full(qag), _full(kvag), _full(qg), _full(kg), _full(wq), _full(wk), _full(wv)],
        out_specs=[_rows(TM, hw)] * 3, out_shape=[_sds((s, hw), MXU_DTYPE)] * 3,
        compiler_params=_cp(("parallel",)),
    )(pc, cos_t, sin_a, sin_b, qag, kvag, qg, kg, wq, wk, wv)


def mla_pre_bwd(pc, cos_t, sin_a, sin_b, qag, kvag, qg, kg, wq, wk, wv, dq, dk, dv):
    s = pc.shape[0]
    hw = MLA_HEADS * SLOT

    def body(pc_ref, cos_ref, sa_ref, sb_ref, qag_ref, kvag_ref, qg_ref, kg_ref, wq_ref, wk_ref, wv_ref,
             dq_ref, dk_ref, dv_ref, dpc_ref, dqag_ref, dkvag_ref, dqg_ref, dkg_ref, dwq_ref, dwk_ref, dwv_ref):
        first = pl.program_id(0) == 0
        diff, tables = _mla_pre_args(pc_ref, cos_ref, sa_ref, sb_ref, qag_ref, kvag_ref, qg_ref, kg_ref,
                                     wq_ref, wk_ref, wv_ref, True)
        _, vjp = jax.vjp(functools.partial(_mla_pre_fn, AD, tables), *diff)
        sl = lambda h: slice(h * SLOT, (h + 1) * SLOT)
        cot = ([dq_ref[:, sl(h)] for h in range(MLA_HEADS)], [dk_ref[:, sl(h)] for h in range(MLA_HEADS)],
               [dv_ref[:, sl(h)] for h in range(MLA_HEADS)])
        dcq, dckv, dkpe, dqag, dkvag, dqg, dkg, dwq, dwk, dwv = vjp(cot)
        dpc_ref[:, 0:Q_LORA] = dcq
        dpc_ref[:, Q_LORA:Q_LORA + KV_LORA] = dckv
        dpc_ref[:, Q_LORA + KV_LORA:Q_LORA + 2 * KV_LORA] = dkpe
        _acc(dqag_ref, dqag, first)
        _acc(dkvag_ref, dkvag, first)
        _acc(dqg_ref, dqg, first)
        _acc(dkg_ref, dkg, first)
        for h in range(MLA_HEADS):
            _acc(dwq_ref.at[:, sl(h)], dwq[h], first)
            _acc(dwk_ref.at[:, sl(h)], dwk[h], first)
            _acc(dwv_ref.at[:, sl(h)], dwv[h], first)

    return pl.pallas_call(
        body, name="mla_pre_bwd", grid=(s // TM,),
        in_specs=[_rows(TM, pc.shape[1]), _rows(TM, SLOT), _rows(TM, SLOT), _rows(TM, SLOT),
                  _full(qag), _full(kvag), _full(qg), _full(kg), _full(wq), _full(wk), _full(wv),
                  _rows(TM, hw), _rows(TM, hw), _rows(TM, hw)],
        out_specs=[_rows(TM, pc.shape[1]), _full(qag), _full(kvag), _full(qg), _full(kg),
                   _full(wq), _full(wk), _full(wv)],
        out_shape=[_sds(pc.shape), _sds(qag.shape), _sds(kvag.shape), _sds(qg.shape), _sds(kg.shape),
                   _sds(wq.shape), _sds(wk.shape), _sds(wv.shape)],
        compiler_params=_cp(("arbitrary",)),
    )(pc, cos_t, sin_a, sin_b, qag, kvag, qg, kg, wq, wk, wv, dq, dk, dv)


ATT_SCALE = QK_DIM ** -0.5
NEG_BIG = -1e30


def attn_fwd(q, k, v, riders=()):
    s = q.shape[0]
    nq = s // TQ
    hp = ATT_HEADS_PER_STEP
    sl = lambda j: slice(j * SLOT, (j + 1) * SLOT)

    wide = ATT_WIDE // TQ

    def body(q_ref, k_ref, v_ref, o_ref, lse_ref):
        qi = pl.program_id(1)
        lane = lax.broadcasted_iota(jnp.int32, (1, SLOT), 1)
        qs = [q_ref[:, sl(j)] for j in range(hp)]

        def scores(ki, n_tiles):
            rk = pl.ds(pl.multiple_of(ki * TQ, TQ), n_tiles * TQ)
            return tuple(_dot(qs[j], k_ref[rk, sl(j)], NT) for j in range(hp))

        def update(ki, carry, raw, n_tiles, masked):
            rk = pl.ds(pl.multiple_of(ki * TQ, TQ), n_tiles * TQ)
            if masked:
                row = lax.broadcasted_iota(jnp.int32, (TQ, n_tiles * TQ), 0) + (n_tiles - 1) * TQ
                col = lax.broadcasted_iota(jnp.int32, (TQ, n_tiles * TQ), 1)
            out = []
            for j in range(hp):
                m, acc = carry[j]
                sc = raw[j] * ATT_SCALE
                if masked:
                    sc = jnp.where(row >= col, sc, NEG_BIG)
                m_new = jnp.maximum(m, jnp.max(sc, axis=-1, keepdims=True))
                acc = jnp.exp(m - m_new) * acc + _dot(jnp.exp(sc - m_new), v_ref[rk, sl(j)], NN)
                out.append((m_new, acc))
            return tuple(out)

        def step(ki, carry, n_tiles, masked):
            return update(ki, carry, scores(ki, n_tiles), n_tiles, masked)

        def wide_step(kw, state):
            carry, raw = state
            ahead = scores(jnp.minimum(kw + 1, jnp.maximum(n_wide - 1, 0)) * wide, wide)
            return update(kw * wide, carry, raw, wide, False), ahead

        def tail_single(cr):
            cr = lax.fori_loop(n_wide * wide, qi, lambda ki, c: step(ki, c, 1, False), cr)
            return step(qi, cr, 1, True)

        n_wide = qi // wide
        init = tuple((jnp.full((TQ, 1), NEG_BIG, F32), jnp.zeros((TQ, SLOT), F32)) for _ in range(hp))
        carry, _ = lax.fori_loop(0, n_wide, wide_step, (init, scores(0, wide)))
        carry = lax.cond(qi % wide == wide - 1, lambda cr: step(qi - (wide - 1), cr, wide, True), tail_single, carry)
        for j in range(hp):
            m, acc = carry[j]
            l = jnp.sum(jnp.where(lane == V_DIM, acc, 0.0), axis=-1, keepdims=True)
            o_ref[:, sl(j)] = jnp.where(lane < V_DIM, acc / l, 0.0)
            lse_ref[j] = m + jnp.log(l)

    head_col = pl.BlockSpec((s, hp * SLOT), lambda g, i: (0, g))
    tile = pl.BlockSpec((TQ, hp * SLOT), lambda g, i: (i, g))
    return _ride(
        body, riders, name="attn_fwd", grid=(MLA_HEADS // hp, nq),
        in_specs=[tile, head_col, head_col],
        out_specs=[tile, pl.BlockSpec((hp, TQ, 1), lambda g, i: (g, i, 0))],
        out_shape=[_sds((s, MLA_HEADS * SLOT)), _sds((MLA_HEADS, s, 1))],
        operands=(q, k, v), sem=("parallel", "parallel"))


def attn_bwd(q, k, v, o, do, lse, riders=()):
    s = q.shape[0]
    nq = s // TQ
    hp = ATT_HEADS_PER_STEP
    sl = lambda j: slice(j * SLOT, (j + 1) * SLOT)
    wide = ATT_WIDE // TQ

    def body(q_ref, k_ref, v_ref, o_ref, do_ref, lse_ref, dq_ref, dk_ref, dv_ref, delta_ref):
        ki = pl.program_id(1)

        @pl.when(ki == 0)
        def _():
            dq_ref[...] = jnp.zeros_like(dq_ref)

            def prep(i, c):
                rows = pl.ds(pl.multiple_of(i * TQ, TQ), TQ)
                for j in range(hp):
                    delta_ref[j, rows, :] = jnp.sum(do_ref[rows, sl(j)] * o_ref[rows, sl(j)], axis=-1, keepdims=True)
                return c

            lax.fori_loop(0, nq, prep, 0)

        kks = [k_ref[:, sl(j)] for j in range(hp)]
        vvs = [v_ref[:, sl(j)] for j in range(hp)]

        def step(qi, carry, n_tiles, masked):
            rq = pl.ds(pl.multiple_of(qi * TQ, TQ), n_tiles * TQ)
            if masked:
                row = lax.broadcasted_iota(jnp.int32, (n_tiles * TQ, TQ), 0)
                col = lax.broadcasted_iota(jnp.int32, (n_tiles * TQ, TQ), 1)
            out = []
            for j in range(hp):
                dk, dv = carry[j]
                qq = q_ref[rq, sl(j)]
                dd = do_ref[rq, sl(j)]
                sc = _dot(qq, kks[j], NT) * ATT_SCALE
                if masked:
                    sc = jnp.where(row >= col, sc, NEG_BIG)
                p = jnp.exp(sc - lse_ref[j, rq, :])
                dv = dv + _dot(p, dd, TN)
                ds = p * (_dot(dd, vvs[j], NT) - delta_ref[j, rq, :]) * ATT_SCALE
                dk = dk + _dot(ds, qq, TN)
                dq_ref[rq, sl(j)] = dq_ref[rq, sl(j)] + _dot(ds, kks[j], NN)
                out.append((dk, dv))
            return tuple(out)

        def head_single(cr):
            cr = step(ki, cr, 1, True)
            return lax.fori_loop(ki + 1, first_wide * wide, lambda qi, c: step(qi, c, 1, False), cr)

        zero = jnp.zeros((TQ, SLOT), F32)
        first_wide = (ki + wide) // wide
        carry = tuple((zero, zero) for _ in range(hp))
        carry = lax.cond(ki % wide == 0, lambda cr: step(ki, cr, wide, True), head_single, carry)
        carry = lax.fori_loop(first_wide, nq // wide, lambda qw, cr: step(qw * wide, cr, wide, False), carry)
        for j in range(hp):
            dk_ref[:, sl(j)] = carry[j][0]
            dv_ref[:, sl(j)] = carry[j][1]

    head_col = pl.BlockSpec((s, hp * SLOT), lambda g, i: (0, g))
    tile = pl.BlockSpec((TQ, hp * SLOT), lambda g, i: (i, g))
    return _ride(
        body, riders, name="attn_bwd", grid=(MLA_HEADS // hp, nq),
        in_specs=[head_col, tile, tile, head_col, head_col, pl.BlockSpec((hp, s, 1), lambda g, i: (g, 0, 0))],
        out_specs=[head_col, tile, tile],
        out_shape=[_sds((s, MLA_HEADS * SLOT))] * 3,
        scratch_shapes=[pltpu.VMEM((hp, s, 1), F32)],
        operands=(q, k, v, o, do, lse), sem=("arbitrary", "arbitrary"))


def _outproj_args(ya_ref, yb_ref, o_ref, mog_ref, woa_ref, wob_ref, woc_ref, cast):
    sl = lambda h: slice(h * SLOT, (h + 1) * SLOT)
    ldw = (lambda r: r[...].astype(F32)) if cast else (lambda r: r[...])
    return (ya_ref[...], yb_ref[...], [o_ref[:, sl(h)] for h in range(MLA_HEADS)],
            [mog_ref[:, sl(h)] for h in range(MLA_HEADS)], ldw(woa_ref), ldw(wob_ref), ldw(woc_ref))


def outproj_fwd(x, ya, yb, o, mog, woa, wob, woc):
    s, d = x.shape

    def body(x_ref, ya_ref, yb_ref, o_ref, mog_ref, woa_ref, wob_ref, woc_ref, x1_ref):
        x1_ref[...] = _outproj(PLAIN, x_ref[...], *_outproj_args(ya_ref, yb_ref, o_ref, mog_ref, woa_ref, wob_ref,
                                                                  woc_ref, False))

    return pl.pallas_call(
        body, name="outproj_fwd", grid=(s // TM,),
        in_specs=[_rows(TM, d), _rows(TM, ya.shape[1]), _rows(TM, yb.shape[1]), _rows(TM, o.shape[1]),
                  _full(mog), _full(woa), _full(wob), _full(woc)],
        out_specs=_rows(TM, d), out_shape=_sds((s, d)),
        compiler_params=_cp(("parallel",)),
    )(x, ya, yb, o, mog, woa, wob, woc)


def outproj_bwd(ya, yb, o, mog, woa, wob, woc, dx2, dx1p, riders=()):
    s, d = dx2.shape
    npart = dx1p.shape[0]

    def body(ya_ref, yb_ref, o_ref, mog_ref, woa_ref, wob_ref, woc_ref, dx2_ref, dx1p_ref,
             dx1_ref, dya_ref, dyb_ref, do_ref, dmog_ref, dwoa_ref, dwob_ref, dwoc_ref):
        first = pl.program_id(0) == 0
        sl = lambda h: slice(h * SLOT, (h + 1) * SLOT)
        dx1 = dx2_ref[...]
        for p in range(npart):
            dx1 = dx1 + dx1p_ref[p]
        dx1_ref[...] = dx1
        args = _outproj_args(ya_ref, yb_ref, o_ref, mog_ref, woa_ref, wob_ref, woc_ref, True)
        _, vjp = jax.vjp(lambda *a: _outproj(AD, jnp.zeros_like(dx1), *a), *args)
        dya, dyb, do, dmog, dwoa, dwob, dwoc = vjp(dx1)
        dya_ref[...] = dya
        dyb_ref[...] = dyb
        _acc(dwoa_ref, dwoa, first)
        _acc(dwob_ref, dwob, first)
        _acc(dwoc_ref, dwoc, first)
        for h in range(MLA_HEADS):
            do_ref[:, sl(h)] = do[h]
            _acc(dmog_ref.at[:, sl(h)], dmog[h], first)

    return _ride(
        body, riders, name="outproj_bwd", grid=(s // TM,),
        in_specs=[_rows(TM, ya.shape[1]), _rows(TM, yb.shape[1]), _rows(TM, o.shape[1]),
                  _full(mog), _full(woa), _full(wob), _full(woc), _rows(TM, d),
                  pl.BlockSpec((npart, TM, d), lambda i: (0, i, 0))],
        out_specs=[_rows(TM, d), _rows(TM, ya.shape[1]), _rows(TM, yb.shape[1]), _rows(TM, o.shape[1]),
                   _full(mog), _full(woa), _full(wob), _full(woc)],
        out_shape=[_sds((s, d)), _sds(ya.shape), _sds(yb.shape), _sds(o.shape),
                   _sds(mog.shape), _sds(woa.shape), _sds(wob.shape), _sds(woc.shape)],
        operands=(ya, yb, o, mog, woa, wob, woc, dx2, dx1p), sem=("arbitrary",))


def ffn_fwd(x1, g2, w1, w2, riders=()):
    s, d = x1.shape
    npart, _, fs = w1.shape

    def body(x1_ref, g_ref, w1_ref, w2_ref, x2_ref, r_ref):
        p = pl.program_id(1)
        x1v = x1_ref[...]
        r = jnp.maximum(PLAIN.mm(_rms(x1v, g_ref[...]), w1_ref[...]), 0.0)
        r_ref[...] = r.astype(r_ref.dtype)
        part = PLAIN.mm(r * r, w2_ref[...])

        @pl.when(p == 0)
        def _():
            x2_ref[...] = x1v + part

        @pl.when(p != 0)
        def _():
            x2_ref[...] = x2_ref[...] + part

    tm = min(2 * TM_FFN, s)
    return _ride(
        body, riders, name="ffn_fwd", grid=(s // tm, npart),
        in_specs=[pl.BlockSpec((tm, d), lambda i, p: (i, 0)), pl.BlockSpec(g2.shape, lambda i, p: (0, 0)),
                  pl.BlockSpec((None, d, fs), lambda i, p: (p, 0, 0)), pl.BlockSpec((None, fs, d), lambda i, p: (p, 0, 0))],
        out_specs=[pl.BlockSpec((tm, d), lambda i, p: (i, 0)), pl.BlockSpec((tm, fs), lambda i, p: (i, p))],
        out_shape=[_sds((s, d)), _sds((s, npart * fs), MXU_DTYPE)],
        operands=(x1, g2, w1, w2), sem=("parallel", "arbitrary"))


def ffn_bwd(x1, g2, w1, w2, r, dx2, riders=()):
    s, d = x1.shape
    npart, _, fs = w1.shape
    tm = TM_FFN

    def body(x1_ref, g_ref, w1_ref, w2_ref, r_ref, dx2_ref, dx1p_ref, dg_ref, dw1_ref, dw2_ref, dw1b_ref, dw2b_ref):
        p = pl.program_id(0)
        i = pl.program_id(1)
        h2, vjp_norm = jax.vjp(_rms, x1_ref[...], g_ref[...])
        rr = r_ref[...].astype(F32)
        dy = dx2_ref[...]
        da = _dot(dy, w2_ref[...], NT) * (2.0 * rr)
        dx1, dg = vjp_norm(_dot(da, w1_ref[...], NT))
        dx1p_ref[...] = dx1
        _acc(dg_ref, dg, (p == 0) & (i == 0))
        _acc(dw1_ref, _dot(h2, da, TN), i == 0)
        _acc(dw2_ref, _dot(rr * rr, dy, TN), i == 0)

        @pl.when(i == s // tm - 1)
        def _():
            dw1b_ref[...] = dw1_ref[...].astype(BF16)
            dw2b_ref[...] = dw2_ref[...].astype(BF16)

    return _ride(
        body, riders, name="ffn_bwd", grid=(npart, s // tm),
        in_specs=[pl.BlockSpec((tm, d), lambda p, i: (i, 0)), pl.BlockSpec(g2.shape, lambda p, i: (0, 0)),
                  pl.BlockSpec((None, d, fs), lambda p, i: (p, 0, 0)), pl.BlockSpec((None, fs, d), lambda p, i: (p, 0, 0)),
                  pl.BlockSpec((tm, fs), lambda p, i: (i, p)), pl.BlockSpec((tm, d), lambda p, i: (i, 0))],
        out_specs=[pl.BlockSpec((None, tm, d), lambda p, i: (p, i, 0)), pl.BlockSpec(g2.shape, lambda p, i: (0, 0)),
                   pl.BlockSpec((None, d, fs), lambda p, i: (p, 0, 0)), pl.BlockSpec((None, fs, d), lambda p, i: (p, 0, 0)),
                   pl.BlockSpec((None, d, fs), lambda p, i: (p, 0, 0)), pl.BlockSpec((None, fs, d), lambda p, i: (p, 0, 0))],
        out_shape=[_sds((npart, s, d)), _sds(g2.shape), _sds(w1.shape), _sds(w2.shape),
                   _sds(w1.shape, BF16), _sds(w2.shape, BF16)],
        operands=(x1, g2, w1, w2, r, dx2), sem=("arbitrary", "arbitrary"))


def loss_head(y, target):
    s, d = y.shape

    def body(y_ref, t_ref, dy_ref, loss_ref):
        err = y_ref[...] - t_ref[...]
        dy_ref[...] = err * (1.0 / d)
        part = jnp.sum(jnp.sum(err * err, axis=-1, keepdims=True), axis=0, keepdims=True) * (0.5 / d)
        _acc(loss_ref, jnp.broadcast_to(part, loss_ref.shape), pl.program_id(0) == 0)

    return pl.pallas_call(
        body, name="loss_head", grid=(s // TM,),
        in_specs=[_rows(TM, d), _rows(TM, d)],
        out_specs=[_rows(TM, d), pl.BlockSpec((1, SLOT), lambda i: (0, 0))],
        out_shape=[_sds((s, d)), _sds((1, SLOT))],
        compiler_params=_cp(("arbitrary",)),
    )(y, target)


def _row_block(r):
    for b in (512, 256, 128, 64, 32, 16, 8):
        if r % b == 0:
            return b
    return r


def sum_cores(arrs, gots, half, me):
    n = len(arrs)

    def body(sp_ref, *refs):
        for i in range(n):
            a_ref, g_ref, wire_ref, own_ref = refs[i], refs[n + i], refs[2 * n + i], refs[3 * n + i]
            tot = a_ref[...] + g_ref[...]
            wire_ref[...] = tot.astype(wire_ref.dtype)

            @pl.when(pl.program_id(0) == sp_ref[1])
            def _(own_ref=own_ref, tot=tot):
                own_ref[...] = tot

    shapes = [a.shape[2:] for a in arrs]
    grid_spec = pltpu.PrefetchScalarGridSpec(
        num_scalar_prefetch=1, grid=(N_CHIPS,),
        in_specs=[pl.BlockSpec((None, None) + sh, lambda p, sp: (p, sp[0], 0, 0)) for sh in shapes]
        + [pl.BlockSpec((None,) + sh, lambda p, sp: (p, 0, 0)) for sh in shapes],
        out_specs=[pl.BlockSpec((None,) + sh, lambda p, sp: (p, 0, 0)) for sh in shapes]
        + [pl.BlockSpec(sh, lambda p, sp: (0, 0)) for sh in shapes])
    outs = pl.pallas_call(body, name="sum_cores", grid_spec=grid_spec,
                          out_shape=[_sds((N_CHIPS,) + sh, BF16) for sh in shapes] + [_sds(sh) for sh in shapes],
                          compiler_params=_cp(("arbitrary",)))(jnp.stack([half, me]).astype(jnp.int32), *arrs, *gots)
    return outs[:n], outs[n:]


SUM_STEPS = 4


def sum_chips(owns, recvs, half, places):
    n = len(owns)
    dests = [d for _, _, d in places if d is not None]

    def body(sp_ref, *refs):
        del sp_ref
        for i in range(n):
            own_ref, out_ref = refs[4 * i], refs[4 * n + len(dests) + i]
            r0, r1, r2 = (refs[4 * i + 1 + j][...].astype(F32) for j in range(3))
            out_ref[...] = ((own_ref[...] + r0) + r1) + r2

    in_specs, out_specs, operands, aliases = [], [], [], {}
    for i, (own, recv, (layer, _, dest)) in enumerate(zip(owns, recvs, places)):
        r, c = own.shape
        br = r // SUM_STEPS
        in_specs.append(pl.BlockSpec((br, c), lambda i, sp: (i, 0)))
        in_specs += [pl.BlockSpec((None, br, c), functools.partial(lambda i, sp, j: (j, i, 0), j=j)) for j in range(3)]
        out_specs.append(pl.BlockSpec((None, None, br, c), functools.partial(lambda i, sp, l: (l, sp[0], i, 0), l=layer)))
        operands += [own, recv, recv, recv]
        if dest is not None:
            aliases[1 + 4 * n + len(aliases)] = i
    grid_spec = pltpu.PrefetchScalarGridSpec(num_scalar_prefetch=1, grid=(SUM_STEPS,), in_specs=in_specs + [ANY] * len(dests),
                                             out_specs=out_specs)
    return pl.pallas_call(body, name="sum_chips", grid_spec=grid_spec, input_output_aliases=aliases,
                          out_shape=[_sds((nl, 2) + o.shape) for o, (_, nl, _) in zip(owns, places)],
                          compiler_params=_cp(("parallel",)))(half.reshape(1).astype(jnp.int32), *operands, *dests)


def adamw(w, g, m, v, name, riders=()):
    r, c = w.shape
    br = _row_block(r)
    c1 = 1.0 / (1.0 - ADAM_B1 ** ADAM_STEP)
    c2 = 1.0 / (1.0 - ADAM_B2 ** ADAM_STEP)

    def body(w_ref, g_ref, m_ref, v_ref, d_ref, nm_ref, nv_ref):
        gg = g_ref[...]
        nm = ADAM_B1 * m_ref[...] + (1.0 - ADAM_B1) * gg
        nv = ADAM_B2 * v_ref[...] + (1.0 - ADAM_B2) * (gg * gg)
        d_ref[...] = -ADAM_LR * ((nm * c1) / (jnp.sqrt(nv * c2) + ADAM_EPS) + ADAM_WD * w_ref[...])
        nm_ref[...] = nm
        nv_ref[...] = nv

    return _ride(body, riders, name=name, grid=(r // br,), in_specs=[_rows(br, c)] * 4, out_specs=[_rows(br, c)] * 3,
                 out_shape=[_sds((r, c))] * 3, operands=(w, g, m, v), sem=("parallel",))


def _place():
    x, y, c = lax.axis_index("x"), lax.axis_index("y"), lax.axis_index("c")
    chips = [(1 - x, y), (x, 1 - y), (1 - x, 1 - y)]
    return x, y, c, chips


def _remote(src, dst, send_sem, recv_sem, to):
    return pltpu.make_async_remote_copy(src_ref=src, dst_ref=dst, send_sem=send_sem, recv_sem=recv_sem,
                                        device_id=to, device_id_type=MESH)


def gather_rider(arrs):
    n = len(arrs)
    me_chip = 2 * lax.axis_index("x") + lax.axis_index("y")
    bufs = [lax.dynamic_update_index_in_dim(lax.empty((N_CHIPS,) + a.shape, a.dtype), a, me_chip, 0) for a in arrs]

    def plan(ins, outs, sems):
        send_sems, recv_sems = sems
        x, y, c, chips = _place()
        me = 2 * x + y
        half, other, sibling = pl.ds(2 * c, 2), pl.ds(2 - 2 * c, 2), (x, y, 1 - c)
        cp = lambda i, k, src, dst, to: _remote(src, dst, send_sems.at[i, k], recv_sems.at[i, k], to)
        pairs = [(i, j, cx, cy) for i in range(n) for j, (cx, cy) in enumerate(chips)]
        blk = lambda i, cx, cy, part: outs[i].at[2 * cx + cy, part]
        first = lambda: [cp(i, j, ins[i].at[half], outs[i].at[me, half], (cx, cy, c)) for i, j, cx, cy in pairs]
        landed = lambda: [cp(i, j, blk(i, cx, cy, half), blk(i, cx, cy, half), (cx, cy, c)) for i, j, cx, cy in pairs]
        passed = lambda: [cp(i, 3 + j, blk(i, cx, cy, half), blk(i, cx, cy, half), sibling) for i, j, cx, cy in pairs]
        from_sibling = lambda: [cp(i, 3 + j, blk(i, cx, cy, other), blk(i, cx, cy, other), sibling) for i, j, cx, cy in pairs]
        return first, landed, passed, from_sibling

    def start(ins, outs, sems):
        for cp in plan(ins, outs, sems)[0]():
            cp.start()

    def finish(ins, outs, sems):
        first, landed, passed, from_sibling = plan(ins, outs, sems)
        forwards = passed()
        for a, b in zip(landed(), forwards):
            a.wait_recv()
            b.start()
        for cp in from_sibling():
            cp.wait_recv()
        for cp in first() + forwards:
            cp.wait_send()

    return Rider(list(arrs) + bufs, [_sds((N_CHIPS,) + a.shape, a.dtype) for a in arrs], {n + i: i for i in range(n)},
                 [pltpu.SemaphoreType.DMA((n, 6)), pltpu.SemaphoreType.DMA((n, 6))], start, finish, {"sibling", "chips"})


class Reducer:
    def __init__(self, arrs, places=None, send=None):
        self.a = list(arrs)
        self.send = list(send) if send is not None else self.a
        self.n = len(self.a)
        self.places = places if places is not None else [(0, 1, None)] * self.n
        self.c = lax.axis_index("c")
        self.me = 2 * lax.axis_index("x") + lax.axis_index("y")

    def swap_rider(self):
        n = self.n

        def plan(ins, outs, sems):
            x, y, c, _ = _place()
            return [_remote(ins[i].at[p, 1 - c], outs[i].at[p], sems[0].at[i, p], sems[1].at[i, p], (x, y, 1 - c))
                    for i in range(n) for p in range(N_CHIPS)]

        return Rider(self.send, [_sds((N_CHIPS,) + a.shape[2:], a.dtype) for a in self.send], {},
                     [pltpu.SemaphoreType.DMA((n, N_CHIPS)), pltpu.SemaphoreType.DMA((n, N_CHIPS))],
                     lambda *r: [cp.start() for cp in plan(*r)], lambda *r: [cp.wait() for cp in plan(*r)], {"sibling"})

    def after_swap(self, got):
        self.wire, self.own = sum_cores(self.a, got, self.c, self.me)

    def scatter_rider(self):
        n = self.n

        def plan(ins, outs, sems):
            x, y, c, chips = _place()
            return [_remote(ins[i].at[2 * cx + cy], outs[i].at[j], sems[0].at[i, j], sems[1].at[i, j], (cx, cy, c))
                    for i in range(n) for j, (cx, cy) in enumerate(chips)]

        return Rider(self.wire, [_sds((3,) + w.shape[1:], w.dtype) for w in self.wire], {},
                     [pltpu.SemaphoreType.DMA((n, 3)), pltpu.SemaphoreType.DMA((n, 3))],
                     lambda *r: [cp.start() for cp in plan(*r)], lambda *r: [cp.wait() for cp in plan(*r)], {"chips"})

    def after_scatter(self, recv):
        if callable(self.places):
            self.places = self.places()
        self.full = sum_chips(self.own, recv, self.c, self.places)

    def share_rider(self):
        n = self.n
        layers = [layer for layer, _, _ in self.places]

        def plan(ins, outs, sems):
            x, y, c, _ = _place()
            return [_remote(ins[i].at[layers[i], c], outs[i].at[layers[i], c], sems[0].at[i], sems[1].at[i], (x, y, 1 - c))
                    for i in range(n)]

        return Rider(self.full, [_sds(f.shape) for f in self.full], {i: i for i in range(n)},
                     [pltpu.SemaphoreType.DMA((n,)), pltpu.SemaphoreType.DMA((n,))],
                     lambda *r: [cp.start() for cp in plan(*r)], lambda *r: [cp.wait() for cp in plan(*r)], {"sibling"})

    def run(self):
        self.after_swap(run_rider(self.swap_rider(), "swap_halves"))
        self.after_scatter(run_rider(self.scatter_rider(), "scatter_chips"))
        return run_rider(self.share_rider(), "share_halves")


def _pad_slots(a, live):
    lead = a.shape[:-1]
    a = a.reshape(lead + (MLA_HEADS, live))
    a = jnp.pad(a, [(0, 0)] * len(lead) + [(0, 0), (0, SLOT - live)])
    return a.reshape(lead + (MLA_HEADS * SLOT,))


def _unpad_slots(a, live):
    lead = a.shape[:-1]
    return a.reshape(lead + (MLA_HEADS, SLOT))[..., :live].reshape(lead + (MLA_HEADS * live,))


def _rope_tables(positions, s):
    half = QK_ROPE // 2
    inv_freq = ROPE_THETA ** (-jnp.arange(half, dtype=F32) / half)
    ang = positions.reshape(s).astype(F32)[:, None] * inv_freq[None, :]
    cos, sin = jnp.cos(ang), jnp.sin(ang)
    one = jnp.ones((s, QK_NOPE), F32)
    z64, z16, z32 = jnp.zeros((s, QK_NOPE), F32), jnp.zeros((s, half), F32), jnp.zeros((s, SLOT - QK_DIM), F32)
    cos_t = jnp.concatenate([one, cos, cos, z32], axis=1)
    sin_a = jnp.concatenate([z64, -sin, z16, z32], axis=1)
    sin_b = jnp.concatenate([z64, z16, sin, z32], axis=1)
    return cos_t, sin_a, sin_b


def _out_weights(full):
    w_out = jnp.concatenate([full["w_out"][p] for p in range(N_CHIPS)], axis=0)
    woc = w_out[512:].reshape(MLA_HEADS, V_DIM, D_MODEL)
    woc = jnp.pad(woc, ((0, 0), (0, SLOT - V_DIM), (0, 0))).reshape(MLA_HEADS * SLOT, D_MODEL)
    return dict(woa=w_out[:256], wob=w_out[256:512], woc=woc)


def _layer_weights(full, small, l):
    w_in = jnp.concatenate([full["w_in"][p] for p in range(N_CHIPS)], axis=1)
    wc = jnp.pad(w_in[:, 1536:], ((0, 0), (0, 512 - (w_in.shape[1] - 1536))))
    w_uq = jnp.concatenate([full["mla_w_uq"][p] for p in range(N_CHIPS)], axis=1)
    w_ukv = jnp.concatenate([full["mla_w_ukv"][p] for p in range(N_CHIPS)], axis=1)
    ukv = w_ukv.reshape(KV_LORA, MLA_HEADS, QK_NOPE + V_DIM)
    row = lambda a: a.reshape(1, -1)
    return dict(
        g1=row(small["norm1_gain"][l]), wa=w_in[:, :512], wb=w_in[:, 512:1536], wc=wc,
        vg=row(small["gm_v_gain"][l]), ws=small["gm_w_s"][l], bs=small["gm_b_s"][l].reshape(4, CHUNK, 1),
        gog=row(small["gm_out_gain"][l]), hog=small["hg_out_gain"][l].reshape(-1, 1),
        qag=row(small["mla_q_a_gain"][l]), kvag=row(small["mla_kv_a_gain"][l]),
        qg=row(jnp.pad(small["mla_q_gain"][l], (0, SLOT - QK_DIM))), kg=row(jnp.pad(small["mla_k_gain"][l], (0, SLOT - QK_DIM))),
        wq=_pad_slots(w_uq, QK_DIM), wk=_pad_slots(ukv[..., :QK_NOPE].reshape(KV_LORA, -1), QK_NOPE),
        wv=_pad_slots(ukv[..., QK_NOPE:].reshape(KV_LORA, -1), V_DIM),
        mog=row(_pad_slots(small["mla_out_gain"][l], V_DIM)),
        g2=row(small["norm2_gain"][l]),
    )


def _shard_cols(a):
    r, c4 = a.shape
    return a.reshape(r, N_CHIPS, c4 // N_CHIPS).transpose(1, 0, 2)


def local_step(x, positions, target, small, comm):
    s = x.shape[0]
    cos_t, sin_a, sin_b = _rope_tables(positions, s)
    lbs = lower_bounds_fwd(small["hg_lower_bound"])
    lw, saved = [], []
    for l in range(DEPTH):
        w = _layer_weights(comm.part(l, "in"), small, l)
        lw.append(w)
        lb = lbs[l].reshape(-1, 1)
        pa, pb, pc = inproj_fwd(x, w["g1"], w["wa"], w["wb"], w["wc"])
        ya = gm_fwd(pa, w["vg"], w["ws"], w["bs"], w["gog"])
        (yb, states), got = hg_fwd(pb, lb, w["hog"], [comm.gather_rider(l, "ff1")])
        comm.gathered(l, "ff1", got[0])
        q, k, v = mla_pre_fwd(pc, cos_t, sin_a, sin_b, w["qag"], w["kvag"], w["qg"], w["kg"], w["wq"], w["wk"], w["wv"])
        (o, lse), got = attn_fwd(q, k, v, [comm.gather_rider(l, "ff2"), comm.gather_rider(l, "out")])
        comm.gathered(l, "ff2", got[0])
        comm.gathered(l, "out", got[1])
        w.update(_out_weights(comm.part(l, "out")))
        x1 = outproj_fwd(x, ya, yb, o, w["mog"], w["woa"], w["wob"], w["woc"])
        w["w1"], w["w2"] = comm.part(l, "ff1")["w_ff1"], comm.part(l, "ff2")["w_ff2"]
        rider = comm.gather_rider(l + 1, "in") if l + 1 < DEPTH else None
        (x2, r), got = ffn_fwd(x1, w["g2"], w["w1"], w["w2"], [rider])
        comm.gathered(l + 1, "in", got[0])
        saved.append(dict(x=x, pa=pa, pb=pb, pc=pc, ya=ya, yb=yb, states=states, q=q, k=k, v=v, o=o, lse=lse, x1=x1, r=r, lb=lb))
        x = x2
    dx, loss_part = loss_head(x, target)
    groups = [dict() for _ in range(DEPTH)]
    sm = {n: [None] * DEPTH for n in ("norm1_gain", "gm_v_gain", "gm_w_s", "gm_b_s", "gm_out_gain", "hg_out_gain",
                                       "mla_q_a_gain", "mla_kv_a_gain", "mla_q_gain", "mla_k_gain", "mla_out_gain",
                                       "norm2_gain")}
    dlbs = [None] * DEPTH
    halves = lambda g: g.reshape(N_CHIPS, 2, g.shape[1] // 2, g.shape[2])
    take = lambda red, f: None if red is None else f(red)
    red_mix = red_late = None
    for l in reversed(range(DEPTH)):
        w, a = lw[l], saved[l]
        (dx1p, dg2, dw1, dw2, dw1b, dw2b), got = ffn_bwd(a["x1"], w["g2"], w["w1"], w["w2"], a["r"], dx,
                                             [take(red_mix, Reducer.swap_rider), take(red_late, Reducer.share_rider)])
        if red_mix:
            red_mix.after_swap(got[0])
        if red_late:
            comm.reduced(MIX, got[1])
        ffn_arrs = [halves(dw1), halves(dw2)]
        red_ffn = comm.reducer(ffn_arrs, FFN, l, send=[halves(dw1b), halves(dw2b)])
        (dx1, dya, dyb, do, dmog, dwoa, dwob, dwoc), got = outproj_bwd(
            a["ya"], a["yb"], a["o"], w["mog"], w["woa"], w["wob"], w["woc"], dx, dx1p, [take(red_ffn, Reducer.swap_rider)])
        if red_ffn:
            red_ffn.after_swap(got[0])
        (dq, dk, dv), got = attn_bwd(a["q"], a["k"], a["v"], a["o"], do, a["lse"], [take(red_ffn, Reducer.scatter_rider)])
        if red_ffn:
            red_ffn.after_scatter(got[0])
        dpc, dqag, dkvag, dqg, dkg, dwq, dwk, dwv = mla_pre_bwd(a["pc"], cos_t, sin_a, sin_b, w["qag"], w["kvag"], w["qg"],
                                                                  w["kg"], w["wq"], w["wk"], w["wv"], dq, dk, dv)
        (dpb, dlb, dhog), got = hg_bwd(a["pb"], a["lb"], w["hog"], a["states"], dyb,
                                       [take(red_ffn, Reducer.share_rider), take(red_mix, Reducer.scatter_rider)])
        if red_ffn:
            comm.reduced(FFN, got[0])
        else:
            groups[l].update(zip(FFN, ffn_arrs))
        if red_mix:
            red_mix.after_scatter(got[1])
        red_late = red_mix
        dpa, dvg, dws, dbs, dgog = gm_bwd(a["pa"], w["vg"], w["ws"], w["bs"], w["gog"], dya)
        dx, dg1, dwa, dwb, dwc = inproj_bwd(a["x"], w["g1"], w["wa"], w["wb"], w["wc"], dpa, dpb, dpc, dx1)
        dukv = jnp.concatenate([dwk.reshape(KV_LORA, MLA_HEADS, SLOT)[..., :QK_NOPE],
                                dwv.reshape(KV_LORA, MLA_HEADS, SLOT)[..., :V_DIM]], axis=-1)
        dwo = jnp.concatenate([dwoa, dwob, dwoc.reshape(MLA_HEADS, SLOT, D_MODEL)[:, :V_DIM].reshape(-1, D_MODEL)], axis=0)
        mix_arrs = [halves(_shard_cols(jnp.concatenate([dwa, dwb, dwc[:, :1952 - 1536]], axis=1))),
                    halves(_shard_cols(_unpad_slots(dwq, QK_DIM))), halves(_shard_cols(dukv.reshape(KV_LORA, -1))),
                    halves(dwo.reshape(N_CHIPS, -1, D_MODEL))]
        red_mix = comm.reducer(mix_arrs, MIX, l) if l > 0 else None
        if red_mix is None:
            groups[l].update(zip(MIX, mix_arrs))
        sm["norm1_gain"][l] = dg1[0]
        sm["gm_v_gain"][l] = dvg[0]
        sm["gm_w_s"][l] = dws
        sm["gm_b_s"][l] = dbs[..., 0]
        sm["gm_out_gain"][l] = dgog[0]
        sm["hg_out_gain"][l] = dhog[:, 0]
        sm["mla_q_a_gain"][l] = dqag[0]
        sm["mla_kv_a_gain"][l] = dkvag[0]
        sm["mla_q_gain"][l] = dqg[0, :QK_DIM]
        sm["mla_k_gain"][l] = dkg[0, :QK_DIM]
        sm["mla_out_gain"][l] = _unpad_slots(dmog[0], V_DIM)
        sm["norm2_gain"][l] = dg2[0]
        dlbs[l] = dlb[:, 0]
    sm["hg_lower_bound"] = [lower_bounds_bwd(small["hg_lower_bound"], jnp.stack(dlbs))]
    return loss_part, dx, groups, sm, red_late


MIX = ("w_in", "mla_w_uq", "mla_w_ukv", "w_out")
FFN = ("w_ff1", "w_ff2")
BIG = MIX + FFN
PARTS = {"in": ("w_in", "mla_w_uq", "mla_w_ukv"), "out": ("w_out",), "ff1": ("w_ff1",), "ff2": ("w_ff2",)}
SMALL = ("norm1_gain", "gm_v_gain", "gm_w_s", "gm_b_s", "gm_out_gain", "hg_lower_bound", "hg_out_gain",
         "mla_q_a_gain", "mla_kv_a_gain", "mla_q_gain", "mla_k_gain", "mla_out_gain", "norm2_gain")
ORDER = ("norm1_gain", "w_in", "gm_v_gain", "gm_w_s", "gm_b_s", "gm_out_gain", "hg_lower_bound", "hg_out_gain",
         "mla_q_a_gain", "mla_w_uq", "mla_kv_a_gain", "mla_w_ukv", "mla_q_gain", "mla_k_gain", "mla_out_gain",
         "w_out", "norm2_gain", "w_ff1", "w_ff2")
PACK_ROWS = 320


def _pack(pieces):
    flat = jnp.concatenate([a.reshape(-1) for a in pieces])
    total = 2 * N_CHIPS * PACK_ROWS * SLOT
    return jnp.pad(flat, (0, total - flat.shape[0]))


def _unpack(flat, shapes):
    out, off = [], 0
    for sh in shapes:
        size = 1
        for d in sh:
            size *= d
        out.append(flat[off:off + size].reshape(sh))
        off += size
    return out


class ChipComm:
    def __init__(self, shards):
        self.shards = shards
        self.full = {}
        self.grads = {}

    def gather_rider(self, l, part):
        return gather_rider([self.shards[n][l].astype(MXU_DTYPE).reshape(4, self.shards[n].shape[1] // 4, -1)
                             for n in PARTS[part]])

    def gathered(self, l, part, outs):
        if outs is not None:
            self.full[l, part] = {n: o.reshape((N_CHIPS,) + self.shards[n].shape[1:]) for n, o in zip(PARTS[part], outs)}

    def part(self, l, part):
        if (l, part) not in self.full:
            self.gathered(l, part, run_rider(self.gather_rider(l, part), "gather_weights"))
        return self.full[l, part]

    def places(self, names, l):
        return [(l, self.shards[n].shape[0], self.grads.get(n)) for n in names]

    def reducer(self, arrs, names, l, send=None):
        return Reducer(arrs, lambda: self.places(names, l), send)

    def reduced(self, names, outs):
        self.grads.update(zip(names, outs))


def kernel(x, positions, norm1_gain, w_in, gm_v_gain, gm_w_s, gm_b_s, gm_out_gain, hg_lower_bound, hg_out_gain, mla_q_a_gain, mla_w_uq, mla_kv_a_gain, mla_w_ukv, mla_q_gain, mla_k_gain, mla_out_gain, w_out, norm2_gain, w_ff1, w_ff2, loss_target, m_norm1_gain, m_w_in, m_gm_v_gain, m_gm_w_s, m_gm_b_s, m_gm_out_gain, m_hg_lower_bound, m_hg_out_gain, m_mla_q_a_gain, m_mla_w_uq, m_mla_kv_a_gain, m_mla_w_ukv, m_mla_q_gain, m_mla_k_gain, m_mla_out_gain, m_w_out, m_norm2_gain, m_w_ff1, m_w_ff2, v_norm1_gain, v_w_in, v_gm_v_gain, v_gm_w_s, v_gm_b_s, v_gm_out_gain, v_hg_lower_bound, v_hg_out_gain, v_mla_q_a_gain, v_mla_w_uq, v_mla_kv_a_gain, v_mla_w_ukv, v_mla_q_gain, v_mla_k_gain, v_mla_out_gain, v_w_out, v_norm2_gain, v_w_ff1, v_w_ff2):
    given = dict(locals())
    weights = {n: given[n] for n in ORDER}
    moms = {n: given["m_" + n] for n in ORDER}
    vars_ = {n: given["v_" + n] for n in ORDER}
    s, d = x.shape[1], x.shape[2]

    small = {n: weights[n] for n in SMALL}
    comm = ChipComm({n: weights[n] for n in BIG})
    loss_part, dx, groups, small_g, red_late = local_step(x.reshape(s, d), positions, loss_target.reshape(s, d), small, comm)
    loss = lax.psum(loss_part[0, 0], ("x", "y", "c"))

    pack_g = _pack([jnp.stack(small_g[n]) for n in SMALL]).reshape(N_CHIPS, 2, PACK_ROWS, SLOT)
    last = Reducer([groups[0][n] for n in MIX] + [pack_g], lambda: comm.places(MIX, 0) + [(0, 1, None)])
    swap = last.swap_rider()
    outs = run_rider(_merge_riders([swap, red_late.share_rider()]), "swap_halves")
    last.after_swap(outs[:len(swap.out_shapes)])
    comm.reduced(MIX, outs[len(swap.out_shapes):])
    last.after_scatter(run_rider(last.scatter_rider(), "scatter_chips"))
    reduced = run_rider(last.share_rider(), "share_halves")
    comm.reduced(MIX, reduced[:-1])
    pack_full = run_rider(gather_rider([reduced[-1].reshape(4, PACK_ROWS // 2, SLOT)]), "gather_small")[0].reshape(-1)
    grads = {n: comm.grads[n].reshape(weights[n].shape) for n in BIG}
    grads.update(zip(SMALL, _unpack(pack_full, [weights[n].shape for n in SMALL])))

    delta, new_m, new_v = {}, {}, {}
    flat2 = lambda a: a.reshape(-1, a.shape[-1])
    for n in ORDER:
        outs, _ = adamw(flat2(weights[n]), flat2(grads[n]), flat2(moms[n]), flat2(vars_[n]), "adamw_" + n)
        delta[n], new_m[n], new_v[n] = [o.reshape(weights[n].shape) for o in outs]

    return (loss, dx.reshape(x.shape), *[grads[n] for n in ORDER], *[delta[n] for n in ORDER],
            *[new_m[n] for n in ORDER], *[new_v[n] for n in ORDER])
```

```python
import functools

import jax
import jax.numpy as jnp
from jax import lax
from jax.experimental import pallas as pl
from jax.experimental.pallas import tpu as pltpu

F32 = jnp.float32
BF16 = jnp.bfloat16
MXU_DTYPE = BF16

D_MODEL = 1024
DEPTH = 4
CHUNK = 128
HG_CHUNK = 128
HG_CHUNKS = 4
GM_CHUNKS = 4
EPS = 1e-6
HEAD64 = 64
MLA_HEADS = 8
QK_NOPE = 64
QK_ROPE = 32
QK_DIM = 96
V_DIM = 64
Q_LORA = 256
KV_LORA = 128
SLOT = 128
ROPE_THETA = 10000.0
N_CHIPS = 4

ADAM_LR = 0.001
ADAM_B1 = 0.9
ADAM_B2 = 0.999
ADAM_EPS = 1e-08
ADAM_WD = 0.01
ADAM_STEP = 10

TM = 512
TM_FFN = 512
TQ = 256
ATT_HEADS_PER_STEP = 4
ATT_WIDE = 512
V7X_VMEM_BYTES = 64 * 1024 * 1024
VMEM_LIMIT = V7X_VMEM_BYTES * 7 // 8

NN = (((1,), (0,)), ((), ()))
NT = (((1,), (1,)), ((), ()))
TN = (((0,), (0,)), ((), ()))
BNN = (((2,), (1,)), ((0,), (0,)))
BNT = (((2,), (2,)), ((0,), (0,)))
BTN = (((1,), (1,)), ((0,), (0,)))


def _dot(a, b, dims):
    return lax.dot_general(a.astype(MXU_DTYPE), b.astype(MXU_DTYPE), dims, preferred_element_type=F32)


def _hdot(a, b, dims=NN):
    return lax.dot_general(a, b, dims, precision=lax.Precision.HIGHEST, preferred_element_type=F32)


def _make_ad(dims, da_dims, da_swap, db_dims, db_swap):
    @jax.custom_vjp
    def f(a, b):
        return _dot(a, b, dims)

    def fwd(a, b):
        return _dot(a, b, dims), (a, b)

    def bwd(res, g):
        a, b = res
        da = _dot(b, g, da_dims) if da_swap else _dot(g, b, da_dims)
        db = _dot(g, a, db_dims) if db_swap else _dot(a, g, db_dims)
        return da, db

    f.defvjp(fwd, bwd)
    return f


@functools.partial(jax.custom_vjp, nondiff_argnums=(1,))
def _roll_ad(x, shift):
    return pltpu.roll(x, shift, 1)


def _roll_ad_fwd(x, shift):
    return pltpu.roll(x, shift, 1), None


def _roll_ad_bwd(shift, _, g):
    return (pltpu.roll(g, (g.shape[1] - shift) % g.shape[1], 1),)


_roll_ad.defvjp(_roll_ad_fwd, _roll_ad_bwd)


class _Ops:
    pass


PLAIN = _Ops()
PLAIN.mm = lambda a, b: _dot(a, b, NN)
PLAIN.bmm = lambda a, b: _dot(a, b, BNN)
PLAIN.bmm_nt = lambda a, b: _dot(a, b, BNT)
PLAIN.bmm_tn = lambda a, b: _dot(a, b, BTN)
PLAIN.roll = lambda x, s: pltpu.roll(x, s, 1)

AD = _Ops()
AD.mm = _make_ad(NN, NT, False, TN, False)
AD.bmm = _make_ad(BNN, BNT, False, BTN, False)
AD.bmm_nt = _make_ad(BNT, BNN, False, BTN, True)
AD.bmm_tn = _make_ad(BTN, BNT, True, BNN, False)
AD.roll = _roll_ad


def _sigmoid(x):
    return jax.nn.sigmoid(x)


def _gelu(x):
    return 0.5 * x * (1.0 + jnp.tanh(0.7978845608028654 * (x + 0.044715 * (x * x * x))))


def _rms(x, g):
    return x * lax.rsqrt(jnp.mean(x * x, axis=-1, keepdims=True) + EPS) * g


def _head_masks256():
    lane = lax.broadcasted_iota(jnp.int32, (1, 4 * HEAD64), 1)
    return [(jnp.right_shift(lane, 6) == h).astype(F32) for h in range(4)]


def _headnorm256(x, g):
    ms = jnp.zeros_like(x)
    sq = x * x
    for m in _head_masks256():
        ms = ms + m * (jnp.sum(sq * m, axis=-1, keepdims=True) * (1.0 / HEAD64))
    return x * lax.rsqrt(ms + EPS) * g


def _slot_norm(x, g, n):
    return x * lax.rsqrt(jnp.sum(x * x, axis=-1, keepdims=True) * (1.0 / n) + EPS) * g


def _rope(ops, x, cos_t, sin_a, sin_b):
    return x * cos_t + ops.roll(x, SLOT - QK_ROPE // 2) * sin_a + ops.roll(x, QK_ROPE // 2) * sin_b


def _inproj(ops, x, g1, wa, wb, wc):
    h = _rms(x, g1)
    return ops.mm(h, wa), ops.mm(h, wb), ops.mm(h, wc)


def _gm_chunk(ops, ur, vr, vg, ws4, bs, og):
    c = ur.shape[0]
    masks = _head_masks256()
    mh = jnp.concatenate([m[None] for m in masks], axis=0)
    u = _gelu(ur)
    v = _headnorm256(_gelu(vr), vg)
    t = lax.broadcasted_iota(jnp.int32, (c, c), 0)
    s = lax.broadcasted_iota(jnp.int32, (c, c), 1)
    w = jnp.where((t >= s)[None], ws4, 0.0)
    y = jnp.sum(ops.bmm(w, v[None] * mh), axis=0)
    for h in range(4):
        y = y + bs[h] * masks[h]
    return _headnorm256(u * y, og)


def _hg_chunk(ops, st, qr, fr, ir, gr, lb, og):
    c, n = qr.shape
    nh = n // HEAD64
    heads = lambda x: x.reshape(nh, HEAD64, x.shape[-1])
    tr = lambda x: heads(x.T)
    lb4, og4 = heads(lb), heads(og)
    qx = tr(qr)
    q = qx * _sigmoid(qx)
    f = lb4 + (1.0 - lb4) * _sigmoid(tr(fr))
    k = 1.0 - f
    logf = jnp.log(f)
    v = tr(ir)
    gx = tr(gr)
    s = lax.broadcasted_iota(jnp.int32, (c, c), 0)
    t = lax.broadcasted_iota(jnp.int32, (c, c), 1)
    tl = lax.broadcasted_iota(jnp.int32, (1, c), 1).reshape(1, 1, c)
    b2 = _hdot(logf.reshape(n, c), (s <= t).astype(F32))
    b = heads(b2)
    btot = jnp.sum(logf, axis=2, keepdims=True)
    inter = ops.bmm_tn(st, q * jnp.exp(b))
    p4 = jnp.zeros((nh, c, c), F32)
    tt, ss = s, t
    lg = c.bit_length() - 2
    while lg >= 0:
        m = 1 << lg
        bnd = jnp.left_shift(jnp.right_shift(t, lg + 1), lg + 1) + (m - 1)
        r = heads(_hdot(b2, (s == bnd).astype(F32)))
        right = jnp.bitwise_and(jnp.right_shift(tl, lg), 1) == 1
        qe = jnp.where(right, q * jnp.exp(jnp.where(right, b - r, 0.0)), 0.0)
        ke = jnp.where(right, 0.0, k * jnp.exp(jnp.where(right, 0.0, r - b)))
        lm = ((jnp.right_shift(tt, lg + 1) == jnp.right_shift(ss, lg + 1))
              & (jnp.bitwise_and(jnp.right_shift(tt, lg), 1) == 1)
              & (jnp.bitwise_and(jnp.right_shift(ss, lg), 1) == 0))
        p4 = jnp.where(lm[None], ops.bmm_tn(qe, ke), p4)
        lg -= 1
    intra = ops.bmm_nt(v, p4)
    o = inter + intra + jnp.sum(q * k, axis=1, keepdims=True) * v
    st_new = st * jnp.exp(btot) + ops.bmm_nt(k * jnp.exp(btot - b), v)
    y = o * lax.rsqrt(jnp.mean(o * o, axis=1, keepdims=True) + EPS) * og4 * (gx * _sigmoid(gx))
    return st_new, y.reshape(n, c).T


def _mla_pre(ops, cq, ckv, kpe, cos_t, sin_a, sin_b, qag, kvag, qg, kg, wq, wk, wv):
    cqn = _rms(cq, qag)
    ckvn = _rms(ckv, kvag)
    kper = ops.roll(kpe, QK_NOPE)
    qs, ks, vs = [], [], []
    for h in range(MLA_HEADS):
        qh = _slot_norm(ops.mm(cqn, wq[h]), qg, QK_DIM)
        qs.append(_rope(ops, qh, cos_t, sin_a, sin_b))
        kh = _slot_norm(ops.mm(ckvn, wk[h]) + kper, kg, QK_DIM)
        ks.append(_rope(ops, kh, cos_t, sin_a, sin_b))
        vs.append(ops.mm(ckvn, wv[h]))
    return qs, ks, vs


def _outproj(ops, x, ya, yb, o, mog, woa, wob, woc):
    yc = jnp.concatenate([_slot_norm(o[h], mog[h], V_DIM) for h in range(MLA_HEADS)], axis=1)
    return x + ops.mm(ya, woa) + ops.mm(yb, wob) + ops.mm(yc, woc)


def _lower_bounds(r0, r1, r2, r3):
    mx = jnp.maximum(jnp.maximum(r0, r1), jnp.maximum(r2, r3))
    e0, e1, e2, e3 = jnp.exp(r0 - mx), jnp.exp(r1 - mx), jnp.exp(r2 - mx), jnp.exp(r3 - mx)
    inv = 1.0 / (e0 + e1 + e2 + e3)
    s1, s2, s3 = e1 * inv, e2 * inv, e3 * inv
    return jnp.zeros_like(r0), s1, s1 + s2, s1 + s2 + s3


def _cp(sem):
    return pltpu.CompilerParams(dimension_semantics=sem, vmem_limit_bytes=VMEM_LIMIT)


def _rows(tm, n):
    return pl.BlockSpec((tm, n), lambda i: (i, 0))


def _full(a):
    nd = len(a.shape)
    return pl.BlockSpec(a.shape, lambda *_: (0,) * nd, pipeline_mode=pl.Buffered(1))


def _sds(shape, dtype=F32):
    return jax.ShapeDtypeStruct(shape, dtype)


def _acc(ref, val, first):
    @pl.when(first)
    def _():
        ref[...] = val

    @pl.when(jnp.logical_not(first))
    def _():
        ref[...] = ref[...] + val


def _f32(ref):
    return ref[...].astype(F32)


MESH = pl.DeviceIdType.MESH
ANY = pl.BlockSpec(memory_space=pl.ANY)


class Rider:
    def __init__(self, arrays, out_shapes, aliases, sems, start, finish, peers):
        self.arrays, self.out_shapes, self.aliases, self.sems = list(arrays), list(out_shapes), dict(aliases), list(sems)
        self.start, self.finish = start, finish
        self.peers = frozenset(peers)


BARRIER_IDS = {frozenset({"sibling"}): 0, frozenset({"chips"}): 1, frozenset({"sibling", "chips"}): 2}


def _entry_barrier(kinds):
    x, y, c = lax.axis_index("x"), lax.axis_index("y"), lax.axis_index("c")
    peers = [(x, y, 1 - c)] if "sibling" in kinds else []
    if "chips" in kinds:
        peers += [(1 - x, y, c), (x, 1 - y, c), (1 - x, 1 - y, c)]
    barrier = pltpu.get_barrier_semaphore()
    for peer in peers:
        pl.semaphore_signal(barrier, inc=1, device_id=peer, device_id_type=MESH)
    pl.semaphore_wait(barrier, len(peers))


def run_rider(rider, name):
    n_in, n_out = len(rider.arrays), len(rider.out_shapes)

    def body(*refs):
        ins, outs, sems = refs[:n_in], refs[n_in:n_in + n_out], refs[n_in + n_out:]
        _entry_barrier(rider.peers)
        rider.start(ins, outs, sems)
        rider.finish(ins, outs, sems)

    return pl.pallas_call(
        body, name=name, in_specs=[ANY] * n_in, out_specs=[ANY] * n_out, out_shape=rider.out_shapes,
        input_output_aliases=rider.aliases, scratch_shapes=rider.sems,
        compiler_params=pltpu.CompilerParams(collective_id=BARRIER_IDS[rider.peers]),
    )(*rider.arrays)


def _merge_riders(riders):
    bounds, a0, o0, s0 = [], 0, 0, 0
    for r in riders:
        bounds.append((a0, o0, s0))
        a0, o0, s0 = a0 + len(r.arrays), o0 + len(r.out_shapes), s0 + len(r.sems)

    def part(k, ins, outs, sems):
        a, o, s = bounds[k]
        r = riders[k]
        return ins[a:a + len(r.arrays)], outs[o:o + len(r.out_shapes)], sems[s:s + len(r.sems)]

    return Rider(
        [x for r in riders for x in r.arrays], [x for r in riders for x in r.out_shapes],
        {bounds[k][0] + i: bounds[k][1] + o for k, r in enumerate(riders) for i, o in r.aliases.items()},
        [x for r in riders for x in r.sems],
        lambda *refs: [r.start(*part(k, *refs)) for k, r in enumerate(riders)],
        lambda *refs: [r.finish(*part(k, *refs)) for k, r in enumerate(riders)],
        frozenset().union(*[r.peers for r in riders]))


def _ride(compute, riders, *, name, grid, in_specs, out_specs, out_shape, operands, scratch_shapes=(), sem=None):
    single = not isinstance(out_shape, (list, tuple))
    if single:
        out_specs, out_shape = [out_specs], [out_shape]
    live = [r for r in riders if r is not None]
    if not live:
        res = pl.pallas_call(compute, name=name, grid=grid, in_specs=in_specs, out_specs=out_specs, out_shape=out_shape,
                             scratch_shapes=list(scratch_shapes), compiler_params=_cp(sem))(*operands)
        return (res[0] if single else res), [None] * len(riders)
    rider = live[0] if len(live) == 1 else _merge_riders(live)
    n_in, n_out, n_s = len(in_specs), len(out_specs), len(scratch_shapes)
    r_in, r_out = len(rider.arrays), len(rider.out_shapes)

    def body(*refs):
        ins, rins = refs[:n_in], refs[n_in:n_in + r_in]
        outs = refs[n_in + r_in:n_in + r_in + n_out]
        routs = refs[n_in + r_in + n_out:n_in + r_in + n_out + r_out]
        scr = refs[n_in + r_in + n_out + r_out:n_in + r_in + n_out + r_out + n_s]
        rsems = refs[n_in + r_in + n_out + r_out + n_s:]
        first = functools.reduce(jnp.logical_and, [pl.program_id(a) == 0 for a in range(len(grid))])
        last = functools.reduce(jnp.logical_and, [pl.program_id(a) == grid[a] - 1 for a in range(len(grid))])

        @pl.when(first)
        def _():
            _entry_barrier(rider.peers)
            rider.start(rins, routs, rsems)

        compute(*ins, *outs, *scr)

        @pl.when(last)
        def _():
            rider.finish(rins, routs, rsems)

    res = pl.pallas_call(
        body, name=name, grid=grid, in_specs=list(in_specs) + [ANY] * r_in, out_specs=list(out_specs) + [ANY] * r_out,
        out_shape=list(out_shape) + rider.out_shapes,
        input_output_aliases={n_in + k: n_out + v for k, v in rider.aliases.items()},
        scratch_shapes=list(scratch_shapes) + rider.sems,
        compiler_params=pltpu.CompilerParams(dimension_semantics=("arbitrary",) * len(grid), vmem_limit_bytes=VMEM_LIMIT,
                                             collective_id=BARRIER_IDS[rider.peers]),
    )(*operands, *rider.arrays)
    main, rest, per_rider = res[:n_out], list(res[n_out:]), []
    for r in riders:
        per_rider.append(None if r is None else [rest.pop(0) for _ in r.out_shapes])
    return (main[0] if single else main), per_rider


def inproj_fwd(x, g1, wa, wb, wc):
    s, d = x.shape

    def body(x_ref, g_ref, wa_ref, wb_ref, wc_ref, pa_ref, pb_ref, pc_ref):
        pa, pb, pc = _inproj(PLAIN, x_ref[...], g_ref[...], wa_ref[...], wb_ref[...], wc_ref[...])
        pa_ref[...] = pa
        pb_ref[...] = pb
        pc_ref[...] = pc

    return pl.pallas_call(
        body, name="inproj_fwd", grid=(s // TM,),
        in_specs=[_rows(TM, d), _full(g1), _full(wa), _full(wb), _full(wc)],
        out_specs=[_rows(TM, wa.shape[1]), _rows(TM, wb.shape[1]), _rows(TM, wc.shape[1])],
        out_shape=[_sds((s, wa.shape[1])), _sds((s, wb.shape[1])), _sds((s, wc.shape[1]))],
        compiler_params=_cp(("parallel",)),
    )(x, g1, wa, wb, wc)


def inproj_bwd(x, g1, wa, wb, wc, dpa, dpb, dpc, dres):
    s, d = x.shape

    def body(x_ref, g_ref, wa_ref, wb_ref, wc_ref, dpa_ref, dpb_ref, dpc_ref, dres_ref,
             dx_ref, dg_ref, dwa_ref, dwb_ref, dwc_ref):
        first = pl.program_id(0) == 0
        _, vjp = jax.vjp(functools.partial(_inproj, AD), x_ref[...], g_ref[...],
                         _f32(wa_ref), _f32(wb_ref), _f32(wc_ref))
        dx, dg, dwa, dwb, dwc = vjp((dpa_ref[...], dpb_ref[...], dpc_ref[...]))
        dx_ref[...] = dx + dres_ref[...]
        _acc(dg_ref, dg, first)
        _acc(dwa_ref, dwa, first)
        _acc(dwb_ref, dwb, first)
        _acc(dwc_ref, dwc, first)

    return pl.pallas_call(
        body, name="inproj_bwd", grid=(s // TM,),
        in_specs=[_rows(TM, d), _full(g1), _full(wa), _full(wb), _full(wc),
                  _rows(TM, wa.shape[1]), _rows(TM, wb.shape[1]), _rows(TM, wc.shape[1]), _rows(TM, d)],
        out_specs=[_rows(TM, d), _full(g1), _full(wa), _full(wb), _full(wc)],
        out_shape=[_sds((s, d)), _sds(g1.shape), _sds(wa.shape), _sds(wb.shape), _sds(wc.shape)],
        compiler_params=_cp(("arbitrary",)),
    )(x, g1, wa, wb, wc, dpa, dpb, dpc, dres)


def gm_fwd(pa, vg, ws4, bs, og):
    s = pa.shape[0]
    w = pa.shape[1] // 2

    def body(pa_ref, vg_ref, ws_ref, bs_ref, og_ref, ya_ref):
        bsl = [bs_ref[h] for h in range(4)]
        for j in range(GM_CHUNKS):
            rows = slice(j * CHUNK, (j + 1) * CHUNK)
            ya_ref[rows, :] = _gm_chunk(PLAIN, pa_ref[rows, 0:w], pa_ref[rows, w:2 * w], vg_ref[...], ws_ref[...], bsl,
                                        og_ref[...])

    tm = GM_CHUNKS * CHUNK
    return pl.pallas_call(
        body, name="gm_fwd", grid=(s // tm,),
        in_specs=[_rows(tm, 2 * w), _full(vg), _full(ws4), _full(bs), _full(og)],
        out_specs=_rows(tm, w), out_shape=_sds((s, w)),
        compiler_params=_cp(("parallel",)),
    )(pa, vg, ws4, bs, og)


def gm_bwd(pa, vg, ws4, bs, og, dya):
    s = pa.shape[0]
    w = pa.shape[1] // 2

    def body(pa_ref, vg_ref, ws_ref, bs_ref, og_ref, dya_ref, dpa_ref, dvg_ref, dws_ref, dbs_ref, dog_ref):
        first = pl.program_id(0) == 0
        bsl = [bs_ref[h] for h in range(4)]
        tot = None
        for j in range(GM_CHUNKS):
            rows = slice(j * CHUNK, (j + 1) * CHUNK)
            _, vjp = jax.vjp(functools.partial(_gm_chunk, AD), pa_ref[rows, 0:w], pa_ref[rows, w:2 * w],
                             vg_ref[...], ws_ref[...], bsl, og_ref[...])
            du, dv, *dws = vjp(dya_ref[rows, :])
            dpa_ref[rows, 0:w] = du
            dpa_ref[rows, w:2 * w] = dv
            tot = dws if tot is None else jax.tree.map(jnp.add, tot, dws)
        dvg, dws, dbs, dog = tot
        _acc(dvg_ref, dvg, first)
        _acc(dws_ref, dws, first)
        _acc(dog_ref, dog, first)
        for h in range(4):
            _acc(dbs_ref.at[h], dbs[h], first)

    tm = GM_CHUNKS * CHUNK
    return pl.pallas_call(
        body, name="gm_bwd", grid=(s // tm,),
        in_specs=[_rows(tm, 2 * w), _full(vg), _full(ws4), _full(bs), _full(og), _rows(tm, w)],
        out_specs=[_rows(tm, 2 * w), _full(vg), _full(ws4), _full(bs), _full(og)],
        out_shape=[_sds((s, 2 * w)), _sds(vg.shape), _sds(ws4.shape), _sds(bs.shape), _sds(og.shape)],
        compiler_params=_cp(("arbitrary",)),
    )(pa, vg, ws4, bs, og, dya)


def hg_fwd(pb, lb, og, riders=()):
    s = pb.shape[0]
    w = pb.shape[1] // 4
    tm = HG_CHUNKS * HG_CHUNK
    st_shape = (w // HEAD64, HEAD64, HEAD64)

    def body(pb_ref, lb_ref, og_ref, yb_ref, states_ref, st_ref):
        @pl.when(pl.program_id(0) == 0)
        def _():
            st_ref[...] = jnp.zeros_like(st_ref)

        st = st_ref[...]
        for j in range(HG_CHUNKS):
            rows = slice(j * HG_CHUNK, (j + 1) * HG_CHUNK)
            states_ref[j] = st
            st, y = _hg_chunk(PLAIN, st, pb_ref[rows, 0:w], pb_ref[rows, w:2 * w], pb_ref[rows, 2 * w:3 * w],
                              pb_ref[rows, 3 * w:4 * w], lb_ref[...], og_ref[...])
            yb_ref[rows, :] = y
        st_ref[...] = st

    return _ride(
        body, riders, name="hg_fwd", grid=(s // tm,),
        in_specs=[_rows(tm, 4 * w), _full(lb), _full(og)],
        out_specs=[_rows(tm, w), pl.BlockSpec((HG_CHUNKS,) + st_shape, lambda i: (i, 0, 0, 0))],
        out_shape=[_sds((s, w)), _sds((s // HG_CHUNK,) + st_shape)],
        scratch_shapes=[pltpu.VMEM(st_shape, F32)],
        operands=(pb, lb, og), sem=("arbitrary",))


def hg_bwd(pb, lb, og, states, dyb, riders=()):
    s = pb.shape[0]
    w = pb.shape[1] // 4
    tm = HG_CHUNKS * HG_CHUNK
    nc = s // tm
    st_shape = (w // HEAD64, HEAD64, HEAD64)

    def body(pb_ref, lb_ref, og_ref, states_ref, dyb_ref, dpb_ref, dlb_ref, dog_ref, dst_ref):
        first = pl.program_id(0) == 0

        @pl.when(first)
        def _():
            dst_ref[...] = jnp.zeros_like(dst_ref)

        dst, dlb, dog = dst_ref[...], None, None
        for j in reversed(range(HG_CHUNKS)):
            rows = slice(j * HG_CHUNK, (j + 1) * HG_CHUNK)
            _, vjp = jax.vjp(functools.partial(_hg_chunk, AD), states_ref[j], pb_ref[rows, 0:w], pb_ref[rows, w:2 * w],
                             pb_ref[rows, 2 * w:3 * w], pb_ref[rows, 3 * w:4 * w], lb_ref[...], og_ref[...])
            dst, dq, df, di, dg, dlb_j, dog_j = vjp((dst, dyb_ref[rows, :]))
            dpb_ref[rows, 0:w] = dq
            dpb_ref[rows, w:2 * w] = df
            dpb_ref[rows, 2 * w:3 * w] = di
            dpb_ref[rows, 3 * w:4 * w] = dg
            dlb = dlb_j if dlb is None else dlb + dlb_j
            dog = dog_j if dog is None else dog + dog_j
        dst_ref[...] = dst
        _acc(dlb_ref, dlb, first)
        _acc(dog_ref, dog, first)

    rev = lambda i: (nc - 1 - i, 0)
    return _ride(
        body, riders, name="hg_bwd", grid=(nc,),
        in_specs=[pl.BlockSpec((tm, 4 * w), rev), _full(lb), _full(og),
                  pl.BlockSpec((HG_CHUNKS,) + st_shape, lambda i: (nc - 1 - i, 0, 0, 0)), pl.BlockSpec((tm, w), rev)],
        out_specs=[pl.BlockSpec((tm, 4 * w), rev), _full(lb), _full(og)],
        out_shape=[_sds((s, 4 * w)), _sds(lb.shape), _sds(og.shape)],
        scratch_shapes=[pltpu.VMEM(st_shape, F32)],
        operands=(pb, lb, og, states, dyb), sem=("arbitrary",))


def lower_bounds_fwd(hlb):
    def body(h_ref, o_ref):
        outs = _lower_bounds(*[h_ref[pl.ds(i, 1), :] for i in range(DEPTH)])
        for i in range(DEPTH):
            o_ref[pl.ds(i, 1), :] = outs[i]

    return pl.pallas_call(body, name="lower_bounds_fwd", out_shape=_sds(hlb.shape))(hlb)


def lower_bounds_bwd(hlb, dlbs):
    def body(h_ref, d_ref, o_ref):
        _, vjp = jax.vjp(_lower_bounds, *[h_ref[pl.ds(i, 1), :] for i in range(DEPTH)])
        outs = vjp(tuple(d_ref[pl.ds(i, 1), :] for i in range(DEPTH)))
        for i in range(DEPTH):
            o_ref[pl.ds(i, 1), :] = outs[i]

    return pl.pallas_call(body, name="lower_bounds_bwd", out_shape=_sds(hlb.shape))(hlb, dlbs)


def _mla_pre_args(pc_ref, cos_ref, sa_ref, sb_ref, qag_ref, kvag_ref, qg_ref, kg_ref, wq_ref, wk_ref, wv_ref, cast):
    sl = lambda h: slice(h * SLOT, (h + 1) * SLOT)
    ld = (lambda r, h: r[:, sl(h)].astype(F32)) if cast else (lambda r, h: r[:, sl(h)])
    diff = (pc_ref[:, 0:Q_LORA], pc_ref[:, Q_LORA:Q_LORA + KV_LORA], pc_ref[:, Q_LORA + KV_LORA:Q_LORA + 2 * KV_LORA],
            qag_ref[...], kvag_ref[...], qg_ref[...], kg_ref[...],
            [ld(wq_ref, h) for h in range(MLA_HEADS)], [ld(wk_ref, h) for h in range(MLA_HEADS)],
            [ld(wv_ref, h) for h in range(MLA_HEADS)])
    tables = (cos_ref[...], sa_ref[...], sb_ref[...])
    return diff, tables


def _mla_pre_fn(ops, tables, cq, ckv, kpe, qag, kvag, qg, kg, wq, wk, wv):
    return _mla_pre(ops, cq, ckv, kpe, *tables, qag, kvag, qg, kg, wq, wk, wv)


def mla_pre_fwd(pc, cos_t, sin_a, sin_b, qag, kvag, qg, kg, wq, wk, wv):
    s = pc.shape[0]
    hw = MLA_HEADS * SLOT

    def body(pc_ref, cos_ref, sa_ref, sb_ref, qag_ref, kvag_ref, qg_ref, kg_ref, wq_ref, wk_ref, wv_ref,
             q_ref, k_ref, v_ref):
        diff, tables = _mla_pre_args(pc_ref, cos_ref, sa_ref, sb_ref, qag_ref, kvag_ref, qg_ref, kg_ref,
                                     wq_ref, wk_ref, wv_ref, False)
        qs, ks, vs = _mla_pre_fn(PLAIN, tables, *diff)
        ones_lane = (lax.broadcasted_iota(jnp.int32, (1, SLOT), 1) == V_DIM).astype(F32)
        for h in range(MLA_HEADS):
            q_ref[:, h * SLOT:(h + 1) * SLOT] = qs[h].astype(q_ref.dtype)
            k_ref[:, h * SLOT:(h + 1) * SLOT] = ks[h].astype(k_ref.dtype)
            v_ref[:, h * SLOT:(h + 1) * SLOT] = (vs[h] + ones_lane).astype(v_ref.dtype)

    return pl.pallas_call(
        body, name="mla_pre_fwd", grid=(s // TM,),
        in_specs=[_rows(TM, pc.shape[1]), _rows(TM, SLOT), _rows(TM, SLOT), _rows(TM, SLOT),
                  _full(qag), _full(kvag), _full(qg), _full(kg), _full(wq), _full(wk), _full(wv)],
        out_specs=[_rows(TM, hw)] * 3, out_shape=[_sds((s, hw), MXU_DTYPE)] * 3,
        compiler_params=_cp(("parallel",)),
    )(pc, cos_t, sin_a, sin_b, qag, kvag, qg, kg, wq, wk, wv)


def mla_pre_bwd(pc, cos_t, sin_a, sin_b, qag, kvag, qg, kg, wq, wk, wv, dq, dk, dv):
    s = pc.shape[0]
    hw = MLA_HEADS * SLOT

    def body(pc_ref, cos_ref, sa_ref, sb_ref, qag_ref, kvag_ref, qg_ref, kg_ref, wq_ref, wk_ref, wv_ref,
             dq_ref, dk_ref, dv_ref, dpc_ref, dqag_ref, dkvag_ref, dqg_ref, dkg_ref, dwq_ref, dwk_ref, dwv_ref):
        first = pl.program_id(0) == 0
        diff, tables = _mla_pre_args(pc_ref, cos_ref, sa_ref, sb_ref, qag_ref, kvag_ref, qg_ref, kg_ref,
                                     wq_ref, wk_ref, wv_ref, True)
        _, vjp = jax.vjp(functools.partial(_mla_pre_fn, AD, tables), *diff)
        sl = lambda h: slice(h * SLOT, (h + 1) * SLOT)
        cot = ([dq_ref[:, sl(h)] for h in range(MLA_HEADS)], [dk_ref[:, sl(h)] for h in range(MLA_HEADS)],
               [dv_ref[:, sl(h)] for h in range(MLA_HEADS)])
        dcq, dckv, dkpe, dqag, dkvag, dqg, dkg, dwq, dwk, dwv = vjp(cot)
        dpc_ref[:, 0:Q_LORA] = dcq
        dpc_ref[:, Q_LORA:Q_LORA + KV_LORA] = dckv
        dpc_ref[:, Q_LORA + KV_LORA:Q_LORA + 2 * KV_LORA] = dkpe
        _acc(dqag_ref, dqag, first)
        _acc(dkvag_ref, dkvag, first)
        _acc(dqg_ref, dqg, first)
        _acc(dkg_ref, dkg, first)
        for h in range(MLA_HEADS):
            _acc(dwq_ref.at[:, sl(h)], dwq[h], first)
            _acc(dwk_ref.at[:, sl(h)], dwk[h], first)
            _acc(dwv_ref.at[:, sl(h)], dwv[h], first)

    return pl.pallas_call(
        body, name="mla_pre_bwd", grid=(s // TM,),
        in_specs=[_rows(TM, pc.shape[1]), _rows(TM, SLOT), _rows(TM, SLOT), _rows(TM, SLOT),
                  _full(qag), _full(kvag), _full(qg), _full(kg), _full(wq), _full(wk), _full(wv),
                  _rows(TM, hw), _rows(TM, hw), _rows(TM, hw)],
        out_specs=[_rows(TM, pc.shape[1]), _full(qag), _full(kvag), _full(qg), _full(kg),
                   _full(wq), _full(wk), _full(wv)],
        out_shape=[_sds(pc.shape), _sds(qag.shape), _sds(kvag.shape), _sds(qg.shape), _sds(kg.shape),
                   _sds(wq.shape), _sds(wk.shape), _sds(wv.shape)],
        compiler_params=_cp(("arbitrary",)),
    )(pc, cos_t, sin_a, sin_b, qag, kvag, qg, kg, wq, wk, wv, dq, dk, dv)


ATT_SCALE = QK_DIM ** -0.5
NEG_BIG = -1e30


def attn_fwd(q, k, v, riders=()):
    s = q.shape[0]
    nq = s // TQ
    hp = ATT_HEADS_PER_STEP
    sl = lambda j: slice(j * SLOT, (j + 1) * SLOT)

    wide = ATT_WIDE // TQ

    def body(q_ref, k_ref, v_ref, o_ref, lse_ref):
        qi = pl.program_id(1)
        lane = lax.broadcasted_iota(jnp.int32, (1, SLOT), 1)
        qs = [q_ref[:, sl(j)] for j in range(hp)]

        def scores(ki, n_tiles):
            rk = pl.ds(pl.multiple_of(ki * TQ, TQ), n_tiles * TQ)
            return tuple(_dot(qs[j], k_ref[rk, sl(j)], NT) for j in range(hp))

        def update(ki, carry, raw, n_tiles, masked):
            rk = pl.ds(pl.multiple_of(ki * TQ, TQ), n_tiles * TQ)
            if masked:
                row = lax.broadcasted_iota(jnp.int32, (TQ, n_tiles * TQ), 0) + (n_tiles - 1) * TQ
                col = lax.broadcasted_iota(jnp.int32, (TQ, n_tiles * TQ), 1)
            out = []
            for j in range(hp):
                m, acc = carry[j]
                sc = raw[j] * ATT_SCALE
                if masked:
                    sc = jnp.where(row >= col, sc, NEG_BIG)
                m_new = jnp.maximum(m, jnp.max(sc, axis=-1, keepdims=True))
                acc = jnp.exp(m - m_new) * acc + _dot(jnp.exp(sc - m_new), v_ref[rk, sl(j)], NN)
                out.append((m_new, acc))
            return tuple(out)

        def step(ki, carry, n_tiles, masked):
            return update(ki, carry, scores(ki, n_tiles), n_tiles, masked)

        def wide_step(kw, state):
            carry, raw = state
            ahead = scores(jnp.minimum(kw + 1, jnp.maximum(n_wide - 1, 0)) * wide, wide)
            return update(kw * wide, carry, raw, wide, False), ahead

        def tail_single(cr):
            cr = lax.fori_loop(n_wide * wide, qi, lambda ki, c: step(ki, c, 1, False), cr)
            return step(qi, cr, 1, True)

        n_wide = qi // wide
        init = tuple((jnp.full((TQ, 1), NEG_BIG, F32), jnp.zeros((TQ, SLOT), F32)) for _ in range(hp))
        carry, _ = lax.fori_loop(0, n_wide, wide_step, (init, scores(0, wide)))
        carry = lax.cond(qi % wide == wide - 1, lambda cr: step(qi - (wide - 1), cr, wide, True), tail_single, carry)
        for j in range(hp):
            m, acc = carry[j]
            l = jnp.sum(jnp.where(lane == V_DIM, acc, 0.0), axis=-1, keepdims=True)
            o_ref[:, sl(j)] = jnp.where(lane < V_DIM, acc / l, 0.0)
            lse_ref[j] = m + jnp.log(l)

    head_col = pl.BlockSpec((s, hp * SLOT), lambda g, i: (0, g))
    tile = pl.BlockSpec((TQ, hp * SLOT), lambda g, i: (i, g))
    return _ride(
        body, riders, name="attn_fwd", grid=(MLA_HEADS // hp, nq),
        in_specs=[tile, head_col, head_col],
        out_specs=[tile, pl.BlockSpec((hp, TQ, 1), lambda g, i: (g, i, 0))],
        out_shape=[_sds((s, MLA_HEADS * SLOT)), _sds((MLA_HEADS, s, 1))],
        operands=(q, k, v), sem=("parallel", "parallel"))


def attn_bwd(q, k, v, o, do, lse, riders=()):
    s = q.shape[0]
    nq = s // TQ
    hp = ATT_HEADS_PER_STEP
    sl = lambda j: slice(j * SLOT, (j + 1) * SLOT)
    wide = ATT_WIDE // TQ

    def body(q_ref, k_ref, v_ref, o_ref, do_ref, lse_ref, dq_ref, dk_ref, dv_ref, delta_ref):
        ki = pl.program_id(1)

        @pl.when(ki == 0)
        def _():
            dq_ref[...] = jnp.zeros_like(dq_ref)

            def prep(i, c):
                rows = pl.ds(pl.multiple_of(i * TQ, TQ), TQ)
                for j in range(hp):
                    delta_ref[j, rows, :] = jnp.sum(do_ref[rows, sl(j)] * o_ref[rows, sl(j)], axis=-1, keepdims=True)
                return c

            lax.fori_loop(0, nq, prep, 0)

        kks = [k_ref[:, sl(j)] for j in range(hp)]
        vvs = [v_ref[:, sl(j)] for j in range(hp)]

        def raws(qi, n_tiles):
            rq = pl.ds(pl.multiple_of(qi * TQ, TQ), n_tiles * TQ)
            return tuple(_dot(q_ref[rq, sl(j)], kks[j], NT) for j in range(hp))

        def step(qi, carry, n_tiles, masked, raw=None):
            rq = pl.ds(pl.multiple_of(qi * TQ, TQ), n_tiles * TQ)
            raw = raws(qi, n_tiles) if raw is None else raw
            if masked:
                row = lax.broadcasted_iota(jnp.int32, (n_tiles * TQ, TQ), 0)
                col = lax.broadcasted_iota(jnp.int32, (n_tiles * TQ, TQ), 1)
            out = []
            for j in range(hp):
                dk, dv = carry[j]
                qq = q_ref[rq, sl(j)]
                dd = do_ref[rq, sl(j)]
                sc = raw[j] * ATT_SCALE
                if masked:
                    sc = jnp.where(row >= col, sc, NEG_BIG)
                p = jnp.exp(sc - lse_ref[j, rq, :])
                dv = dv + _dot(p, dd, TN)
                ds = p * (_dot(dd, vvs[j], NT) - delta_ref[j, rq, :]) * ATT_SCALE
                dk = dk + _dot(ds, qq, TN)
                dq_ref[rq, sl(j)] = dq_ref[rq, sl(j)] + _dot(ds, kks[j], NN)
                out.append((dk, dv))
            return tuple(out)

        def head_single(cr):
            cr = step(ki, cr, 1, True)
            return lax.fori_loop(ki + 1, first_wide * wide, lambda qi, c: step(qi, c, 1, False), cr)

        zero = jnp.zeros((TQ, SLOT), F32)
        first_wide = (ki + wide) // wide
        carry = tuple((zero, zero) for _ in range(hp))
        carry = lax.cond(ki % wide == 0, lambda cr: step(ki, cr, wide, True), head_single, carry)
        n_w = nq // wide

        def wide_step(qw, state):
            cr, raw = state
            ahead = raws(jnp.minimum(qw + 1, n_w - 1) * wide, wide)
            return step(qw * wide, cr, wide, False, raw), ahead

        carry, _ = lax.fori_loop(first_wide, n_w, wide_step, (carry, raws(jnp.minimum(first_wide, n_w - 1) * wide, wide)))
        for j in range(hp):
            dk_ref[:, sl(j)] = carry[j][0]
            dv_ref[:, sl(j)] = carry[j][1]

    head_col = pl.BlockSpec((s, hp * SLOT), lambda g, i: (0, g))
    tile = pl.BlockSpec((TQ, hp * SLOT), lambda g, i: (i, g))
    return _ride(
        body, riders, name="attn_bwd", grid=(MLA_HEADS // hp, nq),
        in_specs=[head_col, tile, tile, head_col, head_col, pl.BlockSpec((hp, s, 1), lambda g, i: (g, 0, 0))],
        out_specs=[head_col, tile, tile],
        out_shape=[_sds((s, MLA_HEADS * SLOT))] * 3,
        scratch_shapes=[pltpu.VMEM((hp, s, 1), F32)],
        operands=(q, k, v, o, do, lse), sem=("arbitrary", "arbitrary"))


def _outproj_args(ya_ref, yb_ref, o_ref, mog_ref, woa_ref, wob_ref, woc_ref, cast):
    sl = lambda h: slice(h * SLOT, (h + 1) * SLOT)
    ldw = (lambda r: r[...].astype(F32)) if cast else (lambda r: r[...])
    return (ya_ref[...], yb_ref[...], [o_ref[:, sl(h)] for h in range(MLA_HEADS)],
            [mog_ref[:, sl(h)] for h in range(MLA_HEADS)], ldw(woa_ref), ldw(wob_ref), ldw(woc_ref))


def outproj_fwd(x, ya, yb, o, mog, woa, wob, woc):
    s, d = x.shape

    def body(x_ref, ya_ref, yb_ref, o_ref, mog_ref, woa_ref, wob_ref, woc_ref, x1_ref):
        x1_ref[...] = _outproj(PLAIN, x_ref[...], *_outproj_args(ya_ref, yb_ref, o_ref, mog_ref, woa_ref, wob_ref,
                                                                  woc_ref, False))

    return pl.pallas_call(
        body, name="outproj_fwd", grid=(s // TM,),
        in_specs=[_rows(TM, d), _rows(TM, ya.shape[1]), _rows(TM, yb.shape[1]), _rows(TM, o.shape[1]),
                  _full(mog), _full(woa), _full(wob), _full(woc)],
        out_specs=_rows(TM, d), out_shape=_sds((s, d)),
        compiler_params=_cp(("parallel",)),
    )(x, ya, yb, o, mog, woa, wob, woc)


def outproj_bwd(ya, yb, o, mog, woa, wob, woc, dx2, dx1p, riders=()):
    s, d = dx2.shape
    npart = dx1p.shape[0]

    def body(ya_ref, yb_ref, o_ref, mog_ref, woa_ref, wob_ref, woc_ref, dx2_ref, dx1p_ref,
             dx1_ref, dya_ref, dyb_ref, do_ref, dmog_ref, dwoa_ref, dwob_ref, dwoc_ref):
        first = pl.program_id(0) == 0
        sl = lambda h: slice(h * SLOT, (h + 1) * SLOT)
        dx1 = dx2_ref[...]
        for p in range(npart):
            dx1 = dx1 + dx1p_ref[p]
        dx1_ref[...] = dx1
        args = _outproj_args(ya_ref, yb_ref, o_ref, mog_ref, woa_ref, wob_ref, woc_ref, True)
        _, vjp = jax.vjp(lambda *a: _outproj(AD, jnp.zeros_like(dx1), *a), *args)
        dya, dyb, do, dmog, dwoa, dwob, dwoc = vjp(dx1)
        dya_ref[...] = dya
        dyb_ref[...] = dyb
        _acc(dwoa_ref, dwoa, first)
        _acc(dwob_ref, dwob, first)
        _acc(dwoc_ref, dwoc, first)
        for h in range(MLA_HEADS):
            do_ref[:, sl(h)] = do[h]
            _acc(dmog_ref.at[:, sl(h)], dmog[h], first)

    return _ride(
        body, riders, name="outproj_bwd", grid=(s // TM,),
        in_specs=[_rows(TM, ya.shape[1]), _rows(TM, yb.shape[1]), _rows(TM, o.shape[1]),
                  _full(mog), _full(woa), _full(wob), _full(woc), _rows(TM, d),
                  pl.BlockSpec((npart, TM, d), lambda i: (0, i, 0))],
        out_specs=[_rows(TM, d), _rows(TM, ya.shape[1]), _rows(TM, yb.shape[1]), _rows(TM, o.shape[1]),
                   _full(mog), _full(woa), _full(wob), _full(woc)],
        out_shape=[_sds((s, d)), _sds(ya.shape), _sds(yb.shape), _sds(o.shape),
                   _sds(mog.shape), _sds(woa.shape), _sds(wob.shape), _sds(woc.shape)],
        operands=(ya, yb, o, mog, woa, wob, woc, dx2, dx1p), sem=("arbitrary",))


def ffn_fwd(x1, g2, w1, w2, riders=()):
    s, d = x1.shape
    npart, _, fs = w1.shape

    def body(x1_ref, g_ref, w1_ref, w2_ref, x2_ref, r_ref):
        p = pl.program_id(1)
        x1v = x1_ref[...]
        r = jnp.maximum(PLAIN.mm(_rms(x1v, g_ref[...]), w1_ref[...]), 0.0)
        r_ref[...] = r.astype(r_ref.dtype)
        part = PLAIN.mm(r * r, w2_ref[...])

        @pl.when(p == 0)
        def _():
            x2_ref[...] = x1v + part

        @pl.when(p != 0)
        def _():
            x2_ref[...] = x2_ref[...] + part

    tm = min(2 * TM_FFN, s)
    return _ride(
        body, riders, name="ffn_fwd", grid=(s // tm, npart),
        in_specs=[pl.BlockSpec((tm, d), lambda i, p: (i, 0)), pl.BlockSpec(g2.shape, lambda i, p: (0, 0)),
                  pl.BlockSpec((None, d, fs), lambda i, p: (p, 0, 0)), pl.BlockSpec((None, fs, d), lambda i, p: (p, 0, 0))],
        out_specs=[pl.BlockSpec((tm, d), lambda i, p: (i, 0)), pl.BlockSpec((tm, fs), lambda i, p: (i, p))],
        out_shape=[_sds((s, d)), _sds((s, npart * fs), MXU_DTYPE)],
        operands=(x1, g2, w1, w2), sem=("parallel", "arbitrary"))


def ffn_bwd(x1, g2, w1, w2, r, dx2, riders=()):
    s, d = x1.shape
    npart, _, fs = w1.shape
    tm = TM_FFN

    def body(x1_ref, g_ref, w1_ref, w2_ref, r_ref, dx2_ref, dx1p_ref, dg_ref, dw1_ref, dw2_ref, dw1b_ref, dw2b_ref):
        p = pl.program_id(0)
        i = pl.program_id(1)
        h2, vjp_norm = jax.vjp(_rms, x1_ref[...], g_ref[...])
        rr = r_ref[...].astype(F32)
        dy = dx2_ref[...]
        da = _dot(dy, w2_ref[...], NT) * (2.0 * rr)
        dx1, dg = vjp_norm(_dot(da, w1_ref[...], NT))
        dx1p_ref[...] = dx1
        _acc(dg_ref, dg, (p == 0) & (i == 0))
        _acc(dw1_ref, _dot(h2, da, TN), i == 0)
        _acc(dw2_ref, _dot(rr * rr, dy, TN), i == 0)

        @pl.when(i == s // tm - 1)
        def _():
            dw1b_ref[...] = dw1_ref[...].astype(BF16)
            dw2b_ref[...] = dw2_ref[...].astype(BF16)

    return _ride(
        body, riders, name="ffn_bwd", grid=(npart, s // tm),
        in_specs=[pl.BlockSpec((tm, d), lambda p, i: (i, 0)), pl.BlockSpec(g2.shape, lambda p, i: (0, 0)),
                  pl.BlockSpec((None, d, fs), lambda p, i: (p, 0, 0)), pl.BlockSpec((None, fs, d), lambda p, i: (p, 0, 0)),
                  pl.BlockSpec((tm, fs), lambda p, i: (i, p)), pl.BlockSpec((tm, d), lambda p, i: (i, 0))],
        out_specs=[pl.BlockSpec((None, tm, d), lambda p, i: (p, i, 0)), pl.BlockSpec(g2.shape, lambda p, i: (0, 0)),
                   pl.BlockSpec((None, d, fs), lambda p, i: (p, 0, 0)), pl.BlockSpec((None, fs, d), lambda p, i: (p, 0, 0)),
                   pl.BlockSpec((None, d, fs), lambda p, i: (p, 0, 0)), pl.BlockSpec((None, fs, d), lambda p, i: (p, 0, 0))],
        out_shape=[_sds((npart, s, d)), _sds(g2.shape), _sds(w1.shape), _sds(w2.shape),
                   _sds(w1.shape, BF16), _sds(w2.shape, BF16)],
        operands=(x1, g2, w1, w2, r, dx2), sem=("arbitrary", "arbitrary"))


def loss_head(y, target):
    s, d = y.shape

    def body(y_ref, t_ref, dy_ref, loss_ref):
        err = y_ref[...] - t_ref[...]
        dy_ref[...] = err * (1.0 / d)
        part = jnp.sum(jnp.sum(err * err, axis=-1, keepdims=True), axis=0, keepdims=True) * (0.5 / d)
        _acc(loss_ref, jnp.broadcast_to(part, loss_ref.shape), pl.program_id(0) == 0)

    return pl.pallas_call(
        body, name="loss_head", grid=(s // TM,),
        in_specs=[_rows(TM, d), _rows(TM, d)],
        out_specs=[_rows(TM, d), pl.BlockSpec((1, SLOT), lambda i: (0, 0))],
        out_shape=[_sds((s, d)), _sds((1, SLOT))],
        compiler_params=_cp(("arbitrary",)),
    )(y, target)


def _row_block(r):
    for b in (512, 256, 128, 64, 32, 16, 8):
        if r % b == 0:
            return b
    return r


def sum_cores(arrs, gots, half, me):
    n = len(arrs)

    def body(sp_ref, *refs):
        for i in range(n):
            a_ref, g_ref, wire_ref, own_ref = refs[i], refs[n + i], refs[2 * n + i], refs[3 * n + i]
            tot = a_ref[...] + g_ref[...]
            wire_ref[...] = tot.astype(wire_ref.dtype)

            @pl.when(pl.program_id(0) == sp_ref[1])
            def _(own_ref=own_ref, tot=tot):
                own_ref[...] = tot

    shapes = [a.shape[2:] for a in arrs]
    grid_spec = pltpu.PrefetchScalarGridSpec(
        num_scalar_prefetch=1, grid=(N_CHIPS,),
        in_specs=[pl.BlockSpec((None, None) + sh, lambda p, sp: (p, sp[0], 0, 0)) for sh in shapes]
        + [pl.BlockSpec((None,) + sh, lambda p, sp: (p, 0, 0)) for sh in shapes],
        out_specs=[pl.BlockSpec((None,) + sh, lambda p, sp: (p, 0, 0)) for sh in shapes]
        + [pl.BlockSpec(sh, lambda p, sp: (0, 0)) for sh in shapes])
    outs = pl.pallas_call(body, name="sum_cores", grid_spec=grid_spec,
                          out_shape=[_sds((N_CHIPS,) + sh, BF16) for sh in shapes] + [_sds(sh) for sh in shapes],
                          compiler_params=_cp(("arbitrary",)))(jnp.stack([half, me]).astype(jnp.int32), *arrs, *gots)
    return outs[:n], outs[n:]


SUM_STEPS = 4


def sum_chips(owns, recvs, half, places):
    n = len(owns)
    dests = [d for _, _, d in places if d is not None]

    def body(sp_ref, *refs):
        del sp_ref
        for i in range(n):
            own_ref, out_ref = refs[4 * i], refs[4 * n + len(dests) + i]
            r0, r1, r2 = (refs[4 * i + 1 + j][...].astype(F32) for j in range(3))
            out_ref[...] = ((own_ref[...] + r0) + r1) + r2

    in_specs, out_specs, operands, aliases = [], [], [], {}
    for i, (own, recv, (layer, _, dest)) in enumerate(zip(owns, recvs, places)):
        r, c = own.shape
        br = r // SUM_STEPS
        in_specs.append(pl.BlockSpec((br, c), lambda i, sp: (i, 0)))
        in_specs += [pl.BlockSpec((None, br, c), functools.partial(lambda i, sp, j: (j, i, 0), j=j)) for j in range(3)]
        out_specs.append(pl.BlockSpec((None, None, br, c), functools.partial(lambda i, sp, l: (l, sp[0], i, 0), l=layer)))
        operands += [own, recv, recv, recv]
        if dest is not None:
            aliases[1 + 4 * n + len(aliases)] = i
    grid_spec = pltpu.PrefetchScalarGridSpec(num_scalar_prefetch=1, grid=(SUM_STEPS,), in_specs=in_specs + [ANY] * len(dests),
                                             out_specs=out_specs)
    return pl.pallas_call(body, name="sum_chips", grid_spec=grid_spec, input_output_aliases=aliases,
                          out_shape=[_sds((nl, 2) + o.shape) for o, (_, nl, _) in zip(owns, places)],
                          compiler_params=_cp(("parallel",)))(half.reshape(1).astype(jnp.int32), *operands, *dests)


def adamw(w, g, m, v, name, riders=()):
    r, c = w.shape
    br = _row_block(r)
    c1 = 1.0 / (1.0 - ADAM_B1 ** ADAM_STEP)
    c2 = 1.0 / (1.0 - ADAM_B2 ** ADAM_STEP)

    def body(w_ref, g_ref, m_ref, v_ref, d_ref, nm_ref, nv_ref):
        gg = g_ref[...]
        nm = ADAM_B1 * m_ref[...] + (1.0 - ADAM_B1) * gg
        nv = ADAM_B2 * v_ref[...] + (1.0 - ADAM_B2) * (gg * gg)
        d_ref[...] = -ADAM_LR * ((nm * c1) / (jnp.sqrt(nv * c2) + ADAM_EPS) + ADAM_WD * w_ref[...])
        nm_ref[...] = nm
        nv_ref[...] = nv

    return _ride(body, riders, name=name, grid=(r // br,), in_specs=[_rows(br, c)] * 4, out_specs=[_rows(br, c)] * 3,
                 out_shape=[_sds((r, c))] * 3, operands=(w, g, m, v), sem=("parallel",))


def _place():
    x, y, c = lax.axis_index("x"), lax.axis_index("y"), lax.axis_index("c")
    chips = [(1 - x, y), (x, 1 - y), (1 - x, 1 - y)]
    return x, y, c, chips


def _remote(src, dst, send_sem, recv_sem, to):
    return pltpu.make_async_remote_copy(src_ref=src, dst_ref=dst, send_sem=send_sem, recv_sem=recv_sem,
                                        device_id=to, device_id_type=MESH)


def gather_rider(arrs):
    n = len(arrs)
    me_chip = 2 * lax.axis_index("x") + lax.axis_index("y")
    bufs = [lax.dynamic_update_index_in_dim(lax.empty((N_CHIPS,) + a.shape, a.dtype), a, me_chip, 0) for a in arrs]

    def plan(ins, outs, sems):
        send_sems, recv_sems = sems
        x, y, c, chips = _place()
        me = 2 * x + y
        half, other, sibling = pl.ds(2 * c, 2), pl.ds(2 - 2 * c, 2), (x, y, 1 - c)
        cp = lambda i, k, src, dst, to: _remote(src, dst, send_sems.at[i, k], recv_sems.at[i, k], to)
        pairs = [(i, j, cx, cy) for i in range(n) for j, (cx, cy) in enumerate(chips)]
        blk = lambda i, cx, cy, part: outs[i].at[2 * cx + cy, part]
        first = lambda: [cp(i, j, ins[i].at[half], outs[i].at[me, half], (cx, cy, c)) for i, j, cx, cy in pairs]
        landed = lambda: [cp(i, j, blk(i, cx, cy, half), blk(i, cx, cy, half), (cx, cy, c)) for i, j, cx, cy in pairs]
        passed = lambda: [cp(i, 3 + j, blk(i, cx, cy, half), blk(i, cx, cy, half), sibling) for i, j, cx, cy in pairs]
        from_sibling = lambda: [cp(i, 3 + j, blk(i, cx, cy, other), blk(i, cx, cy, other), sibling) for i, j, cx, cy in pairs]
        return first, landed, passed, from_sibling

    def start(ins, outs, sems):
        for cp in plan(ins, outs, sems)[0]():
            cp.start()

    def finish(ins, outs, sems):
        first, landed, passed, from_sibling = plan(ins, outs, sems)
        forwards = passed()
        for a, b in zip(landed(), forwards):
            a.wait_recv()
            b.start()
        for cp in from_sibling():
            cp.wait_recv()
        for cp in first() + forwards:
            cp.wait_send()

    return Rider(list(arrs) + bufs, [_sds((N_CHIPS,) + a.shape, a.dtype) for a in arrs], {n + i: i for i in range(n)},
                 [pltpu.SemaphoreType.DMA((n, 6)), pltpu.SemaphoreType.DMA((n, 6))], start, finish, {"sibling", "chips"})


class Reducer:
    def __init__(self, arrs, places=None, send=None):
        self.a = list(arrs)
        self.send = list(send) if send is not None else self.a
        self.n = len(self.a)
        self.places = places if places is not None else [(0, 1, None)] * self.n
        self.c = lax.axis_index("c")
        self.me = 2 * lax.axis_index("x") + lax.axis_index("y")

    def swap_rider(self):
        n = self.n

        def plan(ins, outs, sems):
            x, y, c, _ = _place()
            return [_remote(ins[i].at[p, 1 - c], outs[i].at[p], sems[0].at[i, p], sems[1].at[i, p], (x, y, 1 - c))
                    for i in range(n) for p in range(N_CHIPS)]

        return Rider(self.send, [_sds((N_CHIPS,) + a.shape[2:], a.dtype) for a in self.send], {},
                     [pltpu.SemaphoreType.DMA((n, N_CHIPS)), pltpu.SemaphoreType.DMA((n, N_CHIPS))],
                     lambda *r: [cp.start() for cp in plan(*r)], lambda *r: [cp.wait() for cp in plan(*r)], {"sibling"})

    def after_swap(self, got):
        self.wire, self.own = sum_cores(self.a, got, self.c, self.me)

    def scatter_rider(self):
        n = self.n

        def plan(ins, outs, sems):
            x, y, c, chips = _place()
            return [_remote(ins[i].at[2 * cx + cy], outs[i].at[j], sems[0].at[i, j], sems[1].at[i, j], (cx, cy, c))
                    for i in range(n) for j, (cx, cy) in enumerate(chips)]

        return Rider(self.wire, [_sds((3,) + w.shape[1:], w.dtype) for w in self.wire], {},
                     [pltpu.SemaphoreType.DMA((n, 3)), pltpu.SemaphoreType.DMA((n, 3))],
                     lambda *r: [cp.start() for cp in plan(*r)], lambda *r: [cp.wait() for cp in plan(*r)], {"chips"})

    def after_scatter(self, recv):
        if callable(self.places):
            self.places = self.places()
        self.full = sum_chips(self.own, recv, self.c, self.places)

    def share_rider(self):
        n = self.n
        layers = [layer for layer, _, _ in self.places]

        def plan(ins, outs, sems):
            x, y, c, _ = _place()
            return [_remote(ins[i].at[layers[i], c], outs[i].at[layers[i], c], sems[0].at[i], sems[1].at[i], (x, y, 1 - c))
                    for i in range(n)]

        return Rider(self.full, [_sds(f.shape) for f in self.full], {i: i for i in range(n)},
                     [pltpu.SemaphoreType.DMA((n,)), pltpu.SemaphoreType.DMA((n,))],
                     lambda *r: [cp.start() for cp in plan(*r)], lambda *r: [cp.wait() for cp in plan(*r)], {"sibling"})

    def run(self):
        self.after_swap(run_rider(self.swap_rider(), "swap_halves"))
        self.after_scatter(run_rider(self.scatter_rider(), "scatter_chips"))
        return run_rider(self.share_rider(), "share_halves")


def _pad_slots(a, live):
    lead = a.shape[:-1]
    a = a.reshape(lead + (MLA_HEADS, live))
    a = jnp.pad(a, [(0, 0)] * len(lead) + [(0, 0), (0, SLOT - live)])
    return a.reshape(lead + (MLA_HEADS * SLOT,))


def _unpad_slots(a, live):
    lead = a.shape[:-1]
    return a.reshape(lead + (MLA_HEADS, SLOT))[..., :live].reshape(lead + (MLA_HEADS * live,))


def _rope_tables(positions, s):
    half = QK_ROPE // 2
    inv_freq = ROPE_THETA ** (-jnp.arange(half, dtype=F32) / half)
    ang = positions.reshape(s).astype(F32)[:, None] * inv_freq[None, :]
    cos, sin = jnp.cos(ang), jnp.sin(ang)
    one = jnp.ones((s, QK_NOPE), F32)
    z64, z16, z32 = jnp.zeros((s, QK_NOPE), F32), jnp.zeros((s, half), F32), jnp.zeros((s, SLOT - QK_DIM), F32)
    cos_t = jnp.concatenate([one, cos, cos, z32], axis=1)
    sin_a = jnp.concatenate([z64, -sin, z16, z32], axis=1)
    sin_b = jnp.concatenate([z64, z16, sin, z32], axis=1)
    return cos_t, sin_a, sin_b


def _out_weights(full):
    w_out = jnp.concatenate([full["w_out"][p] for p in range(N_CHIPS)], axis=0)
    woc = w_out[512:].reshape(MLA_HEADS, V_DIM, D_MODEL)
    woc = jnp.pad(woc, ((0, 0), (0, SLOT - V_DIM), (0, 0))).reshape(MLA_HEADS * SLOT, D_MODEL)
    return dict(woa=w_out[:256], wob=w_out[256:512], woc=woc)


def _layer_weights(full, small, l):
    w_in = jnp.concatenate([full["w_in"][p] for p in range(N_CHIPS)], axis=1)
    wc = jnp.pad(w_in[:, 1536:], ((0, 0), (0, 512 - (w_in.shape[1] - 1536))))
    w_uq = jnp.concatenate([full["mla_w_uq"][p] for p in range(N_CHIPS)], axis=1)
    w_ukv = jnp.concatenate([full["mla_w_ukv"][p] for p in range(N_CHIPS)], axis=1)
    ukv = w_ukv.reshape(KV_LORA, MLA_HEADS, QK_NOPE + V_DIM)
    row = lambda a: a.reshape(1, -1)
    return dict(
        g1=row(small["norm1_gain"][l]), wa=w_in[:, :512], wb=w_in[:, 512:1536], wc=wc,
        vg=row(small["gm_v_gain"][l]), ws=small["gm_w_s"][l], bs=small["gm_b_s"][l].reshape(4, CHUNK, 1),
        gog=row(small["gm_out_gain"][l]), hog=small["hg_out_gain"][l].reshape(-1, 1),
        qag=row(small["mla_q_a_gain"][l]), kvag=row(small["mla_kv_a_gain"][l]),
        qg=row(jnp.pad(small["mla_q_gain"][l], (0, SLOT - QK_DIM))), kg=row(jnp.pad(small["mla_k_gain"][l], (0, SLOT - QK_DIM))),
        wq=_pad_slots(w_uq, QK_DIM), wk=_pad_slots(ukv[..., :QK_NOPE].reshape(KV_LORA, -1), QK_NOPE),
        wv=_pad_slots(ukv[..., QK_NOPE:].reshape(KV_LORA, -1), V_DIM),
        mog=row(_pad_slots(small["mla_out_gain"][l], V_DIM)),
        g2=row(small["norm2_gain"][l]),
    )


def _shard_cols(a):
    r, c4 = a.shape
    return a.reshape(r, N_CHIPS, c4 // N_CHIPS).transpose(1, 0, 2)


def local_step(x, positions, target, small, comm):
    s = x.shape[0]
    cos_t, sin_a, sin_b = _rope_tables(positions, s)
    lbs = lower_bounds_fwd(small["hg_lower_bound"])
    lw, saved = [], []
    for l in range(DEPTH):
        w = _layer_weights(comm.part(l, "in"), small, l)
        lw.append(w)
        lb = lbs[l].reshape(-1, 1)
        pa, pb, pc = inproj_fwd(x, w["g1"], w["wa"], w["wb"], w["wc"])
        ya = gm_fwd(pa, w["vg"], w["ws"], w["bs"], w["gog"])
        (yb, states), got = hg_fwd(pb, lb, w["hog"], [comm.gather_rider(l, "ff1")])
        comm.gathered(l, "ff1", got[0])
        q, k, v = mla_pre_fwd(pc, cos_t, sin_a, sin_b, w["qag"], w["kvag"], w["qg"], w["kg"], w["wq"], w["wk"], w["wv"])
        (o, lse), got = attn_fwd(q, k, v, [comm.gather_rider(l, "ff2"), comm.gather_rider(l, "out")])
        comm.gathered(l, "ff2", got[0])
        comm.gathered(l, "out", got[1])
        w.update(_out_weights(comm.part(l, "out")))
        x1 = outproj_fwd(x, ya, yb, o, w["mog"], w["woa"], w["wob"], w["woc"])
        w["w1"], w["w2"] = comm.part(l, "ff1")["w_ff1"], comm.part(l, "ff2")["w_ff2"]
        rider = comm.gather_rider(l + 1, "in") if l + 1 < DEPTH else None
        (x2, r), got = ffn_fwd(x1, w["g2"], w["w1"], w["w2"], [rider])
        comm.gathered(l + 1, "in", got[0])
        saved.append(dict(x=x, pa=pa, pb=pb, pc=pc, ya=ya, yb=yb, states=states, q=q, k=k, v=v, o=o, lse=lse, x1=x1, r=r, lb=lb))
        x = x2
    dx, loss_part = loss_head(x, target)
    groups = [dict() for _ in range(DEPTH)]
    sm = {n: [None] * DEPTH for n in ("norm1_gain", "gm_v_gain", "gm_w_s", "gm_b_s", "gm_out_gain", "hg_out_gain",
                                       "mla_q_a_gain", "mla_kv_a_gain", "mla_q_gain", "mla_k_gain", "mla_out_gain",
                                       "norm2_gain")}
    dlbs = [None] * DEPTH
    halves = lambda g: g.reshape(N_CHIPS, 2, g.shape[1] // 2, g.shape[2])
    take = lambda red, f: None if red is None else f(red)
    red_mix = red_late = None
    for l in reversed(range(DEPTH)):
        w, a = lw[l], saved[l]
        (dx1p, dg2, dw1, dw2, dw1b, dw2b), got = ffn_bwd(a["x1"], w["g2"], w["w1"], w["w2"], a["r"], dx,
                                             [take(red_mix, Reducer.swap_rider), take(red_late, Reducer.share_rider)])
        if red_mix:
            red_mix.after_swap(got[0])
        if red_late:
            comm.reduced(MIX, got[1])
        ffn_arrs = [halves(dw1), halves(dw2)]
        red_ffn = comm.reducer(ffn_arrs, FFN, l, send=[halves(dw1b), halves(dw2b)])
        (dx1, dya, dyb, do, dmog, dwoa, dwob, dwoc), got = outproj_bwd(
            a["ya"], a["yb"], a["o"], w["mog"], w["woa"], w["wob"], w["woc"], dx, dx1p, [take(red_ffn, Reducer.swap_rider)])
        if red_ffn:
            red_ffn.after_swap(got[0])
        (dq, dk, dv), got = attn_bwd(a["q"], a["k"], a["v"], a["o"], do, a["lse"], [take(red_ffn, Reducer.scatter_rider)])
        if red_ffn:
            red_ffn.after_scatter(got[0])
        dpc, dqag, dkvag, dqg, dkg, dwq, dwk, dwv = mla_pre_bwd(a["pc"], cos_t, sin_a, sin_b, w["qag"], w["kvag"], w["qg"],
                                                                  w["kg"], w["wq"], w["wk"], w["wv"], dq, dk, dv)
        (dpb, dlb, dhog), got = hg_bwd(a["pb"], a["lb"], w["hog"], a["states"], dyb,
                                       [take(red_ffn, Reducer.share_rider), take(red_mix, Reducer.scatter_rider)])
        if red_ffn:
            comm.reduced(FFN, got[0])
        else:
            groups[l].update(zip(FFN, ffn_arrs))
        if red_mix:
            red_mix.after_scatter(got[1])
        red_late = red_mix
        dpa, dvg, dws, dbs, dgog = gm_bwd(a["pa"], w["vg"], w["ws"], w["bs"], w["gog"], dya)
        dx, dg1, dwa, dwb, dwc = inproj_bwd(a["x"], w["g1"], w["wa"], w["wb"], w["wc"], dpa, dpb, dpc, dx1)
        dukv = jnp.concatenate([dwk.reshape(KV_LORA, MLA_HEADS, SLOT)[..., :QK_NOPE],
                                dwv.reshape(KV_LORA, MLA_HEADS, SLOT)[..., :V_DIM]], axis=-1)
        dwo = jnp.concatenate([dwoa, dwob, dwoc.reshape(MLA_HEADS, SLOT, D_MODEL)[:, :V_DIM].reshape(-1, D_MODEL)], axis=0)
        mix_arrs = [halves(_shard_cols(jnp.concatenate([dwa, dwb, dwc[:, :1952 - 1536]], axis=1))),
                    halves(_shard_cols(_unpad_slots(dwq, QK_DIM))), halves(_shard_cols(dukv.reshape(KV_LORA, -1))),
                    halves(dwo.reshape(N_CHIPS, -1, D_MODEL))]
        red_mix = comm.reducer(mix_arrs, MIX, l) if l > 0 else None
        if red_mix is None:
            groups[l].update(zip(MIX, mix_arrs))
        sm["norm1_gain"][l] = dg1[0]
        sm["gm_v_gain"][l] = dvg[0]
        sm["gm_w_s"][l] = dws
        sm["gm_b_s"][l] = dbs[..., 0]
        sm["gm_out_gain"][l] = dgog[0]
        sm["hg_out_gain"][l] = dhog[:, 0]
        sm["mla_q_a_gain"][l] = dqag[0]
        sm["mla_kv_a_gain"][l] = dkvag[0]
        sm["mla_q_gain"][l] = dqg[0, :QK_DIM]
        sm["mla_k_gain"][l] = dkg[0, :QK_DIM]
        sm["mla_out_gain"][l] = _unpad_slots(dmog[0], V_DIM)
        sm["norm2_gain"][l] = dg2[0]
        dlbs[l] = dlb[:, 0]
    sm["hg_lower_bound"] = [lower_bounds_bwd(small["hg_lower_bound"], jnp.stack(dlbs))]
    return loss_part, dx, groups, sm, red_late


MIX = ("w_in", "mla_w_uq", "mla_w_ukv", "w_out")
FFN = ("w_ff1", "w_ff2")
BIG = MIX + FFN
PARTS = {"in": ("w_in", "mla_w_uq", "mla_w_ukv"), "out": ("w_out",), "ff1": ("w_ff1",), "ff2": ("w_ff2",)}
SMALL = ("norm1_gain", "gm_v_gain", "gm_w_s", "gm_b_s", "gm_out_gain", "hg_lower_bound", "hg_out_gain",
         "mla_q_a_gain", "mla_kv_a_gain", "mla_q_gain", "mla_k_gain", "mla_out_gain", "norm2_gain")
ORDER = ("norm1_gain", "w_in", "gm_v_gain", "gm_w_s", "gm_b_s", "gm_out_gain", "hg_lower_bound", "hg_out_gain",
         "mla_q_a_gain", "mla_w_uq", "mla_kv_a_gain", "mla_w_ukv", "mla_q_gain", "mla_k_gain", "mla_out_gain",
         "w_out", "norm2_gain", "w_ff1", "w_ff2")
PACK_ROWS = 320


def _pack(pieces):
    flat = jnp.concatenate([a.reshape(-1) for a in pieces])
    total = 2 * N_CHIPS * PACK_ROWS * SLOT
    return jnp.pad(flat, (0, total - flat.shape[0]))


def _unpack(flat, shapes):
    out, off = [], 0
    for sh in shapes:
        size = 1
        for d in sh:
            size *= d
        out.append(flat[off:off + size].reshape(sh))
        off += size
    return out


class ChipComm:
    def __init__(self, shards):
        self.shards = shards
        self.full = {}
        self.grads = {}

    def gather_rider(self, l, part):
        return gather_rider([self.shards[n][l].astype(MXU_DTYPE).reshape(4, self.shards[n].shape[1] // 4, -1)
                             for n in PARTS[part]])

    def gathered(self, l, part, outs):
        if outs is not None:
            self.full[l, part] = {n: o.reshape((N_CHIPS,) + self.shards[n].shape[1:]) for n, o in zip(PARTS[part], outs)}

    def part(self, l, part):
        if (l, part) not in self.full:
            self.gathered(l, part, run_rider(self.gather_rider(l, part), "gather_weights"))
        return self.full[l, part]

    def places(self, names, l):
        return [(l, self.shards[n].shape[0], self.grads.get(n)) for n in names]

    def reducer(self, arrs, names, l, send=None):
        return Reducer(arrs, lambda: self.places(names, l), send)

    def reduced(self, names, outs):
        self.grads.update(zip(names, outs))


def kernel(x, positions, norm1_gain, w_in, gm_v_gain, gm_w_s, gm_b_s, gm_out_gain, hg_lower_bound, hg_out_gain, mla_q_a_gain, mla_w_uq, mla_kv_a_gain, mla_w_ukv, mla_q_gain, mla_k_gain, mla_out_gain, w_out, norm2_gain, w_ff1, w_ff2, loss_target, m_norm1_gain, m_w_in, m_gm_v_gain, m_gm_w_s, m_gm_b_s, m_gm_out_gain, m_hg_lower_bound, m_hg_out_gain, m_mla_q_a_gain, m_mla_w_uq, m_mla_kv_a_gain, m_mla_w_ukv, m_mla_q_gain, m_mla_k_gain, m_mla_out_gain, m_w_out, m_norm2_gain, m_w_ff1, m_w_ff2, v_norm1_gain, v_w_in, v_gm_v_gain, v_gm_w_s, v_gm_b_s, v_gm_out_gain, v_hg_lower_bound, v_hg_out_gain, v_mla_q_a_gain, v_mla_w_uq, v_mla_kv_a_gain, v_mla_w_ukv, v_mla_q_gain, v_mla_k_gain, v_mla_out_gain, v_w_out, v_norm2_gain, v_w_ff1, v_w_ff2):
    given = dict(locals())
    weights = {n: given[n] for n in ORDER}
    moms = {n: given["m_" + n] for n in ORDER}
    vars_ = {n: given["v_" + n] for n in ORDER}
    s, d = x.shape[1], x.shape[2]

    small = {n: weights[n] for n in SMALL}
    comm = ChipComm({n: weights[n] for n in BIG})
    loss_part, dx, groups, small_g, red_late = local_step(x.reshape(s, d), positions, loss_target.reshape(s, d), small, comm)
    loss = lax.psum(loss_part[0, 0], ("x", "y", "c"))

    pack_g = _pack([jnp.stack(small_g[n]) for n in SMALL]).reshape(N_CHIPS, 2, PACK_ROWS, SLOT)
    last = Reducer([groups[0][n] for n in MIX] + [pack_g], lambda: comm.places(MIX, 0) + [(0, 1, None)])
    swap = last.swap_rider()
    outs = run_rider(_merge_riders([swap, red_late.share_rider()]), "swap_halves")
    last.after_swap(outs[:len(swap.out_shapes)])
    comm.reduced(MIX, outs[len(swap.out_shapes):])
    last.after_scatter(run_rider(last.scatter_rider(), "scatter_chips"))
    reduced = run_rider(last.share_rider(), "share_halves")
    comm.reduced(MIX, reduced[:-1])
    pack_full = run_rider(gather_rider([reduced[-1].reshape(4, PACK_ROWS // 2, SLOT)]), "gather_small")[0].reshape(-1)
    grads = {n: comm.grads[n].reshape(weights[n].shape) for n in BIG}
    grads.update(zip(SMALL, _unpack(pack_full, [weights[n].shape for n in SMALL])))

    delta, new_m, new_v = {}, {}, {}
    flat2 = lambda a: a.reshape(-1, a.shape[-1])
    for n in ORDER:
        outs, _ = adamw(flat2(weights[n]), flat2(grads[n]), flat2(moms[n]), flat2(vars_[n]), "adamw_" + n)
        delta[n], new_m[n], new_v[n] = [o.reshape(weights[n].shape) for o in outs]

    return (loss, dx.reshape(x.shape), *[grads[n] for n in ORDER], *[delta[n] for n in ORDER],
            *[new_m[n] for n in ORDER], *[new_v[n] for n in ORDER])
```
